```python
import math
import jax, jax.numpy as jnp
from jax import lax
import numpy as np

D_MODEL = 1024
BATCH = 16
SEQ = 2048
DEPTH = 4

CHUNK = 64
N_META = 16
SSD_PAD = (-N_META) % CHUNK
D_MIX = 2 * D_MODEL
D_A = D_MIX // 4
D_B = D_MIX // 2
D_C = D_MIX // 4
CONV_A_K = 3
SSM_HEAD_DIM = 64
SSM_HEADS = D_B // SSM_HEAD_DIM
SSM_GROUPS = 2
SSM_HPG = SSM_HEADS // SSM_GROUPS
SSM_STATE = 128
SSM_CONV_K = 4
CONF_K = 31
NORM_EPS = 1e-6
LN_EPS = 1e-5

IN_SIZES = [D_A, D_A, D_A, D_A,
            D_B, D_B, SSM_GROUPS * SSM_STATE, SSM_GROUPS * SSM_STATE, SSM_HEADS,
            D_C, D_C, D_C]
N_IN = int(sum(IN_SIZES))
IN_SPLITS = [int(v) for v in np.cumsum(IN_SIZES)[:-1]]

kernel_name = "hybrid_conv_ssd_conformer_trunk"


def rmsnorm(x, g):
    xf = x.astype(jnp.float32)
    y = xf * lax.rsqrt(jnp.mean(xf * xf, axis=-1, keepdims=True) + NORM_EPS)
    return (y * g.astype(jnp.float32)).astype(x.dtype)


def layernorm(x, g, b):
    xf = x.astype(jnp.float32)
    mu = jnp.mean(xf, axis=-1, keepdims=True)
    var = jnp.mean(jnp.square(xf - mu), axis=-1, keepdims=True)
    y = (xf - mu) * lax.rsqrt(var + LN_EPS)
    return (y * g.astype(jnp.float32) + b.astype(jnp.float32)).astype(x.dtype)


def causal_dwconv(x, w, b=None):
    k, c = w.shape
    out = lax.conv_general_dilated(
        x, w.astype(x.dtype).reshape(k, 1, c), window_strides=(1,), padding=[(k - 1, 0)],
        dimension_numbers=('NWC', 'WIO', 'NWC'), feature_group_count=c)
    if b is not None:
        out = out + b.astype(x.dtype)
    return out


def ssd_scan(x, b_mat, c_mat, dt_raw, dt_bias, a_log, d_skip):
    in_dtype = x.dtype
    f32 = jnp.float32
    bsz, l, _ = x.shape
    dt = jax.nn.softplus(dt_raw.astype(f32) + dt_bias.astype(f32))
    a = -jnp.exp(a_log.astype(f32))
    xg = x.astype(f32).reshape(bsz, l, SSM_GROUPS, SSM_HPG, SSM_HEAD_DIM)
    dtg = dt.reshape(bsz, l, SSM_GROUPS, SSM_HPG)
    xdt = xg * dtg[..., None]
    da = dtg * a.reshape(SSM_GROUPS, SSM_HPG)
    bm = b_mat.astype(f32).reshape(bsz, l, SSM_GROUPS, SSM_STATE)
    cm = c_mat.astype(f32).reshape(bsz, l, SSM_GROUPS, SSM_STATE)

    def front_pad(t):
        return jnp.pad(t, [(0, 0), (SSD_PAD, 0)] + [(0, 0)] * (t.ndim - 2))

    xdt, da, bm, cm = front_pad(xdt), front_pad(da), front_pad(bm), front_pad(cm)
    lp = l + SSD_PAD
    nc = lp // CHUNK
    xdt = xdt.reshape(bsz, nc, CHUNK, SSM_GROUPS, SSM_HPG, SSM_HEAD_DIM)
    bm = bm.reshape(bsz, nc, CHUNK, SSM_GROUPS, SSM_STATE)
    cm = cm.reshape(bsz, nc, CHUNK, SSM_GROUPS, SSM_STATE)
    da = da.reshape(bsz, nc, CHUNK, SSM_GROUPS, SSM_HPG).transpose(0, 3, 4, 1, 2)
    a_cum = jnp.cumsum(da, axis=-1)

    causal = jnp.tril(jnp.ones((CHUNK, CHUNK), dtype=bool))
    seg = a_cum[..., :, None] - a_cum[..., None, :]
    decay = jnp.exp(jnp.where(causal, seg, -jnp.inf))
    cb = jnp.einsum('bclgn,bcsgn->bcgls', cm, bm)
    y_diag = jnp.einsum('bcgls,bgecls,bcsgep->bclgep', cb, decay, xdt)

    decay_states = jnp.exp(a_cum[..., -1:] - a_cum)
    states = jnp.einsum('bclgn,bgecl,bclgep->bcgepn', bm, decay_states, xdt)
    chunk_decay = jnp.exp(a_cum[..., -1])

    def step(carry, inp):
        st, dec = inp
        new = carry * dec[..., None, None] + st
        return new, carry

    init = jnp.zeros((bsz, SSM_GROUPS, SSM_HPG, SSM_HEAD_DIM, SSM_STATE), f32)
    _, prev = lax.scan(step, init, (jnp.moveaxis(states, 1, 0), jnp.moveaxis(chunk_decay, 3, 0)))
    prev = jnp.moveaxis(prev, 0, 1)

    y_off = jnp.einsum('bclgn,bcgepn,bgecl->bclgep', cm, prev, jnp.exp(a_cum))
    y = (y_diag + y_off).reshape(bsz, lp, SSM_GROUPS, SSM_HPG, SSM_HEAD_DIM)[:, SSD_PAD:]
    y = y + xg * d_skip.astype(f32).reshape(SSM_GROUPS, SSM_HPG)[..., None]
    return y.reshape(bsz, l, D_B).astype(in_dtype)


def hybrid_mixer(h, w_in, w_out, conv_a_w, ssm_conv_w, ssm_conv_b, dt_bias, a_log, d_skip,
                 ssm_norm_g, conf_conv_w, conf_conv_b, conf_ln_g, conf_ln_b):
    proj = jnp.einsum('bld,dn->bln', h, w_in.astype(h.dtype))
    (a_b, a_c, a_x, a_z, b_z, b_x, b_bm, b_cm, b_dt, c_a, c_g, c_z) = jnp.split(proj, IN_SPLITS, axis=-1)

    y_a = a_b * causal_dwconv(a_c * a_x, conv_a_w) * jax.nn.silu(a_z)

    xbc = jax.nn.silu(causal_dwconv(jnp.concatenate([b_x, b_bm, b_cm], axis=-1), ssm_conv_w, ssm_conv_b))
    xs, bs, cs = jnp.split(xbc, [D_B, D_B + SSM_GROUPS * SSM_STATE], axis=-1)
    y_b = ssd_scan(xs, bs, cs, b_dt, dt_bias, a_log, d_skip)
    y_b = rmsnorm(y_b * jax.nn.silu(b_z), ssm_norm_g)

    u = c_a * jax.nn.sigmoid(c_g)
    u = causal_dwconv(u, conf_conv_w, conf_conv_b)
    u = layernorm(u, conf_ln_g, conf_ln_b)
    y_c = jax.nn.silu(u) * jax.nn.silu(c_z)

    y = jnp.concatenate([y_a, y_b, y_c], axis=-1)
    return jnp.einsum('blm,md->bld', y, w_out.astype(y.dtype))


def _fwd_setup_inputs(seed: int = 0) -> dict:
    key = jax.random.key(seed)
    ks = jax.random.split(key, 20)
    f32 = jnp.float32
    x = jax.random.normal(ks[0], (BATCH, SEQ, D_MODEL), f32)
    meta = jax.random.normal(ks[1], (N_META, D_MODEL), f32)
    pre_g = 1.0 + 0.05 * jax.random.normal(ks[2], (DEPTH, D_MODEL), f32)
    post_g = 1.0 + 0.05 * jax.random.normal(ks[3], (DEPTH, D_MODEL), f32)
    w_in = jax.random.normal(ks[4], (DEPTH, D_MODEL, N_IN), f32) * D_MODEL ** -0.5
    w_out = jax.random.normal(ks[5], (DEPTH, D_MIX, D_MODEL), f32) * D_MIX ** -0.5
    conv_a_w = jax.random.normal(ks[6], (DEPTH, CONV_A_K, D_A), f32) * CONV_A_K ** -0.5
    n_xbc = D_B + 2 * SSM_GROUPS * SSM_STATE
    ssm_conv_w = jax.random.normal(ks[7], (DEPTH, SSM_CONV_K, n_xbc), f32) * SSM_CONV_K ** -0.5
    ssm_conv_b = 0.02 * jax.random.normal(ks[8], (DEPTH, n_xbc), f32)
    dt0 = jnp.exp(jax.random.uniform(ks[9], (DEPTH, SSM_HEADS), f32, math.log(1e-3), math.log(1e-1)))
    dt_bias = dt0 + jnp.log(-jnp.expm1(-dt0))
    a_log = jnp.log(jax.random.uniform(ks[10], (DEPTH, SSM_HEADS), f32, 1.0, 16.0))
    d_skip = 1.0 + 0.1 * jax.random.normal(ks[11], (DEPTH, SSM_HEADS), f32)
    ssm_norm_g = 1.0 + 0.05 * jax.random.normal(ks[12], (DEPTH, D_B), f32)
    conf_conv_w = jax.random.normal(ks[13], (DEPTH, CONF_K, D_C), f32) * CONF_K ** -0.5
    conf_conv_b = 0.02 * jax.random.normal(ks[14], (DEPTH, D_C), f32)
    conf_ln_g = 1.0 + 0.05 * jax.random.normal(ks[15], (DEPTH, D_C), f32)
    conf_ln_b = 0.02 * jax.random.normal(ks[16], (DEPTH, D_C), f32)
    return {"x": x, "meta": meta, "pre_g": pre_g, "post_g": post_g, "w_in": w_in, "w_out": w_out,
            "conv_a_w": conv_a_w, "ssm_conv_w": ssm_conv_w, "ssm_conv_b": ssm_conv_b,
            "dt_bias": dt_bias, "a_log": a_log, "d_skip": d_skip, "ssm_norm_g": ssm_norm_g,
            "conf_conv_w": conf_conv_w, "conf_conv_b": conf_conv_b,
            "conf_ln_g": conf_ln_g, "conf_ln_b": conf_ln_b}


def _fwd_reference(x, meta, pre_g, post_g, w_in, w_out, conv_a_w, ssm_conv_w, ssm_conv_b,
              dt_bias, a_log, d_skip, ssm_norm_g, conf_conv_w, conf_conv_b, conf_ln_g, conf_ln_b):
    bsz = x.shape[0]
    meta_b = jnp.broadcast_to(meta.astype(x.dtype)[None], (bsz, N_META, D_MODEL))
    h = jnp.concatenate([meta_b, x], axis=1)
    for i in range(DEPTH):
        m = hybrid_mixer(rmsnorm(h, pre_g[i]), w_in[i], w_out[i], conv_a_w[i], ssm_conv_w[i],
                         ssm_conv_b[i], dt_bias[i], a_log[i], d_skip[i], ssm_norm_g[i],
                         conf_conv_w[i], conf_conv_b[i], conf_ln_g[i], conf_ln_b[i])
        h = h + rmsnorm(m, post_g[i])
    return h[:, N_META:]


import jax as _jax
import jax.numpy as _jnp

TWIN_FORMAT = 'train_step'
FWD_PARAMS = ['x', 'meta', 'pre_g', 'post_g', 'w_in', 'w_out', 'conv_a_w', 'ssm_conv_w', 'ssm_conv_b', 'dt_bias', 'a_log', 'd_skip', 'ssm_norm_g', 'conf_conv_w', 'conf_conv_b', 'conf_ln_g', 'conf_ln_b']
TWIN_WEIGHTS = ['meta', 'pre_g', 'post_g', 'w_in', 'w_out', 'conv_a_w', 'ssm_conv_w', 'ssm_conv_b', 'dt_bias', 'a_log', 'd_skip', 'ssm_norm_g', 'conf_conv_w', 'conf_conv_b', 'conf_ln_g', 'conf_ln_b']
TWIN_DIFF_INPUT = 'x'
TWIN_INPUTS = ['x', 'meta', 'pre_g', 'post_g', 'w_in', 'w_out', 'conv_a_w', 'ssm_conv_w', 'ssm_conv_b', 'dt_bias', 'a_log', 'd_skip', 'ssm_norm_g', 'conf_conv_w', 'conf_conv_b', 'conf_ln_g', 'conf_ln_b', 'loss_target', 'm_meta', 'm_pre_g', 'm_post_g', 'm_w_in', 'm_w_out', 'm_conv_a_w', 'm_ssm_conv_w', 'm_ssm_conv_b', 'm_dt_bias', 'm_a_log', 'm_d_skip', 'm_ssm_norm_g', 'm_conf_conv_w', 'm_conf_conv_b', 'm_conf_ln_g', 'm_conf_ln_b', 'v_meta', 'v_pre_g', 'v_post_g', 'v_w_in', 'v_w_out', 'v_conv_a_w', 'v_ssm_conv_w', 'v_ssm_conv_b', 'v_dt_bias', 'v_a_log', 'v_d_skip', 'v_ssm_norm_g', 'v_conf_conv_w', 'v_conf_conv_b', 'v_conf_ln_g', 'v_conf_ln_b']
TWIN_OUTPUTS = ['loss', 'grad_x', 'grad_meta', 'grad_pre_g', 'grad_post_g', 'grad_w_in', 'grad_w_out', 'grad_conv_a_w', 'grad_ssm_conv_w', 'grad_ssm_conv_b', 'grad_dt_bias', 'grad_a_log', 'grad_d_skip', 'grad_ssm_norm_g', 'grad_conf_conv_w', 'grad_conf_conv_b', 'grad_conf_ln_g', 'grad_conf_ln_b', 'delta_meta', 'delta_pre_g', 'delta_post_g', 'delta_w_in', 'delta_w_out', 'delta_conv_a_w', 'delta_ssm_conv_w', 'delta_ssm_conv_b', 'delta_dt_bias', 'delta_a_log', 'delta_d_skip', 'delta_ssm_norm_g', 'delta_conf_conv_w', 'delta_conf_conv_b', 'delta_conf_ln_g', 'delta_conf_ln_b', 'new_m_meta', 'new_m_pre_g', 'new_m_post_g', 'new_m_w_in', 'new_m_w_out', 'new_m_conv_a_w', 'new_m_ssm_conv_w', 'new_m_ssm_conv_b', 'new_m_dt_bias', 'new_m_a_log', 'new_m_d_skip', 'new_m_ssm_norm_g', 'new_m_conf_conv_w', 'new_m_conf_conv_b', 'new_m_conf_ln_g', 'new_m_conf_ln_b', 'new_v_meta', 'new_v_pre_g', 'new_v_post_g', 'new_v_w_in', 'new_v_w_out', 'new_v_conv_a_w', 'new_v_ssm_conv_w', 'new_v_ssm_conv_b', 'new_v_dt_bias', 'new_v_a_log', 'new_v_d_skip', 'new_v_ssm_norm_g', 'new_v_conf_conv_w', 'new_v_conf_conv_b', 'new_v_conf_ln_g', 'new_v_conf_ln_b']
TWIN_LEAF_KINDS = {'loss': 'loss', 'grad_x': 'grad_x', 'grad_meta': 'grad_w', 'grad_pre_g': 'grad_w', 'grad_post_g': 'grad_w', 'grad_w_in': 'grad_w', 'grad_w_out': 'grad_w', 'grad_conv_a_w': 'grad_w', 'grad_ssm_conv_w': 'grad_w', 'grad_ssm_conv_b': 'grad_w', 'grad_dt_bias': 'grad_w', 'grad_a_log': 'grad_w', 'grad_d_skip': 'grad_w', 'grad_ssm_norm_g': 'grad_w', 'grad_conf_conv_w': 'grad_w', 'grad_conf_conv_b': 'grad_w', 'grad_conf_ln_g': 'grad_w', 'grad_conf_ln_b': 'grad_w', 'delta_meta': 'delta_w', 'delta_pre_g': 'delta_w', 'delta_post_g': 'delta_w', 'delta_w_in': 'delta_w', 'delta_w_out': 'delta_w', 'delta_conv_a_w': 'delta_w', 'delta_ssm_conv_w': 'delta_w', 'delta_ssm_conv_b': 'delta_w', 'delta_dt_bias': 'delta_w', 'delta_a_log': 'delta_w', 'delta_d_skip': 'delta_w', 'delta_ssm_norm_g': 'delta_w', 'delta_conf_conv_w': 'delta_w', 'delta_conf_conv_b': 'delta_w', 'delta_conf_ln_g': 'delta_w', 'delta_conf_ln_b': 'delta_w', 'new_m_meta': 'new_m', 'new_m_pre_g': 'new_m', 'new_m_post_g': 'new_m', 'new_m_w_in': 'new_m', 'new_m_w_out': 'new_m', 'new_m_conv_a_w': 'new_m', 'new_m_ssm_conv_w': 'new_m', 'new_m_ssm_conv_b': 'new_m', 'new_m_dt_bias': 'new_m', 'new_m_a_log': 'new_m', 'new_m_d_skip': 'new_m', 'new_m_ssm_norm_g': 'new_m', 'new_m_conf_conv_w': 'new_m', 'new_m_conf_conv_b': 'new_m', 'new_m_conf_ln_g': 'new_m', 'new_m_conf_ln_b': 'new_m', 'new_v_meta': 'new_v', 'new_v_pre_g': 'new_v', 'new_v_post_g': 'new_v', 'new_v_w_in': 'new_v', 'new_v_w_out': 'new_v', 'new_v_conv_a_w': 'new_v', 'new_v_ssm_conv_w': 'new_v', 'new_v_ssm_conv_b': 'new_v', 'new_v_dt_bias': 'new_v', 'new_v_a_log': 'new_v', 'new_v_d_skip': 'new_v', 'new_v_ssm_norm_g': 'new_v', 'new_v_conf_conv_w': 'new_v', 'new_v_conf_conv_b': 'new_v', 'new_v_conf_ln_g': 'new_v', 'new_v_conf_ln_b': 'new_v'}


def _forward(args):
    return _fwd_reference(*[args[k] for k in FWD_PARAMS])


def _output_shape():
    out = _jax.eval_shape(lambda: _forward(_fwd_setup_inputs(0)))
    return out.shape, out.dtype

N_MICROBATCH = 1
ADAM_LR = 0.001
ADAM_B1 = 0.9
ADAM_B2 = 0.999
ADAM_EPS = 1e-08
ADAM_WD = 0.01
ADAM_STEP = 10
PER_EXAMPLE_BATCH_AXIS = {'x': 0, 'loss_target': 0}
SHARED_INPUTS = []
_WEIGHT_DTYPES = {'meta': _jnp.float32, 'pre_g': _jnp.float32, 'post_g': _jnp.float32, 'w_in': _jnp.float32, 'w_out': _jnp.float32, 'conv_a_w': _jnp.float32, 'ssm_conv_w': _jnp.float32, 'ssm_conv_b': _jnp.float32, 'dt_bias': _jnp.float32, 'a_log': _jnp.float32, 'd_skip': _jnp.float32, 'ssm_norm_g': _jnp.float32, 'conf_conv_w': _jnp.float32, 'conf_conv_b': _jnp.float32, 'conf_ln_g': _jnp.float32, 'conf_ln_b': _jnp.float32}
MOMENT_SCALE = {'meta': 5.500168e-02, 'pre_g': 1.319236e+00, 'post_g': 3.198539e+01, 'w_in': 5.402601e-01, 'w_out': 1.947706e+00, 'conv_a_w': 3.718308e-01, 'ssm_conv_w': 1.166012e+00, 'ssm_conv_b': 3.842473e+00, 'dt_bias': 1.431580e+00, 'a_log': 4.606785e+00, 'd_skip': 5.822565e+00, 'ssm_norm_g': 2.049669e+00, 'conf_conv_w': 2.506409e-01, 'conf_conv_b': 1.755373e+00, 'conf_ln_g': 6.397609e-01, 'conf_ln_b': 9.775375e-01}


def _to_microbatches(a, axis):
    t = _jnp.moveaxis(a, axis, 0)
    t = t.reshape((N_MICROBATCH, t.shape[0] // N_MICROBATCH) + t.shape[1:])
    return _jnp.moveaxis(t, 1, axis + 1)


def setup_inputs(seed: int = 0) -> dict:
    inp = _fwd_setup_inputs(seed)
    key = _jax.random.fold_in(_jax.random.key(seed), 7919)
    shape, _ = _output_shape()
    out = dict(inp)
    out["loss_target"] = _jax.random.normal(_jax.random.fold_in(key, 0), shape, _jnp.float32)
    for i, name in enumerate(TWIN_WEIGHTS):
        w = inp[name].astype(_jnp.float32)
        if MOMENT_SCALE is None:
            s = _jnp.sqrt(_jnp.mean(_jnp.square(w)) + 1e-30)
        else:
            s = MOMENT_SCALE[name]
        km, kv = _jax.random.split(_jax.random.fold_in(key, i + 1))
        out[name] = w
        out["m_" + name] = s * _jax.random.normal(km, w.shape, _jnp.float32)
        out["v_" + name] = (s * s) * _jax.random.uniform(kv, w.shape, _jnp.float32, 0.5, 1.5)
    if N_MICROBATCH > 1:
        for name, axis in PER_EXAMPLE_BATCH_AXIS.items():
            out[name] = _to_microbatches(out[name], axis)
    return {'x': out['x'], 'meta': out['meta'], 'pre_g': out['pre_g'], 'post_g': out['post_g'], 'w_in': out['w_in'], 'w_out': out['w_out'], 'conv_a_w': out['conv_a_w'], 'ssm_conv_w': out['ssm_conv_w'], 'ssm_conv_b': out['ssm_conv_b'], 'dt_bias': out['dt_bias'], 'a_log': out['a_log'], 'd_skip': out['d_skip'], 'ssm_norm_g': out['ssm_norm_g'], 'conf_conv_w': out['conf_conv_w'], 'conf_conv_b': out['conf_conv_b'], 'conf_ln_g': out['conf_ln_g'], 'conf_ln_b': out['conf_ln_b'], 'loss_target': out['loss_target'], 'm_meta': out['m_meta'], 'm_pre_g': out['m_pre_g'], 'm_post_g': out['m_post_g'], 'm_w_in': out['m_w_in'], 'm_w_out': out['m_w_out'], 'm_conv_a_w': out['m_conv_a_w'], 'm_ssm_conv_w': out['m_ssm_conv_w'], 'm_ssm_conv_b': out['m_ssm_conv_b'], 'm_dt_bias': out['m_dt_bias'], 'm_a_log': out['m_a_log'], 'm_d_skip': out['m_d_skip'], 'm_ssm_norm_g': out['m_ssm_norm_g'], 'm_conf_conv_w': out['m_conf_conv_w'], 'm_conf_conv_b': out['m_conf_conv_b'], 'm_conf_ln_g': out['m_conf_ln_g'], 'm_conf_ln_b': out['m_conf_ln_b'], 'v_meta': out['v_meta'], 'v_pre_g': out['v_pre_g'], 'v_post_g': out['v_post_g'], 'v_w_in': out['v_w_in'], 'v_w_out': out['v_w_out'], 'v_conv_a_w': out['v_conv_a_w'], 'v_ssm_conv_w': out['v_ssm_conv_w'], 'v_ssm_conv_b': out['v_ssm_conv_b'], 'v_dt_bias': out['v_dt_bias'], 'v_a_log': out['v_a_log'], 'v_d_skip': out['v_d_skip'], 'v_ssm_norm_g': out['v_ssm_norm_g'], 'v_conf_conv_w': out['v_conf_conv_w'], 'v_conf_conv_b': out['v_conf_conv_b'], 'v_conf_ln_g': out['v_conf_ln_g'], 'v_conf_ln_b': out['v_conf_ln_b']}


def _loss(weights, diff, rest, loss_target):
    with _jax.named_scope("forward"):
        args = {**rest, TWIN_DIFF_INPUT: diff, **{k: w.astype(_WEIGHT_DTYPES[k]) for k, w in weights.items()}}
        y = _forward(args)
    with _jax.named_scope("loss_head"):
        err = _jnp.square(y.astype(_jnp.float32) - loss_target)
        return 0.5 * _jnp.sum(_jnp.mean(err, axis=-1)) if err.ndim else 0.5 * err


def _adamw(w, g, m, v):
    m = ADAM_B1 * m + (1.0 - ADAM_B1) * g
    v = ADAM_B2 * v + (1.0 - ADAM_B2) * _jnp.square(g)
    m_hat = m / (1.0 - ADAM_B1 ** ADAM_STEP)
    v_hat = v / (1.0 - ADAM_B2 ** ADAM_STEP)
    delta = -ADAM_LR * (m_hat / (_jnp.sqrt(v_hat) + ADAM_EPS) + ADAM_WD * w)
    return delta, m, v


def reference(x, meta, pre_g, post_g, w_in, w_out, conv_a_w, ssm_conv_w, ssm_conv_b, dt_bias, a_log, d_skip, ssm_norm_g, conf_conv_w, conf_conv_b, conf_ln_g, conf_ln_b, loss_target, m_meta, m_pre_g, m_post_g, m_w_in, m_w_out, m_conv_a_w, m_ssm_conv_w, m_ssm_conv_b, m_dt_bias, m_a_log, m_d_skip, m_ssm_norm_g, m_conf_conv_w, m_conf_conv_b, m_conf_ln_g, m_conf_ln_b, v_meta, v_pre_g, v_post_g, v_w_in, v_w_out, v_conv_a_w, v_ssm_conv_w, v_ssm_conv_b, v_dt_bias, v_a_log, v_d_skip, v_ssm_norm_g, v_conf_conv_w, v_conf_conv_b, v_conf_ln_g, v_conf_ln_b):
    given = dict(x=x, meta=meta, pre_g=pre_g, post_g=post_g, w_in=w_in, w_out=w_out, conv_a_w=conv_a_w, ssm_conv_w=ssm_conv_w, ssm_conv_b=ssm_conv_b, dt_bias=dt_bias, a_log=a_log, d_skip=d_skip, ssm_norm_g=ssm_norm_g, conf_conv_w=conf_conv_w, conf_conv_b=conf_conv_b, conf_ln_g=conf_ln_g, conf_ln_b=conf_ln_b, loss_target=loss_target, m_meta=m_meta, m_pre_g=m_pre_g, m_post_g=m_post_g, m_w_in=m_w_in, m_w_out=m_w_out, m_conv_a_w=m_conv_a_w, m_ssm_conv_w=m_ssm_conv_w, m_ssm_conv_b=m_ssm_conv_b, m_dt_bias=m_dt_bias, m_a_log=m_a_log, m_d_skip=m_d_skip, m_ssm_norm_g=m_ssm_norm_g, m_conf_conv_w=m_conf_conv_w, m_conf_conv_b=m_conf_conv_b, m_conf_ln_g=m_conf_ln_g, m_conf_ln_b=m_conf_ln_b, v_meta=v_meta, v_pre_g=v_pre_g, v_post_g=v_post_g, v_w_in=v_w_in, v_w_out=v_w_out, v_conv_a_w=v_conv_a_w, v_ssm_conv_w=v_ssm_conv_w, v_ssm_conv_b=v_ssm_conv_b, v_dt_bias=v_dt_bias, v_a_log=v_a_log, v_d_skip=v_d_skip, v_ssm_norm_g=v_ssm_norm_g, v_conf_conv_w=v_conf_conv_w, v_conf_conv_b=v_conf_conv_b, v_conf_ln_g=v_conf_ln_g, v_conf_ln_b=v_conf_ln_b)
    weights = {n: given[n] for n in TWIN_WEIGHTS}
    shared = {n: given[n] for n in SHARED_INPUTS}
    per_example = {n: given[n] for n in ['x']}
    grad_fn = _jax.value_and_grad(_loss, argnums=(0, 1))

    def one_microbatch(ex, loss_target):
        ex = dict(ex)
        diff = ex.pop(TWIN_DIFF_INPUT)
        return grad_fn(weights, diff, {**shared, **ex}, loss_target)

    if N_MICROBATCH == 1:
        loss, (grad_w, grad_x) = one_microbatch(per_example, given["loss_target"])
    else:
        def body(carry, xs):
            loss_sum, grad_sum = carry
            l_k, (gw_k, gx_k) = one_microbatch(xs[0], xs[1])
            with _jax.named_scope("update"):
                return (loss_sum + l_k, _jax.tree.map(_jnp.add, grad_sum, gw_k)), gx_k

        init = (_jnp.zeros((), _jnp.float32), _jax.tree.map(_jnp.zeros_like, weights))
        (loss, grad_w), grad_x = _jax.lax.scan(body, init, (per_example, given["loss_target"]))
    with _jax.named_scope("update"):
        delta_w, new_m, new_v = {}, {}, {}
        for n in TWIN_WEIGHTS:
            delta_w[n], new_m[n], new_v[n] = _adamw(weights[n], grad_w[n], given["m_" + n], given["v_" + n])
    return (loss, grad_x, *[grad_w[n] for n in TWIN_WEIGHTS], *[delta_w[n] for n in TWIN_WEIGHTS],
            *[new_m[n] for n in TWIN_WEIGHTS], *[new_v[n] for n in TWIN_WEIGHTS])
```

```python
import jax
import jax.numpy as jnp
from jax import lax
from jax.experimental import pallas as pl
from jax.experimental.pallas import tpu as pltpu

F32 = jnp.float32
BF16 = jnp.bfloat16

N_META = 16
TT = 128
SSM_STATE = 128
SSM_GROUPS = 2
SSM_HEAD_DIM = 64
CONV_A_K = 3
SSM_CONV_K = 4
CONF_K = 31
NORM_EPS = 1e-6
LN_EPS = 1e-5
LANES = 128
DT_PAD = LANES
CONF_HALO = 32
SMALL_HALO = 8
VMEM_LIMIT = 56 * 1024 * 1024
N_CHIPS = 4
N_DEV = 8

ADAM_LR = 0.001
ADAM_B1 = 0.9
ADAM_B2 = 0.999
ADAM_EPS = 1e-08
ADAM_WD = 0.01
ADAM_STEP = 10

MESH = pl.DeviceIdType.MESH
ANY = pl.BlockSpec(memory_space=pl.ANY)


class Dims:
    def __init__(self, bl, seq, d):
        self.BL, self.S, self.D = bl, seq, d
        self.L = seq + N_META
        self.Lp = -(-self.L // TT) * TT
        self.NT = self.Lp // TT
        self.R = bl * self.Lp
        self.DA = d // 2
        self.DB = d
        self.DC = d // 2
        self.H = self.DB // SSM_HEAD_DIM
        self.HPG = self.H // SSM_GROUPS
        self.GN = SSM_GROUPS * SSM_STATE
        self.WA = 4 * self.DA
        self.WB = 2 * self.DB + 2 * self.GN
        self.WC = 3 * self.DC
        self.NP = self.WA + self.WB + self.WC + DT_PAD
        self.NIN = self.WA + self.WB + self.H + self.WC
        self.XBC = self.DB + 2 * self.GN
        assert self.H % 2 == 0 and self.HPG % 2 == 0 and self.H <= DT_PAD
        assert self.DA % LANES == 0 and (self.WA + self.WB) % self.DC == 0 and self.WA % self.DB == 0


def _row_tile(n, target):
    best = None
    for t in range(16, min(n, target) + 1, 16):
        if n % t == 0:
            best = t
    assert best is not None
    return best


def _col_tile(n, target):
    best = None
    for t in range(LANES, min(n, target) + 1, LANES):
        if n % t == 0:
            best = t
    assert best is not None
    return best


def _params(sem=None):
    return pltpu.CompilerParams(dimension_semantics=sem, vmem_limit_bytes=VMEM_LIMIT)


def _sigmoid(x):
    return 1.0 / (1.0 + jnp.exp(-x))


def _silu_and_grad(x):
    s = _sigmoid(x)
    return x * s, s * (1.0 + x * (1.0 - s))


def _dot(a, b):
    return jnp.dot(a, b, preferred_element_type=F32)


def _dot_nt(a, b):
    return lax.dot_general(a, b, (((1,), (1,)), ((), ())), preferred_element_type=F32)


def _dot_tn(a, b):
    return lax.dot_general(a, b, (((0,), (0,)), ((), ())), preferred_element_type=F32)


def _split3(x):
    x1 = x.astype(BF16)
    r1 = x - x1.astype(F32)
    x2 = r1.astype(BF16)
    x3 = (r1 - x2.astype(F32)).astype(BF16)
    return x1, x2, x3


def _embed(x, meta, dm):
    dc = _col_tile(dm.D, 256)
    s, lp = dm.S, dm.Lp

    def body(x_ref, meta_ref, h_ref):
        h_ref[0:N_META, :] = meta_ref[...]
        h_ref[N_META:N_META + s, :] = x_ref[0]
        if lp > N_META + s:
            h_ref[N_META + s:lp, :] = jnp.zeros((lp - N_META - s, dc), F32)

    return pl.pallas_call(
        body, name="embed", grid=(dm.BL, dm.D // dc),
        in_specs=[pl.BlockSpec((1, s, dc), lambda b, j: (b, 0, j)),
                  pl.BlockSpec((N_META, dc), lambda b, j: (0, j))],
        out_specs=pl.BlockSpec((lp, dc), lambda b, j: (b, j)),
        out_shape=jax.ShapeDtypeStruct((dm.R, dm.D), F32),
        compiler_params=_params(("parallel", "parallel")),
    )(x, meta)


def _loss_head(h, target, dm):
    dc = _col_tile(dm.D, 256)
    s, lp, nj = dm.S, dm.Lp, dm.D // dc

    def body(h_ref, t_ref, dh_ref, l_ref):
        diff = h_ref[N_META:N_META + s, :] - t_ref[0]
        dh_ref[0:N_META, :] = jnp.zeros((N_META, dc), F32)
        dh_ref[N_META:N_META + s, :] = diff * (1.0 / dm.D)
        if lp > N_META + s:
            dh_ref[N_META + s:lp, :] = jnp.zeros((lp - N_META - s, dc), F32)
        l_ref[...] = jnp.full((8, LANES), (0.5 / dm.D) * jnp.sum(diff * diff), F32)

    dh, part = pl.pallas_call(
        body, name="loss_head", grid=(dm.BL, nj),
        in_specs=[pl.BlockSpec((lp, dc), lambda b, j: (b, j)),
                  pl.BlockSpec((1, s, dc), lambda b, j: (b, 0, j))],
        out_specs=[pl.BlockSpec((lp, dc), lambda b, j: (b, j)),
                   pl.BlockSpec((8, LANES), lambda b, j: (b * nj + j, 0))],
        out_shape=[jax.ShapeDtypeStruct((dm.R, dm.D), F32),
                   jax.ShapeDtypeStruct((dm.BL * nj * 8, LANES), F32)],
        compiler_params=_params(("parallel", "parallel")),
    )(h, target)
    return dh, jnp.sum(part[::8, 0])


def _unembed(dh, dm):
    dc = _col_tile(dm.D, 256)
    s, lp = dm.S, dm.Lp

    def body(dh_ref, gx_ref, gm_ref):
        gx_ref[0] = dh_ref[N_META:N_META + s, :]

        @pl.when(pl.program_id(1) == 0)
        def _():
            gm_ref[...] = dh_ref[0:N_META, :]

        @pl.when(pl.program_id(1) > 0)
        def _():
            gm_ref[...] = gm_ref[...] + dh_ref[0:N_META, :]

    return pl.pallas_call(
        body, name="unembed", grid=(dm.D // dc, dm.BL),
        in_specs=[pl.BlockSpec((lp, dc), lambda j, b: (b, j))],
        out_specs=[pl.BlockSpec((1, s, dc), lambda j, b: (b, 0, j)),
                   pl.BlockSpec((N_META, dc), lambda j, b: (0, j))],
        out_shape=[jax.ShapeDtypeStruct((dm.BL, s, dm.D), F32),
                   jax.ShapeDtypeStruct((N_META, dm.D), F32)],
        compiler_params=_params(("parallel", "arbitrary")),
    )(dh)


def _fwd_in(h, pre_g, w, dm):
    tm = _row_tile(dm.R, 1088)
    tn = _col_tile(dm.NP, 896)

    def body(h_ref, g_ref, w_ref, proj_ref, hn_ref):
        @pl.when(pl.program_id(1) == 0)
        def _():
            xf = h_ref[...]
            r = lax.rsqrt(jnp.mean(xf * xf, axis=-1, keepdims=True) + NORM_EPS)
            hn_ref[...] = (xf * r * g_ref[...]).astype(BF16)

        proj_ref[...] = _dot(hn_ref[...], w_ref[...])

    return pl.pallas_call(
        body, name="fwd_in", grid=(dm.R // tm, dm.NP // tn),
        in_specs=[pl.BlockSpec((tm, dm.D), lambda i, j: (i, 0)),
                  pl.BlockSpec((1, dm.D), lambda i, j: (0, 0)),
                  pl.BlockSpec((dm.D, tn), lambda i, j: (0, j))],
        out_specs=[pl.BlockSpec((tm, tn), lambda i, j: (i, j)),
                   pl.BlockSpec((tm, dm.D), lambda i, j: (i, 0))],
        out_shape=[jax.ShapeDtypeStruct((dm.R, dm.NP), F32),
                   jax.ShapeDtypeStruct((dm.R, dm.D), BF16)],
        compiler_params=_params(("parallel", "arbitrary")),
    )(h, pre_g, w)


def _fwd_out(ya, yb, yc, w_out, h, post_g, dm):
    tm = _row_tile(dm.Lp, 544)
    tiles_per_seq = dm.Lp // tm
    da, db, dc = dm.DA, dm.DB, dm.DC

    def body(ya_ref, yb_ref, yc_ref, w_ref, h_ref, g_ref, hn_ref, m_ref):
        m = _dot(ya_ref[...], w_ref[0:da, :])
        m = m + _dot(yb_ref[...], w_ref[da:da + db, :])
        m = m + _dot(yc_ref[...], w_ref[da + db:da + db + dc, :])
        m_ref[...] = m
        r = lax.rsqrt(jnp.mean(m * m, axis=-1, keepdims=True) + NORM_EPS)
        t = (pl.program_id(0) % tiles_per_seq) * tm + lax.broadcasted_iota(jnp.int32, (tm, 1), 0)
        keep = (t < dm.L).astype(F32)
        hn_ref[...] = (h_ref[...] + m * r * g_ref[...]) * keep

    row = lambda i: (i, 0)
    fixed = lambda i: (0, 0)
    return pl.pallas_call(
        body, name="fwd_out", grid=(dm.R // tm,),
        in_specs=[pl.BlockSpec((tm, da), row), pl.BlockSpec((tm, db), row), pl.BlockSpec((tm, dc), row),
                  pl.BlockSpec((2 * dm.D, dm.D), fixed), pl.BlockSpec((tm, dm.D), row),
                  pl.BlockSpec((1, dm.D), fixed)],
        out_specs=[pl.BlockSpec((tm, dm.D), row), pl.BlockSpec((tm, dm.D), row)],
        out_shape=[jax.ShapeDtypeStruct((dm.R, dm.D), F32), jax.ShapeDtypeStruct((dm.R, dm.D), F32)],
        compiler_params=_params(("parallel",)),
    )(ya, yb, yc, w_out, h, post_g)


def _bwd_out(dh, m, post_g, w_out, ya, yb, yc, dm):
    tm = _row_tile(dm.R, 272)
    da, db, dc = dm.DA, dm.DB, dm.DC

    def body(dh_ref, m_ref, g_ref, w_ref, ya_ref, yb_ref, yc_ref, dya_ref, dyb_ref, dyc_ref, dw_ref, dg_ref):
        @pl.when(pl.program_id(0) == 0)
        def _():
            dw_ref[...] = jnp.zeros_like(dw_ref)
            dg_ref[...] = jnp.zeros_like(dg_ref)

        m = m_ref[...]
        dh_ = dh_ref[...]
        r = lax.rsqrt(jnp.mean(m * m, axis=-1, keepdims=True) + NORM_EPS)
        n = m * r
        dg_ref[0:1, :] = dg_ref[0:1, :] + jnp.sum(dh_ * n, axis=0, keepdims=True)
        dn = dh_ * g_ref[...]
        dm_ = (r * (dn - n * jnp.mean(dn * n, axis=-1, keepdims=True))).astype(BF16)
        dya_ref[...] = _dot_nt(dm_, w_ref[0:da, :])
        dyb_ref[...] = _dot_nt(dm_, w_ref[da:da + db, :])
        dyc_ref[...] = _dot_nt(dm_, w_ref[da + db:da + db + dc, :])
        dw_ref[0:da, :] = dw_ref[0:da, :] + _dot_tn(ya_ref[...], dm_)
        dw_ref[da:da + db, :] = dw_ref[da:da + db, :] + _dot_tn(yb_ref[...], dm_)
        dw_ref[da + db:da + db + dc, :] = dw_ref[da + db:da + db + dc, :] + _dot_tn(yc_ref[...], dm_)

    row = lambda i: (i, 0)
    fixed = lambda i: (0, 0)
    return pl.pallas_call(
        body, name="bwd_out", grid=(dm.R // tm,),
        in_specs=[pl.BlockSpec((tm, dm.D), row), pl.BlockSpec((tm, dm.D), row), pl.BlockSpec((1, dm.D), fixed),
                  pl.BlockSpec((2 * dm.D, dm.D), fixed),
                  pl.BlockSpec((tm, da), row), pl.BlockSpec((tm, db), row), pl.BlockSpec((tm, dc), row)],
        out_specs=[pl.BlockSpec((tm, da), row), pl.BlockSpec((tm, db), row), pl.BlockSpec((tm, dc), row),
                   pl.BlockSpec((2 * dm.D, dm.D), fixed), pl.BlockSpec((8, dm.D), fixed)],
        out_shape=[jax.ShapeDtypeStruct((dm.R, da), F32), jax.ShapeDtypeStruct((dm.R, db), F32),
                   jax.ShapeDtypeStruct((dm.R, dc), F32),
                   jax.ShapeDtypeStruct((2 * dm.D, dm.D), F32), jax.ShapeDtypeStruct((8, dm.D), F32)],
        compiler_params=_params(("arbitrary",)),
    )(dh, m, post_g, w_out, ya, yb, yc)


def _bwd_in_dx(dpa, dpb, dpc, dpt, w, h, dh, pre_g, dm):
    tm = _row_tile(dm.R, 272)
    wa, wb, wc = dm.WA, dm.WB, dm.WC

    def body(dpa_ref, dpb_ref, dpc_ref, dpt_ref, w_ref, h_ref, dh_ref, g_ref, out_ref, dg_ref):
        @pl.when(pl.program_id(0) == 0)
        def _():
            dg_ref[...] = jnp.zeros_like(dg_ref)

        dhn = _dot_nt(dpa_ref[...], w_ref[:, 0:wa])
        dhn = dhn + _dot_nt(dpb_ref[...], w_ref[:, wa:wa + wb])
        dhn = dhn + _dot_nt(dpc_ref[...], w_ref[:, wa + wb:wa + wb + wc])
        dhn = dhn + _dot_nt(dpt_ref[...], w_ref[:, wa + wb + wc:wa + wb + wc + DT_PAD])
        xf = h_ref[...]
        r = lax.rsqrt(jnp.mean(xf * xf, axis=-1, keepdims=True) + NORM_EPS)
        n = xf * r
        dg_ref[0:1, :] = dg_ref[0:1, :] + jnp.sum(dhn * n, axis=0, keepdims=True)
        dn = dhn * g_ref[...]
        out_ref[...] = dh_ref[...] + r * (dn - n * jnp.mean(dn * n, axis=-1, keepdims=True))

    row = lambda i: (i, 0)
    fixed = lambda i: (0, 0)
    return pl.pallas_call(
        body, name="bwd_in_dx", grid=(dm.R // tm,),
        in_specs=[pl.BlockSpec((tm, wa), row), pl.BlockSpec((tm, wb), row), pl.BlockSpec((tm, wc), row),
                  pl.BlockSpec((tm, DT_PAD), row), pl.BlockSpec((dm.D, dm.NP), fixed),
                  pl.BlockSpec((tm, dm.D), row), pl.BlockSpec((tm, dm.D), row), pl.BlockSpec((1, dm.D), fixed)],
        out_specs=[pl.BlockSpec((tm, dm.D), row), pl.BlockSpec((8, dm.D), fixed)],
        out_shape=[jax.ShapeDtypeStruct((dm.R, dm.D), F32), jax.ShapeDtypeStruct((8, dm.D), F32)],
        compiler_params=_params(("arbitrary",)),
    )(dpa, dpb, dpc, dpt, w, h, dh, pre_g)


def _bwd_in_dw(hn, dp, dm, piece):
    width = dp.shape[1]
    tm = _row_tile(dm.R, 1088)
    tn = _col_tile(width, 512)

    def body(hn_ref, dp_ref, dw_ref):
        @pl.when(pl.program_id(1) == 0)
        def _():
            dw_ref[...] = jnp.zeros_like(dw_ref)

        dw_ref[...] = dw_ref[...] + _dot_tn(hn_ref[...], dp_ref[...])

    return pl.pallas_call(
        body, name="bwd_in_dw_" + piece, grid=(width // tn, dm.R // tm),
        in_specs=[pl.BlockSpec((tm, dm.D), lambda j, i: (i, 0)), pl.BlockSpec((tm, tn), lambda j, i: (i, j))],
        out_specs=pl.BlockSpec((dm.D, tn), lambda j, i: (0, j)),
        out_shape=jax.ShapeDtypeStruct((dm.D, width), F32),
        compiler_params=_params(("parallel", "arbitrary")),
    )(hn, dp)


def _tile_index(dm, reverse):
    if reverse:
        return lambda b, i: b * dm.NT + (dm.NT - 1 - i)
    return lambda b, i: b * dm.NT + i


def _halo_index(dm, rows):
    per_tile = TT // rows
    return lambda b, i: jnp.maximum((b * dm.NT + (dm.NT - 1 - i)) * per_tile - 1, 0)


def _mix_a_fwd(proj, conv_w, dm):
    da = dm.DA
    ti = _tile_index(dm, False)

    def body(ab_ref, ac_ref, ax_ref, az_ref, w_ref, y_ref, pbuf):
        i = pl.program_id(1)

        @pl.when(i == 0)
        def _():
            pbuf[0:SMALL_HALO, :] = jnp.zeros((SMALL_HALO, da), F32)

        @pl.when(i > 0)
        def _():
            pbuf[0:SMALL_HALO, :] = pbuf[TT:TT + SMALL_HALO, :]

        for lb in range(da // LANES):
            cs = slice(lb * LANES, (lb + 1) * LANES)
            p = ac_ref[:, cs] * ax_ref[:, cs]
            pbuf[SMALL_HALO:SMALL_HALO + TT, cs] = p
            q = (w_ref[0:1, cs] * pbuf[6:6 + TT, cs] + w_ref[1:2, cs] * pbuf[7:7 + TT, cs] + w_ref[2:3, cs] * p)
            az = az_ref[:, cs]
            y_ref[:, cs] = (ab_ref[:, cs] * q * (az * _sigmoid(az))).astype(BF16)

    col = lambda k: pl.BlockSpec((TT, da), lambda b, i: (ti(b, i), k))
    return pl.pallas_call(
        body, name="mix_a_fwd", grid=(dm.BL, dm.NT),
        in_specs=[col(0), col(1), col(2), col(3), pl.BlockSpec((CONV_A_K, da), lambda b, i: (0, 0))],
        out_specs=pl.BlockSpec((TT, da), lambda b, i: (ti(b, i), 0)),
        out_shape=jax.ShapeDtypeStruct((dm.R, da), BF16),
        scratch_shapes=[pltpu.VMEM((SMALL_HALO + TT, da), F32)],
        compiler_params=_params(("parallel", "arbitrary")),
    )(proj, proj, proj, proj, conv_w)


def _mix_a_bwd(proj, dya, conv_w, dm):
    da = dm.DA
    ti = _tile_index(dm, True)
    hi = _halo_index(dm, SMALL_HALO)

    def body(ab_ref, ac_ref, ax_ref, az_ref, ach_ref, axh_ref, dy_ref, w_ref, dp_ref, dw_ref, pbuf, dqbuf):
        i = pl.program_id(1)
        halo_on = jnp.where(i == dm.NT - 1, 0.0, 1.0)

        @pl.when(i == 0)
        def _():
            dw_ref[...] = jnp.zeros_like(dw_ref)
            dqbuf[TT:TT + SMALL_HALO, :] = jnp.zeros((SMALL_HALO, da), F32)

        @pl.when(i > 0)
        def _():
            dqbuf[TT:TT + SMALL_HALO, :] = dqbuf[0:SMALL_HALO, :]

        for lb in range(da // LANES):
            cs = slice(lb * LANES, (lb + 1) * LANES)
            pbuf[0:SMALL_HALO, cs] = ach_ref[:, cs] * axh_ref[:, cs] * halo_on
            ac, ax, ab, az = ac_ref[:, cs], ax_ref[:, cs], ab_ref[:, cs], az_ref[:, cs]
            p = ac * ax
            pbuf[SMALL_HALO:SMALL_HALO + TT, cs] = p
            p1 = pbuf[7:7 + TT, cs]
            p2 = pbuf[6:6 + TT, cs]
            w0, w1, w2 = w_ref[0:1, cs], w_ref[1:2, cs], w_ref[2:3, cs]
            q = w0 * p2 + w1 * p1 + w2 * p
            sz, dsz = _silu_and_grad(az)
            dy = dy_ref[:, cs]
            t1 = dy * ab
            dq = t1 * sz
            dqbuf[0:TT, cs] = dq
            dpv = w2 * dq + w1 * dqbuf[1:1 + TT, cs] + w0 * dqbuf[2:2 + TT, cs]
            dp_ref[:, lb * LANES:(lb + 1) * LANES] = (dy * q * sz).astype(BF16)
            dp_ref[:, da + lb * LANES:da + (lb + 1) * LANES] = (dpv * ax).astype(BF16)
            dp_ref[:, 2 * da + lb * LANES:2 * da + (lb + 1) * LANES] = (dpv * ac).astype(BF16)
            dp_ref[:, 3 * da + lb * LANES:3 * da + (lb + 1) * LANES] = (t1 * q * dsz).astype(BF16)
            dw_ref[0, 0:1, cs] = dw_ref[0, 0:1, cs] + jnp.sum(dq * p2, axis=0, keepdims=True)
            dw_ref[0, 1:2, cs] = dw_ref[0, 1:2, cs] + jnp.sum(dq * p1, axis=0, keepdims=True)
            dw_ref[0, 2:3, cs] = dw_ref[0, 2:3, cs] + jnp.sum(dq * p, axis=0, keepdims=True)

    col = lambda k: pl.BlockSpec((TT, da), lambda b, i: (ti(b, i), k))
    halo = lambda k: pl.BlockSpec((SMALL_HALO, da), lambda b, i: (hi(b, i), k))
    return pl.pallas_call(
        body, name="mix_a_bwd", grid=(dm.BL, dm.NT),
        in_specs=[col(0), col(1), col(2), col(3), halo(1), halo(2),
                  pl.BlockSpec((TT, da), lambda b, i: (ti(b, i), 0)),
                  pl.BlockSpec((CONV_A_K, da), lambda b, i: (0, 0))],
        out_specs=[pl.BlockSpec((TT, dm.WA), lambda b, i: (ti(b, i), 0)),
                   pl.BlockSpec((1, 8, da), lambda b, i: (b, 0, 0))],
        out_shape=[jax.ShapeDtypeStruct((dm.R, dm.WA), BF16), jax.ShapeDtypeStruct((dm.BL, 8, da), F32)],
        scratch_shapes=[pltpu.VMEM((SMALL_HALO + TT, da), F32), pltpu.VMEM((TT + SMALL_HALO, da), F32)],
        compiler_params=_params(("parallel", "arbitrary")),
    )(proj, proj, proj, proj, proj, proj, dya, conv_w)


def _conf_conv(ubuf, w_ref, b_ref, u1buf, dc):
    for lb in range(dc // LANES):
        cs = slice(lb * LANES, (lb + 1) * LANES)
        acc = jnp.broadcast_to(b_ref[0:1, cs], (TT, LANES))
        for k in range(CONF_K):
            off = CONF_HALO - (CONF_K - 1) + k
            acc = acc + w_ref[k:k + 1, cs] * ubuf[off:off + TT, cs]
        u1buf[:, cs] = acc


def _mix_c_fwd(proj, conv_w, conv_b, ln_g, ln_b, dm):
    dc = dm.DC
    c0 = (dm.WA + dm.WB) // dc
    ti = _tile_index(dm, False)

    def body(ca_ref, cg_ref, cz_ref, w_ref, b_ref, g_ref, be_ref, y_ref, ubuf, u1buf):
        i = pl.program_id(1)

        @pl.when(i == 0)
        def _():
            ubuf[0:CONF_HALO, :] = jnp.zeros((CONF_HALO, dc), F32)

        @pl.when(i > 0)
        def _():
            ubuf[0:CONF_HALO, :] = ubuf[TT:TT + CONF_HALO, :]

        ubuf[CONF_HALO:CONF_HALO + TT, :] = ca_ref[...] * _sigmoid(cg_ref[...])
        _conf_conv(ubuf, w_ref, b_ref, u1buf, dc)
        u1 = u1buf[...]
        mu = jnp.mean(u1, axis=-1, keepdims=True)
        xc = u1 - mu
        rstd = lax.rsqrt(jnp.mean(xc * xc, axis=-1, keepdims=True) + LN_EPS)
        u2 = xc * rstd * g_ref[...] + be_ref[...]
        cz = cz_ref[...]
        y_ref[...] = ((u2 * _sigmoid(u2)) * (cz * _sigmoid(cz))).astype(BF16)

    col = lambda k: pl.BlockSpec((TT, dc), lambda b, i: (ti(b, i), c0 + k))
    vec = pl.BlockSpec((1, dc), lambda b, i: (0, 0))
    return pl.pallas_call(
        body, name="mix_c_fwd", grid=(dm.BL, dm.NT),
        in_specs=[col(0), col(1), col(2), pl.BlockSpec((CONF_K, dc), lambda b, i: (0, 0)), vec, vec, vec],
        out_specs=pl.BlockSpec((TT, dc), lambda b, i: (ti(b, i), 0)),
        out_shape=jax.ShapeDtypeStruct((dm.R, dc), BF16),
        scratch_shapes=[pltpu.VMEM((CONF_HALO + TT, dc), F32), pltpu.VMEM((TT, dc), F32)],
        compiler_params=_params(("parallel", "arbitrary")),
    )(proj, proj, proj, conv_w, conv_b, ln_g, ln_b)


def _mix_c_bwd(proj, dyc, conv_w, conv_b, ln_g, ln_b, dm):
    dc = dm.DC
    c0 = (dm.WA + dm.WB) // dc
    ti = _tile_index(dm, True)
    hi = _halo_index(dm, CONF_HALO)

    def body(ca_ref, cg_ref, cz_ref, cah_ref, cgh_ref, dy_ref, w_ref, b_ref, g_ref, be_ref,
             dp_ref, dw_ref, dv_ref, ubuf, u1buf, dubuf, du0buf):
        i = pl.program_id(1)
        halo_on = jnp.where(i == dm.NT - 1, 0.0, 1.0)

        @pl.when(i == 0)
        def _():
            dw_ref[...] = jnp.zeros_like(dw_ref)
            dv_ref[...] = jnp.zeros_like(dv_ref)
            dubuf[TT:TT + CONF_HALO, :] = jnp.zeros((CONF_HALO, dc), F32)

        @pl.when(i > 0)
        def _():
            dubuf[TT:TT + CONF_HALO, :] = dubuf[0:CONF_HALO, :]

        ubuf[0:CONF_HALO, :] = cah_ref[...] * _sigmoid(cgh_ref[...]) * halo_on
        sgg = _sigmoid(cg_ref[...])
        ubuf[CONF_HALO:CONF_HALO + TT, :] = ca_ref[...] * sgg
        _conf_conv(ubuf, w_ref, b_ref, u1buf, dc)
        u1 = u1buf[...]
        mu = jnp.mean(u1, axis=-1, keepdims=True)
        xc = u1 - mu
        rstd = lax.rsqrt(jnp.mean(xc * xc, axis=-1, keepdims=True) + LN_EPS)
        xhat = xc * rstd
        u2 = xhat * g_ref[...] + be_ref[...]
        su, dsu = _silu_and_grad(u2)
        sz, dsz = _silu_and_grad(cz_ref[...])
        dy = dy_ref[...]
        du2 = dy * dsu * sz
        dp_ref[:, 2 * dc:3 * dc] = (dy * su * dsz).astype(BF16)
        dxhat = du2 * g_ref[...]
        du1 = rstd * (dxhat - jnp.mean(dxhat, axis=-1, keepdims=True)
                      - xhat * jnp.mean(dxhat * xhat, axis=-1, keepdims=True))
        dv_ref[0, 0:1, :] = dv_ref[0, 0:1, :] + jnp.sum(du1, axis=0, keepdims=True)
        dv_ref[0, 1:2, :] = dv_ref[0, 1:2, :] + jnp.sum(du2 * xhat, axis=0, keepdims=True)
        dv_ref[0, 2:3, :] = dv_ref[0, 2:3, :] + jnp.sum(du2, axis=0, keepdims=True)
        dubuf[0:TT, :] = du1
        for lb in range(dc // LANES):
            cs = slice(lb * LANES, (lb + 1) * LANES)
            d1 = dubuf[0:TT, cs]
            acc = jnp.zeros((TT, LANES), F32)
            for k in range(CONF_K):
                acc = acc + w_ref[k:k + 1, cs] * dubuf[CONF_K - 1 - k:CONF_K - 1 - k + TT, cs]
                off = CONF_HALO - (CONF_K - 1) + k
                dw_ref[0, k:k + 1, cs] = dw_ref[0, k:k + 1, cs] + jnp.sum(
                    d1 * ubuf[off:off + TT, cs], axis=0, keepdims=True)
            du0buf[:, cs] = acc
        du0 = du0buf[...]
        dp_ref[:, 0:dc] = (du0 * sgg).astype(BF16)
        dp_ref[:, dc:2 * dc] = (du0 * ca_ref[...] * sgg * (1.0 - sgg)).astype(BF16)

    col = lambda k: pl.BlockSpec((TT, dc), lambda b, i: (ti(b, i), c0 + k))
    halo = lambda k: pl.BlockSpec((CONF_HALO, dc), lambda b, i: (hi(b, i), c0 + k))
    vec = pl.BlockSpec((1, dc), lambda b, i: (0, 0))
    return pl.pallas_call(
        body, name="mix_c_bwd", grid=(dm.BL, dm.NT),
        in_specs=[col(0), col(1), col(2), halo(0), halo(1),
                  pl.BlockSpec((TT, dc), lambda b, i: (ti(b, i), 0)),
                  pl.BlockSpec((CONF_K, dc), lambda b, i: (0, 0)), vec, vec, vec],
        out_specs=[pl.BlockSpec((TT, dm.WC), lambda b, i: (ti(b, i), 0)),
                   pl.BlockSpec((1, 32, dc), lambda b, i: (b, 0, 0)),
                   pl.BlockSpec((1, 8, dc), lambda b, i: (b, 0, 0))],
        out_shape=[jax.ShapeDtypeStruct((dm.R, dm.WC), BF16),
                   jax.ShapeDtypeStruct((dm.BL, 32, dc), F32),
                   jax.ShapeDtypeStruct((dm.BL, 8, dc), F32)],
        scratch_shapes=[pltpu.VMEM((CONF_HALO + TT, dc), F32), pltpu.VMEM((TT, dc), F32),
                        pltpu.VMEM((TT + CONF_HALO, dc), F32), pltpu.VMEM((TT, dc), F32)],
        compiler_params=_params(("parallel", "arbitrary")),
    )(proj, proj, proj, proj, proj, dyc, conv_w, conv_b, ln_g, ln_b)


def _ssm_conv(rbuf, w_ref, b_ref, width):
    for lb in range(width // LANES):
        cs = slice(lb * LANES, (lb + 1) * LANES)
        acc = jnp.broadcast_to(b_ref[0:1, cs], (TT, LANES))
        for k in range(SSM_CONV_K):
            off = SMALL_HALO - (SSM_CONV_K - 1) + k
            acc = acc + w_ref[k:k + 1, cs] * rbuf[off:off + TT, cs]
        yield cs, acc


def _softplus(z):
    return jnp.maximum(z, 0.0) + jnp.log(1.0 + jnp.exp(-jnp.abs(z)))


def _tri(lower):
    r = lax.broadcasted_iota(jnp.int32, (TT, TT), 0)
    c = lax.broadcasted_iota(jnp.int32, (TT, TT), 1)
    return (c <= r) if lower else (c >= r)


def _exact_01_dot(mat01, x):
    x1, x2, x3 = _split3(x)
    return _dot(mat01, x1) + _dot(mat01, x2) + _dot(mat01, x3)


def _head_scalars(dt_ref, dtb_ref, alog_ref):
    z = dt_ref[...] + dtb_ref[...]
    dtv = _softplus(z)
    a = -jnp.exp(alog_ref[...])
    ac = _exact_01_dot(_tri(True).astype(F32).astype(BF16), dtv * a)
    eac = jnp.exp(ac)
    dst = jnp.exp(ac[TT - 1:TT, :] - ac)
    return z, dtv, a, ac, eac, dst


def _decay(ac, ac_t, h, causal):
    seg = ac[:, h:h + 1] - ac_t[h:h + 1, :]
    return jnp.where(causal, jnp.exp(jnp.where(causal, seg, 0.0)), 0.0)


def _pair_mask(h):
    lane = lax.broadcasted_iota(jnp.int32, (1, LANES), 1)
    return ((lane >= SSM_HEAD_DIM) if (h % 2) else (lane < SSM_HEAD_DIM)).astype(F32)


def _mix_b_fwd(proj, conv_w, conv_b, dt_bias, a_log, dskx, norm_g, expand, dm):
    db, gn, xbc_w, hpg = dm.DB, dm.GN, dm.XBC, dm.HPG
    gw = db // SSM_GROUPS
    ti = _tile_index(dm, False)

    def body(bz_ref, bx_ref, bc_ref, dt_ref, w_ref, b_ref, dtb_ref, alog_ref, dsk_ref, g_ref, e_ref,
             y_ref, yraw_ref, sprev_ref, rbuf, xbuf, state, ybuf, exbuf, xdtbuf):
        i = pl.program_id(1)

        @pl.when(i == 0)
        def _():
            rbuf[0:SMALL_HALO, :] = jnp.zeros((SMALL_HALO, xbc_w), F32)
            state[...] = jnp.zeros_like(state)

        @pl.when(i > 0)
        def _():
            rbuf[0:SMALL_HALO, :] = rbuf[TT:TT + SMALL_HALO, :]

        rbuf[SMALL_HALO:SMALL_HALO + TT, 0:db] = bx_ref[...]
        rbuf[SMALL_HALO:SMALL_HALO + TT, db:xbc_w] = bc_ref[...]
        for cs, pre in _ssm_conv(rbuf, w_ref, b_ref, xbc_w):
            xbuf[:, cs] = pre * _sigmoid(pre)

        _, dtv, _, ac, eac, dst = _head_scalars(dt_ref, dtb_ref, alog_ref)
        exbuf[...] = _dot(jnp.concatenate([dtv, eac, dst], axis=0).astype(BF16), e_ref[...])
        ac_t = ac.T
        causal = _tri(True)
        sprev_ref[0, 0] = state[...]

        xdtbuf[...] = xbuf[:, 0:db] * exbuf[0:TT, :]
        ybuf[...] = xbuf[:, 0:db] * dsk_ref[...]
        for g in range(SSM_GROUPS):
            gs = slice(g * gw, (g + 1) * gw)
            bg = xbuf[:, db + g * SSM_STATE:db + (g + 1) * SSM_STATE].astype(BF16)
            cg = xbuf[:, db + gn + g * SSM_STATE:db + gn + (g + 1) * SSM_STATE].astype(BF16)
            cb = _dot_nt(cg, bg)
            for e in range(0, hpg, 2):
                h = g * hpg + e
                ps = slice(h * SSM_HEAD_DIM, (h + 2) * SSM_HEAD_DIM)
                xp = xdtbuf[:, ps]
                acc = jnp.zeros((TT, LANES), F32)
                for hh in (h, h + 1):
                    mm = (cb * _decay(ac, ac_t, hh, causal)).astype(BF16)
                    acc = acc + _dot(mm, (xp * _pair_mask(hh)).astype(BF16))
                ybuf[:, ps] = ybuf[:, ps] + acc
            sg = state[:, gs]
            ybuf[:, gs] = ybuf[:, gs] + exbuf[TT:2 * TT, gs] * _dot(cg, sg.astype(BF16))
            state[:, gs] = sg * exbuf[2 * TT - 1:2 * TT, gs] + _dot_tn(
                bg, (xdtbuf[:, gs] * exbuf[2 * TT:3 * TT, gs]).astype(BF16))

        yraw = ybuf[...]
        yraw_ref[...] = yraw
        bz = bz_ref[...]
        v = yraw * (bz * _sigmoid(bz))
        r = lax.rsqrt(jnp.mean(v * v, axis=-1, keepdims=True) + NORM_EPS)
        y_ref[...] = (v * r * g_ref[...]).astype(BF16)

    tile = lambda w, k: pl.BlockSpec((TT, w), lambda b, i: (ti(b, i), k))
    fixed = lambda r, w: pl.BlockSpec((r, w), lambda b, i: (0, 0))
    return pl.pallas_call(
        body, name="mix_b_fwd", grid=(dm.BL, dm.NT),
        in_specs=[tile(db, dm.WA // db), tile(db, dm.WA // db + 1), tile(2 * gn, (dm.WA + 2 * db) // (2 * gn)),
                  tile(DT_PAD, dm.NP // DT_PAD - 1),
                  fixed(SSM_CONV_K, xbc_w), fixed(1, xbc_w), fixed(1, DT_PAD), fixed(1, DT_PAD),
                  fixed(1, db), fixed(1, db), fixed(DT_PAD, db)],
        out_specs=[pl.BlockSpec((TT, db), lambda b, i: (ti(b, i), 0)),
                   pl.BlockSpec((TT, db), lambda b, i: (ti(b, i), 0)),
                   pl.BlockSpec((1, 1, SSM_STATE, db), lambda b, i: (b, i, 0, 0))],
        out_shape=[jax.ShapeDtypeStruct((dm.R, db), BF16), jax.ShapeDtypeStruct((dm.R, db), F32),
                   jax.ShapeDtypeStruct((dm.BL, dm.NT, SSM_STATE, db), F32)],
        scratch_shapes=[pltpu.VMEM((SMALL_HALO + TT, xbc_w), F32), pltpu.VMEM((TT, xbc_w), F32),
                        pltpu.VMEM((SSM_STATE, db), F32), pltpu.VMEM((TT, db), F32),
                        pltpu.VMEM((3 * TT, db), F32), pltpu.VMEM((TT, db), F32)],
        compiler_params=_params(("parallel", "arbitrary")),
    )(proj, proj, proj, proj, conv_w, conv_b, dt_bias, a_log, dskx, norm_g, expand)


def _mix_b_bwd(proj, dyb, yraw, sprev, conv_w, conv_b, dt_bias, a_log, dskx, norm_g, expand, expand_t, dm):
    db, gn, xbc_w, hpg = dm.DB, dm.GN, dm.XBC, dm.HPG
    gw = db // SSM_GROUPS
    ti = _tile_index(dm, True)
    hi = _halo_index(dm, SMALL_HALO)

    def body(bz_ref, bx_ref, bc_ref, dt_ref, bxh_ref, bch_ref, dy_ref, yraw_ref, sprev_ref,
             w_ref, b_ref, dtb_ref, alog_ref, dsk_ref, g_ref, e_ref, et_ref,
             dp_ref, dpt_ref, dwc_ref, dch_ref, dhd_ref,
             rbuf, xbuf, dsbuf, dstate, dxbuf, z1buf, dprebuf, exbuf, xdtbuf, dyrbuf, uvec):
        i = pl.program_id(1)
        halo_on = jnp.where(i == dm.NT - 1, 0.0, 1.0)

        @pl.when(i == 0)
        def _():
            dwc_ref[...] = jnp.zeros_like(dwc_ref)
            dch_ref[...] = jnp.zeros_like(dch_ref)
            dhd_ref[...] = jnp.zeros_like(dhd_ref)
            dstate[...] = jnp.zeros_like(dstate)
            dprebuf[TT:TT + SMALL_HALO, :] = jnp.zeros((SMALL_HALO, xbc_w), F32)

        @pl.when(i > 0)
        def _():
            dprebuf[TT:TT + SMALL_HALO, :] = dprebuf[0:SMALL_HALO, :]

        rbuf[0:SMALL_HALO, 0:db] = bxh_ref[...] * halo_on
        rbuf[0:SMALL_HALO, db:xbc_w] = bch_ref[...] * halo_on
        rbuf[SMALL_HALO:SMALL_HALO + TT, 0:db] = bx_ref[...]
        rbuf[SMALL_HALO:SMALL_HALO + TT, db:xbc_w] = bc_ref[...]
        for cs, pre in _ssm_conv(rbuf, w_ref, b_ref, xbc_w):
            sl, dsl = _silu_and_grad(pre)
            xbuf[:, cs] = sl
            dsbuf[:, cs] = dsl

        z, dtv, a, ac, eac, dst = _head_scalars(dt_ref, dtb_ref, alog_ref)
        exbuf[...] = _dot(jnp.concatenate([dtv, eac, dst], axis=0).astype(BF16), e_ref[...])
        ac_t = ac.T
        causal = _tri(True)
        xdtbuf[...] = xbuf[:, 0:db] * exbuf[0:TT, :]

        yraw = yraw_ref[...]
        sz, dsz = _silu_and_grad(bz_ref[...])
        v = yraw * sz
        r = lax.rsqrt(jnp.mean(v * v, axis=-1, keepdims=True) + NORM_EPS)
        dy = dy_ref[...]
        dyg = dy * g_ref[...]
        dv = r * dyg - v * (r * r * r * jnp.mean(dyg * v, axis=-1, keepdims=True))
        dch_ref[0, 0:1, :] = dch_ref[0, 0:1, :] + jnp.sum(dy * v * r, axis=0, keepdims=True)
        dyr = dv * sz
        dyrbuf[...] = dyr
        dp_ref[:, 0:db] = (dv * yraw * dsz).astype(BF16)
        dch_ref[0, 1:2, :] = dch_ref[0, 1:2, :] + jnp.sum(dyr * xbuf[:, 0:db], axis=0, keepdims=True)

        lane_row = lax.broadcasted_iota(jnp.int32, (1, LANES), 1)
        sub_col = lax.broadcasted_iota(jnp.int32, (LANES, 1), 0)
        dac = jnp.zeros((TT, LANES), F32)
        colacc = jnp.zeros((LANES, TT), F32)
        for g in range(SSM_GROUPS):
            gs = slice(g * gw, (g + 1) * gw)
            bs_ = slice(db + g * SSM_STATE, db + (g + 1) * SSM_STATE)
            cs_ = slice(db + gn + g * SSM_STATE, db + gn + (g + 1) * SSM_STATE)
            bg = xbuf[:, bs_].astype(BF16)
            cg = xbuf[:, cs_].astype(BF16)
            cb = _dot_nt(cg, bg)
            dcb = jnp.zeros((TT, TT), F32)
            for e in range(0, hpg, 2):
                h = g * hpg + e
                ps = slice(h * SSM_HEAD_DIM, (h + 2) * SSM_HEAD_DIM)
                xp16 = xdtbuf[:, ps].astype(BF16)
                dyp = dyrbuf[:, ps]
                acc = jnp.zeros((TT, LANES), F32)
                for hh in (h, h + 1):
                    dec = _decay(ac, ac_t, hh, causal)
                    mm = cb * dec
                    dyh = (dyp * _pair_mask(hh)).astype(BF16)
                    dmm = _dot_nt(dyh, xp16)
                    acc = acc + _dot_tn(mm.astype(BF16), dyh)
                    dcb = dcb + dmm * dec
                    gm = dmm * mm
                    dac = dac + jnp.sum(gm, axis=1, keepdims=True) * (lane_row == hh).astype(F32)
                    colacc = colacc + (sub_col == hh).astype(F32) * jnp.sum(gm, axis=0, keepdims=True)
                dxbuf[:, ps] = acc
            sg32 = sprev_ref[0, 0, :, gs]
            sg = sg32.astype(BF16)
            dsn = dstate[:, gs]
            dsn16 = dsn.astype(BF16)
            dcb16 = dcb.astype(BF16)
            eacx = exbuf[TT:2 * TT, gs]
            dstx = exbuf[2 * TT:3 * TT, gs]
            cdx = exbuf[2 * TT - 1:2 * TT, gs]
            dye16 = (dyrbuf[:, gs] * eacx).astype(BF16)
            xdt_g = xdtbuf[:, gs]
            dxbuf[:, cs_] = _dot(dcb16, bg) + _dot_nt(dye16, sg)
            dst_x = dstx * _dot(bg, dsn16)
            dxbuf[:, bs_] = _dot_tn(dcb16, cg) + _dot_nt((dstx * xdt_g).astype(BF16), dsn16)
            dstate[:, gs] = cdx * dsn + _dot_tn(cg, dye16)
            z1buf[:, gs] = dyrbuf[:, gs] * (eacx * _dot(cg, sg)) - xdt_g * dst_x
            uvec[:, gs] = jnp.broadcast_to(
                jnp.sum(xdt_g * dst_x, axis=0, keepdims=True) + jnp.sum(dsn * cdx * sg32, axis=0, keepdims=True),
                (8, gw))
            dxbuf[:, gs] = dxbuf[:, gs] + dst_x

        zz = _dot(jnp.concatenate([z1buf[...], dxbuf[:, 0:db] * xbuf[:, 0:db]], axis=0).astype(BF16), et_ref[...])
        u1, u2, u3 = _split3(uvec[...])
        ulast = (_dot(u1, et_ref[...]) + _dot(u2, et_ref[...]) + _dot(u3, et_ref[...]))[0:1, :]
        is_last = (lax.broadcasted_iota(jnp.int32, (TT, 1), 0) == TT - 1).astype(F32)
        dac = dac - colacc.T + zz[0:TT] + is_last * ulast
        dda = _exact_01_dot(_tri(False).astype(F32).astype(BF16), dac)
        ddt = dda * a + zz[TT:2 * TT]
        dhd_ref[0, 1:2, :] = dhd_ref[0, 1:2, :] + jnp.sum(dda * dtv, axis=0, keepdims=True) * a
        ddtraw = ddt * _sigmoid(z)
        dhd_ref[0, 0:1, :] = dhd_ref[0, 0:1, :] + jnp.sum(ddtraw, axis=0, keepdims=True)
        dpt_ref[...] = ddtraw.astype(BF16)
        dxbuf[:, 0:db] = dyrbuf[...] * dsk_ref[...] + dxbuf[:, 0:db] * exbuf[0:TT, :]

        for lb in range(xbc_w // LANES):
            cs = slice(lb * LANES, (lb + 1) * LANES)
            dpre = dxbuf[:, cs] * dsbuf[:, cs]
            dprebuf[0:TT, cs] = dpre
            dwc_ref[0, SSM_CONV_K:SSM_CONV_K + 1, cs] = dwc_ref[0, SSM_CONV_K:SSM_CONV_K + 1, cs] + jnp.sum(
                dpre, axis=0, keepdims=True)
            draw = w_ref[SSM_CONV_K - 1:SSM_CONV_K, cs] * dpre
            for k in range(SSM_CONV_K - 1):
                sh = SSM_CONV_K - 1 - k
                draw = draw + w_ref[k:k + 1, cs] * dprebuf[sh:sh + TT, cs]
            for k in range(SSM_CONV_K):
                off = SMALL_HALO - (SSM_CONV_K - 1) + k
                dwc_ref[0, k:k + 1, cs] = dwc_ref[0, k:k + 1, cs] + jnp.sum(
                    dpre * rbuf[off:off + TT, cs], axis=0, keepdims=True)
            dp_ref[:, db + lb * LANES:db + (lb + 1) * LANES] = draw.astype(BF16)

    tile = lambda w, k: pl.BlockSpec((TT, w), lambda b, i: (ti(b, i), k))
    halo = lambda w, k: pl.BlockSpec((SMALL_HALO, w), lambda b, i: (hi(b, i), k))
    fixed = lambda r, w: pl.BlockSpec((r, w), lambda b, i: (0, 0))
    kz = dm.WA // db
    kc = (dm.WA + 2 * db) // (2 * gn)
    return pl.pallas_call(
        body, name="mix_b_bwd", grid=(dm.BL, dm.NT),
        in_specs=[tile(db, kz), tile(db, kz + 1), tile(2 * gn, kc), tile(DT_PAD, dm.NP // DT_PAD - 1),
                  halo(db, kz + 1), halo(2 * gn, kc),
                  pl.BlockSpec((TT, db), lambda b, i: (ti(b, i), 0)),
                  pl.BlockSpec((TT, db), lambda b, i: (ti(b, i), 0)),
                  pl.BlockSpec((1, 1, SSM_STATE, db), lambda b, i: (b, dm.NT - 1 - i, 0, 0)),
                  fixed(SSM_CONV_K, xbc_w), fixed(1, xbc_w), fixed(1, DT_PAD), fixed(1, DT_PAD),
                  fixed(1, db), fixed(1, db), fixed(DT_PAD, db), fixed(db, DT_PAD)],
        out_specs=[pl.BlockSpec((TT, dm.WB), lambda b, i: (ti(b, i), 0)),
                   pl.BlockSpec((TT, DT_PAD), lambda b, i: (ti(b, i), 0)),
                   pl.BlockSpec((1, 8, xbc_w), lambda b, i: (b, 0, 0)),
                   pl.BlockSpec((1, 8, db), lambda b, i: (b, 0, 0)),
                   pl.BlockSpec((1, 8, DT_PAD), lambda b, i: (b, 0, 0))],
        out_shape=[jax.ShapeDtypeStruct((dm.R, dm.WB), BF16), jax.ShapeDtypeStruct((dm.R, DT_PAD), BF16),
                   jax.ShapeDtypeStruct((dm.BL, 8, xbc_w), F32), jax.ShapeDtypeStruct((dm.BL, 8, db), F32),
                   jax.ShapeDtypeStruct((dm.BL, 8, DT_PAD), F32)],
        scratch_shapes=[pltpu.VMEM((SMALL_HALO + TT, xbc_w), F32), pltpu.VMEM((TT, xbc_w), F32),
                        pltpu.VMEM((TT, xbc_w), F32), pltpu.VMEM((SSM_STATE, db), F32),
                        pltpu.VMEM((TT, xbc_w), F32), pltpu.VMEM((TT, db), F32),
                        pltpu.VMEM((TT + SMALL_HALO, xbc_w), F32), pltpu.VMEM((3 * TT, db), F32),
                        pltpu.VMEM((TT, db), F32), pltpu.VMEM((TT, db), F32), pltpu.VMEM((8, db), F32)],
        compiler_params=_params(("parallel", "arbitrary")),
    )(proj, proj, proj, proj, proj, proj, dyb, yraw, sprev,
      conv_w, conv_b, dt_bias, a_log, dskx, norm_g, expand, expand_t)


def _to_my_columns(w, dm):
    ab = dm.WA + dm.WB
    pad = jnp.zeros(w.shape[:-1] + (DT_PAD - dm.H,), w.dtype)
    return jnp.concatenate([w[..., :ab], w[..., ab + dm.H:], w[..., ab:ab + dm.H], pad], axis=-1)


def _local_step(x, target, p, dm):
    depth = p["w_in"].shape[0]
    head_of = jnp.arange(dm.DB) // SSM_HEAD_DIM
    expand = (jnp.arange(DT_PAD)[:, None] == head_of[None, :]).astype(BF16)
    expand_t = expand.T
    pad_h = lambda v: jnp.pad(v, (0, DT_PAD - dm.H))[None]

    h = _embed(x, p["meta"], dm)
    saved = []
    for i in range(depth):
        proj, hn = _fwd_in(h, p["pre_g"][i][None], p["w_in"][i], dm)
        ya = _mix_a_fwd(proj, p["conv_a_w"][i], dm)
        yb, yraw, sprev = _mix_b_fwd(proj, p["ssm_conv_w"][i], p["ssm_conv_b"][i][None], pad_h(p["dt_bias"][i]),
                                     pad_h(p["a_log"][i]), jnp.repeat(p["d_skip"][i], SSM_HEAD_DIM)[None],
                                     p["ssm_norm_g"][i][None], expand, dm)
        yc = _mix_c_fwd(proj, p["conf_conv_w"][i], p["conf_conv_b"][i][None], p["conf_ln_g"][i][None],
                        p["conf_ln_b"][i][None], dm)
        h_new, m = _fwd_out(ya, yb, yc, p["w_out"][i], h, p["post_g"][i][None], dm)
        saved.append((h, hn, proj, ya, yb, yc, yraw, sprev, m))
        h = h_new

    dh, loss = _loss_head(h, target, dm)

    g = {k: [None] * depth for k in ("pre_g", "post_g", "w_in", "w_out", "conv_a_w", "ssm_conv_w", "ssm_conv_b",
                                      "dt_bias", "a_log", "d_skip", "ssm_norm_g", "conf_conv_w", "conf_conv_b",
                                      "conf_ln_g", "conf_ln_b")}
    for i in reversed(range(depth)):
        h_in, hn, proj, ya, yb, yc, yraw, sprev, m = saved[i]
        dya, dyb, dyc, dwo, dpost = _bwd_out(dh, m, p["post_g"][i][None], p["w_out"][i], ya, yb, yc, dm)
        dpa, dwa = _mix_a_bwd(proj, dya, p["conv_a_w"][i], dm)
        dpb, dpt, dwcv, dch, dhd = _mix_b_bwd(
            proj, dyb, yraw, sprev, p["ssm_conv_w"][i], p["ssm_conv_b"][i][None], pad_h(p["dt_bias"][i]),
            pad_h(p["a_log"][i]), jnp.repeat(p["d_skip"][i], SSM_HEAD_DIM)[None], p["ssm_norm_g"][i][None],
            expand, expand_t, dm)
        dpc, dwcf, dvc = _mix_c_bwd(proj, dyc, p["conf_conv_w"][i], p["conf_conv_b"][i][None],
                                    p["conf_ln_g"][i][None], p["conf_ln_b"][i][None], dm)
        dh, dpre = _bwd_in_dx(dpa, dpb, dpc, dpt, p["w_in"][i], h_in, dh, p["pre_g"][i][None], dm)
        dw_a = _bwd_in_dw(hn, dpa, dm, "a")
        dw_b = _bwd_in_dw(hn, dpb, dm, "b")
        dw_c = _bwd_in_dw(hn, dpc, dm, "c")
        dw_t = _bwd_in_dw(hn, dpt, dm, "dt")
        g["w_in"][i] = jnp.concatenate([dw_a, dw_b, dw_t[:, :dm.H], dw_c], axis=1)
        g["w_out"][i] = dwo
        g["pre_g"][i] = dpre[0]
        g["post_g"][i] = dpost[0]
        g["conv_a_w"][i] = jnp.sum(dwa, axis=0)[:CONV_A_K]
        dwcv = jnp.sum(dwcv, axis=0)
        g["ssm_conv_w"][i] = dwcv[:SSM_CONV_K]
        g["ssm_conv_b"][i] = dwcv[SSM_CONV_K]
        dch = jnp.sum(dch, axis=0)
        g["ssm_norm_g"][i] = dch[0]
        g["d_skip"][i] = jnp.sum(dch[1].reshape(dm.H, SSM_HEAD_DIM), axis=1)
        dhd = jnp.sum(dhd, axis=0)
        g["dt_bias"][i] = dhd[0, :dm.H]
        g["a_log"][i] = dhd[1, :dm.H]
        g["conf_conv_w"][i] = jnp.sum(dwcf, axis=0)[:CONF_K]
        dvc = jnp.sum(dvc, axis=0)
        g["conf_conv_b"][i] = dvc[0]
        g["conf_ln_g"][i] = dvc[1]
        g["conf_ln_b"][i] = dvc[2]

    grad_x, gmeta = _unembed(dh, dm)
    grads = {k: jnp.stack(v) for k, v in g.items()}
    grads["meta"] = gmeta
    return loss, grad_x, grads


def _adamw_math(w, g, m, v):
    m = ADAM_B1 * m + (1.0 - ADAM_B1) * g
    v = ADAM_B2 * v + (1.0 - ADAM_B2) * (g * g)
    m_hat = m / (1.0 - ADAM_B1 ** ADAM_STEP)
    v_hat = v / (1.0 - ADAM_B2 ** ADAM_STEP)
    delta = -ADAM_LR * (m_hat / (jnp.sqrt(v_hat) + ADAM_EPS) + ADAM_WD * w)
    return delta, m, v


def _adamw_small(w, g, m, v, name):
    def body(w_ref, g_ref, m_ref, v_ref, d_out, m_out, v_out):
        d_out[...], m_out[...], v_out[...] = _adamw_math(w_ref[...], g_ref[...], m_ref[...], v_ref[...])

    shape = jax.ShapeDtypeStruct(w.shape, F32)
    return pl.pallas_call(body, name="adamw_" + name, out_shape=[shape, shape, shape],
                          compiler_params=_params())(w, g, m, v)


def _adamw_big(w, g, m, v, name):
    depth, rows, cols = w.shape
    tr = _row_tile(rows, 256)

    def body(w_ref, g_ref, m_ref, v_ref, d_out, m_out, v_out):
        d_out[...], m_out[...], v_out[...] = _adamw_math(w_ref[...], g_ref[...], m_ref[...], v_ref[...])

    spec = pl.BlockSpec((1, tr, cols), lambda l, r: (l, r, 0))
    shape = jax.ShapeDtypeStruct(w.shape, F32)
    return pl.pallas_call(body, name="adamw_" + name, grid=(depth, rows // tr),
                          in_specs=[spec] * 4, out_specs=[spec] * 3, out_shape=[shape] * 3,
                          compiler_params=_params(("parallel", "parallel")))(w, g, m, v)


def _place():
    x, y, c = lax.axis_index("x"), lax.axis_index("y"), lax.axis_index("c")
    chips = [(1 - x, y), (x, 1 - y), (1 - x, 1 - y)]
    return x, y, c, chips


def _gather_weights(big, small):
    nb, ns = len(big), len(small)
    half = [a.shape[0] // 2 for a in big]

    def body(*refs):
        ins, outs = refs[:nb + ns], refs[nb + ns:2 * (nb + ns)]
        send_sems, recv_sems, local_sems = refs[2 * (nb + ns):]
        x, y, c, chips = _place()
        me = 2 * x + y
        sib = (x, y, 1 - c)
        local = []
        for a in range(nb + ns):
            cp = pltpu.make_async_copy(ins[a], outs[a].at[me], local_sems.at[a])
            cp.start()
            local.append(cp)

        def rcopy(k, src, dst, to):
            return pltpu.make_async_remote_copy(src_ref=src, dst_ref=dst, send_sem=send_sems.at[k],
                                                recv_sem=recv_sems.at[k], device_id=to, device_id_type=MESH)

        sends = []
        for a in range(nb):
            mine = pl.ds(c * half[a], half[a])
            for j, chip in enumerate(chips):
                cp = rcopy(6 * a + j, ins[a].at[mine], outs[a].at[me, mine], (*chip, c))
                cp.start()
                sends.append(cp)
        for a in range(ns):
            for j, chip in enumerate(chips):
                cp = rcopy(6 * nb + 3 * a + j, ins[nb + a], outs[nb + a].at[me], (*chip, c))
                cp.start()
                sends.append(cp)
        for a in range(nb):
            mine = pl.ds(c * half[a], half[a])
            for j, chip in enumerate(chips):
                frm = 2 * chip[0] + chip[1]
                got = outs[a].at[frm, mine]
                rcopy(6 * a + j, got, got, (*chip, c)).wait_recv()
                cp = rcopy(6 * a + 3 + j, got, got, sib)
                cp.start()
                sends.append(cp)
        for a in range(ns):
            for j, chip in enumerate(chips):
                got = outs[nb + a].at[2 * chip[0] + chip[1]]
                rcopy(6 * nb + 3 * a + j, got, got, (*chip, c)).wait_recv()
        for a in range(nb):
            theirs = pl.ds((1 - c) * half[a], half[a])
            for j, chip in enumerate(chips):
                got = outs[a].at[2 * chip[0] + chip[1], theirs]
                rcopy(6 * a + 3 + j, got, got, sib).wait_recv()
        for cp in sends:
            cp.wait_send()
        for cp in local:
            cp.wait()

    arrays = list(big) + list(small)
    nsem = 6 * nb + 3 * ns
    return pl.pallas_call(
        body, name="gather_weights",
        in_specs=[ANY] * len(arrays), out_specs=[ANY] * len(arrays),
        out_shape=[jax.ShapeDtypeStruct((N_CHIPS,) + a.shape, a.dtype) for a in arrays],
        scratch_shapes=[pltpu.SemaphoreType.DMA((nsem,)), pltpu.SemaphoreType.DMA((nsem,)),
                        pltpu.SemaphoreType.DMA((len(arrays),))],
    )(*arrays)


def _swap_halves(arrays):
    n = len(arrays)

    def body(*refs):
        ins, outs = refs[:n], refs[n:2 * n]
        send_sems, recv_sems = refs[2 * n:]
        x, y, c, _ = _place()
        sib = (x, y, 1 - c)
        cps = []
        for a in range(n):
            cp = pltpu.make_async_remote_copy(src_ref=ins[a].at[:, 1 - c], dst_ref=outs[a], send_sem=send_sems.at[a],
                                              recv_sem=recv_sems.at[a], device_id=sib, device_id_type=MESH)
            cp.start()
            cps.append(cp)
        for cp in cps:
            cp.wait()

    return pl.pallas_call(
        body, name="grad_swap_halves",
        in_specs=[ANY] * n, out_specs=[ANY] * n,
        out_shape=[jax.ShapeDtypeStruct((a.shape[0],) + a.shape[2:], a.dtype) for a in arrays],
        scratch_shapes=[pltpu.SemaphoreType.DMA((n,)), pltpu.SemaphoreType.DMA((n,))],
    )(*arrays)


def _send_to_owners(arrays):
    n = len(arrays)

    def body(*refs):
        ins, outs = refs[:n], refs[n:2 * n]
        send_sems, recv_sems = refs[2 * n:]
        x, y, c, chips = _place()
        cps = []
        for a in range(n):
            for j, chip in enumerate(chips):
                cp = pltpu.make_async_remote_copy(
                    src_ref=ins[a].at[2 * chip[0] + chip[1]], dst_ref=outs[a].at[j], send_sem=send_sems.at[3 * a + j],
                    recv_sem=recv_sems.at[3 * a + j], device_id=(*chip, c), device_id_type=MESH)
                cp.start()
                cps.append(cp)
        for cp in cps:
            cp.wait()

    return pl.pallas_call(
        body, name="grad_send_to_owners",
        in_specs=[ANY] * n, out_specs=[ANY] * n,
        out_shape=[jax.ShapeDtypeStruct((3,) + a.shape[1:], a.dtype) for a in arrays],
        scratch_shapes=[pltpu.SemaphoreType.DMA((3 * n,)), pltpu.SemaphoreType.DMA((3 * n,))],
    )(*arrays)


def _join_halves(arrays):
    n = len(arrays)

    def body(*refs):
        ins, outs = refs[:n], refs[n:2 * n]
        send_sems, recv_sems, local_sems = refs[2 * n:]
        x, y, c, _ = _place()
        sib = (x, y, 1 - c)
        cps, local = [], []
        for a in range(n):
            lc = pltpu.make_async_copy(ins[a], outs[a].at[c], local_sems.at[a])
            lc.start()
            local.append(lc)
            cp = pltpu.make_async_remote_copy(src_ref=ins[a], dst_ref=outs[a].at[c], send_sem=send_sems.at[a],
                                              recv_sem=recv_sems.at[a], device_id=sib, device_id_type=MESH)
            cp.start()
            cps.append(cp)
        for cp in cps:
            cp.wait()
        for lc in local:
            lc.wait()

    return pl.pallas_call(
        body, name="grad_join_halves",
        in_specs=[ANY] * n, out_specs=[ANY] * n,
        out_shape=[jax.ShapeDtypeStruct((2,) + a.shape, a.dtype) for a in arrays],
        scratch_shapes=[pltpu.SemaphoreType.DMA((n,)), pltpu.SemaphoreType.DMA((n,)),
                        pltpu.SemaphoreType.DMA((n,))],
    )(*arrays)


def _gather_all(buf):
    def body(in_ref, out_ref, send_sems, recv_sems, local_sem):
        x, y, c, _ = _place()
        me = 4 * x + 2 * y + c
        lc = pltpu.make_async_copy(in_ref, out_ref.at[me], local_sem)
        lc.start()
        cps = []
        def peer(k):
            return (1 - x if (k >> 2) & 1 else x, 1 - y if (k >> 1) & 1 else y, 1 - c if k & 1 else c)

        for k in range(1, N_DEV):
            cp = pltpu.make_async_remote_copy(src_ref=in_ref, dst_ref=out_ref.at[me], send_sem=send_sems.at[k - 1],
                                              recv_sem=recv_sems.at[k - 1], device_id=peer(k), device_id_type=MESH)
            cp.start()
            cps.append(cp)
        for k in range(1, N_DEV):
            px, py, pc = peer(k)
            frm = 4 * px + 2 * py + pc
            pltpu.make_async_remote_copy(src_ref=in_ref, dst_ref=out_ref.at[frm], send_sem=send_sems.at[k - 1],
                                         recv_sem=recv_sems.at[k - 1], device_id=(x, y, c),
                                         device_id_type=MESH).wait_recv()
        for cp in cps:
            cp.wait_send()
        lc.wait()

    return pl.pallas_call(
        body, name="small_grads_gather_all",
        in_specs=[ANY], out_specs=ANY,
        out_shape=jax.ShapeDtypeStruct((N_DEV,) + buf.shape, buf.dtype),
        scratch_shapes=[pltpu.SemaphoreType.DMA((N_DEV - 1,)), pltpu.SemaphoreType.DMA((N_DEV - 1,)),
                        pltpu.SemaphoreType.DMA],
    )(buf)


def _sum_leading(buf, name):
    n, rows, cols = buf.shape
    tr = _row_tile(rows, 512) if rows % 16 == 0 else rows

    def body(in_ref, out_ref):
        acc = in_ref[0]
        for k in range(1, n):
            acc = acc + in_ref[k]
        out_ref[...] = acc

    return pl.pallas_call(
        body, name=name, grid=(rows // tr,),
        in_specs=[pl.BlockSpec((n, tr, cols), lambda i: (0, i, 0))],
        out_specs=pl.BlockSpec((tr, cols), lambda i: (i, 0)),
        out_shape=jax.ShapeDtypeStruct((rows, cols), F32),
        compiler_params=_params(("parallel",)),
    )(buf)


def _add_halves(g, got, c, name):
    _, _, nl, rows, cols = g.shape
    tr = _row_tile(rows, 256)

    def body(c_ref, g_ref, got_ref, out_ref):
        out_ref[0] = (g_ref[0, 0] + got_ref[0]).astype(BF16)

    return pl.pallas_call(
        body, name=name,
        grid_spec=pltpu.PrefetchScalarGridSpec(
            num_scalar_prefetch=1, grid=(N_CHIPS, nl, rows // tr),
            in_specs=[pl.BlockSpec((1, 1, 1, tr, cols), lambda s, l, r, c_ref: (s, c_ref[0], l, r, 0)),
                      pl.BlockSpec((1, 1, tr, cols), lambda s, l, r, c_ref: (s, l, r, 0))],
            out_specs=pl.BlockSpec((1, 1, tr, cols), lambda s, l, r, c_ref: (s, l, r, 0))),
        out_shape=jax.ShapeDtypeStruct((N_CHIPS, nl, rows, cols), BF16),
        compiler_params=_params(("parallel", "parallel", "parallel")),
    )(c, g, got)


def _add_owner(p, got, me, name):
    _, nl, rows, cols = p.shape
    tr = _row_tile(rows, 256)

    def body(me_ref, p_ref, got_ref, out_ref):
        acc = p_ref[0, 0].astype(F32)
        for j in range(3):
            acc = acc + got_ref[j, 0].astype(F32)
        out_ref[0] = acc

    return pl.pallas_call(
        body, name=name,
        grid_spec=pltpu.PrefetchScalarGridSpec(
            num_scalar_prefetch=1, grid=(nl, rows // tr),
            in_specs=[pl.BlockSpec((1, 1, tr, cols), lambda l, r, me_ref: (me_ref[0], l, r, 0)),
                      pl.BlockSpec((3, 1, tr, cols), lambda l, r, me_ref: (0, l, r, 0))],
            out_specs=pl.BlockSpec((1, tr, cols), lambda l, r, me_ref: (l, r, 0))),
        out_shape=jax.ShapeDtypeStruct((nl, rows, cols), F32),
        compiler_params=_params(("parallel", "parallel")),
    )(me, p, got)


def _reduce_scatter_big(g_in, g_out):
    c = lax.axis_index("c").astype(jnp.int32).reshape(1)
    me = (2 * lax.axis_index("x") + lax.axis_index("y")).astype(jnp.int32).reshape(1)
    gs = [a.reshape((N_CHIPS, 2, a.shape[1] // 2) + a.shape[2:]) for a in (g_in, g_out)]
    got = _swap_halves(gs)
    ps = [_add_halves(a, b, c, "grad_add_sibling_" + n) for a, b, n in zip(gs, got, ("w_in", "w_out"))]
    got = _send_to_owners(ps)
    qs = [_add_owner(a, b, me, "grad_add_chips_" + n) for a, b, n in zip(ps, got, ("w_in", "w_out"))]
    full = _join_halves(qs)
    return [a.reshape((a.shape[0] * a.shape[1],) + a.shape[2:]) for a in full]


_SHARDED_SMALL = ("meta", "conv_a_w", "ssm_conv_w", "conf_conv_w")
_REPLICATED = ("pre_g", "post_g", "ssm_conv_b", "dt_bias", "a_log", "d_skip", "ssm_norm_g", "conf_conv_b",
               "conf_ln_g", "conf_ln_b")
_WEIGHTS = ("meta", "pre_g", "post_g", "w_in", "w_out", "conv_a_w", "ssm_conv_w", "ssm_conv_b", "dt_bias", "a_log",
            "d_skip", "ssm_norm_g", "conf_conv_w", "conf_conv_b", "conf_ln_g", "conf_ln_b")


def _unshard_last(a):
    return jnp.moveaxis(a, 0, -2).reshape(a.shape[1:-1] + (N_CHIPS * a.shape[-1],))


def _shard_last(a):
    return jnp.moveaxis(a.reshape(a.shape[:-1] + (N_CHIPS, a.shape[-1] // N_CHIPS)), -2, 0)


def kernel(x, meta, pre_g, post_g, w_in, w_out, conv_a_w, ssm_conv_w, ssm_conv_b, dt_bias, a_log, d_skip, ssm_norm_g, conf_conv_w, conf_conv_b, conf_ln_g, conf_ln_b, loss_target, m_meta, m_pre_g, m_post_g, m_w_in, m_w_out, m_conv_a_w, m_ssm_conv_w, m_ssm_conv_b, m_dt_bias, m_a_log, m_d_skip, m_ssm_norm_g, m_conf_conv_w, m_conf_conv_b, m_conf_ln_g, m_conf_ln_b, v_meta, v_pre_g, v_post_g, v_w_in, v_w_out, v_conv_a_w, v_ssm_conv_w, v_ssm_conv_b, v_dt_bias, v_a_log, v_d_skip, v_ssm_norm_g, v_conf_conv_w, v_conf_conv_b, v_conf_ln_g, v_conf_ln_b):
    w = dict(meta=meta, pre_g=pre_g, post_g=post_g, w_in=w_in, w_out=w_out, conv_a_w=conv_a_w,
             ssm_conv_w=ssm_conv_w, ssm_conv_b=ssm_conv_b, dt_bias=dt_bias, a_log=a_log, d_skip=d_skip,
             ssm_norm_g=ssm_norm_g, conf_conv_w=conf_conv_w, conf_conv_b=conf_conv_b, conf_ln_g=conf_ln_g,
             conf_ln_b=conf_ln_b)
    mom = dict(meta=m_meta, pre_g=m_pre_g, post_g=m_post_g, w_in=m_w_in, w_out=m_w_out, conv_a_w=m_conv_a_w,
               ssm_conv_w=m_ssm_conv_w, ssm_conv_b=m_ssm_conv_b, dt_bias=m_dt_bias, a_log=m_a_log, d_skip=m_d_skip,
               ssm_norm_g=m_ssm_norm_g, conf_conv_w=m_conf_conv_w, conf_conv_b=m_conf_conv_b,
               conf_ln_g=m_conf_ln_g, conf_ln_b=m_conf_ln_b)
    vel = dict(meta=v_meta, pre_g=v_pre_g, post_g=v_post_g, w_in=v_w_in, w_out=v_w_out, conv_a_w=v_conv_a_w,
               ssm_conv_w=v_ssm_conv_w, ssm_conv_b=v_ssm_conv_b, dt_bias=v_dt_bias, a_log=v_a_log, d_skip=v_d_skip,
               ssm_norm_g=v_ssm_norm_g, conf_conv_w=v_conf_conv_w, conf_conv_b=v_conf_conv_b,
               conf_ln_g=v_conf_ln_g, conf_ln_b=v_conf_ln_b)
    bl, seq, d = x.shape
    dm = Dims(bl, seq, d)
    depth = w_in.shape[0]
    chip = 2 * lax.axis_index("x") + lax.axis_index("y")

    got = _gather_weights([w_in.astype(BF16), w_out.astype(BF16)],
                          [meta, conv_a_w, ssm_conv_w, conf_conv_w])
    p = dict(w)
    p["w_in"] = _to_my_columns(_unshard_last(got[0]), dm)
    p["w_out"] = jnp.moveaxis(got[1], 0, 1).reshape(depth, 2 * d, d)
    for name, a in zip(_SHARDED_SMALL, got[2:]):
        p[name] = _unshard_last(a)

    loss, grad_x, g = _local_step(x, loss_target, p, dm)
    loss = lax.psum(loss, ("x", "y", "c"))

    g_in = _shard_last(g["w_in"])
    g_out = jnp.moveaxis(g["w_out"].reshape(depth, N_CHIPS, 2 * d // N_CHIPS, d), 1, 0)
    grads = {}
    grads["w_in"], grads["w_out"] = _reduce_scatter_big(g_in, g_out)

    small = [n for n in _WEIGHTS if n not in ("w_in", "w_out")]
    flat = jnp.concatenate([g[n].reshape(-1) for n in small])
    rows = -(-flat.shape[0] // (16 * LANES)) * 16
    flat = jnp.pad(flat, (0, rows * LANES - flat.shape[0])).reshape(rows, LANES)
    total = _sum_leading(_gather_all(flat), "small_grads_sum").reshape(-1)
    off = 0
    for n in small:
        size = g[n].size
        full = total[off:off + size].reshape(g[n].shape)
        off += size
        if n in _SHARDED_SMALL:
            full = lax.dynamic_index_in_dim(_shard_last(full), chip, axis=0, keepdims=False)
        grads[n] = full

    deltas, new_m, new_v = {}, {}, {}
    for n in _WEIGHTS:
        fn = _adamw_big if n in ("w_in", "w_out") else _adamw_small
        deltas[n], new_m[n], new_v[n] = fn(w[n], grads[n], mom[n], vel[n], n)

    return (loss, grad_x, *[grads[n] for n in _WEIGHTS], *[deltas[n] for n in _WEIGHTS],
            *[new_m[n] for n in _WEIGHTS], *[new_v[n] for n in _WEIGHTS])
```

```python
import jax
import jax.numpy as jnp
from jax import lax
from jax.experimental import pallas as pl
from jax.experimental.pallas import tpu as pltpu

F32 = jnp.float32
BF16 = jnp.bfloat16

N_META = 16
TT = 128
SSM_STATE = 128
SSM_GROUPS = 2
SSM_HEAD_DIM = 64
CONV_A_K = 3
SSM_CONV_K = 4
CONF_K = 31
NORM_EPS = 1e-6
LN_EPS = 1e-5
LANES = 128
DT_PAD = LANES
CONF_HALO = 32
SMALL_HALO = 8
VMEM_LIMIT = 56 * 1024 * 1024
N_CHIPS = 4
N_DEV = 8

ADAM_LR = 0.001
ADAM_B1 = 0.9
ADAM_B2 = 0.999
ADAM_EPS = 1e-08
ADAM_WD = 0.01
ADAM_STEP = 10

MESH = pl.DeviceIdType.MESH
ANY = pl.BlockSpec(memory_space=pl.ANY)


class Dims:
    def __init__(self, bl, seq, d):
        self.BL, self.S, self.D = bl, seq, d
        self.L = seq + N_META
        self.Lp = -(-self.L // TT) * TT
        self.NT = self.Lp // TT
        self.R = bl * self.Lp
        self.DA = d // 2
        self.DB = d
        self.DC = d // 2
        self.H = self.DB // SSM_HEAD_DIM
        self.HPG = self.H // SSM_GROUPS
        self.GN = SSM_GROUPS * SSM_STATE
        self.WA = 4 * self.DA
        self.WB = 2 * self.DB + 2 * self.GN
        self.WC = 3 * self.DC
        self.NP = self.WA + self.WB + self.WC + DT_PAD
        self.NIN = self.WA + self.WB + self.H + self.WC
        self.XBC = self.DB + 2 * self.GN
        assert self.H % 2 == 0 and self.HPG % 2 == 0 and self.H <= DT_PAD
        assert self.DA % LANES == 0 and (self.WA + self.WB) % self.DC == 0 and self.WA % self.DB == 0


def _row_tile(n, target):
    best = None
    for t in range(16, min(n, target) + 1, 16):
        if n % t == 0:
            best = t
    assert best is not None
    return best


def _col_tile(n, target):
    best = None
    for t in range(LANES, min(n, target) + 1, LANES):
        if n % t == 0:
            best = t
    assert best is not None
    return best


def _params(sem=None):
    return pltpu.CompilerParams(dimension_semantics=sem, vmem_limit_bytes=VMEM_LIMIT)


def _sigmoid(x):
    return 1.0 / (1.0 + jnp.exp(-x))


def _silu_and_grad(x):
    s = _sigmoid(x)
    return x * s, s * (1.0 + x * (1.0 - s))


def _dot(a, b):
    return jnp.dot(a, b, preferred_element_type=F32)


def _dot_nt(a, b):
    return lax.dot_general(a, b, (((1,), (1,)), ((), ())), preferred_element_type=F32)


def _dot_tn(a, b):
    return lax.dot_general(a, b, (((0,), (0,)), ((), ())), preferred_element_type=F32)


def _split3(x):
    x1 = x.astype(BF16)
    r1 = x - x1.astype(F32)
    x2 = r1.astype(BF16)
    x3 = (r1 - x2.astype(F32)).astype(BF16)
    return x1, x2, x3


class Rider:
    def __init__(self, plan, ins, out_shapes, aliases, nsem):
        self.plan, self.ins, self.out_shapes, self.aliases, self.nsem = plan, list(ins), list(out_shapes), aliases, nsem


def _place():
    x, y, c = lax.axis_index("x"), lax.axis_index("y"), lax.axis_index("c")
    chips = [(1 - x, y), (x, 1 - y), (1 - x, 1 - y)]
    return x, y, c, chips


def _remote(k, src, dst, to, send_sems, recv_sems):
    return pltpu.make_async_remote_copy(src_ref=src, dst_ref=dst, send_sem=send_sems.at[k], recv_sem=recv_sems.at[k],
                                        device_id=to, device_id_type=MESH)


def _call(body, name, grid, in_specs, out_specs, out_shape, scratch_shapes, sem, args, rider=None):
    if rider is None:
        outs = pl.pallas_call(body, name=name, grid=grid, in_specs=in_specs, out_specs=out_specs, out_shape=out_shape,
                              scratch_shapes=scratch_shapes, compiler_params=_params(sem))(*args)
        return list(outs), []
    n_in, n_out, n_scr = len(args), len(out_shape), len(scratch_shapes)
    r_in, r_out = len(rider.ins), len(rider.out_shapes)

    def hosted(*refs):
        ins, rins = refs[:n_in], refs[n_in:n_in + r_in]
        o0 = n_in + r_in
        outs, routs = refs[o0:o0 + n_out], refs[o0 + n_out:o0 + n_out + r_out]
        scr = refs[o0 + n_out + r_out:o0 + n_out + r_out + n_scr]
        send_sems, recv_sems = refs[o0 + n_out + r_out + n_scr:]
        first = pl.program_id(0) == 0
        last = pl.program_id(0) == grid[0] - 1
        for ax in range(1, len(grid)):
            first = jnp.logical_and(first, pl.program_id(ax) == 0)
            last = jnp.logical_and(last, pl.program_id(ax) == grid[ax] - 1)

        @pl.when(first)
        def _():
            starts, _ = rider.plan(rins, routs, send_sems, recv_sems)
            for cp in starts:
                cp.start()

        body(*ins, *outs, *scr)

        @pl.when(last)
        def _():
            _, waits = rider.plan(rins, routs, send_sems, recv_sems)
            for wait in waits:
                wait()

    res = pl.pallas_call(
        hosted, name=name, grid=grid,
        in_specs=list(in_specs) + [ANY] * r_in, out_specs=list(out_specs) + [ANY] * r_out,
        out_shape=list(out_shape) + rider.out_shapes,
        input_output_aliases={n_in + k: n_out + v for k, v in rider.aliases.items()},
        scratch_shapes=list(scratch_shapes) + [pltpu.SemaphoreType.DMA((rider.nsem,)),
                                               pltpu.SemaphoreType.DMA((rider.nsem,))],
        compiler_params=_params(("arbitrary",) * len(grid)),
    )(*args, *rider.ins)
    return list(res[:n_out]), list(res[n_out:])


def _exchange(name, rider):
    r_in, r_out = len(rider.ins), len(rider.out_shapes)

    def body(*refs):
        rins, routs = refs[:r_in], refs[r_in:r_in + r_out]
        send_sems, recv_sems = refs[r_in + r_out:]
        starts, waits = rider.plan(rins, routs, send_sems, recv_sems)
        for cp in starts:
            cp.start()
        for wait in waits:
            wait()

    res = pl.pallas_call(
        body, name=name, in_specs=[ANY] * r_in, out_specs=[ANY] * r_out, out_shape=rider.out_shapes,
        input_output_aliases=dict(rider.aliases),
        scratch_shapes=[pltpu.SemaphoreType.DMA((rider.nsem,)), pltpu.SemaphoreType.DMA((rider.nsem,))],
    )(*rider.ins)
    return list(res)


def _same(arrays):
    return [jax.ShapeDtypeStruct(a.shape, a.dtype) for a in arrays]


def _ride_gather_ici(bases):
    n = len(bases)

    def plan(ins, outs, ss, rs):
        x, y, c, chips = _place()
        me = 2 * x + y
        starts, waits = [], []
        for a in range(n):
            half = outs[a].shape[1] // 2
            mine = pl.ds(c * half, half)
            for j, chip in enumerate(chips):
                cp = _remote(3 * a + j, outs[a].at[me, mine], outs[a].at[me, mine], (*chip, c), ss, rs)
                got = outs[a].at[2 * chip[0] + chip[1], mine]
                starts.append(cp)
                waits += [cp.wait_send, _remote(3 * a + j, got, got, (*chip, c), ss, rs).wait_recv]
        return starts, waits

    return Rider(plan, bases, _same(bases), {a: a for a in range(n)}, 3 * n)


def _ride_gather_d2d(bases):
    n = len(bases)

    def plan(ins, outs, ss, rs):
        x, y, c, chips = _place()
        sib = (x, y, 1 - c)
        starts, waits = [], []
        for a in range(n):
            half = outs[a].shape[1] // 2
            for j, chip in enumerate(chips):
                frm = 2 * chip[0] + chip[1]
                got = outs[a].at[frm, pl.ds(c * half, half)]
                theirs = outs[a].at[frm, pl.ds((1 - c) * half, half)]
                cp = _remote(3 * a + j, got, got, sib, ss, rs)
                starts.append(cp)
                waits += [cp.wait_send, _remote(3 * a + j, theirs, theirs, sib, ss, rs).wait_recv]
        return starts, waits

    return Rider(plan, bases, _same(bases), {a: a for a in range(n)}, 3 * n)


def _ride_gather_small(bases):
    n = len(bases)

    def plan(ins, outs, ss, rs):
        x, y, c, chips = _place()
        me = 2 * x + y
        starts, waits = [], []
        for a in range(n):
            for j, chip in enumerate(chips):
                cp = _remote(3 * a + j, outs[a].at[me], outs[a].at[me], (*chip, c), ss, rs)
                got = outs[a].at[2 * chip[0] + chip[1]]
                starts.append(cp)
                waits += [cp.wait_send, _remote(3 * a + j, got, got, (*chip, c), ss, rs).wait_recv]
        return starts, waits

    return Rider(plan, bases, _same(bases), {a: a for a in range(n)}, 3 * n)


def _ride_swap_halves(gs):
    n = len(gs)

    def plan(ins, outs, ss, rs):
        x, y, c, _ = _place()
        cps = [_remote(a, ins[a].at[:, 1 - c], outs[a], (x, y, 1 - c), ss, rs) for a in range(n)]
        return cps, [cp.wait for cp in cps]

    shapes = [jax.ShapeDtypeStruct((g.shape[0],) + g.shape[2:], g.dtype) for g in gs]
    return Rider(plan, gs, shapes, {}, n)


def _ride_to_owners(ps):
    n = len(ps)

    def plan(ins, outs, ss, rs):
        x, y, c, chips = _place()
        cps = []
        for a in range(n):
            for j, chip in enumerate(chips):
                cps.append(_remote(3 * a + j, ins[a].at[2 * chip[0] + chip[1]], outs[a].at[j], (*chip, c), ss, rs))
        return cps, [cp.wait for cp in cps]

    shapes = [jax.ShapeDtypeStruct((3,) + p.shape[1:], p.dtype) for p in ps]
    return Rider(plan, ps, shapes, {}, 3 * n)


def _ride_join_halves(qs):
    n = len(qs)

    def plan(ins, outs, ss, rs):
        x, y, c, _ = _place()
        sib = (x, y, 1 - c)
        starts, waits = [], []
        for a in range(n):
            cp = _remote(a, outs[a].at[c], outs[a].at[c], sib, ss, rs)
            starts.append(cp)
            waits += [cp.wait_send, _remote(a, outs[a].at[1 - c], outs[a].at[1 - c], sib, ss, rs).wait_recv]
        return starts, waits

    return Rider(plan, qs, _same(qs), {a: a for a in range(n)}, n)


def _ride_gather_all(base):
    def plan(ins, outs, ss, rs):
        x, y, c, _ = _place()
        me = 4 * x + 2 * y + c

        def peer(k):
            return (1 - x if (k >> 2) & 1 else x, 1 - y if (k >> 1) & 1 else y, 1 - c if k & 1 else c)

        starts, waits = [], []
        for k in range(1, N_DEV):
            px, py, pc = peer(k)
            cp = _remote(k - 1, outs[0].at[me], outs[0].at[me], (px, py, pc), ss, rs)
            got = outs[0].at[4 * px + 2 * py + pc]
            starts.append(cp)
            waits += [cp.wait_send, _remote(k - 1, got, got, (px, py, pc), ss, rs).wait_recv]
        return starts, waits

    return Rider(plan, [base], _same([base]), {0: 0}, N_DEV - 1)


def _embed(x, meta, dm):
    dc = _col_tile(dm.D, 256)
    s, lp = dm.S, dm.Lp

    def body(x_ref, meta_ref, h_ref):
        h_ref[0:N_META, :] = meta_ref[...]
        h_ref[N_META:N_META + s, :] = x_ref[0]
        if lp > N_META + s:
            h_ref[N_META + s:lp, :] = jnp.zeros((lp - N_META - s, dc), F32)

    return pl.pallas_call(
        body, name="embed", grid=(dm.BL, dm.D // dc),
        in_specs=[pl.BlockSpec((1, s, dc), lambda b, j: (b, 0, j)),
                  pl.BlockSpec((N_META, dc), lambda b, j: (0, j))],
        out_specs=pl.BlockSpec((lp, dc), lambda b, j: (b, j)),
        out_shape=jax.ShapeDtypeStruct((dm.R, dm.D), F32),
        compiler_params=_params(("parallel", "parallel")),
    )(x, meta)


def _loss_head(h, target, dm):
    dc = _col_tile(dm.D, 256)
    s, lp, nj = dm.S, dm.Lp, dm.D // dc

    def body(h_ref, t_ref, dh_ref, l_ref):
        diff = h_ref[N_META:N_META + s, :] - t_ref[0]
        dh_ref[0:N_META, :] = jnp.zeros((N_META, dc), F32)
        dh_ref[N_META:N_META + s, :] = diff * (1.0 / dm.D)
        if lp > N_META + s:
            dh_ref[N_META + s:lp, :] = jnp.zeros((lp - N_META - s, dc), F32)
        l_ref[...] = jnp.full((8, LANES), (0.5 / dm.D) * jnp.sum(diff * diff), F32)

    dh, part = pl.pallas_call(
        body, name="loss_head", grid=(dm.BL, nj),
        in_specs=[pl.BlockSpec((lp, dc), lambda b, j: (b, j)),
                  pl.BlockSpec((1, s, dc), lambda b, j: (b, 0, j))],
        out_specs=[pl.BlockSpec((lp, dc), lambda b, j: (b, j)),
                   pl.BlockSpec((8, LANES), lambda b, j: (b * nj + j, 0))],
        out_shape=[jax.ShapeDtypeStruct((dm.R, dm.D), F32),
                   jax.ShapeDtypeStruct((dm.BL * nj * 8, LANES), F32)],
        compiler_params=_params(("parallel", "parallel")),
    )(h, target)
    return dh, jnp.sum(part[::8, 0])


def _unembed(dh, dm):
    dc = _col_tile(dm.D, 256)
    s, lp = dm.S, dm.Lp

    def body(dh_ref, gx_ref, gm_ref):
        gx_ref[0] = dh_ref[N_META:N_META + s, :]

        @pl.when(pl.program_id(1) == 0)
        def _():
            gm_ref[...] = dh_ref[0:N_META, :]

        @pl.when(pl.program_id(1) > 0)
        def _():
            gm_ref[...] = gm_ref[...] + dh_ref[0:N_META, :]

    return pl.pallas_call(
        body, name="unembed", grid=(dm.D // dc, dm.BL),
        in_specs=[pl.BlockSpec((lp, dc), lambda j, b: (b, j))],
        out_specs=[pl.BlockSpec((1, s, dc), lambda j, b: (b, 0, j)),
                   pl.BlockSpec((N_META, dc), lambda j, b: (0, j))],
        out_shape=[jax.ShapeDtypeStruct((dm.BL, s, dm.D), F32),
                   jax.ShapeDtypeStruct((N_META, dm.D), F32)],
        compiler_params=_params(("parallel", "arbitrary")),
    )(dh)


def _fwd_in(h, pre_g, w, dm, rider=None):
    tm = _row_tile(dm.R, 1088)
    tn = _col_tile(dm.NP, 896)

    def body(h_ref, g_ref, w_ref, proj_ref, hn_ref):
        @pl.when(pl.program_id(1) == 0)
        def _():
            xf = h_ref[...]
            r = lax.rsqrt(jnp.mean(xf * xf, axis=-1, keepdims=True) + NORM_EPS)
            hn_ref[...] = (xf * r * g_ref[...]).astype(BF16)

        proj_ref[...] = _dot(hn_ref[...], w_ref[...])

    return _call(
        body, "fwd_in", (dm.R // tm, dm.NP // tn),
        [pl.BlockSpec((tm, dm.D), lambda i, j: (i, 0)),
         pl.BlockSpec((1, dm.D), lambda i, j: (0, 0)),
         pl.BlockSpec((dm.D, tn), lambda i, j: (0, j))],
        [pl.BlockSpec((tm, tn), lambda i, j: (i, j)),
         pl.BlockSpec((tm, dm.D), lambda i, j: (i, 0))],
        [jax.ShapeDtypeStruct((dm.R, dm.NP), F32), jax.ShapeDtypeStruct((dm.R, dm.D), BF16)],
        [], ("parallel", "arbitrary"), (h, pre_g, w), rider)


def _fwd_out(ya, yb, yc, w_out, h, post_g, dm, rider=None):
    tm = _row_tile(dm.Lp, 544)
    tiles_per_seq = dm.Lp // tm
    da, db, dc = dm.DA, dm.DB, dm.DC

    def body(ya_ref, yb_ref, yc_ref, w_ref, h_ref, g_ref, hn_ref, m_ref):
        m = _dot(ya_ref[...], w_ref[0:da, :])
        m = m + _dot(yb_ref[...], w_ref[da:da + db, :])
        m = m + _dot(yc_ref[...], w_ref[da + db:da + db + dc, :])
        m_ref[...] = m
        r = lax.rsqrt(jnp.mean(m * m, axis=-1, keepdims=True) + NORM_EPS)
        t = (pl.program_id(0) % tiles_per_seq) * tm + lax.broadcasted_iota(jnp.int32, (tm, 1), 0)
        keep = (t < dm.L).astype(F32)
        hn_ref[...] = (h_ref[...] + m * r * g_ref[...]) * keep

    row = lambda i: (i, 0)
    fixed = lambda i: (0, 0)
    return _call(
        body, "fwd_out", (dm.R // tm,),
        [pl.BlockSpec((tm, da), row), pl.BlockSpec((tm, db), row), pl.BlockSpec((tm, dc), row),
         pl.BlockSpec((2 * dm.D, dm.D), fixed), pl.BlockSpec((tm, dm.D), row), pl.BlockSpec((1, dm.D), fixed)],
        [pl.BlockSpec((tm, dm.D), row), pl.BlockSpec((tm, dm.D), row)],
        [jax.ShapeDtypeStruct((dm.R, dm.D), F32), jax.ShapeDtypeStruct((dm.R, dm.D), F32)],
        [], ("parallel",), (ya, yb, yc, w_out, h, post_g), rider)


def _bwd_out(dh, m, post_g, w_out, ya, yb, yc, dm, rider=None):
    tm = _row_tile(dm.R, 272)
    da, db, dc = dm.DA, dm.DB, dm.DC

    def body(dh_ref, m_ref, g_ref, w_ref, ya_ref, yb_ref, yc_ref, dya_ref, dyb_ref, dyc_ref, dw_ref, dg_ref):
        @pl.when(pl.program_id(0) == 0)
        def _():
            dw_ref[...] = jnp.zeros_like(dw_ref)
            dg_ref[...] = jnp.zeros_like(dg_ref)

        m = m_ref[...]
        dh_ = dh_ref[...]
        r = lax.rsqrt(jnp.mean(m * m, axis=-1, keepdims=True) + NORM_EPS)
        n = m * r
        dg_ref[0:1, :] = dg_ref[0:1, :] + jnp.sum(dh_ * n, axis=0, keepdims=True)
        dn = dh_ * g_ref[...]
        dm_ = (r * (dn - n * jnp.mean(dn * n, axis=-1, keepdims=True))).astype(BF16)
        dya_ref[...] = _dot_nt(dm_, w_ref[0:da, :])
        dyb_ref[...] = _dot_nt(dm_, w_ref[da:da + db, :])
        dyc_ref[...] = _dot_nt(dm_, w_ref[da + db:da + db + dc, :])
        dw_ref[0:da, :] = dw_ref[0:da, :] + _dot_tn(ya_ref[...], dm_)
        dw_ref[da:da + db, :] = dw_ref[da:da + db, :] + _dot_tn(yb_ref[...], dm_)
        dw_ref[da + db:da + db + dc, :] = dw_ref[da + db:da + db + dc, :] + _dot_tn(yc_ref[...], dm_)

    row = lambda i: (i, 0)
    fixed = lambda i: (0, 0)
    return _call(
        body, "bwd_out", (dm.R // tm,),
        [pl.BlockSpec((tm, dm.D), row), pl.BlockSpec((tm, dm.D), row), pl.BlockSpec((1, dm.D), fixed),
         pl.BlockSpec((2 * dm.D, dm.D), fixed),
         pl.BlockSpec((tm, da), row), pl.BlockSpec((tm, db), row), pl.BlockSpec((tm, dc), row)],
        [pl.BlockSpec((tm, da), row), pl.BlockSpec((tm, db), row), pl.BlockSpec((tm, dc), row),
         pl.BlockSpec((2 * dm.D, dm.D), fixed), pl.BlockSpec((8, dm.D), fixed)],
        [jax.ShapeDtypeStruct((dm.R, da), F32), jax.ShapeDtypeStruct((dm.R, db), F32),
         jax.ShapeDtypeStruct((dm.R, dc), F32),
         jax.ShapeDtypeStruct((2 * dm.D, dm.D), F32), jax.ShapeDtypeStruct((8, dm.D), F32)],
        [], ("arbitrary",), (dh, m, post_g, w_out, ya, yb, yc), rider)


def _bwd_in_dx(dpa, dpb, dpc, dpt, w, h, dh, pre_g, dm, rider=None):
    tm = _row_tile(dm.R, 272)
    wa, wb, wc = dm.WA, dm.WB, dm.WC

    def body(dpa_ref, dpb_ref, dpc_ref, dpt_ref, w_ref, h_ref, dh_ref, g_ref, out_ref, dg_ref):
        @pl.when(pl.program_id(0) == 0)
        def _():
            dg_ref[...] = jnp.zeros_like(dg_ref)

        dhn = _dot_nt(dpa_ref[...], w_ref[:, 0:wa])
        dhn = dhn + _dot_nt(dpb_ref[...], w_ref[:, wa:wa + wb])
        dhn = dhn + _dot_nt(dpc_ref[...], w_ref[:, wa + wb:wa + wb + wc])
        dhn = dhn + _dot_nt(dpt_ref[...], w_ref[:, wa + wb + wc:wa + wb + wc + DT_PAD])
        xf = h_ref[...]
        r = lax.rsqrt(jnp.mean(xf * xf, axis=-1, keepdims=True) + NORM_EPS)
        n = xf * r
        dg_ref[0:1, :] = dg_ref[0:1, :] + jnp.sum(dhn * n, axis=0, keepdims=True)
        dn = dhn * g_ref[...]
        out_ref[...] = dh_ref[...] + r * (dn - n * jnp.mean(dn * n, axis=-1, keepdims=True))

    row = lambda i: (i, 0)
    fixed = lambda i: (0, 0)
    return _call(
        body, "bwd_in_dx", (dm.R // tm,),
        [pl.BlockSpec((tm, wa), row), pl.BlockSpec((tm, wb), row), pl.BlockSpec((tm, wc), row),
         pl.BlockSpec((tm, DT_PAD), row), pl.BlockSpec((dm.D, dm.NP), fixed),
         pl.BlockSpec((tm, dm.D), row), pl.BlockSpec((tm, dm.D), row), pl.BlockSpec((1, dm.D), fixed)],
        [pl.BlockSpec((tm, dm.D), row), pl.BlockSpec((8, dm.D), fixed)],
        [jax.ShapeDtypeStruct((dm.R, dm.D), F32), jax.ShapeDtypeStruct((8, dm.D), F32)],
        [], ("arbitrary",), (dpa, dpb, dpc, dpt, w, h, dh, pre_g), rider)


def _bwd_in_dw(hn, dp, dm, piece):
    width = dp.shape[1]
    tm = _row_tile(dm.R, 1088)
    tn = _col_tile(width, 512)

    def body(hn_ref, dp_ref, dw_ref):
        @pl.when(pl.program_id(1) == 0)
        def _():
            dw_ref[...] = jnp.zeros_like(dw_ref)

        dw_ref[...] = dw_ref[...] + _dot_tn(hn_ref[...], dp_ref[...])

    return pl.pallas_call(
        body, name="bwd_in_dw_" + piece, grid=(width // tn, dm.R // tm),
        in_specs=[pl.BlockSpec((tm, dm.D), lambda j, i: (i, 0)), pl.BlockSpec((tm, tn), lambda j, i: (i, j))],
        out_specs=pl.BlockSpec((dm.D, tn), lambda j, i: (0, j)),
        out_shape=jax.ShapeDtypeStruct((dm.D, width), F32),
        compiler_params=_params(("parallel", "arbitrary")),
    )(hn, dp)


def _tile_index(dm, reverse):
    if reverse:
        return lambda b, i: b * dm.NT + (dm.NT - 1 - i)
    return lambda b, i: b * dm.NT + i


def _halo_index(dm, rows):
    per_tile = TT // rows
    return lambda b, i: jnp.maximum((b * dm.NT + (dm.NT - 1 - i)) * per_tile - 1, 0)


def _mix_a_fwd(proj, conv_w, dm):
    da = dm.DA
    ti = _tile_index(dm, False)

    def body(ab_ref, ac_ref, ax_ref, az_ref, w_ref, y_ref, pbuf):
        i = pl.program_id(1)

        @pl.when(i == 0)
        def _():
            pbuf[0:SMALL_HALO, :] = jnp.zeros((SMALL_HALO, da), F32)

        @pl.when(i > 0)
        def _():
            pbuf[0:SMALL_HALO, :] = pbuf[TT:TT + SMALL_HALO, :]

        for lb in range(da // LANES):
            cs = slice(lb * LANES, (lb + 1) * LANES)
            p = ac_ref[:, cs] * ax_ref[:, cs]
            pbuf[SMALL_HALO:SMALL_HALO + TT, cs] = p
            q = (w_ref[0:1, cs] * pbuf[6:6 + TT, cs] + w_ref[1:2, cs] * pbuf[7:7 + TT, cs] + w_ref[2:3, cs] * p)
            az = az_ref[:, cs]
            y_ref[:, cs] = (ab_ref[:, cs] * q * (az * _sigmoid(az))).astype(BF16)

    col = lambda k: pl.BlockSpec((TT, da), lambda b, i: (ti(b, i), k))
    return pl.pallas_call(
        body, name="mix_a_fwd", grid=(dm.BL, dm.NT),
        in_specs=[col(0), col(1), col(2), col(3), pl.BlockSpec((CONV_A_K, da), lambda b, i: (0, 0))],
        out_specs=pl.BlockSpec((TT, da), lambda b, i: (ti(b, i), 0)),
        out_shape=jax.ShapeDtypeStruct((dm.R, da), BF16),
        scratch_shapes=[pltpu.VMEM((SMALL_HALO + TT, da), F32)],
        compiler_params=_params(("parallel", "arbitrary")),
    )(proj, proj, proj, proj, conv_w)


def _mix_a_bwd(proj, dya, conv_w, dm):
    da = dm.DA
    ti = _tile_index(dm, True)
    hi = _halo_index(dm, SMALL_HALO)

    def body(ab_ref, ac_ref, ax_ref, az_ref, ach_ref, axh_ref, dy_ref, w_ref, dp_ref, dw_ref, pbuf, dqbuf):
        i = pl.program_id(1)
        halo_on = jnp.where(i == dm.NT - 1, 0.0, 1.0)

        @pl.when(i == 0)
        def _():
            dw_ref[...] = jnp.zeros_like(dw_ref)
            dqbuf[TT:TT + SMALL_HALO, :] = jnp.zeros((SMALL_HALO, da), F32)

        @pl.when(i > 0)
        def _():
            dqbuf[TT:TT + SMALL_HALO, :] = dqbuf[0:SMALL_HALO, :]

        for lb in range(da // LANES):
            cs = slice(lb * LANES, (lb + 1) * LANES)
            pbuf[0:SMALL_HALO, cs] = ach_ref[:, cs] * axh_ref[:, cs] * halo_on
            ac, ax, ab, az = ac_ref[:, cs], ax_ref[:, cs], ab_ref[:, cs], az_ref[:, cs]
            p = ac * ax
            pbuf[SMALL_HALO:SMALL_HALO + TT, cs] = p
            p1 = pbuf[7:7 + TT, cs]
            p2 = pbuf[6:6 + TT, cs]
            w0, w1, w2 = w_ref[0:1, cs], w_ref[1:2, cs], w_ref[2:3, cs]
            q = w0 * p2 + w1 * p1 + w2 * p
            sz, dsz = _silu_and_grad(az)
            dy = dy_ref[:, cs]
            t1 = dy * ab
            dq = t1 * sz
            dqbuf[0:TT, cs] = dq
            dpv = w2 * dq + w1 * dqbuf[1:1 + TT, cs] + w0 * dqbuf[2:2 + TT, cs]
            dp_ref[:, lb * LANES:(lb + 1) * LANES] = (dy * q * sz).astype(BF16)
            dp_ref[:, da + lb * LANES:da + (lb + 1) * LANES] = (dpv * ax).astype(BF16)
            dp_ref[:, 2 * da + lb * LANES:2 * da + (lb + 1) * LANES] = (dpv * ac).astype(BF16)
            dp_ref[:, 3 * da + lb * LANES:3 * da + (lb + 1) * LANES] = (t1 * q * dsz).astype(BF16)
            dw_ref[0, 0:1, cs] = dw_ref[0, 0:1, cs] + jnp.sum(dq * p2, axis=0, keepdims=True)
            dw_ref[0, 1:2, cs] = dw_ref[0, 1:2, cs] + jnp.sum(dq * p1, axis=0, keepdims=True)
            dw_ref[0, 2:3, cs] = dw_ref[0, 2:3, cs] + jnp.sum(dq * p, axis=0, keepdims=True)

    col = lambda k: pl.BlockSpec((TT, da), lambda b, i: (ti(b, i), k))
    halo = lambda k: pl.BlockSpec((SMALL_HALO, da), lambda b, i: (hi(b, i), k))
    return pl.pallas_call(
        body, name="mix_a_bwd", grid=(dm.BL, dm.NT),
        in_specs=[col(0), col(1), col(2), col(3), halo(1), halo(2),
                  pl.BlockSpec((TT, da), lambda b, i: (ti(b, i), 0)),
                  pl.BlockSpec((CONV_A_K, da), lambda b, i: (0, 0))],
        out_specs=[pl.BlockSpec((TT, dm.WA), lambda b, i: (ti(b, i), 0)),
                   pl.BlockSpec((1, 8, da), lambda b, i: (b, 0, 0))],
        out_shape=[jax.ShapeDtypeStruct((dm.R, dm.WA), BF16), jax.ShapeDtypeStruct((dm.BL, 8, da), F32)],
        scratch_shapes=[pltpu.VMEM((SMALL_HALO + TT, da), F32), pltpu.VMEM((TT + SMALL_HALO, da), F32)],
        compiler_params=_params(("parallel", "arbitrary")),
    )(proj, proj, proj, proj, proj, proj, dya, conv_w)


def _conf_conv(ubuf, w_ref, b_ref, u1buf, dc):
    for lb in range(dc // LANES):
        cs = slice(lb * LANES, (lb + 1) * LANES)
        acc = jnp.broadcast_to(b_ref[0:1, cs], (TT, LANES))
        for k in range(CONF_K):
            off = CONF_HALO - (CONF_K - 1) + k
            acc = acc + w_ref[k:k + 1, cs] * ubuf[off:off + TT, cs]
        u1buf[:, cs] = acc


def _mix_c_fwd(proj, conv_w, conv_b, ln_g, ln_b, dm):
    dc = dm.DC
    c0 = (dm.WA + dm.WB) // dc
    ti = _tile_index(dm, False)

    def body(ca_ref, cg_ref, cz_ref, w_ref, b_ref, g_ref, be_ref, y_ref, ubuf, u1buf):
        i = pl.program_id(1)

        @pl.when(i == 0)
        def _():
            ubuf[0:CONF_HALO, :] = jnp.zeros((CONF_HALO, dc), F32)

        @pl.when(i > 0)
        def _():
            ubuf[0:CONF_HALO, :] = ubuf[TT:TT + CONF_HALO, :]

        ubuf[CONF_HALO:CONF_HALO + TT, :] = ca_ref[...] * _sigmoid(cg_ref[...])
        _conf_conv(ubuf, w_ref, b_ref, u1buf, dc)
        u1 = u1buf[...]
        mu = jnp.mean(u1, axis=-1, keepdims=True)
        xc = u1 - mu
        rstd = lax.rsqrt(jnp.mean(xc * xc, axis=-1, keepdims=True) + LN_EPS)
        u2 = xc * rstd * g_ref[...] + be_ref[...]
        cz = cz_ref[...]
        y_ref[...] = ((u2 * _sigmoid(u2)) * (cz * _sigmoid(cz))).astype(BF16)

    col = lambda k: pl.BlockSpec((TT, dc), lambda b, i: (ti(b, i), c0 + k))
    vec = pl.BlockSpec((1, dc), lambda b, i: (0, 0))
    return pl.pallas_call(
        body, name="mix_c_fwd", grid=(dm.BL, dm.NT),
        in_specs=[col(0), col(1), col(2), pl.BlockSpec((CONF_K, dc), lambda b, i: (0, 0)), vec, vec, vec],
        out_specs=pl.BlockSpec((TT, dc), lambda b, i: (ti(b, i), 0)),
        out_shape=jax.ShapeDtypeStruct((dm.R, dc), BF16),
        scratch_shapes=[pltpu.VMEM((CONF_HALO + TT, dc), F32), pltpu.VMEM((TT, dc), F32)],
        compiler_params=_params(("parallel", "arbitrary")),
    )(proj, proj, proj, conv_w, conv_b, ln_g, ln_b)


def _mix_c_bwd(proj, dyc, conv_w, conv_b, ln_g, ln_b, dm):
    dc = dm.DC
    c0 = (dm.WA + dm.WB) // dc
    ti = _tile_index(dm, True)
    hi = _halo_index(dm, CONF_HALO)

    def body(ca_ref, cg_ref, cz_ref, cah_ref, cgh_ref, dy_ref, w_ref, b_ref, g_ref, be_ref,
             dp_ref, dw_ref, dv_ref, ubuf, u1buf, dubuf, du0buf):
        i = pl.program_id(1)
        halo_on = jnp.where(i == dm.NT - 1, 0.0, 1.0)

        @pl.when(i == 0)
        def _():
            dw_ref[...] = jnp.zeros_like(dw_ref)
            dv_ref[...] = jnp.zeros_like(dv_ref)
            dubuf[TT:TT + CONF_HALO, :] = jnp.zeros((CONF_HALO, dc), F32)

        @pl.when(i > 0)
        def _():
            dubuf[TT:TT + CONF_HALO, :] = dubuf[0:CONF_HALO, :]

        ubuf[0:CONF_HALO, :] = cah_ref[...] * _sigmoid(cgh_ref[...]) * halo_on
        sgg = _sigmoid(cg_ref[...])
        ubuf[CONF_HALO:CONF_HALO + TT, :] = ca_ref[...] * sgg
        _conf_conv(ubuf, w_ref, b_ref, u1buf, dc)
        u1 = u1buf[...]
        mu = jnp.mean(u1, axis=-1, keepdims=True)
        xc = u1 - mu
        rstd = lax.rsqrt(jnp.mean(xc * xc, axis=-1, keepdims=True) + LN_EPS)
        xhat = xc * rstd
        u2 = xhat * g_ref[...] + be_ref[...]
        su, dsu = _silu_and_grad(u2)
        sz, dsz = _silu_and_grad(cz_ref[...])
        dy = dy_ref[...]
        du2 = dy * dsu * sz
        dp_ref[:, 2 * dc:3 * dc] = (dy * su * dsz).astype(BF16)
        dxhat = du2 * g_ref[...]
        du1 = rstd * (dxhat - jnp.mean(dxhat, axis=-1, keepdims=True)
                      - xhat * jnp.mean(dxhat * xhat, axis=-1, keepdims=True))
        dv_ref[0, 0:1, :] = dv_ref[0, 0:1, :] + jnp.sum(du1, axis=0, keepdims=True)
        dv_ref[0, 1:2, :] = dv_ref[0, 1:2, :] + jnp.sum(du2 * xhat, axis=0, keepdims=True)
        dv_ref[0, 2:3, :] = dv_ref[0, 2:3, :] + jnp.sum(du2, axis=0, keepdims=True)
        dubuf[0:TT, :] = du1
        for lb in range(dc // LANES):
            cs = slice(lb * LANES, (lb + 1) * LANES)
            d1 = dubuf[0:TT, cs]
            acc = jnp.zeros((TT, LANES), F32)
            for k in range(CONF_K):
                acc = acc + w_ref[k:k + 1, cs] * dubuf[CONF_K - 1 - k:CONF_K - 1 - k + TT, cs]
                off = CONF_HALO - (CONF_K - 1) + k
                dw_ref[0, k:k + 1, cs] = dw_ref[0, k:k + 1, cs] + jnp.sum(
                    d1 * ubuf[off:off + TT, cs], axis=0, keepdims=True)
            du0buf[:, cs] = acc
        du0 = du0buf[...]
        dp_ref[:, 0:dc] = (du0 * sgg).astype(BF16)
        dp_ref[:, dc:2 * dc] = (du0 * ca_ref[...] * sgg * (1.0 - sgg)).astype(BF16)

    col = lambda k: pl.BlockSpec((TT, dc), lambda b, i: (ti(b, i), c0 + k))
    halo = lambda k: pl.BlockSpec((CONF_HALO, dc), lambda b, i: (hi(b, i), c0 + k))
    vec = pl.BlockSpec((1, dc), lambda b, i: (0, 0))
    return pl.pallas_call(
        body, name="mix_c_bwd", grid=(dm.BL, dm.NT),
        in_specs=[col(0), col(1), col(2), halo(0), halo(1),
                  pl.BlockSpec((TT, dc), lambda b, i: (ti(b, i), 0)),
                  pl.BlockSpec((CONF_K, dc), lambda b, i: (0, 0)), vec, vec, vec],
        out_specs=[pl.BlockSpec((TT, dm.WC), lambda b, i: (ti(b, i), 0)),
                   pl.BlockSpec((1, 32, dc), lambda b, i: (b, 0, 0)),
                   pl.BlockSpec((1, 8, dc), lambda b, i: (b, 0, 0))],
        out_shape=[jax.ShapeDtypeStruct((dm.R, dm.WC), BF16),
                   jax.ShapeDtypeStruct((dm.BL, 32, dc), F32),
                   jax.ShapeDtypeStruct((dm.BL, 8, dc), F32)],
        scratch_shapes=[pltpu.VMEM((CONF_HALO + TT, dc), F32), pltpu.VMEM((TT, dc), F32),
                        pltpu.VMEM((TT + CONF_HALO, dc), F32), pltpu.VMEM((TT, dc), F32)],
        compiler_params=_params(("parallel", "arbitrary")),
    )(proj, proj, proj, proj, proj, dyc, conv_w, conv_b, ln_g, ln_b)


def _ssm_conv(rbuf, w_ref, b_ref, width):
    for lb in range(width // LANES):
        cs = slice(lb * LANES, (lb + 1) * LANES)
        acc = jnp.broadcast_to(b_ref[0:1, cs], (TT, LANES))
        for k in range(SSM_CONV_K):
            off = SMALL_HALO - (SSM_CONV_K - 1) + k
            acc = acc + w_ref[k:k + 1, cs] * rbuf[off:off + TT, cs]
        yield cs, acc


def _softplus(z):
    return jnp.maximum(z, 0.0) + jnp.log(1.0 + jnp.exp(-jnp.abs(z)))


def _tri(lower):
    r = lax.broadcasted_iota(jnp.int32, (TT, TT), 0)
    c = lax.broadcasted_iota(jnp.int32, (TT, TT), 1)
    return (c <= r) if lower else (c >= r)


def _exact_01_dot(mat01, x):
    x1, x2, x3 = _split3(x)
    return _dot(mat01, x1) + _dot(mat01, x2) + _dot(mat01, x3)


def _head_scalars(dt_ref, dtb_ref, alog_ref):
    z = dt_ref[...] + dtb_ref[...]
    dtv = _softplus(z)
    a = -jnp.exp(alog_ref[...])
    ac = _exact_01_dot(_tri(True).astype(F32).astype(BF16), dtv * a)
    eac = jnp.exp(ac)
    dst = jnp.exp(ac[TT - 1:TT, :] - ac)
    return z, dtv, a, ac, eac, dst


def _decay(ac, ac_t, h, causal):
    seg = ac[:, h:h + 1] - ac_t[h:h + 1, :]
    return jnp.where(causal, jnp.exp(jnp.where(causal, seg, 0.0)), 0.0)


def _pair_mask(h):
    lane = lax.broadcasted_iota(jnp.int32, (1, LANES), 1)
    return ((lane >= SSM_HEAD_DIM) if (h % 2) else (lane < SSM_HEAD_DIM)).astype(F32)


def _mix_b_fwd(proj, conv_w, conv_b, dt_bias, a_log, dskx, norm_g, expand, dm):
    db, gn, xbc_w, hpg = dm.DB, dm.GN, dm.XBC, dm.HPG
    gw = db // SSM_GROUPS
    ti = _tile_index(dm, False)

    def body(bz_ref, bx_ref, bc_ref, dt_ref, w_ref, b_ref, dtb_ref, alog_ref, dsk_ref, g_ref, e_ref,
             y_ref, yraw_ref, sprev_ref, rbuf, xbuf, state, ybuf, exbuf, xdtbuf):
        i = pl.program_id(1)

        @pl.when(i == 0)
        def _():
            rbuf[0:SMALL_HALO, :] = jnp.zeros((SMALL_HALO, xbc_w), F32)
            state[...] = jnp.zeros_like(state)

        @pl.when(i > 0)
        def _():
            rbuf[0:SMALL_HALO, :] = rbuf[TT:TT + SMALL_HALO, :]

        rbuf[SMALL_HALO:SMALL_HALO + TT, 0:db] = bx_ref[...]
        rbuf[SMALL_HALO:SMALL_HALO + TT, db:xbc_w] = bc_ref[...]
        for cs, pre in _ssm_conv(rbuf, w_ref, b_ref, xbc_w):
            xbuf[:, cs] = pre * _sigmoid(pre)

        _, dtv, _, ac, eac, dst = _head_scalars(dt_ref, dtb_ref, alog_ref)
        exbuf[...] = _dot(jnp.concatenate([dtv, eac, dst], axis=0).astype(BF16), e_ref[...])
        ac_t = ac.T
        causal = _tri(True)
        sprev_ref[0, 0] = state[...]

        xdtbuf[...] = xbuf[:, 0:db] * exbuf[0:TT, :]
        ybuf[...] = xbuf[:, 0:db] * dsk_ref[...]
        for g in range(SSM_GROUPS):
            gs = slice(g * gw, (g + 1) * gw)
            bg = xbuf[:, db + g * SSM_STATE:db + (g + 1) * SSM_STATE].astype(BF16)
            cg = xbuf[:, db + gn + g * SSM_STATE:db + gn + (g + 1) * SSM_STATE].astype(BF16)
            cb = _dot_nt(cg, bg)
            for e in range(0, hpg, 2):
                h = g * hpg + e
                ps = slice(h * SSM_HEAD_DIM, (h + 2) * SSM_HEAD_DIM)
                xp = xdtbuf[:, ps]
                acc = jnp.zeros((TT, LANES), F32)
                for hh in (h, h + 1):
                    mm = (cb * _decay(ac, ac_t, hh, causal)).astype(BF16)
                    acc = acc + _dot(mm, (xp * _pair_mask(hh)).astype(BF16))
                ybuf[:, ps] = ybuf[:, ps] + acc
            sg = state[:, gs]
            ybuf[:, gs] = ybuf[:, gs] + exbuf[TT:2 * TT, gs] * _dot(cg, sg.astype(BF16))
            state[:, gs] = sg * exbuf[2 * TT - 1:2 * TT, gs] + _dot_tn(
                bg, (xdtbuf[:, gs] * exbuf[2 * TT:3 * TT, gs]).astype(BF16))

        yraw = ybuf[...]
        yraw_ref[...] = yraw
        bz = bz_ref[...]
        v = yraw * (bz * _sigmoid(bz))
        r = lax.rsqrt(jnp.mean(v * v, axis=-1, keepdims=True) + NORM_EPS)
        y_ref[...] = (v * r * g_ref[...]).astype(BF16)

    tile = lambda w, k: pl.BlockSpec((TT, w), lambda b, i: (ti(b, i), k))
    fixed = lambda r, w: pl.BlockSpec((r, w), lambda b, i: (0, 0))
    return pl.pallas_call(
        body, name="mix_b_fwd", grid=(dm.BL, dm.NT),
        in_specs=[tile(db, dm.WA // db), tile(db, dm.WA // db + 1), tile(2 * gn, (dm.WA + 2 * db) // (2 * gn)),
                  tile(DT_PAD, dm.NP // DT_PAD - 1),
                  fixed(SSM_CONV_K, xbc_w), fixed(1, xbc_w), fixed(1, DT_PAD), fixed(1, DT_PAD),
                  fixed(1, db), fixed(1, db), fixed(DT_PAD, db)],
        out_specs=[pl.BlockSpec((TT, db), lambda b, i: (ti(b, i), 0)),
                   pl.BlockSpec((TT, db), lambda b, i: (ti(b, i), 0)),
                   pl.BlockSpec((1, 1, SSM_STATE, db), lambda b, i: (b, i, 0, 0))],
        out_shape=[jax.ShapeDtypeStruct((dm.R, db), BF16), jax.ShapeDtypeStruct((dm.R, db), F32),
                   jax.ShapeDtypeStruct((dm.BL, dm.NT, SSM_STATE, db), F32)],
        scratch_shapes=[pltpu.VMEM((SMALL_HALO + TT, xbc_w), F32), pltpu.VMEM((TT, xbc_w), F32),
                        pltpu.VMEM((SSM_STATE, db), F32), pltpu.VMEM((TT, db), F32),
                        pltpu.VMEM((3 * TT, db), F32), pltpu.VMEM((TT, db), F32)],
        compiler_params=_params(("parallel", "arbitrary")),
    )(proj, proj, proj, proj, conv_w, conv_b, dt_bias, a_log, dskx, norm_g, expand)


def _mix_b_bwd(proj, dyb, yraw, sprev, conv_w, conv_b, dt_bias, a_log, dskx, norm_g, expand, expand_t, dm,
               rider=None):
    db, gn, xbc_w, hpg = dm.DB, dm.GN, dm.XBC, dm.HPG
    gw = db // SSM_GROUPS
    ti = _tile_index(dm, True)
    hi = _halo_index(dm, SMALL_HALO)

    def body(bz_ref, bx_ref, bc_ref, dt_ref, bxh_ref, bch_ref, dy_ref, yraw_ref, sprev_ref,
             w_ref, b_ref, dtb_ref, alog_ref, dsk_ref, g_ref, e_ref, et_ref,
             dp_ref, dpt_ref, dwc_ref, dch_ref, dhd_ref,
             rbuf, xbuf, dsbuf, dstate, dxbuf, z1buf, dprebuf, exbuf, xdtbuf, dyrbuf, uvec):
        i = pl.program_id(1)
        halo_on = jnp.where(i == dm.NT - 1, 0.0, 1.0)

        @pl.when(i == 0)
        def _():
            dwc_ref[...] = jnp.zeros_like(dwc_ref)
            dch_ref[...] = jnp.zeros_like(dch_ref)
            dhd_ref[...] = jnp.zeros_like(dhd_ref)
            dstate[...] = jnp.zeros_like(dstate)
            dprebuf[TT:TT + SMALL_HALO, :] = jnp.zeros((SMALL_HALO, xbc_w), F32)

        @pl.when(i > 0)
        def _():
            dprebuf[TT:TT + SMALL_HALO, :] = dprebuf[0:SMALL_HALO, :]

        rbuf[0:SMALL_HALO, 0:db] = bxh_ref[...] * halo_on
        rbuf[0:SMALL_HALO, db:xbc_w] = bch_ref[...] * halo_on
        rbuf[SMALL_HALO:SMALL_HALO + TT, 0:db] = bx_ref[...]
        rbuf[SMALL_HALO:SMALL_HALO + TT, db:xbc_w] = bc_ref[...]
        for cs, pre in _ssm_conv(rbuf, w_ref, b_ref, xbc_w):
            sl, dsl = _silu_and_grad(pre)
            xbuf[:, cs] = sl
            dsbuf[:, cs] = dsl

        z, dtv, a, ac, eac, dst = _head_scalars(dt_ref, dtb_ref, alog_ref)
        exbuf[...] = _dot(jnp.concatenate([dtv, eac, dst], axis=0).astype(BF16), e_ref[...])
        ac_t = ac.T
        causal = _tri(True)
        xdtbuf[...] = xbuf[:, 0:db] * exbuf[0:TT, :]

        yraw = yraw_ref[...]
        sz, dsz = _silu_and_grad(bz_ref[...])
        v = yraw * sz
        r = lax.rsqrt(jnp.mean(v * v, axis=-1, keepdims=True) + NORM_EPS)
        dy = dy_ref[...]
        dyg = dy * g_ref[...]
        dv = r * dyg - v * (r * r * r * jnp.mean(dyg * v, axis=-1, keepdims=True))
        dch_ref[0, 0:1, :] = dch_ref[0, 0:1, :] + jnp.sum(dy * v * r, axis=0, keepdims=True)
        dyr = dv * sz
        dyrbuf[...] = dyr
        dp_ref[:, 0:db] = (dv * yraw * dsz).astype(BF16)
        dch_ref[0, 1:2, :] = dch_ref[0, 1:2, :] + jnp.sum(dyr * xbuf[:, 0:db], axis=0, keepdims=True)

        lane_row = lax.broadcasted_iota(jnp.int32, (1, LANES), 1)
        sub_col = lax.broadcasted_iota(jnp.int32, (LANES, 1), 0)
        dac = jnp.zeros((TT, LANES), F32)
        colacc = jnp.zeros((LANES, TT), F32)
        for g in range(SSM_GROUPS):
            gs = slice(g * gw, (g + 1) * gw)
            bs_ = slice(db + g * SSM_STATE, db + (g + 1) * SSM_STATE)
            cs_ = slice(db + gn + g * SSM_STATE, db + gn + (g + 1) * SSM_STATE)
            bg = xbuf[:, bs_].astype(BF16)
            cg = xbuf[:, cs_].astype(BF16)
            cb = _dot_nt(cg, bg)
            dcb = jnp.zeros((TT, TT), F32)
            for e in range(0, hpg, 2):
                h = g * hpg + e
                ps = slice(h * SSM_HEAD_DIM, (h + 2) * SSM_HEAD_DIM)
                xp16 = xdtbuf[:, ps].astype(BF16)
                dyp = dyrbuf[:, ps]
                acc = jnp.zeros((TT, LANES), F32)
                for hh in (h, h + 1):
                    dec = _decay(ac, ac_t, hh, causal)
                    mm = cb * dec
                    dyh = (dyp * _pair_mask(hh)).astype(BF16)
                    dmm = _dot_nt(dyh, xp16)
                    acc = acc + _dot_tn(mm.astype(BF16), dyh)
                    dcb = dcb + dmm * dec
                    gm = dmm * mm
                    dac = dac + jnp.sum(gm, axis=1, keepdims=True) * (lane_row == hh).astype(F32)
                    colacc = colacc + (sub_col == hh).astype(F32) * jnp.sum(gm, axis=0, keepdims=True)
                dxbuf[:, ps] = acc
            sg32 = sprev_ref[0, 0, :, gs]
            sg = sg32.astype(BF16)
            dsn = dstate[:, gs]
            dsn16 = dsn.astype(BF16)
            dcb16 = dcb.astype(BF16)
            eacx = exbuf[TT:2 * TT, gs]
            dstx = exbuf[2 * TT:3 * TT, gs]
            cdx = exbuf[2 * TT - 1:2 * TT, gs]
            dye16 = (dyrbuf[:, gs] * eacx).astype(BF16)
            xdt_g = xdtbuf[:, gs]
            dxbuf[:, cs_] = _dot(dcb16, bg) + _dot_nt(dye16, sg)
            dst_x = dstx * _dot(bg, dsn16)
            dxbuf[:, bs_] = _dot_tn(dcb16, cg) + _dot_nt((dstx * xdt_g).astype(BF16), dsn16)
            dstate[:, gs] = cdx * dsn + _dot_tn(cg, dye16)
            z1buf[:, gs] = dyrbuf[:, gs] * (eacx * _dot(cg, sg)) - xdt_g * dst_x
            uvec[:, gs] = jnp.broadcast_to(
                jnp.sum(xdt_g * dst_x, axis=0, keepdims=True) + jnp.sum(dsn * cdx * sg32, axis=0, keepdims=True),
                (8, gw))
            dxbuf[:, gs] = dxbuf[:, gs] + dst_x

        zz = _dot(jnp.concatenate([z1buf[...], dxbuf[:, 0:db] * xbuf[:, 0:db]], axis=0).astype(BF16), et_ref[...])
        u1, u2, u3 = _split3(uvec[...])
        ulast = (_dot(u1, et_ref[...]) + _dot(u2, et_ref[...]) + _dot(u3, et_ref[...]))[0:1, :]
        is_last = (lax.broadcasted_iota(jnp.int32, (TT, 1), 0) == TT - 1).astype(F32)
        dac = dac - colacc.T + zz[0:TT] + is_last * ulast
        dda = _exact_01_dot(_tri(False).astype(F32).astype(BF16), dac)
        ddt = dda * a + zz[TT:2 * TT]
        dhd_ref[0, 1:2, :] = dhd_ref[0, 1:2, :] + jnp.sum(dda * dtv, axis=0, keepdims=True) * a
        ddtraw = ddt * _sigmoid(z)
        dhd_ref[0, 0:1, :] = dhd_ref[0, 0:1, :] + jnp.sum(ddtraw, axis=0, keepdims=True)
        dpt_ref[...] = ddtraw.astype(BF16)
        dxbuf[:, 0:db] = dyrbuf[...] * dsk_ref[...] + dxbuf[:, 0:db] * exbuf[0:TT, :]

        for lb in range(xbc_w // LANES):
            cs = slice(lb * LANES, (lb + 1) * LANES)
            dpre = dxbuf[:, cs] * dsbuf[:, cs]
            dprebuf[0:TT, cs] = dpre
            dwc_ref[0, SSM_CONV_K:SSM_CONV_K + 1, cs] = dwc_ref[0, SSM_CONV_K:SSM_CONV_K + 1, cs] + jnp.sum(
                dpre, axis=0, keepdims=True)
            draw = w_ref[SSM_CONV_K - 1:SSM_CONV_K, cs] * dpre
            for k in range(SSM_CONV_K - 1):
                sh = SSM_CONV_K - 1 - k
                draw = draw + w_ref[k:k + 1, cs] * dprebuf[sh:sh + TT, cs]
            for k in range(SSM_CONV_K):
                off = SMALL_HALO - (SSM_CONV_K - 1) + k
                dwc_ref[0, k:k + 1, cs] = dwc_ref[0, k:k + 1, cs] + jnp.sum(
                    dpre * rbuf[off:off + TT, cs], axis=0, keepdims=True)
            dp_ref[:, db + lb * LANES:db + (lb + 1) * LANES] = draw.astype(BF16)

    tile = lambda w, k: pl.BlockSpec((TT, w), lambda b, i: (ti(b, i), k))
    halo = lambda w, k: pl.BlockSpec((SMALL_HALO, w), lambda b, i: (hi(b, i), k))
    fixed = lambda r, w: pl.BlockSpec((r, w), lambda b, i: (0, 0))
    kz = dm.WA // db
    kc = (dm.WA + 2 * db) // (2 * gn)
    return _call(
        body, "mix_b_bwd", (dm.BL, dm.NT),
        [tile(db, kz), tile(db, kz + 1), tile(2 * gn, kc), tile(DT_PAD, dm.NP // DT_PAD - 1),
         halo(db, kz + 1), halo(2 * gn, kc),
         pl.BlockSpec((TT, db), lambda b, i: (ti(b, i), 0)),
         pl.BlockSpec((TT, db), lambda b, i: (ti(b, i), 0)),
         pl.BlockSpec((1, 1, SSM_STATE, db), lambda b, i: (b, dm.NT - 1 - i, 0, 0)),
         fixed(SSM_CONV_K, xbc_w), fixed(1, xbc_w), fixed(1, DT_PAD), fixed(1, DT_PAD),
         fixed(1, db), fixed(1, db), fixed(DT_PAD, db), fixed(db, DT_PAD)],
        [pl.BlockSpec((TT, dm.WB), lambda b, i: (ti(b, i), 0)),
         pl.BlockSpec((TT, DT_PAD), lambda b, i: (ti(b, i), 0)),
         pl.BlockSpec((1, 8, xbc_w), lambda b, i: (b, 0, 0)),
         pl.BlockSpec((1, 8, db), lambda b, i: (b, 0, 0)),
         pl.BlockSpec((1, 8, DT_PAD), lambda b, i: (b, 0, 0))],
        [jax.ShapeDtypeStruct((dm.R, dm.WB), BF16), jax.ShapeDtypeStruct((dm.R, DT_PAD), BF16),
         jax.ShapeDtypeStruct((dm.BL, 8, xbc_w), F32), jax.ShapeDtypeStruct((dm.BL, 8, db), F32),
         jax.ShapeDtypeStruct((dm.BL, 8, DT_PAD), F32)],
        [pltpu.VMEM((SMALL_HALO + TT, xbc_w), F32), pltpu.VMEM((TT, xbc_w), F32),
         pltpu.VMEM((TT, xbc_w), F32), pltpu.VMEM((SSM_STATE, db), F32),
         pltpu.VMEM((TT, xbc_w), F32), pltpu.VMEM((TT, db), F32),
         pltpu.VMEM((TT + SMALL_HALO, xbc_w), F32), pltpu.VMEM((3 * TT, db), F32),
         pltpu.VMEM((TT, db), F32), pltpu.VMEM((TT, db), F32), pltpu.VMEM((8, db), F32)],
        ("parallel", "arbitrary"),
        (proj, proj, proj, proj, proj, proj, dyb, yraw, sprev,
         conv_w, conv_b, dt_bias, a_log, dskx, norm_g, expand, expand_t), rider)


def _head_consts(dm):
    head_of = jnp.arange(dm.DB) // SSM_HEAD_DIM
    expand = (jnp.arange(DT_PAD)[:, None] == head_of[None, :]).astype(BF16)
    return expand, expand.T


def _ssm_params(lw, dm):
    pad_h = lambda v: jnp.pad(v, (0, DT_PAD - dm.H))[None]
    return (lw["ssm_conv_w"], lw["ssm_conv_b"][None], pad_h(lw["dt_bias"]), pad_h(lw["a_log"]),
            jnp.repeat(lw["d_skip"], SSM_HEAD_DIM)[None], lw["ssm_norm_g"][None])


def _layer_fwd(h, lw, w_in, w_out, cst, dm, ride_in=None, ride_out=None):
    (proj, hn), got_in = _fwd_in(h, lw["pre_g"][None], w_in, dm, ride_in)
    ya = _mix_a_fwd(proj, lw["conv_a_w"], dm)
    yb, yraw, sprev = _mix_b_fwd(proj, *_ssm_params(lw, dm), cst[0], dm)
    yc = _mix_c_fwd(proj, lw["conf_conv_w"], lw["conf_conv_b"][None], lw["conf_ln_g"][None],
                    lw["conf_ln_b"][None], dm)
    (h_new, m), got_out = _fwd_out(ya, yb, yc, w_out, h, lw["post_g"][None], dm,
                                   None if ride_out is None else ride_out(got_in))
    return h_new, (h, hn, proj, ya, yb, yc, yraw, sprev, m), got_out


def _layer_bwd(dh, saved, lw, w_in, w_out, cst, dm, reduce=None):
    h_in, hn, proj, ya, yb, yc, yraw, sprev, m = saved
    (dya, dyb, dyc, dwo, dpost), got = _bwd_out(dh, m, lw["post_g"][None], w_out, ya, yb, yc, dm,
                                                None if reduce is None else reduce.swap())
    dpa, dwa = _mix_a_bwd(proj, dya, lw["conv_a_w"], dm)
    (dpb, dpt, dwcv, dch, dhd), got = _mix_b_bwd(proj, dyb, yraw, sprev, *_ssm_params(lw, dm), cst[0], cst[1], dm,
                                                 None if reduce is None else reduce.to_owners(got))
    dpc, dwcf, dvc = _mix_c_bwd(proj, dyc, lw["conf_conv_w"], lw["conf_conv_b"][None], lw["conf_ln_g"][None],
                                lw["conf_ln_b"][None], dm)
    (dh, dpre), got = _bwd_in_dx(dpa, dpb, dpc, dpt, w_in, h_in, dh, lw["pre_g"][None], dm,
                                 None if reduce is None else reduce.join(got))
    if reduce is not None:
        reduce.finish(got)
    pieces = [_bwd_in_dw(hn, dp, dm, n) for dp, n in ((dpa, "a"), (dpb, "b"), (dpc, "c"), (dpt, "dt"))]
    dwcv, dch, dhd, dvc = (jnp.sum(a, axis=0) for a in (dwcv, dch, dhd, dvc))
    small = dict(pre_g=dpre[0], post_g=dpost[0], conv_a_w=jnp.sum(dwa, axis=0)[:CONV_A_K],
                 ssm_conv_w=dwcv[:SSM_CONV_K], ssm_conv_b=dwcv[SSM_CONV_K], ssm_norm_g=dch[0],
                 d_skip=jnp.sum(dch[1].reshape(dm.H, SSM_HEAD_DIM), axis=1), dt_bias=dhd[0, :dm.H],
                 a_log=dhd[1, :dm.H], conf_conv_w=jnp.sum(dwcf, axis=0)[:CONF_K], conf_conv_b=dvc[0],
                 conf_ln_g=dvc[1], conf_ln_b=dvc[2])
    return dh, pieces, dwo, small


def _shard_runs(dm):
    ab = dm.WA + dm.WB
    order = [(0, 0, ab), (ab, dm.NP - DT_PAD, dm.H), (ab + dm.H, ab, dm.WC)]
    k = dm.NIN // N_CHIPS
    runs = []
    for s in range(N_CHIPS):
        for o0, m0, wd in order:
            lo, hi = max(o0, s * k), min(o0 + wd, (s + 1) * k)
            if lo < hi:
                runs.append((s, lo - s * k, m0 + lo - o0, hi - lo))
    return runs


def _w_in_from_shards(base, dm):
    tr = _row_tile(dm.D, 256)
    k = dm.NIN // N_CHIPS
    runs = _shard_runs(dm)

    def body(in_ref, out_ref):
        for s, sc, mc, wd in runs:
            out_ref[:, mc:mc + wd] = in_ref[s, :, sc:sc + wd]
        out_ref[:, dm.NP - DT_PAD + dm.H:dm.NP] = jnp.zeros((tr, DT_PAD - dm.H), BF16)

    return pl.pallas_call(
        body, name="w_in_from_shards", grid=(dm.D // tr,),
        in_specs=[pl.BlockSpec((N_CHIPS, tr, k), lambda r: (0, r, 0))],
        out_specs=pl.BlockSpec((tr, dm.NP), lambda r: (r, 0)),
        out_shape=jax.ShapeDtypeStruct((dm.D, dm.NP), BF16),
        compiler_params=_params(("parallel",)),
    )(base)


def _grad_to_shards(pieces, dm):
    tr = _row_tile(dm.D, 256)
    k = dm.NIN // N_CHIPS
    starts = [0, dm.WA, dm.WA + dm.WB, dm.NP - DT_PAD]
    widths = [dm.WA, dm.WB, dm.WC, DT_PAD]
    runs = _shard_runs(dm)

    def body(a_ref, b_ref, c_ref, t_ref, out_ref):
        refs = (a_ref, b_ref, c_ref, t_ref)
        for s, sc, mc, wd in runs:
            for p in range(4):
                lo, hi = max(mc, starts[p]), min(mc + wd, starts[p] + widths[p])
                if lo < hi:
                    out_ref[s, :, sc + lo - mc:sc + hi - mc] = refs[p][:, lo - starts[p]:hi - starts[p]].astype(BF16)

    return pl.pallas_call(
        body, name="grad_to_shards", grid=(dm.D // tr,),
        in_specs=[pl.BlockSpec((tr, w), lambda r: (r, 0)) for w in widths],
        out_specs=pl.BlockSpec((N_CHIPS, tr, k), lambda r: (0, r, 0)),
        out_shape=jax.ShapeDtypeStruct((N_CHIPS, dm.D, k), BF16),
        compiler_params=_params(("parallel",)),
    )(*pieces)


def _place_own(w, me):
    rows, cols = w.shape
    tr = _row_tile(rows, 256)

    def body(me_ref, w_ref, out_ref):
        out_ref[0] = w_ref[...].astype(BF16)

    return pl.pallas_call(
        body, name="place_own",
        grid_spec=pltpu.PrefetchScalarGridSpec(
            num_scalar_prefetch=1, grid=(rows // tr,),
            in_specs=[pl.BlockSpec((tr, cols), lambda r, me_ref: (r, 0))],
            out_specs=pl.BlockSpec((1, tr, cols), lambda r, me_ref: (me_ref[0], r, 0))),
        out_shape=jax.ShapeDtypeStruct((N_CHIPS, rows, cols), BF16),
        compiler_params=_params(("parallel",)),
    )(me, w)


def _add_halves(g, got, c, name):
    _, _, rows, cols = g.shape
    tr = _row_tile(rows, 256)

    def body(c_ref, g_ref, got_ref, out_ref):
        out_ref[0] = (g_ref[0, 0].astype(F32) + got_ref[0].astype(F32)).astype(BF16)

    return pl.pallas_call(
        body, name=name,
        grid_spec=pltpu.PrefetchScalarGridSpec(
            num_scalar_prefetch=1, grid=(N_CHIPS, rows // tr),
            in_specs=[pl.BlockSpec((1, 1, tr, cols), lambda s, r, c_ref: (s, c_ref[0], r, 0)),
                      pl.BlockSpec((1, tr, cols), lambda s, r, c_ref: (s, r, 0))],
            out_specs=pl.BlockSpec((1, tr, cols), lambda s, r, c_ref: (s, r, 0))),
        out_shape=jax.ShapeDtypeStruct((N_CHIPS, rows, cols), BF16),
        compiler_params=_params(("parallel", "parallel")),
    )(c, g, got)


def _add_owner(p, got, where, name):
    _, rows, cols = p.shape
    tr = _row_tile(rows, 256)

    def body(w_ref, p_ref, got_ref, out_ref):
        acc = p_ref[0].astype(F32)
        for j in range(3):
            acc = acc + got_ref[j].astype(F32)
        out_ref[0] = acc

    return pl.pallas_call(
        body, name=name,
        grid_spec=pltpu.PrefetchScalarGridSpec(
            num_scalar_prefetch=1, grid=(rows // tr,),
            in_specs=[pl.BlockSpec((1, tr, cols), lambda r, w_ref: (w_ref[0], r, 0)),
                      pl.BlockSpec((3, tr, cols), lambda r, w_ref: (0, r, 0))],
            out_specs=pl.BlockSpec((1, tr, cols), lambda r, w_ref: (w_ref[1], r, 0))),
        out_shape=jax.ShapeDtypeStruct((2, rows, cols), F32),
        compiler_params=_params(("parallel",)),
    )(where, p, got)


class _GradReduce:
    def __init__(self, gs):
        self.gs = [g.reshape((N_CHIPS, 2, g.shape[1] // 2) + g.shape[2:]) for g in gs]
        self.c = lax.axis_index("c").astype(jnp.int32).reshape(1)
        chip = (2 * lax.axis_index("x") + lax.axis_index("y")).astype(jnp.int32)
        self.where = jnp.stack([chip, self.c[0]])
        self.result = None

    def swap(self):
        return _ride_swap_halves(self.gs)

    def to_owners(self, got):
        self.ps = [_add_halves(g, r, self.c, "grad_add_sibling_" + n) for g, r, n in zip(self.gs, got, ("in", "out"))]
        return _ride_to_owners(self.ps)

    def join(self, got):
        qs = [_add_owner(p, r, self.where, "grad_add_chips_" + n) for p, r, n in zip(self.ps, got, ("in", "out"))]
        return _ride_join_halves(qs)

    def finish(self, got):
        self.result = [a.reshape((a.shape[0] * a.shape[1],) + a.shape[2:]) for a in got]

    def run_alone(self):
        got = _exchange("grad_swap_halves", self.swap())
        got = _exchange("grad_to_owners", self.to_owners(got))
        self.finish(_exchange("grad_join_halves", self.join(got)))


def _adamw_math(w, g, m, v):
    m = ADAM_B1 * m + (1.0 - ADAM_B1) * g
    v = ADAM_B2 * v + (1.0 - ADAM_B2) * (g * g)
    m_hat = m / (1.0 - ADAM_B1 ** ADAM_STEP)
    v_hat = v / (1.0 - ADAM_B2 ** ADAM_STEP)
    delta = -ADAM_LR * (m_hat / (jnp.sqrt(v_hat) + ADAM_EPS) + ADAM_WD * w)
    return delta, m, v


def _adamw_small(w, g, m, v, name):
    def body(w_ref, g_ref, m_ref, v_ref, d_out, m_out, v_out):
        d_out[...], m_out[...], v_out[...] = _adamw_math(w_ref[...], g_ref[...], m_ref[...], v_ref[...])

    shape = jax.ShapeDtypeStruct(w.shape, F32)
    return pl.pallas_call(body, name="adamw_" + name, out_shape=[shape, shape, shape],
                          compiler_params=_params())(w, g, m, v)


def _adamw_layer(i, w, g, m, v, prev, name):
    depth, rows, cols = w.shape
    tr = _row_tile(rows, 256)
    n_prev = 0 if prev is None else 4

    def body(*refs):
        w_ref, g_ref, m_ref, v_ref = refs[:4]
        g_out, d_out, m_out, v_out = refs[4 + n_prev:]
        gv = g_ref[...]
        g_out[0] = gv
        d_out[0], m_out[0], v_out[0] = _adamw_math(w_ref[0], gv, m_ref[0], v_ref[0])

    lay = pl.BlockSpec((1, tr, cols), lambda r: (i, r, 0))
    shape = jax.ShapeDtypeStruct(w.shape, F32)
    return pl.pallas_call(
        body, name="adamw_" + name, grid=(rows // tr,),
        in_specs=[lay, pl.BlockSpec((tr, cols), lambda r: (r, 0)), lay, lay] + [ANY] * n_prev,
        out_specs=[lay] * 4, out_shape=[shape] * 4,
        input_output_aliases={4 + k: k for k in range(n_prev)},
        compiler_params=_params(("parallel",)),
    )(w, g, m, v, *(prev or ()))


def _sum_leading(buf, name):
    n, rows, cols = buf.shape
    tr = _row_tile(rows, 512)

    def body(in_ref, out_ref):
        acc = in_ref[0]
        for k in range(1, n):
            acc = acc + in_ref[k]
        out_ref[...] = acc

    return pl.pallas_call(
        body, name=name, grid=(rows // tr,),
        in_specs=[pl.BlockSpec((n, tr, cols), lambda i: (0, i, 0))],
        out_specs=pl.BlockSpec((tr, cols), lambda i: (i, 0)),
        out_shape=jax.ShapeDtypeStruct((rows, cols), F32),
        compiler_params=_params(("parallel",)),
    )(buf)


_SHARDED_SMALL = ("meta", "conv_a_w", "ssm_conv_w", "conf_conv_w")
_LAYER_SMALL = ("pre_g", "post_g", "conv_a_w", "ssm_conv_w", "ssm_conv_b", "dt_bias", "a_log", "d_skip",
                "ssm_norm_g", "conf_conv_w", "conf_conv_b", "conf_ln_g", "conf_ln_b")
_WEIGHTS = ("meta", "pre_g", "post_g", "w_in", "w_out", "conv_a_w", "ssm_conv_w", "ssm_conv_b", "dt_bias", "a_log",
            "d_skip", "ssm_norm_g", "conf_conv_w", "conf_conv_b", "conf_ln_g", "conf_ln_b")


def _unshard_last(a):
    return jnp.moveaxis(a, 0, -2).reshape(a.shape[1:-1] + (N_CHIPS * a.shape[-1],))


def _shard_last(a):
    return jnp.moveaxis(a.reshape(a.shape[:-1] + (N_CHIPS, a.shape[-1] // N_CHIPS)), -2, 0)


def _with_own_block(a, n, at):
    return lax.dynamic_update_index_in_dim(jnp.zeros((n,) + a.shape, a.dtype), a, at, 0)


def kernel(x, meta, pre_g, post_g, w_in, w_out, conv_a_w, ssm_conv_w, ssm_conv_b, dt_bias, a_log, d_skip, ssm_norm_g, conf_conv_w, conf_conv_b, conf_ln_g, conf_ln_b, loss_target, m_meta, m_pre_g, m_post_g, m_w_in, m_w_out, m_conv_a_w, m_ssm_conv_w, m_ssm_conv_b, m_dt_bias, m_a_log, m_d_skip, m_ssm_norm_g, m_conf_conv_w, m_conf_conv_b, m_conf_ln_g, m_conf_ln_b, v_meta, v_pre_g, v_post_g, v_w_in, v_w_out, v_conv_a_w, v_ssm_conv_w, v_ssm_conv_b, v_dt_bias, v_a_log, v_d_skip, v_ssm_norm_g, v_conf_conv_w, v_conf_conv_b, v_conf_ln_g, v_conf_ln_b):
    w = dict(meta=meta, pre_g=pre_g, post_g=post_g, w_in=w_in, w_out=w_out, conv_a_w=conv_a_w,
             ssm_conv_w=ssm_conv_w, ssm_conv_b=ssm_conv_b, dt_bias=dt_bias, a_log=a_log, d_skip=d_skip,
             ssm_norm_g=ssm_norm_g, conf_conv_w=conf_conv_w, conf_conv_b=conf_conv_b, conf_ln_g=conf_ln_g,
             conf_ln_b=conf_ln_b)
    mom = dict(meta=m_meta, pre_g=m_pre_g, post_g=m_post_g, w_in=m_w_in, w_out=m_w_out, conv_a_w=m_conv_a_w,
               ssm_conv_w=m_ssm_conv_w, ssm_conv_b=m_ssm_conv_b, dt_bias=m_dt_bias, a_log=m_a_log, d_skip=m_d_skip,
               ssm_norm_g=m_ssm_norm_g, conf_conv_w=m_conf_conv_w, conf_conv_b=m_conf_conv_b,
               conf_ln_g=m_conf_ln_g, conf_ln_b=m_conf_ln_b)
    vel = dict(meta=v_meta, pre_g=v_pre_g, post_g=v_post_g, w_in=v_w_in, w_out=v_w_out, conv_a_w=v_conv_a_w,
               ssm_conv_w=v_ssm_conv_w, ssm_conv_b=v_ssm_conv_b, dt_bias=v_dt_bias, a_log=v_a_log, d_skip=v_d_skip,
               ssm_norm_g=v_ssm_norm_g, conf_conv_w=v_conf_conv_w, conf_conv_b=v_conf_conv_b,
               conf_ln_g=v_conf_ln_g, conf_ln_b=v_conf_ln_b)
    bl, seq, d = x.shape
    dm = Dims(bl, seq, d)
    depth = w_in.shape[0]
    chip = (2 * lax.axis_index("x") + lax.axis_index("y")).astype(jnp.int32)
    dev = 2 * chip + lax.axis_index("c").astype(jnp.int32)
    cst = _head_consts(dm)

    got = _exchange("gather_small_weights",
                    _ride_gather_small([_with_own_block(w[n], N_CHIPS, chip) for n in _SHARDED_SMALL]))
    full = dict(w)
    for n, a in zip(_SHARDED_SMALL, got):
        full[n] = _unshard_last(a)

    bases = [[_place_own(w_in[i], chip.reshape(1)), _place_own(w_out[i], chip.reshape(1))] for i in range(depth)]
    gathered = _exchange("gather_d2d_first", _ride_gather_d2d(_exchange("gather_ici_first",
                                                                         _ride_gather_ici(bases[0]))))
    h = _embed(x, full["meta"], dm)
    saved, proj_w = [], []
    for i in range(depth):
        lw = {n: full[n][i] for n in _LAYER_SMALL}
        proj_w.append((_w_in_from_shards(gathered[0], dm), gathered[1].reshape(2 * d, d)))
        nxt = i + 1 < depth
        h, keep, gathered = _layer_fwd(h, lw, proj_w[i][0], proj_w[i][1], cst, dm,
                                       _ride_gather_ici(bases[i + 1]) if nxt else None,
                                       _ride_gather_d2d if nxt else None)
        saved.append(keep)

    dh, loss = _loss_head(h, loss_target, dm)
    loss = lax.psum(loss, ("x", "y", "c"))

    small_g = {n: [None] * depth for n in _LAYER_SMALL}
    big = {"w_in": None, "w_out": None}
    reduce = None
    for i in reversed(range(depth)):
        lw = {n: full[n][i] for n in _LAYER_SMALL}
        dh, pieces, dwo, sg = _layer_bwd(dh, saved[i], lw, proj_w[i][0], proj_w[i][1], cst, dm, reduce)
        for n in _LAYER_SMALL:
            small_g[n][i] = sg[n]
        if reduce is not None:
            for n, g in zip(("w_in", "w_out"), reduce.result):
                big[n] = _adamw_layer(i + 1, w[n], g, mom[n], vel[n], big[n], n)
        reduce = _GradReduce([_grad_to_shards(pieces, dm), dwo.reshape(N_CHIPS, 2 * d // N_CHIPS, d)])
    reduce.run_alone()
    for n, g in zip(("w_in", "w_out"), reduce.result):
        big[n] = _adamw_layer(0, w[n], g, mom[n], vel[n], big[n], n)
    grad_x, gmeta = _unembed(dh, dm)

    g = {n: jnp.stack(v) for n, v in small_g.items()}
    g["meta"] = gmeta
    small = [n for n in _WEIGHTS if n not in ("w_in", "w_out")]
    flat = jnp.concatenate([g[n].reshape(-1) for n in small])
    rows = -(-flat.shape[0] // (16 * LANES)) * 16
    flat = jnp.pad(flat, (0, rows * LANES - flat.shape[0])).reshape(rows, LANES)
    parts = _exchange("small_grads_gather_all", _ride_gather_all(_with_own_block(flat, N_DEV, dev)))[0]
    total = _sum_leading(parts, "small_grads_sum").reshape(-1)
    grads, deltas, new_m, new_v = {}, {}, {}, {}
    off = 0
    for n in small:
        size = g[n].size
        fullg = total[off:off + size].reshape(g[n].shape)
        off += size
        if n in _SHARDED_SMALL:
            fullg = lax.dynamic_index_in_dim(_shard_last(fullg), chip, axis=0, keepdims=False)
        grads[n] = fullg
        deltas[n], new_m[n], new_v[n] = _adamw_small(w[n], fullg, mom[n], vel[n], n)
    for n in ("w_in", "w_out"):
        grads[n], deltas[n], new_m[n], new_v[n] = big[n]

    return (loss, grad_x, *[grads[n] for n in _WEIGHTS], *[deltas[n] for n in _WEIGHTS],
            *[new_m[n] for n in _WEIGHTS], *[new_v[n] for n in _WEIGHTS])
```

```python
import jax
import jax.numpy as jnp
from jax import lax
from jax.experimental import pallas as pl
from jax.experimental.pallas import tpu as pltpu

F32 = jnp.float32
BF16 = jnp.bfloat16

N_META = 16
TT = 128
SSM_STATE = 128
SSM_GROUPS = 2
SSM_HEAD_DIM = 64
CONV_A_K = 3
SSM_CONV_K = 4
CONF_K = 31
NORM_EPS = 1e-6
LN_EPS = 1e-5
LANES = 128
MXU_DIM = 256
DT_PAD = LANES
CONF_HALO = 32
SMALL_HALO = 8
VMEM_LIMIT = 56 * 1024 * 1024
N_CHIPS = 4
N_DEV = 8

ADAM_LR = 0.001
ADAM_B1 = 0.9
ADAM_B2 = 0.999
ADAM_EPS = 1e-08
ADAM_WD = 0.01
ADAM_STEP = 10

MESH = pl.DeviceIdType.MESH
ANY = pl.BlockSpec(memory_space=pl.ANY)


class Dims:
    def __init__(self, bl, seq, d):
        self.BL, self.S, self.D = bl, seq, d
        self.L = seq + N_META
        self.Lp = -(-self.L // TT) * TT
        self.NT = self.Lp // TT
        self.R = bl * self.Lp
        self.DA = d // 2
        self.DB = d
        self.DC = d // 2
        self.H = self.DB // SSM_HEAD_DIM
        self.HPG = self.H // SSM_GROUPS
        self.GN = SSM_GROUPS * SSM_STATE
        self.WA = 4 * self.DA
        self.WB = 2 * self.DB + 2 * self.GN
        self.WC = 3 * self.DC
        self.NP = self.WA + self.WB + self.WC + DT_PAD
        self.NIN = self.WA + self.WB + self.H + self.WC
        self.XBC = self.DB + 2 * self.GN
        assert self.H % 2 == 0 and self.HPG % 2 == 0 and self.H <= DT_PAD
        assert self.DA % LANES == 0 and (self.WA + self.WB) % self.DC == 0 and self.WA % self.DB == 0


def _row_tile(n, target):
    best = None
    for t in range(16, min(n, target) + 1, 16):
        if n % t == 0:
            best = t
    assert best is not None
    return best


def _col_tile(n, target):
    best = None
    for t in range(LANES, min(n, target) + 1, LANES):
        if n % t == 0:
            best = t
    assert best is not None
    return best


def _params(sem=None):
    return pltpu.CompilerParams(dimension_semantics=sem, vmem_limit_bytes=VMEM_LIMIT)


def _sigmoid(x):
    return 1.0 / (1.0 + jnp.exp(-x))


def _silu_and_grad(x):
    s = _sigmoid(x)
    return x * s, s * (1.0 + x * (1.0 - s))


def _dot(a, b):
    return jnp.dot(a, b, preferred_element_type=F32)


def _dot_nt(a, b):
    return lax.dot_general(a, b, (((1,), (1,)), ((), ())), preferred_element_type=F32)


def _dot_tn(a, b):
    return lax.dot_general(a, b, (((0,), (0,)), ((), ())), preferred_element_type=F32)


def _split3(x):
    x1 = x.astype(BF16)
    r1 = x - x1.astype(F32)
    x2 = r1.astype(BF16)
    x3 = (r1 - x2.astype(F32)).astype(BF16)
    return x1, x2, x3


class Rider:
    def __init__(self, plan, ins, out_shapes, aliases, nsem):
        self.plan, self.ins, self.out_shapes, self.aliases, self.nsem = plan, list(ins), list(out_shapes), aliases, nsem


def _place():
    x, y, c = lax.axis_index("x"), lax.axis_index("y"), lax.axis_index("c")
    chips = [(1 - x, y), (x, 1 - y), (1 - x, 1 - y)]
    return x, y, c, chips


def _remote(k, src, dst, to, send_sems, recv_sems):
    return pltpu.make_async_remote_copy(src_ref=src, dst_ref=dst, send_sem=send_sems.at[k], recv_sem=recv_sems.at[k],
                                        device_id=to, device_id_type=MESH)


def _call(body, name, grid, in_specs, out_specs, out_shape, scratch_shapes, sem, args, rider=None):
    if rider is None:
        outs = pl.pallas_call(body, name=name, grid=grid, in_specs=in_specs, out_specs=out_specs, out_shape=out_shape,
                              scratch_shapes=scratch_shapes, compiler_params=_params(sem))(*args)
        return list(outs), []
    n_in, n_out, n_scr = len(args), len(out_shape), len(scratch_shapes)
    r_in, r_out = len(rider.ins), len(rider.out_shapes)

    def hosted(*refs):
        ins, rins = refs[:n_in], refs[n_in:n_in + r_in]
        o0 = n_in + r_in
        outs, routs = refs[o0:o0 + n_out], refs[o0 + n_out:o0 + n_out + r_out]
        scr = refs[o0 + n_out + r_out:o0 + n_out + r_out + n_scr]
        send_sems, recv_sems = refs[o0 + n_out + r_out + n_scr:]
        first = pl.program_id(0) == 0
        last = pl.program_id(0) == grid[0] - 1
        for ax in range(1, len(grid)):
            first = jnp.logical_and(first, pl.program_id(ax) == 0)
            last = jnp.logical_and(last, pl.program_id(ax) == grid[ax] - 1)

        @pl.when(first)
        def _():
            starts, _ = rider.plan(rins, routs, send_sems, recv_sems)
            for cp in starts:
                cp.start()

        body(*ins, *outs, *scr)

        @pl.when(last)
        def _():
            _, waits = rider.plan(rins, routs, send_sems, recv_sems)
            for wait in waits:
                wait()

    res = pl.pallas_call(
        hosted, name=name, grid=grid,
        in_specs=list(in_specs) + [ANY] * r_in, out_specs=list(out_specs) + [ANY] * r_out,
        out_shape=list(out_shape) + rider.out_shapes,
        input_output_aliases={n_in + k: n_out + v for k, v in rider.aliases.items()},
        scratch_shapes=list(scratch_shapes) + [pltpu.SemaphoreType.DMA((rider.nsem,)),
                                               pltpu.SemaphoreType.DMA((rider.nsem,))],
        compiler_params=_params(("arbitrary",) * len(grid)),
    )(*args, *rider.ins)
    return list(res[:n_out]), list(res[n_out:])


def _exchange(name, rider):
    r_in, r_out = len(rider.ins), len(rider.out_shapes)

    def body(*refs):
        rins, routs = refs[:r_in], refs[r_in:r_in + r_out]
        send_sems, recv_sems = refs[r_in + r_out:]
        starts, waits = rider.plan(rins, routs, send_sems, recv_sems)
        for cp in starts:
            cp.start()
        for wait in waits:
            wait()

    res = pl.pallas_call(
        body, name=name, in_specs=[ANY] * r_in, out_specs=[ANY] * r_out, out_shape=rider.out_shapes,
        input_output_aliases=dict(rider.aliases),
        scratch_shapes=[pltpu.SemaphoreType.DMA((rider.nsem,)), pltpu.SemaphoreType.DMA((rider.nsem,))],
    )(*rider.ins)
    return list(res)


def _same(arrays):
    return [jax.ShapeDtypeStruct(a.shape, a.dtype) for a in arrays]


def _ride_gather_ici(bases):
    n = len(bases)

    def plan(ins, outs, ss, rs):
        x, y, c, chips = _place()
        me = 2 * x + y
        starts, waits = [], []
        for a in range(n):
            half = outs[a].shape[1] // 2
            mine = pl.ds(c * half, half)
            for j, chip in enumerate(chips):
                cp = _remote(3 * a + j, outs[a].at[me, mine], outs[a].at[me, mine], (*chip, c), ss, rs)
                got = outs[a].at[2 * chip[0] + chip[1], mine]
                starts.append(cp)
                waits += [cp.wait_send, _remote(3 * a + j, got, got, (*chip, c), ss, rs).wait_recv]
        return starts, waits

    return Rider(plan, bases, _same(bases), {a: a for a in range(n)}, 3 * n)


def _ride_gather_d2d(bases):
    n = len(bases)

    def plan(ins, outs, ss, rs):
        x, y, c, chips = _place()
        sib = (x, y, 1 - c)
        starts, waits = [], []
        for a in range(n):
            half = outs[a].shape[1] // 2
            for j, chip in enumerate(chips):
                frm = 2 * chip[0] + chip[1]
                got = outs[a].at[frm, pl.ds(c * half, half)]
                theirs = outs[a].at[frm, pl.ds((1 - c) * half, half)]
                cp = _remote(3 * a + j, got, got, sib, ss, rs)
                starts.append(cp)
                waits += [cp.wait_send, _remote(3 * a + j, theirs, theirs, sib, ss, rs).wait_recv]
        return starts, waits

    return Rider(plan, bases, _same(bases), {a: a for a in range(n)}, 3 * n)


def _ride_gather_small(bases):
    n = len(bases)

    def plan(ins, outs, ss, rs):
        x, y, c, chips = _place()
        me = 2 * x + y
        starts, waits = [], []
        for a in range(n):
            for j, chip in enumerate(chips):
                cp = _remote(3 * a + j, outs[a].at[me], outs[a].at[me], (*chip, c), ss, rs)
                got = outs[a].at[2 * chip[0] + chip[1]]
                starts.append(cp)
                waits += [cp.wait_send, _remote(3 * a + j, got, got, (*chip, c), ss, rs).wait_recv]
        return starts, waits

    return Rider(plan, bases, _same(bases), {a: a for a in range(n)}, 3 * n)


def _ride_swap_halves(gs):
    n = len(gs)

    def plan(ins, outs, ss, rs):
        x, y, c, _ = _place()
        cps = [_remote(a, ins[a].at[:, 1 - c], outs[a], (x, y, 1 - c), ss, rs) for a in range(n)]
        return cps, [cp.wait for cp in cps]

    shapes = [jax.ShapeDtypeStruct((g.shape[0],) + g.shape[2:], g.dtype) for g in gs]
    return Rider(plan, gs, shapes, {}, n)


def _ride_to_owners(ps):
    n = len(ps)

    def plan(ins, outs, ss, rs):
        x, y, c, chips = _place()
        cps = []
        for a in range(n):
            for j, chip in enumerate(chips):
                cps.append(_remote(3 * a + j, ins[a].at[2 * chip[0] + chip[1]], outs[a].at[j], (*chip, c), ss, rs))
        return cps, [cp.wait for cp in cps]

    shapes = [jax.ShapeDtypeStruct((3,) + p.shape[1:], p.dtype) for p in ps]
    return Rider(plan, ps, shapes, {}, 3 * n)


def _ride_join_halves(qs):
    n = len(qs)

    def plan(ins, outs, ss, rs):
        x, y, c, _ = _place()
        sib = (x, y, 1 - c)
        starts, waits = [], []
        for a in range(n):
            cp = _remote(a, outs[a].at[c], outs[a].at[c], sib, ss, rs)
            starts.append(cp)
            waits += [cp.wait_send, _remote(a, outs[a].at[1 - c], outs[a].at[1 - c], sib, ss, rs).wait_recv]
        return starts, waits

    return Rider(plan, qs, _same(qs), {a: a for a in range(n)}, n)


def _ride_gather_all(base):
    def plan(ins, outs, ss, rs):
        x, y, c, _ = _place()
        me = 4 * x + 2 * y + c

        def peer(k):
            return (1 - x if (k >> 2) & 1 else x, 1 - y if (k >> 1) & 1 else y, 1 - c if k & 1 else c)

        starts, waits = [], []
        for k in range(1, N_DEV):
            px, py, pc = peer(k)
            cp = _remote(k - 1, outs[0].at[me], outs[0].at[me], (px, py, pc), ss, rs)
            got = outs[0].at[4 * px + 2 * py + pc]
            starts.append(cp)
            waits += [cp.wait_send, _remote(k - 1, got, got, (px, py, pc), ss, rs).wait_recv]
        return starts, waits

    return Rider(plan, [base], _same([base]), {0: 0}, N_DEV - 1)


def _embed(x, meta, dm):
    dc = _col_tile(dm.D, 256)
    s, lp = dm.S, dm.Lp

    def body(x_ref, meta_ref, h_ref):
        h_ref[0:N_META, :] = meta_ref[...]
        h_ref[N_META:N_META + s, :] = x_ref[0]
        if lp > N_META + s:
            h_ref[N_META + s:lp, :] = jnp.zeros((lp - N_META - s, dc), F32)

    return pl.pallas_call(
        body, name="embed", grid=(dm.BL, dm.D // dc),
        in_specs=[pl.BlockSpec((1, s, dc), lambda b, j: (b, 0, j)),
                  pl.BlockSpec((N_META, dc), lambda b, j: (0, j))],
        out_specs=pl.BlockSpec((lp, dc), lambda b, j: (b, j)),
        out_shape=jax.ShapeDtypeStruct((dm.R, dm.D), F32),
        compiler_params=_params(("parallel", "parallel")),
    )(x, meta)


def _loss_head(h, target, dm):
    dc = _col_tile(dm.D, 256)
    s, lp, nj = dm.S, dm.Lp, dm.D // dc

    def body(h_ref, t_ref, dh_ref, l_ref):
        diff = h_ref[N_META:N_META + s, :] - t_ref[0]
        dh_ref[0:N_META, :] = jnp.zeros((N_META, dc), F32)
        dh_ref[N_META:N_META + s, :] = diff * (1.0 / dm.D)
        if lp > N_META + s:
            dh_ref[N_META + s:lp, :] = jnp.zeros((lp - N_META - s, dc), F32)
        l_ref[...] = jnp.full((8, LANES), (0.5 / dm.D) * jnp.sum(diff * diff), F32)

    dh, part = pl.pallas_call(
        body, name="loss_head", grid=(dm.BL, nj),
        in_specs=[pl.BlockSpec((lp, dc), lambda b, j: (b, j)),
                  pl.BlockSpec((1, s, dc), lambda b, j: (b, 0, j))],
        out_specs=[pl.BlockSpec((lp, dc), lambda b, j: (b, j)),
                   pl.BlockSpec((8, LANES), lambda b, j: (b * nj + j, 0))],
        out_shape=[jax.ShapeDtypeStruct((dm.R, dm.D), F32),
                   jax.ShapeDtypeStruct((dm.BL * nj * 8, LANES), F32)],
        compiler_params=_params(("parallel", "parallel")),
    )(h, target)
    return dh, jnp.sum(part[::8, 0])


def _unembed(dh, dm):
    dc = _col_tile(dm.D, 256)
    s, lp = dm.S, dm.Lp

    def body(dh_ref, gx_ref, gm_ref):
        gx_ref[0] = dh_ref[N_META:N_META + s, :]

        @pl.when(pl.program_id(1) == 0)
        def _():
            gm_ref[...] = dh_ref[0:N_META, :]

        @pl.when(pl.program_id(1) > 0)
        def _():
            gm_ref[...] = gm_ref[...] + dh_ref[0:N_META, :]

    return pl.pallas_call(
        body, name="unembed", grid=(dm.D // dc, dm.BL),
        in_specs=[pl.BlockSpec((lp, dc), lambda j, b: (b, j))],
        out_specs=[pl.BlockSpec((1, s, dc), lambda j, b: (b, 0, j)),
                   pl.BlockSpec((N_META, dc), lambda j, b: (0, j))],
        out_shape=[jax.ShapeDtypeStruct((dm.BL, s, dm.D), F32),
                   jax.ShapeDtypeStruct((N_META, dm.D), F32)],
        compiler_params=_params(("parallel", "arbitrary")),
    )(dh)


def _fwd_in(h, pre_g, w, dm, rider=None):
    tm = _row_tile(dm.R, 1088)
    tn = _col_tile(dm.NP, 896)
    nj = dm.NP // tn

    def body(h_ref, g_ref, w_ref, proj_ref, dt_ref, hn_ref):
        @pl.when(pl.program_id(1) == 0)
        def _():
            xf = h_ref[...]
            r = lax.rsqrt(jnp.mean(xf * xf, axis=-1, keepdims=True) + NORM_EPS)
            hn_ref[...] = (xf * r * g_ref[...]).astype(BF16)

        res = _dot(hn_ref[...], w_ref[...])
        proj_ref[...] = res.astype(BF16)

        @pl.when(pl.program_id(1) == nj - 1)
        def _():
            dt_ref[...] = res[:, tn - DT_PAD:tn]

    return _call(
        body, "fwd_in", (dm.R // tm, nj),
        [pl.BlockSpec((tm, dm.D), lambda i, j: (i, 0)),
         pl.BlockSpec((1, dm.D), lambda i, j: (0, 0)),
         pl.BlockSpec((dm.D, tn), lambda i, j: (0, j))],
        [pl.BlockSpec((tm, tn), lambda i, j: (i, j)),
         pl.BlockSpec((tm, DT_PAD), lambda i, j: (i, 0)),
         pl.BlockSpec((tm, dm.D), lambda i, j: (i, 0))],
        [jax.ShapeDtypeStruct((dm.R, dm.NP), BF16), jax.ShapeDtypeStruct((dm.R, DT_PAD), F32),
         jax.ShapeDtypeStruct((dm.R, dm.D), BF16)],
        [], ("parallel", "arbitrary"), (h, pre_g, w), rider)


def _fwd_out(ya, yb, yc, w_out, h, post_g, dm, rider=None):
    tm = _row_tile(dm.Lp, 544)
    tiles_per_seq = dm.Lp // tm
    da, db, dc = dm.DA, dm.DB, dm.DC

    def body(ya_ref, yb_ref, yc_ref, w_ref, h_ref, g_ref, hn_ref, m_ref):
        m = _dot(ya_ref[...], w_ref[0:da, :])
        m = m + _dot(yb_ref[...], w_ref[da:da + db, :])
        m = m + _dot(yc_ref[...], w_ref[da + db:da + db + dc, :])
        m_ref[...] = m
        r = lax.rsqrt(jnp.mean(m * m, axis=-1, keepdims=True) + NORM_EPS)
        t = (pl.program_id(0) % tiles_per_seq) * tm + lax.broadcasted_iota(jnp.int32, (tm, 1), 0)
        keep = (t < dm.L).astype(F32)
        hn_ref[...] = (h_ref[...] + m * r * g_ref[...]) * keep

    row = lambda i: (i, 0)
    fixed = lambda i: (0, 0)
    return _call(
        body, "fwd_out", (dm.R // tm,),
        [pl.BlockSpec((tm, da), row), pl.BlockSpec((tm, db), row), pl.BlockSpec((tm, dc), row),
         pl.BlockSpec((2 * dm.D, dm.D), fixed), pl.BlockSpec((tm, dm.D), row), pl.BlockSpec((1, dm.D), fixed)],
        [pl.BlockSpec((tm, dm.D), row), pl.BlockSpec((tm, dm.D), row)],
        [jax.ShapeDtypeStruct((dm.R, dm.D), F32), jax.ShapeDtypeStruct((dm.R, dm.D), F32)],
        [], ("parallel",), (ya, yb, yc, w_out, h, post_g), rider)


def _bwd_out(dh, m, post_g, w_out, ya, yb, yc, dm, rider=None):
    tm = _row_tile(dm.R, MXU_DIM)
    da, db, dc = dm.DA, dm.DB, dm.DC

    def body(dh_ref, m_ref, g_ref, w_ref, ya_ref, yb_ref, yc_ref, dya_ref, dyb_ref, dyc_ref, dw_ref, dg_ref):
        @pl.when(pl.program_id(0) == 0)
        def _():
            dw_ref[...] = jnp.zeros_like(dw_ref)
            dg_ref[...] = jnp.zeros_like(dg_ref)

        m = m_ref[...]
        dh_ = dh_ref[...]
        r = lax.rsqrt(jnp.mean(m * m, axis=-1, keepdims=True) + NORM_EPS)
        n = m * r
        dg_ref[0:1, :] = dg_ref[0:1, :] + jnp.sum(dh_ * n, axis=0, keepdims=True)
        dn = dh_ * g_ref[...]
        dm_ = (r * (dn - n * jnp.mean(dn * n, axis=-1, keepdims=True))).astype(BF16)
        dya_ref[...] = _dot_nt(dm_, w_ref[0:da, :])
        dyb_ref[...] = _dot_nt(dm_, w_ref[da:da + db, :])
        dyc_ref[...] = _dot_nt(dm_, w_ref[da + db:da + db + dc, :])
        dw_ref[0:da, :] = dw_ref[0:da, :] + _dot_tn(ya_ref[...], dm_)
        dw_ref[da:da + db, :] = dw_ref[da:da + db, :] + _dot_tn(yb_ref[...], dm_)
        dw_ref[da + db:da + db + dc, :] = dw_ref[da + db:da + db + dc, :] + _dot_tn(yc_ref[...], dm_)

    row = lambda i: (i, 0)
    fixed = lambda i: (0, 0)
    return _call(
        body, "bwd_out", (dm.R // tm,),
        [pl.BlockSpec((tm, dm.D), row), pl.BlockSpec((tm, dm.D), row), pl.BlockSpec((1, dm.D), fixed),
         pl.BlockSpec((2 * dm.D, dm.D), fixed),
         pl.BlockSpec((tm, da), row), pl.BlockSpec((tm, db), row), pl.BlockSpec((tm, dc), row)],
        [pl.BlockSpec((tm, da), row), pl.BlockSpec((tm, db), row), pl.BlockSpec((tm, dc), row),
         pl.BlockSpec((2 * dm.D, dm.D), fixed), pl.BlockSpec((8, dm.D), fixed)],
        [jax.ShapeDtypeStruct((dm.R, da), F32), jax.ShapeDtypeStruct((dm.R, db), F32),
         jax.ShapeDtypeStruct((dm.R, dc), F32),
         jax.ShapeDtypeStruct((2 * dm.D, dm.D), F32), jax.ShapeDtypeStruct((8, dm.D), F32)],
        [], ("arbitrary",), (dh, m, post_g, w_out, ya, yb, yc), rider)


def _bwd_in_dx(dpa, dpb, dpc, dpt, w, h, dh, pre_g, dm, rider=None):
    tm = _row_tile(dm.R, 272)
    wa, wb, wc = dm.WA, dm.WB, dm.WC

    def body(dpa_ref, dpb_ref, dpc_ref, dpt_ref, w_ref, h_ref, dh_ref, g_ref, out_ref, dg_ref):
        @pl.when(pl.program_id(0) == 0)
        def _():
            dg_ref[...] = jnp.zeros_like(dg_ref)

        dhn = _dot_nt(dpa_ref[...], w_ref[:, 0:wa])
        dhn = dhn + _dot_nt(dpb_ref[...], w_ref[:, wa:wa + wb])
        dhn = dhn + _dot_nt(dpc_ref[...], w_ref[:, wa + wb:wa + wb + wc])
        dhn = dhn + _dot_nt(dpt_ref[...], w_ref[:, wa + wb + wc:wa + wb + wc + DT_PAD])
        xf = h_ref[...]
        r = lax.rsqrt(jnp.mean(xf * xf, axis=-1, keepdims=True) + NORM_EPS)
        n = xf * r
        dg_ref[0:1, :] = dg_ref[0:1, :] + jnp.sum(dhn * n, axis=0, keepdims=True)
        dn = dhn * g_ref[...]
        out_ref[...] = dh_ref[...] + r * (dn - n * jnp.mean(dn * n, axis=-1, keepdims=True))

    row = lambda i: (i, 0)
    fixed = lambda i: (0, 0)
    return _call(
        body, "bwd_in_dx", (dm.R // tm,),
        [pl.BlockSpec((tm, wa), row), pl.BlockSpec((tm, wb), row), pl.BlockSpec((tm, wc), row),
         pl.BlockSpec((tm, DT_PAD), row), pl.BlockSpec((dm.D, dm.NP), fixed),
         pl.BlockSpec((tm, dm.D), row), pl.BlockSpec((tm, dm.D), row), pl.BlockSpec((1, dm.D), fixed)],
        [pl.BlockSpec((tm, dm.D), row), pl.BlockSpec((8, dm.D), fixed)],
        [jax.ShapeDtypeStruct((dm.R, dm.D), F32), jax.ShapeDtypeStruct((8, dm.D), F32)],
        [], ("arbitrary",), (dpa, dpb, dpc, dpt, w, h, dh, pre_g), rider)


def _bwd_in_dw(hn, dp, dm, piece):
    width = dp.shape[1]
    tn = _col_tile(width, 512)

    def body(hn_ref, dp_ref, dw_ref):
        dw_ref[...] = _dot_tn(hn_ref[...], dp_ref[...])

    return pl.pallas_call(
        body, name="bwd_in_dw_" + piece, grid=(width // tn,),
        in_specs=[pl.BlockSpec((dm.R, dm.D), lambda j: (0, 0)), pl.BlockSpec((dm.R, tn), lambda j: (0, j))],
        out_specs=pl.BlockSpec((dm.D, tn), lambda j: (0, j)),
        out_shape=jax.ShapeDtypeStruct((dm.D, width), F32),
        compiler_params=_params(("parallel",)),
    )(hn, dp)


def _tile_index(dm, reverse):
    if reverse:
        return lambda b, i: b * dm.NT + (dm.NT - 1 - i)
    return lambda b, i: b * dm.NT + i


def _halo_index(dm, rows):
    per_tile = TT // rows
    return lambda b, i: jnp.maximum((b * dm.NT + (dm.NT - 1 - i)) * per_tile - 1, 0)


HALO_BLOCK = 16


def _last_rows(x):
    return x.astype(F32)[HALO_BLOCK - SMALL_HALO:HALO_BLOCK]


def _mix_a_fwd(proj, conv_w, dm):
    da = dm.DA
    ti = _tile_index(dm, False)

    def body(ab_ref, ac_ref, ax_ref, az_ref, w_ref, y_ref, pbuf):
        i = pl.program_id(1)

        @pl.when(i == 0)
        def _():
            pbuf[0:SMALL_HALO, :] = jnp.zeros((SMALL_HALO, da), F32)

        @pl.when(i > 0)
        def _():
            pbuf[0:SMALL_HALO, :] = pbuf[TT:TT + SMALL_HALO, :]

        for lb in range(da // LANES):
            cs = slice(lb * LANES, (lb + 1) * LANES)
            p = ac_ref[:, cs].astype(F32) * ax_ref[:, cs].astype(F32)
            pbuf[SMALL_HALO:SMALL_HALO + TT, cs] = p
            q = (w_ref[0:1, cs] * pbuf[6:6 + TT, cs] + w_ref[1:2, cs] * pbuf[7:7 + TT, cs] + w_ref[2:3, cs] * p)
            az = az_ref[:, cs].astype(F32)
            y_ref[:, cs] = (ab_ref[:, cs].astype(F32) * q * (az * _sigmoid(az))).astype(BF16)

    col = lambda k: pl.BlockSpec((TT, da), lambda b, i: (ti(b, i), k))
    return pl.pallas_call(
        body, name="mix_a_fwd", grid=(dm.BL, dm.NT),
        in_specs=[col(0), col(1), col(2), col(3), pl.BlockSpec((CONV_A_K, da), lambda b, i: (0, 0))],
        out_specs=pl.BlockSpec((TT, da), lambda b, i: (ti(b, i), 0)),
        out_shape=jax.ShapeDtypeStruct((dm.R, da), BF16),
        scratch_shapes=[pltpu.VMEM((SMALL_HALO + TT, da), F32)],
        compiler_params=_params(("parallel", "arbitrary")),
    )(proj, proj, proj, proj, conv_w)


def _mix_a_bwd(proj, dya, conv_w, dm):
    da = dm.DA
    ti = _tile_index(dm, True)
    hi = _halo_index(dm, HALO_BLOCK)

    def body(ab_ref, ac_ref, ax_ref, az_ref, ach_ref, axh_ref, dy_ref, w_ref, dp_ref, dw_ref, pbuf, dqbuf):
        i = pl.program_id(1)
        halo_on = jnp.where(i == dm.NT - 1, 0.0, 1.0)

        @pl.when(i == 0)
        def _():
            dw_ref[...] = jnp.zeros_like(dw_ref)
            dqbuf[TT:TT + SMALL_HALO, :] = jnp.zeros((SMALL_HALO, da), F32)

        @pl.when(i > 0)
        def _():
            dqbuf[TT:TT + SMALL_HALO, :] = dqbuf[0:SMALL_HALO, :]

        for lb in range(da // LANES):
            cs = slice(lb * LANES, (lb + 1) * LANES)
            pbuf[0:SMALL_HALO, cs] = (_last_rows(ach_ref[:, cs]) * _last_rows(axh_ref[:, cs])) * halo_on
            ac, ax, ab, az = (r[:, cs].astype(F32) for r in (ac_ref, ax_ref, ab_ref, az_ref))
            p = ac * ax
            pbuf[SMALL_HALO:SMALL_HALO + TT, cs] = p
            p1 = pbuf[7:7 + TT, cs]
            p2 = pbuf[6:6 + TT, cs]
            w0, w1, w2 = w_ref[0:1, cs], w_ref[1:2, cs], w_ref[2:3, cs]
            q = w0 * p2 + w1 * p1 + w2 * p
            sz, dsz = _silu_and_grad(az)
            dy = dy_ref[:, cs]
            t1 = dy * ab
            dq = t1 * sz
            dqbuf[0:TT, cs] = dq
            dpv = w2 * dq + w1 * dqbuf[1:1 + TT, cs] + w0 * dqbuf[2:2 + TT, cs]
            dp_ref[:, lb * LANES:(lb + 1) * LANES] = (dy * q * sz).astype(BF16)
            dp_ref[:, da + lb * LANES:da + (lb + 1) * LANES] = (dpv * ax).astype(BF16)
            dp_ref[:, 2 * da + lb * LANES:2 * da + (lb + 1) * LANES] = (dpv * ac).astype(BF16)
            dp_ref[:, 3 * da + lb * LANES:3 * da + (lb + 1) * LANES] = (t1 * q * dsz).astype(BF16)
            dw_ref[0, 0:1, cs] = dw_ref[0, 0:1, cs] + jnp.sum(dq * p2, axis=0, keepdims=True)
            dw_ref[0, 1:2, cs] = dw_ref[0, 1:2, cs] + jnp.sum(dq * p1, axis=0, keepdims=True)
            dw_ref[0, 2:3, cs] = dw_ref[0, 2:3, cs] + jnp.sum(dq * p, axis=0, keepdims=True)

    col = lambda k: pl.BlockSpec((TT, da), lambda b, i: (ti(b, i), k))
    halo = lambda k: pl.BlockSpec((HALO_BLOCK, da), lambda b, i: (hi(b, i), k))
    return pl.pallas_call(
        body, name="mix_a_bwd", grid=(dm.BL, dm.NT),
        in_specs=[col(0), col(1), col(2), col(3), halo(1), halo(2),
                  pl.BlockSpec((TT, da), lambda b, i: (ti(b, i), 0)),
                  pl.BlockSpec((CONV_A_K, da), lambda b, i: (0, 0))],
        out_specs=[pl.BlockSpec((TT, dm.WA), lambda b, i: (ti(b, i), 0)),
                   pl.BlockSpec((1, 8, da), lambda b, i: (b, 0, 0))],
        out_shape=[jax.ShapeDtypeStruct((dm.R, dm.WA), BF16), jax.ShapeDtypeStruct((dm.BL, 8, da), F32)],
        scratch_shapes=[pltpu.VMEM((SMALL_HALO + TT, da), F32), pltpu.VMEM((TT + SMALL_HALO, da), F32)],
        compiler_params=_params(("parallel", "arbitrary")),
    )(proj, proj, proj, proj, proj, proj, dya, conv_w)


SUBLANES = 8
SHIFT_ROWS = TT + CONF_HALO - SUBLANES


TAP_ROWS = 64


def _split_lanes(buf, rows, val):
    for lb in range(val.shape[1] // LANES):
        buf[lb, rows, :] = val[:, lb * LANES:(lb + 1) * LANES]


def _join_lanes(buf):
    return jnp.concatenate([buf[lb] for lb in range(buf.shape[0])], axis=1)


def _fill_shifted(buf, shifted):
    def step(lb, carry):
        for r in range(1, SUBLANES):
            shifted[lb, r - 1, 0:SHIFT_ROWS, :] = buf[lb, r:r + SHIFT_ROWS, :]
        return carry

    lax.fori_loop(0, buf.shape[0], step, 0)


def _window(buf, shifted, d, r0, lb):
    r = d % SUBLANES
    rows = pl.ds(pl.multiple_of(r0 + (d - r), SUBLANES), TAP_ROWS)
    return buf[lb, rows, :] if r == 0 else shifted[lb, r - 1, rows, :]


def _tap_loop(nlb, body):
    per_lb = TT // TAP_ROWS

    def step(it, carry):
        lb = it // per_lb
        body(lb, pl.ds(pl.multiple_of(lb * LANES, LANES), LANES), pl.multiple_of((it % per_lb) * TAP_ROWS, TAP_ROWS))
        return carry

    lax.fori_loop(0, nlb * per_lb, step, 0)


TAP_CHAINS = 4


def _tree_sum(terms):
    sums = list(terms[:TAP_CHAINS])
    for n, t in enumerate(terms[TAP_CHAINS:]):
        sums[n % TAP_CHAINS] = sums[n % TAP_CHAINS] + t
    while len(sums) > 1:
        sums = [a + b for a, b in zip(sums[0::2], sums[1::2])] + ([sums[-1]] if len(sums) % 2 else [])
    return sums[0]


def _conf_conv(ubuf, ushift, w_ref, b_ref, u1buf):
    _fill_shifted(ubuf, ushift)

    def piece(lb, cs, r0):
        taps = [w_ref[k:k + 1, cs] * _window(ubuf, ushift, CONF_HALO - (CONF_K - 1) + k, r0, lb)
                for k in range(CONF_K)]
        u1buf[lb, pl.ds(r0, TAP_ROWS), :] = _tree_sum(taps) + b_ref[0:1, cs]

    _tap_loop(ubuf.shape[0], piece)


def _mix_c_fwd(proj, conv_w, conv_b, ln_g, ln_b, dm):
    dc = dm.DC
    nlb = dc // LANES
    c0 = (dm.WA + dm.WB) // dc
    ti = _tile_index(dm, False)

    def body(ca_ref, cg_ref, cz_ref, w_ref, b_ref, g_ref, be_ref, y_ref, ubuf, u1buf, ushift):
        i = pl.program_id(1)

        @pl.when(i == 0)
        def _():
            ubuf[:, 0:CONF_HALO, :] = jnp.zeros((nlb, CONF_HALO, LANES), F32)

        @pl.when(i > 0)
        def _():
            ubuf[:, 0:CONF_HALO, :] = ubuf[:, TT:TT + CONF_HALO, :]

        _split_lanes(ubuf, slice(CONF_HALO, CONF_HALO + TT),
                     ca_ref[...].astype(F32) * _sigmoid(cg_ref[...].astype(F32)))
        _conf_conv(ubuf, ushift, w_ref, b_ref, u1buf)
        u1 = _join_lanes(u1buf)
        mu = jnp.mean(u1, axis=-1, keepdims=True)
        xc = u1 - mu
        rstd = lax.rsqrt(jnp.mean(xc * xc, axis=-1, keepdims=True) + LN_EPS)
        u2 = xc * rstd * g_ref[...] + be_ref[...]
        cz = cz_ref[...].astype(F32)
        y_ref[...] = ((u2 * _sigmoid(u2)) * (cz * _sigmoid(cz))).astype(BF16)

    col = lambda k: pl.BlockSpec((TT, dc), lambda b, i: (ti(b, i), c0 + k))
    vec = pl.BlockSpec((1, dc), lambda b, i: (0, 0))
    return pl.pallas_call(
        body, name="mix_c_fwd", grid=(dm.BL, dm.NT),
        in_specs=[col(0), col(1), col(2), pl.BlockSpec((CONF_K, dc), lambda b, i: (0, 0)), vec, vec, vec],
        out_specs=pl.BlockSpec((TT, dc), lambda b, i: (ti(b, i), 0)),
        out_shape=jax.ShapeDtypeStruct((dm.R, dc), BF16),
        scratch_shapes=[pltpu.VMEM((nlb, CONF_HALO + TT, LANES), F32), pltpu.VMEM((nlb, TT, LANES), F32),
                        pltpu.VMEM((nlb, SUBLANES - 1, SHIFT_ROWS, LANES), F32)],
        compiler_params=_params(("parallel", "arbitrary")),
    )(proj, proj, proj, conv_w, conv_b, ln_g, ln_b)


def _mix_c_bwd(proj, dyc, conv_w, conv_b, ln_g, ln_b, dm):
    dc = dm.DC
    nlb = dc // LANES
    c0 = (dm.WA + dm.WB) // dc
    ti = _tile_index(dm, True)
    hi = _halo_index(dm, CONF_HALO)

    def body(ca_ref, cg_ref, cz_ref, cah_ref, cgh_ref, dy_ref, w_ref, b_ref, g_ref, be_ref,
             dp_ref, dw_ref, dv_ref, ubuf, u1buf, dubuf, du0buf, ushift, dshift, dwacc):
        i = pl.program_id(1)
        halo_on = jnp.where(i == dm.NT - 1, 0.0, 1.0)

        @pl.when(i == 0)
        def _():
            dwacc[...] = jnp.zeros_like(dwacc)
            dv_ref[...] = jnp.zeros_like(dv_ref)
            dubuf[:, TT:TT + CONF_HALO, :] = jnp.zeros((nlb, CONF_HALO, LANES), F32)

        @pl.when(i > 0)
        def _():
            dubuf[:, TT:TT + CONF_HALO, :] = dubuf[:, 0:CONF_HALO, :]

        _split_lanes(ubuf, slice(0, CONF_HALO),
                     cah_ref[...].astype(F32) * _sigmoid(cgh_ref[...].astype(F32)) * halo_on)
        sgg = _sigmoid(cg_ref[...].astype(F32))
        ca = ca_ref[...].astype(F32)
        _split_lanes(ubuf, slice(CONF_HALO, CONF_HALO + TT), ca * sgg)
        _conf_conv(ubuf, ushift, w_ref, b_ref, u1buf)
        u1 = _join_lanes(u1buf)
        mu = jnp.mean(u1, axis=-1, keepdims=True)
        xc = u1 - mu
        rstd = lax.rsqrt(jnp.mean(xc * xc, axis=-1, keepdims=True) + LN_EPS)
        xhat = xc * rstd
        u2 = xhat * g_ref[...] + be_ref[...]
        su, dsu = _silu_and_grad(u2)
        sz, dsz = _silu_and_grad(cz_ref[...].astype(F32))
        dy = dy_ref[...]
        du2 = dy * dsu * sz
        dp_ref[:, 2 * dc:3 * dc] = (dy * su * dsz).astype(BF16)
        dxhat = du2 * g_ref[...]
        du1 = rstd * (dxhat - jnp.mean(dxhat, axis=-1, keepdims=True)
                      - xhat * jnp.mean(dxhat * xhat, axis=-1, keepdims=True))
        dv_ref[0, 0:1, :] = dv_ref[0, 0:1, :] + jnp.sum(du1, axis=0, keepdims=True)
        dv_ref[0, 1:2, :] = dv_ref[0, 1:2, :] + jnp.sum(du2 * xhat, axis=0, keepdims=True)
        dv_ref[0, 2:3, :] = dv_ref[0, 2:3, :] + jnp.sum(du2, axis=0, keepdims=True)
        _split_lanes(dubuf, slice(0, TT), du1)
        _fill_shifted(dubuf, dshift)

        def piece(lb, cs, r0):
            du0buf[lb, pl.ds(r0, TAP_ROWS), :] = _tree_sum(
                [w_ref[k:k + 1, cs] * _window(dubuf, dshift, CONF_K - 1 - k, r0, lb) for k in range(CONF_K)])
            d1 = dubuf[lb, pl.ds(r0, TAP_ROWS), :]
            for k in range(CONF_K):
                prod = d1 * _window(ubuf, ushift, CONF_HALO - (CONF_K - 1) + k, r0, lb)
                dwacc[lb, k] = dwacc[lb, k] + jnp.sum(prod.reshape(TAP_ROWS // SUBLANES, SUBLANES, LANES), axis=0)

        _tap_loop(nlb, piece)
        du0 = _join_lanes(du0buf)
        dp_ref[:, 0:dc] = (du0 * sgg).astype(BF16)
        dp_ref[:, dc:2 * dc] = (du0 * ca * sgg * (1.0 - sgg)).astype(BF16)

        @pl.when(i == dm.NT - 1)
        def _():
            for lb in range(nlb):
                dw_ref[0, 0:CONF_K, lb * LANES:(lb + 1) * LANES] = jnp.sum(dwacc[lb], axis=1)
            dw_ref[0, CONF_K:CONF_K + 1, :] = jnp.zeros((1, dc), F32)

    col = lambda k: pl.BlockSpec((TT, dc), lambda b, i: (ti(b, i), c0 + k))
    halo = lambda k: pl.BlockSpec((CONF_HALO, dc), lambda b, i: (hi(b, i), c0 + k))
    vec = pl.BlockSpec((1, dc), lambda b, i: (0, 0))
    return pl.pallas_call(
        body, name="mix_c_bwd", grid=(dm.BL, dm.NT),
        in_specs=[col(0), col(1), col(2), halo(0), halo(1),
                  pl.BlockSpec((TT, dc), lambda b, i: (ti(b, i), 0)),
                  pl.BlockSpec((CONF_K, dc), lambda b, i: (0, 0)), vec, vec, vec],
        out_specs=[pl.BlockSpec((TT, dm.WC), lambda b, i: (ti(b, i), 0)),
                   pl.BlockSpec((1, 32, dc), lambda b, i: (b, 0, 0)),
                   pl.BlockSpec((1, 8, dc), lambda b, i: (b, 0, 0))],
        out_shape=[jax.ShapeDtypeStruct((dm.R, dm.WC), BF16),
                   jax.ShapeDtypeStruct((dm.BL, 32, dc), F32),
                   jax.ShapeDtypeStruct((dm.BL, 8, dc), F32)],
        scratch_shapes=[pltpu.VMEM((nlb, CONF_HALO + TT, LANES), F32), pltpu.VMEM((nlb, TT, LANES), F32),
                        pltpu.VMEM((nlb, TT + CONF_HALO, LANES), F32), pltpu.VMEM((nlb, TT, LANES), F32),
                        pltpu.VMEM((nlb, SUBLANES - 1, SHIFT_ROWS, LANES), F32),
                        pltpu.VMEM((nlb, SUBLANES - 1, SHIFT_ROWS, LANES), F32),
                        pltpu.VMEM((nlb, CONF_K, SUBLANES, LANES), F32)],
        compiler_params=_params(("parallel", "arbitrary")),
    )(proj, proj, proj, proj, proj, dyc, conv_w, conv_b, ln_g, ln_b)


def _ssm_conv(rbuf, w_ref, b_ref, width):
    for lb in range(width // LANES):
        cs = slice(lb * LANES, (lb + 1) * LANES)
        acc = jnp.broadcast_to(b_ref[0:1, cs], (TT, LANES))
        for k in range(SSM_CONV_K):
            off = SMALL_HALO - (SSM_CONV_K - 1) + k
            acc = acc + w_ref[k:k + 1, cs] * rbuf[off:off + TT, cs]
        yield cs, acc


def _softplus(z):
    return jnp.maximum(z, 0.0) + jnp.log(1.0 + jnp.exp(-jnp.abs(z)))


def _tri(lower):
    r = lax.broadcasted_iota(jnp.int32, (TT, TT), 0)
    c = lax.broadcasted_iota(jnp.int32, (TT, TT), 1)
    return (c <= r) if lower else (c >= r)


def _exact_01_dot(mat01, x):
    x1, x2, x3 = _split3(x)
    return _dot(mat01, x1) + _dot(mat01, x2) + _dot(mat01, x3)


def _head_scalars(dt_ref, dtb_ref, alog_ref):
    z = dt_ref[...] + dtb_ref[...]
    dtv = _softplus(z)
    a = -jnp.exp(alog_ref[...])
    ac = _exact_01_dot(_tri(True).astype(F32).astype(BF16), dtv * a)
    eac = jnp.exp(ac)
    dst = jnp.exp(ac[TT - 1:TT, :] - ac)
    return z, dtv, a, ac, eac, dst


def _decay(ac, ac_t, h, causal):
    seg = ac[:, h:h + 1] - ac_t[h:h + 1, :]
    return jnp.where(causal, jnp.exp(jnp.where(causal, seg, 0.0)), 0.0)


def _pair_mask(h):
    lane = lax.broadcasted_iota(jnp.int32, (1, LANES), 1)
    return ((lane >= SSM_HEAD_DIM) if (h % 2) else (lane < SSM_HEAD_DIM)).astype(F32)


def _mix_b_fwd(proj, projdt, conv_w, conv_b, dt_bias, a_log, dskx, norm_g, expand, dm):
    db, gn, xbc_w, hpg = dm.DB, dm.GN, dm.XBC, dm.HPG
    gw = db // SSM_GROUPS
    ti = _tile_index(dm, False)

    def body(bz_ref, bx_ref, bc_ref, dt_ref, w_ref, b_ref, dtb_ref, alog_ref, dsk_ref, g_ref, e_ref,
             y_ref, yraw_ref, sprev_ref, rbuf, xbuf, state, ybuf, exbuf, xdtbuf):
        i = pl.program_id(1)

        @pl.when(i == 0)
        def _():
            rbuf[0:SMALL_HALO, :] = jnp.zeros((SMALL_HALO, xbc_w), F32)
            state[...] = jnp.zeros_like(state)

        @pl.when(i > 0)
        def _():
            rbuf[0:SMALL_HALO, :] = rbuf[TT:TT + SMALL_HALO, :]

        rbuf[SMALL_HALO:SMALL_HALO + TT, 0:db] = bx_ref[...].astype(F32)
        rbuf[SMALL_HALO:SMALL_HALO + TT, db:xbc_w] = bc_ref[...].astype(F32)
        for cs, pre in _ssm_conv(rbuf, w_ref, b_ref, xbc_w):
            xbuf[:, cs] = pre * _sigmoid(pre)

        _, dtv, _, ac, eac, dst = _head_scalars(dt_ref, dtb_ref, alog_ref)
        exbuf[...] = _dot(jnp.concatenate([dtv, eac, dst], axis=0).astype(BF16), e_ref[...])
        ac_t = ac.T
        causal = _tri(True)
        sprev_ref[0, 0] = state[...]

        xdtbuf[...] = xbuf[:, 0:db] * exbuf[0:TT, :]
        ybuf[...] = xbuf[:, 0:db] * dsk_ref[...]
        for g in range(SSM_GROUPS):
            gs = slice(g * gw, (g + 1) * gw)
            bg = xbuf[:, db + g * SSM_STATE:db + (g + 1) * SSM_STATE].astype(BF16)
            cg = xbuf[:, db + gn + g * SSM_STATE:db + gn + (g + 1) * SSM_STATE].astype(BF16)
            cb = _dot_nt(cg, bg)
            for e in range(0, hpg, 2):
                h = g * hpg + e
                ps = slice(h * SSM_HEAD_DIM, (h + 2) * SSM_HEAD_DIM)
                xp = xdtbuf[:, ps]
                acc = jnp.zeros((TT, LANES), F32)
                for hh in (h, h + 1):
                    mm = (cb * _decay(ac, ac_t, hh, causal)).astype(BF16)
                    acc = acc + _dot(mm, (xp * _pair_mask(hh)).astype(BF16))
                ybuf[:, ps] = ybuf[:, ps] + acc
            sg = state[:, gs]
            ybuf[:, gs] = ybuf[:, gs] + exbuf[TT:2 * TT, gs] * _dot(cg, sg.astype(BF16))
            state[:, gs] = sg * exbuf[2 * TT - 1:2 * TT, gs] + _dot_tn(
                bg, (xdtbuf[:, gs] * exbuf[2 * TT:3 * TT, gs]).astype(BF16))

        yraw = ybuf[...]
        yraw_ref[...] = yraw
        bz = bz_ref[...].astype(F32)
        v = yraw * (bz * _sigmoid(bz))
        r = lax.rsqrt(jnp.mean(v * v, axis=-1, keepdims=True) + NORM_EPS)
        y_ref[...] = (v * r * g_ref[...]).astype(BF16)

    tile = lambda w, k: pl.BlockSpec((TT, w), lambda b, i: (ti(b, i), k))
    fixed = lambda r, w: pl.BlockSpec((r, w), lambda b, i: (0, 0))
    return pl.pallas_call(
        body, name="mix_b_fwd", grid=(dm.BL, dm.NT),
        in_specs=[tile(db, dm.WA // db), tile(db, dm.WA // db + 1), tile(2 * gn, (dm.WA + 2 * db) // (2 * gn)),
                  tile(DT_PAD, 0),
                  fixed(SSM_CONV_K, xbc_w), fixed(1, xbc_w), fixed(1, DT_PAD), fixed(1, DT_PAD),
                  fixed(1, db), fixed(1, db), fixed(DT_PAD, db)],
        out_specs=[pl.BlockSpec((TT, db), lambda b, i: (ti(b, i), 0)),
                   pl.BlockSpec((TT, db), lambda b, i: (ti(b, i), 0)),
                   pl.BlockSpec((1, 1, SSM_STATE, db), lambda b, i: (b, i, 0, 0))],
        out_shape=[jax.ShapeDtypeStruct((dm.R, db), BF16), jax.ShapeDtypeStruct((dm.R, db), F32),
                   jax.ShapeDtypeStruct((dm.BL, dm.NT, SSM_STATE, db), F32)],
        scratch_shapes=[pltpu.VMEM((SMALL_HALO + TT, xbc_w), F32), pltpu.VMEM((TT, xbc_w), F32),
                        pltpu.VMEM((SSM_STATE, db), F32), pltpu.VMEM((TT, db), F32),
                        pltpu.VMEM((3 * TT, db), F32), pltpu.VMEM((TT, db), F32)],
        compiler_params=_params(("parallel", "arbitrary")),
    )(proj, proj, proj, projdt, conv_w, conv_b, dt_bias, a_log, dskx, norm_g, expand)


def _mix_b_bwd(proj, projdt, dyb, yraw, sprev, conv_w, conv_b, dt_bias, a_log, dskx, norm_g, expand, expand_t, dm,
               rider=None):
    db, gn, xbc_w, hpg = dm.DB, dm.GN, dm.XBC, dm.HPG
    gw = db // SSM_GROUPS
    ti = _tile_index(dm, True)
    hi = _halo_index(dm, HALO_BLOCK)

    def body(bz_ref, bx_ref, bc_ref, dt_ref, bxh_ref, bch_ref, dy_ref, yraw_ref, sprev_ref,
             w_ref, b_ref, dtb_ref, alog_ref, dsk_ref, g_ref, e_ref, et_ref,
             dp_ref, dpt_ref, dwc_ref, dch_ref, dhd_ref,
             rbuf, xbuf, dsbuf, dstate, dxbuf, z1buf, dprebuf, exbuf, xdtbuf, dyrbuf, uvec):
        i = pl.program_id(1)
        halo_on = jnp.where(i == dm.NT - 1, 0.0, 1.0)

        @pl.when(i == 0)
        def _():
            dwc_ref[...] = jnp.zeros_like(dwc_ref)
            dch_ref[...] = jnp.zeros_like(dch_ref)
            dhd_ref[...] = jnp.zeros_like(dhd_ref)
            dstate[...] = jnp.zeros_like(dstate)
            dprebuf[TT:TT + SMALL_HALO, :] = jnp.zeros((SMALL_HALO, xbc_w), F32)

        @pl.when(i > 0)
        def _():
            dprebuf[TT:TT + SMALL_HALO, :] = dprebuf[0:SMALL_HALO, :]

        rbuf[0:SMALL_HALO, 0:db] = _last_rows(bxh_ref[...]) * halo_on
        rbuf[0:SMALL_HALO, db:xbc_w] = _last_rows(bch_ref[...]) * halo_on
        rbuf[SMALL_HALO:SMALL_HALO + TT, 0:db] = bx_ref[...].astype(F32)
        rbuf[SMALL_HALO:SMALL_HALO + TT, db:xbc_w] = bc_ref[...].astype(F32)
        for cs, pre in _ssm_conv(rbuf, w_ref, b_ref, xbc_w):
            sl, dsl = _silu_and_grad(pre)
            xbuf[:, cs] = sl
            dsbuf[:, cs] = dsl

        z, dtv, a, ac, eac, dst = _head_scalars(dt_ref, dtb_ref, alog_ref)
        exbuf[...] = _dot(jnp.concatenate([dtv, eac, dst], axis=0).astype(BF16), e_ref[...])
        ac_t = ac.T
        causal = _tri(True)
        xdtbuf[...] = xbuf[:, 0:db] * exbuf[0:TT, :]

        yraw = yraw_ref[...]
        sz, dsz = _silu_and_grad(bz_ref[...].astype(F32))
        v = yraw * sz
        r = lax.rsqrt(jnp.mean(v * v, axis=-1, keepdims=True) + NORM_EPS)
        dy = dy_ref[...]
        dyg = dy * g_ref[...]
        dv = r * dyg - v * (r * r * r * jnp.mean(dyg * v, axis=-1, keepdims=True))
        dch_ref[0, 0:1, :] = dch_ref[0, 0:1, :] + jnp.sum(dy * v * r, axis=0, keepdims=True)
        dyr = dv * sz
        dyrbuf[...] = dyr
        dp_ref[:, 0:db] = (dv * yraw * dsz).astype(BF16)
        dch_ref[0, 1:2, :] = dch_ref[0, 1:2, :] + jnp.sum(dyr * xbuf[:, 0:db], axis=0, keepdims=True)

        lane_row = lax.broadcasted_iota(jnp.int32, (1, LANES), 1)
        sub_col = lax.broadcasted_iota(jnp.int32, (LANES, 1), 0)
        dac = jnp.zeros((TT, LANES), F32)
        colacc = jnp.zeros((LANES, TT), F32)
        for g in range(SSM_GROUPS):
            gs = slice(g * gw, (g + 1) * gw)
            bs_ = slice(db + g * SSM_STATE, db + (g + 1) * SSM_STATE)
            cs_ = slice(db + gn + g * SSM_STATE, db + gn + (g + 1) * SSM_STATE)
            bg = xbuf[:, bs_].astype(BF16)
            cg = xbuf[:, cs_].astype(BF16)
            cb = _dot_nt(cg, bg)
            dcb = jnp.zeros((TT, TT), F32)
            for e in range(0, hpg, 2):
                h = g * hpg + e
                ps = slice(h * SSM_HEAD_DIM, (h + 2) * SSM_HEAD_DIM)
                xp16 = xdtbuf[:, ps].astype(BF16)
                dyp = dyrbuf[:, ps]
                acc = jnp.zeros((TT, LANES), F32)
                for hh in (h, h + 1):
                    dec = _decay(ac, ac_t, hh, causal)
                    mm = cb * dec
                    dyh = (dyp * _pair_mask(hh)).astype(BF16)
                    dmm = _dot_nt(dyh, xp16)
                    acc = acc + _dot_tn(mm.astype(BF16), dyh)
                    dcb = dcb + dmm * dec
                    gm = dmm * mm
                    dac = dac + jnp.sum(gm, axis=1, keepdims=True) * (lane_row == hh).astype(F32)
                    colacc = colacc + (sub_col == hh).astype(F32) * jnp.sum(gm, axis=0, keepdims=True)
                dxbuf[:, ps] = acc
            sg32 = sprev_ref[0, 0, :, gs]
            sg = sg32.astype(BF16)
            dsn = dstate[:, gs]
            dsn16 = dsn.astype(BF16)
            dcb16 = dcb.astype(BF16)
            eacx = exbuf[TT:2 * TT, gs]
            dstx = exbuf[2 * TT:3 * TT, gs]
            cdx = exbuf[2 * TT - 1:2 * TT, gs]
            dye16 = (dyrbuf[:, gs] * eacx).astype(BF16)
            xdt_g = xdtbuf[:, gs]
            dxbuf[:, cs_] = _dot(dcb16, bg) + _dot_nt(dye16, sg)
            dst_x = dstx * _dot(bg, dsn16)
            dxbuf[:, bs_] = _dot_tn(dcb16, cg) + _dot_nt((dstx * xdt_g).astype(BF16), dsn16)
            dstate[:, gs] = cdx * dsn + _dot_tn(cg, dye16)
            z1buf[:, gs] = dyrbuf[:, gs] * (eacx * _dot(cg, sg)) - xdt_g * dst_x
            uvec[:, gs] = jnp.broadcast_to(
                jnp.sum(xdt_g * dst_x, axis=0, keepdims=True) + jnp.sum(dsn * cdx * sg32, axis=0, keepdims=True),
                (8, gw))
            dxbuf[:, gs] = dxbuf[:, gs] + dst_x

        zz = _dot(jnp.concatenate([z1buf[...], dxbuf[:, 0:db] * xbuf[:, 0:db]], axis=0).astype(BF16), et_ref[...])
        u1, u2, u3 = _split3(uvec[...])
        ulast = (_dot(u1, et_ref[...]) + _dot(u2, et_ref[...]) + _dot(u3, et_ref[...]))[0:1, :]
        is_last = (lax.broadcasted_iota(jnp.int32, (TT, 1), 0) == TT - 1).astype(F32)
        dac = dac - colacc.T + zz[0:TT] + is_last * ulast
        dda = _exact_01_dot(_tri(False).astype(F32).astype(BF16), dac)
        ddt = dda * a + zz[TT:2 * TT]
        dhd_ref[0, 1:2, :] = dhd_ref[0, 1:2, :] + jnp.sum(dda * dtv, axis=0, keepdims=True) * a
        ddtraw = ddt * _sigmoid(z)
        dhd_ref[0, 0:1, :] = dhd_ref[0, 0:1, :] + jnp.sum(ddtraw, axis=0, keepdims=True)
        dpt_ref[...] = ddtraw.astype(BF16)
        dxbuf[:, 0:db] = dyrbuf[...] * dsk_ref[...] + dxbuf[:, 0:db] * exbuf[0:TT, :]

        for lb in range(xbc_w // LANES):
            cs = slice(lb * LANES, (lb + 1) * LANES)
            dpre = dxbuf[:, cs] * dsbuf[:, cs]
            dprebuf[0:TT, cs] = dpre
            dwc_ref[0, SSM_CONV_K:SSM_CONV_K + 1, cs] = dwc_ref[0, SSM_CONV_K:SSM_CONV_K + 1, cs] + jnp.sum(
                dpre, axis=0, keepdims=True)
            draw = w_ref[SSM_CONV_K - 1:SSM_CONV_K, cs] * dpre
            for k in range(SSM_CONV_K - 1):
                sh = SSM_CONV_K - 1 - k
                draw = draw + w_ref[k:k + 1, cs] * dprebuf[sh:sh + TT, cs]
            for k in range(SSM_CONV_K):
                off = SMALL_HALO - (SSM_CONV_K - 1) + k
                dwc_ref[0, k:k + 1, cs] = dwc_ref[0, k:k + 1, cs] + jnp.sum(
                    dpre * rbuf[off:off + TT, cs], axis=0, keepdims=True)
            dp_ref[:, db + lb * LANES:db + (lb + 1) * LANES] = draw.astype(BF16)

    tile = lambda w, k: pl.BlockSpec((TT, w), lambda b, i: (ti(b, i), k))
    halo = lambda w, k: pl.BlockSpec((HALO_BLOCK, w), lambda b, i: (hi(b, i), k))
    fixed = lambda r, w: pl.BlockSpec((r, w), lambda b, i: (0, 0))
    kz = dm.WA // db
    kc = (dm.WA + 2 * db) // (2 * gn)
    return _call(
        body, "mix_b_bwd", (dm.BL, dm.NT),
        [tile(db, kz), tile(db, kz + 1), tile(2 * gn, kc), tile(DT_PAD, 0),
         halo(db, kz + 1), halo(2 * gn, kc),
         pl.BlockSpec((TT, db), lambda b, i: (ti(b, i), 0)),
         pl.BlockSpec((TT, db), lambda b, i: (ti(b, i), 0)),
         pl.BlockSpec((1, 1, SSM_STATE, db), lambda b, i: (b, dm.NT - 1 - i, 0, 0)),
         fixed(SSM_CONV_K, xbc_w), fixed(1, xbc_w), fixed(1, DT_PAD), fixed(1, DT_PAD),
         fixed(1, db), fixed(1, db), fixed(DT_PAD, db), fixed(db, DT_PAD)],
        [pl.BlockSpec((TT, dm.WB), lambda b, i: (ti(b, i), 0)),
         pl.BlockSpec((TT, DT_PAD), lambda b, i: (ti(b, i), 0)),
         pl.BlockSpec((1, 8, xbc_w), lambda b, i: (b, 0, 0)),
         pl.BlockSpec((1, 8, db), lambda b, i: (b, 0, 0)),
         pl.BlockSpec((1, 8, DT_PAD), lambda b, i: (b, 0, 0))],
        [jax.ShapeDtypeStruct((dm.R, dm.WB), BF16), jax.ShapeDtypeStruct((dm.R, DT_PAD), BF16),
         jax.ShapeDtypeStruct((dm.BL, 8, xbc_w), F32), jax.ShapeDtypeStruct((dm.BL, 8, db), F32),
         jax.ShapeDtypeStruct((dm.BL, 8, DT_PAD), F32)],
        [pltpu.VMEM((SMALL_HALO + TT, xbc_w), F32), pltpu.VMEM((TT, xbc_w), F32),
         pltpu.VMEM((TT, xbc_w), F32), pltpu.VMEM((SSM_STATE, db), F32),
         pltpu.VMEM((TT, xbc_w), F32), pltpu.VMEM((TT, db), F32),
         pltpu.VMEM((TT + SMALL_HALO, xbc_w), F32), pltpu.VMEM((3 * TT, db), F32),
         pltpu.VMEM((TT, db), F32), pltpu.VMEM((TT, db), F32), pltpu.VMEM((8, db), F32)],
        ("parallel", "arbitrary"),
        (proj, proj, proj, projdt, proj, proj, dyb, yraw, sprev,
         conv_w, conv_b, dt_bias, a_log, dskx, norm_g, expand, expand_t), rider)


def _head_consts(dm):
    head_of = jnp.arange(dm.DB) // SSM_HEAD_DIM
    expand = (jnp.arange(DT_PAD)[:, None] == head_of[None, :]).astype(BF16)
    return expand, expand.T


def _ssm_params(lw, dm):
    pad_h = lambda v: jnp.pad(v, (0, DT_PAD - dm.H))[None]
    return (lw["ssm_conv_w"], lw["ssm_conv_b"][None], pad_h(lw["dt_bias"]), pad_h(lw["a_log"]),
            jnp.repeat(lw["d_skip"], SSM_HEAD_DIM)[None], lw["ssm_norm_g"][None])


def _layer_fwd(h, lw, w_in, w_out, cst, dm, ride_in=None, ride_out=None):
    (proj, projdt, hn), got_in = _fwd_in(h, lw["pre_g"][None], w_in, dm, ride_in)
    ya = _mix_a_fwd(proj, lw["conv_a_w"], dm)
    yb, yraw, sprev = _mix_b_fwd(proj, projdt, *_ssm_params(lw, dm), cst[0], dm)
    yc = _mix_c_fwd(proj, lw["conf_conv_w"], lw["conf_conv_b"][None], lw["conf_ln_g"][None],
                    lw["conf_ln_b"][None], dm)
    (h_new, m), got_out = _fwd_out(ya, yb, yc, w_out, h, lw["post_g"][None], dm,
                                   None if ride_out is None else ride_out(got_in))
    return h_new, (h, hn, proj, projdt, ya, yb, yc, yraw, sprev, m), got_out


def _layer_bwd(dh, saved, lw, w_in, w_out, cst, dm, reduce=None):
    h_in, hn, proj, projdt, ya, yb, yc, yraw, sprev, m = saved
    (dya, dyb, dyc, dwo, dpost), got = _bwd_out(dh, m, lw["post_g"][None], w_out, ya, yb, yc, dm,
                                                None if reduce is None else reduce.swap())
    dpa, dwa = _mix_a_bwd(proj, dya, lw["conv_a_w"], dm)
    (dpb, dpt, dwcv, dch, dhd), got = _mix_b_bwd(proj, projdt, dyb, yraw, sprev, *_ssm_params(lw, dm), cst[0],
                                                 cst[1], dm, None if reduce is None else reduce.to_owners(got))
    dpc, dwcf, dvc = _mix_c_bwd(proj, dyc, lw["conf_conv_w"], lw["conf_conv_b"][None], lw["conf_ln_g"][None],
                                lw["conf_ln_b"][None], dm)
    (dh, dpre), got = _bwd_in_dx(dpa, dpb, dpc, dpt, w_in, h_in, dh, lw["pre_g"][None], dm,
                                 None if reduce is None else reduce.join(got))
    if reduce is not None:
        reduce.finish(got)
    pieces = [_bwd_in_dw(hn, dp, dm, n) for dp, n in ((dpa, "a"), (dpb, "b"), (dpc, "c"), (dpt, "dt"))]
    dwcv, dch, dhd, dvc = (jnp.sum(a, axis=0) for a in (dwcv, dch, dhd, dvc))
    small = dict(pre_g=dpre[0], post_g=dpost[0], conv_a_w=jnp.sum(dwa, axis=0)[:CONV_A_K],
                 ssm_conv_w=dwcv[:SSM_CONV_K], ssm_conv_b=dwcv[SSM_CONV_K], ssm_norm_g=dch[0],
                 d_skip=jnp.sum(dch[1].reshape(dm.H, SSM_HEAD_DIM), axis=1), dt_bias=dhd[0, :dm.H],
                 a_log=dhd[1, :dm.H], conf_conv_w=jnp.sum(dwcf, axis=0)[:CONF_K], conf_conv_b=dvc[0],
                 conf_ln_g=dvc[1], conf_ln_b=dvc[2])
    return dh, pieces, dwo, small


def _shard_runs(dm):
    ab = dm.WA + dm.WB
    order = [(0, 0, ab), (ab, dm.NP - DT_PAD, dm.H), (ab + dm.H, ab, dm.WC)]
    k = dm.NIN // N_CHIPS
    runs = []
    for s in range(N_CHIPS):
        for o0, m0, wd in order:
            lo, hi = max(o0, s * k), min(o0 + wd, (s + 1) * k)
            if lo < hi:
                runs.append((s, lo - s * k, m0 + lo - o0, hi - lo))
    return runs


def _w_in_from_shards(base, dm):
    tr = _row_tile(dm.D, 256)
    k = dm.NIN // N_CHIPS
    runs = _shard_runs(dm)

    def body(in_ref, out_ref):
        for s, sc, mc, wd in runs:
            out_ref[:, mc:mc + wd] = in_ref[s, :, sc:sc + wd]
        out_ref[:, dm.NP - DT_PAD + dm.H:dm.NP] = jnp.zeros((tr, DT_PAD - dm.H), BF16)

    return pl.pallas_call(
        body, name="w_in_from_shards", grid=(dm.D // tr,),
        in_specs=[pl.BlockSpec((N_CHIPS, tr, k), lambda r: (0, r, 0))],
        out_specs=pl.BlockSpec((tr, dm.NP), lambda r: (r, 0)),
        out_shape=jax.ShapeDtypeStruct((dm.D, dm.NP), BF16),
        compiler_params=_params(("parallel",)),
    )(base)


def _grad_to_shards(pieces, dm):
    tr = _row_tile(dm.D, 256)
    k = dm.NIN // N_CHIPS
    starts = [0, dm.WA, dm.WA + dm.WB, dm.NP - DT_PAD]
    widths = [dm.WA, dm.WB, dm.WC, DT_PAD]
    runs = _shard_runs(dm)

    def body(a_ref, b_ref, c_ref, t_ref, out_ref):
        refs = (a_ref, b_ref, c_ref, t_ref)
        for s, sc, mc, wd in runs:
            for p in range(4):
                lo, hi = max(mc, starts[p]), min(mc + wd, starts[p] + widths[p])
                if lo < hi:
                    out_ref[s, :, sc + lo - mc:sc + hi - mc] = refs[p][:, lo - starts[p]:hi - starts[p]].astype(BF16)

    return pl.pallas_call(
        body, name="grad_to_shards", grid=(dm.D // tr,),
        in_specs=[pl.BlockSpec((tr, w), lambda r: (r, 0)) for w in widths],
        out_specs=pl.BlockSpec((N_CHIPS, tr, k), lambda r: (0, r, 0)),
        out_shape=jax.ShapeDtypeStruct((N_CHIPS, dm.D, k), BF16),
        compiler_params=_params(("parallel",)),
    )(*pieces)


def _place_own(w, me):
    rows, cols = w.shape
    tr = _row_tile(rows, 256)

    def body(me_ref, w_ref, out_ref):
        out_ref[0] = w_ref[...].astype(BF16)

    return pl.pallas_call(
        body, name="place_own",
        grid_spec=pltpu.PrefetchScalarGridSpec(
            num_scalar_prefetch=1, grid=(rows // tr,),
            in_specs=[pl.BlockSpec((tr, cols), lambda r, me_ref: (r, 0))],
            out_specs=pl.BlockSpec((1, tr, cols), lambda r, me_ref: (me_ref[0], r, 0))),
        out_shape=jax.ShapeDtypeStruct((N_CHIPS, rows, cols), BF16),
        compiler_params=_params(("parallel",)),
    )(me, w)


def _add_halves(g, got, c, name):
    _, _, rows, cols = g.shape
    tr = _row_tile(rows, 256)

    def body(c_ref, g_ref, got_ref, out_ref):
        out_ref[0] = (g_ref[0, 0].astype(F32) + got_ref[0].astype(F32)).astype(BF16)

    return pl.pallas_call(
        body, name=name,
        grid_spec=pltpu.PrefetchScalarGridSpec(
            num_scalar_prefetch=1, grid=(N_CHIPS, rows // tr),
            in_specs=[pl.BlockSpec((1, 1, tr, cols), lambda s, r, c_ref: (s, c_ref[0], r, 0)),
                      pl.BlockSpec((1, tr, cols), lambda s, r, c_ref: (s, r, 0))],
            out_specs=pl.BlockSpec((1, tr, cols), lambda s, r, c_ref: (s, r, 0))),
        out_shape=jax.ShapeDtypeStruct((N_CHIPS, rows, cols), BF16),
        compiler_params=_params(("parallel", "parallel")),
    )(c, g, got)


def _add_owner(p, got, where, name):
    _, rows, cols = p.shape
    tr = _row_tile(rows, 256)

    def body(w_ref, p_ref, got_ref, out_ref):
        acc = p_ref[0].astype(F32)
        for j in range(3):
            acc = acc + got_ref[j].astype(F32)
        out_ref[0] = acc

    return pl.pallas_call(
        body, name=name,
        grid_spec=pltpu.PrefetchScalarGridSpec(
            num_scalar_prefetch=1, grid=(rows // tr,),
            in_specs=[pl.BlockSpec((1, tr, cols), lambda r, w_ref: (w_ref[0], r, 0)),
                      pl.BlockSpec((3, tr, cols), lambda r, w_ref: (0, r, 0))],
            out_specs=pl.BlockSpec((1, tr, cols), lambda r, w_ref: (w_ref[1], r, 0))),
        out_shape=jax.ShapeDtypeStruct((2, rows, cols), F32),
        compiler_params=_params(("parallel",)),
    )(where, p, got)


class _GradReduce:
    def __init__(self, gs):
        self.gs = [g.reshape((N_CHIPS, 2, g.shape[1] // 2) + g.shape[2:]) for g in gs]
        self.c = lax.axis_index("c").astype(jnp.int32).reshape(1)
        chip = (2 * lax.axis_index("x") + lax.axis_index("y")).astype(jnp.int32)
        self.where = jnp.stack([chip, self.c[0]])
        self.result = None

    def swap(self):
        return _ride_swap_halves(self.gs)

    def to_owners(self, got):
        self.ps = [_add_halves(g, r, self.c, "grad_add_sibling_" + n) for g, r, n in zip(self.gs, got, ("in", "out"))]
        return _ride_to_owners(self.ps)

    def join(self, got):
        qs = [_add_owner(p, r, self.where, "grad_add_chips_" + n) for p, r, n in zip(self.ps, got, ("in", "out"))]
        return _ride_join_halves(qs)

    def finish(self, got):
        self.result = [a.reshape((a.shape[0] * a.shape[1],) + a.shape[2:]) for a in got]

    def run_alone(self):
        got = _exchange("grad_swap_halves", self.swap())
        got = _exchange("grad_to_owners", self.to_owners(got))
        self.finish(_exchange("grad_join_halves", self.join(got)))


def _adamw_math(w, g, m, v):
    m = ADAM_B1 * m + (1.0 - ADAM_B1) * g
    v = ADAM_B2 * v + (1.0 - ADAM_B2) * (g * g)
    m_hat = m / (1.0 - ADAM_B1 ** ADAM_STEP)
    v_hat = v / (1.0 - ADAM_B2 ** ADAM_STEP)
    delta = -ADAM_LR * (m_hat / (jnp.sqrt(v_hat) + ADAM_EPS) + ADAM_WD * w)
    return delta, m, v


def _adamw_small(w, g, m, v, name):
    def body(w_ref, g_ref, m_ref, v_ref, d_out, m_out, v_out):
        d_out[...], m_out[...], v_out[...] = _adamw_math(w_ref[...], g_ref[...], m_ref[...], v_ref[...])

    shape = jax.ShapeDtypeStruct(w.shape, F32)
    return pl.pallas_call(body, name="adamw_" + name, out_shape=[shape, shape, shape],
                          compiler_params=_params())(w, g, m, v)


def _adamw_layer(i, w, g, m, v, prev, name):
    depth, rows, cols = w.shape
    tr = _row_tile(rows, 256)
    n_prev = 0 if prev is None else 4

    def body(*refs):
        w_ref, g_ref, m_ref, v_ref = refs[:4]
        g_out, d_out, m_out, v_out = refs[4 + n_prev:]
        gv = g_ref[...]
        g_out[0] = gv
        d_out[0], m_out[0], v_out[0] = _adamw_math(w_ref[0], gv, m_ref[0], v_ref[0])

    lay = pl.BlockSpec((1, tr, cols), lambda r: (i, r, 0))
    shape = jax.ShapeDtypeStruct(w.shape, F32)
    return pl.pallas_call(
        body, name="adamw_" + name, grid=(rows // tr,),
        in_specs=[lay, pl.BlockSpec((tr, cols), lambda r: (r, 0)), lay, lay] + [ANY] * n_prev,
        out_specs=[lay] * 4, out_shape=[shape] * 4,
        input_output_aliases={4 + k: k for k in range(n_prev)},
        compiler_params=_params(("parallel",)),
    )(w, g, m, v, *(prev or ()))


def _sum_leading(buf, name):
    n, rows, cols = buf.shape
    tr = _row_tile(rows, 512)

    def body(in_ref, out_ref):
        acc = in_ref[0]
        for k in range(1, n):
            acc = acc + in_ref[k]
        out_ref[...] = acc

    return pl.pallas_call(
        body, name=name, grid=(rows // tr,),
        in_specs=[pl.BlockSpec((n, tr, cols), lambda i: (0, i, 0))],
        out_specs=pl.BlockSpec((tr, cols), lambda i: (i, 0)),
        out_shape=jax.ShapeDtypeStruct((rows, cols), F32),
        compiler_params=_params(("parallel",)),
    )(buf)


_SHARDED_SMALL = ("meta", "conv_a_w", "ssm_conv_w", "conf_conv_w")
_LAYER_SMALL = ("pre_g", "post_g", "conv_a_w", "ssm_conv_w", "ssm_conv_b", "dt_bias", "a_log", "d_skip",
                "ssm_norm_g", "conf_conv_w", "conf_conv_b", "conf_ln_g", "conf_ln_b")
_WEIGHTS = ("meta", "pre_g", "post_g", "w_in", "w_out", "conv_a_w", "ssm_conv_w", "ssm_conv_b", "dt_bias", "a_log",
            "d_skip", "ssm_norm_g", "conf_conv_w", "conf_conv_b", "conf_ln_g", "conf_ln_b")


def _unshard_last(a):
    return jnp.moveaxis(a, 0, -2).reshape(a.shape[1:-1] + (N_CHIPS * a.shape[-1],))


def _shard_last(a):
    return jnp.moveaxis(a.reshape(a.shape[:-1] + (N_CHIPS, a.shape[-1] // N_CHIPS)), -2, 0)


def _with_own_block(a, n, at):
    return lax.dynamic_update_index_in_dim(jnp.zeros((n,) + a.shape, a.dtype), a, at, 0)


def kernel(x, meta, pre_g, post_g, w_in, w_out, conv_a_w, ssm_conv_w, ssm_conv_b, dt_bias, a_log, d_skip, ssm_norm_g, conf_conv_w, conf_conv_b, conf_ln_g, conf_ln_b, loss_target, m_meta, m_pre_g, m_post_g, m_w_in, m_w_out, m_conv_a_w, m_ssm_conv_w, m_ssm_conv_b, m_dt_bias, m_a_log, m_d_skip, m_ssm_norm_g, m_conf_conv_w, m_conf_conv_b, m_conf_ln_g, m_conf_ln_b, v_meta, v_pre_g, v_post_g, v_w_in, v_w_out, v_conv_a_w, v_ssm_conv_w, v_ssm_conv_b, v_dt_bias, v_a_log, v_d_skip, v_ssm_norm_g, v_conf_conv_w, v_conf_conv_b, v_conf_ln_g, v_conf_ln_b):
    w = dict(meta=meta, pre_g=pre_g, post_g=post_g, w_in=w_in, w_out=w_out, conv_a_w=conv_a_w,
             ssm_conv_w=ssm_conv_w, ssm_conv_b=ssm_conv_b, dt_bias=dt_bias, a_log=a_log, d_skip=d_skip,
             ssm_norm_g=ssm_norm_g, conf_conv_w=conf_conv_w, conf_conv_b=conf_conv_b, conf_ln_g=conf_ln_g,
             conf_ln_b=conf_ln_b)
    mom = dict(meta=m_meta, pre_g=m_pre_g, post_g=m_post_g, w_in=m_w_in, w_out=m_w_out, conv_a_w=m_conv_a_w,
               ssm_conv_w=m_ssm_conv_w, ssm_conv_b=m_ssm_conv_b, dt_bias=m_dt_bias, a_log=m_a_log, d_skip=m_d_skip,
               ssm_norm_g=m_ssm_norm_g, conf_conv_w=m_conf_conv_w, conf_conv_b=m_conf_conv_b,
               conf_ln_g=m_conf_ln_g, conf_ln_b=m_conf_ln_b)
    vel = dict(meta=v_meta, pre_g=v_pre_g, post_g=v_post_g, w_in=v_w_in, w_out=v_w_out, conv_a_w=v_conv_a_w,
               ssm_conv_w=v_ssm_conv_w, ssm_conv_b=v_ssm_conv_b, dt_bias=v_dt_bias, a_log=v_a_log, d_skip=v_d_skip,
               ssm_norm_g=v_ssm_norm_g, conf_conv_w=v_conf_conv_w, conf_conv_b=v_conf_conv_b,
               conf_ln_g=v_conf_ln_g, conf_ln_b=v_conf_ln_b)
    bl, seq, d = x.shape
    dm = Dims(bl, seq, d)
    depth = w_in.shape[0]
    chip = (2 * lax.axis_index("x") + lax.axis_index("y")).astype(jnp.int32)
    dev = 2 * chip + lax.axis_index("c").astype(jnp.int32)
    cst = _head_consts(dm)

    got = _exchange("gather_small_weights",
                    _ride_gather_small([_with_own_block(w[n], N_CHIPS, chip) for n in _SHARDED_SMALL]))
    full = dict(w)
    for n, a in zip(_SHARDED_SMALL, got):
        full[n] = _unshard_last(a)

    bases = [[_place_own(w_in[i], chip.reshape(1)), _place_own(w_out[i], chip.reshape(1))] for i in range(depth)]
    gathered = _exchange("gather_d2d_first", _ride_gather_d2d(_exchange("gather_ici_first",
                                                                         _ride_gather_ici(bases[0]))))
    h = _embed(x, full["meta"], dm)
    saved, proj_w = [], []
    for i in range(depth):
        lw = {n: full[n][i] for n in _LAYER_SMALL}
        proj_w.append((_w_in_from_shards(gathered[0], dm), gathered[1].reshape(2 * d, d)))
        nxt = i + 1 < depth
        h, keep, gathered = _layer_fwd(h, lw, proj_w[i][0], proj_w[i][1], cst, dm,
                                       _ride_gather_ici(bases[i + 1]) if nxt else None,
                                       _ride_gather_d2d if nxt else None)
        saved.append(keep)

    dh, loss = _loss_head(h, loss_target, dm)
    loss = lax.psum(loss, ("x", "y", "c"))

    small_g = {n: [None] * depth for n in _LAYER_SMALL}
    big = {"w_in": None, "w_out": None}
    reduce = None
    for i in reversed(range(depth)):
        lw = {n: full[n][i] for n in _LAYER_SMALL}
        dh, pieces, dwo, sg = _layer_bwd(dh, saved[i], lw, proj_w[i][0], proj_w[i][1], cst, dm, reduce)
        for n in _LAYER_SMALL:
            small_g[n][i] = sg[n]
        if reduce is not None:
            for n, g in zip(("w_in", "w_out"), reduce.result):
                big[n] = _adamw_layer(i + 1, w[n], g, mom[n], vel[n], big[n], n)
        reduce = _GradReduce([_grad_to_shards(pieces, dm), dwo.reshape(N_CHIPS, 2 * d // N_CHIPS, d)])
    reduce.run_alone()
    for n, g in zip(("w_in", "w_out"), reduce.result):
        big[n] = _adamw_layer(0, w[n], g, mom[n], vel[n], big[n], n)
    grad_x, gmeta = _unembed(dh, dm)

    g = {n: jnp.stack(v) for n, v in small_g.items()}
    g["meta"] = gmeta
    small = [n for n in _WEIGHTS if n not in ("w_in", "w_out")]
    flat = jnp.concatenate([g[n].reshape(-1) for n in small])
    rows = -(-flat.shape[0] // (16 * LANES)) * 16
    flat = jnp.pad(flat, (0, rows * LANES - flat.shape[0])).reshape(rows, LANES)
    parts = _exchange("small_grads_gather_all", _ride_gather_all(_with_own_block(flat, N_DEV, dev)))[0]
    total = _sum_leading(parts, "small_grads_sum").reshape(-1)
    grads, deltas, new_m, new_v = {}, {}, {}, {}
    off = 0
    for n in small:
        size = g[n].size
        fullg = total[off:off + size].reshape(g[n].shape)
        off += size
        if n in _SHARDED_SMALL:
            fullg = lax.dynamic_index_in_dim(_shard_last(fullg), chip, axis=0, keepdims=False)
        grads[n] = fullg
        deltas[n], new_m[n], new_v[n] = _adamw_small(w[n], fullg, mom[n], vel[n], n)
    for n in ("w_in", "w_out"):
        grads[n], deltas[n], new_m[n], new_v[n] = big[n]

    return (loss, grad_x, *[grads[n] for n in _WEIGHTS], *[deltas[n] for n in _WEIGHTS],
            *[new_m[n] for n in _WEIGHTS], *[new_v[n] for n in _WEIGHTS])
```

```python
import jax
import jax.numpy as jnp
from jax import lax
from jax.experimental import pallas as pl
from jax.experimental.pallas import tpu as pltpu

F32 = jnp.float32
BF16 = jnp.bfloat16

N_META = 16
TT = 128
SSM_STATE = 128
SSM_GROUPS = 2
SSM_HEAD_DIM = 64
CONV_A_K = 3
SSM_CONV_K = 4
CONF_K = 31
NORM_EPS = 1e-6
LN_EPS = 1e-5
LANES = 128
MXU_DIM = 256
DT_PAD = LANES
CONF_HALO = 32
SMALL_HALO = 8
VMEM_LIMIT = 56 * 1024 * 1024
N_CHIPS = 4
N_DEV = 8

ADAM_LR = 0.001
ADAM_B1 = 0.9
ADAM_B2 = 0.999
ADAM_EPS = 1e-08
ADAM_WD = 0.01
ADAM_STEP = 10

MESH = pl.DeviceIdType.MESH
ANY = pl.BlockSpec(memory_space=pl.ANY)


class Dims:
    def __init__(self, bl, seq, d):
        self.BL, self.S, self.D = bl, seq, d
        self.L = seq + N_META
        self.Lp = -(-self.L // TT) * TT
        self.NT = self.Lp // TT
        self.R = bl * self.Lp
        self.DA = d // 2
        self.DB = d
        self.DC = d // 2
        self.H = self.DB // SSM_HEAD_DIM
        self.HPG = self.H // SSM_GROUPS
        self.GN = SSM_GROUPS * SSM_STATE
        self.WA = 4 * self.DA
        self.WB = 2 * self.DB + 2 * self.GN
        self.WC = 3 * self.DC
        self.NP = self.WA + self.WB + self.WC + DT_PAD
        self.NIN = self.WA + self.WB + self.H + self.WC
        self.XBC = self.DB + 2 * self.GN
        assert self.H % 2 == 0 and self.HPG % 2 == 0 and self.H <= DT_PAD
        assert self.DA % LANES == 0 and (self.WA + self.WB) % self.DC == 0 and self.WA % self.DB == 0


def _row_tile(n, target):
    best = None
    for t in range(16, min(n, target) + 1, 16):
        if n % t == 0:
            best = t
    assert best is not None
    return best


def _col_tile(n, target):
    best = None
    for t in range(LANES, min(n, target) + 1, LANES):
        if n % t == 0:
            best = t
    assert best is not None
    return best


def _params(sem=None):
    return pltpu.CompilerParams(dimension_semantics=sem, vmem_limit_bytes=VMEM_LIMIT)


def _sigmoid(x):
    return 1.0 / (1.0 + jnp.exp(-x))


def _silu_and_grad(x):
    s = _sigmoid(x)
    return x * s, s * (1.0 + x * (1.0 - s))


def _dot(a, b):
    return jnp.dot(a, b, preferred_element_type=F32)


def _dot_nt(a, b):
    return lax.dot_general(a, b, (((1,), (1,)), ((), ())), preferred_element_type=F32)


def _dot_tn(a, b):
    return lax.dot_general(a, b, (((0,), (0,)), ((), ())), preferred_element_type=F32)


def _split3(x):
    x1 = x.astype(BF16)
    r1 = x - x1.astype(F32)
    x2 = r1.astype(BF16)
    x3 = (r1 - x2.astype(F32)).astype(BF16)
    return x1, x2, x3


class Rider:
    def __init__(self, plan, ins, out_shapes, aliases, nsem):
        self.plan, self.ins, self.out_shapes, self.aliases, self.nsem = plan, list(ins), list(out_shapes), aliases, nsem


def _place():
    x, y, c = lax.axis_index("x"), lax.axis_index("y"), lax.axis_index("c")
    chips = [(1 - x, y), (x, 1 - y), (1 - x, 1 - y)]
    return x, y, c, chips


def _remote(k, src, dst, to, send_sems, recv_sems):
    return pltpu.make_async_remote_copy(src_ref=src, dst_ref=dst, send_sem=send_sems.at[k], recv_sem=recv_sems.at[k],
                                        device_id=to, device_id_type=MESH)


def _call(body, name, grid, in_specs, out_specs, out_shape, scratch_shapes, sem, args, rider=None):
    if rider is None:
        outs = pl.pallas_call(body, name=name, grid=grid, in_specs=in_specs, out_specs=out_specs, out_shape=out_shape,
                              scratch_shapes=scratch_shapes, compiler_params=_params(sem))(*args)
        return list(outs), []
    n_in, n_out, n_scr = len(args), len(out_shape), len(scratch_shapes)
    r_in, r_out = len(rider.ins), len(rider.out_shapes)

    def hosted(*refs):
        ins, rins = refs[:n_in], refs[n_in:n_in + r_in]
        o0 = n_in + r_in
        outs, routs = refs[o0:o0 + n_out], refs[o0 + n_out:o0 + n_out + r_out]
        scr = refs[o0 + n_out + r_out:o0 + n_out + r_out + n_scr]
        send_sems, recv_sems = refs[o0 + n_out + r_out + n_scr:]
        first = pl.program_id(0) == 0
        last = pl.program_id(0) == grid[0] - 1
        for ax in range(1, len(grid)):
            first = jnp.logical_and(first, pl.program_id(ax) == 0)
            last = jnp.logical_and(last, pl.program_id(ax) == grid[ax] - 1)

        @pl.when(first)
        def _():
            starts, _ = rider.plan(rins, routs, send_sems, recv_sems)
            for cp in starts:
                cp.start()

        body(*ins, *outs, *scr)

        @pl.when(last)
        def _():
            _, waits = rider.plan(rins, routs, send_sems, recv_sems)
            for wait in waits:
                wait()

    res = pl.pallas_call(
        hosted, name=name, grid=grid,
        in_specs=list(in_specs) + [ANY] * r_in, out_specs=list(out_specs) + [ANY] * r_out,
        out_shape=list(out_shape) + rider.out_shapes,
        input_output_aliases={n_in + k: n_out + v for k, v in rider.aliases.items()},
        scratch_shapes=list(scratch_shapes) + [pltpu.SemaphoreType.DMA((rider.nsem,)),
                                               pltpu.SemaphoreType.DMA((rider.nsem,))],
        compiler_params=_params(("arbitrary",) * len(grid)),
    )(*args, *rider.ins)
    return list(res[:n_out]), list(res[n_out:])


def _exchange(name, rider):
    r_in, r_out = len(rider.ins), len(rider.out_shapes)

    def body(*refs):
        rins, routs = refs[:r_in], refs[r_in:r_in + r_out]
        send_sems, recv_sems = refs[r_in + r_out:]
        starts, waits = rider.plan(rins, routs, send_sems, recv_sems)
        for cp in starts:
            cp.start()
        for wait in waits:
            wait()

    res = pl.pallas_call(
        body, name=name, in_specs=[ANY] * r_in, out_specs=[ANY] * r_out, out_shape=rider.out_shapes,
        input_output_aliases=dict(rider.aliases),
        scratch_shapes=[pltpu.SemaphoreType.DMA((rider.nsem,)), pltpu.SemaphoreType.DMA((rider.nsem,))],
    )(*rider.ins)
    return list(res)


def _same(arrays):
    return [jax.ShapeDtypeStruct(a.shape, a.dtype) for a in arrays]


class _SemsFrom:
    def __init__(self, sems, first):
        self.sems, self.first = sems, first

    @property
    def at(self):
        return self

    def __getitem__(self, k):
        return self.sems.at[self.first + k]


def _ride_both(r1, r2):
    n_in, n_out = len(r1.ins), len(r1.out_shapes)

    def plan(ins, outs, ss, rs):
        s1, w1 = r1.plan(ins[:n_in], outs[:n_out], ss, rs)
        s2, w2 = r2.plan(ins[n_in:], outs[n_out:], _SemsFrom(ss, r1.nsem), _SemsFrom(rs, r1.nsem))
        return s1 + s2, w1 + w2

    aliases = dict(r1.aliases)
    aliases.update({n_in + k: n_out + v for k, v in r2.aliases.items()})
    return Rider(plan, r1.ins + r2.ins, r1.out_shapes + r2.out_shapes, aliases, r1.nsem + r2.nsem)


def _ride_gather_ici(bases, part=0, nparts=1):
    n = len(bases)

    def plan(ins, outs, ss, rs):
        x, y, c, chips = _place()
        me = 2 * x + y
        starts, waits = [], []
        for a in range(n):
            half = outs[a].shape[1] // 2
            mine = pl.ds(c * half + part * (half // nparts), half // nparts)
            for j, chip in enumerate(chips):
                cp = _remote(3 * a + j, outs[a].at[me, mine], outs[a].at[me, mine], (*chip, c), ss, rs)
                got = outs[a].at[2 * chip[0] + chip[1], mine]
                starts.append(cp)
                waits += [cp.wait_send, _remote(3 * a + j, got, got, (*chip, c), ss, rs).wait_recv]
        return starts, waits

    return Rider(plan, bases, _same(bases), {a: a for a in range(n)}, 3 * n)


def _ride_gather_d2d(bases):
    n = len(bases)

    def plan(ins, outs, ss, rs):
        x, y, c, chips = _place()
        sib = (x, y, 1 - c)
        starts, waits = [], []
        for a in range(n):
            half = outs[a].shape[1] // 2
            for j, chip in enumerate(chips):
                frm = 2 * chip[0] + chip[1]
                got = outs[a].at[frm, pl.ds(c * half, half)]
                theirs = outs[a].at[frm, pl.ds((1 - c) * half, half)]
                cp = _remote(3 * a + j, got, got, sib, ss, rs)
                starts.append(cp)
                waits += [cp.wait_send, _remote(3 * a + j, theirs, theirs, sib, ss, rs).wait_recv]
        return starts, waits

    return Rider(plan, bases, _same(bases), {a: a for a in range(n)}, 3 * n)


def _ride_gather_small(bases):
    n = len(bases)

    def plan(ins, outs, ss, rs):
        x, y, c, chips = _place()
        me = 2 * x + y
        starts, waits = [], []
        for a in range(n):
            for j, chip in enumerate(chips):
                cp = _remote(3 * a + j, outs[a].at[me], outs[a].at[me], (*chip, c), ss, rs)
                got = outs[a].at[2 * chip[0] + chip[1]]
                starts.append(cp)
                waits += [cp.wait_send, _remote(3 * a + j, got, got, (*chip, c), ss, rs).wait_recv]
        return starts, waits

    return Rider(plan, bases, _same(bases), {a: a for a in range(n)}, 3 * n)


def _ride_swap_halves(gs):
    n = len(gs)

    def plan(ins, outs, ss, rs):
        x, y, c, _ = _place()
        cps = [_remote(a, ins[a].at[:, 1 - c], outs[a], (x, y, 1 - c), ss, rs) for a in range(n)]
        return cps, [cp.wait for cp in cps]

    shapes = [jax.ShapeDtypeStruct((g.shape[0],) + g.shape[2:], g.dtype) for g in gs]
    return Rider(plan, gs, shapes, {}, n)


def _ride_to_owners(ps):
    n = len(ps)

    def plan(ins, outs, ss, rs):
        x, y, c, chips = _place()
        cps = []
        for a in range(n):
            for j, chip in enumerate(chips):
                cps.append(_remote(3 * a + j, ins[a].at[2 * chip[0] + chip[1]], outs[a].at[j], (*chip, c), ss, rs))
        return cps, [cp.wait for cp in cps]

    shapes = [jax.ShapeDtypeStruct((3,) + p.shape[1:], p.dtype) for p in ps]
    return Rider(plan, ps, shapes, {}, 3 * n)


def _ride_join_halves(qs):
    n = len(qs)

    def plan(ins, outs, ss, rs):
        x, y, c, _ = _place()
        sib = (x, y, 1 - c)
        starts, waits = [], []
        for a in range(n):
            cp = _remote(a, outs[a].at[c], outs[a].at[c], sib, ss, rs)
            starts.append(cp)
            waits += [cp.wait_send, _remote(a, outs[a].at[1 - c], outs[a].at[1 - c], sib, ss, rs).wait_recv]
        return starts, waits

    return Rider(plan, qs, _same(qs), {a: a for a in range(n)}, n)


def _ride_gather_all(base):
    def plan(ins, outs, ss, rs):
        x, y, c, _ = _place()
        me = 4 * x + 2 * y + c

        def peer(k):
            return (1 - x if (k >> 2) & 1 else x, 1 - y if (k >> 1) & 1 else y, 1 - c if k & 1 else c)

        starts, waits = [], []
        for k in range(1, N_DEV):
            px, py, pc = peer(k)
            cp = _remote(k - 1, outs[0].at[me], outs[0].at[me], (px, py, pc), ss, rs)
            got = outs[0].at[4 * px + 2 * py + pc]
            starts.append(cp)
            waits += [cp.wait_send, _remote(k - 1, got, got, (px, py, pc), ss, rs).wait_recv]
        return starts, waits

    return Rider(plan, [base], _same([base]), {0: 0}, N_DEV - 1)


def _embed(x, meta, dm):
    dc = _col_tile(dm.D, 256)
    s, lp = dm.S, dm.Lp

    def body(x_ref, meta_ref, h_ref):
        h_ref[0:N_META, :] = meta_ref[...]
        h_ref[N_META:N_META + s, :] = x_ref[0]
        if lp > N_META + s:
            h_ref[N_META + s:lp, :] = jnp.zeros((lp - N_META - s, dc), F32)

    return pl.pallas_call(
        body, name="embed", grid=(dm.BL, dm.D // dc),
        in_specs=[pl.BlockSpec((1, s, dc), lambda b, j: (b, 0, j)),
                  pl.BlockSpec((N_META, dc), lambda b, j: (0, j))],
        out_specs=pl.BlockSpec((lp, dc), lambda b, j: (b, j)),
        out_shape=jax.ShapeDtypeStruct((dm.R, dm.D), F32),
        compiler_params=_params(("parallel", "parallel")),
    )(x, meta)


def _loss_head(h, target, dm):
    dc = _col_tile(dm.D, 256)
    s, lp, nj = dm.S, dm.Lp, dm.D // dc

    def body(h_ref, t_ref, dh_ref, l_ref):
        diff = h_ref[N_META:N_META + s, :] - t_ref[0]
        dh_ref[0:N_META, :] = jnp.zeros((N_META, dc), F32)
        dh_ref[N_META:N_META + s, :] = diff * (1.0 / dm.D)
        if lp > N_META + s:
            dh_ref[N_META + s:lp, :] = jnp.zeros((lp - N_META - s, dc), F32)
        l_ref[...] = jnp.full((8, LANES), (0.5 / dm.D) * jnp.sum(diff * diff), F32)

    dh, part = pl.pallas_call(
        body, name="loss_head", grid=(dm.BL, nj),
        in_specs=[pl.BlockSpec((lp, dc), lambda b, j: (b, j)),
                  pl.BlockSpec((1, s, dc), lambda b, j: (b, 0, j))],
        out_specs=[pl.BlockSpec((lp, dc), lambda b, j: (b, j)),
                   pl.BlockSpec((8, LANES), lambda b, j: (b * nj + j, 0))],
        out_shape=[jax.ShapeDtypeStruct((dm.R, dm.D), F32),
                   jax.ShapeDtypeStruct((dm.BL * nj * 8, LANES), F32)],
        compiler_params=_params(("parallel", "parallel")),
    )(h, target)
    return dh, jnp.sum(part[::8, 0])


def _unembed(dh, dm):
    dc = _col_tile(dm.D, 256)
    s, lp = dm.S, dm.Lp

    def body(dh_ref, gx_ref, gm_ref):
        gx_ref[0] = dh_ref[N_META:N_META + s, :]

        @pl.when(pl.program_id(1) == 0)
        def _():
            gm_ref[...] = dh_ref[0:N_META, :]

        @pl.when(pl.program_id(1) > 0)
        def _():
            gm_ref[...] = gm_ref[...] + dh_ref[0:N_META, :]

    return pl.pallas_call(
        body, name="unembed", grid=(dm.D // dc, dm.BL),
        in_specs=[pl.BlockSpec((lp, dc), lambda j, b: (b, j))],
        out_specs=[pl.BlockSpec((1, s, dc), lambda j, b: (b, 0, j)),
                   pl.BlockSpec((N_META, dc), lambda j, b: (0, j))],
        out_shape=[jax.ShapeDtypeStruct((dm.BL, s, dm.D), F32),
                   jax.ShapeDtypeStruct((N_META, dm.D), F32)],
        compiler_params=_params(("parallel", "arbitrary")),
    )(dh)


def _fwd_in(h, pre_g, w, dm, rider=None):
    tm = _row_tile(dm.R, 1088)
    tn = _col_tile(dm.NP, 896)
    nj = dm.NP // tn

    def body(h_ref, g_ref, w_ref, proj_ref, dt_ref, hn_ref):
        @pl.when(pl.program_id(1) == 0)
        def _():
            xf = h_ref[...]
            r = lax.rsqrt(jnp.mean(xf * xf, axis=-1, keepdims=True) + NORM_EPS)
            hn_ref[...] = (xf * r * g_ref[...]).astype(BF16)

        res = _dot(hn_ref[...], w_ref[...])
        proj_ref[...] = res.astype(BF16)

        @pl.when(pl.program_id(1) == nj - 1)
        def _():
            dt_ref[...] = res[:, tn - DT_PAD:tn]

    return _call(
        body, "fwd_in", (dm.R // tm, nj),
        [pl.BlockSpec((tm, dm.D), lambda i, j: (i, 0)),
         pl.BlockSpec((1, dm.D), lambda i, j: (0, 0)),
         pl.BlockSpec((dm.D, tn), lambda i, j: (0, j))],
        [pl.BlockSpec((tm, tn), lambda i, j: (i, j)),
         pl.BlockSpec((tm, DT_PAD), lambda i, j: (i, 0)),
         pl.BlockSpec((tm, dm.D), lambda i, j: (i, 0))],
        [jax.ShapeDtypeStruct((dm.R, dm.NP), BF16), jax.ShapeDtypeStruct((dm.R, DT_PAD), F32),
         jax.ShapeDtypeStruct((dm.R, dm.D), BF16)],
        [], ("parallel", "arbitrary"), (h, pre_g, w), rider)


def _fwd_out(ya, yb, yc, w_out, h, post_g, dm, rider=None):
    tm = _row_tile(dm.Lp, 544)
    tiles_per_seq = dm.Lp // tm
    da, db, dc = dm.DA, dm.DB, dm.DC

    def body(ya_ref, yb_ref, yc_ref, w_ref, h_ref, g_ref, hn_ref, m_ref):
        m = _dot(ya_ref[...], w_ref[0:da, :])
        m = m + _dot(yb_ref[...], w_ref[da:da + db, :])
        m = m + _dot(yc_ref[...], w_ref[da + db:da + db + dc, :])
        m_ref[...] = m
        r = lax.rsqrt(jnp.mean(m * m, axis=-1, keepdims=True) + NORM_EPS)
        t = (pl.program_id(0) % tiles_per_seq) * tm + lax.broadcasted_iota(jnp.int32, (tm, 1), 0)
        keep = (t < dm.L).astype(F32)
        hn_ref[...] = (h_ref[...] + m * r * g_ref[...]) * keep

    row = lambda i: (i, 0)
    fixed = lambda i: (0, 0)
    return _call(
        body, "fwd_out", (dm.R // tm,),
        [pl.BlockSpec((tm, da), row), pl.BlockSpec((tm, db), row), pl.BlockSpec((tm, dc), row),
         pl.BlockSpec((2 * dm.D, dm.D), fixed), pl.BlockSpec((tm, dm.D), row), pl.BlockSpec((1, dm.D), fixed)],
        [pl.BlockSpec((tm, dm.D), row), pl.BlockSpec((tm, dm.D), row)],
        [jax.ShapeDtypeStruct((dm.R, dm.D), F32), jax.ShapeDtypeStruct((dm.R, dm.D), F32)],
        [], ("parallel",), (ya, yb, yc, w_out, h, post_g), rider)


def _bwd_out(dh, m, post_g, w_out, ya, yb, yc, dm, rider=None):
    tm = _row_tile(dm.R, MXU_DIM)
    da, db, dc = dm.DA, dm.DB, dm.DC

    def body(dh_ref, m_ref, g_ref, w_ref, ya_ref, yb_ref, yc_ref, dya_ref, dyb_ref, dyc_ref, dw_ref, dg_ref):
        @pl.when(pl.program_id(0) == 0)
        def _():
            dw_ref[...] = jnp.zeros_like(dw_ref)
            dg_ref[...] = jnp.zeros_like(dg_ref)

        m = m_ref[...]
        dh_ = dh_ref[...]
        r = lax.rsqrt(jnp.mean(m * m, axis=-1, keepdims=True) + NORM_EPS)
        n = m * r
        dg_ref[0:1, :] = dg_ref[0:1, :] + jnp.sum(dh_ * n, axis=0, keepdims=True)
        dn = dh_ * g_ref[...]
        dm_ = (r * (dn - n * jnp.mean(dn * n, axis=-1, keepdims=True))).astype(BF16)
        dya_ref[...] = _dot_nt(dm_, w_ref[0:da, :])
        dyb_ref[...] = _dot_nt(dm_, w_ref[da:da + db, :])
        dyc_ref[...] = _dot_nt(dm_, w_ref[da + db:da + db + dc, :])
        dw_ref[0:da, :] = dw_ref[0:da, :] + _dot_tn(ya_ref[...], dm_)
        dw_ref[da:da + db, :] = dw_ref[da:da + db, :] + _dot_tn(yb_ref[...], dm_)
        dw_ref[da + db:da + db + dc, :] = dw_ref[da + db:da + db + dc, :] + _dot_tn(yc_ref[...], dm_)

    row = lambda i: (i, 0)
    fixed = lambda i: (0, 0)
    return _call(
        body, "bwd_out", (dm.R // tm,),
        [pl.BlockSpec((tm, dm.D), row), pl.BlockSpec((tm, dm.D), row), pl.BlockSpec((1, dm.D), fixed),
         pl.BlockSpec((2 * dm.D, dm.D), fixed),
         pl.BlockSpec((tm, da), row), pl.BlockSpec((tm, db), row), pl.BlockSpec((tm, dc), row)],
        [pl.BlockSpec((tm, da), row), pl.BlockSpec((tm, db), row), pl.BlockSpec((tm, dc), row),
         pl.BlockSpec((2 * dm.D, dm.D), fixed), pl.BlockSpec((8, dm.D), fixed)],
        [jax.ShapeDtypeStruct((dm.R, da), F32), jax.ShapeDtypeStruct((dm.R, db), F32),
         jax.ShapeDtypeStruct((dm.R, dc), F32),
         jax.ShapeDtypeStruct((2 * dm.D, dm.D), F32), jax.ShapeDtypeStruct((8, dm.D), F32)],
        [], ("arbitrary",), (dh, m, post_g, w_out, ya, yb, yc), rider)


def _bwd_in_dx(dpa, dpb, dpc, dpt, w, h, dh, pre_g, dm, rider=None):
    tm = _row_tile(dm.R, 272)
    wa, wb, wc = dm.WA, dm.WB, dm.WC

    def body(dpa_ref, dpb_ref, dpc_ref, dpt_ref, w_ref, h_ref, dh_ref, g_ref, out_ref, dg_ref):
        @pl.when(pl.program_id(0) == 0)
        def _():
            dg_ref[...] = jnp.zeros_like(dg_ref)

        dhn = _dot_nt(dpa_ref[...], w_ref[:, 0:wa])
        dhn = dhn + _dot_nt(dpb_ref[...], w_ref[:, wa:wa + wb])
        dhn = dhn + _dot_nt(dpc_ref[...], w_ref[:, wa + wb:wa + wb + wc])
        dhn = dhn + _dot_nt(dpt_ref[...], w_ref[:, wa + wb + wc:wa + wb + wc + DT_PAD])
        xf = h_ref[...]
        r = lax.rsqrt(jnp.mean(xf * xf, axis=-1, keepdims=True) + NORM_EPS)
        n = xf * r
        dg_ref[0:1, :] = dg_ref[0:1, :] + jnp.sum(dhn * n, axis=0, keepdims=True)
        dn = dhn * g_ref[...]
        out_ref[...] = dh_ref[...] + r * (dn - n * jnp.mean(dn * n, axis=-1, keepdims=True))

    row = lambda i: (i, 0)
    fixed = lambda i: (0, 0)
    return _call(
        body, "bwd_in_dx", (dm.R // tm,),
        [pl.BlockSpec((tm, wa), row), pl.BlockSpec((tm, wb), row), pl.BlockSpec((tm, wc), row),
         pl.BlockSpec((tm, DT_PAD), row), pl.BlockSpec((dm.D, dm.NP), fixed),
         pl.BlockSpec((tm, dm.D), row), pl.BlockSpec((tm, dm.D), row), pl.BlockSpec((1, dm.D), fixed)],
        [pl.BlockSpec((tm, dm.D), row), pl.BlockSpec((8, dm.D), fixed)],
        [jax.ShapeDtypeStruct((dm.R, dm.D), F32), jax.ShapeDtypeStruct((8, dm.D), F32)],
        [], ("arbitrary",), (dpa, dpb, dpc, dpt, w, h, dh, pre_g), rider)


def _bwd_in_dw(hn, dp, dm, piece):
    width = dp.shape[1]
    tn = _col_tile(width, 512)

    def body(hn_ref, dp_ref, dw_ref):
        dw_ref[...] = _dot_tn(hn_ref[...], dp_ref[...])

    return pl.pallas_call(
        body, name="bwd_in_dw_" + piece, grid=(width // tn,),
        in_specs=[pl.BlockSpec((dm.R, dm.D), lambda j: (0, 0)), pl.BlockSpec((dm.R, tn), lambda j: (0, j))],
        out_specs=pl.BlockSpec((dm.D, tn), lambda j: (0, j)),
        out_shape=jax.ShapeDtypeStruct((dm.D, width), F32),
        compiler_params=_params(("parallel",)),
    )(hn, dp)


def _tile_index(dm, reverse):
    if reverse:
        return lambda b, i: b * dm.NT + (dm.NT - 1 - i)
    return lambda b, i: b * dm.NT + i


def _halo_index(dm, rows):
    per_tile = TT // rows
    return lambda b, i: jnp.maximum((b * dm.NT + (dm.NT - 1 - i)) * per_tile - 1, 0)


HALO_BLOCK = 16


def _last_rows(x):
    return x.astype(F32)[HALO_BLOCK - SMALL_HALO:HALO_BLOCK]


def _mix_a_fwd(proj, conv_w, dm):
    da = dm.DA
    ti = _tile_index(dm, False)

    def body(ab_ref, ac_ref, ax_ref, az_ref, w_ref, y_ref, pbuf):
        i = pl.program_id(1)

        @pl.when(i == 0)
        def _():
            pbuf[0:SMALL_HALO, :] = jnp.zeros((SMALL_HALO, da), F32)

        @pl.when(i > 0)
        def _():
            pbuf[0:SMALL_HALO, :] = pbuf[TT:TT + SMALL_HALO, :]

        for lb in range(da // LANES):
            cs = slice(lb * LANES, (lb + 1) * LANES)
            p = ac_ref[:, cs].astype(F32) * ax_ref[:, cs].astype(F32)
            pbuf[SMALL_HALO:SMALL_HALO + TT, cs] = p
            q = (w_ref[0:1, cs] * pbuf[6:6 + TT, cs] + w_ref[1:2, cs] * pbuf[7:7 + TT, cs] + w_ref[2:3, cs] * p)
            az = az_ref[:, cs].astype(F32)
            y_ref[:, cs] = (ab_ref[:, cs].astype(F32) * q * (az * _sigmoid(az))).astype(BF16)

    col = lambda k: pl.BlockSpec((TT, da), lambda b, i: (ti(b, i), k))
    return pl.pallas_call(
        body, name="mix_a_fwd", grid=(dm.BL, dm.NT),
        in_specs=[col(0), col(1), col(2), col(3), pl.BlockSpec((CONV_A_K, da), lambda b, i: (0, 0))],
        out_specs=pl.BlockSpec((TT, da), lambda b, i: (ti(b, i), 0)),
        out_shape=jax.ShapeDtypeStruct((dm.R, da), BF16),
        scratch_shapes=[pltpu.VMEM((SMALL_HALO + TT, da), F32)],
        compiler_params=_params(("parallel", "arbitrary")),
    )(proj, proj, proj, proj, conv_w)


def _mix_a_bwd(proj, dya, conv_w, dm):
    da = dm.DA
    ti = _tile_index(dm, True)
    hi = _halo_index(dm, HALO_BLOCK)

    def body(ab_ref, ac_ref, ax_ref, az_ref, ach_ref, axh_ref, dy_ref, w_ref, dp_ref, dw_ref, pbuf, dqbuf):
        i = pl.program_id(1)
        halo_on = jnp.where(i == dm.NT - 1, 0.0, 1.0)

        @pl.when(i == 0)
        def _():
            dw_ref[...] = jnp.zeros_like(dw_ref)
            dqbuf[TT:TT + SMALL_HALO, :] = jnp.zeros((SMALL_HALO, da), F32)

        @pl.when(i > 0)
        def _():
            dqbuf[TT:TT + SMALL_HALO, :] = dqbuf[0:SMALL_HALO, :]

        for lb in range(da // LANES):
            cs = slice(lb * LANES, (lb + 1) * LANES)
            pbuf[0:SMALL_HALO, cs] = (_last_rows(ach_ref[:, cs]) * _last_rows(axh_ref[:, cs])) * halo_on
            ac, ax, ab, az = (r[:, cs].astype(F32) for r in (ac_ref, ax_ref, ab_ref, az_ref))
            p = ac * ax
            pbuf[SMALL_HALO:SMALL_HALO + TT, cs] = p
            p1 = pbuf[7:7 + TT, cs]
            p2 = pbuf[6:6 + TT, cs]
            w0, w1, w2 = w_ref[0:1, cs], w_ref[1:2, cs], w_ref[2:3, cs]
            q = w0 * p2 + w1 * p1 + w2 * p
            sz, dsz = _silu_and_grad(az)
            dy = dy_ref[:, cs]
            t1 = dy * ab
            dq = t1 * sz
            dqbuf[0:TT, cs] = dq
            dpv = w2 * dq + w1 * dqbuf[1:1 + TT, cs] + w0 * dqbuf[2:2 + TT, cs]
            dp_ref[:, lb * LANES:(lb + 1) * LANES] = (dy * q * sz).astype(BF16)
            dp_ref[:, da + lb * LANES:da + (lb + 1) * LANES] = (dpv * ax).astype(BF16)
            dp_ref[:, 2 * da + lb * LANES:2 * da + (lb + 1) * LANES] = (dpv * ac).astype(BF16)
            dp_ref[:, 3 * da + lb * LANES:3 * da + (lb + 1) * LANES] = (t1 * q * dsz).astype(BF16)
            dw_ref[0, 0:1, cs] = dw_ref[0, 0:1, cs] + jnp.sum(dq * p2, axis=0, keepdims=True)
            dw_ref[0, 1:2, cs] = dw_ref[0, 1:2, cs] + jnp.sum(dq * p1, axis=0, keepdims=True)
            dw_ref[0, 2:3, cs] = dw_ref[0, 2:3, cs] + jnp.sum(dq * p, axis=0, keepdims=True)

    col = lambda k: pl.BlockSpec((TT, da), lambda b, i: (ti(b, i), k))
    halo = lambda k: pl.BlockSpec((HALO_BLOCK, da), lambda b, i: (hi(b, i), k))
    return pl.pallas_call(
        body, name="mix_a_bwd", grid=(dm.BL, dm.NT),
        in_specs=[col(0), col(1), col(2), col(3), halo(1), halo(2),
                  pl.BlockSpec((TT, da), lambda b, i: (ti(b, i), 0)),
                  pl.BlockSpec((CONV_A_K, da), lambda b, i: (0, 0))],
        out_specs=[pl.BlockSpec((TT, dm.WA), lambda b, i: (ti(b, i), 0)),
                   pl.BlockSpec((1, 8, da), lambda b, i: (b, 0, 0))],
        out_shape=[jax.ShapeDtypeStruct((dm.R, dm.WA), BF16), jax.ShapeDtypeStruct((dm.BL, 8, da), F32)],
        scratch_shapes=[pltpu.VMEM((SMALL_HALO + TT, da), F32), pltpu.VMEM((TT + SMALL_HALO, da), F32)],
        compiler_params=_params(("parallel", "arbitrary")),
    )(proj, proj, proj, proj, proj, proj, dya, conv_w)


SUBLANES = 8
SHIFT_ROWS = TT + CONF_HALO - SUBLANES


TAP_ROWS = 64


def _split_lanes(buf, rows, val):
    for lb in range(val.shape[1] // LANES):
        buf[lb, rows, :] = val[:, lb * LANES:(lb + 1) * LANES]


def _join_lanes(buf):
    return jnp.concatenate([buf[lb] for lb in range(buf.shape[0])], axis=1)


def _fill_shifted(buf, shifted):
    def step(lb, carry):
        for r in range(1, SUBLANES):
            shifted[lb, r - 1, 0:SHIFT_ROWS, :] = buf[lb, r:r + SHIFT_ROWS, :]
        return carry

    lax.fori_loop(0, buf.shape[0], step, 0)


def _window(buf, shifted, d, r0, lb):
    r = d % SUBLANES
    rows = pl.ds(pl.multiple_of(r0 + (d - r), SUBLANES), TAP_ROWS)
    return buf[lb, rows, :] if r == 0 else shifted[lb, r - 1, rows, :]


def _tap_loop(nlb, body):
    per_lb = TT // TAP_ROWS

    def step(it, carry):
        lb = it // per_lb
        body(lb, pl.ds(pl.multiple_of(lb * LANES, LANES), LANES), pl.multiple_of((it % per_lb) * TAP_ROWS, TAP_ROWS))
        return carry

    lax.fori_loop(0, nlb * per_lb, step, 0)


TAP_CHAINS = 4


def _tree_sum(terms):
    sums = list(terms[:TAP_CHAINS])
    for n, t in enumerate(terms[TAP_CHAINS:]):
        sums[n % TAP_CHAINS] = sums[n % TAP_CHAINS] + t
    while len(sums) > 1:
        sums = [a + b for a, b in zip(sums[0::2], sums[1::2])] + ([sums[-1]] if len(sums) % 2 else [])
    return sums[0]


def _conf_conv(ubuf, ushift, w_ref, b_ref, u1buf):
    _fill_shifted(ubuf, ushift)

    def piece(lb, cs, r0):
        taps = [w_ref[k:k + 1, cs] * _window(ubuf, ushift, CONF_HALO - (CONF_K - 1) + k, r0, lb)
                for k in range(CONF_K)]
        u1buf[lb, pl.ds(r0, TAP_ROWS), :] = _tree_sum(taps) + b_ref[0:1, cs]

    _tap_loop(ubuf.shape[0], piece)


def _mix_c_fwd(proj, conv_w, conv_b, ln_g, ln_b, dm):
    dc = dm.DC
    nlb = dc // LANES
    c0 = (dm.WA + dm.WB) // dc
    ti = _tile_index(dm, False)

    def body(ca_ref, cg_ref, cz_ref, w_ref, b_ref, g_ref, be_ref, y_ref, ubuf, u1buf, ushift):
        i = pl.program_id(1)

        @pl.when(i == 0)
        def _():
            ubuf[:, 0:CONF_HALO, :] = jnp.zeros((nlb, CONF_HALO, LANES), F32)

        @pl.when(i > 0)
        def _():
            ubuf[:, 0:CONF_HALO, :] = ubuf[:, TT:TT + CONF_HALO, :]

        _split_lanes(ubuf, slice(CONF_HALO, CONF_HALO + TT),
                     ca_ref[...].astype(F32) * _sigmoid(cg_ref[...].astype(F32)))
        _conf_conv(ubuf, ushift, w_ref, b_ref, u1buf)
        u1 = _join_lanes(u1buf)
        mu = jnp.mean(u1, axis=-1, keepdims=True)
        xc = u1 - mu
        rstd = lax.rsqrt(jnp.mean(xc * xc, axis=-1, keepdims=True) + LN_EPS)
        u2 = xc * rstd * g_ref[...] + be_ref[...]
        cz = cz_ref[...].astype(F32)
        y_ref[...] = ((u2 * _sigmoid(u2)) * (cz * _sigmoid(cz))).astype(BF16)

    col = lambda k: pl.BlockSpec((TT, dc), lambda b, i: (ti(b, i), c0 + k))
    vec = pl.BlockSpec((1, dc), lambda b, i: (0, 0))
    return pl.pallas_call(
        body, name="mix_c_fwd", grid=(dm.BL, dm.NT),
        in_specs=[col(0), col(1), col(2), pl.BlockSpec((CONF_K, dc), lambda b, i: (0, 0)), vec, vec, vec],
        out_specs=pl.BlockSpec((TT, dc), lambda b, i: (ti(b, i), 0)),
        out_shape=jax.ShapeDtypeStruct((dm.R, dc), BF16),
        scratch_shapes=[pltpu.VMEM((nlb, CONF_HALO + TT, LANES), F32), pltpu.VMEM((nlb, TT, LANES), F32),
                        pltpu.VMEM((nlb, SUBLANES - 1, SHIFT_ROWS, LANES), F32)],
        compiler_params=_params(("parallel", "arbitrary")),
    )(proj, proj, proj, conv_w, conv_b, ln_g, ln_b)


def _mix_c_bwd(proj, dyc, conv_w, conv_b, ln_g, ln_b, dm):
    dc = dm.DC
    nlb = dc // LANES
    c0 = (dm.WA + dm.WB) // dc
    ti = _tile_index(dm, True)
    hi = _halo_index(dm, CONF_HALO)

    def body(ca_ref, cg_ref, cz_ref, cah_ref, cgh_ref, dy_ref, w_ref, b_ref, g_ref, be_ref,
             dp_ref, dw_ref, dv_ref, ubuf, u1buf, dubuf, du0buf, ushift, dshift, dwacc):
        i = pl.program_id(1)
        halo_on = jnp.where(i == dm.NT - 1, 0.0, 1.0)

        @pl.when(i == 0)
        def _():
            dwacc[...] = jnp.zeros_like(dwacc)
            dv_ref[...] = jnp.zeros_like(dv_ref)
            dubuf[:, TT:TT + CONF_HALO, :] = jnp.zeros((nlb, CONF_HALO, LANES), F32)

        @pl.when(i > 0)
        def _():
            dubuf[:, TT:TT + CONF_HALO, :] = dubuf[:, 0:CONF_HALO, :]

        _split_lanes(ubuf, slice(0, CONF_HALO),
                     cah_ref[...].astype(F32) * _sigmoid(cgh_ref[...].astype(F32)) * halo_on)
        sgg = _sigmoid(cg_ref[...].astype(F32))
        ca = ca_ref[...].astype(F32)
        _split_lanes(ubuf, slice(CONF_HALO, CONF_HALO + TT), ca * sgg)
        _conf_conv(ubuf, ushift, w_ref, b_ref, u1buf)
        u1 = _join_lanes(u1buf)
        mu = jnp.mean(u1, axis=-1, keepdims=True)
        xc = u1 - mu
        rstd = lax.rsqrt(jnp.mean(xc * xc, axis=-1, keepdims=True) + LN_EPS)
        xhat = xc * rstd
        u2 = xhat * g_ref[...] + be_ref[...]
        su, dsu = _silu_and_grad(u2)
        sz, dsz = _silu_and_grad(cz_ref[...].astype(F32))
        dy = dy_ref[...]
        du2 = dy * dsu * sz
        dp_ref[:, 2 * dc:3 * dc] = (dy * su * dsz).astype(BF16)
        dxhat = du2 * g_ref[...]
        du1 = rstd * (dxhat - jnp.mean(dxhat, axis=-1, keepdims=True)
                      - xhat * jnp.mean(dxhat * xhat, axis=-1, keepdims=True))
        dv_ref[0, 0:1, :] = dv_ref[0, 0:1, :] + jnp.sum(du1, axis=0, keepdims=True)
        dv_ref[0, 1:2, :] = dv_ref[0, 1:2, :] + jnp.sum(du2 * xhat, axis=0, keepdims=True)
        dv_ref[0, 2:3, :] = dv_ref[0, 2:3, :] + jnp.sum(du2, axis=0, keepdims=True)
        _split_lanes(dubuf, slice(0, TT), du1)
        _fill_shifted(dubuf, dshift)

        def piece(lb, cs, r0):
            du0buf[lb, pl.ds(r0, TAP_ROWS), :] = _tree_sum(
                [w_ref[k:k + 1, cs] * _window(dubuf, dshift, CONF_K - 1 - k, r0, lb) for k in range(CONF_K)])
            d1 = dubuf[lb, pl.ds(r0, TAP_ROWS), :]
            for k in range(CONF_K):
                prod = d1 * _window(ubuf, ushift, CONF_HALO - (CONF_K - 1) + k, r0, lb)
                dwacc[lb, k] = dwacc[lb, k] + jnp.sum(prod.reshape(TAP_ROWS // SUBLANES, SUBLANES, LANES), axis=0)

        _tap_loop(nlb, piece)
        du0 = _join_lanes(du0buf)
        dp_ref[:, 0:dc] = (du0 * sgg).astype(BF16)
        dp_ref[:, dc:2 * dc] = (du0 * ca * sgg * (1.0 - sgg)).astype(BF16)

        @pl.when(i == dm.NT - 1)
        def _():
            for lb in range(nlb):
                dw_ref[0, 0:CONF_K, lb * LANES:(lb + 1) * LANES] = jnp.sum(dwacc[lb], axis=1)
            dw_ref[0, CONF_K:CONF_K + 1, :] = jnp.zeros((1, dc), F32)

    col = lambda k: pl.BlockSpec((TT, dc), lambda b, i: (ti(b, i), c0 + k))
    halo = lambda k: pl.BlockSpec((CONF_HALO, dc), lambda b, i: (hi(b, i), c0 + k))
    vec = pl.BlockSpec((1, dc), lambda b, i: (0, 0))
    return pl.pallas_call(
        body, name="mix_c_bwd", grid=(dm.BL, dm.NT),
        in_specs=[col(0), col(1), col(2), halo(0), halo(1),
                  pl.BlockSpec((TT, dc), lambda b, i: (ti(b, i), 0)),
                  pl.BlockSpec((CONF_K, dc), lambda b, i: (0, 0)), vec, vec, vec],
        out_specs=[pl.BlockSpec((TT, dm.WC), lambda b, i: (ti(b, i), 0)),
                   pl.BlockSpec((1, 32, dc), lambda b, i: (b, 0, 0)),
                   pl.BlockSpec((1, 8, dc), lambda b, i: (b, 0, 0))],
        out_shape=[jax.ShapeDtypeStruct((dm.R, dm.WC), BF16),
                   jax.ShapeDtypeStruct((dm.BL, 32, dc), F32),
                   jax.ShapeDtypeStruct((dm.BL, 8, dc), F32)],
        scratch_shapes=[pltpu.VMEM((nlb, CONF_HALO + TT, LANES), F32), pltpu.VMEM((nlb, TT, LANES), F32),
                        pltpu.VMEM((nlb, TT + CONF_HALO, LANES), F32), pltpu.VMEM((nlb, TT, LANES), F32),
                        pltpu.VMEM((nlb, SUBLANES - 1, SHIFT_ROWS, LANES), F32),
                        pltpu.VMEM((nlb, SUBLANES - 1, SHIFT_ROWS, LANES), F32),
                        pltpu.VMEM((nlb, CONF_K, SUBLANES, LANES), F32)],
        compiler_params=_params(("parallel", "arbitrary")),
    )(proj, proj, proj, proj, proj, dyc, conv_w, conv_b, ln_g, ln_b)


def _ssm_conv(rbuf, w_ref, b_ref, width):
    for lb in range(width // LANES):
        cs = slice(lb * LANES, (lb + 1) * LANES)
        acc = jnp.broadcast_to(b_ref[0:1, cs], (TT, LANES))
        for k in range(SSM_CONV_K):
            off = SMALL_HALO - (SSM_CONV_K - 1) + k
            acc = acc + w_ref[k:k + 1, cs] * rbuf[off:off + TT, cs]
        yield cs, acc


def _softplus(z):
    return jnp.maximum(z, 0.0) + jnp.log(1.0 + jnp.exp(-jnp.abs(z)))


def _tri(lower):
    r = lax.broadcasted_iota(jnp.int32, (TT, TT), 0)
    c = lax.broadcasted_iota(jnp.int32, (TT, TT), 1)
    return (c <= r) if lower else (c >= r)


def _exact_01_dot(mat01, x):
    x1, x2, x3 = _split3(x)
    return _dot(mat01, x1) + _dot(mat01, x2) + _dot(mat01, x3)


def _head_scalars(dt_ref, dtb_ref, alog_ref):
    z = dt_ref[...] + dtb_ref[...]
    dtv = _softplus(z)
    a = -jnp.exp(alog_ref[...])
    ac = _exact_01_dot(_tri(True).astype(F32).astype(BF16), dtv * a)
    eac = jnp.exp(ac)
    dst = jnp.exp(ac[TT - 1:TT, :] - ac)
    return z, dtv, a, ac, eac, dst


def _decay(ac, ac_t, h, causal):
    seg = ac[:, h:h + 1] - ac_t[h:h + 1, :]
    return jnp.where(causal, jnp.exp(jnp.where(causal, seg, 0.0)), 0.0)


def _pair_mask(h):
    lane = lax.broadcasted_iota(jnp.int32, (1, LANES), 1)
    return ((lane >= SSM_HEAD_DIM) if (h % 2) else (lane < SSM_HEAD_DIM)).astype(F32)


def _mix_b_fwd(proj, projdt, conv_w, conv_b, dt_bias, a_log, dskx, norm_g, expand, dm, rider=None):
    db, gn, xbc_w, hpg = dm.DB, dm.GN, dm.XBC, dm.HPG
    gw = db // SSM_GROUPS
    ti = _tile_index(dm, False)

    def body(bz_ref, bx_ref, bc_ref, dt_ref, w_ref, b_ref, dtb_ref, alog_ref, dsk_ref, g_ref, e_ref,
             y_ref, yraw_ref, sprev_ref, rbuf, xbuf, state, ybuf, exbuf, xdtbuf):
        i = pl.program_id(1)

        @pl.when(i == 0)
        def _():
            rbuf[0:SMALL_HALO, :] = jnp.zeros((SMALL_HALO, xbc_w), F32)
            state[...] = jnp.zeros_like(state)

        @pl.when(i > 0)
        def _():
            rbuf[0:SMALL_HALO, :] = rbuf[TT:TT + SMALL_HALO, :]

        rbuf[SMALL_HALO:SMALL_HALO + TT, 0:db] = bx_ref[...].astype(F32)
        rbuf[SMALL_HALO:SMALL_HALO + TT, db:xbc_w] = bc_ref[...].astype(F32)
        for cs, pre in _ssm_conv(rbuf, w_ref, b_ref, xbc_w):
            xbuf[:, cs] = pre * _sigmoid(pre)

        _, dtv, _, ac, eac, dst = _head_scalars(dt_ref, dtb_ref, alog_ref)
        exbuf[...] = _dot(jnp.concatenate([dtv, eac, dst], axis=0).astype(BF16), e_ref[...])
        ac_t = ac.T
        causal = _tri(True)
        sprev_ref[0, 0] = state[...]

        xdtbuf[...] = xbuf[:, 0:db] * exbuf[0:TT, :]
        ybuf[...] = xbuf[:, 0:db] * dsk_ref[...]
        for g in range(SSM_GROUPS):
            gs = slice(g * gw, (g + 1) * gw)
            bg = xbuf[:, db + g * SSM_STATE:db + (g + 1) * SSM_STATE].astype(BF16)
            cg = xbuf[:, db + gn + g * SSM_STATE:db + gn + (g + 1) * SSM_STATE].astype(BF16)
            cb = _dot_nt(cg, bg)
            for e in range(0, hpg, 2):
                h = g * hpg + e
                ps = slice(h * SSM_HEAD_DIM, (h + 2) * SSM_HEAD_DIM)
                xp = xdtbuf[:, ps]
                acc = jnp.zeros((TT, LANES), F32)
                for hh in (h, h + 1):
                    mm = (cb * _decay(ac, ac_t, hh, causal)).astype(BF16)
                    acc = acc + _dot(mm, (xp * _pair_mask(hh)).astype(BF16))
                ybuf[:, ps] = ybuf[:, ps] + acc
            sg = state[:, gs]
            ybuf[:, gs] = ybuf[:, gs] + exbuf[TT:2 * TT, gs] * _dot(cg, sg.astype(BF16))
            state[:, gs] = sg * exbuf[2 * TT - 1:2 * TT, gs] + _dot_tn(
                bg, (xdtbuf[:, gs] * exbuf[2 * TT:3 * TT, gs]).astype(BF16))

        yraw = ybuf[...]
        yraw_ref[...] = yraw
        bz = bz_ref[...].astype(F32)
        v = yraw * (bz * _sigmoid(bz))
        r = lax.rsqrt(jnp.mean(v * v, axis=-1, keepdims=True) + NORM_EPS)
        y_ref[...] = (v * r * g_ref[...]).astype(BF16)

    tile = lambda w, k: pl.BlockSpec((TT, w), lambda b, i: (ti(b, i), k))
    fixed = lambda r, w: pl.BlockSpec((r, w), lambda b, i: (0, 0))
    return _call(
        body, "mix_b_fwd", (dm.BL, dm.NT),
        [tile(db, dm.WA // db), tile(db, dm.WA // db + 1), tile(2 * gn, (dm.WA + 2 * db) // (2 * gn)),
         tile(DT_PAD, 0),
         fixed(SSM_CONV_K, xbc_w), fixed(1, xbc_w), fixed(1, DT_PAD), fixed(1, DT_PAD),
         fixed(1, db), fixed(1, db), fixed(DT_PAD, db)],
        [pl.BlockSpec((TT, db), lambda b, i: (ti(b, i), 0)),
         pl.BlockSpec((TT, db), lambda b, i: (ti(b, i), 0)),
         pl.BlockSpec((1, 1, SSM_STATE, db), lambda b, i: (b, i, 0, 0))],
        [jax.ShapeDtypeStruct((dm.R, db), BF16), jax.ShapeDtypeStruct((dm.R, db), F32),
         jax.ShapeDtypeStruct((dm.BL, dm.NT, SSM_STATE, db), F32)],
        [pltpu.VMEM((SMALL_HALO + TT, xbc_w), F32), pltpu.VMEM((TT, xbc_w), F32),
         pltpu.VMEM((SSM_STATE, db), F32), pltpu.VMEM((TT, db), F32),
         pltpu.VMEM((3 * TT, db), F32), pltpu.VMEM((TT, db), F32)],
        ("parallel", "arbitrary"),
        (proj, proj, proj, projdt, conv_w, conv_b, dt_bias, a_log, dskx, norm_g, expand), rider)


def _mix_b_bwd(proj, projdt, dyb, yraw, sprev, conv_w, conv_b, dt_bias, a_log, dskx, norm_g, expand, expand_t, dm,
               rider=None):
    db, gn, xbc_w, hpg = dm.DB, dm.GN, dm.XBC, dm.HPG
    gw = db // SSM_GROUPS
    ti = _tile_index(dm, True)
    hi = _halo_index(dm, HALO_BLOCK)

    def body(bz_ref, bx_ref, bc_ref, dt_ref, bxh_ref, bch_ref, dy_ref, yraw_ref, sprev_ref,
             w_ref, b_ref, dtb_ref, alog_ref, dsk_ref, g_ref, e_ref, et_ref,
             dp_ref, dpt_ref, dwc_ref, dch_ref, dhd_ref,
             rbuf, xbuf, dsbuf, dstate, dxbuf, z1buf, dprebuf, exbuf, xdtbuf, dyrbuf, uvec):
        i = pl.program_id(1)
        halo_on = jnp.where(i == dm.NT - 1, 0.0, 1.0)

        @pl.when(i == 0)
        def _():
            dwc_ref[...] = jnp.zeros_like(dwc_ref)
            dch_ref[...] = jnp.zeros_like(dch_ref)
            dhd_ref[...] = jnp.zeros_like(dhd_ref)
            dstate[...] = jnp.zeros_like(dstate)
            dprebuf[TT:TT + SMALL_HALO, :] = jnp.zeros((SMALL_HALO, xbc_w), F32)

        @pl.when(i > 0)
        def _():
            dprebuf[TT:TT + SMALL_HALO, :] = dprebuf[0:SMALL_HALO, :]

        rbuf[0:SMALL_HALO, 0:db] = _last_rows(bxh_ref[...]) * halo_on
        rbuf[0:SMALL_HALO, db:xbc_w] = _last_rows(bch_ref[...]) * halo_on
        rbuf[SMALL_HALO:SMALL_HALO + TT, 0:db] = bx_ref[...].astype(F32)
        rbuf[SMALL_HALO:SMALL_HALO + TT, db:xbc_w] = bc_ref[...].astype(F32)
        for cs, pre in _ssm_conv(rbuf, w_ref, b_ref, xbc_w):
            sl, dsl = _silu_and_grad(pre)
            xbuf[:, cs] = sl
            dsbuf[:, cs] = dsl

        z, dtv, a, ac, eac, dst = _head_scalars(dt_ref, dtb_ref, alog_ref)
        exbuf[...] = _dot(jnp.concatenate([dtv, eac, dst], axis=0).astype(BF16), e_ref[...])
        ac_t = ac.T
        causal = _tri(True)
        xdtbuf[...] = xbuf[:, 0:db] * exbuf[0:TT, :]

        yraw = yraw_ref[...]
        sz, dsz = _silu_and_grad(bz_ref[...].astype(F32))
        v = yraw * sz
        r = lax.rsqrt(jnp.mean(v * v, axis=-1, keepdims=True) + NORM_EPS)
        dy = dy_ref[...]
        dyg = dy * g_ref[...]
        dv = r * dyg - v * (r * r * r * jnp.mean(dyg * v, axis=-1, keepdims=True))
        dch_ref[0, 0:1, :] = dch_ref[0, 0:1, :] + jnp.sum(dy * v * r, axis=0, keepdims=True)
        dyr = dv * sz
        dyrbuf[...] = dyr
        dp_ref[:, 0:db] = (dv * yraw * dsz).astype(BF16)
        dch_ref[0, 1:2, :] = dch_ref[0, 1:2, :] + jnp.sum(dyr * xbuf[:, 0:db], axis=0, keepdims=True)

        lane_row = lax.broadcasted_iota(jnp.int32, (1, LANES), 1)
        sub_col = lax.broadcasted_iota(jnp.int32, (LANES, 1), 0)
        dac = jnp.zeros((TT, LANES), F32)
        colacc = jnp.zeros((LANES, TT), F32)
        for g in range(SSM_GROUPS):
            gs = slice(g * gw, (g + 1) * gw)
            bs_ = slice(db + g * SSM_STATE, db + (g + 1) * SSM_STATE)
            cs_ = slice(db + gn + g * SSM_STATE, db + gn + (g + 1) * SSM_STATE)
            bg = xbuf[:, bs_].astype(BF16)
            cg = xbuf[:, cs_].astype(BF16)
            cb = _dot_nt(cg, bg)
            dcb = jnp.zeros((TT, TT), F32)
            for e in range(0, hpg, 2):
                h = g * hpg + e
                ps = slice(h * SSM_HEAD_DIM, (h + 2) * SSM_HEAD_DIM)
                xp16 = xdtbuf[:, ps].astype(BF16)
                dyp = dyrbuf[:, ps]
                acc = jnp.zeros((TT, LANES), F32)
                for hh in (h, h + 1):
                    dec = _decay(ac, ac_t, hh, causal)
                    mm = cb * dec
                    dyh = (dyp * _pair_mask(hh)).astype(BF16)
                    dmm = _dot_nt(dyh, xp16)
                    acc = acc + _dot_tn(mm.astype(BF16), dyh)
                    dcb = dcb + dmm * dec
                    gm = dmm * mm
                    dac = dac + jnp.sum(gm, axis=1, keepdims=True) * (lane_row == hh).astype(F32)
                    colacc = colacc + (sub_col == hh).astype(F32) * jnp.sum(gm, axis=0, keepdims=True)
                dxbuf[:, ps] = acc
            sg32 = sprev_ref[0, 0, :, gs]
            sg = sg32.astype(BF16)
            dsn = dstate[:, gs]
            dsn16 = dsn.astype(BF16)
            dcb16 = dcb.astype(BF16)
            eacx = exbuf[TT:2 * TT, gs]
            dstx = exbuf[2 * TT:3 * TT, gs]
            cdx = exbuf[2 * TT - 1:2 * TT, gs]
            dye16 = (dyrbuf[:, gs] * eacx).astype(BF16)
            xdt_g = xdtbuf[:, gs]
            dxbuf[:, cs_] = _dot(dcb16, bg) + _dot_nt(dye16, sg)
            dst_x = dstx * _dot(bg, dsn16)
            dxbuf[:, bs_] = _dot_tn(dcb16, cg) + _dot_nt((dstx * xdt_g).astype(BF16), dsn16)
            dstate[:, gs] = cdx * dsn + _dot_tn(cg, dye16)
            z1buf[:, gs] = dyrbuf[:, gs] * (eacx * _dot(cg, sg)) - xdt_g * dst_x
            uvec[:, gs] = jnp.broadcast_to(
                jnp.sum(xdt_g * dst_x, axis=0, keepdims=True) + jnp.sum(dsn * cdx * sg32, axis=0, keepdims=True),
                (8, gw))
            dxbuf[:, gs] = dxbuf[:, gs] + dst_x

        zz = _dot(jnp.concatenate([z1buf[...], dxbuf[:, 0:db] * xbuf[:, 0:db]], axis=0).astype(BF16), et_ref[...])
        u1, u2, u3 = _split3(uvec[...])
        ulast = (_dot(u1, et_ref[...]) + _dot(u2, et_ref[...]) + _dot(u3, et_ref[...]))[0:1, :]
        is_last = (lax.broadcasted_iota(jnp.int32, (TT, 1), 0) == TT - 1).astype(F32)
        dac = dac - colacc.T + zz[0:TT] + is_last * ulast
        dda = _exact_01_dot(_tri(False).astype(F32).astype(BF16), dac)
        ddt = dda * a + zz[TT:2 * TT]
        dhd_ref[0, 1:2, :] = dhd_ref[0, 1:2, :] + jnp.sum(dda * dtv, axis=0, keepdims=True) * a
        ddtraw = ddt * _sigmoid(z)
        dhd_ref[0, 0:1, :] = dhd_ref[0, 0:1, :] + jnp.sum(ddtraw, axis=0, keepdims=True)
        dpt_ref[...] = ddtraw.astype(BF16)
        dxbuf[:, 0:db] = dyrbuf[...] * dsk_ref[...] + dxbuf[:, 0:db] * exbuf[0:TT, :]

        for lb in range(xbc_w // LANES):
            cs = slice(lb * LANES, (lb + 1) * LANES)
            dpre = dxbuf[:, cs] * dsbuf[:, cs]
            dprebuf[0:TT, cs] = dpre
            dwc_ref[0, SSM_CONV_K:SSM_CONV_K + 1, cs] = dwc_ref[0, SSM_CONV_K:SSM_CONV_K + 1, cs] + jnp.sum(
                dpre, axis=0, keepdims=True)
            draw = w_ref[SSM_CONV_K - 1:SSM_CONV_K, cs] * dpre
            for k in range(SSM_CONV_K - 1):
                sh = SSM_CONV_K - 1 - k
                draw = draw + w_ref[k:k + 1, cs] * dprebuf[sh:sh + TT, cs]
            for k in range(SSM_CONV_K):
                off = SMALL_HALO - (SSM_CONV_K - 1) + k
                dwc_ref[0, k:k + 1, cs] = dwc_ref[0, k:k + 1, cs] + jnp.sum(
                    dpre * rbuf[off:off + TT, cs], axis=0, keepdims=True)
            dp_ref[:, db + lb * LANES:db + (lb + 1) * LANES] = draw.astype(BF16)

    tile = lambda w, k: pl.BlockSpec((TT, w), lambda b, i: (ti(b, i), k))
    halo = lambda w, k: pl.BlockSpec((HALO_BLOCK, w), lambda b, i: (hi(b, i), k))
    fixed = lambda r, w: pl.BlockSpec((r, w), lambda b, i: (0, 0))
    kz = dm.WA // db
    kc = (dm.WA + 2 * db) // (2 * gn)
    return _call(
        body, "mix_b_bwd", (dm.BL, dm.NT),
        [tile(db, kz), tile(db, kz + 1), tile(2 * gn, kc), tile(DT_PAD, 0),
         halo(db, kz + 1), halo(2 * gn, kc),
         pl.BlockSpec((TT, db), lambda b, i: (ti(b, i), 0)),
         pl.BlockSpec((TT, db), lambda b, i: (ti(b, i), 0)),
         pl.BlockSpec((1, 1, SSM_STATE, db), lambda b, i: (b, dm.NT - 1 - i, 0, 0)),
         fixed(SSM_CONV_K, xbc_w), fixed(1, xbc_w), fixed(1, DT_PAD), fixed(1, DT_PAD),
         fixed(1, db), fixed(1, db), fixed(DT_PAD, db), fixed(db, DT_PAD)],
        [pl.BlockSpec((TT, dm.WB), lambda b, i: (ti(b, i), 0)),
         pl.BlockSpec((TT, DT_PAD), lambda b, i: (ti(b, i), 0)),
         pl.BlockSpec((1, 8, xbc_w), lambda b, i: (b, 0, 0)),
         pl.BlockSpec((1, 8, db), lambda b, i: (b, 0, 0)),
         pl.BlockSpec((1, 8, DT_PAD), lambda b, i: (b, 0, 0))],
        [jax.ShapeDtypeStruct((dm.R, dm.WB), BF16), jax.ShapeDtypeStruct((dm.R, DT_PAD), BF16),
         jax.ShapeDtypeStruct((dm.BL, 8, xbc_w), F32), jax.ShapeDtypeStruct((dm.BL, 8, db), F32),
         jax.ShapeDtypeStruct((dm.BL, 8, DT_PAD), F32)],
        [pltpu.VMEM((SMALL_HALO + TT, xbc_w), F32), pltpu.VMEM((TT, xbc_w), F32),
         pltpu.VMEM((TT, xbc_w), F32), pltpu.VMEM((SSM_STATE, db), F32),
         pltpu.VMEM((TT, xbc_w), F32), pltpu.VMEM((TT, db), F32),
         pltpu.VMEM((TT + SMALL_HALO, xbc_w), F32), pltpu.VMEM((3 * TT, db), F32),
         pltpu.VMEM((TT, db), F32), pltpu.VMEM((TT, db), F32), pltpu.VMEM((8, db), F32)],
        ("parallel", "arbitrary"),
        (proj, proj, proj, projdt, proj, proj, dyb, yraw, sprev,
         conv_w, conv_b, dt_bias, a_log, dskx, norm_g, expand, expand_t), rider)


def _head_consts(dm):
    head_of = jnp.arange(dm.DB) // SSM_HEAD_DIM
    expand = (jnp.arange(DT_PAD)[:, None] == head_of[None, :]).astype(BF16)
    return expand, expand.T


def _ssm_params(lw, dm):
    pad_h = lambda v: jnp.pad(v, (0, DT_PAD - dm.H))[None]
    return (lw["ssm_conv_w"], lw["ssm_conv_b"][None], pad_h(lw["dt_bias"]), pad_h(lw["a_log"]),
            jnp.repeat(lw["d_skip"], SSM_HEAD_DIM)[None], lw["ssm_norm_g"][None])


def _layer_fwd(h, lw, w_in, w_out, cst, dm, next_bases=None):
    nxt = next_bases is not None
    (proj, projdt, hn), got = _fwd_in(h, lw["pre_g"][None], w_in, dm,
                                      _ride_gather_ici(next_bases, 0, 2) if nxt else None)
    ya = _mix_a_fwd(proj, lw["conv_a_w"], dm)
    (yb, yraw, sprev), got = _mix_b_fwd(proj, projdt, *_ssm_params(lw, dm), cst[0], dm,
                                        _ride_gather_ici(got, 1, 2) if nxt else None)
    yc = _mix_c_fwd(proj, lw["conf_conv_w"], lw["conf_conv_b"][None], lw["conf_ln_g"][None],
                    lw["conf_ln_b"][None], dm)
    (h_new, m), got = _fwd_out(ya, yb, yc, w_out, h, lw["post_g"][None], dm, _ride_gather_d2d(got) if nxt else None)
    return h_new, (h, hn, proj, projdt, ya, yb, yc, yraw, sprev, m), got


def _layer_bwd(dh, saved, lw, w_in, w_out, cst, dm, reduce=None, last=False):
    h_in, hn, proj, projdt, ya, yb, yc, yraw, sprev, m = saved
    (dya, dyb, dyc, dwo, dpost), got = _bwd_out(dh, m, lw["post_g"][None], w_out, ya, yb, yc, dm,
                                                None if reduce is None else reduce.swap())
    dpa, dwa = _mix_a_bwd(proj, dya, lw["conv_a_w"], dm)
    (dpb, dpt, dwcv, dch, dhd), got = _mix_b_bwd(proj, projdt, dyb, yraw, sprev, *_ssm_params(lw, dm), cst[0],
                                                 cst[1], dm, None if reduce is None else reduce.to_owners(got))
    dpc, dwcf, dvc = _mix_c_bwd(proj, dyc, lw["conf_conv_w"], lw["conf_conv_b"][None], lw["conf_ln_g"][None],
                                lw["conf_ln_b"][None], dm)
    def own_reduce():
        pieces = [_bwd_in_dw(hn, dp, dm, n) for dp, n in ((dpa, "a"), (dpb, "b"), (dpc, "c"), (dpt, "dt"))]
        return _GradReduce([_grad_to_shards(pieces, dm), dwo.reshape(N_CHIPS, 2 * dm.D // N_CHIPS, dm.D)])

    rider = None if reduce is None else reduce.join(got)
    n_join = 0 if rider is None else len(rider.out_shapes)
    if last:
        mine = own_reduce()
        to_owners = mine.to_owners(_exchange("grad_swap_halves", mine.swap()))
        rider = to_owners if rider is None else _ride_both(rider, to_owners)
    (dh, dpre), got = _bwd_in_dx(dpa, dpb, dpc, dpt, w_in, h_in, dh, lw["pre_g"][None], dm, rider)
    if reduce is not None:
        reduce.finish(got[:n_join])
    if last:
        mine.finish(_exchange("grad_join_halves", mine.join(got[n_join:])))
    else:
        mine = own_reduce()
    dwcv, dch, dhd, dvc = (jnp.sum(a, axis=0) for a in (dwcv, dch, dhd, dvc))
    small = dict(pre_g=dpre[0], post_g=dpost[0], conv_a_w=jnp.sum(dwa, axis=0)[:CONV_A_K],
                 ssm_conv_w=dwcv[:SSM_CONV_K], ssm_conv_b=dwcv[SSM_CONV_K], ssm_norm_g=dch[0],
                 d_skip=jnp.sum(dch[1].reshape(dm.H, SSM_HEAD_DIM), axis=1), dt_bias=dhd[0, :dm.H],
                 a_log=dhd[1, :dm.H], conf_conv_w=jnp.sum(dwcf, axis=0)[:CONF_K], conf_conv_b=dvc[0],
                 conf_ln_g=dvc[1], conf_ln_b=dvc[2])
    return dh, mine, small


def _shard_runs(dm):
    ab = dm.WA + dm.WB
    order = [(0, 0, ab), (ab, dm.NP - DT_PAD, dm.H), (ab + dm.H, ab, dm.WC)]
    k = dm.NIN // N_CHIPS
    runs = []
    for s in range(N_CHIPS):
        for o0, m0, wd in order:
            lo, hi = max(o0, s * k), min(o0 + wd, (s + 1) * k)
            if lo < hi:
                runs.append((s, lo - s * k, m0 + lo - o0, hi - lo))
    return runs


def _w_in_from_shards(base, dm):
    tr = _row_tile(dm.D, 256)
    k = dm.NIN // N_CHIPS
    runs = _shard_runs(dm)

    def body(in_ref, out_ref):
        for s, sc, mc, wd in runs:
            out_ref[:, mc:mc + wd] = in_ref[s, :, sc:sc + wd]
        out_ref[:, dm.NP - DT_PAD + dm.H:dm.NP] = jnp.zeros((tr, DT_PAD - dm.H), BF16)

    return pl.pallas_call(
        body, name="w_in_from_shards", grid=(dm.D // tr,),
        in_specs=[pl.BlockSpec((N_CHIPS, tr, k), lambda r: (0, r, 0))],
        out_specs=pl.BlockSpec((tr, dm.NP), lambda r: (r, 0)),
        out_shape=jax.ShapeDtypeStruct((dm.D, dm.NP), BF16),
        compiler_params=_params(("parallel",)),
    )(base)


def _grad_to_shards(pieces, dm):
    tr = _row_tile(dm.D, 256)
    k = dm.NIN // N_CHIPS
    starts = [0, dm.WA, dm.WA + dm.WB, dm.NP - DT_PAD]
    widths = [dm.WA, dm.WB, dm.WC, DT_PAD]
    runs = _shard_runs(dm)

    def body(a_ref, b_ref, c_ref, t_ref, out_ref):
        refs = (a_ref, b_ref, c_ref, t_ref)
        for s, sc, mc, wd in runs:
            for p in range(4):
                lo, hi = max(mc, starts[p]), min(mc + wd, starts[p] + widths[p])
                if lo < hi:
                    out_ref[s, :, sc + lo - mc:sc + hi - mc] = refs[p][:, lo - starts[p]:hi - starts[p]].astype(BF16)

    return pl.pallas_call(
        body, name="grad_to_shards", grid=(dm.D // tr,),
        in_specs=[pl.BlockSpec((tr, w), lambda r: (r, 0)) for w in widths],
        out_specs=pl.BlockSpec((N_CHIPS, tr, k), lambda r: (0, r, 0)),
        out_shape=jax.ShapeDtypeStruct((N_CHIPS, dm.D, k), BF16),
        compiler_params=_params(("parallel",)),
    )(*pieces)


def _place_own(w, me):
    rows, cols = w.shape
    tr = _row_tile(rows, 256)

    def body(me_ref, w_ref, out_ref):
        out_ref[0] = w_ref[...].astype(BF16)

    return pl.pallas_call(
        body, name="place_own",
        grid_spec=pltpu.PrefetchScalarGridSpec(
            num_scalar_prefetch=1, grid=(rows // tr,),
            in_specs=[pl.BlockSpec((tr, cols), lambda r, me_ref: (r, 0))],
            out_specs=pl.BlockSpec((1, tr, cols), lambda r, me_ref: (me_ref[0], r, 0))),
        out_shape=jax.ShapeDtypeStruct((N_CHIPS, rows, cols), BF16),
        compiler_params=_params(("parallel",)),
    )(me, w)


def _add_halves(g, got, c, name):
    _, _, rows, cols = g.shape
    tr = _row_tile(rows, 256)

    def body(c_ref, g_ref, got_ref, out_ref):
        out_ref[0] = (g_ref[0, 0].astype(F32) + got_ref[0].astype(F32)).astype(BF16)

    return pl.pallas_call(
        body, name=name,
        grid_spec=pltpu.PrefetchScalarGridSpec(
            num_scalar_prefetch=1, grid=(N_CHIPS, rows // tr),
            in_specs=[pl.BlockSpec((1, 1, tr, cols), lambda s, r, c_ref: (s, c_ref[0], r, 0)),
                      pl.BlockSpec((1, tr, cols), lambda s, r, c_ref: (s, r, 0))],
            out_specs=pl.BlockSpec((1, tr, cols), lambda s, r, c_ref: (s, r, 0))),
        out_shape=jax.ShapeDtypeStruct((N_CHIPS, rows, cols), BF16),
        compiler_params=_params(("parallel", "parallel")),
    )(c, g, got)


def _add_owner(p, got, where, name):
    _, rows, cols = p.shape
    tr = _row_tile(rows, 256)

    def body(w_ref, p_ref, got_ref, out_ref):
        acc = p_ref[0].astype(F32)
        for j in range(3):
            acc = acc + got_ref[j].astype(F32)
        out_ref[0] = acc

    return pl.pallas_call(
        body, name=name,
        grid_spec=pltpu.PrefetchScalarGridSpec(
            num_scalar_prefetch=1, grid=(rows // tr,),
            in_specs=[pl.BlockSpec((1, tr, cols), lambda r, w_ref: (w_ref[0], r, 0)),
                      pl.BlockSpec((3, tr, cols), lambda r, w_ref: (0, r, 0))],
            out_specs=pl.BlockSpec((1, tr, cols), lambda r, w_ref: (w_ref[1], r, 0))),
        out_shape=jax.ShapeDtypeStruct((2, rows, cols), F32),
        compiler_params=_params(("parallel",)),
    )(where, p, got)


class _GradReduce:
    def __init__(self, gs):
        self.gs = [g.reshape((N_CHIPS, 2, g.shape[1] // 2) + g.shape[2:]) for g in gs]
        self.c = lax.axis_index("c").astype(jnp.int32).reshape(1)
        chip = (2 * lax.axis_index("x") + lax.axis_index("y")).astype(jnp.int32)
        self.where = jnp.stack([chip, self.c[0]])
        self.result = None

    def swap(self):
        return _ride_swap_halves(self.gs)

    def to_owners(self, got):
        self.ps = [_add_halves(g, r, self.c, "grad_add_sibling_" + n) for g, r, n in zip(self.gs, got, ("in", "out"))]
        return _ride_to_owners(self.ps)

    def join(self, got):
        qs = [_add_owner(p, r, self.where, "grad_add_chips_" + n) for p, r, n in zip(self.ps, got, ("in", "out"))]
        return _ride_join_halves(qs)

    def finish(self, got):
        self.result = [a.reshape((a.shape[0] * a.shape[1],) + a.shape[2:]) for a in got]


def _adamw_math(w, g, m, v):
    m = ADAM_B1 * m + (1.0 - ADAM_B1) * g
    v = ADAM_B2 * v + (1.0 - ADAM_B2) * (g * g)
    m_hat = m / (1.0 - ADAM_B1 ** ADAM_STEP)
    v_hat = v / (1.0 - ADAM_B2 ** ADAM_STEP)
    delta = -ADAM_LR * (m_hat / (jnp.sqrt(v_hat) + ADAM_EPS) + ADAM_WD * w)
    return delta, m, v


def _adamw_small(w, g, m, v, name):
    def body(w_ref, g_ref, m_ref, v_ref, d_out, m_out, v_out):
        d_out[...], m_out[...], v_out[...] = _adamw_math(w_ref[...], g_ref[...], m_ref[...], v_ref[...])

    shape = jax.ShapeDtypeStruct(w.shape, F32)
    return pl.pallas_call(body, name="adamw_" + name, out_shape=[shape, shape, shape],
                          compiler_params=_params())(w, g, m, v)


def _adamw_layer(i, w, g, m, v, prev, name):
    depth, rows, cols = w.shape
    tr = _row_tile(rows, 256)
    n_prev = 0 if prev is None else 4

    def body(*refs):
        w_ref, g_ref, m_ref, v_ref = refs[:4]
        g_out, d_out, m_out, v_out = refs[4 + n_prev:]
        gv = g_ref[...]
        g_out[0] = gv
        d_out[0], m_out[0], v_out[0] = _adamw_math(w_ref[0], gv, m_ref[0], v_ref[0])

    lay = pl.BlockSpec((1, tr, cols), lambda r: (i, r, 0))
    shape = jax.ShapeDtypeStruct(w.shape, F32)
    return pl.pallas_call(
        body, name="adamw_" + name, grid=(rows // tr,),
        in_specs=[lay, pl.BlockSpec((tr, cols), lambda r: (r, 0)), lay, lay] + [ANY] * n_prev,
        out_specs=[lay] * 4, out_shape=[shape] * 4,
        input_output_aliases={4 + k: k for k in range(n_prev)},
        compiler_params=_params(("parallel",)),
    )(w, g, m, v, *(prev or ()))


def _adamw_cols_major(w, gs, m, v, name):
    depth, rows, cols = w.shape
    tr = max(t for t in range(1, 129) if cols % t == 0)
    wt, mt, vt = (jnp.transpose(a, (2, 0, 1)) for a in (w, m, v))
    gt = jnp.stack([g.T for g in gs], axis=1)

    def body(w_ref, g_ref, m_ref, v_ref, g_out, d_out, m_out, v_out):
        gv = g_ref[...]
        g_out[...] = gv
        d_out[...], m_out[...], v_out[...] = _adamw_math(w_ref[...], gv, m_ref[...], v_ref[...])

    spec = pl.BlockSpec((tr, depth, rows), lambda r: (r, 0, 0))
    shape = jax.ShapeDtypeStruct((cols, depth, rows), F32)
    outs = pl.pallas_call(body, name="adamw_" + name, grid=(cols // tr,), in_specs=[spec] * 4, out_specs=[spec] * 4,
                          out_shape=[shape] * 4, compiler_params=_params(("parallel",)))(wt, gt, mt, vt)
    return [jnp.transpose(a, (1, 2, 0)) for a in outs]


def _sum_leading(buf, name):
    n, rows, cols = buf.shape
    tr = _row_tile(rows, 512)

    def body(in_ref, out_ref):
        acc = in_ref[0]
        for k in range(1, n):
            acc = acc + in_ref[k]
        out_ref[...] = acc

    return pl.pallas_call(
        body, name=name, grid=(rows // tr,),
        in_specs=[pl.BlockSpec((n, tr, cols), lambda i: (0, i, 0))],
        out_specs=pl.BlockSpec((tr, cols), lambda i: (i, 0)),
        out_shape=jax.ShapeDtypeStruct((rows, cols), F32),
        compiler_params=_params(("parallel",)),
    )(buf)


_SHARDED_SMALL = ("meta", "conv_a_w", "ssm_conv_w", "conf_conv_w")
_LAYER_SMALL = ("pre_g", "post_g", "conv_a_w", "ssm_conv_w", "ssm_conv_b", "dt_bias", "a_log", "d_skip",
                "ssm_norm_g", "conf_conv_w", "conf_conv_b", "conf_ln_g", "conf_ln_b")
_WEIGHTS = ("meta", "pre_g", "post_g", "w_in", "w_out", "conv_a_w", "ssm_conv_w", "ssm_conv_b", "dt_bias", "a_log",
            "d_skip", "ssm_norm_g", "conf_conv_w", "conf_conv_b", "conf_ln_g", "conf_ln_b")


def _unshard_last(a):
    return jnp.moveaxis(a, 0, -2).reshape(a.shape[1:-1] + (N_CHIPS * a.shape[-1],))


def _shard_last(a):
    return jnp.moveaxis(a.reshape(a.shape[:-1] + (N_CHIPS, a.shape[-1] // N_CHIPS)), -2, 0)


def _with_own_block(a, n, at):
    return lax.dynamic_update_index_in_dim(jnp.zeros((n,) + a.shape, a.dtype), a, at, 0)


def kernel(x, meta, pre_g, post_g, w_in, w_out, conv_a_w, ssm_conv_w, ssm_conv_b, dt_bias, a_log, d_skip, ssm_norm_g, conf_conv_w, conf_conv_b, conf_ln_g, conf_ln_b, loss_target, m_meta, m_pre_g, m_post_g, m_w_in, m_w_out, m_conv_a_w, m_ssm_conv_w, m_ssm_conv_b, m_dt_bias, m_a_log, m_d_skip, m_ssm_norm_g, m_conf_conv_w, m_conf_conv_b, m_conf_ln_g, m_conf_ln_b, v_meta, v_pre_g, v_post_g, v_w_in, v_w_out, v_conv_a_w, v_ssm_conv_w, v_ssm_conv_b, v_dt_bias, v_a_log, v_d_skip, v_ssm_norm_g, v_conf_conv_w, v_conf_conv_b, v_conf_ln_g, v_conf_ln_b):
    w = dict(meta=meta, pre_g=pre_g, post_g=post_g, w_in=w_in, w_out=w_out, conv_a_w=conv_a_w,
             ssm_conv_w=ssm_conv_w, ssm_conv_b=ssm_conv_b, dt_bias=dt_bias, a_log=a_log, d_skip=d_skip,
             ssm_norm_g=ssm_norm_g, conf_conv_w=conf_conv_w, conf_conv_b=conf_conv_b, conf_ln_g=conf_ln_g,
             conf_ln_b=conf_ln_b)
    mom = dict(meta=m_meta, pre_g=m_pre_g, post_g=m_post_g, w_in=m_w_in, w_out=m_w_out, conv_a_w=m_conv_a_w,
               ssm_conv_w=m_ssm_conv_w, ssm_conv_b=m_ssm_conv_b, dt_bias=m_dt_bias, a_log=m_a_log, d_skip=m_d_skip,
               ssm_norm_g=m_ssm_norm_g, conf_conv_w=m_conf_conv_w, conf_conv_b=m_conf_conv_b,
               conf_ln_g=m_conf_ln_g, conf_ln_b=m_conf_ln_b)
    vel = dict(meta=v_meta, pre_g=v_pre_g, post_g=v_post_g, w_in=v_w_in, w_out=v_w_out, conv_a_w=v_conv_a_w,
               ssm_conv_w=v_ssm_conv_w, ssm_conv_b=v_ssm_conv_b, dt_bias=v_dt_bias, a_log=v_a_log, d_skip=v_d_skip,
               ssm_norm_g=v_ssm_norm_g, conf_conv_w=v_conf_conv_w, conf_conv_b=v_conf_conv_b,
               conf_ln_g=v_conf_ln_g, conf_ln_b=v_conf_ln_b)
    bl, seq, d = x.shape
    dm = Dims(bl, seq, d)
    depth = w_in.shape[0]
    chip = (2 * lax.axis_index("x") + lax.axis_index("y")).astype(jnp.int32)
    dev = 2 * chip + lax.axis_index("c").astype(jnp.int32)
    cst = _head_consts(dm)

    got = _exchange("gather_small_weights",
                    _ride_gather_small([_with_own_block(w[n], N_CHIPS, chip) for n in _SHARDED_SMALL]))
    full = dict(w)
    for n, a in zip(_SHARDED_SMALL, got):
        full[n] = _unshard_last(a)

    bases = [[_place_own(w_in[i], chip.reshape(1)), _place_own(w_out[i], chip.reshape(1))] for i in range(depth)]
    gathered = _exchange("gather_d2d_first", _ride_gather_d2d(_exchange("gather_ici_first",
                                                                         _ride_gather_ici(bases[0]))))
    h = _embed(x, full["meta"], dm)
    saved, proj_w = [], []
    for i in range(depth):
        lw = {n: full[n][i] for n in _LAYER_SMALL}
        proj_w.append((_w_in_from_shards(gathered[0], dm), gathered[1].reshape(2 * d, d)))
        h, keep, gathered = _layer_fwd(h, lw, proj_w[i][0], proj_w[i][1], cst, dm,
                                       bases[i + 1] if i + 1 < depth else None)
        saved.append(keep)

    dh, loss = _loss_head(h, loss_target, dm)
    loss = lax.psum(loss, ("x", "y", "c"))

    small_g = {n: [None] * depth for n in _LAYER_SMALL}
    big = {"w_in": None, "w_out": None}
    g_in = [None] * depth
    reduce = None
    for i in reversed(range(depth)):
        lw = {n: full[n][i] for n in _LAYER_SMALL}
        dh, mine, sg = _layer_bwd(dh, saved[i], lw, proj_w[i][0], proj_w[i][1], cst, dm, reduce, last=i == 0)
        for n in _LAYER_SMALL:
            small_g[n][i] = sg[n]
        if reduce is not None:
            g_in[i + 1] = reduce.result[0]
            big["w_out"] = _adamw_layer(i + 1, w_out, reduce.result[1], m_w_out, v_w_out, big["w_out"], "w_out")
        reduce = mine
    g_in[0] = reduce.result[0]
    big["w_out"] = _adamw_layer(0, w_out, reduce.result[1], m_w_out, v_w_out, big["w_out"], "w_out")
    big["w_in"] = _adamw_cols_major(w_in, g_in, m_w_in, v_w_in, "w_in")
    grad_x, gmeta = _unembed(dh, dm)

    g = {n: jnp.stack(v) for n, v in small_g.items()}
    g["meta"] = gmeta
    small = [n for n in _WEIGHTS if n not in ("w_in", "w_out")]
    flat = jnp.concatenate([g[n].reshape(-1) for n in small])
    rows = -(-flat.shape[0] // (16 * LANES)) * 16
    flat = jnp.pad(flat, (0, rows * LANES - flat.shape[0])).reshape(rows, LANES)
    parts = _exchange("small_grads_gather_all", _ride_gather_all(_with_own_block(flat, N_DEV, dev)))[0]
    total = _sum_leading(parts, "small_grads_sum").reshape(-1)
    grads, deltas, new_m, new_v = {}, {}, {}, {}
    off = 0
    for n in small:
        size = g[n].size
        fullg = total[off:off + size].reshape(g[n].shape)
        off += size
        if n in _SHARDED_SMALL:
            fullg = lax.dynamic_index_in_dim(_shard_last(fullg), chip, axis=0, keepdims=False)
        grads[n] = fullg
        deltas[n], new_m[n], new_v[n] = _adamw_small(w[n], fullg, mom[n], vel[n], n)
    for n in ("w_in", "w_out"):
        grads[n], deltas[n], new_m[n], new_v[n] = big[n]

    return (loss, grad_x, *[grads[n] for n in _WEIGHTS], *[deltas[n] for n in _WEIGHTS],
            *[new_m[n] for n in _WEIGHTS], *[new_v[n] for n in _WEIGHTS])
```

```python
import jax
import jax.numpy as jnp
from jax import lax
from jax.experimental import pallas as pl
from jax.experimental.pallas import tpu as pltpu

F32 = jnp.float32
BF16 = jnp.bfloat16

N_META = 16
TT = 128
SSM_STATE = 128
SSM_GROUPS = 2
SSM_HEAD_DIM = 64
CONV_A_K = 3
SSM_CONV_K = 4
CONF_K = 31
NORM_EPS = 1e-6
LN_EPS = 1e-5
LANES = 128
MXU_DIM = 256
DT_PAD = LANES
CONF_HALO = 32
SMALL_HALO = 8
VMEM_LIMIT = 56 * 1024 * 1024
N_CHIPS = 4
N_DEV = 8

ADAM_LR = 0.001
ADAM_B1 = 0.9
ADAM_B2 = 0.999
ADAM_EPS = 1e-08
ADAM_WD = 0.01
ADAM_STEP = 10

MESH = pl.DeviceIdType.MESH
ANY = pl.BlockSpec(memory_space=pl.ANY)


class Dims:
    def __init__(self, bl, seq, d):
        self.BL, self.S, self.D = bl, seq, d
        self.L = seq + N_META
        self.Lp = -(-self.L // TT) * TT
        self.NT = self.Lp // TT
        self.R = bl * self.Lp
        self.DA = d // 2
        self.DB = d
        self.DC = d // 2
        self.H = self.DB // SSM_HEAD_DIM
        self.HPG = self.H // SSM_GROUPS
        self.GN = SSM_GROUPS * SSM_STATE
        self.WA = 4 * self.DA
        self.WB = 2 * self.DB + 2 * self.GN
        self.WC = 3 * self.DC
        self.NP = self.WA + self.WB + self.WC + DT_PAD
        self.NIN = self.WA + self.WB + self.H + self.WC
        self.XBC = self.DB + 2 * self.GN
        assert self.H % 2 == 0 and self.HPG % 2 == 0 and self.H <= DT_PAD
        assert self.DA % LANES == 0 and (self.WA + self.WB) % self.DC == 0 and self.WA % self.DB == 0


def _row_tile(n, target):
    best = None
    for t in range(16, min(n, target) + 1, 16):
        if n % t == 0:
            best = t
    assert best is not None
    return best


def _col_tile(n, target):
    best = None
    for t in range(LANES, min(n, target) + 1, LANES):
        if n % t == 0:
            best = t
    assert best is not None
    return best


def _params(sem=None):
    return pltpu.CompilerParams(dimension_semantics=sem, vmem_limit_bytes=VMEM_LIMIT)


def _sigmoid(x):
    return 1.0 / (1.0 + jnp.exp(-x))


def _silu_and_grad(x):
    s = _sigmoid(x)
    return x * s, s * (1.0 + x * (1.0 - s))


def _dot(a, b):
    return jnp.dot(a, b, preferred_element_type=F32)


def _dot_nt(a, b):
    return lax.dot_general(a, b, (((1,), (1,)), ((), ())), preferred_element_type=F32)


def _dot_tn(a, b):
    return lax.dot_general(a, b, (((0,), (0,)), ((), ())), preferred_element_type=F32)


def _split3(x):
    x1 = x.astype(BF16)
    r1 = x - x1.astype(F32)
    x2 = r1.astype(BF16)
    x3 = (r1 - x2.astype(F32)).astype(BF16)
    return x1, x2, x3


class Rider:
    def __init__(self, plan, ins, out_shapes, aliases, nsem):
        self.plan, self.ins, self.out_shapes, self.aliases, self.nsem = plan, list(ins), list(out_shapes), aliases, nsem


def _place():
    x, y, c = lax.axis_index("x"), lax.axis_index("y"), lax.axis_index("c")
    chips = [(1 - x, y), (x, 1 - y), (1 - x, 1 - y)]
    return x, y, c, chips


def _remote(k, src, dst, to, send_sems, recv_sems):
    return pltpu.make_async_remote_copy(src_ref=src, dst_ref=dst, send_sem=send_sems.at[k], recv_sem=recv_sems.at[k],
                                        device_id=to, device_id_type=MESH)


def _call(body, name, grid, in_specs, out_specs, out_shape, scratch_shapes, sem, args, rider=None):
    if rider is None:
        outs = pl.pallas_call(body, name=name, grid=grid, in_specs=in_specs, out_specs=out_specs, out_shape=out_shape,
                              scratch_shapes=scratch_shapes, compiler_params=_params(sem))(*args)
        return list(outs), []
    n_in, n_out, n_scr = len(args), len(out_shape), len(scratch_shapes)
    r_in, r_out = len(rider.ins), len(rider.out_shapes)

    def hosted(*refs):
        ins, rins = refs[:n_in], refs[n_in:n_in + r_in]
        o0 = n_in + r_in
        outs, routs = refs[o0:o0 + n_out], refs[o0 + n_out:o0 + n_out + r_out]
        scr = refs[o0 + n_out + r_out:o0 + n_out + r_out + n_scr]
        send_sems, recv_sems = refs[o0 + n_out + r_out + n_scr:]
        first = pl.program_id(0) == 0
        last = pl.program_id(0) == grid[0] - 1
        for ax in range(1, len(grid)):
            first = jnp.logical_and(first, pl.program_id(ax) == 0)
            last = jnp.logical_and(last, pl.program_id(ax) == grid[ax] - 1)

        @pl.when(first)
        def _():
            starts, _ = rider.plan(rins, routs, send_sems, recv_sems)
            for cp in starts:
                cp.start()

        body(*ins, *outs, *scr)

        @pl.when(last)
        def _():
            _, waits = rider.plan(rins, routs, send_sems, recv_sems)
            for wait in waits:
                wait()

    res = pl.pallas_call(
        hosted, name=name, grid=grid,
        in_specs=list(in_specs) + [ANY] * r_in, out_specs=list(out_specs) + [ANY] * r_out,
        out_shape=list(out_shape) + rider.out_shapes,
        input_output_aliases={n_in + k: n_out + v for k, v in rider.aliases.items()},
        scratch_shapes=list(scratch_shapes) + [pltpu.SemaphoreType.DMA((rider.nsem,)),
                                               pltpu.SemaphoreType.DMA((rider.nsem,))],
        compiler_params=_params(("arbitrary",) * len(grid)),
    )(*args, *rider.ins)
    return list(res[:n_out]), list(res[n_out:])


def _exchange(name, rider):
    r_in, r_out = len(rider.ins), len(rider.out_shapes)

    def body(*refs):
        rins, routs = refs[:r_in], refs[r_in:r_in + r_out]
        send_sems, recv_sems = refs[r_in + r_out:]
        starts, waits = rider.plan(rins, routs, send_sems, recv_sems)
        for cp in starts:
            cp.start()
        for wait in waits:
            wait()

    res = pl.pallas_call(
        body, name=name, in_specs=[ANY] * r_in, out_specs=[ANY] * r_out, out_shape=rider.out_shapes,
        input_output_aliases=dict(rider.aliases),
        scratch_shapes=[pltpu.SemaphoreType.DMA((rider.nsem,)), pltpu.SemaphoreType.DMA((rider.nsem,))],
    )(*rider.ins)
    return list(res)


def _same(arrays):
    return [jax.ShapeDtypeStruct(a.shape, a.dtype) for a in arrays]


class _SemsFrom:
    def __init__(self, sems, first):
        self.sems, self.first = sems, first

    @property
    def at(self):
        return self

    def __getitem__(self, k):
        return self.sems.at[self.first + k]


def _ride_both(r1, r2):
    n_in, n_out = len(r1.ins), len(r1.out_shapes)

    def plan(ins, outs, ss, rs):
        s1, w1 = r1.plan(ins[:n_in], outs[:n_out], ss, rs)
        s2, w2 = r2.plan(ins[n_in:], outs[n_out:], _SemsFrom(ss, r1.nsem), _SemsFrom(rs, r1.nsem))
        return s1 + s2, w1 + w2

    aliases = dict(r1.aliases)
    aliases.update({n_in + k: n_out + v for k, v in r2.aliases.items()})
    return Rider(plan, r1.ins + r2.ins, r1.out_shapes + r2.out_shapes, aliases, r1.nsem + r2.nsem)


def _ride_gather_ici(bases, part=0, nparts=1):
    n = len(bases)

    def plan(ins, outs, ss, rs):
        x, y, c, chips = _place()
        me = 2 * x + y
        starts, waits = [], []
        for a in range(n):
            half = outs[a].shape[1] // 2
            mine = pl.ds(c * half + part * (half // nparts), half // nparts)
            for j, chip in enumerate(chips):
                cp = _remote(3 * a + j, outs[a].at[me, mine], outs[a].at[me, mine], (*chip, c), ss, rs)
                got = outs[a].at[2 * chip[0] + chip[1], mine]
                starts.append(cp)
                waits += [cp.wait_send, _remote(3 * a + j, got, got, (*chip, c), ss, rs).wait_recv]
        return starts, waits

    return Rider(plan, bases, _same(bases), {a: a for a in range(n)}, 3 * n)


def _ride_gather_d2d(bases):
    n = len(bases)

    def plan(ins, outs, ss, rs):
        x, y, c, chips = _place()
        sib = (x, y, 1 - c)
        starts, waits = [], []
        for a in range(n):
            half = outs[a].shape[1] // 2
            for j, chip in enumerate(chips):
                frm = 2 * chip[0] + chip[1]
                got = outs[a].at[frm, pl.ds(c * half, half)]
                theirs = outs[a].at[frm, pl.ds((1 - c) * half, half)]
                cp = _remote(3 * a + j, got, got, sib, ss, rs)
                starts.append(cp)
                waits += [cp.wait_send, _remote(3 * a + j, theirs, theirs, sib, ss, rs).wait_recv]
        return starts, waits

    return Rider(plan, bases, _same(bases), {a: a for a in range(n)}, 3 * n)


def _ride_gather_small(bases):
    n = len(bases)

    def plan(ins, outs, ss, rs):
        x, y, c, chips = _place()
        me = 2 * x + y
        starts, waits = [], []
        for a in range(n):
            k = outs[a].shape[-1] // N_CHIPS
            lead = (slice(None),) * (len(outs[a].shape) - 1)
            at = (lambda s: pl.multiple_of(s * k, LANES)) if k % LANES == 0 else (lambda s: s * k)
            cols = lambda s: outs[a].at[lead + (pl.ds(at(s), k),)]
            for j, chip in enumerate(chips):
                cp = _remote(3 * a + j, cols(me), cols(me), (*chip, c), ss, rs)
                got = cols(2 * chip[0] + chip[1])
                starts.append(cp)
                waits += [cp.wait_send, _remote(3 * a + j, got, got, (*chip, c), ss, rs).wait_recv]
        return starts, waits

    return Rider(plan, bases, _same(bases), {a: a for a in range(n)}, 3 * n)


def _ride_swap_halves(gs):
    n = len(gs)

    def plan(ins, outs, ss, rs):
        x, y, c, _ = _place()
        cps = [_remote(a, ins[a].at[:, 1 - c], outs[a], (x, y, 1 - c), ss, rs) for a in range(n)]
        return cps, [cp.wait for cp in cps]

    shapes = [jax.ShapeDtypeStruct((g.shape[0],) + g.shape[2:], g.dtype) for g in gs]
    return Rider(plan, gs, shapes, {}, n)


def _ride_to_owners(ps):
    n = len(ps)

    def plan(ins, outs, ss, rs):
        x, y, c, chips = _place()
        cps = []
        for a in range(n):
            for j, chip in enumerate(chips):
                cps.append(_remote(3 * a + j, ins[a].at[2 * chip[0] + chip[1]], outs[a].at[j], (*chip, c), ss, rs))
        return cps, [cp.wait for cp in cps]

    shapes = [jax.ShapeDtypeStruct((3,) + p.shape[1:], p.dtype) for p in ps]
    return Rider(plan, ps, shapes, {}, 3 * n)


def _ride_join_halves(qs):
    n = len(qs)

    def plan(ins, outs, ss, rs):
        x, y, c, _ = _place()
        sib = (x, y, 1 - c)
        starts, waits = [], []
        for a in range(n):
            cp = _remote(a, outs[a].at[c], outs[a].at[c], sib, ss, rs)
            starts.append(cp)
            waits += [cp.wait_send, _remote(a, outs[a].at[1 - c], outs[a].at[1 - c], sib, ss, rs).wait_recv]
        return starts, waits

    return Rider(plan, qs, _same(qs), {a: a for a in range(n)}, n)


def _ride_gather_all(base):
    def plan(ins, outs, ss, rs):
        x, y, c, _ = _place()
        me = 4 * x + 2 * y + c

        def peer(k):
            return (1 - x if (k >> 2) & 1 else x, 1 - y if (k >> 1) & 1 else y, 1 - c if k & 1 else c)

        starts, waits = [], []
        for k in range(1, N_DEV):
            px, py, pc = peer(k)
            cp = _remote(k - 1, outs[0].at[me], outs[0].at[me], (px, py, pc), ss, rs)
            got = outs[0].at[4 * px + 2 * py + pc]
            starts.append(cp)
            waits += [cp.wait_send, _remote(k - 1, got, got, (px, py, pc), ss, rs).wait_recv]
        return starts, waits

    return Rider(plan, [base], _same([base]), {0: 0}, N_DEV - 1)


def _embed(x, meta, dm):
    dc = _col_tile(dm.D, 256)
    s, lp = dm.S, dm.Lp

    def body(x_ref, meta_ref, h_ref):
        h_ref[0:N_META, :] = meta_ref[...]
        h_ref[N_META:N_META + s, :] = x_ref[0]
        if lp > N_META + s:
            h_ref[N_META + s:lp, :] = jnp.zeros((lp - N_META - s, dc), F32)

    return pl.pallas_call(
        body, name="embed", grid=(dm.BL, dm.D // dc),
        in_specs=[pl.BlockSpec((1, s, dc), lambda b, j: (b, 0, j)),
                  pl.BlockSpec((N_META, dc), lambda b, j: (0, j))],
        out_specs=pl.BlockSpec((lp, dc), lambda b, j: (b, j)),
        out_shape=jax.ShapeDtypeStruct((dm.R, dm.D), F32),
        compiler_params=_params(("parallel", "parallel")),
    )(x, meta)


def _loss_head(h, target, dm):
    dc = _col_tile(dm.D, 256)
    s, lp, nj = dm.S, dm.Lp, dm.D // dc

    def body(h_ref, t_ref, dh_ref, l_ref):
        diff = h_ref[N_META:N_META + s, :] - t_ref[0]
        dh_ref[0:N_META, :] = jnp.zeros((N_META, dc), F32)
        dh_ref[N_META:N_META + s, :] = diff * (1.0 / dm.D)
        if lp > N_META + s:
            dh_ref[N_META + s:lp, :] = jnp.zeros((lp - N_META - s, dc), F32)
        l_ref[...] = jnp.full((8, LANES), (0.5 / dm.D) * jnp.sum(diff * diff), F32)

    dh, part = pl.pallas_call(
        body, name="loss_head", grid=(dm.BL, nj),
        in_specs=[pl.BlockSpec((lp, dc), lambda b, j: (b, j)),
                  pl.BlockSpec((1, s, dc), lambda b, j: (b, 0, j))],
        out_specs=[pl.BlockSpec((lp, dc), lambda b, j: (b, j)),
                   pl.BlockSpec((8, LANES), lambda b, j: (b * nj + j, 0))],
        out_shape=[jax.ShapeDtypeStruct((dm.R, dm.D), F32),
                   jax.ShapeDtypeStruct((dm.BL * nj * 8, LANES), F32)],
        compiler_params=_params(("parallel", "parallel")),
    )(h, target)
    return dh, jnp.sum(part[::8, 0])


def _unembed(dh, dm):
    dc = _col_tile(dm.D, 256)
    s, lp = dm.S, dm.Lp

    def body(dh_ref, gx_ref, gm_ref):
        gx_ref[0] = dh_ref[N_META:N_META + s, :]

        @pl.when(pl.program_id(1) == 0)
        def _():
            gm_ref[...] = dh_ref[0:N_META, :]

        @pl.when(pl.program_id(1) > 0)
        def _():
            gm_ref[...] = gm_ref[...] + dh_ref[0:N_META, :]

    return pl.pallas_call(
        body, name="unembed", grid=(dm.D // dc, dm.BL),
        in_specs=[pl.BlockSpec((lp, dc), lambda j, b: (b, j))],
        out_specs=[pl.BlockSpec((1, s, dc), lambda j, b: (b, 0, j)),
                   pl.BlockSpec((N_META, dc), lambda j, b: (0, j))],
        out_shape=[jax.ShapeDtypeStruct((dm.BL, s, dm.D), F32),
                   jax.ShapeDtypeStruct((N_META, dm.D), F32)],
        compiler_params=_params(("parallel", "arbitrary")),
    )(dh)


def _fwd_in(h, pre_g, w, dm, rider=None):
    tm = _row_tile(dm.R, 1088)
    tn = _col_tile(dm.NP, 896)
    nj = dm.NP // tn

    def body(h_ref, g_ref, w_ref, proj_ref, dt_ref, hn_ref):
        @pl.when(pl.program_id(1) == 0)
        def _():
            xf = h_ref[...]
            r = lax.rsqrt(jnp.mean(xf * xf, axis=-1, keepdims=True) + NORM_EPS)
            hn_ref[...] = (xf * r * g_ref[...]).astype(BF16)

        res = _dot(hn_ref[...], w_ref[...])
        proj_ref[...] = res.astype(BF16)

        @pl.when(pl.program_id(1) == nj - 1)
        def _():
            dt_ref[...] = res[:, tn - DT_PAD:tn]

    return _call(
        body, "fwd_in", (dm.R // tm, nj),
        [pl.BlockSpec((tm, dm.D), lambda i, j: (i, 0)),
         pl.BlockSpec((1, dm.D), lambda i, j: (0, 0)),
         pl.BlockSpec((dm.D, tn), lambda i, j: (0, j))],
        [pl.BlockSpec((tm, tn), lambda i, j: (i, j)),
         pl.BlockSpec((tm, DT_PAD), lambda i, j: (i, 0)),
         pl.BlockSpec((tm, dm.D), lambda i, j: (i, 0))],
        [jax.ShapeDtypeStruct((dm.R, dm.NP), BF16), jax.ShapeDtypeStruct((dm.R, DT_PAD), F32),
         jax.ShapeDtypeStruct((dm.R, dm.D), BF16)],
        [], ("parallel", "arbitrary"), (h, pre_g, w), rider)


def _fwd_out(ya, yb, yc, w_out, h, post_g, dm, rider=None):
    tm = _row_tile(dm.Lp, 544)
    tiles_per_seq = dm.Lp // tm
    da, db, dc = dm.DA, dm.DB, dm.DC

    def body(ya_ref, yb_ref, yc_ref, w_ref, h_ref, g_ref, hn_ref, m_ref):
        m = _dot(ya_ref[...], w_ref[0:da, :])
        m = m + _dot(yb_ref[...], w_ref[da:da + db, :])
        m = m + _dot(yc_ref[...], w_ref[da + db:da + db + dc, :])
        m_ref[...] = m
        r = lax.rsqrt(jnp.mean(m * m, axis=-1, keepdims=True) + NORM_EPS)
        t = (pl.program_id(0) % tiles_per_seq) * tm + lax.broadcasted_iota(jnp.int32, (tm, 1), 0)
        keep = (t < dm.L).astype(F32)
        hn_ref[...] = (h_ref[...] + m * r * g_ref[...]) * keep

    row = lambda i: (i, 0)
    fixed = lambda i: (0, 0)
    return _call(
        body, "fwd_out", (dm.R // tm,),
        [pl.BlockSpec((tm, da), row), pl.BlockSpec((tm, db), row), pl.BlockSpec((tm, dc), row),
         pl.BlockSpec((2 * dm.D, dm.D), fixed), pl.BlockSpec((tm, dm.D), row), pl.BlockSpec((1, dm.D), fixed)],
        [pl.BlockSpec((tm, dm.D), row), pl.BlockSpec((tm, dm.D), row)],
        [jax.ShapeDtypeStruct((dm.R, dm.D), F32), jax.ShapeDtypeStruct((dm.R, dm.D), F32)],
        [], ("parallel",), (ya, yb, yc, w_out, h, post_g), rider)


def _bwd_out(dh, m, post_g, w_out, ya, yb, yc, dm, rider=None):
    tm = _row_tile(dm.R, MXU_DIM)
    da, db, dc = dm.DA, dm.DB, dm.DC

    def body(dh_ref, m_ref, g_ref, w_ref, ya_ref, yb_ref, yc_ref, dya_ref, dyb_ref, dyc_ref, dw_ref, dg_ref):
        @pl.when(pl.program_id(0) == 0)
        def _():
            dw_ref[...] = jnp.zeros_like(dw_ref)
            dg_ref[...] = jnp.zeros_like(dg_ref)

        m = m_ref[...]
        dh_ = dh_ref[...]
        r = lax.rsqrt(jnp.mean(m * m, axis=-1, keepdims=True) + NORM_EPS)
        n = m * r
        dg_ref[0:1, :] = dg_ref[0:1, :] + jnp.sum(dh_ * n, axis=0, keepdims=True)
        dn = dh_ * g_ref[...]
        dm_ = (r * (dn - n * jnp.mean(dn * n, axis=-1, keepdims=True))).astype(BF16)
        dya_ref[...] = _dot_nt(dm_, w_ref[0:da, :])
        dyb_ref[...] = _dot_nt(dm_, w_ref[da:da + db, :])
        dyc_ref[...] = _dot_nt(dm_, w_ref[da + db:da + db + dc, :])
        dw_ref[0:da, :] = dw_ref[0:da, :] + _dot_tn(ya_ref[...], dm_)
        dw_ref[da:da + db, :] = dw_ref[da:da + db, :] + _dot_tn(yb_ref[...], dm_)
        dw_ref[da + db:da + db + dc, :] = dw_ref[da + db:da + db + dc, :] + _dot_tn(yc_ref[...], dm_)

    row = lambda i: (i, 0)
    fixed = lambda i: (0, 0)
    return _call(
        body, "bwd_out", (dm.R // tm,),
        [pl.BlockSpec((tm, dm.D), row), pl.BlockSpec((tm, dm.D), row), pl.BlockSpec((1, dm.D), fixed),
         pl.BlockSpec((2 * dm.D, dm.D), fixed),
         pl.BlockSpec((tm, da), row), pl.BlockSpec((tm, db), row), pl.BlockSpec((tm, dc), row)],
        [pl.BlockSpec((tm, da), row), pl.BlockSpec((tm, db), row), pl.BlockSpec((tm, dc), row),
         pl.BlockSpec((2 * dm.D, dm.D), fixed), pl.BlockSpec((8, dm.D), fixed)],
        [jax.ShapeDtypeStruct((dm.R, da), F32), jax.ShapeDtypeStruct((dm.R, db), F32),
         jax.ShapeDtypeStruct((dm.R, dc), F32),
         jax.ShapeDtypeStruct((2 * dm.D, dm.D), F32), jax.ShapeDtypeStruct((8, dm.D), F32)],
        [], ("arbitrary",), (dh, m, post_g, w_out, ya, yb, yc), rider)


def _bwd_in_dx(dpa, dpb, dpc, dpt, w, h, dh, pre_g, dm, rider=None):
    tm = _row_tile(dm.R, 272)
    wa, wb, wc = dm.WA, dm.WB, dm.WC

    def body(dpa_ref, dpb_ref, dpc_ref, dpt_ref, w_ref, h_ref, dh_ref, g_ref, out_ref, dg_ref):
        @pl.when(pl.program_id(0) == 0)
        def _():
            dg_ref[...] = jnp.zeros_like(dg_ref)

        dhn = _dot_nt(dpa_ref[...], w_ref[:, 0:wa])
        dhn = dhn + _dot_nt(dpb_ref[...], w_ref[:, wa:wa + wb])
        dhn = dhn + _dot_nt(dpc_ref[...], w_ref[:, wa + wb:wa + wb + wc])
        dhn = dhn + _dot_nt(dpt_ref[...], w_ref[:, wa + wb + wc:wa + wb + wc + DT_PAD])
        xf = h_ref[...]
        r = lax.rsqrt(jnp.mean(xf * xf, axis=-1, keepdims=True) + NORM_EPS)
        n = xf * r
        dg_ref[0:1, :] = dg_ref[0:1, :] + jnp.sum(dhn * n, axis=0, keepdims=True)
        dn = dhn * g_ref[...]
        out_ref[...] = dh_ref[...] + r * (dn - n * jnp.mean(dn * n, axis=-1, keepdims=True))

    row = lambda i: (i, 0)
    fixed = lambda i: (0, 0)
    return _call(
        body, "bwd_in_dx", (dm.R // tm,),
        [pl.BlockSpec((tm, wa), row), pl.BlockSpec((tm, wb), row), pl.BlockSpec((tm, wc), row),
         pl.BlockSpec((tm, DT_PAD), row), pl.BlockSpec((dm.D, dm.NP), fixed),
         pl.BlockSpec((tm, dm.D), row), pl.BlockSpec((tm, dm.D), row), pl.BlockSpec((1, dm.D), fixed)],
        [pl.BlockSpec((tm, dm.D), row), pl.BlockSpec((8, dm.D), fixed)],
        [jax.ShapeDtypeStruct((dm.R, dm.D), F32), jax.ShapeDtypeStruct((8, dm.D), F32)],
        [], ("arbitrary",), (dpa, dpb, dpc, dpt, w, h, dh, pre_g), rider)


def _bwd_in_dw(hn, dp, dm, piece):
    width = dp.shape[1]
    tn = _col_tile(width, 512)

    def body(hn_ref, dp_ref, dw_ref):
        dw_ref[...] = _dot_tn(hn_ref[...], dp_ref[...])

    return pl.pallas_call(
        body, name="bwd_in_dw_" + piece, grid=(width // tn,),
        in_specs=[pl.BlockSpec((dm.R, dm.D), lambda j: (0, 0)), pl.BlockSpec((dm.R, tn), lambda j: (0, j))],
        out_specs=pl.BlockSpec((dm.D, tn), lambda j: (0, j)),
        out_shape=jax.ShapeDtypeStruct((dm.D, width), F32),
        compiler_params=_params(("parallel",)),
    )(hn, dp)


def _tile_index(dm, reverse):
    if reverse:
        return lambda b, i: b * dm.NT + (dm.NT - 1 - i)
    return lambda b, i: b * dm.NT + i


def _halo_index(dm, rows):
    per_tile = TT // rows
    return lambda b, i: jnp.maximum((b * dm.NT + (dm.NT - 1 - i)) * per_tile - 1, 0)


HALO_BLOCK = 16


def _last_rows(x):
    return x.astype(F32)[HALO_BLOCK - SMALL_HALO:HALO_BLOCK]


MIX_A_ROWS = 288


def _mix_a_fwd(proj, conv_w, dm):
    da = dm.DA
    ta = _row_tile(dm.Lp, MIX_A_ROWS)
    nta = dm.Lp // ta
    ti = lambda b, i: b * nta + i

    def body(ab_ref, ac_ref, ax_ref, az_ref, w_ref, y_ref, pbuf):
        i = pl.program_id(1)

        @pl.when(i == 0)
        def _():
            pbuf[0:SMALL_HALO, :] = jnp.zeros((SMALL_HALO, da), F32)

        @pl.when(i > 0)
        def _():
            pbuf[0:SMALL_HALO, :] = pbuf[ta:ta + SMALL_HALO, :]

        for lb in range(da // LANES):
            cs = slice(lb * LANES, (lb + 1) * LANES)
            p = ac_ref[:, cs].astype(F32) * ax_ref[:, cs].astype(F32)
            pbuf[SMALL_HALO:SMALL_HALO + ta, cs] = p
            q = (w_ref[0:1, cs] * pbuf[6:6 + ta, cs] + w_ref[1:2, cs] * pbuf[7:7 + ta, cs] + w_ref[2:3, cs] * p)
            az = az_ref[:, cs].astype(F32)
            y_ref[:, cs] = (ab_ref[:, cs].astype(F32) * q * (az * _sigmoid(az))).astype(BF16)

    col = lambda k: pl.BlockSpec((ta, da), lambda b, i: (ti(b, i), k))
    return pl.pallas_call(
        body, name="mix_a_fwd", grid=(dm.BL, nta),
        in_specs=[col(0), col(1), col(2), col(3), pl.BlockSpec((CONV_A_K, da), lambda b, i: (0, 0))],
        out_specs=pl.BlockSpec((ta, da), lambda b, i: (ti(b, i), 0)),
        out_shape=jax.ShapeDtypeStruct((dm.R, da), BF16),
        scratch_shapes=[pltpu.VMEM((SMALL_HALO + ta, da), F32)],
        compiler_params=_params(("parallel", "arbitrary")),
    )(proj, proj, proj, proj, conv_w)


def _mix_a_bwd(proj, dya, conv_w, dm):
    da = dm.DA
    ta = _row_tile(dm.Lp, MIX_A_ROWS)
    nta = dm.Lp // ta
    ti = lambda b, i: b * nta + (nta - 1 - i)
    hi = lambda b, i: jnp.maximum((b * nta + (nta - 1 - i)) * (ta // HALO_BLOCK) - 1, 0)

    def body(ab_ref, ac_ref, ax_ref, az_ref, ach_ref, axh_ref, dy_ref, w_ref, dp_ref, dw_ref, pbuf, dqbuf):
        i = pl.program_id(1)
        halo_on = jnp.where(i == nta - 1, 0.0, 1.0)

        @pl.when(i == 0)
        def _():
            dw_ref[...] = jnp.zeros_like(dw_ref)
            dqbuf[ta:ta + SMALL_HALO, :] = jnp.zeros((SMALL_HALO, da), F32)

        @pl.when(i > 0)
        def _():
            dqbuf[ta:ta + SMALL_HALO, :] = dqbuf[0:SMALL_HALO, :]

        for lb in range(da // LANES):
            cs = slice(lb * LANES, (lb + 1) * LANES)
            pbuf[0:SMALL_HALO, cs] = (_last_rows(ach_ref[:, cs]) * _last_rows(axh_ref[:, cs])) * halo_on
            ac, ax, ab, az = (r[:, cs].astype(F32) for r in (ac_ref, ax_ref, ab_ref, az_ref))
            p = ac * ax
            pbuf[SMALL_HALO:SMALL_HALO + ta, cs] = p
            p1 = pbuf[7:7 + ta, cs]
            p2 = pbuf[6:6 + ta, cs]
            w0, w1, w2 = w_ref[0:1, cs], w_ref[1:2, cs], w_ref[2:3, cs]
            q = w0 * p2 + w1 * p1 + w2 * p
            sz, dsz = _silu_and_grad(az)
            dy = dy_ref[:, cs]
            t1 = dy * ab
            dq = t1 * sz
            dqbuf[0:ta, cs] = dq
            dpv = w2 * dq + w1 * dqbuf[1:1 + ta, cs] + w0 * dqbuf[2:2 + ta, cs]
            dp_ref[:, lb * LANES:(lb + 1) * LANES] = (dy * q * sz).astype(BF16)
            dp_ref[:, da + lb * LANES:da + (lb + 1) * LANES] = (dpv * ax).astype(BF16)
            dp_ref[:, 2 * da + lb * LANES:2 * da + (lb + 1) * LANES] = (dpv * ac).astype(BF16)
            dp_ref[:, 3 * da + lb * LANES:3 * da + (lb + 1) * LANES] = (t1 * q * dsz).astype(BF16)
            dw_ref[0, 0:1, cs] = dw_ref[0, 0:1, cs] + jnp.sum(dq * p2, axis=0, keepdims=True)
            dw_ref[0, 1:2, cs] = dw_ref[0, 1:2, cs] + jnp.sum(dq * p1, axis=0, keepdims=True)
            dw_ref[0, 2:3, cs] = dw_ref[0, 2:3, cs] + jnp.sum(dq * p, axis=0, keepdims=True)

    col = lambda k: pl.BlockSpec((ta, da), lambda b, i: (ti(b, i), k))
    halo = lambda k: pl.BlockSpec((HALO_BLOCK, da), lambda b, i: (hi(b, i), k))
    return pl.pallas_call(
        body, name="mix_a_bwd", grid=(dm.BL, nta),
        in_specs=[col(0), col(1), col(2), col(3), halo(1), halo(2),
                  pl.BlockSpec((ta, da), lambda b, i: (ti(b, i), 0)),
                  pl.BlockSpec((CONV_A_K, da), lambda b, i: (0, 0))],
        out_specs=[pl.BlockSpec((ta, dm.WA), lambda b, i: (ti(b, i), 0)),
                   pl.BlockSpec((1, 8, da), lambda b, i: (b, 0, 0))],
        out_shape=[jax.ShapeDtypeStruct((dm.R, dm.WA), BF16), jax.ShapeDtypeStruct((dm.BL, 8, da), F32)],
        scratch_shapes=[pltpu.VMEM((SMALL_HALO + ta, da), F32), pltpu.VMEM((ta + SMALL_HALO, da), F32)],
        compiler_params=_params(("parallel", "arbitrary")),
    )(proj, proj, proj, proj, proj, proj, dya, conv_w)


SUBLANES = 8
SHIFT_ROWS = TT + CONF_HALO - SUBLANES


TAP_ROWS = 64


def _split_lanes(buf, rows, val):
    for lb in range(val.shape[1] // LANES):
        buf[lb, rows, :] = val[:, lb * LANES:(lb + 1) * LANES]


def _join_lanes(buf):
    return jnp.concatenate([buf[lb] for lb in range(buf.shape[0])], axis=1)


def _fill_shifted(buf, shifted):
    def step(lb, carry):
        for r in range(1, SUBLANES):
            shifted[lb, r - 1, 0:SHIFT_ROWS, :] = buf[lb, r:r + SHIFT_ROWS, :]
        return carry

    lax.fori_loop(0, buf.shape[0], step, 0)


def _window(buf, shifted, d, r0, lb):
    r = d % SUBLANES
    rows = pl.ds(pl.multiple_of(r0 + (d - r), SUBLANES), TAP_ROWS)
    return buf[lb, rows, :] if r == 0 else shifted[lb, r - 1, rows, :]


def _tap_loop(nlb, body):
    per_lb = TT // TAP_ROWS

    def step(it, carry):
        lb = it // per_lb
        body(lb, pl.ds(pl.multiple_of(lb * LANES, LANES), LANES), pl.multiple_of((it % per_lb) * TAP_ROWS, TAP_ROWS))
        return carry

    lax.fori_loop(0, nlb * per_lb, step, 0)


TAP_CHAINS = 4


def _tree_sum(terms):
    sums = list(terms[:TAP_CHAINS])
    for n, t in enumerate(terms[TAP_CHAINS:]):
        sums[n % TAP_CHAINS] = sums[n % TAP_CHAINS] + t
    while len(sums) > 1:
        sums = [a + b for a, b in zip(sums[0::2], sums[1::2])] + ([sums[-1]] if len(sums) % 2 else [])
    return sums[0]


def _conf_conv(ubuf, ushift, w_ref, b_ref, u1buf):
    _fill_shifted(ubuf, ushift)

    def piece(lb, cs, r0):
        taps = [w_ref[k:k + 1, cs] * _window(ubuf, ushift, CONF_HALO - (CONF_K - 1) + k, r0, lb)
                for k in range(CONF_K)]
        u1buf[lb, pl.ds(r0, TAP_ROWS), :] = _tree_sum(taps) + b_ref[0:1, cs]

    _tap_loop(ubuf.shape[0], piece)


def _mix_c_fwd(proj, conv_w, conv_b, ln_g, ln_b, dm):
    dc = dm.DC
    nlb = dc // LANES
    c0 = (dm.WA + dm.WB) // dc
    ti = _tile_index(dm, False)

    def body(ca_ref, cg_ref, cz_ref, w_ref, b_ref, g_ref, be_ref, y_ref, ubuf, u1buf, ushift):
        i = pl.program_id(1)

        @pl.when(i == 0)
        def _():
            ubuf[:, 0:CONF_HALO, :] = jnp.zeros((nlb, CONF_HALO, LANES), F32)

        @pl.when(i > 0)
        def _():
            ubuf[:, 0:CONF_HALO, :] = ubuf[:, TT:TT + CONF_HALO, :]

        _split_lanes(ubuf, slice(CONF_HALO, CONF_HALO + TT),
                     ca_ref[...].astype(F32) * _sigmoid(cg_ref[...].astype(F32)))
        _conf_conv(ubuf, ushift, w_ref, b_ref, u1buf)
        u1 = _join_lanes(u1buf)
        mu = jnp.mean(u1, axis=-1, keepdims=True)
        xc = u1 - mu
        rstd = lax.rsqrt(jnp.mean(xc * xc, axis=-1, keepdims=True) + LN_EPS)
        u2 = xc * rstd * g_ref[...] + be_ref[...]
        cz = cz_ref[...].astype(F32)
        y_ref[...] = ((u2 * _sigmoid(u2)) * (cz * _sigmoid(cz))).astype(BF16)

    col = lambda k: pl.BlockSpec((TT, dc), lambda b, i: (ti(b, i), c0 + k))
    vec = pl.BlockSpec((1, dc), lambda b, i: (0, 0))
    return pl.pallas_call(
        body, name="mix_c_fwd", grid=(dm.BL, dm.NT),
        in_specs=[col(0), col(1), col(2), pl.BlockSpec((CONF_K, dc), lambda b, i: (0, 0)), vec, vec, vec],
        out_specs=pl.BlockSpec((TT, dc), lambda b, i: (ti(b, i), 0)),
        out_shape=jax.ShapeDtypeStruct((dm.R, dc), BF16),
        scratch_shapes=[pltpu.VMEM((nlb, CONF_HALO + TT, LANES), F32), pltpu.VMEM((nlb, TT, LANES), F32),
                        pltpu.VMEM((nlb, SUBLANES - 1, SHIFT_ROWS, LANES), F32)],
        compiler_params=_params(("parallel", "arbitrary")),
    )(proj, proj, proj, conv_w, conv_b, ln_g, ln_b)


def _mix_c_bwd(proj, dyc, conv_w, conv_b, ln_g, ln_b, dm):
    dc = dm.DC
    nlb = dc // LANES
    c0 = (dm.WA + dm.WB) // dc
    ti = _tile_index(dm, True)
    hi = _halo_index(dm, CONF_HALO)

    def body(ca_ref, cg_ref, cz_ref, cah_ref, cgh_ref, dy_ref, w_ref, b_ref, g_ref, be_ref,
             dp_ref, dw_ref, dv_ref, ubuf, u1buf, dubuf, du0buf, ushift, dshift, dwacc):
        i = pl.program_id(1)
        halo_on = jnp.where(i == dm.NT - 1, 0.0, 1.0)

        @pl.when(i == 0)
        def _():
            dwacc[...] = jnp.zeros_like(dwacc)
            dv_ref[...] = jnp.zeros_like(dv_ref)
            dubuf[:, TT:TT + CONF_HALO, :] = jnp.zeros((nlb, CONF_HALO, LANES), F32)

        @pl.when(i > 0)
        def _():
            dubuf[:, TT:TT + CONF_HALO, :] = dubuf[:, 0:CONF_HALO, :]

        _split_lanes(ubuf, slice(0, CONF_HALO),
                     cah_ref[...].astype(F32) * _sigmoid(cgh_ref[...].astype(F32)) * halo_on)
        sgg = _sigmoid(cg_ref[...].astype(F32))
        ca = ca_ref[...].astype(F32)
        _split_lanes(ubuf, slice(CONF_HALO, CONF_HALO + TT), ca * sgg)
        _conf_conv(ubuf, ushift, w_ref, b_ref, u1buf)
        u1 = _join_lanes(u1buf)
        mu = jnp.mean(u1, axis=-1, keepdims=True)
        xc = u1 - mu
        rstd = lax.rsqrt(jnp.mean(xc * xc, axis=-1, keepdims=True) + LN_EPS)
        xhat = xc * rstd
        u2 = xhat * g_ref[...] + be_ref[...]
        su, dsu = _silu_and_grad(u2)
        sz, dsz = _silu_and_grad(cz_ref[...].astype(F32))
        dy = dy_ref[...]
        du2 = dy * dsu * sz
        dp_ref[:, 2 * dc:3 * dc] = (dy * su * dsz).astype(BF16)
        dxhat = du2 * g_ref[...]
        du1 = rstd * (dxhat - jnp.mean(dxhat, axis=-1, keepdims=True)
                      - xhat * jnp.mean(dxhat * xhat, axis=-1, keepdims=True))
        dv_ref[0, 0:1, :] = dv_ref[0, 0:1, :] + jnp.sum(du1, axis=0, keepdims=True)
        dv_ref[0, 1:2, :] = dv_ref[0, 1:2, :] + jnp.sum(du2 * xhat, axis=0, keepdims=True)
        dv_ref[0, 2:3, :] = dv_ref[0, 2:3, :] + jnp.sum(du2, axis=0, keepdims=True)
        _split_lanes(dubuf, slice(0, TT), du1)
        _fill_shifted(dubuf, dshift)

        def piece(lb, cs, r0):
            du0buf[lb, pl.ds(r0, TAP_ROWS), :] = _tree_sum(
                [w_ref[k:k + 1, cs] * _window(dubuf, dshift, CONF_K - 1 - k, r0, lb) for k in range(CONF_K)])
            d1 = dubuf[lb, pl.ds(r0, TAP_ROWS), :]
            for k in range(CONF_K):
                prod = d1 * _window(ubuf, ushift, CONF_HALO - (CONF_K - 1) + k, r0, lb)
                dwacc[lb, k] = dwacc[lb, k] + jnp.sum(prod.reshape(TAP_ROWS // SUBLANES, SUBLANES, LANES), axis=0)

        _tap_loop(nlb, piece)
        du0 = _join_lanes(du0buf)
        dp_ref[:, 0:dc] = (du0 * sgg).astype(BF16)
        dp_ref[:, dc:2 * dc] = (du0 * ca * sgg * (1.0 - sgg)).astype(BF16)

        @pl.when(i == dm.NT - 1)
        def _():
            for lb in range(nlb):
                dw_ref[0, 0:CONF_K, lb * LANES:(lb + 1) * LANES] = jnp.sum(dwacc[lb], axis=1)
            dw_ref[0, CONF_K:CONF_K + 1, :] = jnp.zeros((1, dc), F32)

    col = lambda k: pl.BlockSpec((TT, dc), lambda b, i: (ti(b, i), c0 + k))
    halo = lambda k: pl.BlockSpec((CONF_HALO, dc), lambda b, i: (hi(b, i), c0 + k))
    vec = pl.BlockSpec((1, dc), lambda b, i: (0, 0))
    return pl.pallas_call(
        body, name="mix_c_bwd", grid=(dm.BL, dm.NT),
        in_specs=[col(0), col(1), col(2), halo(0), halo(1),
                  pl.BlockSpec((TT, dc), lambda b, i: (ti(b, i), 0)),
                  pl.BlockSpec((CONF_K, dc), lambda b, i: (0, 0)), vec, vec, vec],
        out_specs=[pl.BlockSpec((TT, dm.WC), lambda b, i: (ti(b, i), 0)),
                   pl.BlockSpec((1, 32, dc), lambda b, i: (b, 0, 0)),
                   pl.BlockSpec((1, 8, dc), lambda b, i: (b, 0, 0))],
        out_shape=[jax.ShapeDtypeStruct((dm.R, dm.WC), BF16),
                   jax.ShapeDtypeStruct((dm.BL, 32, dc), F32),
                   jax.ShapeDtypeStruct((dm.BL, 8, dc), F32)],
        scratch_shapes=[pltpu.VMEM((nlb, CONF_HALO + TT, LANES), F32), pltpu.VMEM((nlb, TT, LANES), F32),
                        pltpu.VMEM((nlb, TT + CONF_HALO, LANES), F32), pltpu.VMEM((nlb, TT, LANES), F32),
                        pltpu.VMEM((nlb, SUBLANES - 1, SHIFT_ROWS, LANES), F32),
                        pltpu.VMEM((nlb, SUBLANES - 1, SHIFT_ROWS, LANES), F32),
                        pltpu.VMEM((nlb, CONF_K, SUBLANES, LANES), F32)],
        compiler_params=_params(("parallel", "arbitrary")),
    )(proj, proj, proj, proj, proj, dyc, conv_w, conv_b, ln_g, ln_b)


def _ssm_conv(rbuf, w_ref, b_ref, width):
    for lb in range(width // LANES):
        cs = slice(lb * LANES, (lb + 1) * LANES)
        acc = jnp.broadcast_to(b_ref[0:1, cs], (TT, LANES))
        for k in range(SSM_CONV_K):
            off = SMALL_HALO - (SSM_CONV_K - 1) + k
            acc = acc + w_ref[k:k + 1, cs] * rbuf[off:off + TT, cs]
        yield cs, acc


def _softplus(z):
    return jnp.maximum(z, 0.0) + jnp.log(1.0 + jnp.exp(-jnp.abs(z)))


def _tri(lower):
    r = lax.broadcasted_iota(jnp.int32, (TT, TT), 0)
    c = lax.broadcasted_iota(jnp.int32, (TT, TT), 1)
    return (c <= r) if lower else (c >= r)


def _exact_01_dot(mat01, x):
    x1, x2, x3 = _split3(x)
    return _dot(mat01, x1) + _dot(mat01, x2) + _dot(mat01, x3)


def _head_scalars(dt_ref, dtb_ref, alog_ref):
    z = dt_ref[...] + dtb_ref[...]
    dtv = _softplus(z)
    a = -jnp.exp(alog_ref[...])
    ac = _exact_01_dot(_tri(True).astype(F32).astype(BF16), dtv * a)
    eac = jnp.exp(ac)
    dst = jnp.exp(ac[TT - 1:TT, :] - ac)
    return z, dtv, a, ac, eac, dst


def _decay(ac, ac_t, h, causal):
    seg = ac[:, h:h + 1] - ac_t[h:h + 1, :]
    return jnp.where(causal, jnp.exp(jnp.where(causal, seg, 0.0)), 0.0)


def _pair_mask(h):
    lane = lax.broadcasted_iota(jnp.int32, (1, LANES), 1)
    return ((lane >= SSM_HEAD_DIM) if (h % 2) else (lane < SSM_HEAD_DIM)).astype(F32)


def _mix_b_fwd(proj, projdt, conv_w, conv_b, dt_bias, a_log, dskx, norm_g, expand, dm, rider=None):
    db, gn, xbc_w, hpg = dm.DB, dm.GN, dm.XBC, dm.HPG
    gw = db // SSM_GROUPS
    ti = _tile_index(dm, False)

    def body(bz_ref, bx_ref, bc_ref, dt_ref, w_ref, b_ref, dtb_ref, alog_ref, dsk_ref, g_ref, e_ref,
             y_ref, yraw_ref, sprev_ref, rbuf, xbuf, state, ybuf, exbuf, xdtbuf):
        i = pl.program_id(1)

        @pl.when(i == 0)
        def _():
            rbuf[0:SMALL_HALO, :] = jnp.zeros((SMALL_HALO, xbc_w), F32)
            state[...] = jnp.zeros_like(state)

        @pl.when(i > 0)
        def _():
            rbuf[0:SMALL_HALO, :] = rbuf[TT:TT + SMALL_HALO, :]

        rbuf[SMALL_HALO:SMALL_HALO + TT, 0:db] = bx_ref[...].astype(F32)
        rbuf[SMALL_HALO:SMALL_HALO + TT, db:xbc_w] = bc_ref[...].astype(F32)
        for cs, pre in _ssm_conv(rbuf, w_ref, b_ref, xbc_w):
            xbuf[:, cs] = pre * _sigmoid(pre)

        _, dtv, _, ac, eac, dst = _head_scalars(dt_ref, dtb_ref, alog_ref)
        exbuf[...] = _dot(jnp.concatenate([dtv, eac, dst], axis=0).astype(BF16), e_ref[...])
        ac_t = ac.T
        causal = _tri(True)
        sprev_ref[0, 0] = state[...]

        xdtbuf[...] = xbuf[:, 0:db] * exbuf[0:TT, :]
        ybuf[...] = xbuf[:, 0:db] * dsk_ref[...]
        for g in range(SSM_GROUPS):
            gs = slice(g * gw, (g + 1) * gw)
            bg = xbuf[:, db + g * SSM_STATE:db + (g + 1) * SSM_STATE].astype(BF16)
            cg = xbuf[:, db + gn + g * SSM_STATE:db + gn + (g + 1) * SSM_STATE].astype(BF16)
            cb = _dot_nt(cg, bg)
            for e in range(0, hpg, 2):
                h = g * hpg + e
                ps = slice(h * SSM_HEAD_DIM, (h + 2) * SSM_HEAD_DIM)
                xp = xdtbuf[:, ps]
                acc = jnp.zeros((TT, LANES), F32)
                for hh in (h, h + 1):
                    mm = (cb * _decay(ac, ac_t, hh, causal)).astype(BF16)
                    acc = acc + _dot(mm, (xp * _pair_mask(hh)).astype(BF16))
                ybuf[:, ps] = ybuf[:, ps] + acc
            sg = state[:, gs]
            ybuf[:, gs] = ybuf[:, gs] + exbuf[TT:2 * TT, gs] * _dot(cg, sg.astype(BF16))
            state[:, gs] = sg * exbuf[2 * TT - 1:2 * TT, gs] + _dot_tn(
                bg, (xdtbuf[:, gs] * exbuf[2 * TT:3 * TT, gs]).astype(BF16))

        yraw = ybuf[...]
        yraw_ref[...] = yraw
        bz = bz_ref[...].astype(F32)
        v = yraw * (bz * _sigmoid(bz))
        r = lax.rsqrt(jnp.mean(v * v, axis=-1, keepdims=True) + NORM_EPS)
        y_ref[...] = (v * r * g_ref[...]).astype(BF16)

    tile = lambda w, k: pl.BlockSpec((TT, w), lambda b, i: (ti(b, i), k))
    fixed = lambda r, w: pl.BlockSpec((r, w), lambda b, i: (0, 0))
    return _call(
        body, "mix_b_fwd", (dm.BL, dm.NT),
        [tile(db, dm.WA // db), tile(db, dm.WA // db + 1), tile(2 * gn, (dm.WA + 2 * db) // (2 * gn)),
         tile(DT_PAD, 0),
         fixed(SSM_CONV_K, xbc_w), fixed(1, xbc_w), fixed(1, DT_PAD), fixed(1, DT_PAD),
         fixed(1, db), fixed(1, db), fixed(DT_PAD, db)],
        [pl.BlockSpec((TT, db), lambda b, i: (ti(b, i), 0)),
         pl.BlockSpec((TT, db), lambda b, i: (ti(b, i), 0)),
         pl.BlockSpec((1, 1, SSM_STATE, db), lambda b, i: (b, i, 0, 0))],
        [jax.ShapeDtypeStruct((dm.R, db), BF16), jax.ShapeDtypeStruct((dm.R, db), F32),
         jax.ShapeDtypeStruct((dm.BL, dm.NT, SSM_STATE, db), F32)],
        [pltpu.VMEM((SMALL_HALO + TT, xbc_w), F32), pltpu.VMEM((TT, xbc_w), F32),
         pltpu.VMEM((SSM_STATE, db), F32), pltpu.VMEM((TT, db), F32),
         pltpu.VMEM((3 * TT, db), F32), pltpu.VMEM((TT, db), F32)],
        ("parallel", "arbitrary"),
        (proj, proj, proj, projdt, conv_w, conv_b, dt_bias, a_log, dskx, norm_g, expand), rider)


def _mix_b_bwd(proj, projdt, dyb, yraw, sprev, conv_w, conv_b, dt_bias, a_log, dskx, norm_g, expand, expand_t, dm,
               rider=None):
    db, gn, xbc_w, hpg = dm.DB, dm.GN, dm.XBC, dm.HPG
    gw = db // SSM_GROUPS
    ti = _tile_index(dm, True)
    hi = _halo_index(dm, HALO_BLOCK)

    def body(bz_ref, bx_ref, bc_ref, dt_ref, bxh_ref, bch_ref, dy_ref, yraw_ref, sprev_ref,
             w_ref, b_ref, dtb_ref, alog_ref, dsk_ref, g_ref, e_ref, et_ref,
             dp_ref, dpt_ref, dwc_ref, dch_ref, dhd_ref,
             rbuf, xbuf, dsbuf, dstate, dxbuf, z1buf, dprebuf, exbuf, xdtbuf, dyrbuf, uvec):
        i = pl.program_id(1)
        halo_on = jnp.where(i == dm.NT - 1, 0.0, 1.0)

        @pl.when(i == 0)
        def _():
            dwc_ref[...] = jnp.zeros_like(dwc_ref)
            dch_ref[...] = jnp.zeros_like(dch_ref)
            dhd_ref[...] = jnp.zeros_like(dhd_ref)
            dstate[...] = jnp.zeros_like(dstate)
            dprebuf[TT:TT + SMALL_HALO, :] = jnp.zeros((SMALL_HALO, xbc_w), F32)

        @pl.when(i > 0)
        def _():
            dprebuf[TT:TT + SMALL_HALO, :] = dprebuf[0:SMALL_HALO, :]

        rbuf[0:SMALL_HALO, 0:db] = _last_rows(bxh_ref[...]) * halo_on
        rbuf[0:SMALL_HALO, db:xbc_w] = _last_rows(bch_ref[...]) * halo_on
        rbuf[SMALL_HALO:SMALL_HALO + TT, 0:db] = bx_ref[...].astype(F32)
        rbuf[SMALL_HALO:SMALL_HALO + TT, db:xbc_w] = bc_ref[...].astype(F32)
        for cs, pre in _ssm_conv(rbuf, w_ref, b_ref, xbc_w):
            sl, dsl = _silu_and_grad(pre)
            xbuf[:, cs] = sl
            dsbuf[:, cs] = dsl

        z, dtv, a, ac, eac, dst = _head_scalars(dt_ref, dtb_ref, alog_ref)
        exbuf[...] = _dot(jnp.concatenate([dtv, eac, dst], axis=0).astype(BF16), e_ref[...])
        ac_t = ac.T
        causal = _tri(True)
        xdtbuf[...] = xbuf[:, 0:db] * exbuf[0:TT, :]

        yraw = yraw_ref[...]
        sz, dsz = _silu_and_grad(bz_ref[...].astype(F32))
        v = yraw * sz
        r = lax.rsqrt(jnp.mean(v * v, axis=-1, keepdims=True) + NORM_EPS)
        dy = dy_ref[...]
        dyg = dy * g_ref[...]
        dv = r * dyg - v * (r * r * r * jnp.mean(dyg * v, axis=-1, keepdims=True))
        dch_ref[0, 0:1, :] = dch_ref[0, 0:1, :] + jnp.sum(dy * v * r, axis=0, keepdims=True)
        dyr = dv * sz
        dyrbuf[...] = dyr
        dp_ref[:, 0:db] = (dv * yraw * dsz).astype(BF16)
        dch_ref[0, 1:2, :] = dch_ref[0, 1:2, :] + jnp.sum(dyr * xbuf[:, 0:db], axis=0, keepdims=True)

        lane_row = lax.broadcasted_iota(jnp.int32, (1, LANES), 1)
        sub_col = lax.broadcasted_iota(jnp.int32, (LANES, 1), 0)
        dac = jnp.zeros((TT, LANES), F32)
        colacc = jnp.zeros((LANES, TT), F32)
        for g in range(SSM_GROUPS):
            gs = slice(g * gw, (g + 1) * gw)
            bs_ = slice(db + g * SSM_STATE, db + (g + 1) * SSM_STATE)
            cs_ = slice(db + gn + g * SSM_STATE, db + gn + (g + 1) * SSM_STATE)
            bg = xbuf[:, bs_].astype(BF16)
            cg = xbuf[:, cs_].astype(BF16)
            cb = _dot_nt(cg, bg)
            dcb = jnp.zeros((TT, TT), F32)
            for e in range(0, hpg, 2):
                h = g * hpg + e
                ps = slice(h * SSM_HEAD_DIM, (h + 2) * SSM_HEAD_DIM)
                xp16 = xdtbuf[:, ps].astype(BF16)
                dyp = dyrbuf[:, ps]
                acc = jnp.zeros((TT, LANES), F32)
                for hh in (h, h + 1):
                    dec = _decay(ac, ac_t, hh, causal)
                    mm = cb * dec
                    dyh = (dyp * _pair_mask(hh)).astype(BF16)
                    dmm = _dot_nt(dyh, xp16)
                    acc = acc + _dot_tn(mm.astype(BF16), dyh)
                    dcb = dcb + dmm * dec
                    gm = dmm * mm
                    dac = dac + jnp.sum(gm, axis=1, keepdims=True) * (lane_row == hh).astype(F32)
                    colacc = colacc + (sub_col == hh).astype(F32) * jnp.sum(gm, axis=0, keepdims=True)
                dxbuf[:, ps] = acc
            sg32 = sprev_ref[0, 0, :, gs]
            sg = sg32.astype(BF16)
            dsn = dstate[:, gs]
            dsn16 = dsn.astype(BF16)
            dcb16 = dcb.astype(BF16)
            eacx = exbuf[TT:2 * TT, gs]
            dstx = exbuf[2 * TT:3 * TT, gs]
            cdx = exbuf[2 * TT - 1:2 * TT, gs]
            dye16 = (dyrbuf[:, gs] * eacx).astype(BF16)
            xdt_g = xdtbuf[:, gs]
            dxbuf[:, cs_] = _dot(dcb16, bg) + _dot_nt(dye16, sg)
            dst_x = dstx * _dot(bg, dsn16)
            dxbuf[:, bs_] = _dot_tn(dcb16, cg) + _dot_nt((dstx * xdt_g).astype(BF16), dsn16)
            dstate[:, gs] = cdx * dsn + _dot_tn(cg, dye16)
            z1buf[:, gs] = dyrbuf[:, gs] * (eacx * _dot(cg, sg)) - xdt_g * dst_x
            uvec[:, gs] = jnp.broadcast_to(
                jnp.sum(xdt_g * dst_x, axis=0, keepdims=True) + jnp.sum(dsn * cdx * sg32, axis=0, keepdims=True),
                (8, gw))
            dxbuf[:, gs] = dxbuf[:, gs] + dst_x

        zz = _dot(jnp.concatenate([z1buf[...], dxbuf[:, 0:db] * xbuf[:, 0:db]], axis=0).astype(BF16), et_ref[...])
        u1, u2, u3 = _split3(uvec[...])
        ulast = (_dot(u1, et_ref[...]) + _dot(u2, et_ref[...]) + _dot(u3, et_ref[...]))[0:1, :]
        is_last = (lax.broadcasted_iota(jnp.int32, (TT, 1), 0) == TT - 1).astype(F32)
        dac = dac - colacc.T + zz[0:TT] + is_last * ulast
        dda = _exact_01_dot(_tri(False).astype(F32).astype(BF16), dac)
        ddt = dda * a + zz[TT:2 * TT]
        dhd_ref[0, 1:2, :] = dhd_ref[0, 1:2, :] + jnp.sum(dda * dtv, axis=0, keepdims=True) * a
        ddtraw = ddt * _sigmoid(z)
        dhd_ref[0, 0:1, :] = dhd_ref[0, 0:1, :] + jnp.sum(ddtraw, axis=0, keepdims=True)
        dpt_ref[...] = ddtraw.astype(BF16)
        dxbuf[:, 0:db] = dyrbuf[...] * dsk_ref[...] + dxbuf[:, 0:db] * exbuf[0:TT, :]

        for lb in range(xbc_w // LANES):
            cs = slice(lb * LANES, (lb + 1) * LANES)
            dpre = dxbuf[:, cs] * dsbuf[:, cs]
            dprebuf[0:TT, cs] = dpre
            dwc_ref[0, SSM_CONV_K:SSM_CONV_K + 1, cs] = dwc_ref[0, SSM_CONV_K:SSM_CONV_K + 1, cs] + jnp.sum(
                dpre, axis=0, keepdims=True)
            draw = w_ref[SSM_CONV_K - 1:SSM_CONV_K, cs] * dpre
            for k in range(SSM_CONV_K - 1):
                sh = SSM_CONV_K - 1 - k
                draw = draw + w_ref[k:k + 1, cs] * dprebuf[sh:sh + TT, cs]
            for k in range(SSM_CONV_K):
                off = SMALL_HALO - (SSM_CONV_K - 1) + k
                dwc_ref[0, k:k + 1, cs] = dwc_ref[0, k:k + 1, cs] + jnp.sum(
                    dpre * rbuf[off:off + TT, cs], axis=0, keepdims=True)
            dp_ref[:, db + lb * LANES:db + (lb + 1) * LANES] = draw.astype(BF16)

    tile = lambda w, k: pl.BlockSpec((TT, w), lambda b, i: (ti(b, i), k))
    halo = lambda w, k: pl.BlockSpec((HALO_BLOCK, w), lambda b, i: (hi(b, i), k))
    fixed = lambda r, w: pl.BlockSpec((r, w), lambda b, i: (0, 0))
    kz = dm.WA // db
    kc = (dm.WA + 2 * db) // (2 * gn)
    return _call(
        body, "mix_b_bwd", (dm.BL, dm.NT),
        [tile(db, kz), tile(db, kz + 1), tile(2 * gn, kc), tile(DT_PAD, 0),
         halo(db, kz + 1), halo(2 * gn, kc),
         pl.BlockSpec((TT, db), lambda b, i: (ti(b, i), 0)),
         pl.BlockSpec((TT, db), lambda b, i: (ti(b, i), 0)),
         pl.BlockSpec((1, 1, SSM_STATE, db), lambda b, i: (b, dm.NT - 1 - i, 0, 0)),
         fixed(SSM_CONV_K, xbc_w), fixed(1, xbc_w), fixed(1, DT_PAD), fixed(1, DT_PAD),
         fixed(1, db), fixed(1, db), fixed(DT_PAD, db), fixed(db, DT_PAD)],
        [pl.BlockSpec((TT, dm.WB), lambda b, i: (ti(b, i), 0)),
         pl.BlockSpec((TT, DT_PAD), lambda b, i: (ti(b, i), 0)),
         pl.BlockSpec((1, 8, xbc_w), lambda b, i: (b, 0, 0)),
         pl.BlockSpec((1, 8, db), lambda b, i: (b, 0, 0)),
         pl.BlockSpec((1, 8, DT_PAD), lambda b, i: (b, 0, 0))],
        [jax.ShapeDtypeStruct((dm.R, dm.WB), BF16), jax.ShapeDtypeStruct((dm.R, DT_PAD), BF16),
         jax.ShapeDtypeStruct((dm.BL, 8, xbc_w), F32), jax.ShapeDtypeStruct((dm.BL, 8, db), F32),
         jax.ShapeDtypeStruct((dm.BL, 8, DT_PAD), F32)],
        [pltpu.VMEM((SMALL_HALO + TT, xbc_w), F32), pltpu.VMEM((TT, xbc_w), F32),
         pltpu.VMEM((TT, xbc_w), F32), pltpu.VMEM((SSM_STATE, db), F32),
         pltpu.VMEM((TT, xbc_w), F32), pltpu.VMEM((TT, db), F32),
         pltpu.VMEM((TT + SMALL_HALO, xbc_w), F32), pltpu.VMEM((3 * TT, db), F32),
         pltpu.VMEM((TT, db), F32), pltpu.VMEM((TT, db), F32), pltpu.VMEM((8, db), F32)],
        ("parallel", "arbitrary"),
        (proj, proj, proj, projdt, proj, proj, dyb, yraw, sprev,
         conv_w, conv_b, dt_bias, a_log, dskx, norm_g, expand, expand_t), rider)


def _head_consts(dm):
    head_of = jnp.arange(dm.DB) // SSM_HEAD_DIM
    expand = (jnp.arange(DT_PAD)[:, None] == head_of[None, :]).astype(BF16)
    return expand, expand.T


def _ssm_params(lw, dm):
    pad_h = lambda v: jnp.pad(v, (0, DT_PAD - dm.H))[None]
    return (lw["ssm_conv_w"], lw["ssm_conv_b"][None], pad_h(lw["dt_bias"]), pad_h(lw["a_log"]),
            jnp.repeat(lw["d_skip"], SSM_HEAD_DIM)[None], lw["ssm_norm_g"][None])


def _layer_fwd(h, lw, w_in, w_out, cst, dm, next_bases=None):
    nxt = next_bases is not None
    (proj, projdt, hn), got = _fwd_in(h, lw["pre_g"][None], w_in, dm,
                                      _ride_gather_ici(next_bases, 0, 2) if nxt else None)
    ya = _mix_a_fwd(proj, lw["conv_a_w"], dm)
    (yb, yraw, sprev), got = _mix_b_fwd(proj, projdt, *_ssm_params(lw, dm), cst[0], dm,
                                        _ride_gather_ici(got, 1, 2) if nxt else None)
    yc = _mix_c_fwd(proj, lw["conf_conv_w"], lw["conf_conv_b"][None], lw["conf_ln_g"][None],
                    lw["conf_ln_b"][None], dm)
    (h_new, m), got = _fwd_out(ya, yb, yc, w_out, h, lw["post_g"][None], dm, _ride_gather_d2d(got) if nxt else None)
    return h_new, (h, hn, proj, projdt, ya, yb, yc, yraw, sprev, m), got


def _layer_bwd(dh, saved, lw, w_in, w_out, cst, dm, reduce=None, last=False):
    h_in, hn, proj, projdt, ya, yb, yc, yraw, sprev, m = saved
    (dya, dyb, dyc, dwo, dpost), got = _bwd_out(dh, m, lw["post_g"][None], w_out, ya, yb, yc, dm,
                                                None if reduce is None else reduce.swap())
    dpa, dwa = _mix_a_bwd(proj, dya, lw["conv_a_w"], dm)
    (dpb, dpt, dwcv, dch, dhd), got = _mix_b_bwd(proj, projdt, dyb, yraw, sprev, *_ssm_params(lw, dm), cst[0],
                                                 cst[1], dm, None if reduce is None else reduce.to_owners(got))
    dpc, dwcf, dvc = _mix_c_bwd(proj, dyc, lw["conf_conv_w"], lw["conf_conv_b"][None], lw["conf_ln_g"][None],
                                lw["conf_ln_b"][None], dm)
    def own_reduce():
        pieces = [_bwd_in_dw(hn, dp, dm, n) for dp, n in ((dpa, "a"), (dpb, "b"), (dpc, "c"), (dpt, "dt"))]
        return _GradReduce([_grad_to_shards(pieces, dm), dwo.reshape(N_CHIPS, 2 * dm.D // N_CHIPS, dm.D)])

    rider = None if reduce is None else reduce.join(got)
    n_join = 0 if rider is None else len(rider.out_shapes)
    if last:
        mine = own_reduce()
        to_owners = mine.to_owners(_exchange("grad_swap_halves", mine.swap()))
        rider = to_owners if rider is None else _ride_both(rider, to_owners)
    (dh, dpre), got = _bwd_in_dx(dpa, dpb, dpc, dpt, w_in, h_in, dh, lw["pre_g"][None], dm, rider)
    if reduce is not None:
        reduce.finish(got[:n_join])
    if last:
        mine.finish(_exchange("grad_join_halves", mine.join(got[n_join:])))
    else:
        mine = own_reduce()
    dwcv, dch, dhd, dvc = (jnp.sum(a, axis=0) for a in (dwcv, dch, dhd, dvc))
    small = dict(pre_g=dpre[0], post_g=dpost[0], conv_a_w=jnp.sum(dwa, axis=0)[:CONV_A_K],
                 ssm_conv_w=dwcv[:SSM_CONV_K], ssm_conv_b=dwcv[SSM_CONV_K], ssm_norm_g=dch[0],
                 d_skip=jnp.sum(dch[1].reshape(dm.H, SSM_HEAD_DIM), axis=1), dt_bias=dhd[0, :dm.H],
                 a_log=dhd[1, :dm.H], conf_conv_w=jnp.sum(dwcf, axis=0)[:CONF_K], conf_conv_b=dvc[0],
                 conf_ln_g=dvc[1], conf_ln_b=dvc[2])
    return dh, mine, small


def _shard_runs(dm):
    ab = dm.WA + dm.WB
    order = [(0, 0, ab), (ab, dm.NP - DT_PAD, dm.H), (ab + dm.H, ab, dm.WC)]
    k = dm.NIN // N_CHIPS
    runs = []
    for s in range(N_CHIPS):
        for o0, m0, wd in order:
            lo, hi = max(o0, s * k), min(o0 + wd, (s + 1) * k)
            if lo < hi:
                runs.append((s, lo - s * k, m0 + lo - o0, hi - lo))
    return runs


def _w_in_from_shards(base, dm):
    tr = _row_tile(dm.D, 256)
    k = dm.NIN // N_CHIPS
    runs = _shard_runs(dm)

    def body(in_ref, out_ref):
        for s, sc, mc, wd in runs:
            out_ref[:, mc:mc + wd] = in_ref[s, :, sc:sc + wd]
        out_ref[:, dm.NP - DT_PAD + dm.H:dm.NP] = jnp.zeros((tr, DT_PAD - dm.H), BF16)

    return pl.pallas_call(
        body, name="w_in_from_shards", grid=(dm.D // tr,),
        in_specs=[pl.BlockSpec((N_CHIPS, tr, k), lambda r: (0, r, 0))],
        out_specs=pl.BlockSpec((tr, dm.NP), lambda r: (r, 0)),
        out_shape=jax.ShapeDtypeStruct((dm.D, dm.NP), BF16),
        compiler_params=_params(("parallel",)),
    )(base)


def _grad_to_shards(pieces, dm):
    tr = _row_tile(dm.D, 256)
    k = dm.NIN // N_CHIPS
    starts = [0, dm.WA, dm.WA + dm.WB, dm.NP - DT_PAD]
    widths = [dm.WA, dm.WB, dm.WC, DT_PAD]
    runs = _shard_runs(dm)

    def body(a_ref, b_ref, c_ref, t_ref, out_ref):
        refs = (a_ref, b_ref, c_ref, t_ref)
        for s, sc, mc, wd in runs:
            for p in range(4):
                lo, hi = max(mc, starts[p]), min(mc + wd, starts[p] + widths[p])
                if lo < hi:
                    out_ref[s, :, sc + lo - mc:sc + hi - mc] = refs[p][:, lo - starts[p]:hi - starts[p]].astype(BF16)

    return pl.pallas_call(
        body, name="grad_to_shards", grid=(dm.D // tr,),
        in_specs=[pl.BlockSpec((tr, w), lambda r: (r, 0)) for w in widths],
        out_specs=pl.BlockSpec((N_CHIPS, tr, k), lambda r: (0, r, 0)),
        out_shape=jax.ShapeDtypeStruct((N_CHIPS, dm.D, k), BF16),
        compiler_params=_params(("parallel",)),
    )(*pieces)


def _place_own(w, me):
    rows, cols = w.shape
    tr = _row_tile(rows, 256)

    def body(me_ref, w_ref, out_ref):
        out_ref[0] = w_ref[...].astype(BF16)

    return pl.pallas_call(
        body, name="place_own",
        grid_spec=pltpu.PrefetchScalarGridSpec(
            num_scalar_prefetch=1, grid=(rows // tr,),
            in_specs=[pl.BlockSpec((tr, cols), lambda r, me_ref: (r, 0))],
            out_specs=pl.BlockSpec((1, tr, cols), lambda r, me_ref: (me_ref[0], r, 0))),
        out_shape=jax.ShapeDtypeStruct((N_CHIPS, rows, cols), BF16),
        compiler_params=_params(("parallel",)),
    )(me, w)


def _add_halves(g, got, c, name):
    _, _, rows, cols = g.shape
    tr = _row_tile(rows, 256)

    def body(c_ref, g_ref, got_ref, out_ref):
        out_ref[0] = (g_ref[0, 0].astype(F32) + got_ref[0].astype(F32)).astype(BF16)

    return pl.pallas_call(
        body, name=name,
        grid_spec=pltpu.PrefetchScalarGridSpec(
            num_scalar_prefetch=1, grid=(N_CHIPS, rows // tr),
            in_specs=[pl.BlockSpec((1, 1, tr, cols), lambda s, r, c_ref: (s, c_ref[0], r, 0)),
                      pl.BlockSpec((1, tr, cols), lambda s, r, c_ref: (s, r, 0))],
            out_specs=pl.BlockSpec((1, tr, cols), lambda s, r, c_ref: (s, r, 0))),
        out_shape=jax.ShapeDtypeStruct((N_CHIPS, rows, cols), BF16),
        compiler_params=_params(("parallel", "parallel")),
    )(c, g, got)


def _add_owner(p, got, where, name):
    _, rows, cols = p.shape
    tr = _row_tile(rows, 256)

    def body(w_ref, p_ref, got_ref, out_ref):
        acc = p_ref[0].astype(F32)
        for j in range(3):
            acc = acc + got_ref[j].astype(F32)
        out_ref[0] = acc

    return pl.pallas_call(
        body, name=name,
        grid_spec=pltpu.PrefetchScalarGridSpec(
            num_scalar_prefetch=1, grid=(rows // tr,),
            in_specs=[pl.BlockSpec((1, tr, cols), lambda r, w_ref: (w_ref[0], r, 0)),
                      pl.BlockSpec((3, tr, cols), lambda r, w_ref: (0, r, 0))],
            out_specs=pl.BlockSpec((1, tr, cols), lambda r, w_ref: (w_ref[1], r, 0))),
        out_shape=jax.ShapeDtypeStruct((2, rows, cols), F32),
        compiler_params=_params(("parallel",)),
    )(where, p, got)


class _GradReduce:
    def __init__(self, gs):
        self.gs = [g.reshape((N_CHIPS, 2, g.shape[1] // 2) + g.shape[2:]) for g in gs]
        self.c = lax.axis_index("c").astype(jnp.int32).reshape(1)
        chip = (2 * lax.axis_index("x") + lax.axis_index("y")).astype(jnp.int32)
        self.where = jnp.stack([chip, self.c[0]])
        self.result = None

    def swap(self):
        return _ride_swap_halves(self.gs)

    def to_owners(self, got):
        self.ps = [_add_halves(g, r, self.c, "grad_add_sibling_" + n) for g, r, n in zip(self.gs, got, ("in", "out"))]
        return _ride_to_owners(self.ps)

    def join(self, got):
        qs = [_add_owner(p, r, self.where, "grad_add_chips_" + n) for p, r, n in zip(self.ps, got, ("in", "out"))]
        return _ride_join_halves(qs)

    def finish(self, got):
        self.result = [a.reshape((a.shape[0] * a.shape[1],) + a.shape[2:]) for a in got]


def _adamw_math(w, g, m, v):
    m = ADAM_B1 * m + (1.0 - ADAM_B1) * g
    v = ADAM_B2 * v + (1.0 - ADAM_B2) * (g * g)
    m_hat = m / (1.0 - ADAM_B1 ** ADAM_STEP)
    v_hat = v / (1.0 - ADAM_B2 ** ADAM_STEP)
    delta = -ADAM_LR * (m_hat / (jnp.sqrt(v_hat) + ADAM_EPS) + ADAM_WD * w)
    return delta, m, v


def _adamw_small(w, g, m, v, name):
    def body(w_ref, g_ref, m_ref, v_ref, d_out, m_out, v_out):
        d_out[...], m_out[...], v_out[...] = _adamw_math(w_ref[...], g_ref[...], m_ref[...], v_ref[...])

    shape = jax.ShapeDtypeStruct(w.shape, F32)
    return pl.pallas_call(body, name="adamw_" + name, out_shape=[shape, shape, shape],
                          compiler_params=_params())(w, g, m, v)


def _adamw_layer(i, w, g, m, v, prev, name):
    depth, rows, cols = w.shape
    tr = _row_tile(rows, 256)
    n_prev = 0 if prev is None else 4

    def body(*refs):
        w_ref, g_ref, m_ref, v_ref = refs[:4]
        g_out, d_out, m_out, v_out = refs[4 + n_prev:]
        gv = g_ref[...]
        g_out[0] = gv
        d_out[0], m_out[0], v_out[0] = _adamw_math(w_ref[0], gv, m_ref[0], v_ref[0])

    lay = pl.BlockSpec((1, tr, cols), lambda r: (i, r, 0))
    shape = jax.ShapeDtypeStruct(w.shape, F32)
    return pl.pallas_call(
        body, name="adamw_" + name, grid=(rows // tr,),
        in_specs=[lay, pl.BlockSpec((tr, cols), lambda r: (r, 0)), lay, lay] + [ANY] * n_prev,
        out_specs=[lay] * 4, out_shape=[shape] * 4,
        input_output_aliases={4 + k: k for k in range(n_prev)},
        compiler_params=_params(("parallel",)),
    )(w, g, m, v, *(prev or ()))


def _adamw_cols_major(w, gs, m, v, name, rider=None):
    depth, rows, cols = w.shape
    tr = max(t for t in range(1, 129) if cols % t == 0)
    wt, mt, vt = (jnp.transpose(a, (2, 0, 1)) for a in (w, m, v))
    gt = jnp.stack([g.T for g in gs], axis=1)

    def body(w_ref, g_ref, m_ref, v_ref, g_out, d_out, m_out, v_out):
        gv = g_ref[...]
        g_out[...] = gv
        d_out[...], m_out[...], v_out[...] = _adamw_math(w_ref[...], gv, m_ref[...], v_ref[...])

    spec = pl.BlockSpec((tr, depth, rows), lambda r: (r, 0, 0))
    shape = jax.ShapeDtypeStruct((cols, depth, rows), F32)
    outs, rode = _call(body, "adamw_" + name, (cols // tr,), [spec] * 4, [spec] * 4, [shape] * 4, [], ("parallel",),
                       (wt, gt, mt, vt), rider)
    return [jnp.transpose(a, (1, 2, 0)) for a in outs], rode


def _sum_leading(buf, name):
    n, rows, cols = buf.shape
    tr = _row_tile(rows, 512)

    def body(in_ref, out_ref):
        acc = in_ref[0]
        for k in range(1, n):
            acc = acc + in_ref[k]
        out_ref[...] = acc

    return pl.pallas_call(
        body, name=name, grid=(rows // tr,),
        in_specs=[pl.BlockSpec((n, tr, cols), lambda i: (0, i, 0))],
        out_specs=pl.BlockSpec((tr, cols), lambda i: (i, 0)),
        out_shape=jax.ShapeDtypeStruct((rows, cols), F32),
        compiler_params=_params(("parallel",)),
    )(buf)


_SHARDED_SMALL = ("meta", "conv_a_w", "ssm_conv_w", "conf_conv_w")
_LAYER_SMALL = ("pre_g", "post_g", "conv_a_w", "ssm_conv_w", "ssm_conv_b", "dt_bias", "a_log", "d_skip",
                "ssm_norm_g", "conf_conv_w", "conf_conv_b", "conf_ln_g", "conf_ln_b")
_WEIGHTS = ("meta", "pre_g", "post_g", "w_in", "w_out", "conv_a_w", "ssm_conv_w", "ssm_conv_b", "dt_bias", "a_log",
            "d_skip", "ssm_norm_g", "conf_conv_w", "conf_conv_b", "conf_ln_g", "conf_ln_b")


def _shard_last(a):
    return jnp.moveaxis(a.reshape(a.shape[:-1] + (N_CHIPS, a.shape[-1] // N_CHIPS)), -2, 0)


def _with_own_block(a, n, at):
    return lax.dynamic_update_index_in_dim(jnp.zeros((n,) + a.shape, a.dtype), a, at, 0)


def _with_own_columns(a, chip):
    k = a.shape[-1]
    return lax.dynamic_update_slice_in_dim(jnp.zeros(a.shape[:-1] + (N_CHIPS * k,), a.dtype), a, chip * k, a.ndim - 1)


def kernel(x, meta, pre_g, post_g, w_in, w_out, conv_a_w, ssm_conv_w, ssm_conv_b, dt_bias, a_log, d_skip, ssm_norm_g, conf_conv_w, conf_conv_b, conf_ln_g, conf_ln_b, loss_target, m_meta, m_pre_g, m_post_g, m_w_in, m_w_out, m_conv_a_w, m_ssm_conv_w, m_ssm_conv_b, m_dt_bias, m_a_log, m_d_skip, m_ssm_norm_g, m_conf_conv_w, m_conf_conv_b, m_conf_ln_g, m_conf_ln_b, v_meta, v_pre_g, v_post_g, v_w_in, v_w_out, v_conv_a_w, v_ssm_conv_w, v_ssm_conv_b, v_dt_bias, v_a_log, v_d_skip, v_ssm_norm_g, v_conf_conv_w, v_conf_conv_b, v_conf_ln_g, v_conf_ln_b):
    w = dict(meta=meta, pre_g=pre_g, post_g=post_g, w_in=w_in, w_out=w_out, conv_a_w=conv_a_w,
             ssm_conv_w=ssm_conv_w, ssm_conv_b=ssm_conv_b, dt_bias=dt_bias, a_log=a_log, d_skip=d_skip,
             ssm_norm_g=ssm_norm_g, conf_conv_w=conf_conv_w, conf_conv_b=conf_conv_b, conf_ln_g=conf_ln_g,
             conf_ln_b=conf_ln_b)
    mom = dict(meta=m_meta, pre_g=m_pre_g, post_g=m_post_g, w_in=m_w_in, w_out=m_w_out, conv_a_w=m_conv_a_w,
               ssm_conv_w=m_ssm_conv_w, ssm_conv_b=m_ssm_conv_b, dt_bias=m_dt_bias, a_log=m_a_log, d_skip=m_d_skip,
               ssm_norm_g=m_ssm_norm_g, conf_conv_w=m_conf_conv_w, conf_conv_b=m_conf_conv_b,
               conf_ln_g=m_conf_ln_g, conf_ln_b=m_conf_ln_b)
    vel = dict(meta=v_meta, pre_g=v_pre_g, post_g=v_post_g, w_in=v_w_in, w_out=v_w_out, conv_a_w=v_conv_a_w,
               ssm_conv_w=v_ssm_conv_w, ssm_conv_b=v_ssm_conv_b, dt_bias=v_dt_bias, a_log=v_a_log, d_skip=v_d_skip,
               ssm_norm_g=v_ssm_norm_g, conf_conv_w=v_conf_conv_w, conf_conv_b=v_conf_conv_b,
               conf_ln_g=v_conf_ln_g, conf_ln_b=v_conf_ln_b)
    bl, seq, d = x.shape
    dm = Dims(bl, seq, d)
    depth = w_in.shape[0]
    chip = (2 * lax.axis_index("x") + lax.axis_index("y")).astype(jnp.int32)
    dev = 2 * chip + lax.axis_index("c").astype(jnp.int32)
    cst = _head_consts(dm)

    full = dict(w)
    full.update(zip(_SHARDED_SMALL, _exchange("gather_small_weights", _ride_gather_small(
        [_with_own_columns(w[n], chip) for n in _SHARDED_SMALL]))))

    bases = [[_place_own(w_in[i], chip.reshape(1)), _place_own(w_out[i], chip.reshape(1))] for i in range(depth)]
    gathered = _exchange("gather_d2d_first", _ride_gather_d2d(_exchange("gather_ici_first",
                                                                         _ride_gather_ici(bases[0]))))
    h = _embed(x, full["meta"], dm)
    saved, proj_w = [], []
    for i in range(depth):
        lw = {n: full[n][i] for n in _LAYER_SMALL}
        proj_w.append((_w_in_from_shards(gathered[0], dm), gathered[1].reshape(2 * d, d)))
        h, keep, gathered = _layer_fwd(h, lw, proj_w[i][0], proj_w[i][1], cst, dm,
                                       bases[i + 1] if i + 1 < depth else None)
        saved.append(keep)

    dh, loss = _loss_head(h, loss_target, dm)
    loss = lax.psum(loss, ("x", "y", "c"))

    small_g = {n: [None] * depth for n in _LAYER_SMALL}
    big = {"w_in": None, "w_out": None}
    g_in = [None] * depth
    reduce = None
    for i in reversed(range(depth)):
        lw = {n: full[n][i] for n in _LAYER_SMALL}
        dh, mine, sg = _layer_bwd(dh, saved[i], lw, proj_w[i][0], proj_w[i][1], cst, dm, reduce, last=i == 0)
        for n in _LAYER_SMALL:
            small_g[n][i] = sg[n]
        if reduce is not None:
            g_in[i + 1] = reduce.result[0]
            big["w_out"] = _adamw_layer(i + 1, w_out, reduce.result[1], m_w_out, v_w_out, big["w_out"], "w_out")
        reduce = mine
    g_in[0] = reduce.result[0]
    big["w_out"] = _adamw_layer(0, w_out, reduce.result[1], m_w_out, v_w_out, big["w_out"], "w_out")
    grad_x, gmeta = _unembed(dh, dm)

    g = {n: jnp.stack(v) for n, v in small_g.items()}
    g["meta"] = gmeta
    small = [n for n in _WEIGHTS if n not in ("w_in", "w_out")]
    flat = jnp.concatenate([g[n].reshape(-1) for n in small])
    rows = -(-flat.shape[0] // (16 * LANES)) * 16
    flat = jnp.pad(flat, (0, rows * LANES - flat.shape[0])).reshape(rows, LANES)
    big["w_in"], parts = _adamw_cols_major(w_in, g_in, m_w_in, v_w_in, "w_in",
                                           _ride_gather_all(_with_own_block(flat, N_DEV, dev)))
    total = _sum_leading(parts[0], "small_grads_sum").reshape(-1)
    grads, deltas, new_m, new_v = {}, {}, {}, {}
    off = 0
    for n in small:
        size = g[n].size
        fullg = total[off:off + size].reshape(g[n].shape)
        off += size
        if n in _SHARDED_SMALL:
            fullg = lax.dynamic_index_in_dim(_shard_last(fullg), chip, axis=0, keepdims=False)
        grads[n] = fullg
        deltas[n], new_m[n], new_v[n] = _adamw_small(w[n], fullg, mom[n], vel[n], n)
    for n in ("w_in", "w_out"):
        grads[n], deltas[n], new_m[n], new_v[n] = big[n]

    return (loss, grad_x, *[grads[n] for n in _WEIGHTS], *[deltas[n] for n in _WEIGHTS],
            *[new_m[n] for n in _WEIGHTS], *[new_v[n] for n in _WEIGHTS])
```

```python
import jax
import jax.numpy as jnp
from jax import lax
from jax.experimental import pallas as pl
from jax.experimental.pallas import tpu as pltpu

F32 = jnp.float32
BF16 = jnp.bfloat16

N_META = 16
TT = 128
SSM_STATE = 128
SSM_GROUPS = 2
SSM_HEAD_DIM = 64
CONV_A_K = 3
SSM_CONV_K = 4
CONF_K = 31
NORM_EPS = 1e-6
LN_EPS = 1e-5
LANES = 128
MXU_DIM = 256
DT_PAD = LANES
CONF_HALO = 32
SMALL_HALO = 8
VMEM_LIMIT = 56 * 1024 * 1024
N_CHIPS = 4
N_DEV = 8

ADAM_LR = 0.001
ADAM_B1 = 0.9
ADAM_B2 = 0.999
ADAM_EPS = 1e-08
ADAM_WD = 0.01
ADAM_STEP = 10

MESH = pl.DeviceIdType.MESH
ANY = pl.BlockSpec(memory_space=pl.ANY)


class Dims:
    def __init__(self, bl, seq, d):
        self.BL, self.S, self.D = bl, seq, d
        self.L = seq + N_META
        self.Lp = -(-self.L // TT) * TT
        self.NT = self.Lp // TT
        self.R = bl * self.Lp
        self.DA = d // 2
        self.DB = d
        self.DC = d // 2
        self.H = self.DB // SSM_HEAD_DIM
        self.HPG = self.H // SSM_GROUPS
        self.GN = SSM_GROUPS * SSM_STATE
        self.WA = 4 * self.DA
        self.WB = 2 * self.DB + 2 * self.GN
        self.WC = 3 * self.DC
        self.NP = self.WA + self.WB + self.WC + DT_PAD
        self.NIN = self.WA + self.WB + self.H + self.WC
        self.XBC = self.DB + 2 * self.GN
        assert self.H % 2 == 0 and self.HPG % 2 == 0 and self.H <= DT_PAD
        assert self.DA % LANES == 0 and (self.WA + self.WB) % self.DC == 0 and self.WA % self.DB == 0


def _row_tile(n, target):
    best = None
    for t in range(16, min(n, target) + 1, 16):
        if n % t == 0:
            best = t
    assert best is not None
    return best


def _col_tile(n, target):
    best = None
    for t in range(LANES, min(n, target) + 1, LANES):
        if n % t == 0:
            best = t
    assert best is not None
    return best


def _params(sem=None):
    return pltpu.CompilerParams(dimension_semantics=sem, vmem_limit_bytes=VMEM_LIMIT)


def _sigmoid(x):
    return 1.0 / (1.0 + jnp.exp(-x))


def _silu_and_grad(x):
    s = _sigmoid(x)
    return x * s, s * (1.0 + x * (1.0 - s))


def _dot(a, b):
    return jnp.dot(a, b, preferred_element_type=F32)


def _dot_nt(a, b):
    return lax.dot_general(a, b, (((1,), (1,)), ((), ())), preferred_element_type=F32)


def _dot_tn(a, b):
    return lax.dot_general(a, b, (((0,), (0,)), ((), ())), preferred_element_type=F32)


def _split3(x):
    x1 = x.astype(BF16)
    r1 = x - x1.astype(F32)
    x2 = r1.astype(BF16)
    x3 = (r1 - x2.astype(F32)).astype(BF16)
    return x1, x2, x3


class Rider:
    def __init__(self, plan, ins, out_shapes, aliases, nsem):
        self.plan, self.ins, self.out_shapes, self.aliases, self.nsem = plan, list(ins), list(out_shapes), aliases, nsem


def _place():
    x, y, c = lax.axis_index("x"), lax.axis_index("y"), lax.axis_index("c")
    chips = [(1 - x, y), (x, 1 - y), (1 - x, 1 - y)]
    return x, y, c, chips


def _remote(k, src, dst, to, send_sems, recv_sems):
    return pltpu.make_async_remote_copy(src_ref=src, dst_ref=dst, send_sem=send_sems.at[k], recv_sem=recv_sems.at[k],
                                        device_id=to, device_id_type=MESH)


def _call(body, name, grid, in_specs, out_specs, out_shape, scratch_shapes, sem, args, rider=None):
    if rider is None:
        outs = pl.pallas_call(body, name=name, grid=grid, in_specs=in_specs, out_specs=out_specs, out_shape=out_shape,
                              scratch_shapes=scratch_shapes, compiler_params=_params(sem))(*args)
        return list(outs), []
    n_in, n_out, n_scr = len(args), len(out_shape), len(scratch_shapes)
    r_in, r_out = len(rider.ins), len(rider.out_shapes)

    def hosted(*refs):
        ins, rins = refs[:n_in], refs[n_in:n_in + r_in]
        o0 = n_in + r_in
        outs, routs = refs[o0:o0 + n_out], refs[o0 + n_out:o0 + n_out + r_out]
        scr = refs[o0 + n_out + r_out:o0 + n_out + r_out + n_scr]
        send_sems, recv_sems = refs[o0 + n_out + r_out + n_scr:]
        first = pl.program_id(0) == 0
        last = pl.program_id(0) == grid[0] - 1
        for ax in range(1, len(grid)):
            first = jnp.logical_and(first, pl.program_id(ax) == 0)
            last = jnp.logical_and(last, pl.program_id(ax) == grid[ax] - 1)

        @pl.when(first)
        def _():
            starts, _ = rider.plan(rins, routs, send_sems, recv_sems)
            for cp in starts:
                cp.start()

        body(*ins, *outs, *scr)

        @pl.when(last)
        def _():
            _, waits = rider.plan(rins, routs, send_sems, recv_sems)
            for wait in waits:
                wait()

    res = pl.pallas_call(
        hosted, name=name, grid=grid,
        in_specs=list(in_specs) + [ANY] * r_in, out_specs=list(out_specs) + [ANY] * r_out,
        out_shape=list(out_shape) + rider.out_shapes,
        input_output_aliases={n_in + k: n_out + v for k, v in rider.aliases.items()},
        scratch_shapes=list(scratch_shapes) + [pltpu.SemaphoreType.DMA((rider.nsem,)),
                                               pltpu.SemaphoreType.DMA((rider.nsem,))],
        compiler_params=_params(("arbitrary",) * len(grid)),
    )(*args, *rider.ins)
    return list(res[:n_out]), list(res[n_out:])


def _exchange(name, rider):
    r_in, r_out = len(rider.ins), len(rider.out_shapes)

    def body(*refs):
        rins, routs = refs[:r_in], refs[r_in:r_in + r_out]
        send_sems, recv_sems = refs[r_in + r_out:]
        starts, waits = rider.plan(rins, routs, send_sems, recv_sems)
        for cp in starts:
            cp.start()
        for wait in waits:
            wait()

    res = pl.pallas_call(
        body, name=name, in_specs=[ANY] * r_in, out_specs=[ANY] * r_out, out_shape=rider.out_shapes,
        input_output_aliases=dict(rider.aliases),
        scratch_shapes=[pltpu.SemaphoreType.DMA((rider.nsem,)), pltpu.SemaphoreType.DMA((rider.nsem,))],
    )(*rider.ins)
    return list(res)


def _same(arrays):
    return [jax.ShapeDtypeStruct(a.shape, a.dtype) for a in arrays]


class _SemsFrom:
    def __init__(self, sems, first):
        self.sems, self.first = sems, first

    @property
    def at(self):
        return self

    def __getitem__(self, k):
        return self.sems.at[self.first + k]


def _ride_both(r1, r2):
    n_in, n_out = len(r1.ins), len(r1.out_shapes)

    def plan(ins, outs, ss, rs):
        s1, w1 = r1.plan(ins[:n_in], outs[:n_out], ss, rs)
        s2, w2 = r2.plan(ins[n_in:], outs[n_out:], _SemsFrom(ss, r1.nsem), _SemsFrom(rs, r1.nsem))
        return s1 + s2, w1 + w2

    aliases = dict(r1.aliases)
    aliases.update({n_in + k: n_out + v for k, v in r2.aliases.items()})
    return Rider(plan, r1.ins + r2.ins, r1.out_shapes + r2.out_shapes, aliases, r1.nsem + r2.nsem)


def _ride_gather_ici(bases, part=0, nparts=1):
    n = len(bases)

    def plan(ins, outs, ss, rs):
        x, y, c, chips = _place()
        me = 2 * x + y
        starts, waits = [], []
        for a in range(n):
            half = outs[a].shape[1] // 2
            mine = pl.ds(c * half + part * (half // nparts), half // nparts)
            for j, chip in enumerate(chips):
                cp = _remote(3 * a + j, outs[a].at[me, mine], outs[a].at[me, mine], (*chip, c), ss, rs)
                got = outs[a].at[2 * chip[0] + chip[1], mine]
                starts.append(cp)
                waits += [cp.wait_send, _remote(3 * a + j, got, got, (*chip, c), ss, rs).wait_recv]
        return starts, waits

    return Rider(plan, bases, _same(bases), {a: a for a in range(n)}, 3 * n)


def _ride_gather_d2d(bases):
    n = len(bases)

    def plan(ins, outs, ss, rs):
        x, y, c, chips = _place()
        sib = (x, y, 1 - c)
        starts, waits = [], []
        for a in range(n):
            half = outs[a].shape[1] // 2
            for j, chip in enumerate(chips):
                frm = 2 * chip[0] + chip[1]
                got = outs[a].at[frm, pl.ds(c * half, half)]
                theirs = outs[a].at[frm, pl.ds((1 - c) * half, half)]
                cp = _remote(3 * a + j, got, got, sib, ss, rs)
                starts.append(cp)
                waits += [cp.wait_send, _remote(3 * a + j, theirs, theirs, sib, ss, rs).wait_recv]
        return starts, waits

    return Rider(plan, bases, _same(bases), {a: a for a in range(n)}, 3 * n)


def _ride_gather_small(bases):
    n = len(bases)

    def plan(ins, outs, ss, rs):
        x, y, c, chips = _place()
        me = 2 * x + y
        starts, waits = [], []
        for a in range(n):
            k = outs[a].shape[-1] // N_CHIPS
            lead = (slice(None),) * (len(outs[a].shape) - 1)
            at = (lambda s: pl.multiple_of(s * k, LANES)) if k % LANES == 0 else (lambda s: s * k)
            cols = lambda s: outs[a].at[lead + (pl.ds(at(s), k),)]
            for j, chip in enumerate(chips):
                cp = _remote(3 * a + j, cols(me), cols(me), (*chip, c), ss, rs)
                got = cols(2 * chip[0] + chip[1])
                starts.append(cp)
                waits += [cp.wait_send, _remote(3 * a + j, got, got, (*chip, c), ss, rs).wait_recv]
        return starts, waits

    return Rider(plan, bases, _same(bases), {a: a for a in range(n)}, 3 * n)


def _ride_swap_halves(gs):
    n = len(gs)

    def plan(ins, outs, ss, rs):
        x, y, c, _ = _place()
        cps = [_remote(a, ins[a].at[:, 1 - c], outs[a], (x, y, 1 - c), ss, rs) for a in range(n)]
        return cps, [cp.wait for cp in cps]

    shapes = [jax.ShapeDtypeStruct((g.shape[0],) + g.shape[2:], g.dtype) for g in gs]
    return Rider(plan, gs, shapes, {}, n)


def _ride_to_owners(ps):
    n = len(ps)

    def plan(ins, outs, ss, rs):
        x, y, c, chips = _place()
        cps = []
        for a in range(n):
            for j, chip in enumerate(chips):
                cps.append(_remote(3 * a + j, ins[a].at[2 * chip[0] + chip[1]], outs[a].at[j], (*chip, c), ss, rs))
        return cps, [cp.wait for cp in cps]

    shapes = [jax.ShapeDtypeStruct((3,) + p.shape[1:], p.dtype) for p in ps]
    return Rider(plan, ps, shapes, {}, 3 * n)


def _ride_join_halves(qs):
    n = len(qs)

    def plan(ins, outs, ss, rs):
        x, y, c, _ = _place()
        sib = (x, y, 1 - c)
        starts, waits = [], []
        for a in range(n):
            cp = _remote(a, outs[a].at[c], outs[a].at[c], sib, ss, rs)
            starts.append(cp)
            waits += [cp.wait_send, _remote(a, outs[a].at[1 - c], outs[a].at[1 - c], sib, ss, rs).wait_recv]
        return starts, waits

    return Rider(plan, qs, _same(qs), {a: a for a in range(n)}, n)


def _ride_gather_all(base):
    def plan(ins, outs, ss, rs):
        x, y, c, _ = _place()
        me = 4 * x + 2 * y + c

        def peer(k):
            return (1 - x if (k >> 2) & 1 else x, 1 - y if (k >> 1) & 1 else y, 1 - c if k & 1 else c)

        starts, waits = [], []
        for k in range(1, N_DEV):
            px, py, pc = peer(k)
            cp = _remote(k - 1, outs[0].at[me], outs[0].at[me], (px, py, pc), ss, rs)
            got = outs[0].at[4 * px + 2 * py + pc]
            starts.append(cp)
            waits += [cp.wait_send, _remote(k - 1, got, got, (px, py, pc), ss, rs).wait_recv]
        return starts, waits

    return Rider(plan, [base], _same([base]), {0: 0}, N_DEV - 1)


def _embed(x, meta, dm):
    dc = _col_tile(dm.D, 256)
    s, lp = dm.S, dm.Lp

    def body(x_ref, meta_ref, h_ref):
        h_ref[0:N_META, :] = meta_ref[...]
        h_ref[N_META:N_META + s, :] = x_ref[0]
        if lp > N_META + s:
            h_ref[N_META + s:lp, :] = jnp.zeros((lp - N_META - s, dc), F32)

    return pl.pallas_call(
        body, name="embed", grid=(dm.BL, dm.D // dc),
        in_specs=[pl.BlockSpec((1, s, dc), lambda b, j: (b, 0, j)),
                  pl.BlockSpec((N_META, dc), lambda b, j: (0, j))],
        out_specs=pl.BlockSpec((lp, dc), lambda b, j: (b, j)),
        out_shape=jax.ShapeDtypeStruct((dm.R, dm.D), F32),
        compiler_params=_params(("parallel", "parallel")),
    )(x, meta)


def _loss_head(h, target, dm):
    dc = _col_tile(dm.D, 256)
    s, lp, nj = dm.S, dm.Lp, dm.D // dc

    def body(h_ref, t_ref, dh_ref, l_ref):
        diff = h_ref[N_META:N_META + s, :] - t_ref[0]
        dh_ref[0:N_META, :] = jnp.zeros((N_META, dc), F32)
        dh_ref[N_META:N_META + s, :] = diff * (1.0 / dm.D)
        if lp > N_META + s:
            dh_ref[N_META + s:lp, :] = jnp.zeros((lp - N_META - s, dc), F32)
        l_ref[...] = jnp.full((8, LANES), (0.5 / dm.D) * jnp.sum(diff * diff), F32)

    dh, part = pl.pallas_call(
        body, name="loss_head", grid=(dm.BL, nj),
        in_specs=[pl.BlockSpec((lp, dc), lambda b, j: (b, j)),
                  pl.BlockSpec((1, s, dc), lambda b, j: (b, 0, j))],
        out_specs=[pl.BlockSpec((lp, dc), lambda b, j: (b, j)),
                   pl.BlockSpec((8, LANES), lambda b, j: (b * nj + j, 0))],
        out_shape=[jax.ShapeDtypeStruct((dm.R, dm.D), F32),
                   jax.ShapeDtypeStruct((dm.BL * nj * 8, LANES), F32)],
        compiler_params=_params(("parallel", "parallel")),
    )(h, target)
    return dh, jnp.sum(part[::8, 0])


def _unembed(dh, dm):
    dc = _col_tile(dm.D, 256)
    s, lp = dm.S, dm.Lp

    def body(dh_ref, gx_ref, gm_ref):
        gx_ref[0] = dh_ref[N_META:N_META + s, :]

        @pl.when(pl.program_id(1) == 0)
        def _():
            gm_ref[...] = dh_ref[0:N_META, :]

        @pl.when(pl.program_id(1) > 0)
        def _():
            gm_ref[...] = gm_ref[...] + dh_ref[0:N_META, :]

    return pl.pallas_call(
        body, name="unembed", grid=(dm.D // dc, dm.BL),
        in_specs=[pl.BlockSpec((lp, dc), lambda j, b: (b, j))],
        out_specs=[pl.BlockSpec((1, s, dc), lambda j, b: (b, 0, j)),
                   pl.BlockSpec((N_META, dc), lambda j, b: (0, j))],
        out_shape=[jax.ShapeDtypeStruct((dm.BL, s, dm.D), F32),
                   jax.ShapeDtypeStruct((N_META, dm.D), F32)],
        compiler_params=_params(("parallel", "arbitrary")),
    )(dh)


def _fwd_in(h, pre_g, w, dm, rider=None):
    tm = _row_tile(dm.R, 1088)
    tn = _col_tile(dm.NP, 896)
    nj = dm.NP // tn

    def body(h_ref, g_ref, w_ref, proj_ref, dt_ref, hn_ref):
        @pl.when(pl.program_id(1) == 0)
        def _():
            xf = h_ref[...]
            r = lax.rsqrt(jnp.mean(xf * xf, axis=-1, keepdims=True) + NORM_EPS)
            hn_ref[...] = (xf * r * g_ref[...]).astype(BF16)

        res = _dot(hn_ref[...], w_ref[...])
        proj_ref[...] = res.astype(BF16)

        @pl.when(pl.program_id(1) == nj - 1)
        def _():
            dt_ref[...] = res[:, tn - DT_PAD:tn]

    return _call(
        body, "fwd_in", (dm.R // tm, nj),
        [pl.BlockSpec((tm, dm.D), lambda i, j: (i, 0)),
         pl.BlockSpec((1, dm.D), lambda i, j: (0, 0)),
         pl.BlockSpec((dm.D, tn), lambda i, j: (0, j))],
        [pl.BlockSpec((tm, tn), lambda i, j: (i, j)),
         pl.BlockSpec((tm, DT_PAD), lambda i, j: (i, 0)),
         pl.BlockSpec((tm, dm.D), lambda i, j: (i, 0))],
        [jax.ShapeDtypeStruct((dm.R, dm.NP), BF16), jax.ShapeDtypeStruct((dm.R, DT_PAD), F32),
         jax.ShapeDtypeStruct((dm.R, dm.D), BF16)],
        [], ("parallel", "arbitrary"), (h, pre_g, w), rider)


def _fwd_out(ya, yb, yc, w_out, h, post_g, dm, rider=None):
    tm = _row_tile(dm.Lp, 544)
    tiles_per_seq = dm.Lp // tm
    da, db, dc = dm.DA, dm.DB, dm.DC

    def body(ya_ref, yb_ref, yc_ref, w_ref, h_ref, g_ref, hn_ref, m_ref):
        m = _dot(ya_ref[...], w_ref[0:da, :])
        m = m + _dot(yb_ref[...], w_ref[da:da + db, :])
        m = m + _dot(yc_ref[...], w_ref[da + db:da + db + dc, :])
        m_ref[...] = m
        r = lax.rsqrt(jnp.mean(m * m, axis=-1, keepdims=True) + NORM_EPS)
        t = (pl.program_id(0) % tiles_per_seq) * tm + lax.broadcasted_iota(jnp.int32, (tm, 1), 0)
        keep = (t < dm.L).astype(F32)
        hn_ref[...] = (h_ref[...] + m * r * g_ref[...]) * keep

    row = lambda i: (i, 0)
    fixed = lambda i: (0, 0)
    return _call(
        body, "fwd_out", (dm.R // tm,),
        [pl.BlockSpec((tm, da), row), pl.BlockSpec((tm, db), row), pl.BlockSpec((tm, dc), row),
         pl.BlockSpec((2 * dm.D, dm.D), fixed), pl.BlockSpec((tm, dm.D), row), pl.BlockSpec((1, dm.D), fixed)],
        [pl.BlockSpec((tm, dm.D), row), pl.BlockSpec((tm, dm.D), row)],
        [jax.ShapeDtypeStruct((dm.R, dm.D), F32), jax.ShapeDtypeStruct((dm.R, dm.D), F32)],
        [], ("parallel",), (ya, yb, yc, w_out, h, post_g), rider)


def _bwd_out(dh, m, post_g, w_out, ya, yb, yc, dm, rider=None):
    tm = _row_tile(dm.R, MXU_DIM)
    da, db, dc = dm.DA, dm.DB, dm.DC

    def body(dh_ref, m_ref, g_ref, w_ref, ya_ref, yb_ref, yc_ref, dya_ref, dyb_ref, dyc_ref, dw_ref, dg_ref):
        @pl.when(pl.program_id(0) == 0)
        def _():
            dw_ref[...] = jnp.zeros_like(dw_ref)
            dg_ref[...] = jnp.zeros_like(dg_ref)

        m = m_ref[...]
        dh_ = dh_ref[...]
        r = lax.rsqrt(jnp.mean(m * m, axis=-1, keepdims=True) + NORM_EPS)
        n = m * r
        dg_ref[0:1, :] = dg_ref[0:1, :] + jnp.sum(dh_ * n, axis=0, keepdims=True)
        dn = dh_ * g_ref[...]
        dm_ = (r * (dn - n * jnp.mean(dn * n, axis=-1, keepdims=True))).astype(BF16)
        dya_ref[...] = _dot_nt(dm_, w_ref[0:da, :])
        dyb_ref[...] = _dot_nt(dm_, w_ref[da:da + db, :])
        dyc_ref[...] = _dot_nt(dm_, w_ref[da + db:da + db + dc, :])
        dw_ref[0:da, :] = dw_ref[0:da, :] + _dot_tn(ya_ref[...], dm_)
        dw_ref[da:da + db, :] = dw_ref[da:da + db, :] + _dot_tn(yb_ref[...], dm_)
        dw_ref[da + db:da + db + dc, :] = dw_ref[da + db:da + db + dc, :] + _dot_tn(yc_ref[...], dm_)

    row = lambda i: (i, 0)
    fixed = lambda i: (0, 0)
    return _call(
        body, "bwd_out", (dm.R // tm,),
        [pl.BlockSpec((tm, dm.D), row), pl.BlockSpec((tm, dm.D), row), pl.BlockSpec((1, dm.D), fixed),
         pl.BlockSpec((2 * dm.D, dm.D), fixed),
         pl.BlockSpec((tm, da), row), pl.BlockSpec((tm, db), row), pl.BlockSpec((tm, dc), row)],
        [pl.BlockSpec((tm, da), row), pl.BlockSpec((tm, db), row), pl.BlockSpec((tm, dc), row),
         pl.BlockSpec((2 * dm.D, dm.D), fixed), pl.BlockSpec((8, dm.D), fixed)],
        [jax.ShapeDtypeStruct((dm.R, da), F32), jax.ShapeDtypeStruct((dm.R, db), F32),
         jax.ShapeDtypeStruct((dm.R, dc), F32),
         jax.ShapeDtypeStruct((2 * dm.D, dm.D), F32), jax.ShapeDtypeStruct((8, dm.D), F32)],
        [], ("arbitrary",), (dh, m, post_g, w_out, ya, yb, yc), rider)


def _bwd_in_dx(dpa, dpb, dpc, dpt, w, h, dh, pre_g, dm, rider=None):
    tm = _row_tile(dm.R, 272)
    wa, wb, wc = dm.WA, dm.WB, dm.WC

    def body(dpa_ref, dpb_ref, dpc_ref, dpt_ref, w_ref, h_ref, dh_ref, g_ref, out_ref, dg_ref):
        @pl.when(pl.program_id(0) == 0)
        def _():
            dg_ref[...] = jnp.zeros_like(dg_ref)

        dhn = _dot_nt(dpa_ref[...], w_ref[:, 0:wa])
        dhn = dhn + _dot_nt(dpb_ref[...], w_ref[:, wa:wa + wb])
        dhn = dhn + _dot_nt(dpc_ref[...], w_ref[:, wa + wb:wa + wb + wc])
        dhn = dhn + _dot_nt(dpt_ref[...], w_ref[:, wa + wb + wc:wa + wb + wc + DT_PAD])
        xf = h_ref[...]
        r = lax.rsqrt(jnp.mean(xf * xf, axis=-1, keepdims=True) + NORM_EPS)
        n = xf * r
        dg_ref[0:1, :] = dg_ref[0:1, :] + jnp.sum(dhn * n, axis=0, keepdims=True)
        dn = dhn * g_ref[...]
        out_ref[...] = dh_ref[...] + r * (dn - n * jnp.mean(dn * n, axis=-1, keepdims=True))

    row = lambda i: (i, 0)
    fixed = lambda i: (0, 0)
    return _call(
        body, "bwd_in_dx", (dm.R // tm,),
        [pl.BlockSpec((tm, wa), row), pl.BlockSpec((tm, wb), row), pl.BlockSpec((tm, wc), row),
         pl.BlockSpec((tm, DT_PAD), row), pl.BlockSpec((dm.D, dm.NP), fixed),
         pl.BlockSpec((tm, dm.D), row), pl.BlockSpec((tm, dm.D), row), pl.BlockSpec((1, dm.D), fixed)],
        [pl.BlockSpec((tm, dm.D), row), pl.BlockSpec((8, dm.D), fixed)],
        [jax.ShapeDtypeStruct((dm.R, dm.D), F32), jax.ShapeDtypeStruct((8, dm.D), F32)],
        [], ("arbitrary",), (dpa, dpb, dpc, dpt, w, h, dh, pre_g), rider)


def _bwd_in_dw(hn, dp, dm, piece):
    width = dp.shape[1]
    tn = _col_tile(width, 512)

    def body(hn_ref, dp_ref, dw_ref):
        dw_ref[...] = _dot_tn(hn_ref[...], dp_ref[...])

    return pl.pallas_call(
        body, name="bwd_in_dw_" + piece, grid=(width // tn,),
        in_specs=[pl.BlockSpec((dm.R, dm.D), lambda j: (0, 0)), pl.BlockSpec((dm.R, tn), lambda j: (0, j))],
        out_specs=pl.BlockSpec((dm.D, tn), lambda j: (0, j)),
        out_shape=jax.ShapeDtypeStruct((dm.D, width), F32),
        compiler_params=_params(("parallel",)),
    )(hn, dp)


def _tile_index(dm, reverse):
    if reverse:
        return lambda b, i: b * dm.NT + (dm.NT - 1 - i)
    return lambda b, i: b * dm.NT + i


def _halo_index(dm, rows):
    per_tile = TT // rows
    return lambda b, i: jnp.maximum((b * dm.NT + (dm.NT - 1 - i)) * per_tile - 1, 0)


HALO_BLOCK = 16


def _last_rows(x):
    return x.astype(F32)[HALO_BLOCK - SMALL_HALO:HALO_BLOCK]


MIX_A_ROWS = 288


def _mix_a_fwd(proj, conv_w, dm):
    da = dm.DA
    ta = _row_tile(dm.Lp, MIX_A_ROWS)
    nta = dm.Lp // ta
    ti = lambda b, i: b * nta + i

    def body(ab_ref, ac_ref, ax_ref, az_ref, w_ref, y_ref, pbuf):
        i = pl.program_id(1)

        @pl.when(i == 0)
        def _():
            pbuf[0:SMALL_HALO, :] = jnp.zeros((SMALL_HALO, da), F32)

        @pl.when(i > 0)
        def _():
            pbuf[0:SMALL_HALO, :] = pbuf[ta:ta + SMALL_HALO, :]

        for lb in range(da // LANES):
            cs = slice(lb * LANES, (lb + 1) * LANES)
            p = ac_ref[:, cs].astype(F32) * ax_ref[:, cs].astype(F32)
            pbuf[SMALL_HALO:SMALL_HALO + ta, cs] = p
            q = (w_ref[0:1, cs] * pbuf[6:6 + ta, cs] + w_ref[1:2, cs] * pbuf[7:7 + ta, cs] + w_ref[2:3, cs] * p)
            az = az_ref[:, cs].astype(F32)
            y_ref[:, cs] = (ab_ref[:, cs].astype(F32) * q * (az * _sigmoid(az))).astype(BF16)

    col = lambda k: pl.BlockSpec((ta, da), lambda b, i: (ti(b, i), k))
    return pl.pallas_call(
        body, name="mix_a_fwd", grid=(dm.BL, nta),
        in_specs=[col(0), col(1), col(2), col(3), pl.BlockSpec((CONV_A_K, da), lambda b, i: (0, 0))],
        out_specs=pl.BlockSpec((ta, da), lambda b, i: (ti(b, i), 0)),
        out_shape=jax.ShapeDtypeStruct((dm.R, da), BF16),
        scratch_shapes=[pltpu.VMEM((SMALL_HALO + ta, da), F32)],
        compiler_params=_params(("parallel", "arbitrary")),
    )(proj, proj, proj, proj, conv_w)


def _mix_a_bwd(proj, dya, conv_w, dm):
    da = dm.DA
    ta = _row_tile(dm.Lp, MIX_A_ROWS)
    nta = dm.Lp // ta
    ti = lambda b, i: b * nta + (nta - 1 - i)
    hi = lambda b, i: jnp.maximum((b * nta + (nta - 1 - i)) * (ta // HALO_BLOCK) - 1, 0)

    def body(ab_ref, ac_ref, ax_ref, az_ref, ach_ref, axh_ref, dy_ref, w_ref, dp_ref, dw_ref, pbuf, dqbuf):
        i = pl.program_id(1)
        halo_on = jnp.where(i == nta - 1, 0.0, 1.0)

        @pl.when(i == 0)
        def _():
            dw_ref[...] = jnp.zeros_like(dw_ref)
            dqbuf[ta:ta + SMALL_HALO, :] = jnp.zeros((SMALL_HALO, da), F32)

        @pl.when(i > 0)
        def _():
            dqbuf[ta:ta + SMALL_HALO, :] = dqbuf[0:SMALL_HALO, :]

        for lb in range(da // LANES):
            cs = slice(lb * LANES, (lb + 1) * LANES)
            pbuf[0:SMALL_HALO, cs] = (_last_rows(ach_ref[:, cs]) * _last_rows(axh_ref[:, cs])) * halo_on
            ac, ax, ab, az = (r[:, cs].astype(F32) for r in (ac_ref, ax_ref, ab_ref, az_ref))
            p = ac * ax
            pbuf[SMALL_HALO:SMALL_HALO + ta, cs] = p
            p1 = pbuf[7:7 + ta, cs]
            p2 = pbuf[6:6 + ta, cs]
            w0, w1, w2 = w_ref[0:1, cs], w_ref[1:2, cs], w_ref[2:3, cs]
            q = w0 * p2 + w1 * p1 + w2 * p
            sz, dsz = _silu_and_grad(az)
            dy = dy_ref[:, cs]
            t1 = dy * ab
            dq = t1 * sz
            dqbuf[0:ta, cs] = dq
            dpv = w2 * dq + w1 * dqbuf[1:1 + ta, cs] + w0 * dqbuf[2:2 + ta, cs]
            dp_ref[:, lb * LANES:(lb + 1) * LANES] = (dy * q * sz).astype(BF16)
            dp_ref[:, da + lb * LANES:da + (lb + 1) * LANES] = (dpv * ax).astype(BF16)
            dp_ref[:, 2 * da + lb * LANES:2 * da + (lb + 1) * LANES] = (dpv * ac).astype(BF16)
            dp_ref[:, 3 * da + lb * LANES:3 * da + (lb + 1) * LANES] = (t1 * q * dsz).astype(BF16)
            dw_ref[0, 0:1, cs] = dw_ref[0, 0:1, cs] + jnp.sum(dq * p2, axis=0, keepdims=True)
            dw_ref[0, 1:2, cs] = dw_ref[0, 1:2, cs] + jnp.sum(dq * p1, axis=0, keepdims=True)
            dw_ref[0, 2:3, cs] = dw_ref[0, 2:3, cs] + jnp.sum(dq * p, axis=0, keepdims=True)

    col = lambda k: pl.BlockSpec((ta, da), lambda b, i: (ti(b, i), k))
    halo = lambda k: pl.BlockSpec((HALO_BLOCK, da), lambda b, i: (hi(b, i), k))
    return pl.pallas_call(
        body, name="mix_a_bwd", grid=(dm.BL, nta),
        in_specs=[col(0), col(1), col(2), col(3), halo(1), halo(2),
                  pl.BlockSpec((ta, da), lambda b, i: (ti(b, i), 0)),
                  pl.BlockSpec((CONV_A_K, da), lambda b, i: (0, 0))],
        out_specs=[pl.BlockSpec((ta, dm.WA), lambda b, i: (ti(b, i), 0)),
                   pl.BlockSpec((1, 8, da), lambda b, i: (b, 0, 0))],
        out_shape=[jax.ShapeDtypeStruct((dm.R, dm.WA), BF16), jax.ShapeDtypeStruct((dm.BL, 8, da), F32)],
        scratch_shapes=[pltpu.VMEM((SMALL_HALO + ta, da), F32), pltpu.VMEM((ta + SMALL_HALO, da), F32)],
        compiler_params=_params(("parallel", "arbitrary")),
    )(proj, proj, proj, proj, proj, proj, dya, conv_w)


SUBLANES = 8
SHIFT_ROWS = TT + CONF_HALO - SUBLANES


TAP_ROWS = 64


def _split_lanes(buf, rows, val):
    for lb in range(val.shape[1] // LANES):
        buf[lb, rows, :] = val[:, lb * LANES:(lb + 1) * LANES]


def _join_lanes(buf):
    return jnp.concatenate([buf[lb] for lb in range(buf.shape[0])], axis=1)


def _fill_shifted(buf, shifted):
    def step(lb, carry):
        for r in range(1, SUBLANES):
            shifted[lb, r - 1, 0:SHIFT_ROWS, :] = buf[lb, r:r + SHIFT_ROWS, :]
        return carry

    lax.fori_loop(0, buf.shape[0], step, 0)


def _window(buf, shifted, d, r0, lb):
    r = d % SUBLANES
    rows = pl.ds(pl.multiple_of(r0 + (d - r), SUBLANES), TAP_ROWS)
    return buf[lb, rows, :] if r == 0 else shifted[lb, r - 1, rows, :]


def _tap_loop(nlb, body):
    per_lb = TT // TAP_ROWS

    def step(it, carry):
        lb = it // per_lb
        body(lb, pl.ds(pl.multiple_of(lb * LANES, LANES), LANES), pl.multiple_of((it % per_lb) * TAP_ROWS, TAP_ROWS))
        return carry

    lax.fori_loop(0, nlb * per_lb, step, 0)


TAP_CHAINS = 4


def _tree_sum(terms):
    sums = list(terms[:TAP_CHAINS])
    for n, t in enumerate(terms[TAP_CHAINS:]):
        sums[n % TAP_CHAINS] = sums[n % TAP_CHAINS] + t
    while len(sums) > 1:
        sums = [a + b for a, b in zip(sums[0::2], sums[1::2])] + ([sums[-1]] if len(sums) % 2 else [])
    return sums[0]


def _conf_conv(ubuf, ushift, w_ref, b_ref, u1buf):
    _fill_shifted(ubuf, ushift)

    def piece(lb, cs, r0):
        taps = [w_ref[k:k + 1, cs] * _window(ubuf, ushift, CONF_HALO - (CONF_K - 1) + k, r0, lb)
                for k in range(CONF_K)]
        u1buf[lb, pl.ds(r0, TAP_ROWS), :] = _tree_sum(taps) + b_ref[0:1, cs]

    _tap_loop(ubuf.shape[0], piece)


def _mix_c_fwd(proj, conv_w, conv_b, ln_g, ln_b, dm):
    dc = dm.DC
    nlb = dc // LANES
    c0 = (dm.WA + dm.WB) // dc
    ti = _tile_index(dm, False)

    def body(ca_ref, cg_ref, cz_ref, w_ref, b_ref, g_ref, be_ref, y_ref, ubuf, u1buf, ushift):
        i = pl.program_id(1)

        @pl.when(i == 0)
        def _():
            ubuf[:, 0:CONF_HALO, :] = jnp.zeros((nlb, CONF_HALO, LANES), F32)

        @pl.when(i > 0)
        def _():
            ubuf[:, 0:CONF_HALO, :] = ubuf[:, TT:TT + CONF_HALO, :]

        _split_lanes(ubuf, slice(CONF_HALO, CONF_HALO + TT),
                     ca_ref[...].astype(F32) * _sigmoid(cg_ref[...].astype(F32)))
        _conf_conv(ubuf, ushift, w_ref, b_ref, u1buf)
        u1 = _join_lanes(u1buf)
        mu = jnp.mean(u1, axis=-1, keepdims=True)
        xc = u1 - mu
        rstd = lax.rsqrt(jnp.mean(xc * xc, axis=-1, keepdims=True) + LN_EPS)
        u2 = xc * rstd * g_ref[...] + be_ref[...]
        cz = cz_ref[...].astype(F32)
        y_ref[...] = ((u2 * _sigmoid(u2)) * (cz * _sigmoid(cz))).astype(BF16)

    col = lambda k: pl.BlockSpec((TT, dc), lambda b, i: (ti(b, i), c0 + k))
    vec = pl.BlockSpec((1, dc), lambda b, i: (0, 0))
    return pl.pallas_call(
        body, name="mix_c_fwd", grid=(dm.BL, dm.NT),
        in_specs=[col(0), col(1), col(2), pl.BlockSpec((CONF_K, dc), lambda b, i: (0, 0)), vec, vec, vec],
        out_specs=pl.BlockSpec((TT, dc), lambda b, i: (ti(b, i), 0)),
        out_shape=jax.ShapeDtypeStruct((dm.R, dc), BF16),
        scratch_shapes=[pltpu.VMEM((nlb, CONF_HALO + TT, LANES), F32), pltpu.VMEM((nlb, TT, LANES), F32),
                        pltpu.VMEM((nlb, SUBLANES - 1, SHIFT_ROWS, LANES), F32)],
        compiler_params=_params(("parallel", "arbitrary")),
    )(proj, proj, proj, conv_w, conv_b, ln_g, ln_b)


def _mix_c_bwd(proj, dyc, conv_w, conv_b, ln_g, ln_b, dm):
    dc = dm.DC
    nlb = dc // LANES
    c0 = (dm.WA + dm.WB) // dc
    ti = _tile_index(dm, True)
    hi = _halo_index(dm, CONF_HALO)

    def body(ca_ref, cg_ref, cz_ref, cah_ref, cgh_ref, dy_ref, w_ref, b_ref, g_ref, be_ref,
             dp_ref, dw_ref, dv_ref, ubuf, u1buf, dubuf, du0buf, ushift, dshift, dwacc):
        i = pl.program_id(1)
        halo_on = jnp.where(i == dm.NT - 1, 0.0, 1.0)

        @pl.when(i == 0)
        def _():
            dwacc[...] = jnp.zeros_like(dwacc)
            dv_ref[...] = jnp.zeros_like(dv_ref)
            dubuf[:, TT:TT + CONF_HALO, :] = jnp.zeros((nlb, CONF_HALO, LANES), F32)

        @pl.when(i > 0)
        def _():
            dubuf[:, TT:TT + CONF_HALO, :] = dubuf[:, 0:CONF_HALO, :]

        _split_lanes(ubuf, slice(0, CONF_HALO),
                     cah_ref[...].astype(F32) * _sigmoid(cgh_ref[...].astype(F32)) * halo_on)
        sgg = _sigmoid(cg_ref[...].astype(F32))
        ca = ca_ref[...].astype(F32)
        _split_lanes(ubuf, slice(CONF_HALO, CONF_HALO + TT), ca * sgg)
        _conf_conv(ubuf, ushift, w_ref, b_ref, u1buf)
        u1 = _join_lanes(u1buf)
        mu = jnp.mean(u1, axis=-1, keepdims=True)
        xc = u1 - mu
        rstd = lax.rsqrt(jnp.mean(xc * xc, axis=-1, keepdims=True) + LN_EPS)
        xhat = xc * rstd
        u2 = xhat * g_ref[...] + be_ref[...]
        su, dsu = _silu_and_grad(u2)
        sz, dsz = _silu_and_grad(cz_ref[...].astype(F32))
        dy = dy_ref[...]
        du2 = dy * dsu * sz
        dp_ref[:, 2 * dc:3 * dc] = (dy * su * dsz).astype(BF16)
        dxhat = du2 * g_ref[...]
        du1 = rstd * (dxhat - jnp.mean(dxhat, axis=-1, keepdims=True)
                      - xhat * jnp.mean(dxhat * xhat, axis=-1, keepdims=True))
        dv_ref[0, 0:1, :] = dv_ref[0, 0:1, :] + jnp.sum(du1, axis=0, keepdims=True)
        dv_ref[0, 1:2, :] = dv_ref[0, 1:2, :] + jnp.sum(du2 * xhat, axis=0, keepdims=True)
        dv_ref[0, 2:3, :] = dv_ref[0, 2:3, :] + jnp.sum(du2, axis=0, keepdims=True)
        _split_lanes(dubuf, slice(0, TT), du1)
        _fill_shifted(dubuf, dshift)

        def piece(lb, cs, r0):
            du0buf[lb, pl.ds(r0, TAP_ROWS), :] = _tree_sum(
                [w_ref[k:k + 1, cs] * _window(dubuf, dshift, CONF_K - 1 - k, r0, lb) for k in range(CONF_K)])
            d1 = dubuf[lb, pl.ds(r0, TAP_ROWS), :]
            for k in range(CONF_K):
                prod = d1 * _window(ubuf, ushift, CONF_HALO - (CONF_K - 1) + k, r0, lb)
                dwacc[lb, k] = dwacc[lb, k] + jnp.sum(prod.reshape(TAP_ROWS // SUBLANES, SUBLANES, LANES), axis=0)

        _tap_loop(nlb, piece)
        du0 = _join_lanes(du0buf)
        dp_ref[:, 0:dc] = (du0 * sgg).astype(BF16)
        dp_ref[:, dc:2 * dc] = (du0 * ca * sgg * (1.0 - sgg)).astype(BF16)

        @pl.when(i == dm.NT - 1)
        def _():
            for lb in range(nlb):
                dw_ref[0, 0:CONF_K, lb * LANES:(lb + 1) * LANES] = jnp.sum(dwacc[lb], axis=1)
            dw_ref[0, CONF_K:CONF_K + 1, :] = jnp.zeros((1, dc), F32)

    col = lambda k: pl.BlockSpec((TT, dc), lambda b, i: (ti(b, i), c0 + k))
    halo = lambda k: pl.BlockSpec((CONF_HALO, dc), lambda b, i: (hi(b, i), c0 + k))
    vec = pl.BlockSpec((1, dc), lambda b, i: (0, 0))
    return pl.pallas_call(
        body, name="mix_c_bwd", grid=(dm.BL, dm.NT),
        in_specs=[col(0), col(1), col(2), halo(0), halo(1),
                  pl.BlockSpec((TT, dc), lambda b, i: (ti(b, i), 0)),
                  pl.BlockSpec((CONF_K, dc), lambda b, i: (0, 0)), vec, vec, vec],
        out_specs=[pl.BlockSpec((TT, dm.WC), lambda b, i: (ti(b, i), 0)),
                   pl.BlockSpec((1, 32, dc), lambda b, i: (b, 0, 0)),
                   pl.BlockSpec((1, 8, dc), lambda b, i: (b, 0, 0))],
        out_shape=[jax.ShapeDtypeStruct((dm.R, dm.WC), BF16),
                   jax.ShapeDtypeStruct((dm.BL, 32, dc), F32),
                   jax.ShapeDtypeStruct((dm.BL, 8, dc), F32)],
        scratch_shapes=[pltpu.VMEM((nlb, CONF_HALO + TT, LANES), F32), pltpu.VMEM((nlb, TT, LANES), F32),
                        pltpu.VMEM((nlb, TT + CONF_HALO, LANES), F32), pltpu.VMEM((nlb, TT, LANES), F32),
                        pltpu.VMEM((nlb, SUBLANES - 1, SHIFT_ROWS, LANES), F32),
                        pltpu.VMEM((nlb, SUBLANES - 1, SHIFT_ROWS, LANES), F32),
                        pltpu.VMEM((nlb, CONF_K, SUBLANES, LANES), F32)],
        compiler_params=_params(("parallel", "arbitrary")),
    )(proj, proj, proj, proj, proj, dyc, conv_w, conv_b, ln_g, ln_b)


RAW_HALO = 16


def _shift_matrix():
    r = jnp.arange((SSM_CONV_K - 1) * TT)[:, None]
    want = TT + r % TT - (SSM_CONV_K - 1 - r // TT)
    return (jnp.arange(2 * TT)[None, :] == want).astype(BF16)


def _ssm_conv(rawwin, sh, s_ref, w_ref, b_ref, width):
    sh[...] = _dot(s_ref[...], rawwin[...])
    for lb in range(width // LANES):
        cs = slice(lb * LANES, (lb + 1) * LANES)
        acc = b_ref[0:1, cs] + w_ref[SSM_CONV_K - 1:SSM_CONV_K, cs] * rawwin[TT:2 * TT, cs].astype(F32)
        for k in range(SSM_CONV_K - 1):
            acc = acc + w_ref[k:k + 1, cs] * sh[k * TT:(k + 1) * TT, cs]
        yield cs, acc


def _softplus(z):
    return jnp.maximum(z, 0.0) + jnp.log(1.0 + jnp.exp(-jnp.abs(z)))


def _tri(lower):
    r = lax.broadcasted_iota(jnp.int32, (TT, TT), 0)
    c = lax.broadcasted_iota(jnp.int32, (TT, TT), 1)
    return (c <= r) if lower else (c >= r)


def _exact_01_dot(mat01, x):
    x1, x2, x3 = _split3(x)
    return _dot(mat01, x1) + _dot(mat01, x2) + _dot(mat01, x3)


def _head_scalars(dt_ref, dtb_ref, alog_ref):
    z = dt_ref[...] + dtb_ref[...]
    dtv = _softplus(z)
    a = -jnp.exp(alog_ref[...])
    ac = _exact_01_dot(_tri(True).astype(F32).astype(BF16), dtv * a)
    eac = jnp.exp(ac)
    dst = jnp.exp(ac[TT - 1:TT, :] - ac)
    return z, dtv, a, ac, eac, dst


FAR_BELOW = -1e30


def _decay(ac, ac_t, h, causal):
    return jnp.exp(jnp.where(causal, ac[:, h:h + 1] - ac_t[h:h + 1, :], FAR_BELOW))


def _own_half(x16, h):
    lane = lax.broadcasted_iota(jnp.int32, (1, LANES), 1)
    keep = (lane >= SSM_HEAD_DIM) if (h % 2) else (lane < SSM_HEAD_DIM)
    return jnp.where(keep, x16, jnp.zeros_like(x16))


def _per_sequence(setup, body, bl, how):
    def all_sequences(*refs):
        views = [[r.at[b] if h is True else (r.at[pl.ds(b, 1)] if h == "keep" else r) for r, h in zip(refs, how)]
                 for b in range(bl)]
        for v in views:
            setup(*v)
        running = [body(*v) for v in views]
        while running:
            running = [g for g in running if next(g, "done") != "done"]

    return all_sequences


def _mix_b_fwd(proj, projdt, conv_w, conv_b, dt_bias, a_log, dskx, norm_g, expand, dm, rider=None):
    db, gn, xbc_w, hpg = dm.DB, dm.GN, dm.XBC, dm.HPG
    gw = db // SSM_GROUPS

    def setup(bz_ref, bx_ref, bc_ref, dt_ref, w_ref, b_ref, dtb_ref, alog_ref, dsk_ref, g_ref, e_ref, s_ref,
              y_ref, yraw_ref, sprev_ref, rawwin, sh, xbuf, state, ybuf, exbuf, xdtbuf):
        i = pl.program_id(0)

        @pl.when(i == 0)
        def _():
            rawwin[0:TT, :] = jnp.zeros((TT, xbc_w), BF16)
            state[...] = jnp.zeros_like(state)

        @pl.when(i > 0)
        def _():
            rawwin[TT - RAW_HALO:TT, :] = rawwin[2 * TT - RAW_HALO:2 * TT, :]

    def body(bz_ref, bx_ref, bc_ref, dt_ref, w_ref, b_ref, dtb_ref, alog_ref, dsk_ref, g_ref, e_ref, s_ref,
             y_ref, yraw_ref, sprev_ref, rawwin, sh, xbuf, state, ybuf, exbuf, xdtbuf):
        rawwin[TT:2 * TT, 0:db] = bx_ref[...]
        rawwin[TT:2 * TT, db:xbc_w] = bc_ref[...]
        for cs, pre in _ssm_conv(rawwin, sh, s_ref, w_ref, b_ref, xbc_w):
            xbuf[:, cs] = pre * _sigmoid(pre)
            yield

        _, dtv, _, ac, eac, dst = _head_scalars(dt_ref, dtb_ref, alog_ref)
        exbuf[...] = _dot(jnp.concatenate([dtv, eac, dst], axis=0).astype(BF16), e_ref[...])
        ac_t = ac.T
        causal = _tri(True)
        sprev_ref[0, 0] = state[...]
        yield

        xdtbuf[...] = xbuf[:, 0:db] * exbuf[0:TT, :]
        ybuf[...] = xbuf[:, 0:db] * dsk_ref[...]
        for g in range(SSM_GROUPS):
            gs = slice(g * gw, (g + 1) * gw)
            bg = xbuf[:, db + g * SSM_STATE:db + (g + 1) * SSM_STATE].astype(BF16)
            cg = xbuf[:, db + gn + g * SSM_STATE:db + gn + (g + 1) * SSM_STATE].astype(BF16)
            cb = _dot_nt(cg, bg)
            for e in range(0, hpg, 2):
                h = g * hpg + e
                ps = slice(h * SSM_HEAD_DIM, (h + 2) * SSM_HEAD_DIM)
                xp16 = xdtbuf[:, ps].astype(BF16)
                acc = jnp.zeros((TT, LANES), F32)
                for hh in (h, h + 1):
                    mm = (cb * _decay(ac, ac_t, hh, causal)).astype(BF16)
                    acc = acc + _dot(mm, _own_half(xp16, hh))
                ybuf[:, ps] = ybuf[:, ps] + acc
                yield
            sg = state[:, gs]
            ybuf[:, gs] = ybuf[:, gs] + exbuf[TT:2 * TT, gs] * _dot(cg, sg.astype(BF16))
            state[:, gs] = sg * exbuf[2 * TT - 1:2 * TT, gs] + _dot_tn(
                bg, (xdtbuf[:, gs] * exbuf[2 * TT:3 * TT, gs]).astype(BF16))
            yield

        yraw = ybuf[...]
        yraw_ref[...] = yraw
        bz = bz_ref[...].astype(F32)
        v = yraw * (bz * _sigmoid(bz))
        r = lax.rsqrt(jnp.mean(v * v, axis=-1, keepdims=True) + NORM_EPS)
        y_ref[...] = (v * r * g_ref[...]).astype(BF16)

    bl = dm.BL
    tile = lambda w, k: pl.BlockSpec((bl, TT, w), lambda i: (0, i, k))
    fixed = lambda r, w: pl.BlockSpec((r, w), lambda i: (0, 0))
    proj3, dt3 = proj.reshape(bl, dm.Lp, dm.NP), projdt.reshape(bl, dm.Lp, DT_PAD)
    scratch = [((2 * TT, xbc_w), BF16), (((SSM_CONV_K - 1) * TT, xbc_w), F32), ((TT, xbc_w), F32),
               ((SSM_STATE, db), F32), ((TT, db), F32), ((3 * TT, db), F32), ((TT, db), F32)]
    (y, yraw, sprev), rode = _call(
        _per_sequence(setup, body, bl, [True] * 4 + [False] * 8 + [True, True, "keep"] + [True] * len(scratch)),
        "mix_b_fwd", (dm.NT,),
        [tile(db, dm.WA // db), tile(db, dm.WA // db + 1), tile(2 * gn, (dm.WA + 2 * db) // (2 * gn)),
         tile(DT_PAD, 0),
         fixed(SSM_CONV_K, xbc_w), fixed(1, xbc_w), fixed(1, DT_PAD), fixed(1, DT_PAD),
         fixed(1, db), fixed(1, db), fixed(DT_PAD, db), fixed((SSM_CONV_K - 1) * TT, 2 * TT)],
        [tile(db, 0), tile(db, 0), pl.BlockSpec((bl, 1, SSM_STATE, db), lambda i: (0, i, 0, 0))],
        [jax.ShapeDtypeStruct((bl, dm.Lp, db), BF16), jax.ShapeDtypeStruct((bl, dm.Lp, db), F32),
         jax.ShapeDtypeStruct((bl, dm.NT, SSM_STATE, db), F32)],
        [pltpu.VMEM((bl,) + s, t) for s, t in scratch],
        ("arbitrary",),
        (proj3, proj3, proj3, dt3, conv_w, conv_b, dt_bias, a_log, dskx, norm_g, expand, _shift_matrix()), rider)
    return (y.reshape(dm.R, db), yraw.reshape(dm.R, db), sprev), rode


def _mix_b_bwd(proj, projdt, dyb, yraw, sprev, conv_w, conv_b, dt_bias, a_log, dskx, norm_g, expand, expand_t, dm,
               rider=None):
    db, gn, xbc_w, hpg = dm.DB, dm.GN, dm.XBC, dm.HPG
    gw = db // SSM_GROUPS

    def setup(bz_ref, bx_ref, bc_ref, dt_ref, bxh_ref, bch_ref, dy_ref, yraw_ref, sprev_ref,
              w_ref, b_ref, dtb_ref, alog_ref, dsk_ref, g_ref, e_ref, et_ref, s_ref,
              dp_ref, dpt_ref, dwc_ref, dch_ref, dhd_ref,
              rawwin, sh, xbuf, dsbuf, dstate, dxbuf, z1buf, dprebuf, exbuf, xdtbuf, dyrbuf, uvec):
        i = pl.program_id(0)

        @pl.when(i == 0)
        def _():
            dwc_ref[...] = jnp.zeros_like(dwc_ref)
            dch_ref[...] = jnp.zeros_like(dch_ref)
            dhd_ref[...] = jnp.zeros_like(dhd_ref)
            dstate[...] = jnp.zeros_like(dstate)
            dprebuf[TT:TT + SMALL_HALO, :] = jnp.zeros((SMALL_HALO, xbc_w), F32)
            rawwin[0:TT - RAW_HALO, :] = jnp.zeros((TT - RAW_HALO, xbc_w), BF16)

        @pl.when(i > 0)
        def _():
            dprebuf[TT:TT + SMALL_HALO, :] = dprebuf[0:SMALL_HALO, :]

    def body(bz_ref, bx_ref, bc_ref, dt_ref, bxh_ref, bch_ref, dy_ref, yraw_ref, sprev_ref,
             w_ref, b_ref, dtb_ref, alog_ref, dsk_ref, g_ref, e_ref, et_ref, s_ref,
             dp_ref, dpt_ref, dwc_ref, dch_ref, dhd_ref,
             rawwin, sh, xbuf, dsbuf, dstate, dxbuf, z1buf, dprebuf, exbuf, xdtbuf, dyrbuf, uvec):
        halo_on = jnp.where(pl.program_id(0) == dm.NT - 1, 0.0, 1.0).astype(BF16)

        rawwin[TT - RAW_HALO:TT, 0:db] = bxh_ref[...] * halo_on
        rawwin[TT - RAW_HALO:TT, db:xbc_w] = bch_ref[...] * halo_on
        rawwin[TT:2 * TT, 0:db] = bx_ref[...]
        rawwin[TT:2 * TT, db:xbc_w] = bc_ref[...]
        for cs, pre in _ssm_conv(rawwin, sh, s_ref, w_ref, b_ref, xbc_w):
            sl, dsl = _silu_and_grad(pre)
            xbuf[:, cs] = sl
            dsbuf[:, cs] = dsl
            yield

        z, dtv, a, ac, eac, dst = _head_scalars(dt_ref, dtb_ref, alog_ref)
        exbuf[...] = _dot(jnp.concatenate([dtv, eac, dst], axis=0).astype(BF16), e_ref[...])
        ac_t = ac.T
        causal = _tri(True)
        xdtbuf[...] = xbuf[:, 0:db] * exbuf[0:TT, :]

        yraw = yraw_ref[...]
        sz, dsz = _silu_and_grad(bz_ref[...].astype(F32))
        v = yraw * sz
        r = lax.rsqrt(jnp.mean(v * v, axis=-1, keepdims=True) + NORM_EPS)
        dy = dy_ref[...]
        dyg = dy * g_ref[...]
        dv = r * dyg - v * (r * r * r * jnp.mean(dyg * v, axis=-1, keepdims=True))
        dch_ref[0, 0:1, :] = dch_ref[0, 0:1, :] + jnp.sum(dy * v * r, axis=0, keepdims=True)
        dyr = dv * sz
        dyrbuf[...] = dyr
        dp_ref[:, 0:db] = (dv * yraw * dsz).astype(BF16)
        dch_ref[0, 1:2, :] = dch_ref[0, 1:2, :] + jnp.sum(dyr * xbuf[:, 0:db], axis=0, keepdims=True)

        lane_row = lax.broadcasted_iota(jnp.int32, (1, LANES), 1)
        sub_col = lax.broadcasted_iota(jnp.int32, (LANES, 1), 0)
        dac = jnp.zeros((TT, LANES), F32)
        colacc = jnp.zeros((LANES, TT), F32)
        for g in range(SSM_GROUPS):
            gs = slice(g * gw, (g + 1) * gw)
            bs_ = slice(db + g * SSM_STATE, db + (g + 1) * SSM_STATE)
            cs_ = slice(db + gn + g * SSM_STATE, db + gn + (g + 1) * SSM_STATE)
            bg = xbuf[:, bs_].astype(BF16)
            cg = xbuf[:, cs_].astype(BF16)
            cb = _dot_nt(cg, bg)
            dcb = jnp.zeros((TT, TT), F32)
            for e in range(0, hpg, 2):
                h = g * hpg + e
                ps = slice(h * SSM_HEAD_DIM, (h + 2) * SSM_HEAD_DIM)
                xp16 = xdtbuf[:, ps].astype(BF16)
                dyp16 = dyrbuf[:, ps].astype(BF16)
                acc = jnp.zeros((TT, LANES), F32)
                for hh in (h, h + 1):
                    dec = _decay(ac, ac_t, hh, causal)
                    mm = cb * dec
                    dyh = _own_half(dyp16, hh)
                    dmm = _dot_nt(dyh, xp16)
                    acc = acc + _dot_tn(mm.astype(BF16), dyh)
                    dcb = dcb + dmm * dec
                    gm = dmm * mm
                    dac = dac + jnp.sum(gm, axis=1, keepdims=True) * (lane_row == hh).astype(F32)
                    colacc = colacc + (sub_col == hh).astype(F32) * jnp.sum(gm, axis=0, keepdims=True)
                dxbuf[:, ps] = acc
                yield
            sg32 = sprev_ref[0, 0, :, gs]
            sg = sg32.astype(BF16)
            dsn = dstate[:, gs]
            dsn16 = dsn.astype(BF16)
            dcb16 = dcb.astype(BF16)
            eacx = exbuf[TT:2 * TT, gs]
            dstx = exbuf[2 * TT:3 * TT, gs]
            cdx = exbuf[2 * TT - 1:2 * TT, gs]
            dye16 = (dyrbuf[:, gs] * eacx).astype(BF16)
            xdt_g = xdtbuf[:, gs]
            dxbuf[:, cs_] = _dot(dcb16, bg) + _dot_nt(dye16, sg)
            dst_x = dstx * _dot(bg, dsn16)
            dxbuf[:, bs_] = _dot_tn(dcb16, cg) + _dot_nt((dstx * xdt_g).astype(BF16), dsn16)
            dstate[:, gs] = cdx * dsn + _dot_tn(cg, dye16)
            z1buf[:, gs] = dyrbuf[:, gs] * (eacx * _dot(cg, sg)) - xdt_g * dst_x
            uvec[:, gs] = jnp.broadcast_to(
                jnp.sum(xdt_g * dst_x, axis=0, keepdims=True) + jnp.sum(dsn * cdx * sg32, axis=0, keepdims=True),
                (8, gw))
            dxbuf[:, gs] = dxbuf[:, gs] + dst_x
            yield

        zz = _dot(jnp.concatenate([z1buf[...], dxbuf[:, 0:db] * xbuf[:, 0:db]], axis=0).astype(BF16), et_ref[...])
        u1, u2, u3 = _split3(uvec[...])
        ulast = (_dot(u1, et_ref[...]) + _dot(u2, et_ref[...]) + _dot(u3, et_ref[...]))[0:1, :]
        is_last = (lax.broadcasted_iota(jnp.int32, (TT, 1), 0) == TT - 1).astype(F32)
        dac = dac - colacc.T + zz[0:TT] + is_last * ulast
        dda = _exact_01_dot(_tri(False).astype(F32).astype(BF16), dac)
        ddt = dda * a + zz[TT:2 * TT]
        dhd_ref[0, 1:2, :] = dhd_ref[0, 1:2, :] + jnp.sum(dda * dtv, axis=0, keepdims=True) * a
        ddtraw = ddt * _sigmoid(z)
        dhd_ref[0, 0:1, :] = dhd_ref[0, 0:1, :] + jnp.sum(ddtraw, axis=0, keepdims=True)
        dpt_ref[...] = ddtraw.astype(BF16)
        dxbuf[:, 0:db] = dyrbuf[...] * dsk_ref[...] + dxbuf[:, 0:db] * exbuf[0:TT, :]

        for lb in range(xbc_w // LANES):
            cs = slice(lb * LANES, (lb + 1) * LANES)
            dpre = dxbuf[:, cs] * dsbuf[:, cs]
            dprebuf[0:TT, cs] = dpre
            dwc_ref[0, SSM_CONV_K:SSM_CONV_K + 1, cs] = dwc_ref[0, SSM_CONV_K:SSM_CONV_K + 1, cs] + jnp.sum(
                dpre, axis=0, keepdims=True)
            draw = w_ref[SSM_CONV_K - 1:SSM_CONV_K, cs] * dpre
            for k in range(SSM_CONV_K - 1):
                ahead = SSM_CONV_K - 1 - k
                draw = draw + w_ref[k:k + 1, cs] * dprebuf[ahead:ahead + TT, cs]
            for k in range(SSM_CONV_K):
                moved = sh[k * TT:(k + 1) * TT, cs] if k < SSM_CONV_K - 1 else rawwin[TT:2 * TT, cs].astype(F32)
                dwc_ref[0, k:k + 1, cs] = dwc_ref[0, k:k + 1, cs] + jnp.sum(dpre * moved, axis=0, keepdims=True)
            dp_ref[:, db + lb * LANES:db + (lb + 1) * LANES] = draw.astype(BF16)
            yield

    bl, nt = dm.BL, dm.NT
    tile = lambda w, k: pl.BlockSpec((bl, TT, w), lambda i: (0, nt - 1 - i, k))
    halo = lambda w, k: pl.BlockSpec((bl, HALO_BLOCK, w),
                                     lambda i: (0, jnp.maximum((nt - 1 - i) * (TT // HALO_BLOCK) - 1, 0), k))
    fixed = lambda r, w: pl.BlockSpec((r, w), lambda i: (0, 0))
    sums = lambda w: pl.BlockSpec((bl, 8, w), lambda i: (0, 0, 0))
    kz = dm.WA // db
    kc = (dm.WA + 2 * db) // (2 * gn)
    proj3, dt3 = proj.reshape(bl, dm.Lp, dm.NP), projdt.reshape(bl, dm.Lp, DT_PAD)
    scratch = [((2 * TT, xbc_w), BF16), (((SSM_CONV_K - 1) * TT, xbc_w), F32), ((TT, xbc_w), F32),
               ((TT, xbc_w), F32), ((SSM_STATE, db), F32), ((TT, xbc_w), F32), ((TT, db), F32),
               ((TT + SMALL_HALO, xbc_w), F32), ((3 * TT, db), F32), ((TT, db), F32), ((TT, db), F32), ((8, db), F32)]
    how = [True] * 8 + ["keep"] + [False] * 9 + [True, True, "keep", "keep", "keep"] + [True] * len(scratch)
    (dp, dpt, dwc, dch, dhd), rode = _call(
        _per_sequence(setup, body, bl, how), "mix_b_bwd", (nt,),
        [tile(db, kz), tile(db, kz + 1), tile(2 * gn, kc), tile(DT_PAD, 0),
         halo(db, kz + 1), halo(2 * gn, kc), tile(db, 0), tile(db, 0),
         pl.BlockSpec((bl, 1, SSM_STATE, db), lambda i: (0, nt - 1 - i, 0, 0)),
         fixed(SSM_CONV_K, xbc_w), fixed(1, xbc_w), fixed(1, DT_PAD), fixed(1, DT_PAD),
         fixed(1, db), fixed(1, db), fixed(DT_PAD, db), fixed(db, DT_PAD), fixed((SSM_CONV_K - 1) * TT, 2 * TT)],
        [tile(dm.WB, 0), tile(DT_PAD, 0), sums(xbc_w), sums(db), sums(DT_PAD)],
        [jax.ShapeDtypeStruct((bl, dm.Lp, dm.WB), BF16), jax.ShapeDtypeStruct((bl, dm.Lp, DT_PAD), BF16),
         jax.ShapeDtypeStruct((bl, 8, xbc_w), F32), jax.ShapeDtypeStruct((bl, 8, db), F32),
         jax.ShapeDtypeStruct((bl, 8, DT_PAD), F32)],
        [pltpu.VMEM((bl,) + s, t) for s, t in scratch],
        ("arbitrary",),
        (proj3, proj3, proj3, dt3, proj3, proj3, dyb.reshape(bl, dm.Lp, db), yraw.reshape(bl, dm.Lp, db), sprev,
         conv_w, conv_b, dt_bias, a_log, dskx, norm_g, expand, expand_t, _shift_matrix()), rider)
    return (dp.reshape(dm.R, dm.WB), dpt.reshape(dm.R, DT_PAD), dwc, dch, dhd), rode


def _head_consts(dm):
    head_of = jnp.arange(dm.DB) // SSM_HEAD_DIM
    expand = (jnp.arange(DT_PAD)[:, None] == head_of[None, :]).astype(BF16)
    return expand, expand.T


def _ssm_params(lw, dm):
    pad_h = lambda v: jnp.pad(v, (0, DT_PAD - dm.H))[None]
    return (lw["ssm_conv_w"], lw["ssm_conv_b"][None], pad_h(lw["dt_bias"]), pad_h(lw["a_log"]),
            jnp.repeat(lw["d_skip"], SSM_HEAD_DIM)[None], lw["ssm_norm_g"][None])


def _layer_fwd(h, lw, w_in, w_out, cst, dm, next_bases=None):
    nxt = next_bases is not None
    (proj, projdt, hn), got = _fwd_in(h, lw["pre_g"][None], w_in, dm,
                                      _ride_gather_ici(next_bases, 0, 2) if nxt else None)
    ya = _mix_a_fwd(proj, lw["conv_a_w"], dm)
    (yb, yraw, sprev), got = _mix_b_fwd(proj, projdt, *_ssm_params(lw, dm), cst[0], dm,
                                        _ride_gather_ici(got, 1, 2) if nxt else None)
    yc = _mix_c_fwd(proj, lw["conf_conv_w"], lw["conf_conv_b"][None], lw["conf_ln_g"][None],
                    lw["conf_ln_b"][None], dm)
    (h_new, m), got = _fwd_out(ya, yb, yc, w_out, h, lw["post_g"][None], dm, _ride_gather_d2d(got) if nxt else None)
    return h_new, (h, hn, proj, projdt, ya, yb, yc, yraw, sprev, m), got


def _layer_bwd(dh, saved, lw, w_in, w_out, cst, dm, reduce=None, last=False):
    h_in, hn, proj, projdt, ya, yb, yc, yraw, sprev, m = saved
    (dya, dyb, dyc, dwo, dpost), got = _bwd_out(dh, m, lw["post_g"][None], w_out, ya, yb, yc, dm,
                                                None if reduce is None else reduce.swap())
    dpa, dwa = _mix_a_bwd(proj, dya, lw["conv_a_w"], dm)
    (dpb, dpt, dwcv, dch, dhd), got = _mix_b_bwd(proj, projdt, dyb, yraw, sprev, *_ssm_params(lw, dm), cst[0],
                                                 cst[1], dm, None if reduce is None else reduce.to_owners(got))
    dpc, dwcf, dvc = _mix_c_bwd(proj, dyc, lw["conf_conv_w"], lw["conf_conv_b"][None], lw["conf_ln_g"][None],
                                lw["conf_ln_b"][None], dm)
    def own_reduce():
        pieces = [_bwd_in_dw(hn, dp, dm, n) for dp, n in ((dpa, "a"), (dpb, "b"), (dpc, "c"), (dpt, "dt"))]
        return _GradReduce([_grad_to_shards(pieces, dm), dwo.reshape(N_CHIPS, 2 * dm.D // N_CHIPS, dm.D)])

    rider = None if reduce is None else reduce.join(got)
    n_join = 0 if rider is None else len(rider.out_shapes)
    if last:
        mine = own_reduce()
        to_owners = mine.to_owners(_exchange("grad_swap_halves", mine.swap()))
        rider = to_owners if rider is None else _ride_both(rider, to_owners)
    (dh, dpre), got = _bwd_in_dx(dpa, dpb, dpc, dpt, w_in, h_in, dh, lw["pre_g"][None], dm, rider)
    if reduce is not None:
        reduce.finish(got[:n_join])
    if last:
        mine.finish(_exchange("grad_join_halves", mine.join(got[n_join:])))
    else:
        mine = own_reduce()
    dwcv, dch, dhd, dvc = (jnp.sum(a, axis=0) for a in (dwcv, dch, dhd, dvc))
    small = dict(pre_g=dpre[0], post_g=dpost[0], conv_a_w=jnp.sum(dwa, axis=0)[:CONV_A_K],
                 ssm_conv_w=dwcv[:SSM_CONV_K], ssm_conv_b=dwcv[SSM_CONV_K], ssm_norm_g=dch[0],
                 d_skip=jnp.sum(dch[1].reshape(dm.H, SSM_HEAD_DIM), axis=1), dt_bias=dhd[0, :dm.H],
                 a_log=dhd[1, :dm.H], conf_conv_w=jnp.sum(dwcf, axis=0)[:CONF_K], conf_conv_b=dvc[0],
                 conf_ln_g=dvc[1], conf_ln_b=dvc[2])
    return dh, mine, small


def _shard_runs(dm):
    ab = dm.WA + dm.WB
    order = [(0, 0, ab), (ab, dm.NP - DT_PAD, dm.H), (ab + dm.H, ab, dm.WC)]
    k = dm.NIN // N_CHIPS
    runs = []
    for s in range(N_CHIPS):
        for o0, m0, wd in order:
            lo, hi = max(o0, s * k), min(o0 + wd, (s + 1) * k)
            if lo < hi:
                runs.append((s, lo - s * k, m0 + lo - o0, hi - lo))
    return runs


def _w_in_from_shards(base, dm):
    tr = _row_tile(dm.D, 256)
    k = dm.NIN // N_CHIPS
    runs = _shard_runs(dm)

    def body(in_ref, out_ref):
        for s, sc, mc, wd in runs:
            out_ref[:, mc:mc + wd] = in_ref[s, :, sc:sc + wd]
        out_ref[:, dm.NP - DT_PAD + dm.H:dm.NP] = jnp.zeros((tr, DT_PAD - dm.H), BF16)

    return pl.pallas_call(
        body, name="w_in_from_shards", grid=(dm.D // tr,),
        in_specs=[pl.BlockSpec((N_CHIPS, tr, k), lambda r: (0, r, 0))],
        out_specs=pl.BlockSpec((tr, dm.NP), lambda r: (r, 0)),
        out_shape=jax.ShapeDtypeStruct((dm.D, dm.NP), BF16),
        compiler_params=_params(("parallel",)),
    )(base)


def _grad_to_shards(pieces, dm):
    tr = _row_tile(dm.D, 256)
    k = dm.NIN // N_CHIPS
    starts = [0, dm.WA, dm.WA + dm.WB, dm.NP - DT_PAD]
    widths = [dm.WA, dm.WB, dm.WC, DT_PAD]
    runs = _shard_runs(dm)

    def body(a_ref, b_ref, c_ref, t_ref, out_ref):
        refs = (a_ref, b_ref, c_ref, t_ref)
        for s, sc, mc, wd in runs:
            for p in range(4):
                lo, hi = max(mc, starts[p]), min(mc + wd, starts[p] + widths[p])
                if lo < hi:
                    out_ref[s, :, sc + lo - mc:sc + hi - mc] = refs[p][:, lo - starts[p]:hi - starts[p]].astype(BF16)

    return pl.pallas_call(
        body, name="grad_to_shards", grid=(dm.D // tr,),
        in_specs=[pl.BlockSpec((tr, w), lambda r: (r, 0)) for w in widths],
        out_specs=pl.BlockSpec((N_CHIPS, tr, k), lambda r: (0, r, 0)),
        out_shape=jax.ShapeDtypeStruct((N_CHIPS, dm.D, k), BF16),
        compiler_params=_params(("parallel",)),
    )(*pieces)


def _place_own(w, layer, me):
    _, rows, cols = w.shape
    tr = _row_tile(rows, 256)

    def body(me_ref, w_ref, out_ref):
        out_ref[0] = w_ref[0].astype(BF16)

    return pl.pallas_call(
        body, name="place_own",
        grid_spec=pltpu.PrefetchScalarGridSpec(
            num_scalar_prefetch=1, grid=(rows // tr,),
            in_specs=[pl.BlockSpec((1, tr, cols), lambda r, me_ref: (layer, r, 0))],
            out_specs=pl.BlockSpec((1, tr, cols), lambda r, me_ref: (me_ref[0], r, 0))),
        out_shape=jax.ShapeDtypeStruct((N_CHIPS, rows, cols), BF16),
        compiler_params=_params(("parallel",)),
    )(me, w)


def _add_halves(g, got, c, name):
    _, _, rows, cols = g.shape
    tr = _row_tile(rows, 256)

    def body(c_ref, g_ref, got_ref, out_ref):
        out_ref[0] = (g_ref[0, 0].astype(F32) + got_ref[0].astype(F32)).astype(BF16)

    return pl.pallas_call(
        body, name=name,
        grid_spec=pltpu.PrefetchScalarGridSpec(
            num_scalar_prefetch=1, grid=(N_CHIPS, rows // tr),
            in_specs=[pl.BlockSpec((1, 1, tr, cols), lambda s, r, c_ref: (s, c_ref[0], r, 0)),
                      pl.BlockSpec((1, tr, cols), lambda s, r, c_ref: (s, r, 0))],
            out_specs=pl.BlockSpec((1, tr, cols), lambda s, r, c_ref: (s, r, 0))),
        out_shape=jax.ShapeDtypeStruct((N_CHIPS, rows, cols), BF16),
        compiler_params=_params(("parallel", "parallel")),
    )(c, g, got)


def _add_owner(p, got, where, name):
    _, rows, cols = p.shape
    tr = _row_tile(rows, 256)

    def body(w_ref, p_ref, got_ref, out_ref):
        acc = p_ref[0].astype(F32)
        for j in range(3):
            acc = acc + got_ref[j].astype(F32)
        out_ref[0] = acc

    return pl.pallas_call(
        body, name=name,
        grid_spec=pltpu.PrefetchScalarGridSpec(
            num_scalar_prefetch=1, grid=(rows // tr,),
            in_specs=[pl.BlockSpec((1, tr, cols), lambda r, w_ref: (w_ref[0], r, 0)),
                      pl.BlockSpec((3, tr, cols), lambda r, w_ref: (0, r, 0))],
            out_specs=pl.BlockSpec((1, tr, cols), lambda r, w_ref: (w_ref[1], r, 0))),
        out_shape=jax.ShapeDtypeStruct((2, rows, cols), F32),
        compiler_params=_params(("parallel",)),
    )(where, p, got)


class _GradReduce:
    def __init__(self, gs):
        self.gs = [g.reshape((N_CHIPS, 2, g.shape[1] // 2) + g.shape[2:]) for g in gs]
        self.c = lax.axis_index("c").astype(jnp.int32).reshape(1)
        chip = (2 * lax.axis_index("x") + lax.axis_index("y")).astype(jnp.int32)
        self.where = jnp.stack([chip, self.c[0]])
        self.result = None

    def swap(self):
        return _ride_swap_halves(self.gs)

    def to_owners(self, got):
        self.ps = [_add_halves(g, r, self.c, "grad_add_sibling_" + n) for g, r, n in zip(self.gs, got, ("in", "out"))]
        return _ride_to_owners(self.ps)

    def join(self, got):
        qs = [_add_owner(p, r, self.where, "grad_add_chips_" + n) for p, r, n in zip(self.ps, got, ("in", "out"))]
        return _ride_join_halves(qs)

    def finish(self, got):
        self.result = [a.reshape((a.shape[0] * a.shape[1],) + a.shape[2:]) for a in got]


def _adamw_math(w, g, m, v):
    m = ADAM_B1 * m + (1.0 - ADAM_B1) * g
    v = ADAM_B2 * v + (1.0 - ADAM_B2) * (g * g)
    m_hat = m / (1.0 - ADAM_B1 ** ADAM_STEP)
    v_hat = v / (1.0 - ADAM_B2 ** ADAM_STEP)
    delta = -ADAM_LR * (m_hat / (jnp.sqrt(v_hat) + ADAM_EPS) + ADAM_WD * w)
    return delta, m, v


def _adamw_small(w, g, m, v, name):
    def body(w_ref, g_ref, m_ref, v_ref, d_out, m_out, v_out):
        d_out[...], m_out[...], v_out[...] = _adamw_math(w_ref[...], g_ref[...], m_ref[...], v_ref[...])

    shape = jax.ShapeDtypeStruct(w.shape, F32)
    return pl.pallas_call(body, name="adamw_" + name, out_shape=[shape, shape, shape],
                          compiler_params=_params())(w, g, m, v)


def _adamw_layer(i, w, g, m, v, prev, name):
    depth, rows, cols = w.shape
    tr = _row_tile(rows, 256)
    n_prev = 0 if prev is None else 4

    def body(*refs):
        w_ref, g_ref, m_ref, v_ref = refs[:4]
        g_out, d_out, m_out, v_out = refs[4 + n_prev:]
        gv = g_ref[...]
        g_out[0] = gv
        d_out[0], m_out[0], v_out[0] = _adamw_math(w_ref[0], gv, m_ref[0], v_ref[0])

    lay = pl.BlockSpec((1, tr, cols), lambda r: (i, r, 0))
    shape = jax.ShapeDtypeStruct(w.shape, F32)
    return pl.pallas_call(
        body, name="adamw_" + name, grid=(rows // tr,),
        in_specs=[lay, pl.BlockSpec((tr, cols), lambda r: (r, 0)), lay, lay] + [ANY] * n_prev,
        out_specs=[lay] * 4, out_shape=[shape] * 4,
        input_output_aliases={4 + k: k for k in range(n_prev)},
        compiler_params=_params(("parallel",)),
    )(w, g, m, v, *(prev or ()))


def _adamw_cols_major(w, gs, m, v, name):
    depth, rows, cols = w.shape
    tr = max(t for t in range(1, 129) if cols % t == 0)
    wt, mt, vt = (jnp.transpose(a, (2, 0, 1)) for a in (w, m, v))
    gt = jnp.stack([g.T for g in gs], axis=1)

    def body(w_ref, g_ref, m_ref, v_ref, g_out, d_out, m_out, v_out):
        gv = g_ref[...]
        g_out[...] = gv
        d_out[...], m_out[...], v_out[...] = _adamw_math(w_ref[...], gv, m_ref[...], v_ref[...])

    spec = pl.BlockSpec((tr, depth, rows), lambda r: (r, 0, 0))
    shape = jax.ShapeDtypeStruct((cols, depth, rows), F32)
    outs = pl.pallas_call(body, name="adamw_" + name, grid=(cols // tr,), in_specs=[spec] * 4, out_specs=[spec] * 4,
                          out_shape=[shape] * 4, compiler_params=_params(("parallel",)))(wt, gt, mt, vt)
    return [jnp.transpose(a, (1, 2, 0)) for a in outs]


def _sum_leading(buf, name):
    n, rows, cols = buf.shape
    tr = _row_tile(rows, 512)

    def body(in_ref, out_ref):
        acc = in_ref[0]
        for k in range(1, n):
            acc = acc + in_ref[k]
        out_ref[...] = acc

    return pl.pallas_call(
        body, name=name, grid=(rows // tr,),
        in_specs=[pl.BlockSpec((n, tr, cols), lambda i: (0, i, 0))],
        out_specs=pl.BlockSpec((tr, cols), lambda i: (i, 0)),
        out_shape=jax.ShapeDtypeStruct((rows, cols), F32),
        compiler_params=_params(("parallel",)),
    )(buf)


_SHARDED_SMALL = ("meta", "conv_a_w", "ssm_conv_w", "conf_conv_w")
_LAYER_SMALL = ("pre_g", "post_g", "conv_a_w", "ssm_conv_w", "ssm_conv_b", "dt_bias", "a_log", "d_skip",
                "ssm_norm_g", "conf_conv_w", "conf_conv_b", "conf_ln_g", "conf_ln_b")
_WEIGHTS = ("meta", "pre_g", "post_g", "w_in", "w_out", "conv_a_w", "ssm_conv_w", "ssm_conv_b", "dt_bias", "a_log",
            "d_skip", "ssm_norm_g", "conf_conv_w", "conf_conv_b", "conf_ln_g", "conf_ln_b")


def _shard_last(a):
    return jnp.moveaxis(a.reshape(a.shape[:-1] + (N_CHIPS, a.shape[-1] // N_CHIPS)), -2, 0)


def _with_own_block(a, n, at):
    return lax.dynamic_update_index_in_dim(jnp.zeros((n,) + a.shape, a.dtype), a, at, 0)


def _with_own_columns(a, chip):
    k = a.shape[-1]
    return lax.dynamic_update_slice_in_dim(jnp.zeros(a.shape[:-1] + (N_CHIPS * k,), a.dtype), a, chip * k, a.ndim - 1)


def kernel(x, meta, pre_g, post_g, w_in, w_out, conv_a_w, ssm_conv_w, ssm_conv_b, dt_bias, a_log, d_skip, ssm_norm_g, conf_conv_w, conf_conv_b, conf_ln_g, conf_ln_b, loss_target, m_meta, m_pre_g, m_post_g, m_w_in, m_w_out, m_conv_a_w, m_ssm_conv_w, m_ssm_conv_b, m_dt_bias, m_a_log, m_d_skip, m_ssm_norm_g, m_conf_conv_w, m_conf_conv_b, m_conf_ln_g, m_conf_ln_b, v_meta, v_pre_g, v_post_g, v_w_in, v_w_out, v_conv_a_w, v_ssm_conv_w, v_ssm_conv_b, v_dt_bias, v_a_log, v_d_skip, v_ssm_norm_g, v_conf_conv_w, v_conf_conv_b, v_conf_ln_g, v_conf_ln_b):
    w = dict(meta=meta, pre_g=pre_g, post_g=post_g, w_in=w_in, w_out=w_out, conv_a_w=conv_a_w,
             ssm_conv_w=ssm_conv_w, ssm_conv_b=ssm_conv_b, dt_bias=dt_bias, a_log=a_log, d_skip=d_skip,
             ssm_norm_g=ssm_norm_g, conf_conv_w=conf_conv_w, conf_conv_b=conf_conv_b, conf_ln_g=conf_ln_g,
             conf_ln_b=conf_ln_b)
    mom = dict(meta=m_meta, pre_g=m_pre_g, post_g=m_post_g, w_in=m_w_in, w_out=m_w_out, conv_a_w=m_conv_a_w,
               ssm_conv_w=m_ssm_conv_w, ssm_conv_b=m_ssm_conv_b, dt_bias=m_dt_bias, a_log=m_a_log, d_skip=m_d_skip,
               ssm_norm_g=m_ssm_norm_g, conf_conv_w=m_conf_conv_w, conf_conv_b=m_conf_conv_b,
               conf_ln_g=m_conf_ln_g, conf_ln_b=m_conf_ln_b)
    vel = dict(meta=v_meta, pre_g=v_pre_g, post_g=v_post_g, w_in=v_w_in, w_out=v_w_out, conv_a_w=v_conv_a_w,
               ssm_conv_w=v_ssm_conv_w, ssm_conv_b=v_ssm_conv_b, dt_bias=v_dt_bias, a_log=v_a_log, d_skip=v_d_skip,
               ssm_norm_g=v_ssm_norm_g, conf_conv_w=v_conf_conv_w, conf_conv_b=v_conf_conv_b,
               conf_ln_g=v_conf_ln_g, conf_ln_b=v_conf_ln_b)
    bl, seq, d = x.shape
    dm = Dims(bl, seq, d)
    depth = w_in.shape[0]
    chip = (2 * lax.axis_index("x") + lax.axis_index("y")).astype(jnp.int32)
    dev = 2 * chip + lax.axis_index("c").astype(jnp.int32)
    cst = _head_consts(dm)

    full = dict(w)
    full.update(zip(_SHARDED_SMALL, _exchange("gather_small_weights", _ride_gather_small(
        [_with_own_columns(w[n], chip) for n in _SHARDED_SMALL]))))

    bases = [[_place_own(w_in, i, chip.reshape(1)), _place_own(w_out, i, chip.reshape(1))] for i in range(depth)]
    gathered = _exchange("gather_d2d_first", _ride_gather_d2d(_exchange("gather_ici_first",
                                                                         _ride_gather_ici(bases[0]))))
    h = _embed(x, full["meta"], dm)
    saved, proj_w = [], []
    for i in range(depth):
        lw = {n: full[n][i] for n in _LAYER_SMALL}
        proj_w.append((_w_in_from_shards(gathered[0], dm), gathered[1].reshape(2 * d, d)))
        h, keep, gathered = _layer_fwd(h, lw, proj_w[i][0], proj_w[i][1], cst, dm,
                                       bases[i + 1] if i + 1 < depth else None)
        saved.append(keep)

    dh, loss = _loss_head(h, loss_target, dm)
    loss = lax.psum(loss, ("x", "y", "c"))

    small_g = {n: [None] * depth for n in _LAYER_SMALL}
    big = {"w_in": None, "w_out": None}
    g_in = [None] * depth
    reduce = None
    for i in reversed(range(depth)):
        lw = {n: full[n][i] for n in _LAYER_SMALL}
        dh, mine, sg = _layer_bwd(dh, saved[i], lw, proj_w[i][0], proj_w[i][1], cst, dm, reduce, last=i == 0)
        for n in _LAYER_SMALL:
            small_g[n][i] = sg[n]
        if reduce is not None:
            g_in[i + 1] = reduce.result[0]
            big["w_out"] = _adamw_layer(i + 1, w_out, reduce.result[1], m_w_out, v_w_out, big["w_out"], "w_out")
        reduce = mine
    g_in[0] = reduce.result[0]
    big["w_out"] = _adamw_layer(0, w_out, reduce.result[1], m_w_out, v_w_out, big["w_out"], "w_out")
    grad_x, gmeta = _unembed(dh, dm)

    g = {n: jnp.stack(v) for n, v in small_g.items()}
    g["meta"] = gmeta
    small = [n for n in _WEIGHTS if n not in ("w_in", "w_out")]
    flat = jnp.concatenate([g[n].reshape(-1) for n in small])
    rows = -(-flat.shape[0] // (16 * LANES)) * 16
    flat = jnp.pad(flat, (0, rows * LANES - flat.shape[0])).reshape(rows, LANES)
    parts = _exchange("small_grads_gather_all", _ride_gather_all(_with_own_block(flat, N_DEV, dev)))[0]
    total = _sum_leading(parts, "small_grads_sum").reshape(-1)
    big["w_in"] = _adamw_cols_major(w_in, g_in, m_w_in, v_w_in, "w_in")
    grads, deltas, new_m, new_v = {}, {}, {}, {}
    off = 0
    for n in small:
        size = g[n].size
        fullg = total[off:off + size].reshape(g[n].shape)
        off += size
        if n in _SHARDED_SMALL:
            fullg = lax.dynamic_index_in_dim(_shard_last(fullg), chip, axis=0, keepdims=False)
        grads[n] = fullg
        deltas[n], new_m[n], new_v[n] = _adamw_small(w[n], fullg, mom[n], vel[n], n)
    for n in ("w_in", "w_out"):
        grads[n], deltas[n], new_m[n], new_v[n] = big[n]

    return (loss, grad_x, *[grads[n] for n in _WEIGHTS], *[deltas[n] for n in _WEIGHTS],
            *[new_m[n] for n in _WEIGHTS], *[new_v[n] for n in _WEIGHTS])
```

```python
import jax
import jax.numpy as jnp
from jax import lax
from jax.experimental import pallas as pl
from jax.experimental.pallas import tpu as pltpu

F32 = jnp.float32
BF16 = jnp.bfloat16

N_META = 16
TT = 128
SSM_STATE = 128
SSM_GROUPS = 2
SSM_HEAD_DIM = 64
CONV_A_K = 3
SSM_CONV_K = 4
CONF_K = 31
NORM_EPS = 1e-6
LN_EPS = 1e-5
LANES = 128
MXU_DIM = 256
DT_PAD = LANES
CONF_HALO = 32
SMALL_HALO = 8
VMEM_LIMIT = 56 * 1024 * 1024
N_CHIPS = 4
N_DEV = 8

ADAM_LR = 0.001
ADAM_B1 = 0.9
ADAM_B2 = 0.999
ADAM_EPS = 1e-08
ADAM_WD = 0.01
ADAM_STEP = 10

MESH = pl.DeviceIdType.MESH
ANY = pl.BlockSpec(memory_space=pl.ANY)


class Dims:
    def __init__(self, bl, seq, d):
        self.BL, self.S, self.D = bl, seq, d
        self.L = seq + N_META
        self.Lp = -(-self.L // TT) * TT
        self.NT = self.Lp // TT
        self.R = bl * self.Lp
        self.DA = d // 2
        self.DB = d
        self.DC = d // 2
        self.H = self.DB // SSM_HEAD_DIM
        self.HPG = self.H // SSM_GROUPS
        self.GN = SSM_GROUPS * SSM_STATE
        self.WA = 4 * self.DA
        self.WB = 2 * self.DB + 2 * self.GN
        self.WC = 3 * self.DC
        self.NP = self.WA + self.WB + self.WC + DT_PAD
        self.NIN = self.WA + self.WB + self.H + self.WC
        self.XBC = self.DB + 2 * self.GN
        assert self.H % 2 == 0 and self.HPG % 2 == 0 and self.H <= DT_PAD
        assert self.DA % LANES == 0 and (self.WA + self.WB) % self.DC == 0 and self.WA % self.DB == 0


def _row_tile(n, target):
    best = None
    for t in range(16, min(n, target) + 1, 16):
        if n % t == 0:
            best = t
    assert best is not None
    return best


def _col_tile(n, target):
    best = None
    for t in range(LANES, min(n, target) + 1, LANES):
        if n % t == 0:
            best = t
    assert best is not None
    return best


def _params(sem=None):
    return pltpu.CompilerParams(dimension_semantics=sem, vmem_limit_bytes=VMEM_LIMIT)


def _sigmoid(x):
    return 1.0 / (1.0 + jnp.exp(-x))


def _silu_and_grad(x):
    s = _sigmoid(x)
    return x * s, s * (1.0 + x * (1.0 - s))


def _dot(a, b):
    return jnp.dot(a, b, preferred_element_type=F32)


def _dot_nt(a, b):
    return lax.dot_general(a, b, (((1,), (1,)), ((), ())), preferred_element_type=F32)


def _dot_tn(a, b):
    return lax.dot_general(a, b, (((0,), (0,)), ((), ())), preferred_element_type=F32)


def _split3(x):
    x1 = x.astype(BF16)
    r1 = x - x1.astype(F32)
    x2 = r1.astype(BF16)
    x3 = (r1 - x2.astype(F32)).astype(BF16)
    return x1, x2, x3


class Rider:
    def __init__(self, plan, ins, out_shapes, aliases, nsem):
        self.plan, self.ins, self.out_shapes, self.aliases, self.nsem = plan, list(ins), list(out_shapes), aliases, nsem


def _place():
    x, y, c = lax.axis_index("x"), lax.axis_index("y"), lax.axis_index("c")
    chips = [(1 - x, y), (x, 1 - y), (1 - x, 1 - y)]
    return x, y, c, chips


def _remote(k, src, dst, to, send_sems, recv_sems):
    return pltpu.make_async_remote_copy(src_ref=src, dst_ref=dst, send_sem=send_sems.at[k], recv_sem=recv_sems.at[k],
                                        device_id=to, device_id_type=MESH)


def _call(body, name, grid, in_specs, out_specs, out_shape, scratch_shapes, sem, args, rider=None):
    if rider is None:
        outs = pl.pallas_call(body, name=name, grid=grid, in_specs=in_specs, out_specs=out_specs, out_shape=out_shape,
                              scratch_shapes=scratch_shapes, compiler_params=_params(sem))(*args)
        return list(outs), []
    n_in, n_out, n_scr = len(args), len(out_shape), len(scratch_shapes)
    r_in, r_out = len(rider.ins), len(rider.out_shapes)

    def hosted(*refs):
        ins, rins = refs[:n_in], refs[n_in:n_in + r_in]
        o0 = n_in + r_in
        outs, routs = refs[o0:o0 + n_out], refs[o0 + n_out:o0 + n_out + r_out]
        scr = refs[o0 + n_out + r_out:o0 + n_out + r_out + n_scr]
        send_sems, recv_sems = refs[o0 + n_out + r_out + n_scr:]
        first = pl.program_id(0) == 0
        last = pl.program_id(0) == grid[0] - 1
        for ax in range(1, len(grid)):
            first = jnp.logical_and(first, pl.program_id(ax) == 0)
            last = jnp.logical_and(last, pl.program_id(ax) == grid[ax] - 1)

        @pl.when(first)
        def _():
            starts, _ = rider.plan(rins, routs, send_sems, recv_sems)
            for cp in starts:
                cp.start()

        body(*ins, *outs, *scr)

        @pl.when(last)
        def _():
            _, waits = rider.plan(rins, routs, send_sems, recv_sems)
            for wait in waits:
                wait()

    res = pl.pallas_call(
        hosted, name=name, grid=grid,
        in_specs=list(in_specs) + [ANY] * r_in, out_specs=list(out_specs) + [ANY] * r_out,
        out_shape=list(out_shape) + rider.out_shapes,
        input_output_aliases={n_in + k: n_out + v for k, v in rider.aliases.items()},
        scratch_shapes=list(scratch_shapes) + [pltpu.SemaphoreType.DMA((rider.nsem,)),
                                               pltpu.SemaphoreType.DMA((rider.nsem,))],
        compiler_params=_params(("arbitrary",) * len(grid)),
    )(*args, *rider.ins)
    return list(res[:n_out]), list(res[n_out:])


def _exchange(name, rider):
    r_in, r_out = len(rider.ins), len(rider.out_shapes)

    def body(*refs):
        rins, routs = refs[:r_in], refs[r_in:r_in + r_out]
        send_sems, recv_sems = refs[r_in + r_out:]
        starts, waits = rider.plan(rins, routs, send_sems, recv_sems)
        for cp in starts:
            cp.start()
        for wait in waits:
            wait()

    res = pl.pallas_call(
        body, name=name, in_specs=[ANY] * r_in, out_specs=[ANY] * r_out, out_shape=rider.out_shapes,
        input_output_aliases=dict(rider.aliases),
        scratch_shapes=[pltpu.SemaphoreType.DMA((rider.nsem,)), pltpu.SemaphoreType.DMA((rider.nsem,))],
    )(*rider.ins)
    return list(res)


def _same(arrays):
    return [jax.ShapeDtypeStruct(a.shape, a.dtype) for a in arrays]


class _SemsFrom:
    def __init__(self, sems, first):
        self.sems, self.first = sems, first

    @property
    def at(self):
        return self

    def __getitem__(self, k):
        return self.sems.at[self.first + k]


def _ride_both(r1, r2):
    n_in, n_out = len(r1.ins), len(r1.out_shapes)

    def plan(ins, outs, ss, rs):
        s1, w1 = r1.plan(ins[:n_in], outs[:n_out], ss, rs)
        s2, w2 = r2.plan(ins[n_in:], outs[n_out:], _SemsFrom(ss, r1.nsem), _SemsFrom(rs, r1.nsem))
        return s1 + s2, w1 + w2

    aliases = dict(r1.aliases)
    aliases.update({n_in + k: n_out + v for k, v in r2.aliases.items()})
    return Rider(plan, r1.ins + r2.ins, r1.out_shapes + r2.out_shapes, aliases, r1.nsem + r2.nsem)


def _ride_gather_ici(bases, part=0, nparts=1):
    n = len(bases)

    def plan(ins, outs, ss, rs):
        x, y, c, chips = _place()
        me = 2 * x + y
        starts, waits = [], []
        for a in range(n):
            half = outs[a].shape[1] // 2
            mine = pl.ds(c * half + part * (half // nparts), half // nparts)
            for j, chip in enumerate(chips):
                cp = _remote(3 * a + j, outs[a].at[me, mine], outs[a].at[me, mine], (*chip, c), ss, rs)
                got = outs[a].at[2 * chip[0] + chip[1], mine]
                starts.append(cp)
                waits += [cp.wait_send, _remote(3 * a + j, got, got, (*chip, c), ss, rs).wait_recv]
        return starts, waits

    return Rider(plan, bases, _same(bases), {a: a for a in range(n)}, 3 * n)


def _gather_ici_relayed(bases):
    n = len(bases)

    def body(*refs):
        outs = refs[n:2 * n]
        ss, rs = refs[2 * n:]
        x, y, c, _ = _place()
        me, xn, yn, dg = 2 * x + y, 2 * (1 - x) + y, 2 * x + (1 - y), 2 * (1 - x) + (1 - y)
        to_x, to_y = (1 - x, y, c), (x, 1 - y, c)
        sends = []

        def send(k, piece, to):
            cp = _remote(k, piece, piece, to, ss, rs)
            cp.start()
            sends.append(cp)

        def arrived(k, piece, frm):
            _remote(k, piece, piece, frm, ss, rs).wait_recv()

        rows = []
        for a in range(n):
            half = outs[a].shape[1] // 2
            rows.append((pl.ds(c * half, half), pl.ds(c * half, half // 2), pl.ds(c * half + half // 2, half // 2)))
            send(4 * a, outs[a].at[me, rows[a][0]], to_x)
            send(4 * a + 1, outs[a].at[me, rows[a][0]], to_y)
        for a in range(n):
            mine, lo, hi = rows[a]
            arrived(4 * a, outs[a].at[xn, mine], to_x)
            send(4 * a + 2, outs[a].at[xn, lo], to_y)
            arrived(4 * a + 1, outs[a].at[yn, mine], to_y)
            send(4 * a + 3, outs[a].at[yn, hi], to_x)
        for a in range(n):
            mine, lo, hi = rows[a]
            arrived(4 * a + 2, outs[a].at[dg, lo], to_y)
            arrived(4 * a + 3, outs[a].at[dg, hi], to_x)
        for cp in sends:
            cp.wait_send()

    return pl.pallas_call(
        body, name="gather_ici_first", in_specs=[ANY] * n, out_specs=[ANY] * n, out_shape=_same(bases),
        input_output_aliases={a: a for a in range(n)},
        scratch_shapes=[pltpu.SemaphoreType.DMA((4 * n,)), pltpu.SemaphoreType.DMA((4 * n,))],
    )(*bases)


def _ride_gather_d2d(bases):
    n = len(bases)

    def plan(ins, outs, ss, rs):
        x, y, c, chips = _place()
        sib = (x, y, 1 - c)
        starts, waits = [], []
        for a in range(n):
            half = outs[a].shape[1] // 2
            for j, chip in enumerate(chips):
                frm = 2 * chip[0] + chip[1]
                got = outs[a].at[frm, pl.ds(c * half, half)]
                theirs = outs[a].at[frm, pl.ds((1 - c) * half, half)]
                cp = _remote(3 * a + j, got, got, sib, ss, rs)
                starts.append(cp)
                waits += [cp.wait_send, _remote(3 * a + j, theirs, theirs, sib, ss, rs).wait_recv]
        return starts, waits

    return Rider(plan, bases, _same(bases), {a: a for a in range(n)}, 3 * n)


def _ride_gather_small(bases):
    n = len(bases)

    def plan(ins, outs, ss, rs):
        x, y, c, chips = _place()
        me = 2 * x + y
        starts, waits = [], []
        for a in range(n):
            k = outs[a].shape[-1] // N_CHIPS
            lead = (slice(None),) * (len(outs[a].shape) - 1)
            at = (lambda s: pl.multiple_of(s * k, LANES)) if k % LANES == 0 else (lambda s: s * k)
            cols = lambda s: outs[a].at[lead + (pl.ds(at(s), k),)]
            for j, chip in enumerate(chips):
                cp = _remote(3 * a + j, cols(me), cols(me), (*chip, c), ss, rs)
                got = cols(2 * chip[0] + chip[1])
                starts.append(cp)
                waits += [cp.wait_send, _remote(3 * a + j, got, got, (*chip, c), ss, rs).wait_recv]
        return starts, waits

    return Rider(plan, bases, _same(bases), {a: a for a in range(n)}, 3 * n)


def _ride_swap_halves(gs):
    n = len(gs)

    def plan(ins, outs, ss, rs):
        x, y, c, _ = _place()
        cps = [_remote(a, ins[a].at[:, 1 - c], outs[a], (x, y, 1 - c), ss, rs) for a in range(n)]
        return cps, [cp.wait for cp in cps]

    shapes = [jax.ShapeDtypeStruct((g.shape[0],) + g.shape[2:], g.dtype) for g in gs]
    return Rider(plan, gs, shapes, {}, n)


def _ride_to_owners(ps):
    n = len(ps)

    def plan(ins, outs, ss, rs):
        x, y, c, chips = _place()
        cps = []
        for a in range(n):
            for j, chip in enumerate(chips):
                cps.append(_remote(3 * a + j, ins[a].at[2 * chip[0] + chip[1]], outs[a].at[j], (*chip, c), ss, rs))
        return cps, [cp.wait for cp in cps]

    shapes = [jax.ShapeDtypeStruct((3,) + p.shape[1:], p.dtype) for p in ps]
    return Rider(plan, ps, shapes, {}, 3 * n)


def _ride_join_halves(qs):
    n = len(qs)

    def plan(ins, outs, ss, rs):
        x, y, c, _ = _place()
        sib = (x, y, 1 - c)
        starts, waits = [], []
        for a in range(n):
            cp = _remote(a, outs[a].at[c], outs[a].at[c], sib, ss, rs)
            starts.append(cp)
            waits += [cp.wait_send, _remote(a, outs[a].at[1 - c], outs[a].at[1 - c], sib, ss, rs).wait_recv]
        return starts, waits

    return Rider(plan, qs, _same(qs), {a: a for a in range(n)}, n)


def _gather_all(base):
    def body(in_ref, out_ref, ss, rs):
        x, y, c, chips = _place()
        sib = (x, y, 1 - c)
        block = lambda cx, cy, cc: out_ref.at[4 * cx + 2 * cy + cc]
        mine = block(x, y, c)
        first = [_remote(j, mine, mine, (*chip, c), ss, rs) for j, chip in enumerate(chips)]
        first.append(_remote(3, mine, mine, sib, ss, rs))
        for cp in first:
            cp.start()
        passed = []
        for j, chip in enumerate(chips):
            got = block(*chip, c)
            _remote(j, got, got, (*chip, c), ss, rs).wait_recv()
            passed.append(_remote(4 + j, got, got, sib, ss, rs))
            passed[-1].start()
        theirs = block(x, y, 1 - c)
        _remote(3, theirs, theirs, sib, ss, rs).wait_recv()
        for j, chip in enumerate(chips):
            got = block(*chip, 1 - c)
            _remote(4 + j, got, got, sib, ss, rs).wait_recv()
        for cp in first + passed:
            cp.wait_send()

    return pl.pallas_call(
        body, name="small_grads_gather_all", in_specs=[ANY], out_specs=ANY,
        out_shape=jax.ShapeDtypeStruct(base.shape, base.dtype), input_output_aliases={0: 0},
        scratch_shapes=[pltpu.SemaphoreType.DMA((N_DEV - 1,)), pltpu.SemaphoreType.DMA((N_DEV - 1,))],
    )(base)


def _embed(x, meta, dm):
    dc = _col_tile(dm.D, 256)
    s, lp = dm.S, dm.Lp

    def body(x_ref, meta_ref, h_ref):
        h_ref[0:N_META, :] = meta_ref[...]
        h_ref[N_META:N_META + s, :] = x_ref[0]
        if lp > N_META + s:
            h_ref[N_META + s:lp, :] = jnp.zeros((lp - N_META - s, dc), F32)

    return pl.pallas_call(
        body, name="embed", grid=(dm.BL, dm.D // dc),
        in_specs=[pl.BlockSpec((1, s, dc), lambda b, j: (b, 0, j)),
                  pl.BlockSpec((N_META, dc), lambda b, j: (0, j))],
        out_specs=pl.BlockSpec((lp, dc), lambda b, j: (b, j)),
        out_shape=jax.ShapeDtypeStruct((dm.R, dm.D), F32),
        compiler_params=_params(("parallel", "parallel")),
    )(x, meta)


def _loss_head(h, target, dm):
    dc = _col_tile(dm.D, 256)
    s, lp, nj = dm.S, dm.Lp, dm.D // dc

    def body(h_ref, t_ref, dh_ref, l_ref):
        diff = h_ref[N_META:N_META + s, :] - t_ref[0]
        dh_ref[0:N_META, :] = jnp.zeros((N_META, dc), F32)
        dh_ref[N_META:N_META + s, :] = diff * (1.0 / dm.D)
        if lp > N_META + s:
            dh_ref[N_META + s:lp, :] = jnp.zeros((lp - N_META - s, dc), F32)
        l_ref[...] = jnp.full((8, LANES), (0.5 / dm.D) * jnp.sum(diff * diff), F32)

    dh, part = pl.pallas_call(
        body, name="loss_head", grid=(dm.BL, nj),
        in_specs=[pl.BlockSpec((lp, dc), lambda b, j: (b, j)),
                  pl.BlockSpec((1, s, dc), lambda b, j: (b, 0, j))],
        out_specs=[pl.BlockSpec((lp, dc), lambda b, j: (b, j)),
                   pl.BlockSpec((8, LANES), lambda b, j: (b * nj + j, 0))],
        out_shape=[jax.ShapeDtypeStruct((dm.R, dm.D), F32),
                   jax.ShapeDtypeStruct((dm.BL * nj * 8, LANES), F32)],
        compiler_params=_params(("parallel", "parallel")),
    )(h, target)
    return dh, jnp.sum(part[::8, 0])


def _unembed(dh, dm):
    dc = _col_tile(dm.D, 256)
    s, lp = dm.S, dm.Lp

    def body(dh_ref, gx_ref, gm_ref):
        gx_ref[0] = dh_ref[N_META:N_META + s, :]

        @pl.when(pl.program_id(1) == 0)
        def _():
            gm_ref[...] = dh_ref[0:N_META, :]

        @pl.when(pl.program_id(1) > 0)
        def _():
            gm_ref[...] = gm_ref[...] + dh_ref[0:N_META, :]

    return pl.pallas_call(
        body, name="unembed", grid=(dm.D // dc, dm.BL),
        in_specs=[pl.BlockSpec((lp, dc), lambda j, b: (b, j))],
        out_specs=[pl.BlockSpec((1, s, dc), lambda j, b: (b, 0, j)),
                   pl.BlockSpec((N_META, dc), lambda j, b: (0, j))],
        out_shape=[jax.ShapeDtypeStruct((dm.BL, s, dm.D), F32),
                   jax.ShapeDtypeStruct((N_META, dm.D), F32)],
        compiler_params=_params(("parallel", "arbitrary")),
    )(dh)


def _fwd_in(h, pre_g, w, dm, rider=None):
    tm = _row_tile(dm.R, 2176)
    tn = _col_tile(dm.NP, 896)
    nj = dm.NP // tn

    def body(h_ref, g_ref, w_ref, proj_ref, dt_ref, hn_ref):
        @pl.when(pl.program_id(1) == 0)
        def _():
            xf = h_ref[...]
            r = lax.rsqrt(jnp.mean(xf * xf, axis=-1, keepdims=True) + NORM_EPS)
            hn_ref[...] = (xf * r * g_ref[...]).astype(BF16)

        res = _dot(hn_ref[...], w_ref[...])
        proj_ref[...] = res.astype(BF16)

        @pl.when(pl.program_id(1) == nj - 1)
        def _():
            dt_ref[...] = res[:, tn - DT_PAD:tn]

    return _call(
        body, "fwd_in", (dm.R // tm, nj),
        [pl.BlockSpec((tm, dm.D), lambda i, j: (i, 0)),
         pl.BlockSpec((1, dm.D), lambda i, j: (0, 0)),
         pl.BlockSpec((dm.D, tn), lambda i, j: (0, j))],
        [pl.BlockSpec((tm, tn), lambda i, j: (i, j)),
         pl.BlockSpec((tm, DT_PAD), lambda i, j: (i, 0)),
         pl.BlockSpec((tm, dm.D), lambda i, j: (i, 0))],
        [jax.ShapeDtypeStruct((dm.R, dm.NP), BF16), jax.ShapeDtypeStruct((dm.R, DT_PAD), F32),
         jax.ShapeDtypeStruct((dm.R, dm.D), BF16)],
        [], ("parallel", "arbitrary"), (h, pre_g, w), rider)


def _fwd_out(ya, yb, yc, w_out, h, post_g, dm, rider=None):
    tm = _row_tile(dm.Lp, 544)
    tiles_per_seq = dm.Lp // tm
    da, db, dc = dm.DA, dm.DB, dm.DC

    def body(ya_ref, yb_ref, yc_ref, w_ref, h_ref, g_ref, hn_ref, m_ref):
        m = _dot(ya_ref[...], w_ref[0:da, :])
        m = m + _dot(yb_ref[...], w_ref[da:da + db, :])
        m = m + _dot(yc_ref[...], w_ref[da + db:da + db + dc, :])
        m_ref[...] = m
        r = lax.rsqrt(jnp.mean(m * m, axis=-1, keepdims=True) + NORM_EPS)
        t = (pl.program_id(0) % tiles_per_seq) * tm + lax.broadcasted_iota(jnp.int32, (tm, 1), 0)
        keep = (t < dm.L).astype(F32)
        hn_ref[...] = (h_ref[...] + m * r * g_ref[...]) * keep

    row = lambda i: (i, 0)
    fixed = lambda i: (0, 0)
    return _call(
        body, "fwd_out", (dm.R // tm,),
        [pl.BlockSpec((tm, da), row), pl.BlockSpec((tm, db), row), pl.BlockSpec((tm, dc), row),
         pl.BlockSpec((2 * dm.D, dm.D), fixed), pl.BlockSpec((tm, dm.D), row), pl.BlockSpec((1, dm.D), fixed)],
        [pl.BlockSpec((tm, dm.D), row), pl.BlockSpec((tm, dm.D), row)],
        [jax.ShapeDtypeStruct((dm.R, dm.D), F32), jax.ShapeDtypeStruct((dm.R, dm.D), F32)],
        [], ("parallel",), (ya, yb, yc, w_out, h, post_g), rider)


def _bwd_out(dh, m, post_g, w_out, ya, yb, yc, dm, rider=None):
    tm = _row_tile(dm.R, MXU_DIM)
    da, db, dc = dm.DA, dm.DB, dm.DC

    def body(dh_ref, m_ref, g_ref, w_ref, ya_ref, yb_ref, yc_ref, dya_ref, dyb_ref, dyc_ref, dw_ref, dg_ref):
        @pl.when(pl.program_id(0) == 0)
        def _():
            dw_ref[...] = jnp.zeros_like(dw_ref)
            dg_ref[...] = jnp.zeros_like(dg_ref)

        m = m_ref[...]
        dh_ = dh_ref[...]
        r = lax.rsqrt(jnp.mean(m * m, axis=-1, keepdims=True) + NORM_EPS)
        n = m * r
        dg_ref[0:1, :] = dg_ref[0:1, :] + jnp.sum(dh_ * n, axis=0, keepdims=True)
        dn = dh_ * g_ref[...]
        dm_ = (r * (dn - n * jnp.mean(dn * n, axis=-1, keepdims=True))).astype(BF16)
        dya_ref[...] = _dot_nt(dm_, w_ref[0:da, :])
        dyb_ref[...] = _dot_nt(dm_, w_ref[da:da + db, :])
        dyc_ref[...] = _dot_nt(dm_, w_ref[da + db:da + db + dc, :])
        dw_ref[0:da, :] = dw_ref[0:da, :] + _dot_tn(ya_ref[...], dm_)
        dw_ref[da:da + db, :] = dw_ref[da:da + db, :] + _dot_tn(yb_ref[...], dm_)
        dw_ref[da + db:da + db + dc, :] = dw_ref[da + db:da + db + dc, :] + _dot_tn(yc_ref[...], dm_)

    row = lambda i: (i, 0)
    fixed = lambda i: (0, 0)
    return _call(
        body, "bwd_out", (dm.R // tm,),
        [pl.BlockSpec((tm, dm.D), row), pl.BlockSpec((tm, dm.D), row), pl.BlockSpec((1, dm.D), fixed),
         pl.BlockSpec((2 * dm.D, dm.D), fixed),
         pl.BlockSpec((tm, da), row), pl.BlockSpec((tm, db), row), pl.BlockSpec((tm, dc), row)],
        [pl.BlockSpec((tm, da), row), pl.BlockSpec((tm, db), row), pl.BlockSpec((tm, dc), row),
         pl.BlockSpec((2 * dm.D, dm.D), fixed), pl.BlockSpec((8, dm.D), fixed)],
        [jax.ShapeDtypeStruct((dm.R, da), F32), jax.ShapeDtypeStruct((dm.R, db), F32),
         jax.ShapeDtypeStruct((dm.R, dc), F32),
         jax.ShapeDtypeStruct((2 * dm.D, dm.D), F32), jax.ShapeDtypeStruct((8, dm.D), F32)],
        [], ("arbitrary",), (dh, m, post_g, w_out, ya, yb, yc), rider)


def _bwd_in_dx(dpa, dpb, dpc, dpt, w, h, dh, pre_g, dm, rider=None):
    tm = _row_tile(dm.R, 272)
    wa, wb, wc = dm.WA, dm.WB, dm.WC

    def body(dpa_ref, dpb_ref, dpc_ref, dpt_ref, w_ref, h_ref, dh_ref, g_ref, out_ref, dg_ref):
        @pl.when(pl.program_id(0) == 0)
        def _():
            dg_ref[...] = jnp.zeros_like(dg_ref)

        dhn = _dot_nt(dpa_ref[...], w_ref[:, 0:wa])
        dhn = dhn + _dot_nt(dpb_ref[...], w_ref[:, wa:wa + wb])
        dhn = dhn + _dot_nt(dpc_ref[...], w_ref[:, wa + wb:wa + wb + wc])
        dhn = dhn + _dot_nt(dpt_ref[...], w_ref[:, wa + wb + wc:wa + wb + wc + DT_PAD])
        xf = h_ref[...]
        r = lax.rsqrt(jnp.mean(xf * xf, axis=-1, keepdims=True) + NORM_EPS)
        n = xf * r
        dg_ref[0:1, :] = dg_ref[0:1, :] + jnp.sum(dhn * n, axis=0, keepdims=True)
        dn = dhn * g_ref[...]
        out_ref[...] = dh_ref[...] + r * (dn - n * jnp.mean(dn * n, axis=-1, keepdims=True))

    row = lambda i: (i, 0)
    fixed = lambda i: (0, 0)
    return _call(
        body, "bwd_in_dx", (dm.R // tm,),
        [pl.BlockSpec((tm, wa), row), pl.BlockSpec((tm, wb), row), pl.BlockSpec((tm, wc), row),
         pl.BlockSpec((tm, DT_PAD), row), pl.BlockSpec((dm.D, dm.NP), fixed),
         pl.BlockSpec((tm, dm.D), row), pl.BlockSpec((tm, dm.D), row), pl.BlockSpec((1, dm.D), fixed)],
        [pl.BlockSpec((tm, dm.D), row), pl.BlockSpec((8, dm.D), fixed)],
        [jax.ShapeDtypeStruct((dm.R, dm.D), F32), jax.ShapeDtypeStruct((8, dm.D), F32)],
        [], ("arbitrary",), (dpa, dpb, dpc, dpt, w, h, dh, pre_g), rider)


def _bwd_in_dw(hn, dp, dm, piece):
    width = dp.shape[1]
    tn = _col_tile(width, 512)

    def body(hn_ref, dp_ref, dw_ref):
        dw_ref[...] = _dot_tn(hn_ref[...], dp_ref[...])

    return pl.pallas_call(
        body, name="bwd_in_dw_" + piece, grid=(width // tn,),
        in_specs=[pl.BlockSpec((dm.R, dm.D), lambda j: (0, 0)), pl.BlockSpec((dm.R, tn), lambda j: (0, j))],
        out_specs=pl.BlockSpec((dm.D, tn), lambda j: (0, j)),
        out_shape=jax.ShapeDtypeStruct((dm.D, width), F32),
        compiler_params=_params(("parallel",)),
    )(hn, dp)


def _tile_index(dm, reverse):
    if reverse:
        return lambda b, i: b * dm.NT + (dm.NT - 1 - i)
    return lambda b, i: b * dm.NT + i


def _halo_index(dm, rows):
    per_tile = TT // rows
    return lambda b, i: jnp.maximum((b * dm.NT + (dm.NT - 1 - i)) * per_tile - 1, 0)


HALO_BLOCK = 16


def _last_rows(x):
    return x.astype(F32)[HALO_BLOCK - SMALL_HALO:HALO_BLOCK]


MIX_A_ROWS = 288


def _mix_a_fwd(proj, conv_w, dm):
    da = dm.DA
    ta = _row_tile(dm.Lp, MIX_A_ROWS)
    nta = dm.Lp // ta
    bl = dm.BL

    def setup(ab_ref, ac_ref, ax_ref, az_ref, w_ref, y_ref, pbuf):
        i = pl.program_id(0)

        @pl.when(i == 0)
        def _():
            pbuf[0:SMALL_HALO, :] = jnp.zeros((SMALL_HALO, da), F32)

        @pl.when(i > 0)
        def _():
            pbuf[0:SMALL_HALO, :] = pbuf[ta:ta + SMALL_HALO, :]

    def body(ab_ref, ac_ref, ax_ref, az_ref, w_ref, y_ref, pbuf):
        for lb in range(da // LANES):
            cs = slice(lb * LANES, (lb + 1) * LANES)
            p = ac_ref[:, cs].astype(F32) * ax_ref[:, cs].astype(F32)
            pbuf[SMALL_HALO:SMALL_HALO + ta, cs] = p
            q = (w_ref[0:1, cs] * pbuf[6:6 + ta, cs] + w_ref[1:2, cs] * pbuf[7:7 + ta, cs] + w_ref[2:3, cs] * p)
            az = az_ref[:, cs].astype(F32)
            y_ref[:, cs] = (ab_ref[:, cs].astype(F32) * q * (az * _sigmoid(az))).astype(BF16)
            yield

    proj3 = proj.reshape(bl, dm.Lp, dm.NP)
    col = lambda k: pl.BlockSpec((bl, ta, da), lambda i: (0, i, k))
    return pl.pallas_call(
        _per_sequence(setup, body, bl, [True] * 4 + [False] + [True, True]), name="mix_a_fwd", grid=(nta,),
        in_specs=[col(0), col(1), col(2), col(3), pl.BlockSpec((CONV_A_K, da), lambda i: (0, 0))],
        out_specs=col(0),
        out_shape=jax.ShapeDtypeStruct((bl, dm.Lp, da), BF16),
        scratch_shapes=[pltpu.VMEM((bl, SMALL_HALO + ta, da), F32)],
        compiler_params=_params(("arbitrary",)),
    )(proj3, proj3, proj3, proj3, conv_w).reshape(dm.R, da)


def _mix_a_bwd(proj, dya, conv_w, dm):
    da = dm.DA
    ta = _row_tile(dm.Lp, MIX_A_ROWS)
    nta = dm.Lp // ta
    bl = dm.BL

    def setup(ab_ref, ac_ref, ax_ref, az_ref, ach_ref, axh_ref, dy_ref, w_ref, dp_ref, dw_ref, pbuf, dqbuf):
        i = pl.program_id(0)

        @pl.when(i == 0)
        def _():
            dw_ref[...] = jnp.zeros_like(dw_ref)
            dqbuf[ta:ta + SMALL_HALO, :] = jnp.zeros((SMALL_HALO, da), F32)

        @pl.when(i > 0)
        def _():
            dqbuf[ta:ta + SMALL_HALO, :] = dqbuf[0:SMALL_HALO, :]

    def body(ab_ref, ac_ref, ax_ref, az_ref, ach_ref, axh_ref, dy_ref, w_ref, dp_ref, dw_ref, pbuf, dqbuf):
        halo_on = jnp.where(pl.program_id(0) == nta - 1, 0.0, 1.0)
        for lb in range(da // LANES):
            cs = slice(lb * LANES, (lb + 1) * LANES)
            pbuf[0:SMALL_HALO, cs] = (_last_rows(ach_ref[:, cs]) * _last_rows(axh_ref[:, cs])) * halo_on
            ac, ax, ab, az = (r[:, cs].astype(F32) for r in (ac_ref, ax_ref, ab_ref, az_ref))
            p = ac * ax
            pbuf[SMALL_HALO:SMALL_HALO + ta, cs] = p
            p1 = pbuf[7:7 + ta, cs]
            p2 = pbuf[6:6 + ta, cs]
            w0, w1, w2 = w_ref[0:1, cs], w_ref[1:2, cs], w_ref[2:3, cs]
            q = w0 * p2 + w1 * p1 + w2 * p
            sz, dsz = _silu_and_grad(az)
            dy = dy_ref[:, cs]
            t1 = dy * ab
            dq = t1 * sz
            dqbuf[0:ta, cs] = dq
            dpv = w2 * dq + w1 * dqbuf[1:1 + ta, cs] + w0 * dqbuf[2:2 + ta, cs]
            dp_ref[:, lb * LANES:(lb + 1) * LANES] = (dy * q * sz).astype(BF16)
            dp_ref[:, da + lb * LANES:da + (lb + 1) * LANES] = (dpv * ax).astype(BF16)
            dp_ref[:, 2 * da + lb * LANES:2 * da + (lb + 1) * LANES] = (dpv * ac).astype(BF16)
            dp_ref[:, 3 * da + lb * LANES:3 * da + (lb + 1) * LANES] = (t1 * q * dsz).astype(BF16)
            dw_ref[0, 0:1, cs] = dw_ref[0, 0:1, cs] + jnp.sum(dq * p2, axis=0, keepdims=True)
            dw_ref[0, 1:2, cs] = dw_ref[0, 1:2, cs] + jnp.sum(dq * p1, axis=0, keepdims=True)
            dw_ref[0, 2:3, cs] = dw_ref[0, 2:3, cs] + jnp.sum(dq * p, axis=0, keepdims=True)
            yield

    proj3 = proj.reshape(bl, dm.Lp, dm.NP)
    col = lambda w, k: pl.BlockSpec((bl, ta, w), lambda i: (0, nta - 1 - i, k))
    halo = lambda k: pl.BlockSpec((bl, HALO_BLOCK, da),
                                  lambda i: (0, jnp.maximum((nta - 1 - i) * (ta // HALO_BLOCK) - 1, 0), k))
    dp, dw = pl.pallas_call(
        _per_sequence(setup, body, bl, [True] * 7 + [False] + [True, "keep"] + [True, True]),
        name="mix_a_bwd", grid=(nta,),
        in_specs=[col(da, 0), col(da, 1), col(da, 2), col(da, 3), halo(1), halo(2), col(da, 0),
                  pl.BlockSpec((CONV_A_K, da), lambda i: (0, 0))],
        out_specs=[col(dm.WA, 0), pl.BlockSpec((bl, 8, da), lambda i: (0, 0, 0))],
        out_shape=[jax.ShapeDtypeStruct((bl, dm.Lp, dm.WA), BF16), jax.ShapeDtypeStruct((bl, 8, da), F32)],
        scratch_shapes=[pltpu.VMEM((bl, SMALL_HALO + ta, da), F32), pltpu.VMEM((bl, ta + SMALL_HALO, da), F32)],
        compiler_params=_params(("arbitrary",)),
    )(proj3, proj3, proj3, proj3, proj3, proj3, dya.reshape(bl, dm.Lp, da), conv_w)
    return dp.reshape(dm.R, dm.WA), dw


SUBLANES = 8
SHIFT_ROWS = TT + CONF_HALO - SUBLANES


TAP_ROWS = 64


def _split_lanes(buf, rows, val):
    for lb in range(val.shape[1] // LANES):
        buf[lb, rows, :] = val[:, lb * LANES:(lb + 1) * LANES]


def _join_lanes(buf):
    return jnp.concatenate([buf[lb] for lb in range(buf.shape[0])], axis=1)


def _fill_shifted(buf, shifted):
    def step(lb, carry):
        for r in range(1, SUBLANES):
            shifted[lb, r - 1, 0:SHIFT_ROWS, :] = buf[lb, r:r + SHIFT_ROWS, :]
        return carry

    lax.fori_loop(0, buf.shape[0], step, 0)


def _window(buf, shifted, d, r0, lb):
    r = d % SUBLANES
    rows = pl.ds(pl.multiple_of(r0 + (d - r), SUBLANES), TAP_ROWS)
    return buf[lb, rows, :] if r == 0 else shifted[lb, r - 1, rows, :]


def _tap_loop(nlb, body):
    per_lb = TT // TAP_ROWS

    def step(it, carry):
        lb = it // per_lb
        body(lb, pl.ds(pl.multiple_of(lb * LANES, LANES), LANES), pl.multiple_of((it % per_lb) * TAP_ROWS, TAP_ROWS))
        return carry

    lax.fori_loop(0, nlb * per_lb, step, 0)


TAP_CHAINS = 4


def _tree_sum(terms):
    sums = list(terms[:TAP_CHAINS])
    for n, t in enumerate(terms[TAP_CHAINS:]):
        sums[n % TAP_CHAINS] = sums[n % TAP_CHAINS] + t
    while len(sums) > 1:
        sums = [a + b for a, b in zip(sums[0::2], sums[1::2])] + ([sums[-1]] if len(sums) % 2 else [])
    return sums[0]


def _conf_conv(ubuf, ushift, w_ref, b_ref, u1buf):
    _fill_shifted(ubuf, ushift)

    def piece(lb, cs, r0):
        taps = [w_ref[k:k + 1, cs] * _window(ubuf, ushift, CONF_HALO - (CONF_K - 1) + k, r0, lb)
                for k in range(CONF_K)]
        u1buf[lb, pl.ds(r0, TAP_ROWS), :] = _tree_sum(taps) + b_ref[0:1, cs]

    _tap_loop(ubuf.shape[0], piece)


def _mix_c_fwd(proj, conv_w, conv_b, ln_g, ln_b, dm):
    dc = dm.DC
    nlb = dc // LANES
    c0 = (dm.WA + dm.WB) // dc
    ti = _tile_index(dm, False)

    def body(ca_ref, cg_ref, cz_ref, w_ref, b_ref, g_ref, be_ref, y_ref, ubuf, u1buf, ushift):
        i = pl.program_id(1)

        @pl.when(i == 0)
        def _():
            ubuf[:, 0:CONF_HALO, :] = jnp.zeros((nlb, CONF_HALO, LANES), F32)

        @pl.when(i > 0)
        def _():
            ubuf[:, 0:CONF_HALO, :] = ubuf[:, TT:TT + CONF_HALO, :]

        _split_lanes(ubuf, slice(CONF_HALO, CONF_HALO + TT),
                     ca_ref[...].astype(F32) * _sigmoid(cg_ref[...].astype(F32)))
        _conf_conv(ubuf, ushift, w_ref, b_ref, u1buf)
        u1 = _join_lanes(u1buf)
        mu = jnp.mean(u1, axis=-1, keepdims=True)
        xc = u1 - mu
        rstd = lax.rsqrt(jnp.mean(xc * xc, axis=-1, keepdims=True) + LN_EPS)
        u2 = xc * rstd * g_ref[...] + be_ref[...]
        cz = cz_ref[...].astype(F32)
        y_ref[...] = ((u2 * _sigmoid(u2)) * (cz * _sigmoid(cz))).astype(BF16)

    col = lambda k: pl.BlockSpec((TT, dc), lambda b, i: (ti(b, i), c0 + k))
    vec = pl.BlockSpec((1, dc), lambda b, i: (0, 0))
    return pl.pallas_call(
        body, name="mix_c_fwd", grid=(dm.BL, dm.NT),
        in_specs=[col(0), col(1), col(2), pl.BlockSpec((CONF_K, dc), lambda b, i: (0, 0)), vec, vec, vec],
        out_specs=pl.BlockSpec((TT, dc), lambda b, i: (ti(b, i), 0)),
        out_shape=jax.ShapeDtypeStruct((dm.R, dc), BF16),
        scratch_shapes=[pltpu.VMEM((nlb, CONF_HALO + TT, LANES), F32), pltpu.VMEM((nlb, TT, LANES), F32),
                        pltpu.VMEM((nlb, SUBLANES - 1, SHIFT_ROWS, LANES), F32)],
        compiler_params=_params(("parallel", "arbitrary")),
    )(proj, proj, proj, conv_w, conv_b, ln_g, ln_b)


def _mix_c_bwd(proj, dyc, conv_w, conv_b, ln_g, ln_b, dm):
    dc = dm.DC
    nlb = dc // LANES
    c0 = (dm.WA + dm.WB) // dc
    ti = _tile_index(dm, True)
    hi = _halo_index(dm, CONF_HALO)

    def body(ca_ref, cg_ref, cz_ref, cah_ref, cgh_ref, dy_ref, w_ref, b_ref, g_ref, be_ref,
             dp_ref, dw_ref, dv_ref, ubuf, u1buf, dubuf, du0buf, ushift, dshift, dwacc):
        i = pl.program_id(1)
        halo_on = jnp.where(i == dm.NT - 1, 0.0, 1.0)

        @pl.when(i == 0)
        def _():
            dwacc[...] = jnp.zeros_like(dwacc)
            dv_ref[...] = jnp.zeros_like(dv_ref)
            dubuf[:, TT:TT + CONF_HALO, :] = jnp.zeros((nlb, CONF_HALO, LANES), F32)

        @pl.when(i > 0)
        def _():
            dubuf[:, TT:TT + CONF_HALO, :] = dubuf[:, 0:CONF_HALO, :]

        _split_lanes(ubuf, slice(0, CONF_HALO),
                     cah_ref[...].astype(F32) * _sigmoid(cgh_ref[...].astype(F32)) * halo_on)
        sgg = _sigmoid(cg_ref[...].astype(F32))
        ca = ca_ref[...].astype(F32)
        _split_lanes(ubuf, slice(CONF_HALO, CONF_HALO + TT), ca * sgg)
        _conf_conv(ubuf, ushift, w_ref, b_ref, u1buf)
        u1 = _join_lanes(u1buf)
        mu = jnp.mean(u1, axis=-1, keepdims=True)
        xc = u1 - mu
        rstd = lax.rsqrt(jnp.mean(xc * xc, axis=-1, keepdims=True) + LN_EPS)
        xhat = xc * rstd
        u2 = xhat * g_ref[...] + be_ref[...]
        su, dsu = _silu_and_grad(u2)
        sz, dsz = _silu_and_grad(cz_ref[...].astype(F32))
        dy = dy_ref[...]
        du2 = dy * dsu * sz
        dp_ref[:, 2 * dc:3 * dc] = (dy * su * dsz).astype(BF16)
        dxhat = du2 * g_ref[...]
        du1 = rstd * (dxhat - jnp.mean(dxhat, axis=-1, keepdims=True)
                      - xhat * jnp.mean(dxhat * xhat, axis=-1, keepdims=True))
        dv_ref[0, 0:1, :] = dv_ref[0, 0:1, :] + jnp.sum(du1, axis=0, keepdims=True)
        dv_ref[0, 1:2, :] = dv_ref[0, 1:2, :] + jnp.sum(du2 * xhat, axis=0, keepdims=True)
        dv_ref[0, 2:3, :] = dv_ref[0, 2:3, :] + jnp.sum(du2, axis=0, keepdims=True)
        _split_lanes(dubuf, slice(0, TT), du1)
        _fill_shifted(dubuf, dshift)

        def piece(lb, cs, r0):
            du0buf[lb, pl.ds(r0, TAP_ROWS), :] = _tree_sum(
                [w_ref[k:k + 1, cs] * _window(dubuf, dshift, CONF_K - 1 - k, r0, lb) for k in range(CONF_K)])
            d1 = dubuf[lb, pl.ds(r0, TAP_ROWS), :]
            for k in range(CONF_K):
                prod = d1 * _window(ubuf, ushift, CONF_HALO - (CONF_K - 1) + k, r0, lb)
                dwacc[lb, k] = dwacc[lb, k] + jnp.sum(prod.reshape(TAP_ROWS // SUBLANES, SUBLANES, LANES), axis=0)

        _tap_loop(nlb, piece)
        du0 = _join_lanes(du0buf)
        dp_ref[:, 0:dc] = (du0 * sgg).astype(BF16)
        dp_ref[:, dc:2 * dc] = (du0 * ca * sgg * (1.0 - sgg)).astype(BF16)

        @pl.when(i == dm.NT - 1)
        def _():
            for lb in range(nlb):
                dw_ref[0, 0:CONF_K, lb * LANES:(lb + 1) * LANES] = jnp.sum(dwacc[lb], axis=1)
            dw_ref[0, CONF_K:CONF_K + 1, :] = jnp.zeros((1, dc), F32)

    col = lambda k: pl.BlockSpec((TT, dc), lambda b, i: (ti(b, i), c0 + k))
    halo = lambda k: pl.BlockSpec((CONF_HALO, dc), lambda b, i: (hi(b, i), c0 + k))
    vec = pl.BlockSpec((1, dc), lambda b, i: (0, 0))
    return pl.pallas_call(
        body, name="mix_c_bwd", grid=(dm.BL, dm.NT),
        in_specs=[col(0), col(1), col(2), halo(0), halo(1),
                  pl.BlockSpec((TT, dc), lambda b, i: (ti(b, i), 0)),
                  pl.BlockSpec((CONF_K, dc), lambda b, i: (0, 0)), vec, vec, vec],
        out_specs=[pl.BlockSpec((TT, dm.WC), lambda b, i: (ti(b, i), 0)),
                   pl.BlockSpec((1, 32, dc), lambda b, i: (b, 0, 0)),
                   pl.BlockSpec((1, 8, dc), lambda b, i: (b, 0, 0))],
        out_shape=[jax.ShapeDtypeStruct((dm.R, dm.WC), BF16),
                   jax.ShapeDtypeStruct((dm.BL, 32, dc), F32),
                   jax.ShapeDtypeStruct((dm.BL, 8, dc), F32)],
        scratch_shapes=[pltpu.VMEM((nlb, CONF_HALO + TT, LANES), F32), pltpu.VMEM((nlb, TT, LANES), F32),
                        pltpu.VMEM((nlb, TT + CONF_HALO, LANES), F32), pltpu.VMEM((nlb, TT, LANES), F32),
                        pltpu.VMEM((nlb, SUBLANES - 1, SHIFT_ROWS, LANES), F32),
                        pltpu.VMEM((nlb, SUBLANES - 1, SHIFT_ROWS, LANES), F32),
                        pltpu.VMEM((nlb, CONF_K, SUBLANES, LANES), F32)],
        compiler_params=_params(("parallel", "arbitrary")),
    )(proj, proj, proj, proj, proj, dyc, conv_w, conv_b, ln_g, ln_b)


RAW_HALO = 16


def _shift_matrix():
    r = jnp.arange((SSM_CONV_K - 1) * TT)[:, None]
    want = TT + r % TT - (SSM_CONV_K - 1 - r // TT)
    return (jnp.arange(2 * TT)[None, :] == want).astype(BF16)


def _ssm_conv(rawwin, sh, s_ref, w_ref, b_ref, width):
    sh[...] = _dot(s_ref[...], rawwin[...])
    for lb in range(width // LANES):
        cs = slice(lb * LANES, (lb + 1) * LANES)
        acc = b_ref[0:1, cs] + w_ref[SSM_CONV_K - 1:SSM_CONV_K, cs] * rawwin[TT:2 * TT, cs].astype(F32)
        for k in range(SSM_CONV_K - 1):
            acc = acc + w_ref[k:k + 1, cs] * sh[k * TT:(k + 1) * TT, cs]
        yield cs, acc


def _softplus(z):
    return jnp.maximum(z, 0.0) + jnp.log(1.0 + jnp.exp(-jnp.abs(z)))


def _tri(lower):
    r = lax.broadcasted_iota(jnp.int32, (TT, TT), 0)
    c = lax.broadcasted_iota(jnp.int32, (TT, TT), 1)
    return (c <= r) if lower else (c >= r)


def _exact_01_dot(mat01, x):
    x1, x2, x3 = _split3(x)
    return _dot(mat01, x1) + _dot(mat01, x2) + _dot(mat01, x3)


def _head_scalars(dt_ref, dtb_ref, alog_ref):
    z = dt_ref[...] + dtb_ref[...]
    dtv = _softplus(z)
    a = -jnp.exp(alog_ref[...])
    ac = _exact_01_dot(_tri(True).astype(F32).astype(BF16), dtv * a)
    eac = jnp.exp(ac)
    dst = jnp.exp(ac[TT - 1:TT, :] - ac)
    return z, dtv, a, ac, eac, dst


FAR_BELOW = -1e30


def _decay(ac, ac_t, h, causal):
    return jnp.exp(jnp.where(causal, ac[:, h:h + 1] - ac_t[h:h + 1, :], FAR_BELOW))


def _own_half(x16, h):
    lane = lax.broadcasted_iota(jnp.int32, (1, LANES), 1)
    keep = (lane >= SSM_HEAD_DIM) if (h % 2) else (lane < SSM_HEAD_DIM)
    return jnp.where(keep, x16, jnp.zeros_like(x16))


def _per_sequence(setup, body, bl, how):
    def all_sequences(*refs):
        views = [[r.at[b] if h is True else (r.at[pl.ds(b, 1)] if h == "keep" else r) for r, h in zip(refs, how)]
                 for b in range(bl)]
        for v in views:
            setup(*v)
        running = [body(*v) for v in views]
        while running:
            running = [g for g in running if next(g, "done") != "done"]

    return all_sequences


def _mix_b_fwd(proj, projdt, conv_w, conv_b, dt_bias, a_log, dskx, norm_g, expand, dm, rider=None):
    db, gn, xbc_w, hpg = dm.DB, dm.GN, dm.XBC, dm.HPG
    gw = db // SSM_GROUPS

    def setup(bz_ref, bx_ref, bc_ref, dt_ref, w_ref, b_ref, dtb_ref, alog_ref, dsk_ref, g_ref, e_ref, s_ref,
              y_ref, yraw_ref, sprev_ref, rawwin, sh, xbuf, state, ybuf, exbuf, xdtbuf):
        i = pl.program_id(0)

        @pl.when(i == 0)
        def _():
            rawwin[0:TT, :] = jnp.zeros((TT, xbc_w), BF16)
            state[...] = jnp.zeros_like(state)

        @pl.when(i > 0)
        def _():
            rawwin[TT - RAW_HALO:TT, :] = rawwin[2 * TT - RAW_HALO:2 * TT, :]

    def body(bz_ref, bx_ref, bc_ref, dt_ref, w_ref, b_ref, dtb_ref, alog_ref, dsk_ref, g_ref, e_ref, s_ref,
             y_ref, yraw_ref, sprev_ref, rawwin, sh, xbuf, state, ybuf, exbuf, xdtbuf):
        rawwin[TT:2 * TT, 0:db] = bx_ref[...]
        rawwin[TT:2 * TT, db:xbc_w] = bc_ref[...]
        for cs, pre in _ssm_conv(rawwin, sh, s_ref, w_ref, b_ref, xbc_w):
            xbuf[:, cs] = pre * _sigmoid(pre)
            yield

        _, dtv, _, ac, eac, dst = _head_scalars(dt_ref, dtb_ref, alog_ref)
        exbuf[...] = _dot(jnp.concatenate([dtv, eac, dst], axis=0).astype(BF16), e_ref[...])
        ac_t = ac.T
        causal = _tri(True)
        sprev_ref[0, 0] = state[...]
        yield

        xdtbuf[...] = xbuf[:, 0:db] * exbuf[0:TT, :]
        ybuf[...] = xbuf[:, 0:db] * dsk_ref[...]
        for g in range(SSM_GROUPS):
            gs = slice(g * gw, (g + 1) * gw)
            bg = xbuf[:, db + g * SSM_STATE:db + (g + 1) * SSM_STATE].astype(BF16)
            cg = xbuf[:, db + gn + g * SSM_STATE:db + gn + (g + 1) * SSM_STATE].astype(BF16)
            cb = _dot_nt(cg, bg)
            for e in range(0, hpg, 2):
                h = g * hpg + e
                ps = slice(h * SSM_HEAD_DIM, (h + 2) * SSM_HEAD_DIM)
                xp16 = xdtbuf[:, ps].astype(BF16)
                acc = jnp.zeros((TT, LANES), F32)
                for hh in (h, h + 1):
                    mm = (cb * _decay(ac, ac_t, hh, causal)).astype(BF16)
                    acc = acc + _dot(mm, _own_half(xp16, hh))
                ybuf[:, ps] = ybuf[:, ps] + acc
                yield
            sg = state[:, gs]
            ybuf[:, gs] = ybuf[:, gs] + exbuf[TT:2 * TT, gs] * _dot(cg, sg.astype(BF16))
            state[:, gs] = sg * exbuf[2 * TT - 1:2 * TT, gs] + _dot_tn(
                bg, (xdtbuf[:, gs] * exbuf[2 * TT:3 * TT, gs]).astype(BF16))
            yield

        yraw = ybuf[...]
        yraw_ref[...] = yraw
        bz = bz_ref[...].astype(F32)
        v = yraw * (bz * _sigmoid(bz))
        r = lax.rsqrt(jnp.mean(v * v, axis=-1, keepdims=True) + NORM_EPS)
        y_ref[...] = (v * r * g_ref[...]).astype(BF16)

    bl = dm.BL
    tile = lambda w, k: pl.BlockSpec((bl, TT, w), lambda i: (0, i, k))
    fixed = lambda r, w: pl.BlockSpec((r, w), lambda i: (0, 0))
    proj3, dt3 = proj.reshape(bl, dm.Lp, dm.NP), projdt.reshape(bl, dm.Lp, DT_PAD)
    scratch = [((2 * TT, xbc_w), BF16), (((SSM_CONV_K - 1) * TT, xbc_w), F32), ((TT, xbc_w), F32),
               ((SSM_STATE, db), F32), ((TT, db), F32), ((3 * TT, db), F32), ((TT, db), F32)]
    (y, yraw, sprev), rode = _call(
        _per_sequence(setup, body, bl, [True] * 4 + [False] * 8 + [True, True, "keep"] + [True] * len(scratch)),
        "mix_b_fwd", (dm.NT,),
        [tile(db, dm.WA // db), tile(db, dm.WA // db + 1), tile(2 * gn, (dm.WA + 2 * db) // (2 * gn)),
         tile(DT_PAD, 0),
         fixed(SSM_CONV_K, xbc_w), fixed(1, xbc_w), fixed(1, DT_PAD), fixed(1, DT_PAD),
         fixed(1, db), fixed(1, db), fixed(DT_PAD, db), fixed((SSM_CONV_K - 1) * TT, 2 * TT)],
        [tile(db, 0), tile(db, 0), pl.BlockSpec((bl, 1, SSM_STATE, db), lambda i: (0, i, 0, 0))],
        [jax.ShapeDtypeStruct((bl, dm.Lp, db), BF16), jax.ShapeDtypeStruct((bl, dm.Lp, db), F32),
         jax.ShapeDtypeStruct((bl, dm.NT, SSM_STATE, db), F32)],
        [pltpu.VMEM((bl,) + s, t) for s, t in scratch],
        ("arbitrary",),
        (proj3, proj3, proj3, dt3, conv_w, conv_b, dt_bias, a_log, dskx, norm_g, expand, _shift_matrix()), rider)
    return (y.reshape(dm.R, db), yraw.reshape(dm.R, db), sprev), rode


def _mix_b_bwd(proj, projdt, dyb, yraw, sprev, conv_w, conv_b, dt_bias, a_log, dskx, norm_g, expand, expand_t, dm,
               rider=None):
    db, gn, xbc_w, hpg = dm.DB, dm.GN, dm.XBC, dm.HPG
    gw = db // SSM_GROUPS

    def setup(bz_ref, bx_ref, bc_ref, dt_ref, bxh_ref, bch_ref, dy_ref, yraw_ref, sprev_ref,
              w_ref, b_ref, dtb_ref, alog_ref, dsk_ref, g_ref, e_ref, et_ref, s_ref,
              dp_ref, dpt_ref, dwc_ref, dch_ref, dhd_ref,
              rawwin, sh, xbuf, dsbuf, dstate, dxbuf, z1buf, dprebuf, exbuf, xdtbuf, dyrbuf, uvec):
        i = pl.program_id(0)

        @pl.when(i == 0)
        def _():
            dwc_ref[...] = jnp.zeros_like(dwc_ref)
            dch_ref[...] = jnp.zeros_like(dch_ref)
            dhd_ref[...] = jnp.zeros_like(dhd_ref)
            dstate[...] = jnp.zeros_like(dstate)
            dprebuf[TT:TT + SMALL_HALO, :] = jnp.zeros((SMALL_HALO, xbc_w), F32)
            rawwin[0:TT - RAW_HALO, :] = jnp.zeros((TT - RAW_HALO, xbc_w), BF16)

        @pl.when(i > 0)
        def _():
            dprebuf[TT:TT + SMALL_HALO, :] = dprebuf[0:SMALL_HALO, :]

    def body(bz_ref, bx_ref, bc_ref, dt_ref, bxh_ref, bch_ref, dy_ref, yraw_ref, sprev_ref,
             w_ref, b_ref, dtb_ref, alog_ref, dsk_ref, g_ref, e_ref, et_ref, s_ref,
             dp_ref, dpt_ref, dwc_ref, dch_ref, dhd_ref,
             rawwin, sh, xbuf, dsbuf, dstate, dxbuf, z1buf, dprebuf, exbuf, xdtbuf, dyrbuf, uvec):
        halo_on = jnp.where(pl.program_id(0) == dm.NT - 1, 0.0, 1.0).astype(BF16)

        rawwin[TT - RAW_HALO:TT, 0:db] = bxh_ref[...] * halo_on
        rawwin[TT - RAW_HALO:TT, db:xbc_w] = bch_ref[...] * halo_on
        rawwin[TT:2 * TT, 0:db] = bx_ref[...]
        rawwin[TT:2 * TT, db:xbc_w] = bc_ref[...]
        for cs, pre in _ssm_conv(rawwin, sh, s_ref, w_ref, b_ref, xbc_w):
            sl, dsl = _silu_and_grad(pre)
            xbuf[:, cs] = sl
            dsbuf[:, cs] = dsl
            yield

        z, dtv, a, ac, eac, dst = _head_scalars(dt_ref, dtb_ref, alog_ref)
        exbuf[...] = _dot(jnp.concatenate([dtv, eac, dst], axis=0).astype(BF16), e_ref[...])
        ac_t = ac.T
        causal = _tri(True)
        xdtbuf[...] = xbuf[:, 0:db] * exbuf[0:TT, :]

        yraw = yraw_ref[...]
        sz, dsz = _silu_and_grad(bz_ref[...].astype(F32))
        v = yraw * sz
        r = lax.rsqrt(jnp.mean(v * v, axis=-1, keepdims=True) + NORM_EPS)
        dy = dy_ref[...]
        dyg = dy * g_ref[...]
        dv = r * dyg - v * (r * r * r * jnp.mean(dyg * v, axis=-1, keepdims=True))
        dch_ref[0, 0:1, :] = dch_ref[0, 0:1, :] + jnp.sum(dy * v * r, axis=0, keepdims=True)
        dyr = dv * sz
        dyrbuf[...] = dyr
        dp_ref[:, 0:db] = (dv * yraw * dsz).astype(BF16)
        dch_ref[0, 1:2, :] = dch_ref[0, 1:2, :] + jnp.sum(dyr * xbuf[:, 0:db], axis=0, keepdims=True)

        lane_row = lax.broadcasted_iota(jnp.int32, (1, LANES), 1)
        sub_col = lax.broadcasted_iota(jnp.int32, (LANES, 1), 0)
        dac = jnp.zeros((TT, LANES), F32)
        colacc = jnp.zeros((LANES, TT), F32)
        for g in range(SSM_GROUPS):
            gs = slice(g * gw, (g + 1) * gw)
            bs_ = slice(db + g * SSM_STATE, db + (g + 1) * SSM_STATE)
            cs_ = slice(db + gn + g * SSM_STATE, db + gn + (g + 1) * SSM_STATE)
            bg = xbuf[:, bs_].astype(BF16)
            cg = xbuf[:, cs_].astype(BF16)
            cb = _dot_nt(cg, bg)
            dcb = jnp.zeros((TT, TT), F32)
            for e in range(0, hpg, 2):
                h = g * hpg + e
                ps = slice(h * SSM_HEAD_DIM, (h + 2) * SSM_HEAD_DIM)
                xp16 = xdtbuf[:, ps].astype(BF16)
                dyp16 = dyrbuf[:, ps].astype(BF16)
                acc = jnp.zeros((TT, LANES), F32)
                for hh in (h, h + 1):
                    dec = _decay(ac, ac_t, hh, causal)
                    mm = cb * dec
                    dyh = _own_half(dyp16, hh)
                    dmm = _dot_nt(dyh, xp16)
                    acc = acc + _dot_tn(mm.astype(BF16), dyh)
                    dcb = dcb + dmm * dec
                    gm = dmm * mm
                    dac = dac + jnp.sum(gm, axis=1, keepdims=True) * (lane_row == hh).astype(F32)
                    colacc = colacc + (sub_col == hh).astype(F32) * jnp.sum(gm, axis=0, keepdims=True)
                dxbuf[:, ps] = acc
                yield
            sg32 = sprev_ref[0, 0, :, gs]
            sg = sg32.astype(BF16)
            dsn = dstate[:, gs]
            dsn16 = dsn.astype(BF16)
            dcb16 = dcb.astype(BF16)
            eacx = exbuf[TT:2 * TT, gs]
            dstx = exbuf[2 * TT:3 * TT, gs]
            cdx = exbuf[2 * TT - 1:2 * TT, gs]
            dye16 = (dyrbuf[:, gs] * eacx).astype(BF16)
            xdt_g = xdtbuf[:, gs]
            dxbuf[:, cs_] = _dot(dcb16, bg) + _dot_nt(dye16, sg)
            dst_x = dstx * _dot(bg, dsn16)
            dxbuf[:, bs_] = _dot_tn(dcb16, cg) + _dot_nt((dstx * xdt_g).astype(BF16), dsn16)
            dstate[:, gs] = cdx * dsn + _dot_tn(cg, dye16)
            z1buf[:, gs] = dyrbuf[:, gs] * (eacx * _dot(cg, sg)) - xdt_g * dst_x
            uvec[:, gs] = jnp.broadcast_to(
                jnp.sum(xdt_g * dst_x, axis=0, keepdims=True) + jnp.sum(dsn * cdx * sg32, axis=0, keepdims=True),
                (8, gw))
            dxbuf[:, gs] = dxbuf[:, gs] + dst_x
            yield

        zz = _dot(jnp.concatenate([z1buf[...], dxbuf[:, 0:db] * xbuf[:, 0:db]], axis=0).astype(BF16), et_ref[...])
        u1, u2, u3 = _split3(uvec[...])
        ulast = (_dot(u1, et_ref[...]) + _dot(u2, et_ref[...]) + _dot(u3, et_ref[...]))[0:1, :]
        is_last = (lax.broadcasted_iota(jnp.int32, (TT, 1), 0) == TT - 1).astype(F32)
        dac = dac - colacc.T + zz[0:TT] + is_last * ulast
        dda = _exact_01_dot(_tri(False).astype(F32).astype(BF16), dac)
        ddt = dda * a + zz[TT:2 * TT]
        dhd_ref[0, 1:2, :] = dhd_ref[0, 1:2, :] + jnp.sum(dda * dtv, axis=0, keepdims=True) * a
        ddtraw = ddt * _sigmoid(z)
        dhd_ref[0, 0:1, :] = dhd_ref[0, 0:1, :] + jnp.sum(ddtraw, axis=0, keepdims=True)
        dpt_ref[...] = ddtraw.astype(BF16)
        dxbuf[:, 0:db] = dyrbuf[...] * dsk_ref[...] + dxbuf[:, 0:db] * exbuf[0:TT, :]

        for lb in range(xbc_w // LANES):
            cs = slice(lb * LANES, (lb + 1) * LANES)
            dpre = dxbuf[:, cs] * dsbuf[:, cs]
            dprebuf[0:TT, cs] = dpre
            dwc_ref[0, SSM_CONV_K:SSM_CONV_K + 1, cs] = dwc_ref[0, SSM_CONV_K:SSM_CONV_K + 1, cs] + jnp.sum(
                dpre, axis=0, keepdims=True)
            draw = w_ref[SSM_CONV_K - 1:SSM_CONV_K, cs] * dpre
            for k in range(SSM_CONV_K - 1):
                ahead = SSM_CONV_K - 1 - k
                draw = draw + w_ref[k:k + 1, cs] * dprebuf[ahead:ahead + TT, cs]
            for k in range(SSM_CONV_K):
                moved = sh[k * TT:(k + 1) * TT, cs] if k < SSM_CONV_K - 1 else rawwin[TT:2 * TT, cs].astype(F32)
                dwc_ref[0, k:k + 1, cs] = dwc_ref[0, k:k + 1, cs] + jnp.sum(dpre * moved, axis=0, keepdims=True)
            dp_ref[:, db + lb * LANES:db + (lb + 1) * LANES] = draw.astype(BF16)
            yield

    bl, nt = dm.BL, dm.NT
    tile = lambda w, k: pl.BlockSpec((bl, TT, w), lambda i: (0, nt - 1 - i, k))
    halo = lambda w, k: pl.BlockSpec((bl, HALO_BLOCK, w),
                                     lambda i: (0, jnp.maximum((nt - 1 - i) * (TT // HALO_BLOCK) - 1, 0), k))
    fixed = lambda r, w: pl.BlockSpec((r, w), lambda i: (0, 0))
    sums = lambda w: pl.BlockSpec((bl, 8, w), lambda i: (0, 0, 0))
    kz = dm.WA // db
    kc = (dm.WA + 2 * db) // (2 * gn)
    proj3, dt3 = proj.reshape(bl, dm.Lp, dm.NP), projdt.reshape(bl, dm.Lp, DT_PAD)
    scratch = [((2 * TT, xbc_w), BF16), (((SSM_CONV_K - 1) * TT, xbc_w), F32), ((TT, xbc_w), F32),
               ((TT, xbc_w), F32), ((SSM_STATE, db), F32), ((TT, xbc_w), F32), ((TT, db), F32),
               ((TT + SMALL_HALO, xbc_w), F32), ((3 * TT, db), F32), ((TT, db), F32), ((TT, db), F32), ((8, db), F32)]
    how = [True] * 8 + ["keep"] + [False] * 9 + [True, True, "keep", "keep", "keep"] + [True] * len(scratch)
    (dp, dpt, dwc, dch, dhd), rode = _call(
        _per_sequence(setup, body, bl, how), "mix_b_bwd", (nt,),
        [tile(db, kz), tile(db, kz + 1), tile(2 * gn, kc), tile(DT_PAD, 0),
         halo(db, kz + 1), halo(2 * gn, kc), tile(db, 0), tile(db, 0),
         pl.BlockSpec((bl, 1, SSM_STATE, db), lambda i: (0, nt - 1 - i, 0, 0)),
         fixed(SSM_CONV_K, xbc_w), fixed(1, xbc_w), fixed(1, DT_PAD), fixed(1, DT_PAD),
         fixed(1, db), fixed(1, db), fixed(DT_PAD, db), fixed(db, DT_PAD), fixed((SSM_CONV_K - 1) * TT, 2 * TT)],
        [tile(dm.WB, 0), tile(DT_PAD, 0), sums(xbc_w), sums(db), sums(DT_PAD)],
        [jax.ShapeDtypeStruct((bl, dm.Lp, dm.WB), BF16), jax.ShapeDtypeStruct((bl, dm.Lp, DT_PAD), BF16),
         jax.ShapeDtypeStruct((bl, 8, xbc_w), F32), jax.ShapeDtypeStruct((bl, 8, db), F32),
         jax.ShapeDtypeStruct((bl, 8, DT_PAD), F32)],
        [pltpu.VMEM((bl,) + s, t) for s, t in scratch],
        ("arbitrary",),
        (proj3, proj3, proj3, dt3, proj3, proj3, dyb.reshape(bl, dm.Lp, db), yraw.reshape(bl, dm.Lp, db), sprev,
         conv_w, conv_b, dt_bias, a_log, dskx, norm_g, expand, expand_t, _shift_matrix()), rider)
    return (dp.reshape(dm.R, dm.WB), dpt.reshape(dm.R, DT_PAD), dwc, dch, dhd), rode


def _head_consts(dm):
    head_of = jnp.arange(dm.DB) // SSM_HEAD_DIM
    expand = (jnp.arange(DT_PAD)[:, None] == head_of[None, :]).astype(BF16)
    return expand, expand.T


def _ssm_params(lw, dm):
    pad_h = lambda v: jnp.pad(v, (0, DT_PAD - dm.H))[None]
    return (lw["ssm_conv_w"], lw["ssm_conv_b"][None], pad_h(lw["dt_bias"]), pad_h(lw["a_log"]),
            jnp.repeat(lw["d_skip"], SSM_HEAD_DIM)[None], lw["ssm_norm_g"][None])


def _layer_fwd(h, lw, w_in, w_out, cst, dm, next_bases=None):
    nxt = next_bases is not None
    (proj, projdt, hn), got = _fwd_in(h, lw["pre_g"][None], w_in, dm,
                                      _ride_gather_ici(next_bases, 0, 2) if nxt else None)
    ya = _mix_a_fwd(proj, lw["conv_a_w"], dm)
    (yb, yraw, sprev), got = _mix_b_fwd(proj, projdt, *_ssm_params(lw, dm), cst[0], dm,
                                        _ride_gather_ici(got, 1, 2) if nxt else None)
    yc = _mix_c_fwd(proj, lw["conf_conv_w"], lw["conf_conv_b"][None], lw["conf_ln_g"][None],
                    lw["conf_ln_b"][None], dm)
    (h_new, m), got = _fwd_out(ya, yb, yc, w_out, h, lw["post_g"][None], dm, _ride_gather_d2d(got) if nxt else None)
    return h_new, (h, hn, proj, projdt, ya, yb, yc, yraw, sprev, m), got


def _layer_bwd(dh, saved, lw, w_in, w_out, cst, dm, reduce=None, last=False):
    h_in, hn, proj, projdt, ya, yb, yc, yraw, sprev, m = saved
    (dya, dyb, dyc, dwo, dpost), got = _bwd_out(dh, m, lw["post_g"][None], w_out, ya, yb, yc, dm,
                                                None if reduce is None else reduce.swap())
    dpa, dwa = _mix_a_bwd(proj, dya, lw["conv_a_w"], dm)
    (dpb, dpt, dwcv, dch, dhd), got = _mix_b_bwd(proj, projdt, dyb, yraw, sprev, *_ssm_params(lw, dm), cst[0],
                                                 cst[1], dm, None if reduce is None else reduce.to_owners(got))
    dpc, dwcf, dvc = _mix_c_bwd(proj, dyc, lw["conf_conv_w"], lw["conf_conv_b"][None], lw["conf_ln_g"][None],
                                lw["conf_ln_b"][None], dm)
    def own_reduce():
        pieces = [_bwd_in_dw(hn, dp, dm, n) for dp, n in ((dpa, "a"), (dpb, "b"), (dpc, "c"), (dpt, "dt"))]
        return _GradReduce([_grad_to_shards(pieces, dm), dwo.reshape(N_CHIPS, 2 * dm.D // N_CHIPS, dm.D)])

    rider = None if reduce is None else reduce.join(got)
    n_join = 0 if rider is None else len(rider.out_shapes)
    if last:
        mine = own_reduce()
        to_owners = mine.to_owners(_exchange("grad_swap_halves", mine.swap()))
        rider = to_owners if rider is None else _ride_both(rider, to_owners)
    (dh, dpre), got = _bwd_in_dx(dpa, dpb, dpc, dpt, w_in, h_in, dh, lw["pre_g"][None], dm, rider)
    if reduce is not None:
        reduce.finish(got[:n_join])
    if last:
        mine.finish(_exchange("grad_join_halves", mine.join(got[n_join:])))
    else:
        mine = own_reduce()
    dwcv, dch, dhd, dvc = (jnp.sum(a, axis=0) for a in (dwcv, dch, dhd, dvc))
    small = dict(pre_g=dpre[0], post_g=dpost[0], conv_a_w=jnp.sum(dwa, axis=0)[:CONV_A_K],
                 ssm_conv_w=dwcv[:SSM_CONV_K], ssm_conv_b=dwcv[SSM_CONV_K], ssm_norm_g=dch[0],
                 d_skip=jnp.sum(dch[1].reshape(dm.H, SSM_HEAD_DIM), axis=1), dt_bias=dhd[0, :dm.H],
                 a_log=dhd[1, :dm.H], conf_conv_w=jnp.sum(dwcf, axis=0)[:CONF_K], conf_conv_b=dvc[0],
                 conf_ln_g=dvc[1], conf_ln_b=dvc[2])
    return dh, mine, small


def _shard_runs(dm):
    ab = dm.WA + dm.WB
    order = [(0, 0, ab), (ab, dm.NP - DT_PAD, dm.H), (ab + dm.H, ab, dm.WC)]
    k = dm.NIN // N_CHIPS
    runs = []
    for s in range(N_CHIPS):
        for o0, m0, wd in order:
            lo, hi = max(o0, s * k), min(o0 + wd, (s + 1) * k)
            if lo < hi:
                runs.append((s, lo - s * k, m0 + lo - o0, hi - lo))
    return runs


def _w_in_from_shards(base, dm):
    tr = _row_tile(dm.D, 256)
    k = dm.NIN // N_CHIPS
    runs = _shard_runs(dm)

    def body(in_ref, out_ref):
        for s, sc, mc, wd in runs:
            out_ref[:, mc:mc + wd] = in_ref[s, :, sc:sc + wd]
        out_ref[:, dm.NP - DT_PAD + dm.H:dm.NP] = jnp.zeros((tr, DT_PAD - dm.H), BF16)

    return pl.pallas_call(
        body, name="w_in_from_shards", grid=(dm.D // tr,),
        in_specs=[pl.BlockSpec((N_CHIPS, tr, k), lambda r: (0, r, 0))],
        out_specs=pl.BlockSpec((tr, dm.NP), lambda r: (r, 0)),
        out_shape=jax.ShapeDtypeStruct((dm.D, dm.NP), BF16),
        compiler_params=_params(("parallel",)),
    )(base)


def _grad_to_shards(pieces, dm):
    tr = _row_tile(dm.D, 256)
    k = dm.NIN // N_CHIPS
    starts = [0, dm.WA, dm.WA + dm.WB, dm.NP - DT_PAD]
    widths = [dm.WA, dm.WB, dm.WC, DT_PAD]
    runs = _shard_runs(dm)

    def body(a_ref, b_ref, c_ref, t_ref, out_ref):
        refs = (a_ref, b_ref, c_ref, t_ref)
        for s, sc, mc, wd in runs:
            for p in range(4):
                lo, hi = max(mc, starts[p]), min(mc + wd, starts[p] + widths[p])
                if lo < hi:
                    out_ref[s, :, sc + lo - mc:sc + hi - mc] = refs[p][:, lo - starts[p]:hi - starts[p]].astype(BF16)

    return pl.pallas_call(
        body, name="grad_to_shards", grid=(dm.D // tr,),
        in_specs=[pl.BlockSpec((tr, w), lambda r: (r, 0)) for w in widths],
        out_specs=pl.BlockSpec((N_CHIPS, tr, k), lambda r: (0, r, 0)),
        out_shape=jax.ShapeDtypeStruct((N_CHIPS, dm.D, k), BF16),
        compiler_params=_params(("parallel",)),
    )(*pieces)


def _place_own(w, layer, me):
    _, rows, cols = w.shape
    tr = _row_tile(rows, 256)

    def body(me_ref, w_ref, out_ref):
        out_ref[0] = w_ref[0].astype(BF16)

    return pl.pallas_call(
        body, name="place_own",
        grid_spec=pltpu.PrefetchScalarGridSpec(
            num_scalar_prefetch=1, grid=(rows // tr,),
            in_specs=[pl.BlockSpec((1, tr, cols), lambda r, me_ref: (layer, r, 0))],
            out_specs=pl.BlockSpec((1, tr, cols), lambda r, me_ref: (me_ref[0], r, 0))),
        out_shape=jax.ShapeDtypeStruct((N_CHIPS, rows, cols), BF16),
        compiler_params=_params(("parallel",)),
    )(me, w)


def _add_halves(g, got, c, name):
    _, _, rows, cols = g.shape
    tr = _row_tile(rows, 256)

    def body(c_ref, g_ref, got_ref, out_ref):
        out_ref[0] = (g_ref[0, 0].astype(F32) + got_ref[0].astype(F32)).astype(BF16)

    return pl.pallas_call(
        body, name=name,
        grid_spec=pltpu.PrefetchScalarGridSpec(
            num_scalar_prefetch=1, grid=(N_CHIPS, rows // tr),
            in_specs=[pl.BlockSpec((1, 1, tr, cols), lambda s, r, c_ref: (s, c_ref[0], r, 0)),
                      pl.BlockSpec((1, tr, cols), lambda s, r, c_ref: (s, r, 0))],
            out_specs=pl.BlockSpec((1, tr, cols), lambda s, r, c_ref: (s, r, 0))),
        out_shape=jax.ShapeDtypeStruct((N_CHIPS, rows, cols), BF16),
        compiler_params=_params(("parallel", "parallel")),
    )(c, g, got)


def _add_owner(p, got, where, name):
    _, rows, cols = p.shape
    tr = _row_tile(rows, 256)

    def body(w_ref, p_ref, got_ref, out_ref):
        acc = p_ref[0].astype(F32)
        for j in range(3):
            acc = acc + got_ref[j].astype(F32)
        out_ref[0] = acc

    return pl.pallas_call(
        body, name=name,
        grid_spec=pltpu.PrefetchScalarGridSpec(
            num_scalar_prefetch=1, grid=(rows // tr,),
            in_specs=[pl.BlockSpec((1, tr, cols), lambda r, w_ref: (w_ref[0], r, 0)),
                      pl.BlockSpec((3, tr, cols), lambda r, w_ref: (0, r, 0))],
            out_specs=pl.BlockSpec((1, tr, cols), lambda r, w_ref: (w_ref[1], r, 0))),
        out_shape=jax.ShapeDtypeStruct((2, rows, cols), F32),
        compiler_params=_params(("parallel",)),
    )(where, p, got)


class _GradReduce:
    def __init__(self, gs):
        self.gs = [g.reshape((N_CHIPS, 2, g.shape[1] // 2) + g.shape[2:]) for g in gs]
        self.c = lax.axis_index("c").astype(jnp.int32).reshape(1)
        chip = (2 * lax.axis_index("x") + lax.axis_index("y")).astype(jnp.int32)
        self.where = jnp.stack([chip, self.c[0]])
        self.result = None

    def swap(self):
        return _ride_swap_halves(self.gs)

    def to_owners(self, got):
        self.ps = [_add_halves(g, r, self.c, "grad_add_sibling_" + n) for g, r, n in zip(self.gs, got, ("in", "out"))]
        return _ride_to_owners(self.ps)

    def join(self, got):
        qs = [_add_owner(p, r, self.where, "grad_add_chips_" + n) for p, r, n in zip(self.ps, got, ("in", "out"))]
        return _ride_join_halves(qs)

    def finish(self, got):
        self.result = [a.reshape((a.shape[0] * a.shape[1],) + a.shape[2:]) for a in got]


def _adamw_math(w, g, m, v):
    m = ADAM_B1 * m + (1.0 - ADAM_B1) * g
    v = ADAM_B2 * v + (1.0 - ADAM_B2) * (g * g)
    m_hat = m / (1.0 - ADAM_B1 ** ADAM_STEP)
    v_hat = v / (1.0 - ADAM_B2 ** ADAM_STEP)
    delta = -ADAM_LR * (m_hat / (jnp.sqrt(v_hat) + ADAM_EPS) + ADAM_WD * w)
    return delta, m, v


def _adamw_small(w, g, m, v, name):
    def body(w_ref, g_ref, m_ref, v_ref, d_out, m_out, v_out):
        d_out[...], m_out[...], v_out[...] = _adamw_math(w_ref[...], g_ref[...], m_ref[...], v_ref[...])

    shape = jax.ShapeDtypeStruct(w.shape, F32)
    return pl.pallas_call(body, name="adamw_" + name, out_shape=[shape, shape, shape],
                          compiler_params=_params())(w, g, m, v)


def _adamw_layer(i, w, g, m, v, prev, name):
    depth, rows, cols = w.shape
    tr = _row_tile(rows, 256)
    n_prev = 0 if prev is None else 4

    def body(*refs):
        w_ref, g_ref, m_ref, v_ref = refs[:4]
        g_out, d_out, m_out, v_out = refs[4 + n_prev:]
        gv = g_ref[...]
        g_out[0] = gv
        d_out[0], m_out[0], v_out[0] = _adamw_math(w_ref[0], gv, m_ref[0], v_ref[0])

    lay = pl.BlockSpec((1, tr, cols), lambda r: (i, r, 0))
    shape = jax.ShapeDtypeStruct(w.shape, F32)
    return pl.pallas_call(
        body, name="adamw_" + name, grid=(rows // tr,),
        in_specs=[lay, pl.BlockSpec((tr, cols), lambda r: (r, 0)), lay, lay] + [ANY] * n_prev,
        out_specs=[lay] * 4, out_shape=[shape] * 4,
        input_output_aliases={4 + k: k for k in range(n_prev)},
        compiler_params=_params(("parallel",)),
    )(w, g, m, v, *(prev or ()))


def _adamw_cols_major(w, gs, m, v, name):
    depth, rows, cols = w.shape
    tr = max(t for t in range(1, 129) if cols % t == 0)
    wt, mt, vt = (jnp.transpose(a, (2, 0, 1)) for a in (w, m, v))
    gt = jnp.stack([g.T for g in gs], axis=1)

    def body(w_ref, g_ref, m_ref, v_ref, g_out, d_out, m_out, v_out):
        gv = g_ref[...]
        g_out[...] = gv
        d_out[...], m_out[...], v_out[...] = _adamw_math(w_ref[...], gv, m_ref[...], v_ref[...])

    spec = pl.BlockSpec((tr, depth, rows), lambda r: (r, 0, 0))
    shape = jax.ShapeDtypeStruct((cols, depth, rows), F32)
    outs = pl.pallas_call(body, name="adamw_" + name, grid=(cols // tr,), in_specs=[spec] * 4, out_specs=[spec] * 4,
                          out_shape=[shape] * 4, compiler_params=_params(("parallel",)))(wt, gt, mt, vt)
    return [jnp.transpose(a, (1, 2, 0)) for a in outs]


def _sum_leading(buf, name):
    n, rows, cols = buf.shape
    tr = _row_tile(rows, rows)

    def body(in_ref, out_ref):
        acc = in_ref[0]
        for k in range(1, n):
            acc = acc + in_ref[k]
        out_ref[...] = acc

    return pl.pallas_call(
        body, name=name, grid=(rows // tr,),
        in_specs=[pl.BlockSpec((n, tr, cols), lambda i: (0, i, 0))],
        out_specs=pl.BlockSpec((tr, cols), lambda i: (i, 0)),
        out_shape=jax.ShapeDtypeStruct((rows, cols), F32),
        compiler_params=_params(("parallel",)),
    )(buf)


_SHARDED_SMALL = ("meta", "conv_a_w", "ssm_conv_w", "conf_conv_w")
_LAYER_SMALL = ("pre_g", "post_g", "conv_a_w", "ssm_conv_w", "ssm_conv_b", "dt_bias", "a_log", "d_skip",
                "ssm_norm_g", "conf_conv_w", "conf_conv_b", "conf_ln_g", "conf_ln_b")
_WEIGHTS = ("meta", "pre_g", "post_g", "w_in", "w_out", "conv_a_w", "ssm_conv_w", "ssm_conv_b", "dt_bias", "a_log",
            "d_skip", "ssm_norm_g", "conf_conv_w", "conf_conv_b", "conf_ln_g", "conf_ln_b")


def _shard_last(a):
    return jnp.moveaxis(a.reshape(a.shape[:-1] + (N_CHIPS, a.shape[-1] // N_CHIPS)), -2, 0)


def _with_own_block(a, n, at):
    return lax.dynamic_update_index_in_dim(jnp.zeros((n,) + a.shape, a.dtype), a, at, 0)


def _with_own_columns(a, chip):
    k = a.shape[-1]
    return lax.dynamic_update_slice_in_dim(jnp.zeros(a.shape[:-1] + (N_CHIPS * k,), a.dtype), a, chip * k, a.ndim - 1)


def kernel(x, meta, pre_g, post_g, w_in, w_out, conv_a_w, ssm_conv_w, ssm_conv_b, dt_bias, a_log, d_skip, ssm_norm_g, conf_conv_w, conf_conv_b, conf_ln_g, conf_ln_b, loss_target, m_meta, m_pre_g, m_post_g, m_w_in, m_w_out, m_conv_a_w, m_ssm_conv_w, m_ssm_conv_b, m_dt_bias, m_a_log, m_d_skip, m_ssm_norm_g, m_conf_conv_w, m_conf_conv_b, m_conf_ln_g, m_conf_ln_b, v_meta, v_pre_g, v_post_g, v_w_in, v_w_out, v_conv_a_w, v_ssm_conv_w, v_ssm_conv_b, v_dt_bias, v_a_log, v_d_skip, v_ssm_norm_g, v_conf_conv_w, v_conf_conv_b, v_conf_ln_g, v_conf_ln_b):
    w = dict(meta=meta, pre_g=pre_g, post_g=post_g, w_in=w_in, w_out=w_out, conv_a_w=conv_a_w,
             ssm_conv_w=ssm_conv_w, ssm_conv_b=ssm_conv_b, dt_bias=dt_bias, a_log=a_log, d_skip=d_skip,
             ssm_norm_g=ssm_norm_g, conf_conv_w=conf_conv_w, conf_conv_b=conf_conv_b, conf_ln_g=conf_ln_g,
             conf_ln_b=conf_ln_b)
    mom = dict(meta=m_meta, pre_g=m_pre_g, post_g=m_post_g, w_in=m_w_in, w_out=m_w_out, conv_a_w=m_conv_a_w,
               ssm_conv_w=m_ssm_conv_w, ssm_conv_b=m_ssm_conv_b, dt_bias=m_dt_bias, a_log=m_a_log, d_skip=m_d_skip,
               ssm_norm_g=m_ssm_norm_g, conf_conv_w=m_conf_conv_w, conf_conv_b=m_conf_conv_b,
               conf_ln_g=m_conf_ln_g, conf_ln_b=m_conf_ln_b)
    vel = dict(meta=v_meta, pre_g=v_pre_g, post_g=v_post_g, w_in=v_w_in, w_out=v_w_out, conv_a_w=v_conv_a_w,
               ssm_conv_w=v_ssm_conv_w, ssm_conv_b=v_ssm_conv_b, dt_bias=v_dt_bias, a_log=v_a_log, d_skip=v_d_skip,
               ssm_norm_g=v_ssm_norm_g, conf_conv_w=v_conf_conv_w, conf_conv_b=v_conf_conv_b,
               conf_ln_g=v_conf_ln_g, conf_ln_b=v_conf_ln_b)
    bl, seq, d = x.shape
    dm = Dims(bl, seq, d)
    depth = w_in.shape[0]
    chip = (2 * lax.axis_index("x") + lax.axis_index("y")).astype(jnp.int32)
    dev = 2 * chip + lax.axis_index("c").astype(jnp.int32)
    cst = _head_consts(dm)

    full = dict(w)
    full.update(zip(_SHARDED_SMALL, _exchange("gather_small_weights", _ride_gather_small(
        [_with_own_columns(w[n], chip) for n in _SHARDED_SMALL]))))

    bases = [[_place_own(w_in, i, chip.reshape(1)), _place_own(w_out, i, chip.reshape(1))] for i in range(depth)]
    gathered = _exchange("gather_d2d_first", _ride_gather_d2d(_gather_ici_relayed(bases[0])))
    h = _embed(x, full["meta"], dm)
    saved, proj_w = [], []
    for i in range(depth):
        lw = {n: full[n][i] for n in _LAYER_SMALL}
        proj_w.append((_w_in_from_shards(gathered[0], dm), gathered[1].reshape(2 * d, d)))
        h, keep, gathered = _layer_fwd(h, lw, proj_w[i][0], proj_w[i][1], cst, dm,
                                       bases[i + 1] if i + 1 < depth else None)
        saved.append(keep)

    dh, loss = _loss_head(h, loss_target, dm)
    loss = lax.psum(loss, ("x", "y", "c"))

    small_g = {n: [None] * depth for n in _LAYER_SMALL}
    big = {"w_in": None, "w_out": None}
    g_in = [None] * depth
    reduce = None
    for i in reversed(range(depth)):
        lw = {n: full[n][i] for n in _LAYER_SMALL}
        dh, mine, sg = _layer_bwd(dh, saved[i], lw, proj_w[i][0], proj_w[i][1], cst, dm, reduce, last=i == 0)
        for n in _LAYER_SMALL:
            small_g[n][i] = sg[n]
        if reduce is not None:
            g_in[i + 1] = reduce.result[0]
            big["w_out"] = _adamw_layer(i + 1, w_out, reduce.result[1], m_w_out, v_w_out, big["w_out"], "w_out")
        reduce = mine
    g_in[0] = reduce.result[0]
    big["w_out"] = _adamw_layer(0, w_out, reduce.result[1], m_w_out, v_w_out, big["w_out"], "w_out")
    grad_x, gmeta = _unembed(dh, dm)

    g = {n: jnp.stack(v) for n, v in small_g.items()}
    g["meta"] = gmeta
    small = [n for n in _WEIGHTS if n not in ("w_in", "w_out")]
    flat = jnp.concatenate([g[n].reshape(-1) for n in small])
    rows = -(-flat.shape[0] // (16 * LANES)) * 16
    flat = jnp.pad(flat, (0, rows * LANES - flat.shape[0])).reshape(rows, LANES)
    parts = _gather_all(_with_own_block(flat, N_DEV, dev))
    total = _sum_leading(parts, "small_grads_sum").reshape(-1)
    big["w_in"] = _adamw_cols_major(w_in, g_in, m_w_in, v_w_in, "w_in")
    grads, deltas, new_m, new_v = {}, {}, {}, {}
    off = 0
    for n in small:
        size = g[n].size
        fullg = total[off:off + size].reshape(g[n].shape)
        off += size
        if n in _SHARDED_SMALL:
            fullg = lax.dynamic_index_in_dim(_shard_last(fullg), chip, axis=0, keepdims=False)
        grads[n] = fullg
        deltas[n], new_m[n], new_v[n] = _adamw_small(w[n], fullg, mom[n], vel[n], n)
    for n in ("w_in", "w_out"):
        grads[n], deltas[n], new_m[n], new_v[n] = big[n]

    return (loss, grad_x, *[grads[n] for n in _WEIGHTS], *[deltas[n] for n in _WEIGHTS],
            *[new_m[n] for n in _WEIGHTS], *[new_v[n] for n in _WEIGHTS])
```

```python
import jax
import jax.numpy as jnp
from jax import lax
from jax.experimental import pallas as pl
from jax.experimental.pallas import tpu as pltpu

F32 = jnp.float32
BF16 = jnp.bfloat16

N_META = 16
TT = 128
SSM_STATE = 128
SSM_GROUPS = 2
SSM_HEAD_DIM = 64
CONV_A_K = 3
SSM_CONV_K = 4
CONF_K = 31
NORM_EPS = 1e-6
LN_EPS = 1e-5
LANES = 128
MXU_DIM = 256
DT_PAD = LANES
CONF_HALO = 32
SMALL_HALO = 8
VMEM_LIMIT = 56 * 1024 * 1024
N_CHIPS = 4
N_DEV = 8

ADAM_LR = 0.001
ADAM_B1 = 0.9
ADAM_B2 = 0.999
ADAM_EPS = 1e-08
ADAM_WD = 0.01
ADAM_STEP = 10

MESH = pl.DeviceIdType.MESH
ANY = pl.BlockSpec(memory_space=pl.ANY)


class Dims:
    def __init__(self, bl, seq, d):
        self.BL, self.S, self.D = bl, seq, d
        self.L = seq + N_META
        self.Lp = -(-self.L // TT) * TT
        self.NT = self.Lp // TT
        self.R = bl * self.Lp
        self.DA = d // 2
        self.DB = d
        self.DC = d // 2
        self.H = self.DB // SSM_HEAD_DIM
        self.HPG = self.H // SSM_GROUPS
        self.GN = SSM_GROUPS * SSM_STATE
        self.WA = 4 * self.DA
        self.WB = 2 * self.DB + 2 * self.GN
        self.WC = 3 * self.DC
        self.NP = self.WA + self.WB + self.WC + DT_PAD
        self.NIN = self.WA + self.WB + self.H + self.WC
        self.XBC = self.DB + 2 * self.GN
        assert self.H % 2 == 0 and self.HPG % 2 == 0 and self.H <= DT_PAD
        assert self.DA % LANES == 0 and (self.WA + self.WB) % self.DC == 0 and self.WA % self.DB == 0


def _row_tile(n, target):
    best = None
    for t in range(16, min(n, target) + 1, 16):
        if n % t == 0:
            best = t
    assert best is not None
    return best


def _col_tile(n, target):
    best = None
    for t in range(LANES, min(n, target) + 1, LANES):
        if n % t == 0:
            best = t
    assert best is not None
    return best


def _params(sem=None):
    return pltpu.CompilerParams(dimension_semantics=sem, vmem_limit_bytes=VMEM_LIMIT)


def _sigmoid(x):
    return 1.0 / (1.0 + jnp.exp(-x))


def _silu_and_grad(x):
    s = _sigmoid(x)
    return x * s, s * (1.0 + x * (1.0 - s))


def _dot(a, b):
    return jnp.dot(a, b, preferred_element_type=F32)


def _dot_nt(a, b):
    return lax.dot_general(a, b, (((1,), (1,)), ((), ())), preferred_element_type=F32)


def _dot_tn(a, b):
    return lax.dot_general(a, b, (((0,), (0,)), ((), ())), preferred_element_type=F32)


def _split3(x):
    x1 = x.astype(BF16)
    r1 = x - x1.astype(F32)
    x2 = r1.astype(BF16)
    x3 = (r1 - x2.astype(F32)).astype(BF16)
    return x1, x2, x3


class Rider:
    def __init__(self, plan, ins, out_shapes, aliases, nsem):
        self.plan, self.ins, self.out_shapes, self.aliases, self.nsem = plan, list(ins), list(out_shapes), aliases, nsem


def _place():
    x, y, c = lax.axis_index("x"), lax.axis_index("y"), lax.axis_index("c")
    chips = [(1 - x, y), (x, 1 - y), (1 - x, 1 - y)]
    return x, y, c, chips


def _remote(k, src, dst, to, send_sems, recv_sems):
    return pltpu.make_async_remote_copy(src_ref=src, dst_ref=dst, send_sem=send_sems.at[k], recv_sem=recv_sems.at[k],
                                        device_id=to, device_id_type=MESH)


def _call(body, name, grid, in_specs, out_specs, out_shape, scratch_shapes, sem, args, rider=None):
    if rider is None:
        outs = pl.pallas_call(body, name=name, grid=grid, in_specs=in_specs, out_specs=out_specs, out_shape=out_shape,
                              scratch_shapes=scratch_shapes, compiler_params=_params(sem))(*args)
        return list(outs), []
    n_in, n_out, n_scr = len(args), len(out_shape), len(scratch_shapes)
    r_in, r_out = len(rider.ins), len(rider.out_shapes)

    def hosted(*refs):
        ins, rins = refs[:n_in], refs[n_in:n_in + r_in]
        o0 = n_in + r_in
        outs, routs = refs[o0:o0 + n_out], refs[o0 + n_out:o0 + n_out + r_out]
        scr = refs[o0 + n_out + r_out:o0 + n_out + r_out + n_scr]
        send_sems, recv_sems = refs[o0 + n_out + r_out + n_scr:]
        first = pl.program_id(0) == 0
        last = pl.program_id(0) == grid[0] - 1
        for ax in range(1, len(grid)):
            first = jnp.logical_and(first, pl.program_id(ax) == 0)
            last = jnp.logical_and(last, pl.program_id(ax) == grid[ax] - 1)

        @pl.when(first)
        def _():
            starts, _ = rider.plan(rins, routs, send_sems, recv_sems)
            for cp in starts:
                cp.start()

        body(*ins, *outs, *scr)

        @pl.when(last)
        def _():
            _, waits = rider.plan(rins, routs, send_sems, recv_sems)
            for wait in waits:
                wait()

    res = pl.pallas_call(
        hosted, name=name, grid=grid,
        in_specs=list(in_specs) + [ANY] * r_in, out_specs=list(out_specs) + [ANY] * r_out,
        out_shape=list(out_shape) + rider.out_shapes,
        input_output_aliases={n_in + k: n_out + v for k, v in rider.aliases.items()},
        scratch_shapes=list(scratch_shapes) + [pltpu.SemaphoreType.DMA((rider.nsem,)),
                                               pltpu.SemaphoreType.DMA((rider.nsem,))],
        compiler_params=_params(("arbitrary",) * len(grid)),
    )(*args, *rider.ins)
    return list(res[:n_out]), list(res[n_out:])


def _exchange(name, rider):
    r_in, r_out = len(rider.ins), len(rider.out_shapes)

    def body(*refs):
        rins, routs = refs[:r_in], refs[r_in:r_in + r_out]
        send_sems, recv_sems = refs[r_in + r_out:]
        starts, waits = rider.plan(rins, routs, send_sems, recv_sems)
        for cp in starts:
            cp.start()
        for wait in waits:
            wait()

    res = pl.pallas_call(
        body, name=name, in_specs=[ANY] * r_in, out_specs=[ANY] * r_out, out_shape=rider.out_shapes,
        input_output_aliases=dict(rider.aliases),
        scratch_shapes=[pltpu.SemaphoreType.DMA((rider.nsem,)), pltpu.SemaphoreType.DMA((rider.nsem,))],
    )(*rider.ins)
    return list(res)


def _same(arrays):
    return [jax.ShapeDtypeStruct(a.shape, a.dtype) for a in arrays]


class _SemsFrom:
    def __init__(self, sems, first):
        self.sems, self.first = sems, first

    @property
    def at(self):
        return self

    def __getitem__(self, k):
        return self.sems.at[self.first + k]


def _ride_both(r1, r2):
    n_in, n_out = len(r1.ins), len(r1.out_shapes)

    def plan(ins, outs, ss, rs):
        s1, w1 = r1.plan(ins[:n_in], outs[:n_out], ss, rs)
        s2, w2 = r2.plan(ins[n_in:], outs[n_out:], _SemsFrom(ss, r1.nsem), _SemsFrom(rs, r1.nsem))
        return s1 + s2, w1 + w2

    aliases = dict(r1.aliases)
    aliases.update({n_in + k: n_out + v for k, v in r2.aliases.items()})
    return Rider(plan, r1.ins + r2.ins, r1.out_shapes + r2.out_shapes, aliases, r1.nsem + r2.nsem)


def _ride_gather_ici(bases, part=0, nparts=1):
    n = len(bases)

    def plan(ins, outs, ss, rs):
        x, y, c, chips = _place()
        me = 2 * x + y
        starts, waits = [], []
        for a in range(n):
            half = outs[a].shape[1] // 2
            mine = pl.ds(c * half + part * (half // nparts), half // nparts)
            for j, chip in enumerate(chips):
                cp = _remote(3 * a + j, outs[a].at[me, mine], outs[a].at[me, mine], (*chip, c), ss, rs)
                got = outs[a].at[2 * chip[0] + chip[1], mine]
                starts.append(cp)
                waits += [cp.wait_send, _remote(3 * a + j, got, got, (*chip, c), ss, rs).wait_recv]
        return starts, waits

    return Rider(plan, bases, _same(bases), {a: a for a in range(n)}, 3 * n)


def _gather_ici_relayed(bases):
    n = len(bases)

    def body(*refs):
        outs = refs[n:2 * n]
        ss, rs = refs[2 * n:]
        x, y, c, _ = _place()
        me, xn, yn, dg = 2 * x + y, 2 * (1 - x) + y, 2 * x + (1 - y), 2 * (1 - x) + (1 - y)
        to_x, to_y = (1 - x, y, c), (x, 1 - y, c)
        sends = []

        def send(k, piece, to):
            cp = _remote(k, piece, piece, to, ss, rs)
            cp.start()
            sends.append(cp)

        def arrived(k, piece, frm):
            _remote(k, piece, piece, frm, ss, rs).wait_recv()

        rows = []
        for a in range(n):
            half = outs[a].shape[1] // 2
            rows.append((pl.ds(c * half, half), pl.ds(c * half, half // 2), pl.ds(c * half + half // 2, half // 2)))
            send(4 * a, outs[a].at[me, rows[a][0]], to_x)
            send(4 * a + 1, outs[a].at[me, rows[a][0]], to_y)
        for a in range(n):
            mine, lo, hi = rows[a]
            arrived(4 * a, outs[a].at[xn, mine], to_x)
            send(4 * a + 2, outs[a].at[xn, lo], to_y)
            arrived(4 * a + 1, outs[a].at[yn, mine], to_y)
            send(4 * a + 3, outs[a].at[yn, hi], to_x)
        for a in range(n):
            mine, lo, hi = rows[a]
            arrived(4 * a + 2, outs[a].at[dg, lo], to_y)
            arrived(4 * a + 3, outs[a].at[dg, hi], to_x)
        for cp in sends:
            cp.wait_send()

    return pl.pallas_call(
        body, name="gather_ici_first", in_specs=[ANY] * n, out_specs=[ANY] * n, out_shape=_same(bases),
        input_output_aliases={a: a for a in range(n)},
        scratch_shapes=[pltpu.SemaphoreType.DMA((4 * n,)), pltpu.SemaphoreType.DMA((4 * n,))],
    )(*bases)


def _ride_gather_d2d(bases):
    n = len(bases)

    def plan(ins, outs, ss, rs):
        x, y, c, chips = _place()
        sib = (x, y, 1 - c)
        starts, waits = [], []
        for a in range(n):
            half = outs[a].shape[1] // 2
            for j, chip in enumerate(chips):
                frm = 2 * chip[0] + chip[1]
                got = outs[a].at[frm, pl.ds(c * half, half)]
                theirs = outs[a].at[frm, pl.ds((1 - c) * half, half)]
                cp = _remote(3 * a + j, got, got, sib, ss, rs)
                starts.append(cp)
                waits += [cp.wait_send, _remote(3 * a + j, theirs, theirs, sib, ss, rs).wait_recv]
        return starts, waits

    return Rider(plan, bases, _same(bases), {a: a for a in range(n)}, 3 * n)


def _ride_gather_small(bases):
    n = len(bases)

    def plan(ins, outs, ss, rs):
        x, y, c, chips = _place()
        me = 2 * x + y
        starts, waits = [], []
        for a in range(n):
            k = outs[a].shape[-1] // N_CHIPS
            lead = (slice(None),) * (len(outs[a].shape) - 1)
            at = (lambda s: pl.multiple_of(s * k, LANES)) if k % LANES == 0 else (lambda s: s * k)
            cols = lambda s: outs[a].at[lead + (pl.ds(at(s), k),)]
            for j, chip in enumerate(chips):
                cp = _remote(3 * a + j, cols(me), cols(me), (*chip, c), ss, rs)
                got = cols(2 * chip[0] + chip[1])
                starts.append(cp)
                waits += [cp.wait_send, _remote(3 * a + j, got, got, (*chip, c), ss, rs).wait_recv]
        return starts, waits

    return Rider(plan, bases, _same(bases), {a: a for a in range(n)}, 3 * n)


def _ride_swap_halves(gs):
    n = len(gs)

    def plan(ins, outs, ss, rs):
        x, y, c, _ = _place()
        cps = [_remote(a, ins[a].at[:, 1 - c], outs[a], (x, y, 1 - c), ss, rs) for a in range(n)]
        return cps, [cp.wait for cp in cps]

    shapes = [jax.ShapeDtypeStruct((g.shape[0],) + g.shape[2:], g.dtype) for g in gs]
    return Rider(plan, gs, shapes, {}, n)


def _ride_to_owners(ps):
    n = len(ps)

    def plan(ins, outs, ss, rs):
        x, y, c, chips = _place()
        cps = []
        for a in range(n):
            for j, chip in enumerate(chips):
                cps.append(_remote(3 * a + j, ins[a].at[2 * chip[0] + chip[1]], outs[a].at[j], (*chip, c), ss, rs))
        return cps, [cp.wait for cp in cps]

    shapes = [jax.ShapeDtypeStruct((3,) + p.shape[1:], p.dtype) for p in ps]
    return Rider(plan, ps, shapes, {}, 3 * n)


def _ride_join_halves(qs):
    n = len(qs)

    def plan(ins, outs, ss, rs):
        x, y, c, _ = _place()
        sib = (x, y, 1 - c)
        starts, waits = [], []
        for a in range(n):
            cp = _remote(a, outs[a].at[c], outs[a].at[c], sib, ss, rs)
            starts.append(cp)
            waits += [cp.wait_send, _remote(a, outs[a].at[1 - c], outs[a].at[1 - c], sib, ss, rs).wait_recv]
        return starts, waits

    return Rider(plan, qs, _same(qs), {a: a for a in range(n)}, n)


def _gather_all(base):
    def body(in_ref, out_ref, ss, rs):
        x, y, c, chips = _place()
        sib = (x, y, 1 - c)
        block = lambda cx, cy, cc: out_ref.at[4 * cx + 2 * cy + cc]
        mine = block(x, y, c)
        first = [_remote(j, mine, mine, (*chip, c), ss, rs) for j, chip in enumerate(chips)]
        first.append(_remote(3, mine, mine, sib, ss, rs))
        for cp in first:
            cp.start()
        passed = []
        for j, chip in enumerate(chips):
            got = block(*chip, c)
            _remote(j, got, got, (*chip, c), ss, rs).wait_recv()
            passed.append(_remote(4 + j, got, got, sib, ss, rs))
            passed[-1].start()
        theirs = block(x, y, 1 - c)
        _remote(3, theirs, theirs, sib, ss, rs).wait_recv()
        for j, chip in enumerate(chips):
            got = block(*chip, 1 - c)
            _remote(4 + j, got, got, sib, ss, rs).wait_recv()
        for cp in first + passed:
            cp.wait_send()

    return pl.pallas_call(
        body, name="small_grads_gather_all", in_specs=[ANY], out_specs=ANY,
        out_shape=jax.ShapeDtypeStruct(base.shape, base.dtype), input_output_aliases={0: 0},
        scratch_shapes=[pltpu.SemaphoreType.DMA((N_DEV - 1,)), pltpu.SemaphoreType.DMA((N_DEV - 1,))],
    )(base)


def _embed(x, meta, dm):
    dc = _col_tile(dm.D, 256)
    s, lp = dm.S, dm.Lp

    def body(x_ref, meta_ref, h_ref):
        h_ref[0:N_META, :] = meta_ref[...]
        h_ref[N_META:N_META + s, :] = x_ref[0]
        if lp > N_META + s:
            h_ref[N_META + s:lp, :] = jnp.zeros((lp - N_META - s, dc), F32)

    return pl.pallas_call(
        body, name="embed", grid=(dm.BL, dm.D // dc),
        in_specs=[pl.BlockSpec((1, s, dc), lambda b, j: (b, 0, j)),
                  pl.BlockSpec((N_META, dc), lambda b, j: (0, j))],
        out_specs=pl.BlockSpec((lp, dc), lambda b, j: (b, j)),
        out_shape=jax.ShapeDtypeStruct((dm.R, dm.D), F32),
        compiler_params=_params(("parallel", "parallel")),
    )(x, meta)


def _loss_head(h, target, dm):
    dc = _col_tile(dm.D, 256)
    s, lp, nj = dm.S, dm.Lp, dm.D // dc

    def body(h_ref, t_ref, dh_ref, l_ref):
        diff = h_ref[N_META:N_META + s, :] - t_ref[0]
        dh_ref[0:N_META, :] = jnp.zeros((N_META, dc), F32)
        dh_ref[N_META:N_META + s, :] = diff * (1.0 / dm.D)
        if lp > N_META + s:
            dh_ref[N_META + s:lp, :] = jnp.zeros((lp - N_META - s, dc), F32)
        l_ref[...] = jnp.full((8, LANES), (0.5 / dm.D) * jnp.sum(diff * diff), F32)

    dh, part = pl.pallas_call(
        body, name="loss_head", grid=(dm.BL, nj),
        in_specs=[pl.BlockSpec((lp, dc), lambda b, j: (b, j)),
                  pl.BlockSpec((1, s, dc), lambda b, j: (b, 0, j))],
        out_specs=[pl.BlockSpec((lp, dc), lambda b, j: (b, j)),
                   pl.BlockSpec((8, LANES), lambda b, j: (b * nj + j, 0))],
        out_shape=[jax.ShapeDtypeStruct((dm.R, dm.D), F32),
                   jax.ShapeDtypeStruct((dm.BL * nj * 8, LANES), F32)],
        compiler_params=_params(("parallel", "parallel")),
    )(h, target)
    return dh, jnp.sum(part[::8, 0])


def _unembed(dh, dm):
    dc = _col_tile(dm.D, 256)
    s, lp = dm.S, dm.Lp

    def body(dh_ref, gx_ref, gm_ref):
        gx_ref[0] = dh_ref[N_META:N_META + s, :]

        @pl.when(pl.program_id(1) == 0)
        def _():
            gm_ref[...] = dh_ref[0:N_META, :]

        @pl.when(pl.program_id(1) > 0)
        def _():
            gm_ref[...] = gm_ref[...] + dh_ref[0:N_META, :]

    return pl.pallas_call(
        body, name="unembed", grid=(dm.D // dc, dm.BL),
        in_specs=[pl.BlockSpec((lp, dc), lambda j, b: (b, j))],
        out_specs=[pl.BlockSpec((1, s, dc), lambda j, b: (b, 0, j)),
                   pl.BlockSpec((N_META, dc), lambda j, b: (0, j))],
        out_shape=[jax.ShapeDtypeStruct((dm.BL, s, dm.D), F32),
                   jax.ShapeDtypeStruct((N_META, dm.D), F32)],
        compiler_params=_params(("parallel", "arbitrary")),
    )(dh)


def _fwd_in(h, pre_g, w, dm, rider=None):
    tm = _row_tile(dm.R, 2176)
    tn = _col_tile(dm.NP, 896)
    nj = dm.NP // tn

    def body(h_ref, g_ref, w_ref, proj_ref, dt_ref, hn_ref):
        @pl.when(pl.program_id(1) == 0)
        def _():
            xf = h_ref[...]
            r = lax.rsqrt(jnp.mean(xf * xf, axis=-1, keepdims=True) + NORM_EPS)
            hn_ref[...] = (xf * r * g_ref[...]).astype(BF16)

        res = _dot(hn_ref[...], w_ref[...])
        proj_ref[...] = res.astype(BF16)

        @pl.when(pl.program_id(1) == nj - 1)
        def _():
            dt_ref[...] = res[:, tn - DT_PAD:tn]

    return _call(
        body, "fwd_in", (dm.R // tm, nj),
        [pl.BlockSpec((tm, dm.D), lambda i, j: (i, 0)),
         pl.BlockSpec((1, dm.D), lambda i, j: (0, 0)),
         pl.BlockSpec((dm.D, tn), lambda i, j: (0, j))],
        [pl.BlockSpec((tm, tn), lambda i, j: (i, j)),
         pl.BlockSpec((tm, DT_PAD), lambda i, j: (i, 0)),
         pl.BlockSpec((tm, dm.D), lambda i, j: (i, 0))],
        [jax.ShapeDtypeStruct((dm.R, dm.NP), BF16), jax.ShapeDtypeStruct((dm.R, DT_PAD), F32),
         jax.ShapeDtypeStruct((dm.R, dm.D), BF16)],
        [], ("parallel", "arbitrary"), (h, pre_g, w), rider)


def _fwd_out(ya, yb, yc, w_out, h, post_g, dm, rider=None):
    tm = _row_tile(dm.Lp, 544)
    tiles_per_seq = dm.Lp // tm
    da, db, dc = dm.DA, dm.DB, dm.DC

    def body(ya_ref, yb_ref, yc_ref, w_ref, h_ref, g_ref, hn_ref, m_ref):
        m = _dot(ya_ref[...], w_ref[0:da, :])
        m = m + _dot(yb_ref[...], w_ref[da:da + db, :])
        m = m + _dot(yc_ref[...], w_ref[da + db:da + db + dc, :])
        m_ref[...] = m
        r = lax.rsqrt(jnp.mean(m * m, axis=-1, keepdims=True) + NORM_EPS)
        t = (pl.program_id(0) % tiles_per_seq) * tm + lax.broadcasted_iota(jnp.int32, (tm, 1), 0)
        keep = (t < dm.L).astype(F32)
        hn_ref[...] = (h_ref[...] + m * r * g_ref[...]) * keep

    row = lambda i: (i, 0)
    fixed = lambda i: (0, 0)
    return _call(
        body, "fwd_out", (dm.R // tm,),
        [pl.BlockSpec((tm, da), row), pl.BlockSpec((tm, db), row), pl.BlockSpec((tm, dc), row),
         pl.BlockSpec((2 * dm.D, dm.D), fixed), pl.BlockSpec((tm, dm.D), row), pl.BlockSpec((1, dm.D), fixed)],
        [pl.BlockSpec((tm, dm.D), row), pl.BlockSpec((tm, dm.D), row)],
        [jax.ShapeDtypeStruct((dm.R, dm.D), F32), jax.ShapeDtypeStruct((dm.R, dm.D), F32)],
        [], ("parallel",), (ya, yb, yc, w_out, h, post_g), rider)


def _bwd_out(dh, m, post_g, w_out, ya, yb, yc, dm, rider=None):
    tm = _row_tile(dm.R, MXU_DIM)
    da, db, dc = dm.DA, dm.DB, dm.DC

    def body(dh_ref, m_ref, g_ref, w_ref, ya_ref, yb_ref, yc_ref, dya_ref, dyb_ref, dyc_ref, dw_ref, dg_ref):
        @pl.when(pl.program_id(0) == 0)
        def _():
            dw_ref[...] = jnp.zeros_like(dw_ref)
            dg_ref[...] = jnp.zeros_like(dg_ref)

        m = m_ref[...]
        dh_ = dh_ref[...]
        r = lax.rsqrt(jnp.mean(m * m, axis=-1, keepdims=True) + NORM_EPS)
        n = m * r
        dg_ref[0:1, :] = dg_ref[0:1, :] + jnp.sum(dh_ * n, axis=0, keepdims=True)
        dn = dh_ * g_ref[...]
        dm_ = (r * (dn - n * jnp.mean(dn * n, axis=-1, keepdims=True))).astype(BF16)
        dya_ref[...] = _dot_nt(dm_, w_ref[0:da, :])
        dyb_ref[...] = _dot_nt(dm_, w_ref[da:da + db, :])
        dyc_ref[...] = _dot_nt(dm_, w_ref[da + db:da + db + dc, :])
        dw_ref[0:da, :] = dw_ref[0:da, :] + _dot_tn(ya_ref[...], dm_)
        dw_ref[da:da + db, :] = dw_ref[da:da + db, :] + _dot_tn(yb_ref[...], dm_)
        dw_ref[da + db:da + db + dc, :] = dw_ref[da + db:da + db + dc, :] + _dot_tn(yc_ref[...], dm_)

    row = lambda i: (i, 0)
    fixed = lambda i: (0, 0)
    return _call(
        body, "bwd_out", (dm.R // tm,),
        [pl.BlockSpec((tm, dm.D), row), pl.BlockSpec((tm, dm.D), row), pl.BlockSpec((1, dm.D), fixed),
         pl.BlockSpec((2 * dm.D, dm.D), fixed),
         pl.BlockSpec((tm, da), row), pl.BlockSpec((tm, db), row), pl.BlockSpec((tm, dc), row)],
        [pl.BlockSpec((tm, da), row), pl.BlockSpec((tm, db), row), pl.BlockSpec((tm, dc), row),
         pl.BlockSpec((2 * dm.D, dm.D), fixed), pl.BlockSpec((8, dm.D), fixed)],
        [jax.ShapeDtypeStruct((dm.R, da), F32), jax.ShapeDtypeStruct((dm.R, db), F32),
         jax.ShapeDtypeStruct((dm.R, dc), F32),
         jax.ShapeDtypeStruct((2 * dm.D, dm.D), F32), jax.ShapeDtypeStruct((8, dm.D), F32)],
        [], ("arbitrary",), (dh, m, post_g, w_out, ya, yb, yc), rider)


def _bwd_in_dx(dpa, dpb, dpc, dpt, w, h, dh, pre_g, dm, rider=None):
    tm = _row_tile(dm.R, 272)
    wa, wb, wc = dm.WA, dm.WB, dm.WC

    def body(dpa_ref, dpb_ref, dpc_ref, dpt_ref, w_ref, h_ref, dh_ref, g_ref, out_ref, dg_ref):
        @pl.when(pl.program_id(0) == 0)
        def _():
            dg_ref[...] = jnp.zeros_like(dg_ref)

        dhn = _dot_nt(dpa_ref[...], w_ref[:, 0:wa])
        dhn = dhn + _dot_nt(dpb_ref[...], w_ref[:, wa:wa + wb])
        dhn = dhn + _dot_nt(dpc_ref[...], w_ref[:, wa + wb:wa + wb + wc])
        dhn = dhn + _dot_nt(dpt_ref[...], w_ref[:, wa + wb + wc:wa + wb + wc + DT_PAD])
        xf = h_ref[...]
        r = lax.rsqrt(jnp.mean(xf * xf, axis=-1, keepdims=True) + NORM_EPS)
        n = xf * r
        dg_ref[0:1, :] = dg_ref[0:1, :] + jnp.sum(dhn * n, axis=0, keepdims=True)
        dn = dhn * g_ref[...]
        out_ref[...] = dh_ref[...] + r * (dn - n * jnp.mean(dn * n, axis=-1, keepdims=True))

    row = lambda i: (i, 0)
    fixed = lambda i: (0, 0)
    return _call(
        body, "bwd_in_dx", (dm.R // tm,),
        [pl.BlockSpec((tm, wa), row), pl.BlockSpec((tm, wb), row), pl.BlockSpec((tm, wc), row),
         pl.BlockSpec((tm, DT_PAD), row), pl.BlockSpec((dm.D, dm.NP), fixed),
         pl.BlockSpec((tm, dm.D), row), pl.BlockSpec((tm, dm.D), row), pl.BlockSpec((1, dm.D), fixed)],
        [pl.BlockSpec((tm, dm.D), row), pl.BlockSpec((8, dm.D), fixed)],
        [jax.ShapeDtypeStruct((dm.R, dm.D), F32), jax.ShapeDtypeStruct((8, dm.D), F32)],
        [], ("arbitrary",), (dpa, dpb, dpc, dpt, w, h, dh, pre_g), rider)


def _bwd_in_dw(hn, dp, dm, piece):
    width = dp.shape[1]
    tn = _col_tile(width, 512)

    def body(hn_ref, dp_ref, dw_ref):
        dw_ref[...] = _dot_tn(hn_ref[...], dp_ref[...])

    return pl.pallas_call(
        body, name="bwd_in_dw_" + piece, grid=(width // tn,),
        in_specs=[pl.BlockSpec((dm.R, dm.D), lambda j: (0, 0)), pl.BlockSpec((dm.R, tn), lambda j: (0, j))],
        out_specs=pl.BlockSpec((dm.D, tn), lambda j: (0, j)),
        out_shape=jax.ShapeDtypeStruct((dm.D, width), F32),
        compiler_params=_params(("parallel",)),
    )(hn, dp)


def _tile_index(dm, reverse):
    if reverse:
        return lambda b, i: b * dm.NT + (dm.NT - 1 - i)
    return lambda b, i: b * dm.NT + i


def _halo_index(dm, rows):
    per_tile = TT // rows
    return lambda b, i: jnp.maximum((b * dm.NT + (dm.NT - 1 - i)) * per_tile - 1, 0)


HALO_BLOCK = 16


def _last_rows(x):
    return x.astype(F32)[HALO_BLOCK - SMALL_HALO:HALO_BLOCK]


MIX_A_ROWS = 288


def _mix_a_fwd(proj, conv_w, dm):
    da = dm.DA
    ta = _row_tile(dm.Lp, MIX_A_ROWS)
    nta = dm.Lp // ta
    bl = dm.BL

    def setup(ab_ref, ac_ref, ax_ref, az_ref, w_ref, y_ref, pbuf):
        i = pl.program_id(0)

        @pl.when(i == 0)
        def _():
            pbuf[0:SMALL_HALO, :] = jnp.zeros((SMALL_HALO, da), F32)

        @pl.when(i > 0)
        def _():
            pbuf[0:SMALL_HALO, :] = pbuf[ta:ta + SMALL_HALO, :]

    def body(ab_ref, ac_ref, ax_ref, az_ref, w_ref, y_ref, pbuf):
        for lb in range(da // LANES):
            cs = slice(lb * LANES, (lb + 1) * LANES)
            p = ac_ref[:, cs].astype(F32) * ax_ref[:, cs].astype(F32)
            pbuf[SMALL_HALO:SMALL_HALO + ta, cs] = p
            q = (w_ref[0:1, cs] * pbuf[6:6 + ta, cs] + w_ref[1:2, cs] * pbuf[7:7 + ta, cs] + w_ref[2:3, cs] * p)
            az = az_ref[:, cs].astype(F32)
            y_ref[:, cs] = (ab_ref[:, cs].astype(F32) * q * (az * _sigmoid(az))).astype(BF16)
            yield

    proj3 = proj.reshape(bl, dm.Lp, dm.NP)
    col = lambda k: pl.BlockSpec((bl, ta, da), lambda i: (0, i, k))
    return pl.pallas_call(
        _per_sequence(setup, body, bl, [True] * 4 + [False] + [True, True]), name="mix_a_fwd", grid=(nta,),
        in_specs=[col(0), col(1), col(2), col(3), pl.BlockSpec((CONV_A_K, da), lambda i: (0, 0))],
        out_specs=col(0),
        out_shape=jax.ShapeDtypeStruct((bl, dm.Lp, da), BF16),
        scratch_shapes=[pltpu.VMEM((bl, SMALL_HALO + ta, da), F32)],
        compiler_params=_params(("arbitrary",)),
    )(proj3, proj3, proj3, proj3, conv_w).reshape(dm.R, da)


def _mix_a_bwd(proj, dya, conv_w, dm):
    da = dm.DA
    ta = _row_tile(dm.Lp, MIX_A_ROWS)
    nta = dm.Lp // ta
    bl = dm.BL

    def setup(ab_ref, ac_ref, ax_ref, az_ref, ach_ref, axh_ref, dy_ref, w_ref, dp_ref, dw_ref, pbuf, dqbuf):
        i = pl.program_id(0)

        @pl.when(i == 0)
        def _():
            dw_ref[...] = jnp.zeros_like(dw_ref)
            dqbuf[ta:ta + SMALL_HALO, :] = jnp.zeros((SMALL_HALO, da), F32)

        @pl.when(i > 0)
        def _():
            dqbuf[ta:ta + SMALL_HALO, :] = dqbuf[0:SMALL_HALO, :]

    def body(ab_ref, ac_ref, ax_ref, az_ref, ach_ref, axh_ref, dy_ref, w_ref, dp_ref, dw_ref, pbuf, dqbuf):
        halo_on = jnp.where(pl.program_id(0) == nta - 1, 0.0, 1.0)
        for lb in range(da // LANES):
            cs = slice(lb * LANES, (lb + 1) * LANES)
            pbuf[0:SMALL_HALO, cs] = (_last_rows(ach_ref[:, cs]) * _last_rows(axh_ref[:, cs])) * halo_on
            ac, ax, ab, az = (r[:, cs].astype(F32) for r in (ac_ref, ax_ref, ab_ref, az_ref))
            p = ac * ax
            pbuf[SMALL_HALO:SMALL_HALO + ta, cs] = p
            p1 = pbuf[7:7 + ta, cs]
            p2 = pbuf[6:6 + ta, cs]
            w0, w1, w2 = w_ref[0:1, cs], w_ref[1:2, cs], w_ref[2:3, cs]
            q = w0 * p2 + w1 * p1 + w2 * p
            sz, dsz = _silu_and_grad(az)
            dy = dy_ref[:, cs]
            t1 = dy * ab
            dq = t1 * sz
            dqbuf[0:ta, cs] = dq
            dpv = w2 * dq + w1 * dqbuf[1:1 + ta, cs] + w0 * dqbuf[2:2 + ta, cs]
            dp_ref[:, lb * LANES:(lb + 1) * LANES] = (dy * q * sz).astype(BF16)
            dp_ref[:, da + lb * LANES:da + (lb + 1) * LANES] = (dpv * ax).astype(BF16)
            dp_ref[:, 2 * da + lb * LANES:2 * da + (lb + 1) * LANES] = (dpv * ac).astype(BF16)
            dp_ref[:, 3 * da + lb * LANES:3 * da + (lb + 1) * LANES] = (t1 * q * dsz).astype(BF16)
            dw_ref[0, 0:1, cs] = dw_ref[0, 0:1, cs] + jnp.sum(dq * p2, axis=0, keepdims=True)
            dw_ref[0, 1:2, cs] = dw_ref[0, 1:2, cs] + jnp.sum(dq * p1, axis=0, keepdims=True)
            dw_ref[0, 2:3, cs] = dw_ref[0, 2:3, cs] + jnp.sum(dq * p, axis=0, keepdims=True)
            yield

    proj3 = proj.reshape(bl, dm.Lp, dm.NP)
    col = lambda w, k: pl.BlockSpec((bl, ta, w), lambda i: (0, nta - 1 - i, k))
    halo = lambda k: pl.BlockSpec((bl, HALO_BLOCK, da),
                                  lambda i: (0, jnp.maximum((nta - 1 - i) * (ta // HALO_BLOCK) - 1, 0), k))
    dp, dw = pl.pallas_call(
        _per_sequence(setup, body, bl, [True] * 7 + [False] + [True, "keep"] + [True, True]),
        name="mix_a_bwd", grid=(nta,),
        in_specs=[col(da, 0), col(da, 1), col(da, 2), col(da, 3), halo(1), halo(2), col(da, 0),
                  pl.BlockSpec((CONV_A_K, da), lambda i: (0, 0))],
        out_specs=[col(dm.WA, 0), pl.BlockSpec((bl, 8, da), lambda i: (0, 0, 0))],
        out_shape=[jax.ShapeDtypeStruct((bl, dm.Lp, dm.WA), BF16), jax.ShapeDtypeStruct((bl, 8, da), F32)],
        scratch_shapes=[pltpu.VMEM((bl, SMALL_HALO + ta, da), F32), pltpu.VMEM((bl, ta + SMALL_HALO, da), F32)],
        compiler_params=_params(("arbitrary",)),
    )(proj3, proj3, proj3, proj3, proj3, proj3, dya.reshape(bl, dm.Lp, da), conv_w)
    return dp.reshape(dm.R, dm.WA), dw


SUBLANES = 8
SHIFT_ROWS = TT + CONF_HALO - SUBLANES


TAP_ROWS = 64


def _split_lanes(buf, rows, val):
    for lb in range(val.shape[1] // LANES):
        buf[lb, rows, :] = val[:, lb * LANES:(lb + 1) * LANES]


def _join_lanes(buf):
    return jnp.concatenate([buf[lb] for lb in range(buf.shape[0])], axis=1)


def _fill_shifted(buf, shifted):
    def step(lb, carry):
        for r in range(1, SUBLANES):
            shifted[lb, r - 1, 0:SHIFT_ROWS, :] = buf[lb, r:r + SHIFT_ROWS, :]
        return carry

    lax.fori_loop(0, buf.shape[0], step, 0)


def _window(buf, shifted, d, r0, lb):
    r = d % SUBLANES
    rows = pl.ds(pl.multiple_of(r0 + (d - r), SUBLANES), TAP_ROWS)
    return buf[lb, rows, :] if r == 0 else shifted[lb, r - 1, rows, :]


def _tap_loop(nlb, body):
    per_lb = TT // TAP_ROWS

    def step(it, carry):
        lb = it // per_lb
        body(lb, pl.ds(pl.multiple_of(lb * LANES, LANES), LANES), pl.multiple_of((it % per_lb) * TAP_ROWS, TAP_ROWS))
        return carry

    lax.fori_loop(0, nlb * per_lb, step, 0)


TAP_CHAINS = 4


def _tree_sum(terms):
    sums = list(terms[:TAP_CHAINS])
    for n, t in enumerate(terms[TAP_CHAINS:]):
        sums[n % TAP_CHAINS] = sums[n % TAP_CHAINS] + t
    while len(sums) > 1:
        sums = [a + b for a, b in zip(sums[0::2], sums[1::2])] + ([sums[-1]] if len(sums) % 2 else [])
    return sums[0]


def _conf_conv(ubuf, ushift, w_ref, b_ref, u1buf):
    _fill_shifted(ubuf, ushift)

    def piece(lb, cs, r0):
        taps = [w_ref[k:k + 1, cs] * _window(ubuf, ushift, CONF_HALO - (CONF_K - 1) + k, r0, lb)
                for k in range(CONF_K)]
        u1buf[lb, pl.ds(r0, TAP_ROWS), :] = _tree_sum(taps) + b_ref[0:1, cs]

    _tap_loop(ubuf.shape[0], piece)


def _mix_c_fwd(proj, conv_w, conv_b, ln_g, ln_b, dm):
    dc = dm.DC
    nlb = dc // LANES
    c0 = (dm.WA + dm.WB) // dc
    ti = _tile_index(dm, False)

    def body(ca_ref, cg_ref, cz_ref, w_ref, b_ref, g_ref, be_ref, y_ref, u1_ref, ubuf, u1buf, ushift):
        i = pl.program_id(1)

        @pl.when(i == 0)
        def _():
            ubuf[:, 0:CONF_HALO, :] = jnp.zeros((nlb, CONF_HALO, LANES), F32)

        @pl.when(i > 0)
        def _():
            ubuf[:, 0:CONF_HALO, :] = ubuf[:, TT:TT + CONF_HALO, :]

        _split_lanes(ubuf, slice(CONF_HALO, CONF_HALO + TT),
                     ca_ref[...].astype(F32) * _sigmoid(cg_ref[...].astype(F32)))
        _conf_conv(ubuf, ushift, w_ref, b_ref, u1buf)
        u1 = _join_lanes(u1buf)
        u1_ref[...] = u1
        mu = jnp.mean(u1, axis=-1, keepdims=True)
        xc = u1 - mu
        rstd = lax.rsqrt(jnp.mean(xc * xc, axis=-1, keepdims=True) + LN_EPS)
        u2 = xc * rstd * g_ref[...] + be_ref[...]
        cz = cz_ref[...].astype(F32)
        y_ref[...] = ((u2 * _sigmoid(u2)) * (cz * _sigmoid(cz))).astype(BF16)

    col = lambda k: pl.BlockSpec((TT, dc), lambda b, i: (ti(b, i), c0 + k))
    vec = pl.BlockSpec((1, dc), lambda b, i: (0, 0))
    return pl.pallas_call(
        body, name="mix_c_fwd", grid=(dm.BL, dm.NT),
        in_specs=[col(0), col(1), col(2), pl.BlockSpec((CONF_K, dc), lambda b, i: (0, 0)), vec, vec, vec],
        out_specs=[pl.BlockSpec((TT, dc), lambda b, i: (ti(b, i), 0))] * 2,
        out_shape=[jax.ShapeDtypeStruct((dm.R, dc), BF16), jax.ShapeDtypeStruct((dm.R, dc), F32)],
        scratch_shapes=[pltpu.VMEM((nlb, CONF_HALO + TT, LANES), F32), pltpu.VMEM((nlb, TT, LANES), F32),
                        pltpu.VMEM((nlb, SUBLANES - 1, SHIFT_ROWS, LANES), F32)],
        compiler_params=_params(("parallel", "arbitrary")),
    )(proj, proj, proj, conv_w, conv_b, ln_g, ln_b)


def _mix_c_bwd(proj, u1, dyc, conv_w, ln_g, ln_b, dm):
    dc = dm.DC
    nlb = dc // LANES
    c0 = (dm.WA + dm.WB) // dc
    ti = _tile_index(dm, True)
    hi = _halo_index(dm, CONF_HALO)

    def body(ca_ref, cg_ref, cz_ref, cah_ref, cgh_ref, u1_ref, dy_ref, w_ref, g_ref, be_ref,
             dp_ref, dw_ref, dv_ref, ubuf, dubuf, du0buf, ushift, dshift, dwacc):
        i = pl.program_id(1)
        halo_on = jnp.where(i == dm.NT - 1, 0.0, 1.0)

        @pl.when(i == 0)
        def _():
            dwacc[...] = jnp.zeros_like(dwacc)
            dv_ref[...] = jnp.zeros_like(dv_ref)
            dubuf[:, TT:TT + CONF_HALO, :] = jnp.zeros((nlb, CONF_HALO, LANES), F32)

        @pl.when(i > 0)
        def _():
            dubuf[:, TT:TT + CONF_HALO, :] = dubuf[:, 0:CONF_HALO, :]

        _split_lanes(ubuf, slice(0, CONF_HALO),
                     cah_ref[...].astype(F32) * _sigmoid(cgh_ref[...].astype(F32)) * halo_on)
        sgg = _sigmoid(cg_ref[...].astype(F32))
        ca = ca_ref[...].astype(F32)
        _split_lanes(ubuf, slice(CONF_HALO, CONF_HALO + TT), ca * sgg)
        _fill_shifted(ubuf, ushift)
        u1 = u1_ref[...]
        mu = jnp.mean(u1, axis=-1, keepdims=True)
        xc = u1 - mu
        rstd = lax.rsqrt(jnp.mean(xc * xc, axis=-1, keepdims=True) + LN_EPS)
        xhat = xc * rstd
        u2 = xhat * g_ref[...] + be_ref[...]
        su, dsu = _silu_and_grad(u2)
        sz, dsz = _silu_and_grad(cz_ref[...].astype(F32))
        dy = dy_ref[...]
        du2 = dy * dsu * sz
        dp_ref[:, 2 * dc:3 * dc] = (dy * su * dsz).astype(BF16)
        dxhat = du2 * g_ref[...]
        du1 = rstd * (dxhat - jnp.mean(dxhat, axis=-1, keepdims=True)
                      - xhat * jnp.mean(dxhat * xhat, axis=-1, keepdims=True))
        dv_ref[0, 0:1, :] = dv_ref[0, 0:1, :] + jnp.sum(du1, axis=0, keepdims=True)
        dv_ref[0, 1:2, :] = dv_ref[0, 1:2, :] + jnp.sum(du2 * xhat, axis=0, keepdims=True)
        dv_ref[0, 2:3, :] = dv_ref[0, 2:3, :] + jnp.sum(du2, axis=0, keepdims=True)
        _split_lanes(dubuf, slice(0, TT), du1)
        _fill_shifted(dubuf, dshift)

        def piece(lb, cs, r0):
            du0buf[lb, pl.ds(r0, TAP_ROWS), :] = _tree_sum(
                [w_ref[k:k + 1, cs] * _window(dubuf, dshift, CONF_K - 1 - k, r0, lb) for k in range(CONF_K)])
            d1 = dubuf[lb, pl.ds(r0, TAP_ROWS), :]
            for k in range(CONF_K):
                prod = d1 * _window(ubuf, ushift, CONF_HALO - (CONF_K - 1) + k, r0, lb)
                dwacc[lb, k] = dwacc[lb, k] + jnp.sum(prod.reshape(TAP_ROWS // SUBLANES, SUBLANES, LANES), axis=0)

        _tap_loop(nlb, piece)
        du0 = _join_lanes(du0buf)
        dp_ref[:, 0:dc] = (du0 * sgg).astype(BF16)
        dp_ref[:, dc:2 * dc] = (du0 * ca * sgg * (1.0 - sgg)).astype(BF16)

        @pl.when(i == dm.NT - 1)
        def _():
            for lb in range(nlb):
                dw_ref[0, 0:CONF_K, lb * LANES:(lb + 1) * LANES] = jnp.sum(dwacc[lb], axis=1)
            dw_ref[0, CONF_K:CONF_K + 1, :] = jnp.zeros((1, dc), F32)

    col = lambda k: pl.BlockSpec((TT, dc), lambda b, i: (ti(b, i), c0 + k))
    halo = lambda k: pl.BlockSpec((CONF_HALO, dc), lambda b, i: (hi(b, i), c0 + k))
    vec = pl.BlockSpec((1, dc), lambda b, i: (0, 0))
    return pl.pallas_call(
        body, name="mix_c_bwd", grid=(dm.BL, dm.NT),
        in_specs=[col(0), col(1), col(2), halo(0), halo(1),
                  pl.BlockSpec((TT, dc), lambda b, i: (ti(b, i), 0)),
                  pl.BlockSpec((TT, dc), lambda b, i: (ti(b, i), 0)),
                  pl.BlockSpec((CONF_K, dc), lambda b, i: (0, 0)), vec, vec],
        out_specs=[pl.BlockSpec((TT, dm.WC), lambda b, i: (ti(b, i), 0)),
                   pl.BlockSpec((1, 32, dc), lambda b, i: (b, 0, 0)),
                   pl.BlockSpec((1, 8, dc), lambda b, i: (b, 0, 0))],
        out_shape=[jax.ShapeDtypeStruct((dm.R, dm.WC), BF16),
                   jax.ShapeDtypeStruct((dm.BL, 32, dc), F32),
                   jax.ShapeDtypeStruct((dm.BL, 8, dc), F32)],
        scratch_shapes=[pltpu.VMEM((nlb, CONF_HALO + TT, LANES), F32),
                        pltpu.VMEM((nlb, TT + CONF_HALO, LANES), F32), pltpu.VMEM((nlb, TT, LANES), F32),
                        pltpu.VMEM((nlb, SUBLANES - 1, SHIFT_ROWS, LANES), F32),
                        pltpu.VMEM((nlb, SUBLANES - 1, SHIFT_ROWS, LANES), F32),
                        pltpu.VMEM((nlb, CONF_K, SUBLANES, LANES), F32)],
        compiler_params=_params(("parallel", "arbitrary")),
    )(proj, proj, proj, proj, proj, u1, dyc, conv_w, ln_g, ln_b)


RAW_HALO = 16


def _shift_matrix():
    r = jnp.arange((SSM_CONV_K - 1) * TT)[:, None]
    want = TT + r % TT - (SSM_CONV_K - 1 - r // TT)
    return (jnp.arange(2 * TT)[None, :] == want).astype(BF16)


def _ssm_conv(rawwin, sh, s_ref, w_ref, b_ref, width):
    sh[...] = _dot(s_ref[...], rawwin[...])
    for lb in range(width // LANES):
        cs = slice(lb * LANES, (lb + 1) * LANES)
        acc = b_ref[0:1, cs] + w_ref[SSM_CONV_K - 1:SSM_CONV_K, cs] * rawwin[TT:2 * TT, cs].astype(F32)
        for k in range(SSM_CONV_K - 1):
            acc = acc + w_ref[k:k + 1, cs] * sh[k * TT:(k + 1) * TT, cs]
        yield cs, acc


def _softplus(z):
    return jnp.maximum(z, 0.0) + jnp.log(1.0 + jnp.exp(-jnp.abs(z)))


def _tri(lower):
    r = lax.broadcasted_iota(jnp.int32, (TT, TT), 0)
    c = lax.broadcasted_iota(jnp.int32, (TT, TT), 1)
    return (c <= r) if lower else (c >= r)


def _exact_01_dot(mat01, x):
    x1, x2, x3 = _split3(x)
    return _dot(mat01, x1) + _dot(mat01, x2) + _dot(mat01, x3)


def _head_scalars(dt_ref, dtb_ref, alog_ref):
    z = dt_ref[...] + dtb_ref[...]
    dtv = _softplus(z)
    a = -jnp.exp(alog_ref[...])
    ac = _exact_01_dot(_tri(True).astype(F32).astype(BF16), dtv * a)
    eac = jnp.exp(ac)
    dst = jnp.exp(ac[TT - 1:TT, :] - ac)
    return z, dtv, a, ac, eac, dst


FAR_BELOW = -1e30


def _decay(ac, ac_t, h, causal):
    return jnp.exp(jnp.where(causal, ac[:, h:h + 1] - ac_t[h:h + 1, :], FAR_BELOW))


def _own_half(x16, h):
    lane = lax.broadcasted_iota(jnp.int32, (1, LANES), 1)
    keep = (lane >= SSM_HEAD_DIM) if (h % 2) else (lane < SSM_HEAD_DIM)
    return jnp.where(keep, x16, jnp.zeros_like(x16))


def _per_sequence(setup, body, bl, how):
    def all_sequences(*refs):
        views = [[r.at[b] if h is True else (r.at[pl.ds(b, 1)] if h == "keep" else r) for r, h in zip(refs, how)]
                 for b in range(bl)]
        for v in views:
            setup(*v)
        running = [body(*v) for v in views]
        while running:
            running = [g for g in running if next(g, "done") != "done"]

    return all_sequences


def _mix_b_fwd(proj, projdt, conv_w, conv_b, dt_bias, a_log, dskx, norm_g, expand, dm, rider=None):
    db, gn, xbc_w, hpg = dm.DB, dm.GN, dm.XBC, dm.HPG
    gw = db // SSM_GROUPS

    def setup(bz_ref, bx_ref, bc_ref, dt_ref, w_ref, b_ref, dtb_ref, alog_ref, dsk_ref, g_ref, e_ref, s_ref,
              y_ref, yraw_ref, sprev_ref, rawwin, sh, xbuf, state, ybuf, exbuf, xdtbuf):
        i = pl.program_id(0)

        @pl.when(i == 0)
        def _():
            rawwin[0:TT, :] = jnp.zeros((TT, xbc_w), BF16)
            state[...] = jnp.zeros_like(state)

        @pl.when(i > 0)
        def _():
            rawwin[TT - RAW_HALO:TT, :] = rawwin[2 * TT - RAW_HALO:2 * TT, :]

    def body(bz_ref, bx_ref, bc_ref, dt_ref, w_ref, b_ref, dtb_ref, alog_ref, dsk_ref, g_ref, e_ref, s_ref,
             y_ref, yraw_ref, sprev_ref, rawwin, sh, xbuf, state, ybuf, exbuf, xdtbuf):
        rawwin[TT:2 * TT, 0:db] = bx_ref[...]
        rawwin[TT:2 * TT, db:xbc_w] = bc_ref[...]
        for cs, pre in _ssm_conv(rawwin, sh, s_ref, w_ref, b_ref, xbc_w):
            xbuf[:, cs] = pre * _sigmoid(pre)
            yield

        _, dtv, _, ac, eac, dst = _head_scalars(dt_ref, dtb_ref, alog_ref)
        exbuf[...] = _dot(jnp.concatenate([dtv, eac, dst], axis=0).astype(BF16), e_ref[...])
        ac_t = ac.T
        causal = _tri(True)
        sprev_ref[0, 0] = state[...]
        yield

        xdtbuf[...] = xbuf[:, 0:db] * exbuf[0:TT, :]
        ybuf[...] = xbuf[:, 0:db] * dsk_ref[...]
        for g in range(SSM_GROUPS):
            gs = slice(g * gw, (g + 1) * gw)
            bg = xbuf[:, db + g * SSM_STATE:db + (g + 1) * SSM_STATE].astype(BF16)
            cg = xbuf[:, db + gn + g * SSM_STATE:db + gn + (g + 1) * SSM_STATE].astype(BF16)
            cb = _dot_nt(cg, bg)
            for e in range(0, hpg, 2):
                h = g * hpg + e
                ps = slice(h * SSM_HEAD_DIM, (h + 2) * SSM_HEAD_DIM)
                xp16 = xdtbuf[:, ps].astype(BF16)
                acc = jnp.zeros((TT, LANES), F32)
                for hh in (h, h + 1):
                    mm = (cb * _decay(ac, ac_t, hh, causal)).astype(BF16)
                    acc = acc + _dot(mm, _own_half(xp16, hh))
                ybuf[:, ps] = ybuf[:, ps] + acc
                yield
            sg = state[:, gs]
            ybuf[:, gs] = ybuf[:, gs] + exbuf[TT:2 * TT, gs] * _dot(cg, sg.astype(BF16))
            state[:, gs] = sg * exbuf[2 * TT - 1:2 * TT, gs] + _dot_tn(
                bg, (xdtbuf[:, gs] * exbuf[2 * TT:3 * TT, gs]).astype(BF16))
            yield

        yraw = ybuf[...]
        yraw_ref[...] = yraw
        bz = bz_ref[...].astype(F32)
        v = yraw * (bz * _sigmoid(bz))
        r = lax.rsqrt(jnp.mean(v * v, axis=-1, keepdims=True) + NORM_EPS)
        y_ref[...] = (v * r * g_ref[...]).astype(BF16)

    bl = dm.BL
    tile = lambda w, k: pl.BlockSpec((bl, TT, w), lambda i: (0, i, k))
    fixed = lambda r, w: pl.BlockSpec((r, w), lambda i: (0, 0))
    proj3, dt3 = proj.reshape(bl, dm.Lp, dm.NP), projdt.reshape(bl, dm.Lp, DT_PAD)
    scratch = [((2 * TT, xbc_w), BF16), (((SSM_CONV_K - 1) * TT, xbc_w), F32), ((TT, xbc_w), F32),
               ((SSM_STATE, db), F32), ((TT, db), F32), ((3 * TT, db), F32), ((TT, db), F32)]
    (y, yraw, sprev), rode = _call(
        _per_sequence(setup, body, bl, [True] * 4 + [False] * 8 + [True, True, "keep"] + [True] * len(scratch)),
        "mix_b_fwd", (dm.NT,),
        [tile(db, dm.WA // db), tile(db, dm.WA // db + 1), tile(2 * gn, (dm.WA + 2 * db) // (2 * gn)),
         tile(DT_PAD, 0),
         fixed(SSM_CONV_K, xbc_w), fixed(1, xbc_w), fixed(1, DT_PAD), fixed(1, DT_PAD),
         fixed(1, db), fixed(1, db), fixed(DT_PAD, db), fixed((SSM_CONV_K - 1) * TT, 2 * TT)],
        [tile(db, 0), tile(db, 0), pl.BlockSpec((bl, 1, SSM_STATE, db), lambda i: (0, i, 0, 0))],
        [jax.ShapeDtypeStruct((bl, dm.Lp, db), BF16), jax.ShapeDtypeStruct((bl, dm.Lp, db), F32),
         jax.ShapeDtypeStruct((bl, dm.NT, SSM_STATE, db), F32)],
        [pltpu.VMEM((bl,) + s, t) for s, t in scratch],
        ("arbitrary",),
        (proj3, proj3, proj3, dt3, conv_w, conv_b, dt_bias, a_log, dskx, norm_g, expand, _shift_matrix()), rider)
    return (y.reshape(dm.R, db), yraw.reshape(dm.R, db), sprev), rode


def _mix_b_bwd(proj, projdt, dyb, yraw, sprev, conv_w, conv_b, dt_bias, a_log, dskx, norm_g, expand, expand_t, dm,
               rider=None):
    db, gn, xbc_w, hpg = dm.DB, dm.GN, dm.XBC, dm.HPG
    gw = db // SSM_GROUPS

    def setup(bz_ref, bx_ref, bc_ref, dt_ref, bxh_ref, bch_ref, dy_ref, yraw_ref, sprev_ref,
              w_ref, b_ref, dtb_ref, alog_ref, dsk_ref, g_ref, e_ref, et_ref, s_ref,
              dp_ref, dpt_ref, dwc_ref, dch_ref, dhd_ref,
              rawwin, sh, xbuf, dsbuf, dstate, dxbuf, z1buf, dprebuf, exbuf, xdtbuf, dyrbuf, uvec):
        i = pl.program_id(0)

        @pl.when(i == 0)
        def _():
            dwc_ref[...] = jnp.zeros_like(dwc_ref)
            dch_ref[...] = jnp.zeros_like(dch_ref)
            dhd_ref[...] = jnp.zeros_like(dhd_ref)
            dstate[...] = jnp.zeros_like(dstate)
            dprebuf[TT:TT + SMALL_HALO, :] = jnp.zeros((SMALL_HALO, xbc_w), F32)
            rawwin[0:TT - RAW_HALO, :] = jnp.zeros((TT - RAW_HALO, xbc_w), BF16)

        @pl.when(i > 0)
        def _():
            dprebuf[TT:TT + SMALL_HALO, :] = dprebuf[0:SMALL_HALO, :]

    def body(bz_ref, bx_ref, bc_ref, dt_ref, bxh_ref, bch_ref, dy_ref, yraw_ref, sprev_ref,
             w_ref, b_ref, dtb_ref, alog_ref, dsk_ref, g_ref, e_ref, et_ref, s_ref,
             dp_ref, dpt_ref, dwc_ref, dch_ref, dhd_ref,
             rawwin, sh, xbuf, dsbuf, dstate, dxbuf, z1buf, dprebuf, exbuf, xdtbuf, dyrbuf, uvec):
        halo_on = jnp.where(pl.program_id(0) == dm.NT - 1, 0.0, 1.0).astype(BF16)

        rawwin[TT - RAW_HALO:TT, 0:db] = bxh_ref[...] * halo_on
        rawwin[TT - RAW_HALO:TT, db:xbc_w] = bch_ref[...] * halo_on
        rawwin[TT:2 * TT, 0:db] = bx_ref[...]
        rawwin[TT:2 * TT, db:xbc_w] = bc_ref[...]
        for cs, pre in _ssm_conv(rawwin, sh, s_ref, w_ref, b_ref, xbc_w):
            sl, dsl = _silu_and_grad(pre)
            xbuf[:, cs] = sl
            dsbuf[:, cs] = dsl
            yield

        z, dtv, a, ac, eac, dst = _head_scalars(dt_ref, dtb_ref, alog_ref)
        exbuf[...] = _dot(jnp.concatenate([dtv, eac, dst], axis=0).astype(BF16), e_ref[...])
        ac_t = ac.T
        causal = _tri(True)
        xdtbuf[...] = xbuf[:, 0:db] * exbuf[0:TT, :]

        yraw = yraw_ref[...]
        sz, dsz = _silu_and_grad(bz_ref[...].astype(F32))
        v = yraw * sz
        r = lax.rsqrt(jnp.mean(v * v, axis=-1, keepdims=True) + NORM_EPS)
        dy = dy_ref[...]
        dyg = dy * g_ref[...]
        dv = r * dyg - v * (r * r * r * jnp.mean(dyg * v, axis=-1, keepdims=True))
        dch_ref[0, 0:1, :] = dch_ref[0, 0:1, :] + jnp.sum(dy * v * r, axis=0, keepdims=True)
        dyr = dv * sz
        dyrbuf[...] = dyr
        dp_ref[:, 0:db] = (dv * yraw * dsz).astype(BF16)
        dch_ref[0, 1:2, :] = dch_ref[0, 1:2, :] + jnp.sum(dyr * xbuf[:, 0:db], axis=0, keepdims=True)

        lane_row = lax.broadcasted_iota(jnp.int32, (1, LANES), 1)
        sub_col = lax.broadcasted_iota(jnp.int32, (LANES, 1), 0)
        dac = jnp.zeros((TT, LANES), F32)
        colacc = jnp.zeros((LANES, TT), F32)
        for g in range(SSM_GROUPS):
            gs = slice(g * gw, (g + 1) * gw)
            bs_ = slice(db + g * SSM_STATE, db + (g + 1) * SSM_STATE)
            cs_ = slice(db + gn + g * SSM_STATE, db + gn + (g + 1) * SSM_STATE)
            bg = xbuf[:, bs_].astype(BF16)
            cg = xbuf[:, cs_].astype(BF16)
            cb = _dot_nt(cg, bg)
            dcb = jnp.zeros((TT, TT), F32)
            for e in range(0, hpg, 2):
                h = g * hpg + e
                ps = slice(h * SSM_HEAD_DIM, (h + 2) * SSM_HEAD_DIM)
                xp16 = xdtbuf[:, ps].astype(BF16)
                dyp16 = dyrbuf[:, ps].astype(BF16)
                acc = jnp.zeros((TT, LANES), F32)
                for hh in (h, h + 1):
                    dec = _decay(ac, ac_t, hh, causal)
                    mm = cb * dec
                    dyh = _own_half(dyp16, hh)
                    dmm = _dot_nt(dyh, xp16)
                    acc = acc + _dot_tn(mm.astype(BF16), dyh)
                    dcb = dcb + dmm * dec
                    gm = dmm * mm
                    dac = dac + jnp.sum(gm, axis=1, keepdims=True) * (lane_row == hh).astype(F32)
                    colacc = colacc + (sub_col == hh).astype(F32) * jnp.sum(gm, axis=0, keepdims=True)
                dxbuf[:, ps] = acc
                yield
            sg32 = sprev_ref[0, 0, :, gs]
            sg = sg32.astype(BF16)
            dsn = dstate[:, gs]
            dsn16 = dsn.astype(BF16)
            dcb16 = dcb.astype(BF16)
            eacx = exbuf[TT:2 * TT, gs]
            dstx = exbuf[2 * TT:3 * TT, gs]
            cdx = exbuf[2 * TT - 1:2 * TT, gs]
            dye16 = (dyrbuf[:, gs] * eacx).astype(BF16)
            xdt_g = xdtbuf[:, gs]
            dxbuf[:, cs_] = _dot(dcb16, bg) + _dot_nt(dye16, sg)
            dst_x = dstx * _dot(bg, dsn16)
            dxbuf[:, bs_] = _dot_tn(dcb16, cg) + _dot_nt((dstx * xdt_g).astype(BF16), dsn16)
            dstate[:, gs] = cdx * dsn + _dot_tn(cg, dye16)
            z1buf[:, gs] = dyrbuf[:, gs] * (eacx * _dot(cg, sg)) - xdt_g * dst_x
            uvec[:, gs] = jnp.broadcast_to(
                jnp.sum(xdt_g * dst_x, axis=0, keepdims=True) + jnp.sum(dsn * cdx * sg32, axis=0, keepdims=True),
                (8, gw))
            dxbuf[:, gs] = dxbuf[:, gs] + dst_x
            yield

        zz = _dot(jnp.concatenate([z1buf[...], dxbuf[:, 0:db] * xbuf[:, 0:db]], axis=0).astype(BF16), et_ref[...])
        u1, u2, u3 = _split3(uvec[...])
        ulast = (_dot(u1, et_ref[...]) + _dot(u2, et_ref[...]) + _dot(u3, et_ref[...]))[0:1, :]
        is_last = (lax.broadcasted_iota(jnp.int32, (TT, 1), 0) == TT - 1).astype(F32)
        dac = dac - colacc.T + zz[0:TT] + is_last * ulast
        dda = _exact_01_dot(_tri(False).astype(F32).astype(BF16), dac)
        ddt = dda * a + zz[TT:2 * TT]
        dhd_ref[0, 1:2, :] = dhd_ref[0, 1:2, :] + jnp.sum(dda * dtv, axis=0, keepdims=True) * a
        ddtraw = ddt * _sigmoid(z)
        dhd_ref[0, 0:1, :] = dhd_ref[0, 0:1, :] + jnp.sum(ddtraw, axis=0, keepdims=True)
        dpt_ref[...] = ddtraw.astype(BF16)
        dxbuf[:, 0:db] = dyrbuf[...] * dsk_ref[...] + dxbuf[:, 0:db] * exbuf[0:TT, :]

        for lb in range(xbc_w // LANES):
            cs = slice(lb * LANES, (lb + 1) * LANES)
            dpre = dxbuf[:, cs] * dsbuf[:, cs]
            dprebuf[0:TT, cs] = dpre
            dwc_ref[0, SSM_CONV_K:SSM_CONV_K + 1, cs] = dwc_ref[0, SSM_CONV_K:SSM_CONV_K + 1, cs] + jnp.sum(
                dpre, axis=0, keepdims=True)
            draw = w_ref[SSM_CONV_K - 1:SSM_CONV_K, cs] * dpre
            for k in range(SSM_CONV_K - 1):
                ahead = SSM_CONV_K - 1 - k
                draw = draw + w_ref[k:k + 1, cs] * dprebuf[ahead:ahead + TT, cs]
            for k in range(SSM_CONV_K):
                moved = sh[k * TT:(k + 1) * TT, cs] if k < SSM_CONV_K - 1 else rawwin[TT:2 * TT, cs].astype(F32)
                dwc_ref[0, k:k + 1, cs] = dwc_ref[0, k:k + 1, cs] + jnp.sum(dpre * moved, axis=0, keepdims=True)
            dp_ref[:, db + lb * LANES:db + (lb + 1) * LANES] = draw.astype(BF16)
            yield

    bl, nt = dm.BL, dm.NT
    tile = lambda w, k: pl.BlockSpec((bl, TT, w), lambda i: (0, nt - 1 - i, k))
    halo = lambda w, k: pl.BlockSpec((bl, HALO_BLOCK, w),
                                     lambda i: (0, jnp.maximum((nt - 1 - i) * (TT // HALO_BLOCK) - 1, 0), k))
    fixed = lambda r, w: pl.BlockSpec((r, w), lambda i: (0, 0))
    sums = lambda w: pl.BlockSpec((bl, 8, w), lambda i: (0, 0, 0))
    kz = dm.WA // db
    kc = (dm.WA + 2 * db) // (2 * gn)
    proj3, dt3 = proj.reshape(bl, dm.Lp, dm.NP), projdt.reshape(bl, dm.Lp, DT_PAD)
    scratch = [((2 * TT, xbc_w), BF16), (((SSM_CONV_K - 1) * TT, xbc_w), F32), ((TT, xbc_w), F32),
               ((TT, xbc_w), F32), ((SSM_STATE, db), F32), ((TT, xbc_w), F32), ((TT, db), F32),
               ((TT + SMALL_HALO, xbc_w), F32), ((3 * TT, db), F32), ((TT, db), F32), ((TT, db), F32), ((8, db), F32)]
    how = [True] * 8 + ["keep"] + [False] * 9 + [True, True, "keep", "keep", "keep"] + [True] * len(scratch)
    (dp, dpt, dwc, dch, dhd), rode = _call(
        _per_sequence(setup, body, bl, how), "mix_b_bwd", (nt,),
        [tile(db, kz), tile(db, kz + 1), tile(2 * gn, kc), tile(DT_PAD, 0),
         halo(db, kz + 1), halo(2 * gn, kc), tile(db, 0), tile(db, 0),
         pl.BlockSpec((bl, 1, SSM_STATE, db), lambda i: (0, nt - 1 - i, 0, 0)),
         fixed(SSM_CONV_K, xbc_w), fixed(1, xbc_w), fixed(1, DT_PAD), fixed(1, DT_PAD),
         fixed(1, db), fixed(1, db), fixed(DT_PAD, db), fixed(db, DT_PAD), fixed((SSM_CONV_K - 1) * TT, 2 * TT)],
        [tile(dm.WB, 0), tile(DT_PAD, 0), sums(xbc_w), sums(db), sums(DT_PAD)],
        [jax.ShapeDtypeStruct((bl, dm.Lp, dm.WB), BF16), jax.ShapeDtypeStruct((bl, dm.Lp, DT_PAD), BF16),
         jax.ShapeDtypeStruct((bl, 8, xbc_w), F32), jax.ShapeDtypeStruct((bl, 8, db), F32),
         jax.ShapeDtypeStruct((bl, 8, DT_PAD), F32)],
        [pltpu.VMEM((bl,) + s, t) for s, t in scratch],
        ("arbitrary",),
        (proj3, proj3, proj3, dt3, proj3, proj3, dyb.reshape(bl, dm.Lp, db), yraw.reshape(bl, dm.Lp, db), sprev,
         conv_w, conv_b, dt_bias, a_log, dskx, norm_g, expand, expand_t, _shift_matrix()), rider)
    return (dp.reshape(dm.R, dm.WB), dpt.reshape(dm.R, DT_PAD), dwc, dch, dhd), rode


def _head_consts(dm):
    head_of = jnp.arange(dm.DB) // SSM_HEAD_DIM
    expand = (jnp.arange(DT_PAD)[:, None] == head_of[None, :]).astype(BF16)
    return expand, expand.T


def _ssm_params(lw, dm):
    pad_h = lambda v: jnp.pad(v, (0, DT_PAD - dm.H))[None]
    return (lw["ssm_conv_w"], lw["ssm_conv_b"][None], pad_h(lw["dt_bias"]), pad_h(lw["a_log"]),
            jnp.repeat(lw["d_skip"], SSM_HEAD_DIM)[None], lw["ssm_norm_g"][None])


def _layer_fwd(h, lw, w_in, w_out, cst, dm, next_bases=None):
    nxt = next_bases is not None
    (proj, projdt, hn), got = _fwd_in(h, lw["pre_g"][None], w_in, dm,
                                      _ride_gather_ici(next_bases, 0, 2) if nxt else None)
    ya = _mix_a_fwd(proj, lw["conv_a_w"], dm)
    (yb, yraw, sprev), got = _mix_b_fwd(proj, projdt, *_ssm_params(lw, dm), cst[0], dm,
                                        _ride_gather_ici(got, 1, 2) if nxt else None)
    yc, u1 = _mix_c_fwd(proj, lw["conf_conv_w"], lw["conf_conv_b"][None], lw["conf_ln_g"][None],
                        lw["conf_ln_b"][None], dm)
    (h_new, m), got = _fwd_out(ya, yb, yc, w_out, h, lw["post_g"][None], dm, _ride_gather_d2d(got) if nxt else None)
    return h_new, (h, hn, proj, projdt, ya, yb, yc, u1, yraw, sprev, m), got


def _layer_bwd(dh, saved, lw, w_in, w_out, cst, dm, reduce=None, last=False):
    h_in, hn, proj, projdt, ya, yb, yc, u1, yraw, sprev, m = saved
    (dya, dyb, dyc, dwo, dpost), got = _bwd_out(dh, m, lw["post_g"][None], w_out, ya, yb, yc, dm,
                                                None if reduce is None else reduce.swap())
    dpa, dwa = _mix_a_bwd(proj, dya, lw["conv_a_w"], dm)
    (dpb, dpt, dwcv, dch, dhd), got = _mix_b_bwd(proj, projdt, dyb, yraw, sprev, *_ssm_params(lw, dm), cst[0],
                                                 cst[1], dm, None if reduce is None else reduce.to_owners(got))
    dpc, dwcf, dvc = _mix_c_bwd(proj, u1, dyc, lw["conf_conv_w"], lw["conf_ln_g"][None], lw["conf_ln_b"][None], dm)
    def own_reduce():
        pieces = [_bwd_in_dw(hn, dp, dm, n) for dp, n in ((dpa, "a"), (dpb, "b"), (dpc, "c"), (dpt, "dt"))]
        return _GradReduce([_grad_to_shards(pieces, dm), dwo.reshape(N_CHIPS, 2 * dm.D // N_CHIPS, dm.D)])

    rider = None if reduce is None else reduce.join(got)
    n_join = 0 if rider is None else len(rider.out_shapes)
    if last:
        mine = own_reduce()
        to_owners = mine.to_owners(_exchange("grad_swap_halves", mine.swap()))
        rider = to_owners if rider is None else _ride_both(rider, to_owners)
    (dh, dpre), got = _bwd_in_dx(dpa, dpb, dpc, dpt, w_in, h_in, dh, lw["pre_g"][None], dm, rider)
    if reduce is not None:
        reduce.finish(got[:n_join])
    if last:
        mine.finish(_exchange("grad_join_halves", mine.join(got[n_join:])))
    else:
        mine = own_reduce()
    dwcv, dch, dhd, dvc = (jnp.sum(a, axis=0) for a in (dwcv, dch, dhd, dvc))
    small = dict(pre_g=dpre[0], post_g=dpost[0], conv_a_w=jnp.sum(dwa, axis=0)[:CONV_A_K],
                 ssm_conv_w=dwcv[:SSM_CONV_K], ssm_conv_b=dwcv[SSM_CONV_K], ssm_norm_g=dch[0],
                 d_skip=jnp.sum(dch[1].reshape(dm.H, SSM_HEAD_DIM), axis=1), dt_bias=dhd[0, :dm.H],
                 a_log=dhd[1, :dm.H], conf_conv_w=jnp.sum(dwcf, axis=0)[:CONF_K], conf_conv_b=dvc[0],
                 conf_ln_g=dvc[1], conf_ln_b=dvc[2])
    return dh, mine, small


def _shard_runs(dm):
    ab = dm.WA + dm.WB
    order = [(0, 0, ab), (ab, dm.NP - DT_PAD, dm.H), (ab + dm.H, ab, dm.WC)]
    k = dm.NIN // N_CHIPS
    runs = []
    for s in range(N_CHIPS):
        for o0, m0, wd in order:
            lo, hi = max(o0, s * k), min(o0 + wd, (s + 1) * k)
            if lo < hi:
                runs.append((s, lo - s * k, m0 + lo - o0, hi - lo))
    return runs


def _w_in_from_shards(base, dm):
    tr = _row_tile(dm.D, 256)
    k = dm.NIN // N_CHIPS
    runs = _shard_runs(dm)

    def body(in_ref, out_ref):
        for s, sc, mc, wd in runs:
            out_ref[:, mc:mc + wd] = in_ref[s, :, sc:sc + wd]
        out_ref[:, dm.NP - DT_PAD + dm.H:dm.NP] = jnp.zeros((tr, DT_PAD - dm.H), BF16)

    return pl.pallas_call(
        body, name="w_in_from_shards", grid=(dm.D // tr,),
        in_specs=[pl.BlockSpec((N_CHIPS, tr, k), lambda r: (0, r, 0))],
        out_specs=pl.BlockSpec((tr, dm.NP), lambda r: (r, 0)),
        out_shape=jax.ShapeDtypeStruct((dm.D, dm.NP), BF16),
        compiler_params=_params(("parallel",)),
    )(base)


def _grad_to_shards(pieces, dm):
    tr = _row_tile(dm.D, 256)
    k = dm.NIN // N_CHIPS
    starts = [0, dm.WA, dm.WA + dm.WB, dm.NP - DT_PAD]
    widths = [dm.WA, dm.WB, dm.WC, DT_PAD]
    runs = _shard_runs(dm)

    def body(a_ref, b_ref, c_ref, t_ref, out_ref):
        refs = (a_ref, b_ref, c_ref, t_ref)
        for s, sc, mc, wd in runs:
            for p in range(4):
                lo, hi = max(mc, starts[p]), min(mc + wd, starts[p] + widths[p])
                if lo < hi:
                    out_ref[s, :, sc + lo - mc:sc + hi - mc] = refs[p][:, lo - starts[p]:hi - starts[p]].astype(BF16)

    return pl.pallas_call(
        body, name="grad_to_shards", grid=(dm.D // tr,),
        in_specs=[pl.BlockSpec((tr, w), lambda r: (r, 0)) for w in widths],
        out_specs=pl.BlockSpec((N_CHIPS, tr, k), lambda r: (0, r, 0)),
        out_shape=jax.ShapeDtypeStruct((N_CHIPS, dm.D, k), BF16),
        compiler_params=_params(("parallel",)),
    )(*pieces)


def _place_own(w, layer, me):
    _, rows, cols = w.shape
    tr = _row_tile(rows, 256)

    def body(me_ref, w_ref, out_ref):
        out_ref[0] = w_ref[0].astype(BF16)

    return pl.pallas_call(
        body, name="place_own",
        grid_spec=pltpu.PrefetchScalarGridSpec(
            num_scalar_prefetch=1, grid=(rows // tr,),
            in_specs=[pl.BlockSpec((1, tr, cols), lambda r, me_ref: (layer, r, 0))],
            out_specs=pl.BlockSpec((1, tr, cols), lambda r, me_ref: (me_ref[0], r, 0))),
        out_shape=jax.ShapeDtypeStruct((N_CHIPS, rows, cols), BF16),
        compiler_params=_params(("parallel",)),
    )(me, w)


def _add_halves(g, got, c, name):
    _, _, rows, cols = g.shape
    tr = _row_tile(rows, 256)

    def body(c_ref, g_ref, got_ref, out_ref):
        out_ref[0] = (g_ref[0, 0].astype(F32) + got_ref[0].astype(F32)).astype(BF16)

    return pl.pallas_call(
        body, name=name,
        grid_spec=pltpu.PrefetchScalarGridSpec(
            num_scalar_prefetch=1, grid=(N_CHIPS, rows // tr),
            in_specs=[pl.BlockSpec((1, 1, tr, cols), lambda s, r, c_ref: (s, c_ref[0], r, 0)),
                      pl.BlockSpec((1, tr, cols), lambda s, r, c_ref: (s, r, 0))],
            out_specs=pl.BlockSpec((1, tr, cols), lambda s, r, c_ref: (s, r, 0))),
        out_shape=jax.ShapeDtypeStruct((N_CHIPS, rows, cols), BF16),
        compiler_params=_params(("parallel", "parallel")),
    )(c, g, got)


def _add_owner(p, got, where, name):
    _, rows, cols = p.shape
    tr = _row_tile(rows, 256)

    def body(w_ref, p_ref, got_ref, out_ref):
        acc = p_ref[0].astype(F32)
        for j in range(3):
            acc = acc + got_ref[j].astype(F32)
        out_ref[0] = acc

    return pl.pallas_call(
        body, name=name,
        grid_spec=pltpu.PrefetchScalarGridSpec(
            num_scalar_prefetch=1, grid=(rows // tr,),
            in_specs=[pl.BlockSpec((1, tr, cols), lambda r, w_ref: (w_ref[0], r, 0)),
                      pl.BlockSpec((3, tr, cols), lambda r, w_ref: (0, r, 0))],
            out_specs=pl.BlockSpec((1, tr, cols), lambda r, w_ref: (w_ref[1], r, 0))),
        out_shape=jax.ShapeDtypeStruct((2, rows, cols), F32),
        compiler_params=_params(("parallel",)),
    )(where, p, got)


class _GradReduce:
    def __init__(self, gs):
        self.gs = [g.reshape((N_CHIPS, 2, g.shape[1] // 2) + g.shape[2:]) for g in gs]
        self.c = lax.axis_index("c").astype(jnp.int32).reshape(1)
        chip = (2 * lax.axis_index("x") + lax.axis_index("y")).astype(jnp.int32)
        self.where = jnp.stack([chip, self.c[0]])
        self.result = None

    def swap(self):
        return _ride_swap_halves(self.gs)

    def to_owners(self, got):
        self.ps = [_add_halves(g, r, self.c, "grad_add_sibling_" + n) for g, r, n in zip(self.gs, got, ("in", "out"))]
        return _ride_to_owners(self.ps)

    def join(self, got):
        qs = [_add_owner(p, r, self.where, "grad_add_chips_" + n) for p, r, n in zip(self.ps, got, ("in", "out"))]
        return _ride_join_halves(qs)

    def finish(self, got):
        self.result = [a.reshape((a.shape[0] * a.shape[1],) + a.shape[2:]) for a in got]


def _adamw_math(w, g, m, v):
    m = ADAM_B1 * m + (1.0 - ADAM_B1) * g
    v = ADAM_B2 * v + (1.0 - ADAM_B2) * (g * g)
    m_hat = m / (1.0 - ADAM_B1 ** ADAM_STEP)
    v_hat = v / (1.0 - ADAM_B2 ** ADAM_STEP)
    delta = -ADAM_LR * (m_hat / (jnp.sqrt(v_hat) + ADAM_EPS) + ADAM_WD * w)
    return delta, m, v


def _adamw_small(w, g, m, v, name):
    def body(w_ref, g_ref, m_ref, v_ref, d_out, m_out, v_out):
        d_out[...], m_out[...], v_out[...] = _adamw_math(w_ref[...], g_ref[...], m_ref[...], v_ref[...])

    shape = jax.ShapeDtypeStruct(w.shape, F32)
    return pl.pallas_call(body, name="adamw_" + name, out_shape=[shape, shape, shape],
                          compiler_params=_params())(w, g, m, v)


def _adamw_layer(i, w, g, m, v, prev, name):
    depth, rows, cols = w.shape
    tr = _row_tile(rows, 256)
    n_prev = 0 if prev is None else 4

    def body(*refs):
        w_ref, g_ref, m_ref, v_ref = refs[:4]
        g_out, d_out, m_out, v_out = refs[4 + n_prev:]
        gv = g_ref[...]
        g_out[0] = gv
        d_out[0], m_out[0], v_out[0] = _adamw_math(w_ref[0], gv, m_ref[0], v_ref[0])

    lay = pl.BlockSpec((1, tr, cols), lambda r: (i, r, 0))
    shape = jax.ShapeDtypeStruct(w.shape, F32)
    return pl.pallas_call(
        body, name="adamw_" + name, grid=(rows // tr,),
        in_specs=[lay, pl.BlockSpec((tr, cols), lambda r: (r, 0)), lay, lay] + [ANY] * n_prev,
        out_specs=[lay] * 4, out_shape=[shape] * 4,
        input_output_aliases={4 + k: k for k in range(n_prev)},
        compiler_params=_params(("parallel",)),
    )(w, g, m, v, *(prev or ()))


def _adamw_cols_major(w, gs, m, v, name):
    depth, rows, cols = w.shape
    tr = max(t for t in range(1, 129) if cols % t == 0)
    wt, mt, vt = (jnp.transpose(a, (2, 0, 1)) for a in (w, m, v))
    gt = jnp.stack([g.T for g in gs], axis=1)

    def body(w_ref, g_ref, m_ref, v_ref, g_out, d_out, m_out, v_out):
        gv = g_ref[...]
        g_out[...] = gv
        d_out[...], m_out[...], v_out[...] = _adamw_math(w_ref[...], gv, m_ref[...], v_ref[...])

    spec = pl.BlockSpec((tr, depth, rows), lambda r: (r, 0, 0))
    shape = jax.ShapeDtypeStruct((cols, depth, rows), F32)
    outs = pl.pallas_call(body, name="adamw_" + name, grid=(cols // tr,), in_specs=[spec] * 4, out_specs=[spec] * 4,
                          out_shape=[shape] * 4, compiler_params=_params(("parallel",)))(wt, gt, mt, vt)
    return [jnp.transpose(a, (1, 2, 0)) for a in outs]


def _sum_leading(buf, name):
    n, rows, cols = buf.shape
    tr = _row_tile(rows, rows)

    def body(in_ref, out_ref):
        acc = in_ref[0]
        for k in range(1, n):
            acc = acc + in_ref[k]
        out_ref[...] = acc

    return pl.pallas_call(
        body, name=name, grid=(rows // tr,),
        in_specs=[pl.BlockSpec((n, tr, cols), lambda i: (0, i, 0))],
        out_specs=pl.BlockSpec((tr, cols), lambda i: (i, 0)),
        out_shape=jax.ShapeDtypeStruct((rows, cols), F32),
        compiler_params=_params(("parallel",)),
    )(buf)


_SHARDED_SMALL = ("meta", "conv_a_w", "ssm_conv_w", "conf_conv_w")
_LAYER_SMALL = ("pre_g", "post_g", "conv_a_w", "ssm_conv_w", "ssm_conv_b", "dt_bias", "a_log", "d_skip",
                "ssm_norm_g", "conf_conv_w", "conf_conv_b", "conf_ln_g", "conf_ln_b")
_WEIGHTS = ("meta", "pre_g", "post_g", "w_in", "w_out", "conv_a_w", "ssm_conv_w", "ssm_conv_b", "dt_bias", "a_log",
            "d_skip", "ssm_norm_g", "conf_conv_w", "conf_conv_b", "conf_ln_g", "conf_ln_b")


def _shard_last(a):
    return jnp.moveaxis(a.reshape(a.shape[:-1] + (N_CHIPS, a.shape[-1] // N_CHIPS)), -2, 0)


def _with_own_block(a, n, at):
    return lax.dynamic_update_index_in_dim(jnp.zeros((n,) + a.shape, a.dtype), a, at, 0)


def _with_own_columns(a, chip):
    k = a.shape[-1]
    return lax.dynamic_update_slice_in_dim(jnp.zeros(a.shape[:-1] + (N_CHIPS * k,), a.dtype), a, chip * k, a.ndim - 1)


def kernel(x, meta, pre_g, post_g, w_in, w_out, conv_a_w, ssm_conv_w, ssm_conv_b, dt_bias, a_log, d_skip, ssm_norm_g, conf_conv_w, conf_conv_b, conf_ln_g, conf_ln_b, loss_target, m_meta, m_pre_g, m_post_g, m_w_in, m_w_out, m_conv_a_w, m_ssm_conv_w, m_ssm_conv_b, m_dt_bias, m_a_log, m_d_skip, m_ssm_norm_g, m_conf_conv_w, m_conf_conv_b, m_conf_ln_g, m_conf_ln_b, v_meta, v_pre_g, v_post_g, v_w_in, v_w_out, v_conv_a_w, v_ssm_conv_w, v_ssm_conv_b, v_dt_bias, v_a_log, v_d_skip, v_ssm_norm_g, v_conf_conv_w, v_conf_conv_b, v_conf_ln_g, v_conf_ln_b):
    w = dict(meta=meta, pre_g=pre_g, post_g=post_g, w_in=w_in, w_out=w_out, conv_a_w=conv_a_w,
             ssm_conv_w=ssm_conv_w, ssm_conv_b=ssm_conv_b, dt_bias=dt_bias, a_log=a_log, d_skip=d_skip,
             ssm_norm_g=ssm_norm_g, conf_conv_w=conf_conv_w, conf_conv_b=conf_conv_b, conf_ln_g=conf_ln_g,
             conf_ln_b=conf_ln_b)
    mom = dict(meta=m_meta, pre_g=m_pre_g, post_g=m_post_g, w_in=m_w_in, w_out=m_w_out, conv_a_w=m_conv_a_w,
               ssm_conv_w=m_ssm_conv_w, ssm_conv_b=m_ssm_conv_b, dt_bias=m_dt_bias, a_log=m_a_log, d_skip=m_d_skip,
               ssm_norm_g=m_ssm_norm_g, conf_conv_w=m_conf_conv_w, conf_conv_b=m_conf_conv_b,
               conf_ln_g=m_conf_ln_g, conf_ln_b=m_conf_ln_b)
    vel = dict(meta=v_meta, pre_g=v_pre_g, post_g=v_post_g, w_in=v_w_in, w_out=v_w_out, conv_a_w=v_conv_a_w,
               ssm_conv_w=v_ssm_conv_w, ssm_conv_b=v_ssm_conv_b, dt_bias=v_dt_bias, a_log=v_a_log, d_skip=v_d_skip,
               ssm_norm_g=v_ssm_norm_g, conf_conv_w=v_conf_conv_w, conf_conv_b=v_conf_conv_b,
               conf_ln_g=v_conf_ln_g, conf_ln_b=v_conf_ln_b)
    bl, seq, d = x.shape
    dm = Dims(bl, seq, d)
    depth = w_in.shape[0]
    chip = (2 * lax.axis_index("x") + lax.axis_index("y")).astype(jnp.int32)
    dev = 2 * chip + lax.axis_index("c").astype(jnp.int32)
    cst = _head_consts(dm)

    full = dict(w)
    full.update(zip(_SHARDED_SMALL, _exchange("gather_small_weights", _ride_gather_small(
        [_with_own_columns(w[n], chip) for n in _SHARDED_SMALL]))))

    bases = [[_place_own(w_in, i, chip.reshape(1)), _place_own(w_out, i, chip.reshape(1))] for i in range(depth)]
    gathered = _exchange("gather_d2d_first", _ride_gather_d2d(_gather_ici_relayed(bases[0])))
    h = _embed(x, full["meta"], dm)
    saved, proj_w = [], []
    for i in range(depth):
        lw = {n: full[n][i] for n in _LAYER_SMALL}
        proj_w.append((_w_in_from_shards(gathered[0], dm), gathered[1].reshape(2 * d, d)))
        h, keep, gathered = _layer_fwd(h, lw, proj_w[i][0], proj_w[i][1], cst, dm,
                                       bases[i + 1] if i + 1 < depth else None)
        saved.append(keep)

    dh, loss = _loss_head(h, loss_target, dm)
    loss = lax.psum(loss, ("x", "y", "c"))

    small_g = {n: [None] * depth for n in _LAYER_SMALL}
    big = {"w_in": None, "w_out": None}
    g_in = [None] * depth
    reduce = None
    for i in reversed(range(depth)):
        lw = {n: full[n][i] for n in _LAYER_SMALL}
        dh, mine, sg = _layer_bwd(dh, saved[i], lw, proj_w[i][0], proj_w[i][1], cst, dm, reduce, last=i == 0)
        for n in _LAYER_SMALL:
            small_g[n][i] = sg[n]
        if reduce is not None:
            g_in[i + 1] = reduce.result[0]
            big["w_out"] = _adamw_layer(i + 1, w_out, reduce.result[1], m_w_out, v_w_out, big["w_out"], "w_out")
        reduce = mine
    g_in[0] = reduce.result[0]
    big["w_out"] = _adamw_layer(0, w_out, reduce.result[1], m_w_out, v_w_out, big["w_out"], "w_out")
    grad_x, gmeta = _unembed(dh, dm)

    g = {n: jnp.stack(v) for n, v in small_g.items()}
    g["meta"] = gmeta
    small = [n for n in _WEIGHTS if n not in ("w_in", "w_out")]
    flat = jnp.concatenate([g[n].reshape(-1) for n in small])
    rows = -(-flat.shape[0] // (16 * LANES)) * 16
    flat = jnp.pad(flat, (0, rows * LANES - flat.shape[0])).reshape(rows, LANES)
    parts = _gather_all(_with_own_block(flat, N_DEV, dev))
    total = _sum_leading(parts, "small_grads_sum").reshape(-1)
    big["w_in"] = _adamw_cols_major(w_in, g_in, m_w_in, v_w_in, "w_in")
    grads, deltas, new_m, new_v = {}, {}, {}, {}
    off = 0
    for n in small:
        size = g[n].size
        fullg = total[off:off + size].reshape(g[n].shape)
        off += size
        if n in _SHARDED_SMALL:
            fullg = lax.dynamic_index_in_dim(_shard_last(fullg), chip, axis=0, keepdims=False)
        grads[n] = fullg
        deltas[n], new_m[n], new_v[n] = _adamw_small(w[n], fullg, mom[n], vel[n], n)
    for n in ("w_in", "w_out"):
        grads[n], deltas[n], new_m[n], new_v[n] = big[n]

    return (loss, grad_x, *[grads[n] for n in _WEIGHTS], *[deltas[n] for n in _WEIGHTS],
            *[new_m[n] for n in _WEIGHTS], *[new_v[n] for n in _WEIGHTS])
```

```python
import jax
import jax.numpy as jnp
from jax import lax
from jax.experimental import pallas as pl
from jax.experimental.pallas import tpu as pltpu

F32 = jnp.float32
BF16 = jnp.bfloat16

N_META = 16
TT = 128
SSM_STATE = 128
SSM_GROUPS = 2
SSM_HEAD_DIM = 64
CONV_A_K = 3
SSM_CONV_K = 4
CONF_K = 31
NORM_EPS = 1e-6
LN_EPS = 1e-5
LANES = 128
MXU_DIM = 256
DT_PAD = LANES
CONF_HALO = 32
SMALL_HALO = 8
VMEM_LIMIT = 56 * 1024 * 1024
N_CHIPS = 4
N_DEV = 8

ADAM_LR = 0.001
ADAM_B1 = 0.9
ADAM_B2 = 0.999
ADAM_EPS = 1e-08
ADAM_WD = 0.01
ADAM_STEP = 10

MESH = pl.DeviceIdType.MESH
ANY = pl.BlockSpec(memory_space=pl.ANY)


class Dims:
    def __init__(self, bl, seq, d):
        self.BL, self.S, self.D = bl, seq, d
        self.L = seq + N_META
        self.Lp = -(-self.L // TT) * TT
        self.NT = self.Lp // TT
        self.R = bl * self.Lp
        self.DA = d // 2
        self.DB = d
        self.DC = d // 2
        self.H = self.DB // SSM_HEAD_DIM
        self.HPG = self.H // SSM_GROUPS
        self.GN = SSM_GROUPS * SSM_STATE
        self.WA = 4 * self.DA
        self.WB = 2 * self.DB + 2 * self.GN
        self.WC = 3 * self.DC
        self.DT0 = self.WA + self.WB + self.WC
        self.NP = -(-(self.DT0 + DT_PAD) // (5 * MXU_DIM)) * (5 * MXU_DIM)
        self.NIN = self.WA + self.WB + self.H + self.WC
        self.XBC = self.DB + 2 * self.GN
        assert self.H % 2 == 0 and self.HPG % 2 == 0 and self.H <= DT_PAD
        assert self.DA % LANES == 0 and (self.WA + self.WB) % self.DC == 0 and self.WA % self.DB == 0


def _row_tile(n, target):
    best = None
    for t in range(16, min(n, target) + 1, 16):
        if n % t == 0:
            best = t
    assert best is not None
    return best


def _col_tile(n, target):
    best = None
    for t in range(LANES, min(n, target) + 1, LANES):
        if n % t == 0:
            best = t
    assert best is not None
    return best


def _params(sem=None):
    return pltpu.CompilerParams(dimension_semantics=sem, vmem_limit_bytes=VMEM_LIMIT)


def _sigmoid(x):
    return 1.0 / (1.0 + jnp.exp(-x))


def _silu_and_grad(x):
    s = _sigmoid(x)
    return x * s, s * (1.0 + x * (1.0 - s))


def _dot(a, b):
    return jnp.dot(a, b, preferred_element_type=F32)


def _dot_nt(a, b):
    return lax.dot_general(a, b, (((1,), (1,)), ((), ())), preferred_element_type=F32)


def _dot_tn(a, b):
    return lax.dot_general(a, b, (((0,), (0,)), ((), ())), preferred_element_type=F32)


def _split3(x):
    x1 = x.astype(BF16)
    r1 = x - x1.astype(F32)
    x2 = r1.astype(BF16)
    x3 = (r1 - x2.astype(F32)).astype(BF16)
    return x1, x2, x3


class Rider:
    def __init__(self, plan, ins, out_shapes, aliases, nsem):
        self.plan, self.ins, self.out_shapes, self.aliases, self.nsem = plan, list(ins), list(out_shapes), aliases, nsem


def _place():
    x, y, c = lax.axis_index("x"), lax.axis_index("y"), lax.axis_index("c")
    chips = [(1 - x, y), (x, 1 - y), (1 - x, 1 - y)]
    return x, y, c, chips


def _remote(k, src, dst, to, send_sems, recv_sems):
    return pltpu.make_async_remote_copy(src_ref=src, dst_ref=dst, send_sem=send_sems.at[k], recv_sem=recv_sems.at[k],
                                        device_id=to, device_id_type=MESH)


def _call(body, name, grid, in_specs, out_specs, out_shape, scratch_shapes, sem, args, rider=None):
    if rider is None:
        outs = pl.pallas_call(body, name=name, grid=grid, in_specs=in_specs, out_specs=out_specs, out_shape=out_shape,
                              scratch_shapes=scratch_shapes, compiler_params=_params(sem))(*args)
        return list(outs), []
    n_in, n_out, n_scr = len(args), len(out_shape), len(scratch_shapes)
    r_in, r_out = len(rider.ins), len(rider.out_shapes)

    def hosted(*refs):
        ins, rins = refs[:n_in], refs[n_in:n_in + r_in]
        o0 = n_in + r_in
        outs, routs = refs[o0:o0 + n_out], refs[o0 + n_out:o0 + n_out + r_out]
        scr = refs[o0 + n_out + r_out:o0 + n_out + r_out + n_scr]
        send_sems, recv_sems = refs[o0 + n_out + r_out + n_scr:]
        first = pl.program_id(0) == 0
        last = pl.program_id(0) == grid[0] - 1
        for ax in range(1, len(grid)):
            first = jnp.logical_and(first, pl.program_id(ax) == 0)
            last = jnp.logical_and(last, pl.program_id(ax) == grid[ax] - 1)

        @pl.when(first)
        def _():
            starts, _ = rider.plan(rins, routs, send_sems, recv_sems)
            for cp in starts:
                cp.start()

        body(*ins, *outs, *scr)

        @pl.when(last)
        def _():
            _, waits = rider.plan(rins, routs, send_sems, recv_sems)
            for wait in waits:
                wait()

    res = pl.pallas_call(
        hosted, name=name, grid=grid,
        in_specs=list(in_specs) + [ANY] * r_in, out_specs=list(out_specs) + [ANY] * r_out,
        out_shape=list(out_shape) + rider.out_shapes,
        input_output_aliases={n_in + k: n_out + v for k, v in rider.aliases.items()},
        scratch_shapes=list(scratch_shapes) + [pltpu.SemaphoreType.DMA((rider.nsem,)),
                                               pltpu.SemaphoreType.DMA((rider.nsem,))],
        compiler_params=_params(("arbitrary",) * len(grid)),
    )(*args, *rider.ins)
    return list(res[:n_out]), list(res[n_out:])


def _exchange(name, rider):
    r_in, r_out = len(rider.ins), len(rider.out_shapes)

    def body(*refs):
        rins, routs = refs[:r_in], refs[r_in:r_in + r_out]
        send_sems, recv_sems = refs[r_in + r_out:]
        starts, waits = rider.plan(rins, routs, send_sems, recv_sems)
        for cp in starts:
            cp.start()
        for wait in waits:
            wait()

    res = pl.pallas_call(
        body, name=name, in_specs=[ANY] * r_in, out_specs=[ANY] * r_out, out_shape=rider.out_shapes,
        input_output_aliases=dict(rider.aliases),
        scratch_shapes=[pltpu.SemaphoreType.DMA((rider.nsem,)), pltpu.SemaphoreType.DMA((rider.nsem,))],
    )(*rider.ins)
    return list(res)


def _same(arrays):
    return [jax.ShapeDtypeStruct(a.shape, a.dtype) for a in arrays]


class _SemsFrom:
    def __init__(self, sems, first):
        self.sems, self.first = sems, first

    @property
    def at(self):
        return self

    def __getitem__(self, k):
        return self.sems.at[self.first + k]


def _ride_both(r1, r2):
    n_in, n_out = len(r1.ins), len(r1.out_shapes)

    def plan(ins, outs, ss, rs):
        s1, w1 = r1.plan(ins[:n_in], outs[:n_out], ss, rs)
        s2, w2 = r2.plan(ins[n_in:], outs[n_out:], _SemsFrom(ss, r1.nsem), _SemsFrom(rs, r1.nsem))
        return s1 + s2, w1 + w2

    aliases = dict(r1.aliases)
    aliases.update({n_in + k: n_out + v for k, v in r2.aliases.items()})
    return Rider(plan, r1.ins + r2.ins, r1.out_shapes + r2.out_shapes, aliases, r1.nsem + r2.nsem)


def _ride_gather_ici(bases, part=0, nparts=1):
    n = len(bases)

    def plan(ins, outs, ss, rs):
        x, y, c, chips = _place()
        me = 2 * x + y
        starts, waits = [], []
        for a in range(n):
            half = outs[a].shape[1] // 2
            mine = pl.ds(c * half + part * (half // nparts), half // nparts)
            for j, chip in enumerate(chips):
                cp = _remote(3 * a + j, outs[a].at[me, mine], outs[a].at[me, mine], (*chip, c), ss, rs)
                got = outs[a].at[2 * chip[0] + chip[1], mine]
                starts.append(cp)
                waits += [cp.wait_send, _remote(3 * a + j, got, got, (*chip, c), ss, rs).wait_recv]
        return starts, waits

    return Rider(plan, bases, _same(bases), {a: a for a in range(n)}, 3 * n)


def _gather_ici_relayed(bases):
    n = len(bases)

    def body(*refs):
        outs = refs[n:2 * n]
        ss, rs = refs[2 * n:]
        x, y, c, _ = _place()
        me, xn, yn, dg = 2 * x + y, 2 * (1 - x) + y, 2 * x + (1 - y), 2 * (1 - x) + (1 - y)
        to_x, to_y = (1 - x, y, c), (x, 1 - y, c)
        sends = []

        def send(k, piece, to):
            cp = _remote(k, piece, piece, to, ss, rs)
            cp.start()
            sends.append(cp)

        def arrived(k, piece, frm):
            _remote(k, piece, piece, frm, ss, rs).wait_recv()

        rows = []
        for a in range(n):
            half = outs[a].shape[1] // 2
            rows.append((pl.ds(c * half, half), pl.ds(c * half, half // 2), pl.ds(c * half + half // 2, half // 2)))
            send(4 * a, outs[a].at[me, rows[a][0]], to_x)
            send(4 * a + 1, outs[a].at[me, rows[a][0]], to_y)
        for a in range(n):
            mine, lo, hi = rows[a]
            arrived(4 * a, outs[a].at[xn, mine], to_x)
            send(4 * a + 2, outs[a].at[xn, lo], to_y)
            arrived(4 * a + 1, outs[a].at[yn, mine], to_y)
            send(4 * a + 3, outs[a].at[yn, hi], to_x)
        for a in range(n):
            mine, lo, hi = rows[a]
            arrived(4 * a + 2, outs[a].at[dg, lo], to_y)
            arrived(4 * a + 3, outs[a].at[dg, hi], to_x)
        for cp in sends:
            cp.wait_send()

    return pl.pallas_call(
        body, name="gather_ici_first", in_specs=[ANY] * n, out_specs=[ANY] * n, out_shape=_same(bases),
        input_output_aliases={a: a for a in range(n)},
        scratch_shapes=[pltpu.SemaphoreType.DMA((4 * n,)), pltpu.SemaphoreType.DMA((4 * n,))],
    )(*bases)


def _ride_gather_d2d(bases):
    n = len(bases)

    def plan(ins, outs, ss, rs):
        x, y, c, chips = _place()
        sib = (x, y, 1 - c)
        starts, waits = [], []
        for a in range(n):
            half = outs[a].shape[1] // 2
            for j, chip in enumerate(chips):
                frm = 2 * chip[0] + chip[1]
                got = outs[a].at[frm, pl.ds(c * half, half)]
                theirs = outs[a].at[frm, pl.ds((1 - c) * half, half)]
                cp = _remote(3 * a + j, got, got, sib, ss, rs)
                starts.append(cp)
                waits += [cp.wait_send, _remote(3 * a + j, theirs, theirs, sib, ss, rs).wait_recv]
        return starts, waits

    return Rider(plan, bases, _same(bases), {a: a for a in range(n)}, 3 * n)


def _ride_gather_small(bases):
    n = len(bases)

    def plan(ins, outs, ss, rs):
        x, y, c, chips = _place()
        me = 2 * x + y
        starts, waits = [], []
        for a in range(n):
            k = outs[a].shape[-1] // N_CHIPS
            lead = (slice(None),) * (len(outs[a].shape) - 1)
            at = (lambda s: pl.multiple_of(s * k, LANES)) if k % LANES == 0 else (lambda s: s * k)
            cols = lambda s: outs[a].at[lead + (pl.ds(at(s), k),)]
            for j, chip in enumerate(chips):
                cp = _remote(3 * a + j, cols(me), cols(me), (*chip, c), ss, rs)
                got = cols(2 * chip[0] + chip[1])
                starts.append(cp)
                waits += [cp.wait_send, _remote(3 * a + j, got, got, (*chip, c), ss, rs).wait_recv]
        return starts, waits

    return Rider(plan, bases, _same(bases), {a: a for a in range(n)}, 3 * n)


def _ride_swap_halves(gs):
    n = len(gs)

    def plan(ins, outs, ss, rs):
        x, y, c, _ = _place()
        cps = [_remote(a, ins[a].at[:, 1 - c], outs[a], (x, y, 1 - c), ss, rs) for a in range(n)]
        return cps, [cp.wait for cp in cps]

    shapes = [jax.ShapeDtypeStruct((g.shape[0],) + g.shape[2:], g.dtype) for g in gs]
    return Rider(plan, gs, shapes, {}, n)


def _ride_to_owners(ps):
    n = len(ps)

    def plan(ins, outs, ss, rs):
        x, y, c, chips = _place()
        cps = []
        for a in range(n):
            for j, chip in enumerate(chips):
                cps.append(_remote(3 * a + j, ins[a].at[2 * chip[0] + chip[1]], outs[a].at[j], (*chip, c), ss, rs))
        return cps, [cp.wait for cp in cps]

    shapes = [jax.ShapeDtypeStruct((3,) + p.shape[1:], p.dtype) for p in ps]
    return Rider(plan, ps, shapes, {}, 3 * n)


def _ride_join_halves(qs):
    n = len(qs)

    def plan(ins, outs, ss, rs):
        x, y, c, _ = _place()
        sib = (x, y, 1 - c)
        starts, waits = [], []
        for a in range(n):
            cp = _remote(a, outs[a].at[c], outs[a].at[c], sib, ss, rs)
            starts.append(cp)
            waits += [cp.wait_send, _remote(a, outs[a].at[1 - c], outs[a].at[1 - c], sib, ss, rs).wait_recv]
        return starts, waits

    return Rider(plan, qs, _same(qs), {a: a for a in range(n)}, n)


def _gather_all(base):
    def body(in_ref, out_ref, ss, rs):
        x, y, c, chips = _place()
        sib = (x, y, 1 - c)
        block = lambda cx, cy, cc: out_ref.at[4 * cx + 2 * cy + cc]
        mine = block(x, y, c)
        first = [_remote(j, mine, mine, (*chip, c), ss, rs) for j, chip in enumerate(chips)]
        first.append(_remote(3, mine, mine, sib, ss, rs))
        for cp in first:
            cp.start()
        passed = []
        for j, chip in enumerate(chips):
            got = block(*chip, c)
            _remote(j, got, got, (*chip, c), ss, rs).wait_recv()
            passed.append(_remote(4 + j, got, got, sib, ss, rs))
            passed[-1].start()
        theirs = block(x, y, 1 - c)
        _remote(3, theirs, theirs, sib, ss, rs).wait_recv()
        for j, chip in enumerate(chips):
            got = block(*chip, 1 - c)
            _remote(4 + j, got, got, sib, ss, rs).wait_recv()
        for cp in first + passed:
            cp.wait_send()

    return pl.pallas_call(
        body, name="small_grads_gather_all", in_specs=[ANY], out_specs=ANY,
        out_shape=jax.ShapeDtypeStruct(base.shape, base.dtype), input_output_aliases={0: 0},
        scratch_shapes=[pltpu.SemaphoreType.DMA((N_DEV - 1,)), pltpu.SemaphoreType.DMA((N_DEV - 1,))],
    )(base)


def _embed(x, meta, dm):
    dc = _col_tile(dm.D, 256)
    s, lp = dm.S, dm.Lp

    def body(x_ref, meta_ref, h_ref):
        h_ref[0:N_META, :] = meta_ref[...]
        h_ref[N_META:N_META + s, :] = x_ref[0]
        if lp > N_META + s:
            h_ref[N_META + s:lp, :] = jnp.zeros((lp - N_META - s, dc), F32)

    return pl.pallas_call(
        body, name="embed", grid=(dm.BL, dm.D // dc),
        in_specs=[pl.BlockSpec((1, s, dc), lambda b, j: (b, 0, j)),
                  pl.BlockSpec((N_META, dc), lambda b, j: (0, j))],
        out_specs=pl.BlockSpec((lp, dc), lambda b, j: (b, j)),
        out_shape=jax.ShapeDtypeStruct((dm.R, dm.D), F32),
        compiler_params=_params(("parallel", "parallel")),
    )(x, meta)


def _loss_head(h, target, dm):
    dc = _col_tile(dm.D, 256)
    s, lp, nj = dm.S, dm.Lp, dm.D // dc

    def body(h_ref, t_ref, dh_ref, l_ref):
        diff = h_ref[N_META:N_META + s, :] - t_ref[0]
        dh_ref[0:N_META, :] = jnp.zeros((N_META, dc), F32)
        dh_ref[N_META:N_META + s, :] = diff * (1.0 / dm.D)
        if lp > N_META + s:
            dh_ref[N_META + s:lp, :] = jnp.zeros((lp - N_META - s, dc), F32)
        l_ref[...] = jnp.full((8, LANES), (0.5 / dm.D) * jnp.sum(diff * diff), F32)

    dh, part = pl.pallas_call(
        body, name="loss_head", grid=(dm.BL, nj),
        in_specs=[pl.BlockSpec((lp, dc), lambda b, j: (b, j)),
                  pl.BlockSpec((1, s, dc), lambda b, j: (b, 0, j))],
        out_specs=[pl.BlockSpec((lp, dc), lambda b, j: (b, j)),
                   pl.BlockSpec((8, LANES), lambda b, j: (b * nj + j, 0))],
        out_shape=[jax.ShapeDtypeStruct((dm.R, dm.D), F32),
                   jax.ShapeDtypeStruct((dm.BL * nj * 8, LANES), F32)],
        compiler_params=_params(("parallel", "parallel")),
    )(h, target)
    return dh, jnp.sum(part[::8, 0])


def _unembed(dh, dm):
    dc = _col_tile(dm.D, 256)
    s, lp = dm.S, dm.Lp

    def body(dh_ref, gx_ref, gm_ref):
        gx_ref[0] = dh_ref[N_META:N_META + s, :]

        @pl.when(pl.program_id(1) == 0)
        def _():
            gm_ref[...] = dh_ref[0:N_META, :]

        @pl.when(pl.program_id(1) > 0)
        def _():
            gm_ref[...] = gm_ref[...] + dh_ref[0:N_META, :]

    return pl.pallas_call(
        body, name="unembed", grid=(dm.D // dc, dm.BL),
        in_specs=[pl.BlockSpec((lp, dc), lambda j, b: (b, j))],
        out_specs=[pl.BlockSpec((1, s, dc), lambda j, b: (b, 0, j)),
                   pl.BlockSpec((N_META, dc), lambda j, b: (0, j))],
        out_shape=[jax.ShapeDtypeStruct((dm.BL, s, dm.D), F32),
                   jax.ShapeDtypeStruct((N_META, dm.D), F32)],
        compiler_params=_params(("parallel", "arbitrary")),
    )(dh)


def _fwd_in(h, pre_g, w, dm, rider=None):
    tm = _row_tile(dm.R, 1088)
    tn = _col_tile(dm.NP, 5 * MXU_DIM)
    nj = dm.NP // tn

    def body(h_ref, g_ref, w_ref, wdt_ref, proj_ref, dt_ref, hn_ref):
        @pl.when(pl.program_id(1) == 0)
        def _():
            xf = h_ref[...]
            r = lax.rsqrt(jnp.mean(xf * xf, axis=-1, keepdims=True) + NORM_EPS)
            hn_ref[...] = (xf * r * g_ref[...]).astype(BF16)
            dt_ref[...] = _dot(hn_ref[...], wdt_ref[...])

        proj_ref[...] = _dot(hn_ref[...], w_ref[...]).astype(BF16)

    return _call(
        body, "fwd_in", (dm.R // tm, nj),
        [pl.BlockSpec((tm, dm.D), lambda i, j: (i, 0)),
         pl.BlockSpec((1, dm.D), lambda i, j: (0, 0)),
         pl.BlockSpec((dm.D, tn), lambda i, j: (0, j)),
         pl.BlockSpec((dm.D, DT_PAD), lambda i, j: (0, dm.DT0 // DT_PAD))],
        [pl.BlockSpec((tm, tn), lambda i, j: (i, j)),
         pl.BlockSpec((tm, DT_PAD), lambda i, j: (i, 0)),
         pl.BlockSpec((tm, dm.D), lambda i, j: (i, 0))],
        [jax.ShapeDtypeStruct((dm.R, dm.NP), BF16), jax.ShapeDtypeStruct((dm.R, DT_PAD), F32),
         jax.ShapeDtypeStruct((dm.R, dm.D), BF16)],
        [], ("parallel", "arbitrary"), (h, pre_g, w, w), rider)


def _fwd_out(ya, yb, yc, w_out, h, post_g, dm, rider=None):
    tm = _row_tile(dm.Lp, 544)
    tiles_per_seq = dm.Lp // tm
    da, db, dc = dm.DA, dm.DB, dm.DC

    def body(ya_ref, yb_ref, yc_ref, w_ref, h_ref, g_ref, hn_ref, m_ref):
        m = _dot(ya_ref[...], w_ref[0:da, :])
        m = m + _dot(yb_ref[...], w_ref[da:da + db, :])
        m = m + _dot(yc_ref[...], w_ref[da + db:da + db + dc, :])
        m_ref[...] = m
        r = lax.rsqrt(jnp.mean(m * m, axis=-1, keepdims=True) + NORM_EPS)
        t = (pl.program_id(0) % tiles_per_seq) * tm + lax.broadcasted_iota(jnp.int32, (tm, 1), 0)
        keep = (t < dm.L).astype(F32)
        hn_ref[...] = (h_ref[...] + m * r * g_ref[...]) * keep

    row = lambda i: (i, 0)
    fixed = lambda i: (0, 0)
    return _call(
        body, "fwd_out", (dm.R // tm,),
        [pl.BlockSpec((tm, da), row), pl.BlockSpec((tm, db), row), pl.BlockSpec((tm, dc), row),
         pl.BlockSpec((2 * dm.D, dm.D), fixed), pl.BlockSpec((tm, dm.D), row), pl.BlockSpec((1, dm.D), fixed)],
        [pl.BlockSpec((tm, dm.D), row), pl.BlockSpec((tm, dm.D), row)],
        [jax.ShapeDtypeStruct((dm.R, dm.D), F32), jax.ShapeDtypeStruct((dm.R, dm.D), F32)],
        [], ("parallel",), (ya, yb, yc, w_out, h, post_g), rider)


def _bwd_out(dh, m, post_g, w_out, ya, yb, yc, dm, rider=None):
    tm = _row_tile(dm.R, MXU_DIM)
    da, db, dc = dm.DA, dm.DB, dm.DC

    def body(dh_ref, m_ref, g_ref, w_ref, ya_ref, yb_ref, yc_ref, dya_ref, dyb_ref, dyc_ref, dw_ref, dg_ref):
        @pl.when(pl.program_id(0) == 0)
        def _():
            dw_ref[...] = jnp.zeros_like(dw_ref)
            dg_ref[...] = jnp.zeros_like(dg_ref)

        m = m_ref[...]
        dh_ = dh_ref[...]
        r = lax.rsqrt(jnp.mean(m * m, axis=-1, keepdims=True) + NORM_EPS)
        n = m * r
        dg_ref[0:1, :] = dg_ref[0:1, :] + jnp.sum(dh_ * n, axis=0, keepdims=True)
        dn = dh_ * g_ref[...]
        dm_ = (r * (dn - n * jnp.mean(dn * n, axis=-1, keepdims=True))).astype(BF16)
        dya_ref[...] = _dot_nt(dm_, w_ref[0:da, :])
        dyb_ref[...] = _dot_nt(dm_, w_ref[da:da + db, :])
        dyc_ref[...] = _dot_nt(dm_, w_ref[da + db:da + db + dc, :])
        dw_ref[0:da, :] = dw_ref[0:da, :] + _dot_tn(ya_ref[...], dm_)
        dw_ref[da:da + db, :] = dw_ref[da:da + db, :] + _dot_tn(yb_ref[...], dm_)
        dw_ref[da + db:da + db + dc, :] = dw_ref[da + db:da + db + dc, :] + _dot_tn(yc_ref[...], dm_)

    row = lambda i: (i, 0)
    fixed = lambda i: (0, 0)
    return _call(
        body, "bwd_out", (dm.R // tm,),
        [pl.BlockSpec((tm, dm.D), row), pl.BlockSpec((tm, dm.D), row), pl.BlockSpec((1, dm.D), fixed),
         pl.BlockSpec((2 * dm.D, dm.D), fixed),
         pl.BlockSpec((tm, da), row), pl.BlockSpec((tm, db), row), pl.BlockSpec((tm, dc), row)],
        [pl.BlockSpec((tm, da), row), pl.BlockSpec((tm, db), row), pl.BlockSpec((tm, dc), row),
         pl.BlockSpec((2 * dm.D, dm.D), fixed), pl.BlockSpec((8, dm.D), fixed)],
        [jax.ShapeDtypeStruct((dm.R, da), F32), jax.ShapeDtypeStruct((dm.R, db), F32),
         jax.ShapeDtypeStruct((dm.R, dc), F32),
         jax.ShapeDtypeStruct((2 * dm.D, dm.D), F32), jax.ShapeDtypeStruct((8, dm.D), F32)],
        [], ("arbitrary",), (dh, m, post_g, w_out, ya, yb, yc), rider)


def _bwd_in_dx(dpa, dpb, dpc, dpt, w, h, dh, pre_g, dm, rider=None):
    tm = _row_tile(dm.R, 272)
    wa, wb, wc = dm.WA, dm.WB, dm.WC

    def body(dpa_ref, dpb_ref, dpc_ref, dpt_ref, w_ref, h_ref, dh_ref, g_ref, out_ref, dg_ref):
        @pl.when(pl.program_id(0) == 0)
        def _():
            dg_ref[...] = jnp.zeros_like(dg_ref)

        dhn = _dot_nt(dpa_ref[...], w_ref[:, 0:wa])
        dhn = dhn + _dot_nt(dpb_ref[...], w_ref[:, wa:wa + wb])
        dhn = dhn + _dot_nt(dpc_ref[...], w_ref[:, wa + wb:wa + wb + wc])
        dhn = dhn + _dot_nt(dpt_ref[...], w_ref[:, wa + wb + wc:wa + wb + wc + DT_PAD])
        xf = h_ref[...]
        r = lax.rsqrt(jnp.mean(xf * xf, axis=-1, keepdims=True) + NORM_EPS)
        n = xf * r
        dg_ref[0:1, :] = dg_ref[0:1, :] + jnp.sum(dhn * n, axis=0, keepdims=True)
        dn = dhn * g_ref[...]
        out_ref[...] = dh_ref[...] + r * (dn - n * jnp.mean(dn * n, axis=-1, keepdims=True))

    row = lambda i: (i, 0)
    fixed = lambda i: (0, 0)
    return _call(
        body, "bwd_in_dx", (dm.R // tm,),
        [pl.BlockSpec((tm, wa), row), pl.BlockSpec((tm, wb), row), pl.BlockSpec((tm, wc), row),
         pl.BlockSpec((tm, DT_PAD), row), pl.BlockSpec((dm.D, dm.NP), fixed),
         pl.BlockSpec((tm, dm.D), row), pl.BlockSpec((tm, dm.D), row), pl.BlockSpec((1, dm.D), fixed)],
        [pl.BlockSpec((tm, dm.D), row), pl.BlockSpec((8, dm.D), fixed)],
        [jax.ShapeDtypeStruct((dm.R, dm.D), F32), jax.ShapeDtypeStruct((8, dm.D), F32)],
        [], ("arbitrary",), (dpa, dpb, dpc, dpt, w, h, dh, pre_g), rider)


def _bwd_in_dw(hn, dps, dm):
    tn = [_col_tile(dp.shape[1], MXU_DIM) for dp in dps]
    nb = [dp.shape[1] // t for dp, t in zip(dps, tn)]
    first = [sum(nb[:p]) for p in range(len(dps))]
    at = lambda p: (lambda j: (0, jnp.clip(j - first[p], 0, nb[p] - 1)))

    def body(hn_ref, *refs):
        j = pl.program_id(0)
        for p in range(len(dps)):
            @pl.when(jnp.logical_and(j >= first[p], j < first[p] + nb[p]))
            def _(p=p):
                refs[len(dps) + p][...] = _dot_tn(hn_ref[...], refs[p][...])

    return pl.pallas_call(
        body, name="bwd_in_dw", grid=(sum(nb),),
        in_specs=[pl.BlockSpec((dm.R, dm.D), lambda j: (0, 0))] + [
            pl.BlockSpec((dm.R, tn[p]), at(p)) for p in range(len(dps))],
        out_specs=[pl.BlockSpec((dm.D, tn[p]), at(p)) for p in range(len(dps))],
        out_shape=[jax.ShapeDtypeStruct((dm.D, dp.shape[1]), F32) for dp in dps],
        compiler_params=_params(("arbitrary",)),
    )(hn, *dps)


def _tile_index(dm, reverse):
    if reverse:
        return lambda b, i: b * dm.NT + (dm.NT - 1 - i)
    return lambda b, i: b * dm.NT + i


def _halo_index(dm, rows):
    per_tile = TT // rows
    return lambda b, i: jnp.maximum((b * dm.NT + (dm.NT - 1 - i)) * per_tile - 1, 0)


HALO_BLOCK = 16


def _last_rows(x):
    return x.astype(F32)[HALO_BLOCK - SMALL_HALO:HALO_BLOCK]


MIX_A_ROWS = 288


def _mix_a_fwd(proj, conv_w, dm):
    da = dm.DA
    ta = _row_tile(dm.Lp, MIX_A_ROWS)
    nta = dm.Lp // ta
    bl = dm.BL

    def setup(ab_ref, ac_ref, ax_ref, az_ref, w_ref, y_ref, pbuf):
        i = pl.program_id(0)

        @pl.when(i == 0)
        def _():
            pbuf[0:SMALL_HALO, :] = jnp.zeros((SMALL_HALO, da), F32)

        @pl.when(i > 0)
        def _():
            pbuf[0:SMALL_HALO, :] = pbuf[ta:ta + SMALL_HALO, :]

    def body(ab_ref, ac_ref, ax_ref, az_ref, w_ref, y_ref, pbuf):
        for lb in range(da // LANES):
            cs = slice(lb * LANES, (lb + 1) * LANES)
            p = ac_ref[:, cs].astype(F32) * ax_ref[:, cs].astype(F32)
            pbuf[SMALL_HALO:SMALL_HALO + ta, cs] = p
            q = (w_ref[0:1, cs] * pbuf[6:6 + ta, cs] + w_ref[1:2, cs] * pbuf[7:7 + ta, cs] + w_ref[2:3, cs] * p)
            az = az_ref[:, cs].astype(F32)
            y_ref[:, cs] = (ab_ref[:, cs].astype(F32) * q * (az * _sigmoid(az))).astype(BF16)
            yield

    proj3 = proj.reshape(bl, dm.Lp, dm.NP)
    col = lambda k: pl.BlockSpec((bl, ta, da), lambda i: (0, i, k))
    return pl.pallas_call(
        _per_sequence(setup, body, bl, [True] * 4 + [False] + [True, True]), name="mix_a_fwd", grid=(nta,),
        in_specs=[col(0), col(1), col(2), col(3), pl.BlockSpec((CONV_A_K, da), lambda i: (0, 0))],
        out_specs=col(0),
        out_shape=jax.ShapeDtypeStruct((bl, dm.Lp, da), BF16),
        scratch_shapes=[pltpu.VMEM((bl, SMALL_HALO + ta, da), F32)],
        compiler_params=_params(("arbitrary",)),
    )(proj3, proj3, proj3, proj3, conv_w).reshape(dm.R, da)


def _mix_a_bwd(proj, dya, conv_w, dm):
    da = dm.DA
    ta = _row_tile(dm.Lp, MIX_A_ROWS)
    nta = dm.Lp // ta
    bl = dm.BL

    def setup(ab_ref, ac_ref, ax_ref, az_ref, ach_ref, axh_ref, dy_ref, w_ref, dp_ref, dw_ref, pbuf, dqbuf):
        i = pl.program_id(0)

        @pl.when(i == 0)
        def _():
            dw_ref[...] = jnp.zeros_like(dw_ref)
            dqbuf[ta:ta + SMALL_HALO, :] = jnp.zeros((SMALL_HALO, da), F32)

        @pl.when(i > 0)
        def _():
            dqbuf[ta:ta + SMALL_HALO, :] = dqbuf[0:SMALL_HALO, :]

    def body(ab_ref, ac_ref, ax_ref, az_ref, ach_ref, axh_ref, dy_ref, w_ref, dp_ref, dw_ref, pbuf, dqbuf):
        halo_on = jnp.where(pl.program_id(0) == nta - 1, 0.0, 1.0)
        for lb in range(da // LANES):
            cs = slice(lb * LANES, (lb + 1) * LANES)
            pbuf[0:SMALL_HALO, cs] = (_last_rows(ach_ref[:, cs]) * _last_rows(axh_ref[:, cs])) * halo_on
            ac, ax, ab, az = (r[:, cs].astype(F32) for r in (ac_ref, ax_ref, ab_ref, az_ref))
            p = ac * ax
            pbuf[SMALL_HALO:SMALL_HALO + ta, cs] = p
            p1 = pbuf[7:7 + ta, cs]
            p2 = pbuf[6:6 + ta, cs]
            w0, w1, w2 = w_ref[0:1, cs], w_ref[1:2, cs], w_ref[2:3, cs]
            q = w0 * p2 + w1 * p1 + w2 * p
            sz, dsz = _silu_and_grad(az)
            dy = dy_ref[:, cs]
            t1 = dy * ab
            dq = t1 * sz
            dqbuf[0:ta, cs] = dq
            dpv = w2 * dq + w1 * dqbuf[1:1 + ta, cs] + w0 * dqbuf[2:2 + ta, cs]
            dp_ref[:, lb * LANES:(lb + 1) * LANES] = (dy * q * sz).astype(BF16)
            dp_ref[:, da + lb * LANES:da + (lb + 1) * LANES] = (dpv * ax).astype(BF16)
            dp_ref[:, 2 * da + lb * LANES:2 * da + (lb + 1) * LANES] = (dpv * ac).astype(BF16)
            dp_ref[:, 3 * da + lb * LANES:3 * da + (lb + 1) * LANES] = (t1 * q * dsz).astype(BF16)
            dw_ref[0, 0:1, cs] = dw_ref[0, 0:1, cs] + jnp.sum(dq * p2, axis=0, keepdims=True)
            dw_ref[0, 1:2, cs] = dw_ref[0, 1:2, cs] + jnp.sum(dq * p1, axis=0, keepdims=True)
            dw_ref[0, 2:3, cs] = dw_ref[0, 2:3, cs] + jnp.sum(dq * p, axis=0, keepdims=True)
            yield

    proj3 = proj.reshape(bl, dm.Lp, dm.NP)
    col = lambda w, k: pl.BlockSpec((bl, ta, w), lambda i: (0, nta - 1 - i, k))
    halo = lambda k: pl.BlockSpec((bl, HALO_BLOCK, da),
                                  lambda i: (0, jnp.maximum((nta - 1 - i) * (ta // HALO_BLOCK) - 1, 0), k))
    dp, dw = pl.pallas_call(
        _per_sequence(setup, body, bl, [True] * 7 + [False] + [True, "keep"] + [True, True]),
        name="mix_a_bwd", grid=(nta,),
        in_specs=[col(da, 0), col(da, 1), col(da, 2), col(da, 3), halo(1), halo(2), col(da, 0),
                  pl.BlockSpec((CONV_A_K, da), lambda i: (0, 0))],
        out_specs=[col(dm.WA, 0), pl.BlockSpec((bl, 8, da), lambda i: (0, 0, 0))],
        out_shape=[jax.ShapeDtypeStruct((bl, dm.Lp, dm.WA), BF16), jax.ShapeDtypeStruct((bl, 8, da), F32)],
        scratch_shapes=[pltpu.VMEM((bl, SMALL_HALO + ta, da), F32), pltpu.VMEM((bl, ta + SMALL_HALO, da), F32)],
        compiler_params=_params(("arbitrary",)),
    )(proj3, proj3, proj3, proj3, proj3, proj3, dya.reshape(bl, dm.Lp, da), conv_w)
    return dp.reshape(dm.R, dm.WA), dw


SUBLANES = 8
SHIFT_ROWS = TT + CONF_HALO - SUBLANES


TAP_ROWS = 64


def _split_lanes(buf, rows, val):
    for lb in range(val.shape[1] // LANES):
        buf[lb, rows, :] = val[:, lb * LANES:(lb + 1) * LANES]


def _join_lanes(buf):
    return jnp.concatenate([buf[lb] for lb in range(buf.shape[0])], axis=1)


def _fill_shifted(buf, shifted):
    def step(lb, carry):
        for r in range(1, SUBLANES):
            shifted[lb, r - 1, 0:SHIFT_ROWS, :] = buf[lb, r:r + SHIFT_ROWS, :]
        return carry

    lax.fori_loop(0, buf.shape[0], step, 0)


def _window(buf, shifted, d, r0, lb):
    r = d % SUBLANES
    rows = pl.ds(pl.multiple_of(r0 + (d - r), SUBLANES), TAP_ROWS)
    return buf[lb, rows, :] if r == 0 else shifted[lb, r - 1, rows, :]


def _tap_loop(nlb, body):
    per_lb = TT // TAP_ROWS

    def step(it, carry):
        lb = it // per_lb
        body(lb, pl.ds(pl.multiple_of(lb * LANES, LANES), LANES), pl.multiple_of((it % per_lb) * TAP_ROWS, TAP_ROWS))
        return carry

    lax.fori_loop(0, nlb * per_lb, step, 0)


TAP_CHAINS = 4


def _tree_sum(terms):
    sums = list(terms[:TAP_CHAINS])
    for n, t in enumerate(terms[TAP_CHAINS:]):
        sums[n % TAP_CHAINS] = sums[n % TAP_CHAINS] + t
    while len(sums) > 1:
        sums = [a + b for a, b in zip(sums[0::2], sums[1::2])] + ([sums[-1]] if len(sums) % 2 else [])
    return sums[0]


def _conf_conv(ubuf, ushift, w_ref, b_ref, u1buf):
    _fill_shifted(ubuf, ushift)

    def piece(lb, cs, r0):
        taps = [w_ref[k:k + 1, cs] * _window(ubuf, ushift, CONF_HALO - (CONF_K - 1) + k, r0, lb)
                for k in range(CONF_K)]
        u1buf[lb, pl.ds(r0, TAP_ROWS), :] = _tree_sum(taps) + b_ref[0:1, cs]

    _tap_loop(ubuf.shape[0], piece)


def _mix_c_fwd(proj, conv_w, conv_b, ln_g, ln_b, dm):
    dc = dm.DC
    nlb = dc // LANES
    c0 = (dm.WA + dm.WB) // dc
    ti = _tile_index(dm, False)

    def body(ca_ref, cg_ref, cz_ref, w_ref, b_ref, g_ref, be_ref, y_ref, u1_ref, ubuf, u1buf, ushift):
        i = pl.program_id(1)

        @pl.when(i == 0)
        def _():
            ubuf[:, 0:CONF_HALO, :] = jnp.zeros((nlb, CONF_HALO, LANES), F32)

        @pl.when(i > 0)
        def _():
            ubuf[:, 0:CONF_HALO, :] = ubuf[:, TT:TT + CONF_HALO, :]

        _split_lanes(ubuf, slice(CONF_HALO, CONF_HALO + TT),
                     ca_ref[...].astype(F32) * _sigmoid(cg_ref[...].astype(F32)))
        _conf_conv(ubuf, ushift, w_ref, b_ref, u1buf)
        u1 = _join_lanes(u1buf)
        u1_ref[...] = u1
        mu = jnp.mean(u1, axis=-1, keepdims=True)
        xc = u1 - mu
        rstd = lax.rsqrt(jnp.mean(xc * xc, axis=-1, keepdims=True) + LN_EPS)
        u2 = xc * rstd * g_ref[...] + be_ref[...]
        cz = cz_ref[...].astype(F32)
        y_ref[...] = ((u2 * _sigmoid(u2)) * (cz * _sigmoid(cz))).astype(BF16)

    col = lambda k: pl.BlockSpec((TT, dc), lambda b, i: (ti(b, i), c0 + k))
    vec = pl.BlockSpec((1, dc), lambda b, i: (0, 0))
    return pl.pallas_call(
        body, name="mix_c_fwd", grid=(dm.BL, dm.NT),
        in_specs=[col(0), col(1), col(2), pl.BlockSpec((CONF_K, dc), lambda b, i: (0, 0)), vec, vec, vec],
        out_specs=[pl.BlockSpec((TT, dc), lambda b, i: (ti(b, i), 0))] * 2,
        out_shape=[jax.ShapeDtypeStruct((dm.R, dc), BF16), jax.ShapeDtypeStruct((dm.R, dc), F32)],
        scratch_shapes=[pltpu.VMEM((nlb, CONF_HALO + TT, LANES), F32), pltpu.VMEM((nlb, TT, LANES), F32),
                        pltpu.VMEM((nlb, SUBLANES - 1, SHIFT_ROWS, LANES), F32)],
        compiler_params=_params(("parallel", "arbitrary")),
    )(proj, proj, proj, conv_w, conv_b, ln_g, ln_b)


def _mix_c_bwd(proj, u1, dyc, conv_w, ln_g, ln_b, dm):
    dc = dm.DC
    nlb = dc // LANES
    c0 = (dm.WA + dm.WB) // dc
    ti = _tile_index(dm, True)
    hi = _halo_index(dm, CONF_HALO)

    def body(ca_ref, cg_ref, cz_ref, cah_ref, cgh_ref, u1_ref, dy_ref, w_ref, g_ref, be_ref,
             dp_ref, dw_ref, dv_ref, ubuf, dubuf, du0buf, ushift, dshift, dwacc):
        i = pl.program_id(1)
        halo_on = jnp.where(i == dm.NT - 1, 0.0, 1.0)

        @pl.when(i == 0)
        def _():
            dwacc[...] = jnp.zeros_like(dwacc)
            dv_ref[...] = jnp.zeros_like(dv_ref)
            dubuf[:, TT:TT + CONF_HALO, :] = jnp.zeros((nlb, CONF_HALO, LANES), F32)

        @pl.when(i > 0)
        def _():
            dubuf[:, TT:TT + CONF_HALO, :] = dubuf[:, 0:CONF_HALO, :]

        _split_lanes(ubuf, slice(0, CONF_HALO),
                     cah_ref[...].astype(F32) * _sigmoid(cgh_ref[...].astype(F32)) * halo_on)
        sgg = _sigmoid(cg_ref[...].astype(F32))
        ca = ca_ref[...].astype(F32)
        _split_lanes(ubuf, slice(CONF_HALO, CONF_HALO + TT), ca * sgg)
        _fill_shifted(ubuf, ushift)
        u1 = u1_ref[...]
        mu = jnp.mean(u1, axis=-1, keepdims=True)
        xc = u1 - mu
        rstd = lax.rsqrt(jnp.mean(xc * xc, axis=-1, keepdims=True) + LN_EPS)
        xhat = xc * rstd
        u2 = xhat * g_ref[...] + be_ref[...]
        su, dsu = _silu_and_grad(u2)
        sz, dsz = _silu_and_grad(cz_ref[...].astype(F32))
        dy = dy_ref[...]
        du2 = dy * dsu * sz
        dp_ref[:, 2 * dc:3 * dc] = (dy * su * dsz).astype(BF16)
        dxhat = du2 * g_ref[...]
        du1 = rstd * (dxhat - jnp.mean(dxhat, axis=-1, keepdims=True)
                      - xhat * jnp.mean(dxhat * xhat, axis=-1, keepdims=True))
        dv_ref[0, 0:1, :] = dv_ref[0, 0:1, :] + jnp.sum(du1, axis=0, keepdims=True)
        dv_ref[0, 1:2, :] = dv_ref[0, 1:2, :] + jnp.sum(du2 * xhat, axis=0, keepdims=True)
        dv_ref[0, 2:3, :] = dv_ref[0, 2:3, :] + jnp.sum(du2, axis=0, keepdims=True)
        _split_lanes(dubuf, slice(0, TT), du1)
        _fill_shifted(dubuf, dshift)

        def piece(lb, cs, r0):
            du0buf[lb, pl.ds(r0, TAP_ROWS), :] = _tree_sum(
                [w_ref[k:k + 1, cs] * _window(dubuf, dshift, CONF_K - 1 - k, r0, lb) for k in range(CONF_K)])
            d1 = dubuf[lb, pl.ds(r0, TAP_ROWS), :]
            for k in range(CONF_K):
                prod = d1 * _window(ubuf, ushift, CONF_HALO - (CONF_K - 1) + k, r0, lb)
                dwacc[lb, k] = dwacc[lb, k] + jnp.sum(prod.reshape(TAP_ROWS // SUBLANES, SUBLANES, LANES), axis=0)

        _tap_loop(nlb, piece)
        du0 = _join_lanes(du0buf)
        dp_ref[:, 0:dc] = (du0 * sgg).astype(BF16)
        dp_ref[:, dc:2 * dc] = (du0 * ca * sgg * (1.0 - sgg)).astype(BF16)

        @pl.when(i == dm.NT - 1)
        def _():
            for lb in range(nlb):
                dw_ref[0, 0:CONF_K, lb * LANES:(lb + 1) * LANES] = jnp.sum(dwacc[lb], axis=1)
            dw_ref[0, CONF_K:CONF_K + 1, :] = jnp.zeros((1, dc), F32)

    col = lambda k: pl.BlockSpec((TT, dc), lambda b, i: (ti(b, i), c0 + k))
    halo = lambda k: pl.BlockSpec((CONF_HALO, dc), lambda b, i: (hi(b, i), c0 + k))
    vec = pl.BlockSpec((1, dc), lambda b, i: (0, 0))
    return pl.pallas_call(
        body, name="mix_c_bwd", grid=(dm.BL, dm.NT),
        in_specs=[col(0), col(1), col(2), halo(0), halo(1),
                  pl.BlockSpec((TT, dc), lambda b, i: (ti(b, i), 0)),
                  pl.BlockSpec((TT, dc), lambda b, i: (ti(b, i), 0)),
                  pl.BlockSpec((CONF_K, dc), lambda b, i: (0, 0)), vec, vec],
        out_specs=[pl.BlockSpec((TT, dm.WC), lambda b, i: (ti(b, i), 0)),
                   pl.BlockSpec((1, 32, dc), lambda b, i: (b, 0, 0)),
                   pl.BlockSpec((1, 8, dc), lambda b, i: (b, 0, 0))],
        out_shape=[jax.ShapeDtypeStruct((dm.R, dm.WC), BF16),
                   jax.ShapeDtypeStruct((dm.BL, 32, dc), F32),
                   jax.ShapeDtypeStruct((dm.BL, 8, dc), F32)],
        scratch_shapes=[pltpu.VMEM((nlb, CONF_HALO + TT, LANES), F32),
                        pltpu.VMEM((nlb, TT + CONF_HALO, LANES), F32), pltpu.VMEM((nlb, TT, LANES), F32),
                        pltpu.VMEM((nlb, SUBLANES - 1, SHIFT_ROWS, LANES), F32),
                        pltpu.VMEM((nlb, SUBLANES - 1, SHIFT_ROWS, LANES), F32),
                        pltpu.VMEM((nlb, CONF_K, SUBLANES, LANES), F32)],
        compiler_params=_params(("parallel", "arbitrary")),
    )(proj, proj, proj, proj, proj, u1, dyc, conv_w, ln_g, ln_b)


RAW_HALO = 16


def _shift_matrix():
    r = jnp.arange((SSM_CONV_K - 1) * TT)[:, None]
    want = TT + r % TT - (SSM_CONV_K - 1 - r // TT)
    return (jnp.arange(2 * TT)[None, :] == want).astype(BF16)


def _ssm_conv(rawwin, sh, s_ref, w_ref, b_ref, width):
    sh[...] = _dot(s_ref[...], rawwin[...])
    for lb in range(width // LANES):
        cs = slice(lb * LANES, (lb + 1) * LANES)
        acc = b_ref[0:1, cs] + w_ref[SSM_CONV_K - 1:SSM_CONV_K, cs] * rawwin[TT:2 * TT, cs].astype(F32)
        for k in range(SSM_CONV_K - 1):
            acc = acc + w_ref[k:k + 1, cs] * sh[k * TT:(k + 1) * TT, cs]
        yield cs, acc


def _softplus(z):
    return jnp.maximum(z, 0.0) + jnp.log(1.0 + jnp.exp(-jnp.abs(z)))


def _tri(lower):
    r = lax.broadcasted_iota(jnp.int32, (TT, TT), 0)
    c = lax.broadcasted_iota(jnp.int32, (TT, TT), 1)
    return (c <= r) if lower else (c >= r)


def _exact_01_dot(mat01, x):
    x1, x2, x3 = _split3(x)
    return _dot(mat01, x1) + _dot(mat01, x2) + _dot(mat01, x3)


def _head_scalars(dt_ref, dtb_ref, alog_ref):
    z = dt_ref[...] + dtb_ref[...]
    dtv = _softplus(z)
    a = -jnp.exp(alog_ref[...])
    ac = _exact_01_dot(_tri(True).astype(F32).astype(BF16), dtv * a)
    eac = jnp.exp(ac)
    dst = jnp.exp(ac[TT - 1:TT, :] - ac)
    return z, dtv, a, ac, eac, dst


FAR_BELOW = -1e30


def _decay(ac, ac_t, h, causal):
    return jnp.exp(jnp.where(causal, ac[:, h:h + 1] - ac_t[h:h + 1, :], FAR_BELOW))


def _own_half(x16, h):
    lane = lax.broadcasted_iota(jnp.int32, (1, LANES), 1)
    keep = (lane >= SSM_HEAD_DIM) if (h % 2) else (lane < SSM_HEAD_DIM)
    return jnp.where(keep, x16, jnp.zeros_like(x16))


def _per_sequence(setup, body, bl, how):
    def all_sequences(*refs):
        views = [[r.at[b] if h is True else (r.at[pl.ds(b, 1)] if h == "keep" else r) for r, h in zip(refs, how)]
                 for b in range(bl)]
        for v in views:
            setup(*v)
        running = [body(*v) for v in views]
        while running:
            running = [g for g in running if next(g, "done") != "done"]

    return all_sequences


def _mix_b_fwd(proj, projdt, conv_w, conv_b, dt_bias, a_log, dskx, norm_g, expand, dm, rider=None):
    db, gn, xbc_w, hpg = dm.DB, dm.GN, dm.XBC, dm.HPG
    gw = db // SSM_GROUPS

    def setup(bz_ref, bx_ref, bc_ref, dt_ref, w_ref, b_ref, dtb_ref, alog_ref, dsk_ref, g_ref, e_ref, s_ref,
              y_ref, yraw_ref, sprev_ref, rawwin, sh, xbuf, state, ybuf, exbuf, xdtbuf):
        i = pl.program_id(0)

        @pl.when(i == 0)
        def _():
            rawwin[0:TT, :] = jnp.zeros((TT, xbc_w), BF16)
            state[...] = jnp.zeros_like(state)

        @pl.when(i > 0)
        def _():
            rawwin[TT - RAW_HALO:TT, :] = rawwin[2 * TT - RAW_HALO:2 * TT, :]

    def body(bz_ref, bx_ref, bc_ref, dt_ref, w_ref, b_ref, dtb_ref, alog_ref, dsk_ref, g_ref, e_ref, s_ref,
             y_ref, yraw_ref, sprev_ref, rawwin, sh, xbuf, state, ybuf, exbuf, xdtbuf):
        rawwin[TT:2 * TT, 0:db] = bx_ref[...]
        rawwin[TT:2 * TT, db:xbc_w] = bc_ref[...]
        for cs, pre in _ssm_conv(rawwin, sh, s_ref, w_ref, b_ref, xbc_w):
            xbuf[:, cs] = pre * _sigmoid(pre)
            yield

        _, dtv, _, ac, eac, dst = _head_scalars(dt_ref, dtb_ref, alog_ref)
        exbuf[...] = _dot(jnp.concatenate([dtv, eac, dst], axis=0).astype(BF16), e_ref[...])
        ac_t = ac.T
        causal = _tri(True)
        sprev_ref[0, 0] = state[...]
        yield

        xdtbuf[...] = xbuf[:, 0:db] * exbuf[0:TT, :]
        ybuf[...] = xbuf[:, 0:db] * dsk_ref[...]
        for g in range(SSM_GROUPS):
            gs = slice(g * gw, (g + 1) * gw)
            bg = xbuf[:, db + g * SSM_STATE:db + (g + 1) * SSM_STATE].astype(BF16)
            cg = xbuf[:, db + gn + g * SSM_STATE:db + gn + (g + 1) * SSM_STATE].astype(BF16)
            cb = _dot_nt(cg, bg)
            for e in range(0, hpg, 2):
                h = g * hpg + e
                ps = slice(h * SSM_HEAD_DIM, (h + 2) * SSM_HEAD_DIM)
                xp16 = xdtbuf[:, ps].astype(BF16)
                acc = jnp.zeros((TT, LANES), F32)
                for hh in (h, h + 1):
                    mm = (cb * _decay(ac, ac_t, hh, causal)).astype(BF16)
                    acc = acc + _dot(mm, _own_half(xp16, hh))
                ybuf[:, ps] = ybuf[:, ps] + acc
                yield
            sg = state[:, gs]
            ybuf[:, gs] = ybuf[:, gs] + exbuf[TT:2 * TT, gs] * _dot(cg, sg.astype(BF16))
            state[:, gs] = sg * exbuf[2 * TT - 1:2 * TT, gs] + _dot_tn(
                bg, (xdtbuf[:, gs] * exbuf[2 * TT:3 * TT, gs]).astype(BF16))
            yield

        yraw = ybuf[...]
        yraw_ref[...] = yraw
        bz = bz_ref[...].astype(F32)
        v = yraw * (bz * _sigmoid(bz))
        r = lax.rsqrt(jnp.mean(v * v, axis=-1, keepdims=True) + NORM_EPS)
        y_ref[...] = (v * r * g_ref[...]).astype(BF16)

    bl = dm.BL
    tile = lambda w, k: pl.BlockSpec((bl, TT, w), lambda i: (0, i, k))
    fixed = lambda r, w: pl.BlockSpec((r, w), lambda i: (0, 0))
    proj3, dt3 = proj.reshape(bl, dm.Lp, dm.NP), projdt.reshape(bl, dm.Lp, DT_PAD)
    scratch = [((2 * TT, xbc_w), BF16), (((SSM_CONV_K - 1) * TT, xbc_w), F32), ((TT, xbc_w), F32),
               ((SSM_STATE, db), F32), ((TT, db), F32), ((3 * TT, db), F32), ((TT, db), F32)]
    (y, yraw, sprev), rode = _call(
        _per_sequence(setup, body, bl, [True] * 4 + [False] * 8 + [True, True, "keep"] + [True] * len(scratch)),
        "mix_b_fwd", (dm.NT,),
        [tile(db, dm.WA // db), tile(db, dm.WA // db + 1), tile(2 * gn, (dm.WA + 2 * db) // (2 * gn)),
         tile(DT_PAD, 0),
         fixed(SSM_CONV_K, xbc_w), fixed(1, xbc_w), fixed(1, DT_PAD), fixed(1, DT_PAD),
         fixed(1, db), fixed(1, db), fixed(DT_PAD, db), fixed((SSM_CONV_K - 1) * TT, 2 * TT)],
        [tile(db, 0), tile(db, 0), pl.BlockSpec((bl, 1, SSM_STATE, db), lambda i: (0, i, 0, 0))],
        [jax.ShapeDtypeStruct((bl, dm.Lp, db), BF16), jax.ShapeDtypeStruct((bl, dm.Lp, db), F32),
         jax.ShapeDtypeStruct((bl, dm.NT, SSM_STATE, db), F32)],
        [pltpu.VMEM((bl,) + s, t) for s, t in scratch],
        ("arbitrary",),
        (proj3, proj3, proj3, dt3, conv_w, conv_b, dt_bias, a_log, dskx, norm_g, expand, _shift_matrix()), rider)
    return (y.reshape(dm.R, db), yraw.reshape(dm.R, db), sprev), rode


def _mix_b_bwd(proj, projdt, dyb, yraw, sprev, conv_w, conv_b, dt_bias, a_log, dskx, norm_g, expand, expand_t, dm,
               rider=None):
    db, gn, xbc_w, hpg = dm.DB, dm.GN, dm.XBC, dm.HPG
    gw = db // SSM_GROUPS

    def setup(bz_ref, bx_ref, bc_ref, dt_ref, bxh_ref, bch_ref, dy_ref, yraw_ref, sprev_ref,
              w_ref, b_ref, dtb_ref, alog_ref, dsk_ref, g_ref, e_ref, et_ref, s_ref,
              dp_ref, dpt_ref, dwc_ref, dch_ref, dhd_ref,
              rawwin, sh, xbuf, dsbuf, dstate, dxbuf, z1buf, dprebuf, exbuf, xdtbuf, dyrbuf, uvec):
        i = pl.program_id(0)

        @pl.when(i == 0)
        def _():
            dwc_ref[...] = jnp.zeros_like(dwc_ref)
            dch_ref[...] = jnp.zeros_like(dch_ref)
            dhd_ref[...] = jnp.zeros_like(dhd_ref)
            dstate[...] = jnp.zeros_like(dstate)
            dprebuf[TT:TT + SMALL_HALO, :] = jnp.zeros((SMALL_HALO, xbc_w), F32)
            rawwin[0:TT - RAW_HALO, :] = jnp.zeros((TT - RAW_HALO, xbc_w), BF16)

        @pl.when(i > 0)
        def _():
            dprebuf[TT:TT + SMALL_HALO, :] = dprebuf[0:SMALL_HALO, :]

    def body(bz_ref, bx_ref, bc_ref, dt_ref, bxh_ref, bch_ref, dy_ref, yraw_ref, sprev_ref,
             w_ref, b_ref, dtb_ref, alog_ref, dsk_ref, g_ref, e_ref, et_ref, s_ref,
             dp_ref, dpt_ref, dwc_ref, dch_ref, dhd_ref,
             rawwin, sh, xbuf, dsbuf, dstate, dxbuf, z1buf, dprebuf, exbuf, xdtbuf, dyrbuf, uvec):
        halo_on = jnp.where(pl.program_id(0) == dm.NT - 1, 0.0, 1.0).astype(BF16)

        rawwin[TT - RAW_HALO:TT, 0:db] = bxh_ref[...] * halo_on
        rawwin[TT - RAW_HALO:TT, db:xbc_w] = bch_ref[...] * halo_on
        rawwin[TT:2 * TT, 0:db] = bx_ref[...]
        rawwin[TT:2 * TT, db:xbc_w] = bc_ref[...]
        for cs, pre in _ssm_conv(rawwin, sh, s_ref, w_ref, b_ref, xbc_w):
            sl, dsl = _silu_and_grad(pre)
            xbuf[:, cs] = sl
            dsbuf[:, cs] = dsl
            yield

        z, dtv, a, ac, eac, dst = _head_scalars(dt_ref, dtb_ref, alog_ref)
        exbuf[...] = _dot(jnp.concatenate([dtv, eac, dst], axis=0).astype(BF16), e_ref[...])
        ac_t = ac.T
        causal = _tri(True)
        xdtbuf[...] = xbuf[:, 0:db] * exbuf[0:TT, :]

        yraw = yraw_ref[...]
        sz, dsz = _silu_and_grad(bz_ref[...].astype(F32))
        v = yraw * sz
        r = lax.rsqrt(jnp.mean(v * v, axis=-1, keepdims=True) + NORM_EPS)
        dy = dy_ref[...]
        dyg = dy * g_ref[...]
        dv = r * dyg - v * (r * r * r * jnp.mean(dyg * v, axis=-1, keepdims=True))
        dch_ref[0, 0:1, :] = dch_ref[0, 0:1, :] + jnp.sum(dy * v * r, axis=0, keepdims=True)
        dyr = dv * sz
        dyrbuf[...] = dyr
        dp_ref[:, 0:db] = (dv * yraw * dsz).astype(BF16)
        dch_ref[0, 1:2, :] = dch_ref[0, 1:2, :] + jnp.sum(dyr * xbuf[:, 0:db], axis=0, keepdims=True)

        lane_row = lax.broadcasted_iota(jnp.int32, (1, LANES), 1)
        sub_col = lax.broadcasted_iota(jnp.int32, (LANES, 1), 0)
        dac = jnp.zeros((TT, LANES), F32)
        colacc = jnp.zeros((LANES, TT), F32)
        for g in range(SSM_GROUPS):
            gs = slice(g * gw, (g + 1) * gw)
            bs_ = slice(db + g * SSM_STATE, db + (g + 1) * SSM_STATE)
            cs_ = slice(db + gn + g * SSM_STATE, db + gn + (g + 1) * SSM_STATE)
            bg = xbuf[:, bs_].astype(BF16)
            cg = xbuf[:, cs_].astype(BF16)
            cb = _dot_nt(cg, bg)
            dcb = jnp.zeros((TT, TT), F32)
            for e in range(0, hpg, 2):
                h = g * hpg + e
                ps = slice(h * SSM_HEAD_DIM, (h + 2) * SSM_HEAD_DIM)
                xp16 = xdtbuf[:, ps].astype(BF16)
                dyp16 = dyrbuf[:, ps].astype(BF16)
                acc = jnp.zeros((TT, LANES), F32)
                for hh in (h, h + 1):
                    dec = _decay(ac, ac_t, hh, causal)
                    mm = cb * dec
                    dyh = _own_half(dyp16, hh)
                    dmm = _dot_nt(dyh, xp16)
                    acc = acc + _dot_tn(mm.astype(BF16), dyh)
                    dcb = dcb + dmm * dec
                    gm = dmm * mm
                    dac = dac + jnp.sum(gm, axis=1, keepdims=True) * (lane_row == hh).astype(F32)
                    colacc = colacc + (sub_col == hh).astype(F32) * jnp.sum(gm, axis=0, keepdims=True)
                dxbuf[:, ps] = acc
                yield
            sg32 = sprev_ref[0, 0, :, gs]
            sg = sg32.astype(BF16)
            dsn = dstate[:, gs]
            dsn16 = dsn.astype(BF16)
            dcb16 = dcb.astype(BF16)
            eacx = exbuf[TT:2 * TT, gs]
            dstx = exbuf[2 * TT:3 * TT, gs]
            cdx = exbuf[2 * TT - 1:2 * TT, gs]
            dye16 = (dyrbuf[:, gs] * eacx).astype(BF16)
            xdt_g = xdtbuf[:, gs]
            dxbuf[:, cs_] = _dot(dcb16, bg) + _dot_nt(dye16, sg)
            dst_x = dstx * _dot(bg, dsn16)
            dxbuf[:, bs_] = _dot_tn(dcb16, cg) + _dot_nt((dstx * xdt_g).astype(BF16), dsn16)
            dstate[:, gs] = cdx * dsn + _dot_tn(cg, dye16)
            z1buf[:, gs] = dyrbuf[:, gs] * (eacx * _dot(cg, sg)) - xdt_g * dst_x
            uvec[:, gs] = jnp.broadcast_to(
                jnp.sum(xdt_g * dst_x, axis=0, keepdims=True) + jnp.sum(dsn * cdx * sg32, axis=0, keepdims=True),
                (8, gw))
            dxbuf[:, gs] = dxbuf[:, gs] + dst_x
            yield

        zz = _dot(jnp.concatenate([z1buf[...], dxbuf[:, 0:db] * xbuf[:, 0:db]], axis=0).astype(BF16), et_ref[...])
        u1, u2, u3 = _split3(uvec[...])
        ulast = (_dot(u1, et_ref[...]) + _dot(u2, et_ref[...]) + _dot(u3, et_ref[...]))[0:1, :]
        is_last = (lax.broadcasted_iota(jnp.int32, (TT, 1), 0) == TT - 1).astype(F32)
        dac = dac - colacc.T + zz[0:TT] + is_last * ulast
        dda = _exact_01_dot(_tri(False).astype(F32).astype(BF16), dac)
        ddt = dda * a + zz[TT:2 * TT]
        dhd_ref[0, 1:2, :] = dhd_ref[0, 1:2, :] + jnp.sum(dda * dtv, axis=0, keepdims=True) * a
        ddtraw = ddt * _sigmoid(z)
        dhd_ref[0, 0:1, :] = dhd_ref[0, 0:1, :] + jnp.sum(ddtraw, axis=0, keepdims=True)
        dpt_ref[...] = ddtraw.astype(BF16)
        dxbuf[:, 0:db] = dyrbuf[...] * dsk_ref[...] + dxbuf[:, 0:db] * exbuf[0:TT, :]

        for lb in range(xbc_w // LANES):
            cs = slice(lb * LANES, (lb + 1) * LANES)
            dpre = dxbuf[:, cs] * dsbuf[:, cs]
            dprebuf[0:TT, cs] = dpre
            dwc_ref[0, SSM_CONV_K:SSM_CONV_K + 1, cs] = dwc_ref[0, SSM_CONV_K:SSM_CONV_K + 1, cs] + jnp.sum(
                dpre, axis=0, keepdims=True)
            draw = w_ref[SSM_CONV_K - 1:SSM_CONV_K, cs] * dpre
            for k in range(SSM_CONV_K - 1):
                ahead = SSM_CONV_K - 1 - k
                draw = draw + w_ref[k:k + 1, cs] * dprebuf[ahead:ahead + TT, cs]
            for k in range(SSM_CONV_K):
                moved = sh[k * TT:(k + 1) * TT, cs] if k < SSM_CONV_K - 1 else rawwin[TT:2 * TT, cs].astype(F32)
                dwc_ref[0, k:k + 1, cs] = dwc_ref[0, k:k + 1, cs] + jnp.sum(dpre * moved, axis=0, keepdims=True)
            dp_ref[:, db + lb * LANES:db + (lb + 1) * LANES] = draw.astype(BF16)
            yield

    bl, nt = dm.BL, dm.NT
    tile = lambda w, k: pl.BlockSpec((bl, TT, w), lambda i: (0, nt - 1 - i, k))
    halo = lambda w, k: pl.BlockSpec((bl, HALO_BLOCK, w),
                                     lambda i: (0, jnp.maximum((nt - 1 - i) * (TT // HALO_BLOCK) - 1, 0), k))
    fixed = lambda r, w: pl.BlockSpec((r, w), lambda i: (0, 0))
    sums = lambda w: pl.BlockSpec((bl, 8, w), lambda i: (0, 0, 0))
    kz = dm.WA // db
    kc = (dm.WA + 2 * db) // (2 * gn)
    proj3, dt3 = proj.reshape(bl, dm.Lp, dm.NP), projdt.reshape(bl, dm.Lp, DT_PAD)
    scratch = [((2 * TT, xbc_w), BF16), (((SSM_CONV_K - 1) * TT, xbc_w), F32), ((TT, xbc_w), F32),
               ((TT, xbc_w), F32), ((SSM_STATE, db), F32), ((TT, xbc_w), F32), ((TT, db), F32),
               ((TT + SMALL_HALO, xbc_w), F32), ((3 * TT, db), F32), ((TT, db), F32), ((TT, db), F32), ((8, db), F32)]
    how = [True] * 8 + ["keep"] + [False] * 9 + [True, True, "keep", "keep", "keep"] + [True] * len(scratch)
    (dp, dpt, dwc, dch, dhd), rode = _call(
        _per_sequence(setup, body, bl, how), "mix_b_bwd", (nt,),
        [tile(db, kz), tile(db, kz + 1), tile(2 * gn, kc), tile(DT_PAD, 0),
         halo(db, kz + 1), halo(2 * gn, kc), tile(db, 0), tile(db, 0),
         pl.BlockSpec((bl, 1, SSM_STATE, db), lambda i: (0, nt - 1 - i, 0, 0)),
         fixed(SSM_CONV_K, xbc_w), fixed(1, xbc_w), fixed(1, DT_PAD), fixed(1, DT_PAD),
         fixed(1, db), fixed(1, db), fixed(DT_PAD, db), fixed(db, DT_PAD), fixed((SSM_CONV_K - 1) * TT, 2 * TT)],
        [tile(dm.WB, 0), tile(DT_PAD, 0), sums(xbc_w), sums(db), sums(DT_PAD)],
        [jax.ShapeDtypeStruct((bl, dm.Lp, dm.WB), BF16), jax.ShapeDtypeStruct((bl, dm.Lp, DT_PAD), BF16),
         jax.ShapeDtypeStruct((bl, 8, xbc_w), F32), jax.ShapeDtypeStruct((bl, 8, db), F32),
         jax.ShapeDtypeStruct((bl, 8, DT_PAD), F32)],
        [pltpu.VMEM((bl,) + s, t) for s, t in scratch],
        ("arbitrary",),
        (proj3, proj3, proj3, dt3, proj3, proj3, dyb.reshape(bl, dm.Lp, db), yraw.reshape(bl, dm.Lp, db), sprev,
         conv_w, conv_b, dt_bias, a_log, dskx, norm_g, expand, expand_t, _shift_matrix()), rider)
    return (dp.reshape(dm.R, dm.WB), dpt.reshape(dm.R, DT_PAD), dwc, dch, dhd), rode


def _head_consts(dm):
    head_of = jnp.arange(dm.DB) // SSM_HEAD_DIM
    expand = (jnp.arange(DT_PAD)[:, None] == head_of[None, :]).astype(BF16)
    return expand, expand.T


def _ssm_params(lw, dm):
    pad_h = lambda v: jnp.pad(v, (0, DT_PAD - dm.H))[None]
    return (lw["ssm_conv_w"], lw["ssm_conv_b"][None], pad_h(lw["dt_bias"]), pad_h(lw["a_log"]),
            jnp.repeat(lw["d_skip"], SSM_HEAD_DIM)[None], lw["ssm_norm_g"][None])


def _layer_fwd(h, lw, w_in, w_out, cst, dm, next_bases=None):
    nxt = next_bases is not None
    (proj, projdt, hn), got = _fwd_in(h, lw["pre_g"][None], w_in, dm,
                                      _ride_gather_ici(next_bases, 0, 2) if nxt else None)
    ya = _mix_a_fwd(proj, lw["conv_a_w"], dm)
    (yb, yraw, sprev), got = _mix_b_fwd(proj, projdt, *_ssm_params(lw, dm), cst[0], dm,
                                        _ride_gather_ici(got, 1, 2) if nxt else None)
    yc, u1 = _mix_c_fwd(proj, lw["conf_conv_w"], lw["conf_conv_b"][None], lw["conf_ln_g"][None],
                        lw["conf_ln_b"][None], dm)
    (h_new, m), got = _fwd_out(ya, yb, yc, w_out, h, lw["post_g"][None], dm, _ride_gather_d2d(got) if nxt else None)
    return h_new, (h, hn, proj, projdt, ya, yb, yc, u1, yraw, sprev, m), got


def _layer_bwd(dh, saved, lw, w_in, w_out, cst, dm, reduce=None, last=False):
    h_in, hn, proj, projdt, ya, yb, yc, u1, yraw, sprev, m = saved
    (dya, dyb, dyc, dwo, dpost), got = _bwd_out(dh, m, lw["post_g"][None], w_out, ya, yb, yc, dm,
                                                None if reduce is None else reduce.swap())
    dpa, dwa = _mix_a_bwd(proj, dya, lw["conv_a_w"], dm)
    (dpb, dpt, dwcv, dch, dhd), got = _mix_b_bwd(proj, projdt, dyb, yraw, sprev, *_ssm_params(lw, dm), cst[0],
                                                 cst[1], dm, None if reduce is None else reduce.to_owners(got))
    dpc, dwcf, dvc = _mix_c_bwd(proj, u1, dyc, lw["conf_conv_w"], lw["conf_ln_g"][None], lw["conf_ln_b"][None], dm)
    def own_reduce():
        pieces = _bwd_in_dw(hn, [dpa, dpb, dpc, dpt], dm)
        return _GradReduce([_grad_to_shards(pieces, dm), dwo.reshape(N_CHIPS, 2 * dm.D // N_CHIPS, dm.D)])

    rider = None if reduce is None else reduce.join(got)
    n_join = 0 if rider is None else len(rider.out_shapes)
    if last:
        mine = own_reduce()
        to_owners = mine.to_owners(_exchange("grad_swap_halves", mine.swap()))
        rider = to_owners if rider is None else _ride_both(rider, to_owners)
    (dh, dpre), got = _bwd_in_dx(dpa, dpb, dpc, dpt, w_in, h_in, dh, lw["pre_g"][None], dm, rider)
    if reduce is not None:
        reduce.finish(got[:n_join])
    if last:
        mine.finish(_exchange("grad_join_halves", mine.join(got[n_join:])))
    else:
        mine = own_reduce()
    dwcv, dch, dhd, dvc = (jnp.sum(a, axis=0) for a in (dwcv, dch, dhd, dvc))
    small = dict(pre_g=dpre[0], post_g=dpost[0], conv_a_w=jnp.sum(dwa, axis=0)[:CONV_A_K],
                 ssm_conv_w=dwcv[:SSM_CONV_K], ssm_conv_b=dwcv[SSM_CONV_K], ssm_norm_g=dch[0],
                 d_skip=jnp.sum(dch[1].reshape(dm.H, SSM_HEAD_DIM), axis=1), dt_bias=dhd[0, :dm.H],
                 a_log=dhd[1, :dm.H], conf_conv_w=jnp.sum(dwcf, axis=0)[:CONF_K], conf_conv_b=dvc[0],
                 conf_ln_g=dvc[1], conf_ln_b=dvc[2])
    return dh, mine, small


def _shard_runs(dm):
    ab = dm.WA + dm.WB
    order = [(0, 0, ab), (ab, dm.DT0, dm.H), (ab + dm.H, ab, dm.WC)]
    k = dm.NIN // N_CHIPS
    runs = []
    for s in range(N_CHIPS):
        for o0, m0, wd in order:
            lo, hi = max(o0, s * k), min(o0 + wd, (s + 1) * k)
            if lo < hi:
                runs.append((s, lo - s * k, m0 + lo - o0, hi - lo))
    return runs


def _w_in_from_shards(base, dm):
    tr = _row_tile(dm.D, 256)
    k = dm.NIN // N_CHIPS
    runs = _shard_runs(dm)

    def body(in_ref, out_ref):
        for s, sc, mc, wd in runs:
            out_ref[:, mc:mc + wd] = in_ref[s, :, sc:sc + wd]
        out_ref[:, dm.DT0 + dm.H:dm.NP] = jnp.zeros((tr, dm.NP - dm.DT0 - dm.H), BF16)

    return pl.pallas_call(
        body, name="w_in_from_shards", grid=(dm.D // tr,),
        in_specs=[pl.BlockSpec((N_CHIPS, tr, k), lambda r: (0, r, 0))],
        out_specs=pl.BlockSpec((tr, dm.NP), lambda r: (r, 0)),
        out_shape=jax.ShapeDtypeStruct((dm.D, dm.NP), BF16),
        compiler_params=_params(("parallel",)),
    )(base)


def _grad_to_shards(pieces, dm):
    tr = _row_tile(dm.D, 256)
    k = dm.NIN // N_CHIPS
    starts = [0, dm.WA, dm.WA + dm.WB, dm.DT0]
    widths = [dm.WA, dm.WB, dm.WC, DT_PAD]
    runs = _shard_runs(dm)

    def body(a_ref, b_ref, c_ref, t_ref, out_ref):
        refs = (a_ref, b_ref, c_ref, t_ref)
        for s, sc, mc, wd in runs:
            for p in range(4):
                lo, hi = max(mc, starts[p]), min(mc + wd, starts[p] + widths[p])
                if lo < hi:
                    out_ref[s, :, sc + lo - mc:sc + hi - mc] = refs[p][:, lo - starts[p]:hi - starts[p]].astype(BF16)

    return pl.pallas_call(
        body, name="grad_to_shards", grid=(dm.D // tr,),
        in_specs=[pl.BlockSpec((tr, w), lambda r: (r, 0)) for w in widths],
        out_specs=pl.BlockSpec((N_CHIPS, tr, k), lambda r: (0, r, 0)),
        out_shape=jax.ShapeDtypeStruct((N_CHIPS, dm.D, k), BF16),
        compiler_params=_params(("parallel",)),
    )(*pieces)


def _place_own(w, layer, me):
    _, rows, cols = w.shape
    tr = _row_tile(rows, 256)

    def body(me_ref, w_ref, out_ref):
        out_ref[0] = w_ref[0].astype(BF16)

    return pl.pallas_call(
        body, name="place_own",
        grid_spec=pltpu.PrefetchScalarGridSpec(
            num_scalar_prefetch=1, grid=(rows // tr,),
            in_specs=[pl.BlockSpec((1, tr, cols), lambda r, me_ref: (layer, r, 0))],
            out_specs=pl.BlockSpec((1, tr, cols), lambda r, me_ref: (me_ref[0], r, 0))),
        out_shape=jax.ShapeDtypeStruct((N_CHIPS, rows, cols), BF16),
        compiler_params=_params(("parallel",)),
    )(me, w)


def _add_halves(g, got, c, name):
    _, _, rows, cols = g.shape
    tr = _row_tile(rows, 256)

    def body(c_ref, g_ref, got_ref, out_ref):
        out_ref[0] = (g_ref[0, 0].astype(F32) + got_ref[0].astype(F32)).astype(BF16)

    return pl.pallas_call(
        body, name=name,
        grid_spec=pltpu.PrefetchScalarGridSpec(
            num_scalar_prefetch=1, grid=(N_CHIPS, rows // tr),
            in_specs=[pl.BlockSpec((1, 1, tr, cols), lambda s, r, c_ref: (s, c_ref[0], r, 0)),
                      pl.BlockSpec((1, tr, cols), lambda s, r, c_ref: (s, r, 0))],
            out_specs=pl.BlockSpec((1, tr, cols), lambda s, r, c_ref: (s, r, 0))),
        out_shape=jax.ShapeDtypeStruct((N_CHIPS, rows, cols), BF16),
        compiler_params=_params(("parallel", "parallel")),
    )(c, g, got)


def _add_owner(p, got, where, name):
    _, rows, cols = p.shape
    tr = _row_tile(rows, 256)

    def body(w_ref, p_ref, got_ref, out_ref):
        acc = p_ref[0].astype(F32)
        for j in range(3):
            acc = acc + got_ref[j].astype(F32)
        out_ref[0] = acc

    return pl.pallas_call(
        body, name=name,
        grid_spec=pltpu.PrefetchScalarGridSpec(
            num_scalar_prefetch=1, grid=(rows // tr,),
            in_specs=[pl.BlockSpec((1, tr, cols), lambda r, w_ref: (w_ref[0], r, 0)),
                      pl.BlockSpec((3, tr, cols), lambda r, w_ref: (0, r, 0))],
            out_specs=pl.BlockSpec((1, tr, cols), lambda r, w_ref: (w_ref[1], r, 0))),
        out_shape=jax.ShapeDtypeStruct((2, rows, cols), F32),
        compiler_params=_params(("parallel",)),
    )(where, p, got)


class _GradReduce:
    def __init__(self, gs):
        self.gs = [g.reshape((N_CHIPS, 2, g.shape[1] // 2) + g.shape[2:]) for g in gs]
        self.c = lax.axis_index("c").astype(jnp.int32).reshape(1)
        chip = (2 * lax.axis_index("x") + lax.axis_index("y")).astype(jnp.int32)
        self.where = jnp.stack([chip, self.c[0]])
        self.result = None

    def swap(self):
        return _ride_swap_halves(self.gs)

    def to_owners(self, got):
        self.ps = [_add_halves(g, r, self.c, "grad_add_sibling_" + n) for g, r, n in zip(self.gs, got, ("in", "out"))]
        return _ride_to_owners(self.ps)

    def join(self, got):
        qs = [_add_owner(p, r, self.where, "grad_add_chips_" + n) for p, r, n in zip(self.ps, got, ("in", "out"))]
        return _ride_join_halves(qs)

    def finish(self, got):
        self.result = [a.reshape((a.shape[0] * a.shape[1],) + a.shape[2:]) for a in got]


def _adamw_math(w, g, m, v):
    m = ADAM_B1 * m + (1.0 - ADAM_B1) * g
    v = ADAM_B2 * v + (1.0 - ADAM_B2) * (g * g)
    m_hat = m / (1.0 - ADAM_B1 ** ADAM_STEP)
    v_hat = v / (1.0 - ADAM_B2 ** ADAM_STEP)
    delta = -ADAM_LR * (m_hat / (jnp.sqrt(v_hat) + ADAM_EPS) + ADAM_WD * w)
    return delta, m, v


def _adamw_small(w, g, m, v, name):
    def body(w_ref, g_ref, m_ref, v_ref, d_out, m_out, v_out):
        d_out[...], m_out[...], v_out[...] = _adamw_math(w_ref[...], g_ref[...], m_ref[...], v_ref[...])

    shape = jax.ShapeDtypeStruct(w.shape, F32)
    return pl.pallas_call(body, name="adamw_" + name, out_shape=[shape, shape, shape],
                          compiler_params=_params())(w, g, m, v)


def _adamw_layer(i, w, g, m, v, prev, name):
    depth, rows, cols = w.shape
    tr = _row_tile(rows, 256)
    n_prev = 0 if prev is None else 4

    def body(*refs):
        w_ref, g_ref, m_ref, v_ref = refs[:4]
        g_out, d_out, m_out, v_out = refs[4 + n_prev:]
        gv = g_ref[...]
        g_out[0] = gv
        d_out[0], m_out[0], v_out[0] = _adamw_math(w_ref[0], gv, m_ref[0], v_ref[0])

    lay = pl.BlockSpec((1, tr, cols), lambda r: (i, r, 0))
    shape = jax.ShapeDtypeStruct(w.shape, F32)
    return pl.pallas_call(
        body, name="adamw_" + name, grid=(rows // tr,),
        in_specs=[lay, pl.BlockSpec((tr, cols), lambda r: (r, 0)), lay, lay] + [ANY] * n_prev,
        out_specs=[lay] * 4, out_shape=[shape] * 4,
        input_output_aliases={4 + k: k for k in range(n_prev)},
        compiler_params=_params(("parallel",)),
    )(w, g, m, v, *(prev or ()))


def _adamw_cols_major(w, gs, m, v, name):
    depth, rows, cols = w.shape
    tr = max(t for t in range(1, 129) if cols % t == 0)
    wt, mt, vt = (jnp.transpose(a, (2, 0, 1)) for a in (w, m, v))
    gt = jnp.stack([g.T for g in gs], axis=1)

    def body(w_ref, g_ref, m_ref, v_ref, g_out, d_out, m_out, v_out):
        gv = g_ref[...]
        g_out[...] = gv
        d_out[...], m_out[...], v_out[...] = _adamw_math(w_ref[...], gv, m_ref[...], v_ref[...])

    spec = pl.BlockSpec((tr, depth, rows), lambda r: (r, 0, 0))
    shape = jax.ShapeDtypeStruct((cols, depth, rows), F32)
    outs = pl.pallas_call(body, name="adamw_" + name, grid=(cols // tr,), in_specs=[spec] * 4, out_specs=[spec] * 4,
                          out_shape=[shape] * 4, compiler_params=_params(("parallel",)))(wt, gt, mt, vt)
    return [jnp.transpose(a, (1, 2, 0)) for a in outs]


def _sum_leading(buf, name):
    n, rows, cols = buf.shape
    tr = _row_tile(rows, rows)

    def body(in_ref, out_ref):
        acc = in_ref[0]
        for k in range(1, n):
            acc = acc + in_ref[k]
        out_ref[...] = acc

    return pl.pallas_call(
        body, name=name, grid=(rows // tr,),
        in_specs=[pl.BlockSpec((n, tr, cols), lambda i: (0, i, 0))],
        out_specs=pl.BlockSpec((tr, cols), lambda i: (i, 0)),
        out_shape=jax.ShapeDtypeStruct((rows, cols), F32),
        compiler_params=_params(("parallel",)),
    )(buf)


_SHARDED_SMALL = ("meta", "conv_a_w", "ssm_conv_w", "conf_conv_w")
_LAYER_SMALL = ("pre_g", "post_g", "conv_a_w", "ssm_conv_w", "ssm_conv_b", "dt_bias", "a_log", "d_skip",
                "ssm_norm_g", "conf_conv_w", "conf_conv_b", "conf_ln_g", "conf_ln_b")
_WEIGHTS = ("meta", "pre_g", "post_g", "w_in", "w_out", "conv_a_w", "ssm_conv_w", "ssm_conv_b", "dt_bias", "a_log",
            "d_skip", "ssm_norm_g", "conf_conv_w", "conf_conv_b", "conf_ln_g", "conf_ln_b")


def _shard_last(a):
    return jnp.moveaxis(a.reshape(a.shape[:-1] + (N_CHIPS, a.shape[-1] // N_CHIPS)), -2, 0)


def _with_own_block(a, n, at):
    return lax.dynamic_update_index_in_dim(jnp.zeros((n,) + a.shape, a.dtype), a, at, 0)


def _with_own_columns(a, chip):
    k = a.shape[-1]
    return lax.dynamic_update_slice_in_dim(jnp.zeros(a.shape[:-1] + (N_CHIPS * k,), a.dtype), a, chip * k, a.ndim - 1)


def kernel(x, meta, pre_g, post_g, w_in, w_out, conv_a_w, ssm_conv_w, ssm_conv_b, dt_bias, a_log, d_skip, ssm_norm_g, conf_conv_w, conf_conv_b, conf_ln_g, conf_ln_b, loss_target, m_meta, m_pre_g, m_post_g, m_w_in, m_w_out, m_conv_a_w, m_ssm_conv_w, m_ssm_conv_b, m_dt_bias, m_a_log, m_d_skip, m_ssm_norm_g, m_conf_conv_w, m_conf_conv_b, m_conf_ln_g, m_conf_ln_b, v_meta, v_pre_g, v_post_g, v_w_in, v_w_out, v_conv_a_w, v_ssm_conv_w, v_ssm_conv_b, v_dt_bias, v_a_log, v_d_skip, v_ssm_norm_g, v_conf_conv_w, v_conf_conv_b, v_conf_ln_g, v_conf_ln_b):
    w = dict(meta=meta, pre_g=pre_g, post_g=post_g, w_in=w_in, w_out=w_out, conv_a_w=conv_a_w,
             ssm_conv_w=ssm_conv_w, ssm_conv_b=ssm_conv_b, dt_bias=dt_bias, a_log=a_log, d_skip=d_skip,
             ssm_norm_g=ssm_norm_g, conf_conv_w=conf_conv_w, conf_conv_b=conf_conv_b, conf_ln_g=conf_ln_g,
             conf_ln_b=conf_ln_b)
    mom = dict(meta=m_meta, pre_g=m_pre_g, post_g=m_post_g, w_in=m_w_in, w_out=m_w_out, conv_a_w=m_conv_a_w,
               ssm_conv_w=m_ssm_conv_w, ssm_conv_b=m_ssm_conv_b, dt_bias=m_dt_bias, a_log=m_a_log, d_skip=m_d_skip,
               ssm_norm_g=m_ssm_norm_g, conf_conv_w=m_conf_conv_w, conf_conv_b=m_conf_conv_b,
               conf_ln_g=m_conf_ln_g, conf_ln_b=m_conf_ln_b)
    vel = dict(meta=v_meta, pre_g=v_pre_g, post_g=v_post_g, w_in=v_w_in, w_out=v_w_out, conv_a_w=v_conv_a_w,
               ssm_conv_w=v_ssm_conv_w, ssm_conv_b=v_ssm_conv_b, dt_bias=v_dt_bias, a_log=v_a_log, d_skip=v_d_skip,
               ssm_norm_g=v_ssm_norm_g, conf_conv_w=v_conf_conv_w, conf_conv_b=v_conf_conv_b,
               conf_ln_g=v_conf_ln_g, conf_ln_b=v_conf_ln_b)
    bl, seq, d = x.shape
    dm = Dims(bl, seq, d)
    depth = w_in.shape[0]
    chip = (2 * lax.axis_index("x") + lax.axis_index("y")).astype(jnp.int32)
    dev = 2 * chip + lax.axis_index("c").astype(jnp.int32)
    cst = _head_consts(dm)

    full = dict(w)
    full.update(zip(_SHARDED_SMALL, _exchange("gather_small_weights", _ride_gather_small(
        [_with_own_columns(w[n], chip) for n in _SHARDED_SMALL]))))

    bases = [[_place_own(w_in, i, chip.reshape(1)), _place_own(w_out, i, chip.reshape(1))] for i in range(depth)]
    gathered = _exchange("gather_d2d_first", _ride_gather_d2d(_gather_ici_relayed(bases[0])))
    h = _embed(x, full["meta"], dm)
    saved, proj_w = [], []
    for i in range(depth):
        lw = {n: full[n][i] for n in _LAYER_SMALL}
        proj_w.append((_w_in_from_shards(gathered[0], dm), gathered[1].reshape(2 * d, d)))
        h, keep, gathered = _layer_fwd(h, lw, proj_w[i][0], proj_w[i][1], cst, dm,
                                       bases[i + 1] if i + 1 < depth else None)
        saved.append(keep)

    dh, loss = _loss_head(h, loss_target, dm)
    loss = lax.psum(loss, ("x", "y", "c"))

    small_g = {n: [None] * depth for n in _LAYER_SMALL}
    big = {"w_in": None, "w_out": None}
    g_in = [None] * depth
    reduce = None
    for i in reversed(range(depth)):
        lw = {n: full[n][i] for n in _LAYER_SMALL}
        dh, mine, sg = _layer_bwd(dh, saved[i], lw, proj_w[i][0], proj_w[i][1], cst, dm, reduce, last=i == 0)
        for n in _LAYER_SMALL:
            small_g[n][i] = sg[n]
        if reduce is not None:
            g_in[i + 1] = reduce.result[0]
            big["w_out"] = _adamw_layer(i + 1, w_out, reduce.result[1], m_w_out, v_w_out, big["w_out"], "w_out")
        reduce = mine
    g_in[0] = reduce.result[0]
    big["w_out"] = _adamw_layer(0, w_out, reduce.result[1], m_w_out, v_w_out, big["w_out"], "w_out")
    grad_x, gmeta = _unembed(dh, dm)

    g = {n: jnp.stack(v) for n, v in small_g.items()}
    g["meta"] = gmeta
    small = [n for n in _WEIGHTS if n not in ("w_in", "w_out")]
    flat = jnp.concatenate([g[n].reshape(-1) for n in small])
    rows = -(-flat.shape[0] // (16 * LANES)) * 16
    flat = jnp.pad(flat, (0, rows * LANES - flat.shape[0])).reshape(rows, LANES)
    parts = _gather_all(_with_own_block(flat, N_DEV, dev))
    total = _sum_leading(parts, "small_grads_sum").reshape(-1)
    big["w_in"] = _adamw_cols_major(w_in, g_in, m_w_in, v_w_in, "w_in")
    grads, deltas, new_m, new_v = {}, {}, {}, {}
    off = 0
    for n in small:
        size = g[n].size
        fullg = total[off:off + size].reshape(g[n].shape)
        off += size
        if n in _SHARDED_SMALL:
            fullg = lax.dynamic_index_in_dim(_shard_last(fullg), chip, axis=0, keepdims=False)
        grads[n] = fullg
        deltas[n], new_m[n], new_v[n] = _adamw_small(w[n], fullg, mom[n], vel[n], n)
    for n in ("w_in", "w_out"):
        grads[n], deltas[n], new_m[n], new_v[n] = big[n]

    return (loss, grad_x, *[grads[n] for n in _WEIGHTS], *[deltas[n] for n in _WEIGHTS],
            *[new_m[n] for n in _WEIGHTS], *[new_v[n] for n in _WEIGHTS])
```

```python
import jax
import jax.numpy as jnp
from jax import lax
from jax.experimental import pallas as pl
from jax.experimental.pallas import tpu as pltpu

F32 = jnp.float32
BF16 = jnp.bfloat16

N_META = 16
TT = 128
SSM_STATE = 128
SSM_GROUPS = 2
SSM_HEAD_DIM = 64
CONV_A_K = 3
SSM_CONV_K = 4
CONF_K = 31
NORM_EPS = 1e-6
LN_EPS = 1e-5
LANES = 128
MXU_DIM = 256
DT_PAD = LANES
CONF_HALO = 32
SMALL_HALO = 8
VMEM_LIMIT = 56 * 1024 * 1024
N_CHIPS = 4
N_DEV = 8

ADAM_LR = 0.001
ADAM_B1 = 0.9
ADAM_B2 = 0.999
ADAM_EPS = 1e-08
ADAM_WD = 0.01
ADAM_STEP = 10

MESH = pl.DeviceIdType.MESH
ANY = pl.BlockSpec(memory_space=pl.ANY)


class Dims:
    def __init__(self, bl, seq, d):
        self.BL, self.S, self.D = bl, seq, d
        self.L = seq + N_META
        self.Lp = -(-self.L // TT) * TT
        self.NT = self.Lp // TT
        self.R = bl * self.Lp
        self.DA = d // 2
        self.DB = d
        self.DC = d // 2
        self.H = self.DB // SSM_HEAD_DIM
        self.HPG = self.H // SSM_GROUPS
        self.GN = SSM_GROUPS * SSM_STATE
        self.WA = 4 * self.DA
        self.WB = 2 * self.DB + 2 * self.GN
        self.WC = 3 * self.DC
        self.DT0 = self.WA + self.WB + self.WC
        self.NP = -(-(self.DT0 + DT_PAD) // (5 * MXU_DIM)) * (5 * MXU_DIM)
        self.NIN = self.WA + self.WB + self.H + self.WC
        self.XBC = self.DB + 2 * self.GN
        assert self.H % 2 == 0 and self.HPG % 2 == 0 and self.H <= DT_PAD
        assert self.DA % LANES == 0 and (self.WA + self.WB) % self.DC == 0 and self.WA % self.DB == 0


def _row_tile(n, target):
    best = None
    for t in range(16, min(n, target) + 1, 16):
        if n % t == 0:
            best = t
    assert best is not None
    return best


def _col_tile(n, target):
    best = None
    for t in range(LANES, min(n, target) + 1, LANES):
        if n % t == 0:
            best = t
    assert best is not None
    return best


def _params(sem=None):
    return pltpu.CompilerParams(dimension_semantics=sem, vmem_limit_bytes=VMEM_LIMIT)


def _sigmoid(x):
    return 1.0 / (1.0 + jnp.exp(-x))


def _silu_and_grad(x):
    s = _sigmoid(x)
    return x * s, s * (1.0 + x * (1.0 - s))


def _dot(a, b):
    return jnp.dot(a, b, preferred_element_type=F32)


def _dot_nt(a, b):
    return lax.dot_general(a, b, (((1,), (1,)), ((), ())), preferred_element_type=F32)


def _dot_tn(a, b):
    return lax.dot_general(a, b, (((0,), (0,)), ((), ())), preferred_element_type=F32)


def _split3(x):
    x1 = x.astype(BF16)
    r1 = x - x1.astype(F32)
    x2 = r1.astype(BF16)
    x3 = (r1 - x2.astype(F32)).astype(BF16)
    return x1, x2, x3


class Rider:
    def __init__(self, plan, ins, out_shapes, aliases, nsem):
        self.plan, self.ins, self.out_shapes, self.aliases, self.nsem = plan, list(ins), list(out_shapes), aliases, nsem


def _place():
    x, y, c = lax.axis_index("x"), lax.axis_index("y"), lax.axis_index("c")
    chips = [(1 - x, y), (x, 1 - y), (1 - x, 1 - y)]
    return x, y, c, chips


def _remote(k, src, dst, to, send_sems, recv_sems):
    return pltpu.make_async_remote_copy(src_ref=src, dst_ref=dst, send_sem=send_sems.at[k], recv_sem=recv_sems.at[k],
                                        device_id=to, device_id_type=MESH)


def _call(body, name, grid, in_specs, out_specs, out_shape, scratch_shapes, sem, args, rider=None):
    if rider is None:
        outs = pl.pallas_call(body, name=name, grid=grid, in_specs=in_specs, out_specs=out_specs, out_shape=out_shape,
                              scratch_shapes=scratch_shapes, compiler_params=_params(sem))(*args)
        return list(outs), []
    n_in, n_out, n_scr = len(args), len(out_shape), len(scratch_shapes)
    r_in, r_out = len(rider.ins), len(rider.out_shapes)

    def hosted(*refs):
        ins, rins = refs[:n_in], refs[n_in:n_in + r_in]
        o0 = n_in + r_in
        outs, routs = refs[o0:o0 + n_out], refs[o0 + n_out:o0 + n_out + r_out]
        scr = refs[o0 + n_out + r_out:o0 + n_out + r_out + n_scr]
        send_sems, recv_sems = refs[o0 + n_out + r_out + n_scr:]
        first = pl.program_id(0) == 0
        last = pl.program_id(0) == grid[0] - 1
        for ax in range(1, len(grid)):
            first = jnp.logical_and(first, pl.program_id(ax) == 0)
            last = jnp.logical_and(last, pl.program_id(ax) == grid[ax] - 1)

        @pl.when(first)
        def _():
            starts, _ = rider.plan(rins, routs, send_sems, recv_sems)
            for cp in starts:
                cp.start()

        body(*ins, *outs, *scr)

        @pl.when(last)
        def _():
            _, waits = rider.plan(rins, routs, send_sems, recv_sems)
            for wait in waits:
                wait()

    res = pl.pallas_call(
        hosted, name=name, grid=grid,
        in_specs=list(in_specs) + [ANY] * r_in, out_specs=list(out_specs) + [ANY] * r_out,
        out_shape=list(out_shape) + rider.out_shapes,
        input_output_aliases={n_in + k: n_out + v for k, v in rider.aliases.items()},
        scratch_shapes=list(scratch_shapes) + [pltpu.SemaphoreType.DMA((rider.nsem,)),
                                               pltpu.SemaphoreType.DMA((rider.nsem,))],
        compiler_params=_params(("arbitrary",) * len(grid)),
    )(*args, *rider.ins)
    return list(res[:n_out]), list(res[n_out:])


def _exchange(name, rider):
    r_in, r_out = len(rider.ins), len(rider.out_shapes)

    def body(*refs):
        rins, routs = refs[:r_in], refs[r_in:r_in + r_out]
        send_sems, recv_sems = refs[r_in + r_out:]
        starts, waits = rider.plan(rins, routs, send_sems, recv_sems)
        for cp in starts:
            cp.start()
        for wait in waits:
            wait()

    res = pl.pallas_call(
        body, name=name, in_specs=[ANY] * r_in, out_specs=[ANY] * r_out, out_shape=rider.out_shapes,
        input_output_aliases=dict(rider.aliases),
        scratch_shapes=[pltpu.SemaphoreType.DMA((rider.nsem,)), pltpu.SemaphoreType.DMA((rider.nsem,))],
    )(*rider.ins)
    return list(res)


def _same(arrays):
    return [jax.ShapeDtypeStruct(a.shape, a.dtype) for a in arrays]


class _SemsFrom:
    def __init__(self, sems, first):
        self.sems, self.first = sems, first

    @property
    def at(self):
        return self

    def __getitem__(self, k):
        return self.sems.at[self.first + k]


def _ride_both(r1, r2):
    n_in, n_out = len(r1.ins), len(r1.out_shapes)

    def plan(ins, outs, ss, rs):
        s1, w1 = r1.plan(ins[:n_in], outs[:n_out], ss, rs)
        s2, w2 = r2.plan(ins[n_in:], outs[n_out:], _SemsFrom(ss, r1.nsem), _SemsFrom(rs, r1.nsem))
        return s1 + s2, w1 + w2

    aliases = dict(r1.aliases)
    aliases.update({n_in + k: n_out + v for k, v in r2.aliases.items()})
    return Rider(plan, r1.ins + r2.ins, r1.out_shapes + r2.out_shapes, aliases, r1.nsem + r2.nsem)


def _ride_gather_ici(bases, part=0, nparts=1):
    n = len(bases)

    def plan(ins, outs, ss, rs):
        x, y, c, chips = _place()
        me = 2 * x + y
        starts, waits = [], []
        for a in range(n):
            half = outs[a].shape[1] // 2
            mine = pl.ds(c * half + part * (half // nparts), half // nparts)
            for j, chip in enumerate(chips):
                cp = _remote(3 * a + j, outs[a].at[me, mine], outs[a].at[me, mine], (*chip, c), ss, rs)
                got = outs[a].at[2 * chip[0] + chip[1], mine]
                starts.append(cp)
                waits += [cp.wait_send, _remote(3 * a + j, got, got, (*chip, c), ss, rs).wait_recv]
        return starts, waits

    return Rider(plan, bases, _same(bases), {a: a for a in range(n)}, 3 * n)


def _gather_ici_relayed(bases, also):
    n, m = len(bases), len(also.ins)

    def body(*refs):
        outs = refs[n + m:2 * n + m]
        ss, rs = refs[2 * (n + m):]
        beside, beside_waits = also.plan(refs[n:n + m], refs[2 * n + m:2 * (n + m)],
                                         _SemsFrom(ss, 4 * n), _SemsFrom(rs, 4 * n))
        for cp in beside:
            cp.start()
        x, y, c, _ = _place()
        me, xn, yn, dg = 2 * x + y, 2 * (1 - x) + y, 2 * x + (1 - y), 2 * (1 - x) + (1 - y)
        to_x, to_y = (1 - x, y, c), (x, 1 - y, c)
        sends = []

        def send(k, piece, to):
            cp = _remote(k, piece, piece, to, ss, rs)
            cp.start()
            sends.append(cp)

        def arrived(k, piece, frm):
            _remote(k, piece, piece, frm, ss, rs).wait_recv()

        rows = []
        for a in range(n):
            half = outs[a].shape[1] // 2
            rows.append((pl.ds(c * half, half), pl.ds(c * half, half // 2), pl.ds(c * half + half // 2, half // 2)))
            send(4 * a, outs[a].at[me, rows[a][0]], to_x)
            send(4 * a + 1, outs[a].at[me, rows[a][0]], to_y)
        for a in range(n):
            mine, lo, hi = rows[a]
            arrived(4 * a, outs[a].at[xn, mine], to_x)
            send(4 * a + 2, outs[a].at[xn, lo], to_y)
            arrived(4 * a + 1, outs[a].at[yn, mine], to_y)
            send(4 * a + 3, outs[a].at[yn, hi], to_x)
        for a in range(n):
            mine, lo, hi = rows[a]
            arrived(4 * a + 2, outs[a].at[dg, lo], to_y)
            arrived(4 * a + 3, outs[a].at[dg, hi], to_x)
        for cp in sends:
            cp.wait_send()
        for wait in beside_waits:
            wait()

    aliases = {a: a for a in range(n)}
    aliases.update({n + k: n + v for k, v in also.aliases.items()})
    nsem = 4 * n + also.nsem
    res = pl.pallas_call(
        body, name="gather_ici_first", in_specs=[ANY] * (n + m), out_specs=[ANY] * (n + len(also.out_shapes)),
        out_shape=_same(bases) + also.out_shapes, input_output_aliases=aliases,
        scratch_shapes=[pltpu.SemaphoreType.DMA((nsem,)), pltpu.SemaphoreType.DMA((nsem,))],
    )(*bases, *also.ins)
    return list(res[:n]), list(res[n:])


def _ride_gather_d2d(bases):
    n = len(bases)

    def plan(ins, outs, ss, rs):
        x, y, c, chips = _place()
        sib = (x, y, 1 - c)
        starts, waits = [], []
        for a in range(n):
            half = outs[a].shape[1] // 2
            for j, chip in enumerate(chips):
                frm = 2 * chip[0] + chip[1]
                got = outs[a].at[frm, pl.ds(c * half, half)]
                theirs = outs[a].at[frm, pl.ds((1 - c) * half, half)]
                cp = _remote(3 * a + j, got, got, sib, ss, rs)
                starts.append(cp)
                waits += [cp.wait_send, _remote(3 * a + j, theirs, theirs, sib, ss, rs).wait_recv]
        return starts, waits

    return Rider(plan, bases, _same(bases), {a: a for a in range(n)}, 3 * n)


def _ride_gather_small(bases):
    n = len(bases)

    def plan(ins, outs, ss, rs):
        x, y, c, chips = _place()
        me = 2 * x + y
        starts, waits = [], []
        for a in range(n):
            k = outs[a].shape[-1] // N_CHIPS
            lead = (slice(None),) * (len(outs[a].shape) - 1)
            at = (lambda s: pl.multiple_of(s * k, LANES)) if k % LANES == 0 else (lambda s: s * k)
            cols = lambda s: outs[a].at[lead + (pl.ds(at(s), k),)]
            for j, chip in enumerate(chips):
                cp = _remote(3 * a + j, cols(me), cols(me), (*chip, c), ss, rs)
                got = cols(2 * chip[0] + chip[1])
                starts.append(cp)
                waits += [cp.wait_send, _remote(3 * a + j, got, got, (*chip, c), ss, rs).wait_recv]
        return starts, waits

    return Rider(plan, bases, _same(bases), {a: a for a in range(n)}, 3 * n)


def _ride_swap_halves(gs):
    n = len(gs)

    def plan(ins, outs, ss, rs):
        x, y, c, _ = _place()
        cps = [_remote(a, ins[a].at[:, 1 - c], outs[a], (x, y, 1 - c), ss, rs) for a in range(n)]
        return cps, [cp.wait for cp in cps]

    shapes = [jax.ShapeDtypeStruct((g.shape[0],) + g.shape[2:], g.dtype) for g in gs]
    return Rider(plan, gs, shapes, {}, n)


def _ride_to_owners(ps):
    n = len(ps)

    def plan(ins, outs, ss, rs):
        x, y, c, chips = _place()
        cps = []
        for a in range(n):
            for j, chip in enumerate(chips):
                cps.append(_remote(3 * a + j, ins[a].at[2 * chip[0] + chip[1]], outs[a].at[j], (*chip, c), ss, rs))
        return cps, [cp.wait for cp in cps]

    shapes = [jax.ShapeDtypeStruct((3,) + p.shape[1:], p.dtype) for p in ps]
    return Rider(plan, ps, shapes, {}, 3 * n)


def _ride_join_halves(qs):
    n = len(qs)

    def plan(ins, outs, ss, rs):
        x, y, c, _ = _place()
        sib = (x, y, 1 - c)
        starts, waits = [], []
        for a in range(n):
            cp = _remote(a, outs[a].at[c], outs[a].at[c], sib, ss, rs)
            starts.append(cp)
            waits += [cp.wait_send, _remote(a, outs[a].at[1 - c], outs[a].at[1 - c], sib, ss, rs).wait_recv]
        return starts, waits

    return Rider(plan, qs, _same(qs), {a: a for a in range(n)}, n)


def _gather_all(base):
    def body(in_ref, out_ref, ss, rs):
        x, y, c, chips = _place()
        sib = (x, y, 1 - c)
        block = lambda cx, cy, cc: out_ref.at[4 * cx + 2 * cy + cc]
        mine = block(x, y, c)
        first = [_remote(j, mine, mine, (*chip, c), ss, rs) for j, chip in enumerate(chips)]
        first.append(_remote(3, mine, mine, sib, ss, rs))
        for cp in first:
            cp.start()
        passed = []
        for j, chip in enumerate(chips):
            got = block(*chip, c)
            _remote(j, got, got, (*chip, c), ss, rs).wait_recv()
            passed.append(_remote(4 + j, got, got, sib, ss, rs))
            passed[-1].start()
        theirs = block(x, y, 1 - c)
        _remote(3, theirs, theirs, sib, ss, rs).wait_recv()
        for j, chip in enumerate(chips):
            got = block(*chip, 1 - c)
            _remote(4 + j, got, got, sib, ss, rs).wait_recv()
        for cp in first + passed:
            cp.wait_send()

    return pl.pallas_call(
        body, name="small_grads_gather_all", in_specs=[ANY], out_specs=ANY,
        out_shape=jax.ShapeDtypeStruct(base.shape, base.dtype), input_output_aliases={0: 0},
        scratch_shapes=[pltpu.SemaphoreType.DMA((N_DEV - 1,)), pltpu.SemaphoreType.DMA((N_DEV - 1,))],
    )(base)


def _embed(x, meta, dm, rider=None):
    dc = _col_tile(dm.D, 256)
    s, lp = dm.S, dm.Lp

    def body(x_ref, meta_ref, h_ref):
        h_ref[0:N_META, :] = meta_ref[...]
        h_ref[N_META:N_META + s, :] = x_ref[0]
        if lp > N_META + s:
            h_ref[N_META + s:lp, :] = jnp.zeros((lp - N_META - s, dc), F32)

    (h,), rode = _call(
        body, "embed", (dm.BL, dm.D // dc),
        [pl.BlockSpec((1, s, dc), lambda b, j: (b, 0, j)), pl.BlockSpec((N_META, dc), lambda b, j: (0, j))],
        [pl.BlockSpec((lp, dc), lambda b, j: (b, j))], [jax.ShapeDtypeStruct((dm.R, dm.D), F32)],
        [], ("parallel", "parallel"), (x, meta), rider)
    return h, rode


def _loss_head(h, target, dm):
    dc = _col_tile(dm.D, 256)
    s, lp, nj = dm.S, dm.Lp, dm.D // dc

    def body(h_ref, t_ref, dh_ref, l_ref):
        diff = h_ref[N_META:N_META + s, :] - t_ref[0]
        dh_ref[0:N_META, :] = jnp.zeros((N_META, dc), F32)
        dh_ref[N_META:N_META + s, :] = diff * (1.0 / dm.D)
        if lp > N_META + s:
            dh_ref[N_META + s:lp, :] = jnp.zeros((lp - N_META - s, dc), F32)
        l_ref[...] = jnp.full((8, LANES), (0.5 / dm.D) * jnp.sum(diff * diff), F32)

    dh, part = pl.pallas_call(
        body, name="loss_head", grid=(dm.BL, nj),
        in_specs=[pl.BlockSpec((lp, dc), lambda b, j: (b, j)),
                  pl.BlockSpec((1, s, dc), lambda b, j: (b, 0, j))],
        out_specs=[pl.BlockSpec((lp, dc), lambda b, j: (b, j)),
                   pl.BlockSpec((8, LANES), lambda b, j: (b * nj + j, 0))],
        out_shape=[jax.ShapeDtypeStruct((dm.R, dm.D), F32),
                   jax.ShapeDtypeStruct((dm.BL * nj * 8, LANES), F32)],
        compiler_params=_params(("parallel", "parallel")),
    )(h, target)
    return dh, jnp.sum(part[::8, 0])


def _unembed(dh, dm):
    dc = _col_tile(dm.D, 256)
    s, lp = dm.S, dm.Lp

    def body(dh_ref, gx_ref, gm_ref):
        gx_ref[0] = dh_ref[N_META:N_META + s, :]

        @pl.when(pl.program_id(1) == 0)
        def _():
            gm_ref[...] = dh_ref[0:N_META, :]

        @pl.when(pl.program_id(1) > 0)
        def _():
            gm_ref[...] = gm_ref[...] + dh_ref[0:N_META, :]

    return pl.pallas_call(
        body, name="unembed", grid=(dm.D // dc, dm.BL),
        in_specs=[pl.BlockSpec((lp, dc), lambda j, b: (b, j))],
        out_specs=[pl.BlockSpec((1, s, dc), lambda j, b: (b, 0, j)),
                   pl.BlockSpec((N_META, dc), lambda j, b: (0, j))],
        out_shape=[jax.ShapeDtypeStruct((dm.BL, s, dm.D), F32),
                   jax.ShapeDtypeStruct((N_META, dm.D), F32)],
        compiler_params=_params(("parallel", "arbitrary")),
    )(dh)


def _fwd_in(h, pre_g, w, dm, rider=None):
    tm = _row_tile(dm.R, 1088)
    tn = _col_tile(dm.NP, 5 * MXU_DIM)
    nj = dm.NP // tn

    def body(h_ref, g_ref, w_ref, wdt_ref, proj_ref, dt_ref, hn_ref):
        @pl.when(pl.program_id(1) == 0)
        def _():
            xf = h_ref[...]
            r = lax.rsqrt(jnp.mean(xf * xf, axis=-1, keepdims=True) + NORM_EPS)
            hn_ref[...] = (xf * r * g_ref[...]).astype(BF16)
            dt_ref[...] = _dot(hn_ref[...], wdt_ref[...])

        proj_ref[...] = _dot(hn_ref[...], w_ref[...]).astype(BF16)

    return _call(
        body, "fwd_in", (dm.R // tm, nj),
        [pl.BlockSpec((tm, dm.D), lambda i, j: (i, 0)),
         pl.BlockSpec((1, dm.D), lambda i, j: (0, 0)),
         pl.BlockSpec((dm.D, tn), lambda i, j: (0, j)),
         pl.BlockSpec((dm.D, DT_PAD), lambda i, j: (0, dm.DT0 // DT_PAD))],
        [pl.BlockSpec((tm, tn), lambda i, j: (i, j)),
         pl.BlockSpec((tm, DT_PAD), lambda i, j: (i, 0)),
         pl.BlockSpec((tm, dm.D), lambda i, j: (i, 0))],
        [jax.ShapeDtypeStruct((dm.R, dm.NP), BF16), jax.ShapeDtypeStruct((dm.R, DT_PAD), F32),
         jax.ShapeDtypeStruct((dm.R, dm.D), BF16)],
        [], ("parallel", "arbitrary"), (h, pre_g, w, w), rider)


def _fwd_out(ya, yb, yc, w_out, h, post_g, dm, rider=None):
    tm = _row_tile(dm.Lp, 544)
    tiles_per_seq = dm.Lp // tm
    da, db, dc = dm.DA, dm.DB, dm.DC

    def body(ya_ref, yb_ref, yc_ref, w_ref, h_ref, g_ref, hn_ref, m_ref):
        m = _dot(ya_ref[...], w_ref[0:da, :])
        m = m + _dot(yb_ref[...], w_ref[da:da + db, :])
        m = m + _dot(yc_ref[...], w_ref[da + db:da + db + dc, :])
        m_ref[...] = m
        r = lax.rsqrt(jnp.mean(m * m, axis=-1, keepdims=True) + NORM_EPS)
        t = (pl.program_id(0) % tiles_per_seq) * tm + lax.broadcasted_iota(jnp.int32, (tm, 1), 0)
        keep = (t < dm.L).astype(F32)
        hn_ref[...] = (h_ref[...] + m * r * g_ref[...]) * keep

    row = lambda i: (i, 0)
    fixed = lambda i: (0, 0)
    return _call(
        body, "fwd_out", (dm.R // tm,),
        [pl.BlockSpec((tm, da), row), pl.BlockSpec((tm, db), row), pl.BlockSpec((tm, dc), row),
         pl.BlockSpec((2 * dm.D, dm.D), fixed), pl.BlockSpec((tm, dm.D), row), pl.BlockSpec((1, dm.D), fixed)],
        [pl.BlockSpec((tm, dm.D), row), pl.BlockSpec((tm, dm.D), row)],
        [jax.ShapeDtypeStruct((dm.R, dm.D), F32), jax.ShapeDtypeStruct((dm.R, dm.D), F32)],
        [], ("parallel",), (ya, yb, yc, w_out, h, post_g), rider)


def _bwd_out(dh, m, post_g, w_out, ya, yb, yc, dm, rider=None):
    tm = _row_tile(dm.R, MXU_DIM)
    da, db, dc = dm.DA, dm.DB, dm.DC

    def body(dh_ref, m_ref, g_ref, w_ref, ya_ref, yb_ref, yc_ref, dya_ref, dyb_ref, dyc_ref, dw_ref, dg_ref):
        @pl.when(pl.program_id(0) == 0)
        def _():
            dw_ref[...] = jnp.zeros_like(dw_ref)
            dg_ref[...] = jnp.zeros_like(dg_ref)

        m = m_ref[...]
        dh_ = dh_ref[...]
        r = lax.rsqrt(jnp.mean(m * m, axis=-1, keepdims=True) + NORM_EPS)
        n = m * r
        dg_ref[0:1, :] = dg_ref[0:1, :] + jnp.sum(dh_ * n, axis=0, keepdims=True)
        dn = dh_ * g_ref[...]
        dm_ = (r * (dn - n * jnp.mean(dn * n, axis=-1, keepdims=True))).astype(BF16)
        dya_ref[...] = _dot_nt(dm_, w_ref[0:da, :])
        dyb_ref[...] = _dot_nt(dm_, w_ref[da:da + db, :])
        dyc_ref[...] = _dot_nt(dm_, w_ref[da + db:da + db + dc, :])
        dw_ref[0:da, :] = dw_ref[0:da, :] + _dot_tn(ya_ref[...], dm_)
        dw_ref[da:da + db, :] = dw_ref[da:da + db, :] + _dot_tn(yb_ref[...], dm_)
        dw_ref[da + db:da + db + dc, :] = dw_ref[da + db:da + db + dc, :] + _dot_tn(yc_ref[...], dm_)

    row = lambda i: (i, 0)
    fixed = lambda i: (0, 0)
    return _call(
        body, "bwd_out", (dm.R // tm,),
        [pl.BlockSpec((tm, dm.D), row), pl.BlockSpec((tm, dm.D), row), pl.BlockSpec((1, dm.D), fixed),
         pl.BlockSpec((2 * dm.D, dm.D), fixed),
         pl.BlockSpec((tm, da), row), pl.BlockSpec((tm, db), row), pl.BlockSpec((tm, dc), row)],
        [pl.BlockSpec((tm, da), row), pl.BlockSpec((tm, db), row), pl.BlockSpec((tm, dc), row),
         pl.BlockSpec((2 * dm.D, dm.D), fixed), pl.BlockSpec((8, dm.D), fixed)],
        [jax.ShapeDtypeStruct((dm.R, da), F32), jax.ShapeDtypeStruct((dm.R, db), F32),
         jax.ShapeDtypeStruct((dm.R, dc), F32),
         jax.ShapeDtypeStruct((2 * dm.D, dm.D), F32), jax.ShapeDtypeStruct((8, dm.D), F32)],
        [], ("arbitrary",), (dh, m, post_g, w_out, ya, yb, yc), rider)


def _bwd_in_dx(dpa, dpb, dpc, dpt, w, h, dh, pre_g, dm, rider=None):
    tm = _row_tile(dm.R, 272)
    wa, wb, wc = dm.WA, dm.WB, dm.WC

    def body(dpa_ref, dpb_ref, dpc_ref, dpt_ref, w_ref, h_ref, dh_ref, g_ref, out_ref, dg_ref):
        @pl.when(pl.program_id(0) == 0)
        def _():
            dg_ref[...] = jnp.zeros_like(dg_ref)

        dhn = _dot_nt(dpa_ref[...], w_ref[:, 0:wa])
        dhn = dhn + _dot_nt(dpb_ref[...], w_ref[:, wa:wa + wb])
        dhn = dhn + _dot_nt(dpc_ref[...], w_ref[:, wa + wb:wa + wb + wc])
        dhn = dhn + _dot_nt(dpt_ref[...], w_ref[:, wa + wb + wc:wa + wb + wc + DT_PAD])
        xf = h_ref[...]
        r = lax.rsqrt(jnp.mean(xf * xf, axis=-1, keepdims=True) + NORM_EPS)
        n = xf * r
        dg_ref[0:1, :] = dg_ref[0:1, :] + jnp.sum(dhn * n, axis=0, keepdims=True)
        dn = dhn * g_ref[...]
        out_ref[...] = dh_ref[...] + r * (dn - n * jnp.mean(dn * n, axis=-1, keepdims=True))

    row = lambda i: (i, 0)
    fixed = lambda i: (0, 0)
    return _call(
        body, "bwd_in_dx", (dm.R // tm,),
        [pl.BlockSpec((tm, wa), row), pl.BlockSpec((tm, wb), row), pl.BlockSpec((tm, wc), row),
         pl.BlockSpec((tm, DT_PAD), row), pl.BlockSpec((dm.D, dm.NP), fixed),
         pl.BlockSpec((tm, dm.D), row), pl.BlockSpec((tm, dm.D), row), pl.BlockSpec((1, dm.D), fixed)],
        [pl.BlockSpec((tm, dm.D), row), pl.BlockSpec((8, dm.D), fixed)],
        [jax.ShapeDtypeStruct((dm.R, dm.D), F32), jax.ShapeDtypeStruct((8, dm.D), F32)],
        [], ("arbitrary",), (dpa, dpb, dpc, dpt, w, h, dh, pre_g), rider)


def _bwd_in_dw(hn, dps, dm):
    tn = [_col_tile(dp.shape[1], MXU_DIM) for dp in dps]
    nb = [dp.shape[1] // t for dp, t in zip(dps, tn)]
    first = [sum(nb[:p]) for p in range(len(dps))]
    at = lambda p: (lambda j: (0, jnp.clip(j - first[p], 0, nb[p] - 1)))

    def body(hn_ref, *refs):
        j = pl.program_id(0)
        for p in range(len(dps)):
            @pl.when(jnp.logical_and(j >= first[p], j < first[p] + nb[p]))
            def _(p=p):
                refs[len(dps) + p][...] = _dot_tn(hn_ref[...], refs[p][...])

    return pl.pallas_call(
        body, name="bwd_in_dw", grid=(sum(nb),),
        in_specs=[pl.BlockSpec((dm.R, dm.D), lambda j: (0, 0))] + [
            pl.BlockSpec((dm.R, tn[p]), at(p)) for p in range(len(dps))],
        out_specs=[pl.BlockSpec((dm.D, tn[p]), at(p)) for p in range(len(dps))],
        out_shape=[jax.ShapeDtypeStruct((dm.D, dp.shape[1]), F32) for dp in dps],
        compiler_params=_params(("arbitrary",)),
    )(hn, *dps)


def _tile_index(dm, reverse):
    if reverse:
        return lambda b, i: b * dm.NT + (dm.NT - 1 - i)
    return lambda b, i: b * dm.NT + i


def _halo_index(dm, rows):
    per_tile = TT // rows
    return lambda b, i: jnp.maximum((b * dm.NT + (dm.NT - 1 - i)) * per_tile - 1, 0)


HALO_BLOCK = 16


def _last_rows(x):
    return x.astype(F32)[HALO_BLOCK - SMALL_HALO:HALO_BLOCK]


MIX_A_ROWS = 288


def _mix_a_fwd(proj, conv_w, dm):
    da = dm.DA
    ta = _row_tile(dm.Lp, MIX_A_ROWS)
    nta = dm.Lp // ta
    bl = dm.BL

    def setup(ab_ref, ac_ref, ax_ref, az_ref, w_ref, y_ref, pbuf):
        i = pl.program_id(0)

        @pl.when(i == 0)
        def _():
            pbuf[0:SMALL_HALO, :] = jnp.zeros((SMALL_HALO, da), F32)

        @pl.when(i > 0)
        def _():
            pbuf[0:SMALL_HALO, :] = pbuf[ta:ta + SMALL_HALO, :]

    def body(ab_ref, ac_ref, ax_ref, az_ref, w_ref, y_ref, pbuf):
        for lb in range(da // LANES):
            cs = slice(lb * LANES, (lb + 1) * LANES)
            p = ac_ref[:, cs].astype(F32) * ax_ref[:, cs].astype(F32)
            pbuf[SMALL_HALO:SMALL_HALO + ta, cs] = p
            q = (w_ref[0:1, cs] * pbuf[6:6 + ta, cs] + w_ref[1:2, cs] * pbuf[7:7 + ta, cs] + w_ref[2:3, cs] * p)
            az = az_ref[:, cs].astype(F32)
            y_ref[:, cs] = (ab_ref[:, cs].astype(F32) * q * (az * _sigmoid(az))).astype(BF16)
            yield

    proj3 = proj.reshape(bl, dm.Lp, dm.NP)
    col = lambda k: pl.BlockSpec((bl, ta, da), lambda i: (0, i, k))
    return pl.pallas_call(
        _per_sequence(setup, body, bl, [True] * 4 + [False] + [True, True]), name="mix_a_fwd", grid=(nta,),
        in_specs=[col(0), col(1), col(2), col(3), pl.BlockSpec((CONV_A_K, da), lambda i: (0, 0))],
        out_specs=col(0),
        out_shape=jax.ShapeDtypeStruct((bl, dm.Lp, da), BF16),
        scratch_shapes=[pltpu.VMEM((bl, SMALL_HALO + ta, da), F32)],
        compiler_params=_params(("arbitrary",)),
    )(proj3, proj3, proj3, proj3, conv_w).reshape(dm.R, da)


def _mix_a_bwd(proj, dya, conv_w, dm):
    da = dm.DA
    ta = _row_tile(dm.Lp, MIX_A_ROWS)
    nta = dm.Lp // ta
    bl = dm.BL

    def setup(ab_ref, ac_ref, ax_ref, az_ref, ach_ref, axh_ref, dy_ref, w_ref, dp_ref, dw_ref, pbuf, dqbuf):
        i = pl.program_id(0)

        @pl.when(i == 0)
        def _():
            dw_ref[...] = jnp.zeros_like(dw_ref)
            dqbuf[ta:ta + SMALL_HALO, :] = jnp.zeros((SMALL_HALO, da), F32)

        @pl.when(i > 0)
        def _():
            dqbuf[ta:ta + SMALL_HALO, :] = dqbuf[0:SMALL_HALO, :]

    def body(ab_ref, ac_ref, ax_ref, az_ref, ach_ref, axh_ref, dy_ref, w_ref, dp_ref, dw_ref, pbuf, dqbuf):
        halo_on = jnp.where(pl.program_id(0) == nta - 1, 0.0, 1.0)
        for lb in range(da // LANES):
            cs = slice(lb * LANES, (lb + 1) * LANES)
            pbuf[0:SMALL_HALO, cs] = (_last_rows(ach_ref[:, cs]) * _last_rows(axh_ref[:, cs])) * halo_on
            ac, ax, ab, az = (r[:, cs].astype(F32) for r in (ac_ref, ax_ref, ab_ref, az_ref))
            p = ac * ax
            pbuf[SMALL_HALO:SMALL_HALO + ta, cs] = p
            p1 = pbuf[7:7 + ta, cs]
            p2 = pbuf[6:6 + ta, cs]
            w0, w1, w2 = w_ref[0:1, cs], w_ref[1:2, cs], w_ref[2:3, cs]
            q = w0 * p2 + w1 * p1 + w2 * p
            sz, dsz = _silu_and_grad(az)
            dy = dy_ref[:, cs]
            t1 = dy * ab
            dq = t1 * sz
            dqbuf[0:ta, cs] = dq
            dpv = w2 * dq + w1 * dqbuf[1:1 + ta, cs] + w0 * dqbuf[2:2 + ta, cs]
            dp_ref[:, lb * LANES:(lb + 1) * LANES] = (dy * q * sz).astype(BF16)
            dp_ref[:, da + lb * LANES:da + (lb + 1) * LANES] = (dpv * ax).astype(BF16)
            dp_ref[:, 2 * da + lb * LANES:2 * da + (lb + 1) * LANES] = (dpv * ac).astype(BF16)
            dp_ref[:, 3 * da + lb * LANES:3 * da + (lb + 1) * LANES] = (t1 * q * dsz).astype(BF16)
            dw_ref[0, 0:1, cs] = dw_ref[0, 0:1, cs] + jnp.sum(dq * p2, axis=0, keepdims=True)
            dw_ref[0, 1:2, cs] = dw_ref[0, 1:2, cs] + jnp.sum(dq * p1, axis=0, keepdims=True)
            dw_ref[0, 2:3, cs] = dw_ref[0, 2:3, cs] + jnp.sum(dq * p, axis=0, keepdims=True)
            yield

    proj3 = proj.reshape(bl, dm.Lp, dm.NP)
    col = lambda w, k: pl.BlockSpec((bl, ta, w), lambda i: (0, nta - 1 - i, k))
    halo = lambda k: pl.BlockSpec((bl, HALO_BLOCK, da),
                                  lambda i: (0, jnp.maximum((nta - 1 - i) * (ta // HALO_BLOCK) - 1, 0), k))
    dp, dw = pl.pallas_call(
        _per_sequence(setup, body, bl, [True] * 7 + [False] + [True, "keep"] + [True, True]),
        name="mix_a_bwd", grid=(nta,),
        in_specs=[col(da, 0), col(da, 1), col(da, 2), col(da, 3), halo(1), halo(2), col(da, 0),
                  pl.BlockSpec((CONV_A_K, da), lambda i: (0, 0))],
        out_specs=[col(dm.WA, 0), pl.BlockSpec((bl, 8, da), lambda i: (0, 0, 0))],
        out_shape=[jax.ShapeDtypeStruct((bl, dm.Lp, dm.WA), BF16), jax.ShapeDtypeStruct((bl, 8, da), F32)],
        scratch_shapes=[pltpu.VMEM((bl, SMALL_HALO + ta, da), F32), pltpu.VMEM((bl, ta + SMALL_HALO, da), F32)],
        compiler_params=_params(("arbitrary",)),
    )(proj3, proj3, proj3, proj3, proj3, proj3, dya.reshape(bl, dm.Lp, da), conv_w)
    return dp.reshape(dm.R, dm.WA), dw


SUBLANES = 8
SHIFT_ROWS = TT + CONF_HALO - SUBLANES


TAP_ROWS = 64


def _split_lanes(buf, rows, val):
    for lb in range(val.shape[1] // LANES):
        buf[lb, rows, :] = val[:, lb * LANES:(lb + 1) * LANES]


def _join_lanes(buf):
    return jnp.concatenate([buf[lb] for lb in range(buf.shape[0])], axis=1)


def _fill_shifted(buf, shifted):
    def step(lb, carry):
        for r in range(1, SUBLANES):
            shifted[lb, r - 1, 0:SHIFT_ROWS, :] = buf[lb, r:r + SHIFT_ROWS, :]
        return carry

    lax.fori_loop(0, buf.shape[0], step, 0)


def _window(buf, shifted, d, r0, lb):
    r = d % SUBLANES
    rows = pl.ds(pl.multiple_of(r0 + (d - r), SUBLANES), TAP_ROWS)
    return buf[lb, rows, :] if r == 0 else shifted[lb, r - 1, rows, :]


def _tap_loop(nlb, body):
    per_lb = TT // TAP_ROWS

    def step(it, carry):
        lb = it // per_lb
        body(lb, pl.ds(pl.multiple_of(lb * LANES, LANES), LANES), pl.multiple_of((it % per_lb) * TAP_ROWS, TAP_ROWS))
        return carry

    lax.fori_loop(0, nlb * per_lb, step, 0)


TAP_CHAINS = 4


def _tree_sum(terms):
    sums = list(terms[:TAP_CHAINS])
    for n, t in enumerate(terms[TAP_CHAINS:]):
        sums[n % TAP_CHAINS] = sums[n % TAP_CHAINS] + t
    while len(sums) > 1:
        sums = [a + b for a, b in zip(sums[0::2], sums[1::2])] + ([sums[-1]] if len(sums) % 2 else [])
    return sums[0]


def _conf_conv(ubuf, ushift, w_ref, b_ref, u1buf):
    _fill_shifted(ubuf, ushift)

    def piece(lb, cs, r0):
        taps = [w_ref[k:k + 1, cs] * _window(ubuf, ushift, CONF_HALO - (CONF_K - 1) + k, r0, lb)
                for k in range(CONF_K)]
        u1buf[lb, pl.ds(r0, TAP_ROWS), :] = _tree_sum(taps) + b_ref[0:1, cs]

    _tap_loop(ubuf.shape[0], piece)


def _mix_c_fwd(proj, conv_w, conv_b, ln_g, ln_b, dm):
    dc = dm.DC
    nlb = dc // LANES
    c0 = (dm.WA + dm.WB) // dc
    ti = _tile_index(dm, False)

    def body(ca_ref, cg_ref, cz_ref, w_ref, b_ref, g_ref, be_ref, y_ref, u1_ref, ubuf, u1buf, ushift):
        i = pl.program_id(1)

        @pl.when(i == 0)
        def _():
            ubuf[:, 0:CONF_HALO, :] = jnp.zeros((nlb, CONF_HALO, LANES), F32)

        @pl.when(i > 0)
        def _():
            ubuf[:, 0:CONF_HALO, :] = ubuf[:, TT:TT + CONF_HALO, :]

        _split_lanes(ubuf, slice(CONF_HALO, CONF_HALO + TT),
                     ca_ref[...].astype(F32) * _sigmoid(cg_ref[...].astype(F32)))
        _conf_conv(ubuf, ushift, w_ref, b_ref, u1buf)
        u1 = _join_lanes(u1buf)
        u1_ref[...] = u1
        mu = jnp.mean(u1, axis=-1, keepdims=True)
        xc = u1 - mu
        rstd = lax.rsqrt(jnp.mean(xc * xc, axis=-1, keepdims=True) + LN_EPS)
        u2 = xc * rstd * g_ref[...] + be_ref[...]
        cz = cz_ref[...].astype(F32)
        y_ref[...] = ((u2 * _sigmoid(u2)) * (cz * _sigmoid(cz))).astype(BF16)

    col = lambda k: pl.BlockSpec((TT, dc), lambda b, i: (ti(b, i), c0 + k))
    vec = pl.BlockSpec((1, dc), lambda b, i: (0, 0))
    return pl.pallas_call(
        body, name="mix_c_fwd", grid=(dm.BL, dm.NT),
        in_specs=[col(0), col(1), col(2), pl.BlockSpec((CONF_K, dc), lambda b, i: (0, 0)), vec, vec, vec],
        out_specs=[pl.BlockSpec((TT, dc), lambda b, i: (ti(b, i), 0))] * 2,
        out_shape=[jax.ShapeDtypeStruct((dm.R, dc), BF16), jax.ShapeDtypeStruct((dm.R, dc), F32)],
        scratch_shapes=[pltpu.VMEM((nlb, CONF_HALO + TT, LANES), F32), pltpu.VMEM((nlb, TT, LANES), F32),
                        pltpu.VMEM((nlb, SUBLANES - 1, SHIFT_ROWS, LANES), F32)],
        compiler_params=_params(("parallel", "arbitrary")),
    )(proj, proj, proj, conv_w, conv_b, ln_g, ln_b)


def _mix_c_bwd(proj, u1, dyc, conv_w, ln_g, ln_b, dm):
    dc = dm.DC
    nlb = dc // LANES
    c0 = (dm.WA + dm.WB) // dc
    ti = _tile_index(dm, True)
    hi = _halo_index(dm, CONF_HALO)

    def body(ca_ref, cg_ref, cz_ref, cah_ref, cgh_ref, u1_ref, dy_ref, w_ref, g_ref, be_ref,
             dp_ref, dw_ref, dv_ref, ubuf, dubuf, du0buf, ushift, dshift, dwacc):
        i = pl.program_id(1)
        halo_on = jnp.where(i == dm.NT - 1, 0.0, 1.0)

        @pl.when(i == 0)
        def _():
            dwacc[...] = jnp.zeros_like(dwacc)
            dv_ref[...] = jnp.zeros_like(dv_ref)
            dubuf[:, TT:TT + CONF_HALO, :] = jnp.zeros((nlb, CONF_HALO, LANES), F32)

        @pl.when(i > 0)
        def _():
            dubuf[:, TT:TT + CONF_HALO, :] = dubuf[:, 0:CONF_HALO, :]

        _split_lanes(ubuf, slice(0, CONF_HALO),
                     cah_ref[...].astype(F32) * _sigmoid(cgh_ref[...].astype(F32)) * halo_on)
        sgg = _sigmoid(cg_ref[...].astype(F32))
        ca = ca_ref[...].astype(F32)
        _split_lanes(ubuf, slice(CONF_HALO, CONF_HALO + TT), ca * sgg)
        _fill_shifted(ubuf, ushift)
        u1 = u1_ref[...]
        mu = jnp.mean(u1, axis=-1, keepdims=True)
        xc = u1 - mu
        rstd = lax.rsqrt(jnp.mean(xc * xc, axis=-1, keepdims=True) + LN_EPS)
        xhat = xc * rstd
        u2 = xhat * g_ref[...] + be_ref[...]
        su, dsu = _silu_and_grad(u2)
        sz, dsz = _silu_and_grad(cz_ref[...].astype(F32))
        dy = dy_ref[...]
        du2 = dy * dsu * sz
        dp_ref[:, 2 * dc:3 * dc] = (dy * su * dsz).astype(BF16)
        dxhat = du2 * g_ref[...]
        du1 = rstd * (dxhat - jnp.mean(dxhat, axis=-1, keepdims=True)
                      - xhat * jnp.mean(dxhat * xhat, axis=-1, keepdims=True))
        dv_ref[0, 0:1, :] = dv_ref[0, 0:1, :] + jnp.sum(du1, axis=0, keepdims=True)
        dv_ref[0, 1:2, :] = dv_ref[0, 1:2, :] + jnp.sum(du2 * xhat, axis=0, keepdims=True)
        dv_ref[0, 2:3, :] = dv_ref[0, 2:3, :] + jnp.sum(du2, axis=0, keepdims=True)
        _split_lanes(dubuf, slice(0, TT), du1)
        _fill_shifted(dubuf, dshift)

        def piece(lb, cs, r0):
            du0buf[lb, pl.ds(r0, TAP_ROWS), :] = _tree_sum(
                [w_ref[k:k + 1, cs] * _window(dubuf, dshift, CONF_K - 1 - k, r0, lb) for k in range(CONF_K)])
            d1 = dubuf[lb, pl.ds(r0, TAP_ROWS), :]
            for k in range(CONF_K):
                prod = d1 * _window(ubuf, ushift, CONF_HALO - (CONF_K - 1) + k, r0, lb)
                dwacc[lb, k] = dwacc[lb, k] + jnp.sum(prod.reshape(TAP_ROWS // SUBLANES, SUBLANES, LANES), axis=0)

        _tap_loop(nlb, piece)
        du0 = _join_lanes(du0buf)
        dp_ref[:, 0:dc] = (du0 * sgg).astype(BF16)
        dp_ref[:, dc:2 * dc] = (du0 * ca * sgg * (1.0 - sgg)).astype(BF16)

        @pl.when(i == dm.NT - 1)
        def _():
            for lb in range(nlb):
                dw_ref[0, 0:CONF_K, lb * LANES:(lb + 1) * LANES] = jnp.sum(dwacc[lb], axis=1)
            dw_ref[0, CONF_K:CONF_K + 1, :] = jnp.zeros((1, dc), F32)

    col = lambda k: pl.BlockSpec((TT, dc), lambda b, i: (ti(b, i), c0 + k))
    halo = lambda k: pl.BlockSpec((CONF_HALO, dc), lambda b, i: (hi(b, i), c0 + k))
    vec = pl.BlockSpec((1, dc), lambda b, i: (0, 0))
    return pl.pallas_call(
        body, name="mix_c_bwd", grid=(dm.BL, dm.NT),
        in_specs=[col(0), col(1), col(2), halo(0), halo(1),
                  pl.BlockSpec((TT, dc), lambda b, i: (ti(b, i), 0)),
                  pl.BlockSpec((TT, dc), lambda b, i: (ti(b, i), 0)),
                  pl.BlockSpec((CONF_K, dc), lambda b, i: (0, 0)), vec, vec],
        out_specs=[pl.BlockSpec((TT, dm.WC), lambda b, i: (ti(b, i), 0)),
                   pl.BlockSpec((1, 32, dc), lambda b, i: (b, 0, 0)),
                   pl.BlockSpec((1, 8, dc), lambda b, i: (b, 0, 0))],
        out_shape=[jax.ShapeDtypeStruct((dm.R, dm.WC), BF16),
                   jax.ShapeDtypeStruct((dm.BL, 32, dc), F32),
                   jax.ShapeDtypeStruct((dm.BL, 8, dc), F32)],
        scratch_shapes=[pltpu.VMEM((nlb, CONF_HALO + TT, LANES), F32),
                        pltpu.VMEM((nlb, TT + CONF_HALO, LANES), F32), pltpu.VMEM((nlb, TT, LANES), F32),
                        pltpu.VMEM((nlb, SUBLANES - 1, SHIFT_ROWS, LANES), F32),
                        pltpu.VMEM((nlb, SUBLANES - 1, SHIFT_ROWS, LANES), F32),
                        pltpu.VMEM((nlb, CONF_K, SUBLANES, LANES), F32)],
        compiler_params=_params(("parallel", "arbitrary")),
    )(proj, proj, proj, proj, proj, u1, dyc, conv_w, ln_g, ln_b)


RAW_HALO = 16


def _shift_matrix():
    r = jnp.arange((SSM_CONV_K - 1) * TT)[:, None]
    want = TT + r % TT - (SSM_CONV_K - 1 - r // TT)
    return (jnp.arange(2 * TT)[None, :] == want).astype(BF16)


def _ssm_conv(rawwin, sh, s_ref, w_ref, b_ref, width):
    sh[...] = _dot(s_ref[...], rawwin[...])
    for lb in range(width // LANES):
        cs = slice(lb * LANES, (lb + 1) * LANES)
        acc = b_ref[0:1, cs] + w_ref[SSM_CONV_K - 1:SSM_CONV_K, cs] * rawwin[TT:2 * TT, cs].astype(F32)
        for k in range(SSM_CONV_K - 1):
            acc = acc + w_ref[k:k + 1, cs] * sh[k * TT:(k + 1) * TT, cs]
        yield cs, acc


def _softplus(z):
    return jnp.maximum(z, 0.0) + jnp.log(1.0 + jnp.exp(-jnp.abs(z)))


def _tri(lower):
    r = lax.broadcasted_iota(jnp.int32, (TT, TT), 0)
    c = lax.broadcasted_iota(jnp.int32, (TT, TT), 1)
    return (c <= r) if lower else (c >= r)


def _exact_01_dot(mat01, x):
    x1, x2, x3 = _split3(x)
    return _dot(mat01, x1) + _dot(mat01, x2) + _dot(mat01, x3)


def _head_scalars(dt_ref, dtb_ref, alog_ref):
    z = dt_ref[...] + dtb_ref[...]
    dtv = _softplus(z)
    a = -jnp.exp(alog_ref[...])
    ac = _exact_01_dot(_tri(True).astype(F32).astype(BF16), dtv * a)
    eac = jnp.exp(ac)
    dst = jnp.exp(ac[TT - 1:TT, :] - ac)
    return z, dtv, a, ac, eac, dst


FAR_BELOW = -1e30


def _decay(ac, ac_t, h, causal):
    return jnp.exp(jnp.where(causal, ac[:, h:h + 1] - ac_t[h:h + 1, :], FAR_BELOW))


def _own_half(x16, h):
    lane = lax.broadcasted_iota(jnp.int32, (1, LANES), 1)
    keep = (lane >= SSM_HEAD_DIM) if (h % 2) else (lane < SSM_HEAD_DIM)
    return jnp.where(keep, x16, jnp.zeros_like(x16))


def _per_sequence(setup, body, bl, how):
    def all_sequences(*refs):
        views = [[r.at[b] if h is True else (r.at[pl.ds(b, 1)] if h == "keep" else r) for r, h in zip(refs, how)]
                 for b in range(bl)]
        for v in views:
            setup(*v)
        running = [body(*v) for v in views]
        while running:
            running = [g for g in running if next(g, "done") != "done"]

    return all_sequences


def _mix_b_fwd(proj, projdt, conv_w, conv_b, dt_bias, a_log, dskx, norm_g, expand, dm, rider=None):
    db, gn, xbc_w, hpg = dm.DB, dm.GN, dm.XBC, dm.HPG
    gw = db // SSM_GROUPS

    def setup(bz_ref, bx_ref, bc_ref, dt_ref, w_ref, b_ref, dtb_ref, alog_ref, dsk_ref, g_ref, e_ref, s_ref,
              y_ref, yraw_ref, sprev_ref, rawwin, sh, xbuf, state, ybuf, exbuf, xdtbuf):
        i = pl.program_id(0)

        @pl.when(i == 0)
        def _():
            rawwin[0:TT, :] = jnp.zeros((TT, xbc_w), BF16)
            state[...] = jnp.zeros_like(state)

        @pl.when(i > 0)
        def _():
            rawwin[TT - RAW_HALO:TT, :] = rawwin[2 * TT - RAW_HALO:2 * TT, :]

    def body(bz_ref, bx_ref, bc_ref, dt_ref, w_ref, b_ref, dtb_ref, alog_ref, dsk_ref, g_ref, e_ref, s_ref,
             y_ref, yraw_ref, sprev_ref, rawwin, sh, xbuf, state, ybuf, exbuf, xdtbuf):
        rawwin[TT:2 * TT, 0:db] = bx_ref[...]
        rawwin[TT:2 * TT, db:xbc_w] = bc_ref[...]
        for cs, pre in _ssm_conv(rawwin, sh, s_ref, w_ref, b_ref, xbc_w):
            xbuf[:, cs] = pre * _sigmoid(pre)
            yield

        _, dtv, _, ac, eac, dst = _head_scalars(dt_ref, dtb_ref, alog_ref)
        exbuf[...] = _dot(jnp.concatenate([dtv, eac, dst], axis=0).astype(BF16), e_ref[...])
        ac_t = ac.T
        causal = _tri(True)
        sprev_ref[0, 0] = state[...]
        yield

        xdtbuf[...] = xbuf[:, 0:db] * exbuf[0:TT, :]
        ybuf[...] = xbuf[:, 0:db] * dsk_ref[...]
        for g in range(SSM_GROUPS):
            gs = slice(g * gw, (g + 1) * gw)
            bg = xbuf[:, db + g * SSM_STATE:db + (g + 1) * SSM_STATE].astype(BF16)
            cg = xbuf[:, db + gn + g * SSM_STATE:db + gn + (g + 1) * SSM_STATE].astype(BF16)
            cb = _dot_nt(cg, bg)
            for e in range(0, hpg, 2):
                h = g * hpg + e
                ps = slice(h * SSM_HEAD_DIM, (h + 2) * SSM_HEAD_DIM)
                xp16 = xdtbuf[:, ps].astype(BF16)
                acc = jnp.zeros((TT, LANES), F32)
                for hh in (h, h + 1):
                    mm = (cb * _decay(ac, ac_t, hh, causal)).astype(BF16)
                    acc = acc + _dot(mm, _own_half(xp16, hh))
                ybuf[:, ps] = ybuf[:, ps] + acc
                yield
            sg = state[:, gs]
            ybuf[:, gs] = ybuf[:, gs] + exbuf[TT:2 * TT, gs] * _dot(cg, sg.astype(BF16))
            state[:, gs] = sg * exbuf[2 * TT - 1:2 * TT, gs] + _dot_tn(
                bg, (xdtbuf[:, gs] * exbuf[2 * TT:3 * TT, gs]).astype(BF16))
            yield

        yraw = ybuf[...]
        yraw_ref[...] = yraw
        bz = bz_ref[...].astype(F32)
        v = yraw * (bz * _sigmoid(bz))
        r = lax.rsqrt(jnp.mean(v * v, axis=-1, keepdims=True) + NORM_EPS)
        y_ref[...] = (v * r * g_ref[...]).astype(BF16)

    bl = dm.BL
    tile = lambda w, k: pl.BlockSpec((bl, TT, w), lambda i: (0, i, k))
    fixed = lambda r, w: pl.BlockSpec((r, w), lambda i: (0, 0))
    proj3, dt3 = proj.reshape(bl, dm.Lp, dm.NP), projdt.reshape(bl, dm.Lp, DT_PAD)
    scratch = [((2 * TT, xbc_w), BF16), (((SSM_CONV_K - 1) * TT, xbc_w), F32), ((TT, xbc_w), F32),
               ((SSM_STATE, db), F32), ((TT, db), F32), ((3 * TT, db), F32), ((TT, db), F32)]
    (y, yraw, sprev), rode = _call(
        _per_sequence(setup, body, bl, [True] * 4 + [False] * 8 + [True, True, "keep"] + [True] * len(scratch)),
        "mix_b_fwd", (dm.NT,),
        [tile(db, dm.WA // db), tile(db, dm.WA // db + 1), tile(2 * gn, (dm.WA + 2 * db) // (2 * gn)),
         tile(DT_PAD, 0),
         fixed(SSM_CONV_K, xbc_w), fixed(1, xbc_w), fixed(1, DT_PAD), fixed(1, DT_PAD),
         fixed(1, db), fixed(1, db), fixed(DT_PAD, db), fixed((SSM_CONV_K - 1) * TT, 2 * TT)],
        [tile(db, 0), tile(db, 0), pl.BlockSpec((bl, 1, SSM_STATE, db), lambda i: (0, i, 0, 0))],
        [jax.ShapeDtypeStruct((bl, dm.Lp, db), BF16), jax.ShapeDtypeStruct((bl, dm.Lp, db), F32),
         jax.ShapeDtypeStruct((bl, dm.NT, SSM_STATE, db), F32)],
        [pltpu.VMEM((bl,) + s, t) for s, t in scratch],
        ("arbitrary",),
        (proj3, proj3, proj3, dt3, conv_w, conv_b, dt_bias, a_log, dskx, norm_g, expand, _shift_matrix()), rider)
    return (y.reshape(dm.R, db), yraw.reshape(dm.R, db), sprev), rode


def _mix_b_bwd(proj, projdt, dyb, yraw, sprev, conv_w, conv_b, dt_bias, a_log, dskx, norm_g, expand, expand_t, dm,
               rider=None):
    db, gn, xbc_w, hpg = dm.DB, dm.GN, dm.XBC, dm.HPG
    gw = db // SSM_GROUPS

    def setup(bz_ref, bx_ref, bc_ref, dt_ref, bxh_ref, bch_ref, dy_ref, yraw_ref, sprev_ref,
              w_ref, b_ref, dtb_ref, alog_ref, dsk_ref, g_ref, e_ref, et_ref, s_ref,
              dp_ref, dpt_ref, dwc_ref, dch_ref, dhd_ref,
              rawwin, sh, xbuf, dsbuf, dstate, dxbuf, z1buf, dprebuf, exbuf, xdtbuf, dyrbuf, uvec):
        i = pl.program_id(0)

        @pl.when(i == 0)
        def _():
            dwc_ref[...] = jnp.zeros_like(dwc_ref)
            dch_ref[...] = jnp.zeros_like(dch_ref)
            dhd_ref[...] = jnp.zeros_like(dhd_ref)
            dstate[...] = jnp.zeros_like(dstate)
            dprebuf[TT:TT + SMALL_HALO, :] = jnp.zeros((SMALL_HALO, xbc_w), F32)
            rawwin[0:TT - RAW_HALO, :] = jnp.zeros((TT - RAW_HALO, xbc_w), BF16)

        @pl.when(i > 0)
        def _():
            dprebuf[TT:TT + SMALL_HALO, :] = dprebuf[0:SMALL_HALO, :]

    def body(bz_ref, bx_ref, bc_ref, dt_ref, bxh_ref, bch_ref, dy_ref, yraw_ref, sprev_ref,
             w_ref, b_ref, dtb_ref, alog_ref, dsk_ref, g_ref, e_ref, et_ref, s_ref,
             dp_ref, dpt_ref, dwc_ref, dch_ref, dhd_ref,
             rawwin, sh, xbuf, dsbuf, dstate, dxbuf, z1buf, dprebuf, exbuf, xdtbuf, dyrbuf, uvec):
        halo_on = jnp.where(pl.program_id(0) == dm.NT - 1, 0.0, 1.0).astype(BF16)

        rawwin[TT - RAW_HALO:TT, 0:db] = bxh_ref[...] * halo_on
        rawwin[TT - RAW_HALO:TT, db:xbc_w] = bch_ref[...] * halo_on
        rawwin[TT:2 * TT, 0:db] = bx_ref[...]
        rawwin[TT:2 * TT, db:xbc_w] = bc_ref[...]
        for cs, pre in _ssm_conv(rawwin, sh, s_ref, w_ref, b_ref, xbc_w):
            sl, dsl = _silu_and_grad(pre)
            xbuf[:, cs] = sl
            dsbuf[:, cs] = dsl
            yield

        z, dtv, a, ac, eac, dst = _head_scalars(dt_ref, dtb_ref, alog_ref)
        exbuf[...] = _dot(jnp.concatenate([dtv, eac, dst], axis=0).astype(BF16), e_ref[...])
        ac_t = ac.T
        causal = _tri(True)
        xdtbuf[...] = xbuf[:, 0:db] * exbuf[0:TT, :]

        yraw = yraw_ref[...]
        sz, dsz = _silu_and_grad(bz_ref[...].astype(F32))
        v = yraw * sz
        r = lax.rsqrt(jnp.mean(v * v, axis=-1, keepdims=True) + NORM_EPS)
        dy = dy_ref[...]
        dyg = dy * g_ref[...]
        dv = r * dyg - v * (r * r * r * jnp.mean(dyg * v, axis=-1, keepdims=True))
        dch_ref[0, 0:1, :] = dch_ref[0, 0:1, :] + jnp.sum(dy * v * r, axis=0, keepdims=True)
        dyr = dv * sz
        dyrbuf[...] = dyr
        dp_ref[:, 0:db] = (dv * yraw * dsz).astype(BF16)
        dch_ref[0, 1:2, :] = dch_ref[0, 1:2, :] + jnp.sum(dyr * xbuf[:, 0:db], axis=0, keepdims=True)

        lane_row = lax.broadcasted_iota(jnp.int32, (1, LANES), 1)
        sub_col = lax.broadcasted_iota(jnp.int32, (LANES, 1), 0)
        dac = jnp.zeros((TT, LANES), F32)
        colacc = jnp.zeros((LANES, TT), F32)
        for g in range(SSM_GROUPS):
            gs = slice(g * gw, (g + 1) * gw)
            bs_ = slice(db + g * SSM_STATE, db + (g + 1) * SSM_STATE)
            cs_ = slice(db + gn + g * SSM_STATE, db + gn + (g + 1) * SSM_STATE)
            bg = xbuf[:, bs_].astype(BF16)
            cg = xbuf[:, cs_].astype(BF16)
            cb = _dot_nt(cg, bg)
            dcb = jnp.zeros((TT, TT), F32)
            for e in range(0, hpg, 2):
                h = g * hpg + e
                ps = slice(h * SSM_HEAD_DIM, (h + 2) * SSM_HEAD_DIM)
                xp16 = xdtbuf[:, ps].astype(BF16)
                dyp16 = dyrbuf[:, ps].astype(BF16)
                acc = jnp.zeros((TT, LANES), F32)
                for hh in (h, h + 1):
                    dec = _decay(ac, ac_t, hh, causal)
                    mm = cb * dec
                    dyh = _own_half(dyp16, hh)
                    dmm = _dot_nt(dyh, xp16)
                    acc = acc + _dot_tn(mm.astype(BF16), dyh)
                    dcb = dcb + dmm * dec
                    gm = dmm * mm
                    dac = dac + jnp.sum(gm, axis=1, keepdims=True) * (lane_row == hh).astype(F32)
                    colacc = colacc + (sub_col == hh).astype(F32) * jnp.sum(gm, axis=0, keepdims=True)
                dxbuf[:, ps] = acc
                yield
            sg32 = sprev_ref[0, 0, :, gs]
            sg = sg32.astype(BF16)
            dsn = dstate[:, gs]
            dsn16 = dsn.astype(BF16)
            dcb16 = dcb.astype(BF16)
            eacx = exbuf[TT:2 * TT, gs]
            dstx = exbuf[2 * TT:3 * TT, gs]
            cdx = exbuf[2 * TT - 1:2 * TT, gs]
            dye16 = (dyrbuf[:, gs] * eacx).astype(BF16)
            xdt_g = xdtbuf[:, gs]
            dxbuf[:, cs_] = _dot(dcb16, bg) + _dot_nt(dye16, sg)
            dst_x = dstx * _dot(bg, dsn16)
            dxbuf[:, bs_] = _dot_tn(dcb16, cg) + _dot_nt((dstx * xdt_g).astype(BF16), dsn16)
            dstate[:, gs] = cdx * dsn + _dot_tn(cg, dye16)
            z1buf[:, gs] = dyrbuf[:, gs] * (eacx * _dot(cg, sg)) - xdt_g * dst_x
            uvec[:, gs] = jnp.broadcast_to(
                jnp.sum(xdt_g * dst_x, axis=0, keepdims=True) + jnp.sum(dsn * cdx * sg32, axis=0, keepdims=True),
                (8, gw))
            dxbuf[:, gs] = dxbuf[:, gs] + dst_x
            yield

        zz = _dot(jnp.concatenate([z1buf[...], dxbuf[:, 0:db] * xbuf[:, 0:db]], axis=0).astype(BF16), et_ref[...])
        u1, u2, u3 = _split3(uvec[...])
        ulast = (_dot(u1, et_ref[...]) + _dot(u2, et_ref[...]) + _dot(u3, et_ref[...]))[0:1, :]
        is_last = (lax.broadcasted_iota(jnp.int32, (TT, 1), 0) == TT - 1).astype(F32)
        dac = dac - colacc.T + zz[0:TT] + is_last * ulast
        dda = _exact_01_dot(_tri(False).astype(F32).astype(BF16), dac)
        ddt = dda * a + zz[TT:2 * TT]
        dhd_ref[0, 1:2, :] = dhd_ref[0, 1:2, :] + jnp.sum(dda * dtv, axis=0, keepdims=True) * a
        ddtraw = ddt * _sigmoid(z)
        dhd_ref[0, 0:1, :] = dhd_ref[0, 0:1, :] + jnp.sum(ddtraw, axis=0, keepdims=True)
        dpt_ref[...] = ddtraw.astype(BF16)
        dxbuf[:, 0:db] = dyrbuf[...] * dsk_ref[...] + dxbuf[:, 0:db] * exbuf[0:TT, :]

        for lb in range(xbc_w // LANES):
            cs = slice(lb * LANES, (lb + 1) * LANES)
            dpre = dxbuf[:, cs] * dsbuf[:, cs]
            dprebuf[0:TT, cs] = dpre
            dwc_ref[0, SSM_CONV_K:SSM_CONV_K + 1, cs] = dwc_ref[0, SSM_CONV_K:SSM_CONV_K + 1, cs] + jnp.sum(
                dpre, axis=0, keepdims=True)
            draw = w_ref[SSM_CONV_K - 1:SSM_CONV_K, cs] * dpre
            for k in range(SSM_CONV_K - 1):
                ahead = SSM_CONV_K - 1 - k
                draw = draw + w_ref[k:k + 1, cs] * dprebuf[ahead:ahead + TT, cs]
            for k in range(SSM_CONV_K):
                moved = sh[k * TT:(k + 1) * TT, cs] if k < SSM_CONV_K - 1 else rawwin[TT:2 * TT, cs].astype(F32)
                dwc_ref[0, k:k + 1, cs] = dwc_ref[0, k:k + 1, cs] + jnp.sum(dpre * moved, axis=0, keepdims=True)
            dp_ref[:, db + lb * LANES:db + (lb + 1) * LANES] = draw.astype(BF16)
            yield

    bl, nt = dm.BL, dm.NT
    tile = lambda w, k: pl.BlockSpec((bl, TT, w), lambda i: (0, nt - 1 - i, k))
    halo = lambda w, k: pl.BlockSpec((bl, HALO_BLOCK, w),
                                     lambda i: (0, jnp.maximum((nt - 1 - i) * (TT // HALO_BLOCK) - 1, 0), k))
    fixed = lambda r, w: pl.BlockSpec((r, w), lambda i: (0, 0))
    sums = lambda w: pl.BlockSpec((bl, 8, w), lambda i: (0, 0, 0))
    kz = dm.WA // db
    kc = (dm.WA + 2 * db) // (2 * gn)
    proj3, dt3 = proj.reshape(bl, dm.Lp, dm.NP), projdt.reshape(bl, dm.Lp, DT_PAD)
    scratch = [((2 * TT, xbc_w), BF16), (((SSM_CONV_K - 1) * TT, xbc_w), F32), ((TT, xbc_w), F32),
               ((TT, xbc_w), F32), ((SSM_STATE, db), F32), ((TT, xbc_w), F32), ((TT, db), F32),
               ((TT + SMALL_HALO, xbc_w), F32), ((3 * TT, db), F32), ((TT, db), F32), ((TT, db), F32), ((8, db), F32)]
    how = [True] * 8 + ["keep"] + [False] * 9 + [True, True, "keep", "keep", "keep"] + [True] * len(scratch)
    (dp, dpt, dwc, dch, dhd), rode = _call(
        _per_sequence(setup, body, bl, how), "mix_b_bwd", (nt,),
        [tile(db, kz), tile(db, kz + 1), tile(2 * gn, kc), tile(DT_PAD, 0),
         halo(db, kz + 1), halo(2 * gn, kc), tile(db, 0), tile(db, 0),
         pl.BlockSpec((bl, 1, SSM_STATE, db), lambda i: (0, nt - 1 - i, 0, 0)),
         fixed(SSM_CONV_K, xbc_w), fixed(1, xbc_w), fixed(1, DT_PAD), fixed(1, DT_PAD),
         fixed(1, db), fixed(1, db), fixed(DT_PAD, db), fixed(db, DT_PAD), fixed((SSM_CONV_K - 1) * TT, 2 * TT)],
        [tile(dm.WB, 0), tile(DT_PAD, 0), sums(xbc_w), sums(db), sums(DT_PAD)],
        [jax.ShapeDtypeStruct((bl, dm.Lp, dm.WB), BF16), jax.ShapeDtypeStruct((bl, dm.Lp, DT_PAD), BF16),
         jax.ShapeDtypeStruct((bl, 8, xbc_w), F32), jax.ShapeDtypeStruct((bl, 8, db), F32),
         jax.ShapeDtypeStruct((bl, 8, DT_PAD), F32)],
        [pltpu.VMEM((bl,) + s, t) for s, t in scratch],
        ("arbitrary",),
        (proj3, proj3, proj3, dt3, proj3, proj3, dyb.reshape(bl, dm.Lp, db), yraw.reshape(bl, dm.Lp, db), sprev,
         conv_w, conv_b, dt_bias, a_log, dskx, norm_g, expand, expand_t, _shift_matrix()), rider)
    return (dp.reshape(dm.R, dm.WB), dpt.reshape(dm.R, DT_PAD), dwc, dch, dhd), rode


def _head_consts(dm):
    head_of = jnp.arange(dm.DB) // SSM_HEAD_DIM
    expand = (jnp.arange(DT_PAD)[:, None] == head_of[None, :]).astype(BF16)
    return expand, expand.T


def _ssm_params(lw, dm):
    pad_h = lambda v: jnp.pad(v, (0, DT_PAD - dm.H))[None]
    return (lw["ssm_conv_w"], lw["ssm_conv_b"][None], pad_h(lw["dt_bias"]), pad_h(lw["a_log"]),
            jnp.repeat(lw["d_skip"], SSM_HEAD_DIM)[None], lw["ssm_norm_g"][None])


def _layer_fwd(h, lw, w_in, w_out, cst, dm, next_bases=None, own_out=None):
    nxt = next_bases is not None
    n_next = len(next_bases) if nxt else 0

    def beside(rider, extra):
        return extra if rider is None else (rider if extra is None else _ride_both(rider, extra))

    (proj, projdt, hn), got = _fwd_in(
        h, lw["pre_g"][None], w_in, dm,
        beside(_ride_gather_ici(next_bases, 0, 2) if nxt else None,
               None if own_out is None else _ride_gather_ici([own_out])))
    ya = _mix_a_fwd(proj, lw["conv_a_w"], dm)
    (yb, yraw, sprev), got = _mix_b_fwd(
        proj, projdt, *_ssm_params(lw, dm), cst[0], dm,
        beside(_ride_gather_ici(got[:n_next], 1, 2) if nxt else None,
               None if own_out is None else _ride_gather_d2d(got[n_next:])))
    if own_out is not None:
        w_out = got[n_next].reshape(2 * dm.D, dm.D)
        got = got[:n_next]
    yc, u1 = _mix_c_fwd(proj, lw["conf_conv_w"], lw["conf_conv_b"][None], lw["conf_ln_g"][None],
                        lw["conf_ln_b"][None], dm)
    (h_new, m), got = _fwd_out(ya, yb, yc, w_out, h, lw["post_g"][None], dm, _ride_gather_d2d(got) if nxt else None)
    return h_new, (h, hn, proj, projdt, ya, yb, yc, u1, yraw, sprev, m), got, w_out


def _layer_bwd(dh, saved, lw, w_in, w_out, cst, dm, reduce=None, last=False):
    h_in, hn, proj, projdt, ya, yb, yc, u1, yraw, sprev, m = saved
    (dya, dyb, dyc, dwo, dpost), got = _bwd_out(dh, m, lw["post_g"][None], w_out, ya, yb, yc, dm,
                                                None if reduce is None else reduce.swap())
    dpa, dwa = _mix_a_bwd(proj, dya, lw["conv_a_w"], dm)
    (dpb, dpt, dwcv, dch, dhd), got = _mix_b_bwd(proj, projdt, dyb, yraw, sprev, *_ssm_params(lw, dm), cst[0],
                                                 cst[1], dm, None if reduce is None else reduce.to_owners(got))
    dpc, dwcf, dvc = _mix_c_bwd(proj, u1, dyc, lw["conf_conv_w"], lw["conf_ln_g"][None], lw["conf_ln_b"][None], dm)
    def own_reduce():
        pieces = _bwd_in_dw(hn, [dpa, dpb, dpc, dpt], dm)
        return _GradReduce([_grad_to_shards(pieces, dm), dwo.reshape(N_CHIPS, 2 * dm.D // N_CHIPS, dm.D)])

    rider = None if reduce is None else reduce.join(got)
    n_join = 0 if rider is None else len(rider.out_shapes)
    if last:
        mine = own_reduce()
        to_owners = mine.to_owners(_exchange("grad_swap_halves", mine.swap()))
        rider = to_owners if rider is None else _ride_both(rider, to_owners)
    (dh, dpre), got = _bwd_in_dx(dpa, dpb, dpc, dpt, w_in, h_in, dh, lw["pre_g"][None], dm, rider)
    if reduce is not None:
        reduce.finish(got[:n_join])
    if last:
        mine.finish(_exchange("grad_join_halves", mine.join(got[n_join:])))
    else:
        mine = own_reduce()
    dwcv, dch, dhd, dvc = (jnp.sum(a, axis=0) for a in (dwcv, dch, dhd, dvc))
    small = dict(pre_g=dpre[0], post_g=dpost[0], conv_a_w=jnp.sum(dwa, axis=0)[:CONV_A_K],
                 ssm_conv_w=dwcv[:SSM_CONV_K], ssm_conv_b=dwcv[SSM_CONV_K], ssm_norm_g=dch[0],
                 d_skip=jnp.sum(dch[1].reshape(dm.H, SSM_HEAD_DIM), axis=1), dt_bias=dhd[0, :dm.H],
                 a_log=dhd[1, :dm.H], conf_conv_w=jnp.sum(dwcf, axis=0)[:CONF_K], conf_conv_b=dvc[0],
                 conf_ln_g=dvc[1], conf_ln_b=dvc[2])
    return dh, mine, small


def _shard_runs(dm):
    ab = dm.WA + dm.WB
    order = [(0, 0, ab), (ab, dm.DT0, dm.H), (ab + dm.H, ab, dm.WC)]
    k = dm.NIN // N_CHIPS
    runs = []
    for s in range(N_CHIPS):
        for o0, m0, wd in order:
            lo, hi = max(o0, s * k), min(o0 + wd, (s + 1) * k)
            if lo < hi:
                runs.append((s, lo - s * k, m0 + lo - o0, hi - lo))
    return runs


def _w_in_from_shards(base, dm):
    tr = _row_tile(dm.D, 256)
    k = dm.NIN // N_CHIPS
    runs = _shard_runs(dm)

    def body(in_ref, out_ref):
        for s, sc, mc, wd in runs:
            out_ref[:, mc:mc + wd] = in_ref[s, :, sc:sc + wd]
        out_ref[:, dm.DT0 + dm.H:dm.NP] = jnp.zeros((tr, dm.NP - dm.DT0 - dm.H), BF16)

    return pl.pallas_call(
        body, name="w_in_from_shards", grid=(dm.D // tr,),
        in_specs=[pl.BlockSpec((N_CHIPS, tr, k), lambda r: (0, r, 0))],
        out_specs=pl.BlockSpec((tr, dm.NP), lambda r: (r, 0)),
        out_shape=jax.ShapeDtypeStruct((dm.D, dm.NP), BF16),
        compiler_params=_params(("parallel",)),
    )(base)


def _grad_to_shards(pieces, dm):
    tr = _row_tile(dm.D, 256)
    k = dm.NIN // N_CHIPS
    starts = [0, dm.WA, dm.WA + dm.WB, dm.DT0]
    widths = [dm.WA, dm.WB, dm.WC, DT_PAD]
    runs = _shard_runs(dm)

    def body(a_ref, b_ref, c_ref, t_ref, out_ref):
        refs = (a_ref, b_ref, c_ref, t_ref)
        for s, sc, mc, wd in runs:
            for p in range(4):
                lo, hi = max(mc, starts[p]), min(mc + wd, starts[p] + widths[p])
                if lo < hi:
                    out_ref[s, :, sc + lo - mc:sc + hi - mc] = refs[p][:, lo - starts[p]:hi - starts[p]].astype(BF16)

    return pl.pallas_call(
        body, name="grad_to_shards", grid=(dm.D // tr,),
        in_specs=[pl.BlockSpec((tr, w), lambda r: (r, 0)) for w in widths],
        out_specs=pl.BlockSpec((N_CHIPS, tr, k), lambda r: (0, r, 0)),
        out_shape=jax.ShapeDtypeStruct((N_CHIPS, dm.D, k), BF16),
        compiler_params=_params(("parallel",)),
    )(*pieces)


def _place_own(w, layer, me):
    _, rows, cols = w.shape
    tr = _row_tile(rows, 256)

    def body(me_ref, w_ref, out_ref):
        out_ref[0] = w_ref[0].astype(BF16)

    return pl.pallas_call(
        body, name="place_own",
        grid_spec=pltpu.PrefetchScalarGridSpec(
            num_scalar_prefetch=1, grid=(rows // tr,),
            in_specs=[pl.BlockSpec((1, tr, cols), lambda r, me_ref: (layer, r, 0))],
            out_specs=pl.BlockSpec((1, tr, cols), lambda r, me_ref: (me_ref[0], r, 0))),
        out_shape=jax.ShapeDtypeStruct((N_CHIPS, rows, cols), BF16),
        compiler_params=_params(("parallel",)),
    )(me, w)


def _add_halves(g, got, c, name):
    _, _, rows, cols = g.shape
    tr = _row_tile(rows, 256)

    def body(c_ref, g_ref, got_ref, out_ref):
        out_ref[0] = (g_ref[0, 0].astype(F32) + got_ref[0].astype(F32)).astype(BF16)

    return pl.pallas_call(
        body, name=name,
        grid_spec=pltpu.PrefetchScalarGridSpec(
            num_scalar_prefetch=1, grid=(N_CHIPS, rows // tr),
            in_specs=[pl.BlockSpec((1, 1, tr, cols), lambda s, r, c_ref: (s, c_ref[0], r, 0)),
                      pl.BlockSpec((1, tr, cols), lambda s, r, c_ref: (s, r, 0))],
            out_specs=pl.BlockSpec((1, tr, cols), lambda s, r, c_ref: (s, r, 0))),
        out_shape=jax.ShapeDtypeStruct((N_CHIPS, rows, cols), BF16),
        compiler_params=_params(("parallel", "parallel")),
    )(c, g, got)


def _add_owner(p, got, where, name):
    _, rows, cols = p.shape
    tr = _row_tile(rows, 256)

    def body(w_ref, p_ref, got_ref, out_ref):
        acc = p_ref[0].astype(F32)
        for j in range(3):
            acc = acc + got_ref[j].astype(F32)
        out_ref[0] = acc

    return pl.pallas_call(
        body, name=name,
        grid_spec=pltpu.PrefetchScalarGridSpec(
            num_scalar_prefetch=1, grid=(rows // tr,),
            in_specs=[pl.BlockSpec((1, tr, cols), lambda r, w_ref: (w_ref[0], r, 0)),
                      pl.BlockSpec((3, tr, cols), lambda r, w_ref: (0, r, 0))],
            out_specs=pl.BlockSpec((1, tr, cols), lambda r, w_ref: (w_ref[1], r, 0))),
        out_shape=jax.ShapeDtypeStruct((2, rows, cols), F32),
        compiler_params=_params(("parallel",)),
    )(where, p, got)


class _GradReduce:
    def __init__(self, gs):
        self.gs = [g.reshape((N_CHIPS, 2, g.shape[1] // 2) + g.shape[2:]) for g in gs]
        self.c = lax.axis_index("c").astype(jnp.int32).reshape(1)
        chip = (2 * lax.axis_index("x") + lax.axis_index("y")).astype(jnp.int32)
        self.where = jnp.stack([chip, self.c[0]])
        self.result = None

    def swap(self):
        return _ride_swap_halves(self.gs)

    def to_owners(self, got):
        self.ps = [_add_halves(g, r, self.c, "grad_add_sibling_" + n) for g, r, n in zip(self.gs, got, ("in", "out"))]
        return _ride_to_owners(self.ps)

    def join(self, got):
        qs = [_add_owner(p, r, self.where, "grad_add_chips_" + n) for p, r, n in zip(self.ps, got, ("in", "out"))]
        return _ride_join_halves(qs)

    def finish(self, got):
        self.result = [a.reshape((a.shape[0] * a.shape[1],) + a.shape[2:]) for a in got]


def _adamw_math(w, g, m, v):
    m = ADAM_B1 * m + (1.0 - ADAM_B1) * g
    v = ADAM_B2 * v + (1.0 - ADAM_B2) * (g * g)
    m_hat = m / (1.0 - ADAM_B1 ** ADAM_STEP)
    v_hat = v / (1.0 - ADAM_B2 ** ADAM_STEP)
    delta = -ADAM_LR * (m_hat / (jnp.sqrt(v_hat) + ADAM_EPS) + ADAM_WD * w)
    return delta, m, v


def _adamw_small(w, g, m, v, name):
    def body(w_ref, g_ref, m_ref, v_ref, d_out, m_out, v_out):
        d_out[...], m_out[...], v_out[...] = _adamw_math(w_ref[...], g_ref[...], m_ref[...], v_ref[...])

    shape = jax.ShapeDtypeStruct(w.shape, F32)
    return pl.pallas_call(body, name="adamw_" + name, out_shape=[shape, shape, shape],
                          compiler_params=_params())(w, g, m, v)


def _adamw_layer(i, w, g, m, v, prev, name):
    depth, rows, cols = w.shape
    tr = _row_tile(rows, 256)
    n_prev = 0 if prev is None else 4

    def body(*refs):
        w_ref, g_ref, m_ref, v_ref = refs[:4]
        g_out, d_out, m_out, v_out = refs[4 + n_prev:]
        gv = g_ref[...]
        g_out[0] = gv
        d_out[0], m_out[0], v_out[0] = _adamw_math(w_ref[0], gv, m_ref[0], v_ref[0])

    lay = pl.BlockSpec((1, tr, cols), lambda r: (i, r, 0))
    shape = jax.ShapeDtypeStruct(w.shape, F32)
    return pl.pallas_call(
        body, name="adamw_" + name, grid=(rows // tr,),
        in_specs=[lay, pl.BlockSpec((tr, cols), lambda r: (r, 0)), lay, lay] + [ANY] * n_prev,
        out_specs=[lay] * 4, out_shape=[shape] * 4,
        input_output_aliases={4 + k: k for k in range(n_prev)},
        compiler_params=_params(("parallel",)),
    )(w, g, m, v, *(prev or ()))


def _adamw_cols_major(w, gs, m, v, name):
    depth, rows, cols = w.shape
    tr = max(t for t in range(1, 129) if cols % t == 0)
    wt, mt, vt = (jnp.transpose(a, (2, 0, 1)) for a in (w, m, v))
    gt = jnp.stack([g.T for g in gs], axis=1)

    def body(w_ref, g_ref, m_ref, v_ref, g_out, d_out, m_out, v_out):
        gv = g_ref[...]
        g_out[...] = gv
        d_out[...], m_out[...], v_out[...] = _adamw_math(w_ref[...], gv, m_ref[...], v_ref[...])

    spec = pl.BlockSpec((tr, depth, rows), lambda r: (r, 0, 0))
    shape = jax.ShapeDtypeStruct((cols, depth, rows), F32)
    outs = pl.pallas_call(body, name="adamw_" + name, grid=(cols // tr,), in_specs=[spec] * 4, out_specs=[spec] * 4,
                          out_shape=[shape] * 4, compiler_params=_params(("parallel",)))(wt, gt, mt, vt)
    return [jnp.transpose(a, (1, 2, 0)) for a in outs]


def _sum_leading(buf, name):
    n, rows, cols = buf.shape
    tr = _row_tile(rows, rows)

    def body(in_ref, out_ref):
        acc = in_ref[0]
        for k in range(1, n):
            acc = acc + in_ref[k]
        out_ref[...] = acc

    return pl.pallas_call(
        body, name=name, grid=(rows // tr,),
        in_specs=[pl.BlockSpec((n, tr, cols), lambda i: (0, i, 0))],
        out_specs=pl.BlockSpec((tr, cols), lambda i: (i, 0)),
        out_shape=jax.ShapeDtypeStruct((rows, cols), F32),
        compiler_params=_params(("parallel",)),
    )(buf)


_SHARDED_SMALL = ("meta", "conv_a_w", "ssm_conv_w", "conf_conv_w")
_LAYER_SMALL = ("pre_g", "post_g", "conv_a_w", "ssm_conv_w", "ssm_conv_b", "dt_bias", "a_log", "d_skip",
                "ssm_norm_g", "conf_conv_w", "conf_conv_b", "conf_ln_g", "conf_ln_b")
_WEIGHTS = ("meta", "pre_g", "post_g", "w_in", "w_out", "conv_a_w", "ssm_conv_w", "ssm_conv_b", "dt_bias", "a_log",
            "d_skip", "ssm_norm_g", "conf_conv_w", "conf_conv_b", "conf_ln_g", "conf_ln_b")


def _shard_last(a):
    return jnp.moveaxis(a.reshape(a.shape[:-1] + (N_CHIPS, a.shape[-1] // N_CHIPS)), -2, 0)


def _with_own_block(a, n, at):
    return lax.dynamic_update_index_in_dim(jnp.zeros((n,) + a.shape, a.dtype), a, at, 0)


def _with_own_columns(a, chip):
    k = a.shape[-1]
    return lax.dynamic_update_slice_in_dim(jnp.zeros(a.shape[:-1] + (N_CHIPS * k,), a.dtype), a, chip * k, a.ndim - 1)


def kernel(x, meta, pre_g, post_g, w_in, w_out, conv_a_w, ssm_conv_w, ssm_conv_b, dt_bias, a_log, d_skip, ssm_norm_g, conf_conv_w, conf_conv_b, conf_ln_g, conf_ln_b, loss_target, m_meta, m_pre_g, m_post_g, m_w_in, m_w_out, m_conv_a_w, m_ssm_conv_w, m_ssm_conv_b, m_dt_bias, m_a_log, m_d_skip, m_ssm_norm_g, m_conf_conv_w, m_conf_conv_b, m_conf_ln_g, m_conf_ln_b, v_meta, v_pre_g, v_post_g, v_w_in, v_w_out, v_conv_a_w, v_ssm_conv_w, v_ssm_conv_b, v_dt_bias, v_a_log, v_d_skip, v_ssm_norm_g, v_conf_conv_w, v_conf_conv_b, v_conf_ln_g, v_conf_ln_b):
    w = dict(meta=meta, pre_g=pre_g, post_g=post_g, w_in=w_in, w_out=w_out, conv_a_w=conv_a_w,
             ssm_conv_w=ssm_conv_w, ssm_conv_b=ssm_conv_b, dt_bias=dt_bias, a_log=a_log, d_skip=d_skip,
             ssm_norm_g=ssm_norm_g, conf_conv_w=conf_conv_w, conf_conv_b=conf_conv_b, conf_ln_g=conf_ln_g,
             conf_ln_b=conf_ln_b)
    mom = dict(meta=m_meta, pre_g=m_pre_g, post_g=m_post_g, w_in=m_w_in, w_out=m_w_out, conv_a_w=m_conv_a_w,
               ssm_conv_w=m_ssm_conv_w, ssm_conv_b=m_ssm_conv_b, dt_bias=m_dt_bias, a_log=m_a_log, d_skip=m_d_skip,
               ssm_norm_g=m_ssm_norm_g, conf_conv_w=m_conf_conv_w, conf_conv_b=m_conf_conv_b,
               conf_ln_g=m_conf_ln_g, conf_ln_b=m_conf_ln_b)
    vel = dict(meta=v_meta, pre_g=v_pre_g, post_g=v_post_g, w_in=v_w_in, w_out=v_w_out, conv_a_w=v_conv_a_w,
               ssm_conv_w=v_ssm_conv_w, ssm_conv_b=v_ssm_conv_b, dt_bias=v_dt_bias, a_log=v_a_log, d_skip=v_d_skip,
               ssm_norm_g=v_ssm_norm_g, conf_conv_w=v_conf_conv_w, conf_conv_b=v_conf_conv_b,
               conf_ln_g=v_conf_ln_g, conf_ln_b=v_conf_ln_b)
    bl, seq, d = x.shape
    dm = Dims(bl, seq, d)
    depth = w_in.shape[0]
    chip = (2 * lax.axis_index("x") + lax.axis_index("y")).astype(jnp.int32)
    dev = 2 * chip + lax.axis_index("c").astype(jnp.int32)
    cst = _head_consts(dm)

    bases = [[_place_own(w_in, i, chip.reshape(1)), _place_own(w_out, i, chip.reshape(1))] for i in range(depth)]
    first_in, small_w = _gather_ici_relayed(
        [bases[0][0]], _ride_gather_small([_with_own_columns(w[n], chip) for n in _SHARDED_SMALL]))
    full = dict(w)
    full.update(zip(_SHARDED_SMALL, small_w))
    h, gathered = _embed(x, full["meta"], dm, _ride_gather_d2d(first_in))
    saved, proj_w = [], []
    for i in range(depth):
        lw = {n: full[n][i] for n in _LAYER_SMALL}
        w_in_i = _w_in_from_shards(gathered[0], dm)
        h, keep, gathered, w_out_i = _layer_fwd(
            h, lw, w_in_i, None if i == 0 else gathered[1].reshape(2 * d, d), cst, dm,
            bases[i + 1] if i + 1 < depth else None, bases[0][1] if i == 0 else None)
        proj_w.append((w_in_i, w_out_i))
        saved.append(keep)

    dh, loss = _loss_head(h, loss_target, dm)
    loss = lax.psum(loss, ("x", "y", "c"))

    small_g = {n: [None] * depth for n in _LAYER_SMALL}
    big = {"w_in": None, "w_out": None}
    g_in = [None] * depth
    reduce = None
    for i in reversed(range(depth)):
        lw = {n: full[n][i] for n in _LAYER_SMALL}
        dh, mine, sg = _layer_bwd(dh, saved[i], lw, proj_w[i][0], proj_w[i][1], cst, dm, reduce, last=i == 0)
        for n in _LAYER_SMALL:
            small_g[n][i] = sg[n]
        if reduce is not None:
            g_in[i + 1] = reduce.result[0]
            big["w_out"] = _adamw_layer(i + 1, w_out, reduce.result[1], m_w_out, v_w_out, big["w_out"], "w_out")
        reduce = mine
    g_in[0] = reduce.result[0]
    big["w_out"] = _adamw_layer(0, w_out, reduce.result[1], m_w_out, v_w_out, big["w_out"], "w_out")
    grad_x, gmeta = _unembed(dh, dm)

    g = {n: jnp.stack(v) for n, v in small_g.items()}
    g["meta"] = gmeta
    small = [n for n in _WEIGHTS if n not in ("w_in", "w_out")]
    flat = jnp.concatenate([g[n].reshape(-1) for n in small])
    rows = -(-flat.shape[0] // (16 * LANES)) * 16
    flat = jnp.pad(flat, (0, rows * LANES - flat.shape[0])).reshape(rows, LANES)
    parts = _gather_all(_with_own_block(flat, N_DEV, dev))
    total = _sum_leading(parts, "small_grads_sum").reshape(-1)
    big["w_in"] = _adamw_cols_major(w_in, g_in, m_w_in, v_w_in, "w_in")
    grads, deltas, new_m, new_v = {}, {}, {}, {}
    off = 0
    for n in small:
        size = g[n].size
        fullg = total[off:off + size].reshape(g[n].shape)
        off += size
        if n in _SHARDED_SMALL:
            fullg = lax.dynamic_index_in_dim(_shard_last(fullg), chip, axis=0, keepdims=False)
        grads[n] = fullg
        deltas[n], new_m[n], new_v[n] = _adamw_small(w[n], fullg, mom[n], vel[n], n)
    for n in ("w_in", "w_out"):
        grads[n], deltas[n], new_m[n], new_v[n] = big[n]

    return (loss, grad_x, *[grads[n] for n in _WEIGHTS], *[deltas[n] for n in _WEIGHTS],
            *[new_m[n] for n in _WEIGHTS], *[new_v[n] for n in _WEIGHTS])
```

```python
import jax
import jax.numpy as jnp
from jax import lax
from jax.experimental import pallas as pl
from jax.experimental.pallas import tpu as pltpu

F32 = jnp.float32
BF16 = jnp.bfloat16

N_META = 16
TT = 128
SSM_STATE = 128
SSM_GROUPS = 2
SSM_HEAD_DIM = 64
CONV_A_K = 3
SSM_CONV_K = 4
CONF_K = 31
NORM_EPS = 1e-6
LN_EPS = 1e-5
LANES = 128
MXU_DIM = 256
DT_PAD = LANES
CONF_HALO = 32
SMALL_HALO = 8
VMEM_LIMIT = 56 * 1024 * 1024
N_CHIPS = 4
N_DEV = 8

ADAM_LR = 0.001
ADAM_B1 = 0.9
ADAM_B2 = 0.999
ADAM_EPS = 1e-08
ADAM_WD = 0.01
ADAM_STEP = 10

MESH = pl.DeviceIdType.MESH
ANY = pl.BlockSpec(memory_space=pl.ANY)


class Dims:
    def __init__(self, bl, seq, d):
        self.BL, self.S, self.D = bl, seq, d
        self.L = seq + N_META
        self.Lp = -(-self.L // TT) * TT
        self.NT = self.Lp // TT
        self.R = bl * self.Lp
        self.DA = d // 2
        self.DB = d
        self.DC = d // 2
        self.H = self.DB // SSM_HEAD_DIM
        self.HPG = self.H // SSM_GROUPS
        self.GN = SSM_GROUPS * SSM_STATE
        self.WA = 4 * self.DA
        self.WB = 2 * self.DB + 2 * self.GN
        self.WC = 3 * self.DC
        self.DT0 = self.WA + self.WB + self.WC
        self.NP = -(-(self.DT0 + DT_PAD) // (5 * MXU_DIM)) * (5 * MXU_DIM)
        self.NIN = self.WA + self.WB + self.H + self.WC
        self.XBC = self.DB + 2 * self.GN
        assert self.H % 2 == 0 and self.HPG % 2 == 0 and self.H <= DT_PAD
        assert self.DA % LANES == 0 and (self.WA + self.WB) % self.DC == 0 and self.WA % self.DB == 0


def _row_tile(n, target):
    best = None
    for t in range(16, min(n, target) + 1, 16):
        if n % t == 0:
            best = t
    assert best is not None
    return best


def _col_tile(n, target):
    best = None
    for t in range(LANES, min(n, target) + 1, LANES):
        if n % t == 0:
            best = t
    assert best is not None
    return best


def _params(sem=None):
    return pltpu.CompilerParams(dimension_semantics=sem, vmem_limit_bytes=VMEM_LIMIT)


def _sigmoid(x):
    return 1.0 / (1.0 + jnp.exp(-x))


def _silu_and_grad(x):
    s = _sigmoid(x)
    y = x * s
    return y, s + y * (1.0 - s)


def _dot(a, b):
    return jnp.dot(a, b, preferred_element_type=F32)


def _dot_nt(a, b):
    return lax.dot_general(a, b, (((1,), (1,)), ((), ())), preferred_element_type=F32)


def _dot_tn(a, b):
    return lax.dot_general(a, b, (((0,), (0,)), ((), ())), preferred_element_type=F32)


def _split3(x):
    x1 = x.astype(BF16)
    r1 = x - x1.astype(F32)
    x2 = r1.astype(BF16)
    x3 = (r1 - x2.astype(F32)).astype(BF16)
    return x1, x2, x3


class Rider:
    def __init__(self, plan, ins, out_shapes, aliases, nsem):
        self.plan, self.ins, self.out_shapes, self.aliases, self.nsem = plan, list(ins), list(out_shapes), aliases, nsem


def _place():
    x, y, c = lax.axis_index("x"), lax.axis_index("y"), lax.axis_index("c")
    chips = [(1 - x, y), (x, 1 - y), (1 - x, 1 - y)]
    return x, y, c, chips


def _remote(k, src, dst, to, send_sems, recv_sems):
    return pltpu.make_async_remote_copy(src_ref=src, dst_ref=dst, send_sem=send_sems.at[k], recv_sem=recv_sems.at[k],
                                        device_id=to, device_id_type=MESH)


def _call(body, name, grid, in_specs, out_specs, out_shape, scratch_shapes, sem, args, rider=None):
    if rider is None:
        outs = pl.pallas_call(body, name=name, grid=grid, in_specs=in_specs, out_specs=out_specs, out_shape=out_shape,
                              scratch_shapes=scratch_shapes, compiler_params=_params(sem))(*args)
        return list(outs), []
    n_in, n_out, n_scr = len(args), len(out_shape), len(scratch_shapes)
    r_in, r_out = len(rider.ins), len(rider.out_shapes)

    def hosted(*refs):
        ins, rins = refs[:n_in], refs[n_in:n_in + r_in]
        o0 = n_in + r_in
        outs, routs = refs[o0:o0 + n_out], refs[o0 + n_out:o0 + n_out + r_out]
        scr = refs[o0 + n_out + r_out:o0 + n_out + r_out + n_scr]
        send_sems, recv_sems = refs[o0 + n_out + r_out + n_scr:]
        first = pl.program_id(0) == 0
        last = pl.program_id(0) == grid[0] - 1
        for ax in range(1, len(grid)):
            first = jnp.logical_and(first, pl.program_id(ax) == 0)
            last = jnp.logical_and(last, pl.program_id(ax) == grid[ax] - 1)

        @pl.when(first)
        def _():
            starts, _ = rider.plan(rins, routs, send_sems, recv_sems)
            for cp in starts:
                cp.start()

        body(*ins, *outs, *scr)

        @pl.when(last)
        def _():
            _, waits = rider.plan(rins, routs, send_sems, recv_sems)
            for wait in waits:
                wait()

    res = pl.pallas_call(
        hosted, name=name, grid=grid,
        in_specs=list(in_specs) + [ANY] * r_in, out_specs=list(out_specs) + [ANY] * r_out,
        out_shape=list(out_shape) + rider.out_shapes,
        input_output_aliases={n_in + k: n_out + v for k, v in rider.aliases.items()},
        scratch_shapes=list(scratch_shapes) + [pltpu.SemaphoreType.DMA((rider.nsem,)),
                                               pltpu.SemaphoreType.DMA((rider.nsem,))],
        compiler_params=_params(("arbitrary",) * len(grid)),
    )(*args, *rider.ins)
    return list(res[:n_out]), list(res[n_out:])


def _exchange(name, rider):
    r_in, r_out = len(rider.ins), len(rider.out_shapes)

    def body(*refs):
        rins, routs = refs[:r_in], refs[r_in:r_in + r_out]
        send_sems, recv_sems = refs[r_in + r_out:]
        starts, waits = rider.plan(rins, routs, send_sems, recv_sems)
        for cp in starts:
            cp.start()
        for wait in waits:
            wait()

    res = pl.pallas_call(
        body, name=name, in_specs=[ANY] * r_in, out_specs=[ANY] * r_out, out_shape=rider.out_shapes,
        input_output_aliases=dict(rider.aliases),
        scratch_shapes=[pltpu.SemaphoreType.DMA((rider.nsem,)), pltpu.SemaphoreType.DMA((rider.nsem,))],
    )(*rider.ins)
    return list(res)


def _same(arrays):
    return [jax.ShapeDtypeStruct(a.shape, a.dtype) for a in arrays]


class _SemsFrom:
    def __init__(self, sems, first):
        self.sems, self.first = sems, first

    @property
    def at(self):
        return self

    def __getitem__(self, k):
        return self.sems.at[self.first + k]


def _ride_both(r1, r2):
    n_in, n_out = len(r1.ins), len(r1.out_shapes)

    def plan(ins, outs, ss, rs):
        s1, w1 = r1.plan(ins[:n_in], outs[:n_out], ss, rs)
        s2, w2 = r2.plan(ins[n_in:], outs[n_out:], _SemsFrom(ss, r1.nsem), _SemsFrom(rs, r1.nsem))
        return s1 + s2, w1 + w2

    aliases = dict(r1.aliases)
    aliases.update({n_in + k: n_out + v for k, v in r2.aliases.items()})
    return Rider(plan, r1.ins + r2.ins, r1.out_shapes + r2.out_shapes, aliases, r1.nsem + r2.nsem)


def _ride_gather_ici(bases, part=0, nparts=1):
    n = len(bases)

    def plan(ins, outs, ss, rs):
        x, y, c, chips = _place()
        me = 2 * x + y
        starts, waits = [], []
        for a in range(n):
            half = outs[a].shape[1] // 2
            mine = pl.ds(c * half + part * (half // nparts), half // nparts)
            for j, chip in enumerate(chips):
                cp = _remote(3 * a + j, outs[a].at[me, mine], outs[a].at[me, mine], (*chip, c), ss, rs)
                got = outs[a].at[2 * chip[0] + chip[1], mine]
                starts.append(cp)
                waits += [cp.wait_send, _remote(3 * a + j, got, got, (*chip, c), ss, rs).wait_recv]
        return starts, waits

    return Rider(plan, bases, _same(bases), {a: a for a in range(n)}, 3 * n)


def _gather_ici_relayed(bases, also):
    n, m = len(bases), len(also.ins)

    def body(*refs):
        outs = refs[n + m:2 * n + m]
        ss, rs = refs[2 * (n + m):]
        beside, beside_waits = also.plan(refs[n:n + m], refs[2 * n + m:2 * (n + m)],
                                         _SemsFrom(ss, 4 * n), _SemsFrom(rs, 4 * n))
        for cp in beside:
            cp.start()
        x, y, c, _ = _place()
        me, xn, yn, dg = 2 * x + y, 2 * (1 - x) + y, 2 * x + (1 - y), 2 * (1 - x) + (1 - y)
        to_x, to_y = (1 - x, y, c), (x, 1 - y, c)
        sends = []

        def send(k, piece, to):
            cp = _remote(k, piece, piece, to, ss, rs)
            cp.start()
            sends.append(cp)

        def arrived(k, piece, frm):
            _remote(k, piece, piece, frm, ss, rs).wait_recv()

        rows = []
        for a in range(n):
            half = outs[a].shape[1] // 2
            rows.append((pl.ds(c * half, half), pl.ds(c * half, half // 2), pl.ds(c * half + half // 2, half // 2)))
            send(4 * a, outs[a].at[me, rows[a][0]], to_x)
            send(4 * a + 1, outs[a].at[me, rows[a][0]], to_y)
        for a in range(n):
            mine, lo, hi = rows[a]
            arrived(4 * a, outs[a].at[xn, mine], to_x)
            send(4 * a + 2, outs[a].at[xn, lo], to_y)
            arrived(4 * a + 1, outs[a].at[yn, mine], to_y)
            send(4 * a + 3, outs[a].at[yn, hi], to_x)
        for a in range(n):
            mine, lo, hi = rows[a]
            arrived(4 * a + 2, outs[a].at[dg, lo], to_y)
            arrived(4 * a + 3, outs[a].at[dg, hi], to_x)
        for cp in sends:
            cp.wait_send()
        for wait in beside_waits:
            wait()

    aliases = {a: a for a in range(n)}
    aliases.update({n + k: n + v for k, v in also.aliases.items()})
    nsem = 4 * n + also.nsem
    res = pl.pallas_call(
        body, name="gather_ici_first", in_specs=[ANY] * (n + m), out_specs=[ANY] * (n + len(also.out_shapes)),
        out_shape=_same(bases) + also.out_shapes, input_output_aliases=aliases,
        scratch_shapes=[pltpu.SemaphoreType.DMA((nsem,)), pltpu.SemaphoreType.DMA((nsem,))],
    )(*bases, *also.ins)
    return list(res[:n]), list(res[n:])


def _ride_gather_d2d(bases):
    n = len(bases)

    def plan(ins, outs, ss, rs):
        x, y, c, chips = _place()
        sib = (x, y, 1 - c)
        starts, waits = [], []
        for a in range(n):
            half = outs[a].shape[1] // 2
            for j, chip in enumerate(chips):
                frm = 2 * chip[0] + chip[1]
                got = outs[a].at[frm, pl.ds(c * half, half)]
                theirs = outs[a].at[frm, pl.ds((1 - c) * half, half)]
                cp = _remote(3 * a + j, got, got, sib, ss, rs)
                starts.append(cp)
                waits += [cp.wait_send, _remote(3 * a + j, theirs, theirs, sib, ss, rs).wait_recv]
        return starts, waits

    return Rider(plan, bases, _same(bases), {a: a for a in range(n)}, 3 * n)


def _ride_gather_small(bases):
    n = len(bases)

    def plan(ins, outs, ss, rs):
        x, y, c, chips = _place()
        me = 2 * x + y
        starts, waits = [], []
        for a in range(n):
            k = outs[a].shape[-1] // N_CHIPS
            lead = (slice(None),) * (len(outs[a].shape) - 1)
            at = (lambda s: pl.multiple_of(s * k, LANES)) if k % LANES == 0 else (lambda s: s * k)
            cols = lambda s: outs[a].at[lead + (pl.ds(at(s), k),)]
            for j, chip in enumerate(chips):
                cp = _remote(3 * a + j, cols(me), cols(me), (*chip, c), ss, rs)
                got = cols(2 * chip[0] + chip[1])
                starts.append(cp)
                waits += [cp.wait_send, _remote(3 * a + j, got, got, (*chip, c), ss, rs).wait_recv]
        return starts, waits

    return Rider(plan, bases, _same(bases), {a: a for a in range(n)}, 3 * n)


def _ride_swap_halves(gs):
    n = len(gs)

    def plan(ins, outs, ss, rs):
        x, y, c, _ = _place()
        cps = [_remote(a, ins[a].at[:, 1 - c], outs[a], (x, y, 1 - c), ss, rs) for a in range(n)]
        return cps, [cp.wait for cp in cps]

    shapes = [jax.ShapeDtypeStruct((g.shape[0],) + g.shape[2:], g.dtype) for g in gs]
    return Rider(plan, gs, shapes, {}, n)


def _ride_to_owners(ps):
    n = len(ps)

    def plan(ins, outs, ss, rs):
        x, y, c, chips = _place()
        cps = []
        for a in range(n):
            for j, chip in enumerate(chips):
                cps.append(_remote(3 * a + j, ins[a].at[2 * chip[0] + chip[1]], outs[a].at[j], (*chip, c), ss, rs))
        return cps, [cp.wait for cp in cps]

    shapes = [jax.ShapeDtypeStruct((3,) + p.shape[1:], p.dtype) for p in ps]
    return Rider(plan, ps, shapes, {}, 3 * n)


def _ride_join_halves(qs):
    n = len(qs)

    def plan(ins, outs, ss, rs):
        x, y, c, _ = _place()
        sib = (x, y, 1 - c)
        starts, waits = [], []
        for a in range(n):
            cp = _remote(a, outs[a].at[c], outs[a].at[c], sib, ss, rs)
            starts.append(cp)
            waits += [cp.wait_send, _remote(a, outs[a].at[1 - c], outs[a].at[1 - c], sib, ss, rs).wait_recv]
        return starts, waits

    return Rider(plan, qs, _same(qs), {a: a for a in range(n)}, n)


def _gather_all(base):
    def body(in_ref, out_ref, ss, rs):
        x, y, c, chips = _place()
        sib = (x, y, 1 - c)
        block = lambda cx, cy, cc: out_ref.at[4 * cx + 2 * cy + cc]
        mine = block(x, y, c)
        first = [_remote(j, mine, mine, (*chip, c), ss, rs) for j, chip in enumerate(chips)]
        first.append(_remote(3, mine, mine, sib, ss, rs))
        for cp in first:
            cp.start()
        passed = []
        for j, chip in enumerate(chips):
            got = block(*chip, c)
            _remote(j, got, got, (*chip, c), ss, rs).wait_recv()
            passed.append(_remote(4 + j, got, got, sib, ss, rs))
            passed[-1].start()
        theirs = block(x, y, 1 - c)
        _remote(3, theirs, theirs, sib, ss, rs).wait_recv()
        for j, chip in enumerate(chips):
            got = block(*chip, 1 - c)
            _remote(4 + j, got, got, sib, ss, rs).wait_recv()
        for cp in first + passed:
            cp.wait_send()

    return pl.pallas_call(
        body, name="small_grads_gather_all", in_specs=[ANY], out_specs=ANY,
        out_shape=jax.ShapeDtypeStruct(base.shape, base.dtype), input_output_aliases={0: 0},
        scratch_shapes=[pltpu.SemaphoreType.DMA((N_DEV - 1,)), pltpu.SemaphoreType.DMA((N_DEV - 1,))],
    )(base)


def _embed(x, meta, dm, rider=None):
    dc = _col_tile(dm.D, 256)
    s, lp = dm.S, dm.Lp

    def body(x_ref, meta_ref, h_ref):
        h_ref[0:N_META, :] = meta_ref[...]
        h_ref[N_META:N_META + s, :] = x_ref[0]
        if lp > N_META + s:
            h_ref[N_META + s:lp, :] = jnp.zeros((lp - N_META - s, dc), F32)

    (h,), rode = _call(
        body, "embed", (dm.BL, dm.D // dc),
        [pl.BlockSpec((1, s, dc), lambda b, j: (b, 0, j)), pl.BlockSpec((N_META, dc), lambda b, j: (0, j))],
        [pl.BlockSpec((lp, dc), lambda b, j: (b, j))], [jax.ShapeDtypeStruct((dm.R, dm.D), F32)],
        [], ("parallel", "parallel"), (x, meta), rider)
    return h, rode


def _loss_head(h, target, dm):
    dc = _col_tile(dm.D, 256)
    s, lp, nj = dm.S, dm.Lp, dm.D // dc

    def body(h_ref, t_ref, dh_ref, l_ref):
        diff = h_ref[N_META:N_META + s, :] - t_ref[0]
        dh_ref[0:N_META, :] = jnp.zeros((N_META, dc), F32)
        dh_ref[N_META:N_META + s, :] = diff * (1.0 / dm.D)
        if lp > N_META + s:
            dh_ref[N_META + s:lp, :] = jnp.zeros((lp - N_META - s, dc), F32)
        l_ref[...] = jnp.full((8, LANES), (0.5 / dm.D) * jnp.sum(diff * diff), F32)

    dh, part = pl.pallas_call(
        body, name="loss_head", grid=(dm.BL, nj),
        in_specs=[pl.BlockSpec((lp, dc), lambda b, j: (b, j)),
                  pl.BlockSpec((1, s, dc), lambda b, j: (b, 0, j))],
        out_specs=[pl.BlockSpec((lp, dc), lambda b, j: (b, j)),
                   pl.BlockSpec((8, LANES), lambda b, j: (b * nj + j, 0))],
        out_shape=[jax.ShapeDtypeStruct((dm.R, dm.D), F32),
                   jax.ShapeDtypeStruct((dm.BL * nj * 8, LANES), F32)],
        compiler_params=_params(("parallel", "parallel")),
    )(h, target)
    return dh, jnp.sum(part[::8, 0])


def _unembed(dh, dm):
    dc = _col_tile(dm.D, 256)
    s, lp = dm.S, dm.Lp

    def body(dh_ref, gx_ref, gm_ref):
        gx_ref[0] = dh_ref[N_META:N_META + s, :]

        @pl.when(pl.program_id(1) == 0)
        def _():
            gm_ref[...] = dh_ref[0:N_META, :]

        @pl.when(pl.program_id(1) > 0)
        def _():
            gm_ref[...] = gm_ref[...] + dh_ref[0:N_META, :]

    return pl.pallas_call(
        body, name="unembed", grid=(dm.D // dc, dm.BL),
        in_specs=[pl.BlockSpec((lp, dc), lambda j, b: (b, j))],
        out_specs=[pl.BlockSpec((1, s, dc), lambda j, b: (b, 0, j)),
                   pl.BlockSpec((N_META, dc), lambda j, b: (0, j))],
        out_shape=[jax.ShapeDtypeStruct((dm.BL, s, dm.D), F32),
                   jax.ShapeDtypeStruct((N_META, dm.D), F32)],
        compiler_params=_params(("parallel", "arbitrary")),
    )(dh)


def _fwd_in(h, pre_g, w, dm, rider=None):
    tm = _row_tile(dm.R, 1088)
    tn = _col_tile(dm.NP, 5 * MXU_DIM)
    nj = dm.NP // tn

    def body(h_ref, g_ref, w_ref, wdt_ref, proj_ref, dt_ref, hn_ref):
        @pl.when(pl.program_id(1) == 0)
        def _():
            xf = h_ref[...]
            r = lax.rsqrt(jnp.mean(xf * xf, axis=-1, keepdims=True) + NORM_EPS)
            hn_ref[...] = (xf * r * g_ref[...]).astype(BF16)
            dt_ref[...] = _dot(hn_ref[...], wdt_ref[...])

        proj_ref[...] = _dot(hn_ref[...], w_ref[...]).astype(BF16)

    return _call(
        body, "fwd_in", (dm.R // tm, nj),
        [pl.BlockSpec((tm, dm.D), lambda i, j: (i, 0)),
         pl.BlockSpec((1, dm.D), lambda i, j: (0, 0)),
         pl.BlockSpec((dm.D, tn), lambda i, j: (0, j)),
         pl.BlockSpec((dm.D, DT_PAD), lambda i, j: (0, dm.DT0 // DT_PAD))],
        [pl.BlockSpec((tm, tn), lambda i, j: (i, j)),
         pl.BlockSpec((tm, DT_PAD), lambda i, j: (i, 0)),
         pl.BlockSpec((tm, dm.D), lambda i, j: (i, 0))],
        [jax.ShapeDtypeStruct((dm.R, dm.NP), BF16), jax.ShapeDtypeStruct((dm.R, DT_PAD), F32),
         jax.ShapeDtypeStruct((dm.R, dm.D), BF16)],
        [], ("parallel", "arbitrary"), (h, pre_g, w, w), rider)


def _fwd_out(ya, yb, yc, w_out, h, post_g, dm, rider=None):
    tm = _row_tile(dm.Lp, 544)
    tiles_per_seq = dm.Lp // tm
    da, db, dc = dm.DA, dm.DB, dm.DC

    def body(ya_ref, yb_ref, yc_ref, w_ref, h_ref, g_ref, hn_ref, m_ref):
        m = _dot(ya_ref[...], w_ref[0:da, :])
        m = m + _dot(yb_ref[...], w_ref[da:da + db, :])
        m = m + _dot(yc_ref[...], w_ref[da + db:da + db + dc, :])
        m_ref[...] = m
        r = lax.rsqrt(jnp.mean(m * m, axis=-1, keepdims=True) + NORM_EPS)
        t = (pl.program_id(0) % tiles_per_seq) * tm + lax.broadcasted_iota(jnp.int32, (tm, 1), 0)
        keep = (t < dm.L).astype(F32)
        hn_ref[...] = (h_ref[...] + m * r * g_ref[...]) * keep

    row = lambda i: (i, 0)
    fixed = lambda i: (0, 0)
    return _call(
        body, "fwd_out", (dm.R // tm,),
        [pl.BlockSpec((tm, da), row), pl.BlockSpec((tm, db), row), pl.BlockSpec((tm, dc), row),
         pl.BlockSpec((2 * dm.D, dm.D), fixed), pl.BlockSpec((tm, dm.D), row), pl.BlockSpec((1, dm.D), fixed)],
        [pl.BlockSpec((tm, dm.D), row), pl.BlockSpec((tm, dm.D), row)],
        [jax.ShapeDtypeStruct((dm.R, dm.D), F32), jax.ShapeDtypeStruct((dm.R, dm.D), F32)],
        [], ("parallel",), (ya, yb, yc, w_out, h, post_g), rider)


def _bwd_out(dh, m, post_g, w_out, ya, yb, yc, dm, rider=None):
    tm = _row_tile(dm.R, MXU_DIM)
    da, db, dc = dm.DA, dm.DB, dm.DC

    def body(dh_ref, m_ref, g_ref, w_ref, ya_ref, yb_ref, yc_ref, dya_ref, dyb_ref, dyc_ref, dw_ref, dg_ref):
        @pl.when(pl.program_id(0) == 0)
        def _():
            dw_ref[...] = jnp.zeros_like(dw_ref)
            dg_ref[...] = jnp.zeros_like(dg_ref)

        m = m_ref[...]
        dh_ = dh_ref[...]
        r = lax.rsqrt(jnp.mean(m * m, axis=-1, keepdims=True) + NORM_EPS)
        n = m * r
        dg_ref[0:1, :] = dg_ref[0:1, :] + jnp.sum(dh_ * n, axis=0, keepdims=True)
        dn = dh_ * g_ref[...]
        dm_ = (r * (dn - n * jnp.mean(dn * n, axis=-1, keepdims=True))).astype(BF16)
        dya_ref[...] = _dot_nt(dm_, w_ref[0:da, :])
        dyb_ref[...] = _dot_nt(dm_, w_ref[da:da + db, :])
        dyc_ref[...] = _dot_nt(dm_, w_ref[da + db:da + db + dc, :])
        dw_ref[0:da, :] = dw_ref[0:da, :] + _dot_tn(ya_ref[...], dm_)
        dw_ref[da:da + db, :] = dw_ref[da:da + db, :] + _dot_tn(yb_ref[...], dm_)
        dw_ref[da + db:da + db + dc, :] = dw_ref[da + db:da + db + dc, :] + _dot_tn(yc_ref[...], dm_)

    row = lambda i: (i, 0)
    fixed = lambda i: (0, 0)
    return _call(
        body, "bwd_out", (dm.R // tm,),
        [pl.BlockSpec((tm, dm.D), row), pl.BlockSpec((tm, dm.D), row), pl.BlockSpec((1, dm.D), fixed),
         pl.BlockSpec((2 * dm.D, dm.D), fixed),
         pl.BlockSpec((tm, da), row), pl.BlockSpec((tm, db), row), pl.BlockSpec((tm, dc), row)],
        [pl.BlockSpec((tm, da), row), pl.BlockSpec((tm, db), row), pl.BlockSpec((tm, dc), row),
         pl.BlockSpec((2 * dm.D, dm.D), fixed), pl.BlockSpec((8, dm.D), fixed)],
        [jax.ShapeDtypeStruct((dm.R, da), F32), jax.ShapeDtypeStruct((dm.R, db), F32),
         jax.ShapeDtypeStruct((dm.R, dc), F32),
         jax.ShapeDtypeStruct((2 * dm.D, dm.D), F32), jax.ShapeDtypeStruct((8, dm.D), F32)],
        [], ("arbitrary",), (dh, m, post_g, w_out, ya, yb, yc), rider)


def _bwd_in_dx(dpa, dpb, dpc, dpt, w, h, dh, pre_g, dm, rider=None):
    tm = _row_tile(dm.R, 272)
    wa, wb, wc = dm.WA, dm.WB, dm.WC

    def body(dpa_ref, dpb_ref, dpc_ref, dpt_ref, w_ref, h_ref, dh_ref, g_ref, out_ref, dg_ref):
        @pl.when(pl.program_id(0) == 0)
        def _():
            dg_ref[...] = jnp.zeros_like(dg_ref)

        dhn = _dot_nt(dpa_ref[...], w_ref[:, 0:wa])
        dhn = dhn + _dot_nt(dpb_ref[...], w_ref[:, wa:wa + wb])
        dhn = dhn + _dot_nt(dpc_ref[...], w_ref[:, wa + wb:wa + wb + wc])
        dhn = dhn + _dot_nt(dpt_ref[...], w_ref[:, wa + wb + wc:wa + wb + wc + DT_PAD])
        xf = h_ref[...]
        r = lax.rsqrt(jnp.mean(xf * xf, axis=-1, keepdims=True) + NORM_EPS)
        n = xf * r
        dg_ref[0:1, :] = dg_ref[0:1, :] + jnp.sum(dhn * n, axis=0, keepdims=True)
        dn = dhn * g_ref[...]
        out_ref[...] = dh_ref[...] + r * (dn - n * jnp.mean(dn * n, axis=-1, keepdims=True))

    row = lambda i: (i, 0)
    fixed = lambda i: (0, 0)
    return _call(
        body, "bwd_in_dx", (dm.R // tm,),
        [pl.BlockSpec((tm, wa), row), pl.BlockSpec((tm, wb), row), pl.BlockSpec((tm, wc), row),
         pl.BlockSpec((tm, DT_PAD), row), pl.BlockSpec((dm.D, dm.NP), fixed),
         pl.BlockSpec((tm, dm.D), row), pl.BlockSpec((tm, dm.D), row), pl.BlockSpec((1, dm.D), fixed)],
        [pl.BlockSpec((tm, dm.D), row), pl.BlockSpec((8, dm.D), fixed)],
        [jax.ShapeDtypeStruct((dm.R, dm.D), F32), jax.ShapeDtypeStruct((8, dm.D), F32)],
        [], ("arbitrary",), (dpa, dpb, dpc, dpt, w, h, dh, pre_g), rider)


def _bwd_in_dw(hn, dps, dm):
    tn = [_col_tile(dp.shape[1], MXU_DIM) for dp in dps]
    nb = [dp.shape[1] // t for dp, t in zip(dps, tn)]
    first = [sum(nb[:p]) for p in range(len(dps))]
    at = lambda p: (lambda j: (0, jnp.clip(j - first[p], 0, nb[p] - 1)))

    def body(hn_ref, *refs):
        j = pl.program_id(0)
        for p in range(len(dps)):
            @pl.when(jnp.logical_and(j >= first[p], j < first[p] + nb[p]))
            def _(p=p):
                refs[len(dps) + p][...] = _dot_tn(hn_ref[...], refs[p][...])

    return pl.pallas_call(
        body, name="bwd_in_dw", grid=(sum(nb),),
        in_specs=[pl.BlockSpec((dm.R, dm.D), lambda j: (0, 0))] + [
            pl.BlockSpec((dm.R, tn[p]), at(p)) for p in range(len(dps))],
        out_specs=[pl.BlockSpec((dm.D, tn[p]), at(p)) for p in range(len(dps))],
        out_shape=[jax.ShapeDtypeStruct((dm.D, dp.shape[1]), F32) for dp in dps],
        compiler_params=_params(("arbitrary",)),
    )(hn, *dps)


def _tile_index(dm, reverse):
    if reverse:
        return lambda b, i: b * dm.NT + (dm.NT - 1 - i)
    return lambda b, i: b * dm.NT + i


def _halo_index(dm, rows):
    per_tile = TT // rows
    return lambda b, i: jnp.maximum((b * dm.NT + (dm.NT - 1 - i)) * per_tile - 1, 0)


HALO_BLOCK = 16


def _last_rows(x):
    return x.astype(F32)[HALO_BLOCK - SMALL_HALO:HALO_BLOCK]


MIX_A_ROWS = 544


def _mix_a_fwd(proj, conv_w, dm):
    da = dm.DA
    ta = _row_tile(dm.Lp, MIX_A_ROWS)
    nta = dm.Lp // ta
    bl = dm.BL

    def setup(ab_ref, ac_ref, ax_ref, az_ref, w_ref, y_ref, pbuf):
        i = pl.program_id(0)

        @pl.when(i == 0)
        def _():
            pbuf[0:SMALL_HALO, :] = jnp.zeros((SMALL_HALO, da), F32)

        @pl.when(i > 0)
        def _():
            pbuf[0:SMALL_HALO, :] = pbuf[ta:ta + SMALL_HALO, :]

    def body(ab_ref, ac_ref, ax_ref, az_ref, w_ref, y_ref, pbuf):
        for lb in range(da // LANES):
            cs = slice(lb * LANES, (lb + 1) * LANES)
            p = ac_ref[:, cs].astype(F32) * ax_ref[:, cs].astype(F32)
            pbuf[SMALL_HALO:SMALL_HALO + ta, cs] = p
            q = (w_ref[0:1, cs] * pbuf[6:6 + ta, cs] + w_ref[1:2, cs] * pbuf[7:7 + ta, cs] + w_ref[2:3, cs] * p)
            az = az_ref[:, cs].astype(F32)
            y_ref[:, cs] = (ab_ref[:, cs].astype(F32) * q * (az * _sigmoid(az))).astype(BF16)
            yield

    proj3 = proj.reshape(bl, dm.Lp, dm.NP)
    col = lambda k: pl.BlockSpec((bl, ta, da), lambda i: (0, i, k))
    return pl.pallas_call(
        _per_sequence(setup, body, bl, [True] * 4 + [False] + [True, True]), name="mix_a_fwd", grid=(nta,),
        in_specs=[col(0), col(1), col(2), col(3), pl.BlockSpec((CONV_A_K, da), lambda i: (0, 0))],
        out_specs=col(0),
        out_shape=jax.ShapeDtypeStruct((bl, dm.Lp, da), BF16),
        scratch_shapes=[pltpu.VMEM((bl, SMALL_HALO + ta, da), F32)],
        compiler_params=_params(("arbitrary",)),
    )(proj3, proj3, proj3, proj3, conv_w).reshape(dm.R, da)


def _mix_a_bwd(proj, dya, conv_w, dm):
    da = dm.DA
    ta = _row_tile(dm.Lp, MIX_A_ROWS)
    nta = dm.Lp // ta
    bl = dm.BL

    def setup(ab_ref, ac_ref, ax_ref, az_ref, ach_ref, axh_ref, dy_ref, w_ref, dp_ref, dw_ref, pbuf, dqbuf):
        i = pl.program_id(0)

        @pl.when(i == 0)
        def _():
            dw_ref[...] = jnp.zeros_like(dw_ref)
            dqbuf[ta:ta + SMALL_HALO, :] = jnp.zeros((SMALL_HALO, da), F32)

        @pl.when(i > 0)
        def _():
            dqbuf[ta:ta + SMALL_HALO, :] = dqbuf[0:SMALL_HALO, :]

    def body(ab_ref, ac_ref, ax_ref, az_ref, ach_ref, axh_ref, dy_ref, w_ref, dp_ref, dw_ref, pbuf, dqbuf):
        halo_on = jnp.where(pl.program_id(0) == nta - 1, 0.0, 1.0)
        for lb in range(da // LANES):
            cs = slice(lb * LANES, (lb + 1) * LANES)
            pbuf[0:SMALL_HALO, cs] = (_last_rows(ach_ref[:, cs]) * _last_rows(axh_ref[:, cs])) * halo_on
            ac, ax, ab, az = (r[:, cs].astype(F32) for r in (ac_ref, ax_ref, ab_ref, az_ref))
            p = ac * ax
            pbuf[SMALL_HALO:SMALL_HALO + ta, cs] = p
            p1 = pbuf[7:7 + ta, cs]
            p2 = pbuf[6:6 + ta, cs]
            w0, w1, w2 = w_ref[0:1, cs], w_ref[1:2, cs], w_ref[2:3, cs]
            q = w0 * p2 + w1 * p1 + w2 * p
            sz, dsz = _silu_and_grad(az)
            dy = dy_ref[:, cs]
            t1 = dy * ab
            dq = t1 * sz
            dqbuf[0:ta, cs] = dq
            dpv = w2 * dq + w1 * dqbuf[1:1 + ta, cs] + w0 * dqbuf[2:2 + ta, cs]
            dp_ref[:, lb * LANES:(lb + 1) * LANES] = (dy * q * sz).astype(BF16)
            dp_ref[:, da + lb * LANES:da + (lb + 1) * LANES] = (dpv * ax).astype(BF16)
            dp_ref[:, 2 * da + lb * LANES:2 * da + (lb + 1) * LANES] = (dpv * ac).astype(BF16)
            dp_ref[:, 3 * da + lb * LANES:3 * da + (lb + 1) * LANES] = (t1 * q * dsz).astype(BF16)
            dw_ref[0, 0:1, cs] = dw_ref[0, 0:1, cs] + jnp.sum(dq * p2, axis=0, keepdims=True)
            dw_ref[0, 1:2, cs] = dw_ref[0, 1:2, cs] + jnp.sum(dq * p1, axis=0, keepdims=True)
            dw_ref[0, 2:3, cs] = dw_ref[0, 2:3, cs] + jnp.sum(dq * p, axis=0, keepdims=True)
            yield

    proj3 = proj.reshape(bl, dm.Lp, dm.NP)
    col = lambda w, k: pl.BlockSpec((bl, ta, w), lambda i: (0, nta - 1 - i, k))
    halo = lambda k: pl.BlockSpec((bl, HALO_BLOCK, da),
                                  lambda i: (0, jnp.maximum((nta - 1 - i) * (ta // HALO_BLOCK) - 1, 0), k))
    dp, dw = pl.pallas_call(
        _per_sequence(setup, body, bl, [True] * 7 + [False] + [True, "keep"] + [True, True]),
        name="mix_a_bwd", grid=(nta,),
        in_specs=[col(da, 0), col(da, 1), col(da, 2), col(da, 3), halo(1), halo(2), col(da, 0),
                  pl.BlockSpec((CONV_A_K, da), lambda i: (0, 0))],
        out_specs=[col(dm.WA, 0), pl.BlockSpec((bl, 8, da), lambda i: (0, 0, 0))],
        out_shape=[jax.ShapeDtypeStruct((bl, dm.Lp, dm.WA), BF16), jax.ShapeDtypeStruct((bl, 8, da), F32)],
        scratch_shapes=[pltpu.VMEM((bl, SMALL_HALO + ta, da), F32), pltpu.VMEM((bl, ta + SMALL_HALO, da), F32)],
        compiler_params=_params(("arbitrary",)),
    )(proj3, proj3, proj3, proj3, proj3, proj3, dya.reshape(bl, dm.Lp, da), conv_w)
    return dp.reshape(dm.R, dm.WA), dw


SUBLANES = 8
SHIFT_ROWS = TT + CONF_HALO - SUBLANES


TAP_ROWS = 64


def _split_lanes(buf, rows, val):
    for lb in range(val.shape[1] // LANES):
        buf[lb, rows, :] = val[:, lb * LANES:(lb + 1) * LANES]


def _join_lanes(buf):
    return jnp.concatenate([buf[lb] for lb in range(buf.shape[0])], axis=1)


def _fill_shifted(buf, shifted):
    def step(lb, carry):
        for r in range(1, SUBLANES):
            shifted[lb, r - 1, 0:SHIFT_ROWS, :] = buf[lb, r:r + SHIFT_ROWS, :]
        return carry

    lax.fori_loop(0, buf.shape[0], step, 0)


def _window(buf, shifted, d, r0, lb):
    r = d % SUBLANES
    rows = pl.ds(pl.multiple_of(r0 + (d - r), SUBLANES), TAP_ROWS)
    return buf[lb, rows, :] if r == 0 else shifted[lb, r - 1, rows, :]


def _tap_loop(nlb, body):
    per_lb = TT // TAP_ROWS

    def step(it, carry):
        lb = it // per_lb
        body(lb, pl.ds(pl.multiple_of(lb * LANES, LANES), LANES), pl.multiple_of((it % per_lb) * TAP_ROWS, TAP_ROWS))
        return carry

    lax.fori_loop(0, nlb * per_lb, step, 0)


TAP_CHAINS = 4


def _tree_sum(terms):
    sums = list(terms[:TAP_CHAINS])
    for n, t in enumerate(terms[TAP_CHAINS:]):
        sums[n % TAP_CHAINS] = sums[n % TAP_CHAINS] + t
    while len(sums) > 1:
        sums = [a + b for a, b in zip(sums[0::2], sums[1::2])] + ([sums[-1]] if len(sums) % 2 else [])
    return sums[0]


def _conf_conv(ubuf, ushift, w_ref, b_ref, u1buf):
    _fill_shifted(ubuf, ushift)

    def piece(lb, cs, r0):
        taps = [w_ref[k:k + 1, cs] * _window(ubuf, ushift, CONF_HALO - (CONF_K - 1) + k, r0, lb)
                for k in range(CONF_K)]
        u1buf[lb, pl.ds(r0, TAP_ROWS), :] = _tree_sum(taps) + b_ref[0:1, cs]

    _tap_loop(ubuf.shape[0], piece)


def _mix_c_fwd(proj, conv_w, conv_b, ln_g, ln_b, dm):
    dc = dm.DC
    nlb = dc // LANES
    c0 = (dm.WA + dm.WB) // dc
    ti = _tile_index(dm, False)

    def body(ca_ref, cg_ref, cz_ref, w_ref, b_ref, g_ref, be_ref, y_ref, u1_ref, ubuf, u1buf, ushift):
        i = pl.program_id(1)

        @pl.when(i == 0)
        def _():
            ubuf[:, 0:CONF_HALO, :] = jnp.zeros((nlb, CONF_HALO, LANES), F32)

        @pl.when(i > 0)
        def _():
            ubuf[:, 0:CONF_HALO, :] = ubuf[:, TT:TT + CONF_HALO, :]

        _split_lanes(ubuf, slice(CONF_HALO, CONF_HALO + TT),
                     ca_ref[...].astype(F32) * _sigmoid(cg_ref[...].astype(F32)))
        _conf_conv(ubuf, ushift, w_ref, b_ref, u1buf)
        u1 = _join_lanes(u1buf)
        u1_ref[...] = u1
        mu = jnp.mean(u1, axis=-1, keepdims=True)
        xc = u1 - mu
        rstd = lax.rsqrt(jnp.mean(xc * xc, axis=-1, keepdims=True) + LN_EPS)
        u2 = xc * rstd * g_ref[...] + be_ref[...]
        cz = cz_ref[...].astype(F32)
        y_ref[...] = ((u2 * _sigmoid(u2)) * (cz * _sigmoid(cz))).astype(BF16)

    col = lambda k: pl.BlockSpec((TT, dc), lambda b, i: (ti(b, i), c0 + k))
    vec = pl.BlockSpec((1, dc), lambda b, i: (0, 0))
    return pl.pallas_call(
        body, name="mix_c_fwd", grid=(dm.BL, dm.NT),
        in_specs=[col(0), col(1), col(2), pl.BlockSpec((CONF_K, dc), lambda b, i: (0, 0)), vec, vec, vec],
        out_specs=[pl.BlockSpec((TT, dc), lambda b, i: (ti(b, i), 0))] * 2,
        out_shape=[jax.ShapeDtypeStruct((dm.R, dc), BF16), jax.ShapeDtypeStruct((dm.R, dc), F32)],
        scratch_shapes=[pltpu.VMEM((nlb, CONF_HALO + TT, LANES), F32), pltpu.VMEM((nlb, TT, LANES), F32),
                        pltpu.VMEM((nlb, SUBLANES - 1, SHIFT_ROWS, LANES), F32)],
        compiler_params=_params(("parallel", "arbitrary")),
    )(proj, proj, proj, conv_w, conv_b, ln_g, ln_b)


def _mix_c_bwd(proj, u1, dyc, conv_w, ln_g, ln_b, dm):
    dc = dm.DC
    nlb = dc // LANES
    c0 = (dm.WA + dm.WB) // dc
    ti = _tile_index(dm, True)
    hi = _halo_index(dm, CONF_HALO)

    def body(ca_ref, cg_ref, cz_ref, cah_ref, cgh_ref, u1_ref, dy_ref, w_ref, g_ref, be_ref,
             dp_ref, dw_ref, dv_ref, ubuf, dubuf, du0buf, ushift, dshift, dwacc):
        i = pl.program_id(1)
        halo_on = jnp.where(i == dm.NT - 1, 0.0, 1.0)

        @pl.when(i == 0)
        def _():
            dwacc[...] = jnp.zeros_like(dwacc)
            dv_ref[...] = jnp.zeros_like(dv_ref)
            dubuf[:, TT:TT + CONF_HALO, :] = jnp.zeros((nlb, CONF_HALO, LANES), F32)

        @pl.when(i > 0)
        def _():
            dubuf[:, TT:TT + CONF_HALO, :] = dubuf[:, 0:CONF_HALO, :]

        _split_lanes(ubuf, slice(0, CONF_HALO),
                     cah_ref[...].astype(F32) * _sigmoid(cgh_ref[...].astype(F32)) * halo_on)
        sgg = _sigmoid(cg_ref[...].astype(F32))
        ca = ca_ref[...].astype(F32)
        _split_lanes(ubuf, slice(CONF_HALO, CONF_HALO + TT), ca * sgg)
        _fill_shifted(ubuf, ushift)
        u1 = u1_ref[...]
        mu = jnp.mean(u1, axis=-1, keepdims=True)
        xc = u1 - mu
        rstd = lax.rsqrt(jnp.mean(xc * xc, axis=-1, keepdims=True) + LN_EPS)
        xhat = xc * rstd
        u2 = xhat * g_ref[...] + be_ref[...]
        su, dsu = _silu_and_grad(u2)
        sz, dsz = _silu_and_grad(cz_ref[...].astype(F32))
        dy = dy_ref[...]
        du2 = dy * dsu * sz
        dp_ref[:, 2 * dc:3 * dc] = (dy * su * dsz).astype(BF16)
        dxhat = du2 * g_ref[...]
        du1 = rstd * (dxhat - jnp.mean(dxhat, axis=-1, keepdims=True)
                      - xhat * jnp.mean(dxhat * xhat, axis=-1, keepdims=True))
        dv_ref[0, 0:1, :] = dv_ref[0, 0:1, :] + jnp.sum(du1, axis=0, keepdims=True)
        dv_ref[0, 1:2, :] = dv_ref[0, 1:2, :] + jnp.sum(du2 * xhat, axis=0, keepdims=True)
        dv_ref[0, 2:3, :] = dv_ref[0, 2:3, :] + jnp.sum(du2, axis=0, keepdims=True)
        _split_lanes(dubuf, slice(0, TT), du1)
        _fill_shifted(dubuf, dshift)

        def piece(lb, cs, r0):
            du0buf[lb, pl.ds(r0, TAP_ROWS), :] = _tree_sum(
                [w_ref[k:k + 1, cs] * _window(dubuf, dshift, CONF_K - 1 - k, r0, lb) for k in range(CONF_K)])
            d1 = dubuf[lb, pl.ds(r0, TAP_ROWS), :]
            for k in range(CONF_K):
                prod = d1 * _window(ubuf, ushift, CONF_HALO - (CONF_K - 1) + k, r0, lb)
                dwacc[lb, k] = dwacc[lb, k] + jnp.sum(prod.reshape(TAP_ROWS // SUBLANES, SUBLANES, LANES), axis=0)

        _tap_loop(nlb, piece)
        du0 = _join_lanes(du0buf)
        dp_ref[:, 0:dc] = (du0 * sgg).astype(BF16)
        dp_ref[:, dc:2 * dc] = (du0 * ca * sgg * (1.0 - sgg)).astype(BF16)

        @pl.when(i == dm.NT - 1)
        def _():
            for lb in range(nlb):
                dw_ref[0, 0:CONF_K, lb * LANES:(lb + 1) * LANES] = jnp.sum(dwacc[lb], axis=1)
            dw_ref[0, CONF_K:CONF_K + 1, :] = jnp.zeros((1, dc), F32)

    col = lambda k: pl.BlockSpec((TT, dc), lambda b, i: (ti(b, i), c0 + k))
    halo = lambda k: pl.BlockSpec((CONF_HALO, dc), lambda b, i: (hi(b, i), c0 + k))
    vec = pl.BlockSpec((1, dc), lambda b, i: (0, 0))
    return pl.pallas_call(
        body, name="mix_c_bwd", grid=(dm.BL, dm.NT),
        in_specs=[col(0), col(1), col(2), halo(0), halo(1),
                  pl.BlockSpec((TT, dc), lambda b, i: (ti(b, i), 0)),
                  pl.BlockSpec((TT, dc), lambda b, i: (ti(b, i), 0)),
                  pl.BlockSpec((CONF_K, dc), lambda b, i: (0, 0)), vec, vec],
        out_specs=[pl.BlockSpec((TT, dm.WC), lambda b, i: (ti(b, i), 0)),
                   pl.BlockSpec((1, 32, dc), lambda b, i: (b, 0, 0)),
                   pl.BlockSpec((1, 8, dc), lambda b, i: (b, 0, 0))],
        out_shape=[jax.ShapeDtypeStruct((dm.R, dm.WC), BF16),
                   jax.ShapeDtypeStruct((dm.BL, 32, dc), F32),
                   jax.ShapeDtypeStruct((dm.BL, 8, dc), F32)],
        scratch_shapes=[pltpu.VMEM((nlb, CONF_HALO + TT, LANES), F32),
                        pltpu.VMEM((nlb, TT + CONF_HALO, LANES), F32), pltpu.VMEM((nlb, TT, LANES), F32),
                        pltpu.VMEM((nlb, SUBLANES - 1, SHIFT_ROWS, LANES), F32),
                        pltpu.VMEM((nlb, SUBLANES - 1, SHIFT_ROWS, LANES), F32),
                        pltpu.VMEM((nlb, CONF_K, SUBLANES, LANES), F32)],
        compiler_params=_params(("parallel", "arbitrary")),
    )(proj, proj, proj, proj, proj, u1, dyc, conv_w, ln_g, ln_b)


RAW_HALO = 16


def _shift_matrix():
    r = jnp.arange((SSM_CONV_K - 1) * TT)[:, None]
    want = TT + r % TT - (SSM_CONV_K - 1 - r // TT)
    return (jnp.arange(2 * TT)[None, :] == want).astype(BF16)


def _ssm_conv(rawwin, sh, s_ref, w_ref, b_ref, width):
    sh[...] = _dot(s_ref[...], rawwin[...])
    for lb in range(width // LANES):
        cs = slice(lb * LANES, (lb + 1) * LANES)
        acc = b_ref[0:1, cs] + w_ref[SSM_CONV_K - 1:SSM_CONV_K, cs] * rawwin[TT:2 * TT, cs].astype(F32)
        for k in range(SSM_CONV_K - 1):
            acc = acc + w_ref[k:k + 1, cs] * sh[k * TT:(k + 1) * TT, cs]
        yield cs, acc


def _softplus(z):
    return jnp.maximum(z, 0.0) + jnp.log(1.0 + jnp.exp(-jnp.abs(z)))


def _tri(lower):
    r = lax.broadcasted_iota(jnp.int32, (TT, TT), 0)
    c = lax.broadcasted_iota(jnp.int32, (TT, TT), 1)
    return (c <= r) if lower else (c >= r)


def _exact_01_dot(mat01, x):
    x1, x2, x3 = _split3(x)
    return _dot(mat01, x1) + _dot(mat01, x2) + _dot(mat01, x3)


def _head_scalars(dt_ref, dtb_ref, alog_ref):
    z = dt_ref[...] + dtb_ref[...]
    dtv = _softplus(z)
    a = -jnp.exp(alog_ref[...])
    ac = _exact_01_dot(_tri(True).astype(F32).astype(BF16), dtv * a)
    eac = jnp.exp(ac)
    dst = jnp.exp(ac[TT - 1:TT, :] - ac)
    return z, dtv, a, ac, eac, dst


FAR_BELOW = -1e30


def _decay(ac, ac_t, h, causal):
    return jnp.exp(jnp.where(causal, ac[:, h:h + 1] - ac_t[h:h + 1, :], FAR_BELOW))


def _own_half(x16, h):
    lane = lax.broadcasted_iota(jnp.int32, (1, LANES), 1)
    keep = (lane >= SSM_HEAD_DIM) if (h % 2) else (lane < SSM_HEAD_DIM)
    return jnp.where(keep, x16, jnp.zeros_like(x16))


def _per_sequence(setup, body, bl, how):
    def all_sequences(*refs):
        views = [[r.at[b] if h is True else (r.at[pl.ds(b, 1)] if h == "keep" else r) for r, h in zip(refs, how)]
                 for b in range(bl)]
        for v in views:
            setup(*v)
        running = [body(*v) for v in views]
        while running:
            running = [g for g in running if next(g, "done") != "done"]

    return all_sequences


def _mix_b_fwd(proj, projdt, conv_w, conv_b, dt_bias, a_log, dskx, norm_g, expand, dm, rider=None):
    db, gn, xbc_w, hpg = dm.DB, dm.GN, dm.XBC, dm.HPG
    gw = db // SSM_GROUPS

    def setup(bz_ref, bx_ref, bc_ref, dt_ref, w_ref, b_ref, dtb_ref, alog_ref, dsk_ref, g_ref, e_ref, s_ref,
              y_ref, yraw_ref, sprev_ref, rawwin, sh, xbuf, state, ybuf, exbuf, xdtbuf):
        i = pl.program_id(0)

        @pl.when(i == 0)
        def _():
            rawwin[0:TT, :] = jnp.zeros((TT, xbc_w), BF16)
            state[...] = jnp.zeros_like(state)

        @pl.when(i > 0)
        def _():
            rawwin[TT - RAW_HALO:TT, :] = rawwin[2 * TT - RAW_HALO:2 * TT, :]

    def body(bz_ref, bx_ref, bc_ref, dt_ref, w_ref, b_ref, dtb_ref, alog_ref, dsk_ref, g_ref, e_ref, s_ref,
             y_ref, yraw_ref, sprev_ref, rawwin, sh, xbuf, state, ybuf, exbuf, xdtbuf):
        rawwin[TT:2 * TT, 0:db] = bx_ref[...]
        rawwin[TT:2 * TT, db:xbc_w] = bc_ref[...]
        for cs, pre in _ssm_conv(rawwin, sh, s_ref, w_ref, b_ref, xbc_w):
            xbuf[:, cs] = pre * _sigmoid(pre)
            yield

        _, dtv, _, ac, eac, dst = _head_scalars(dt_ref, dtb_ref, alog_ref)
        exbuf[...] = _dot(jnp.concatenate([dtv, eac, dst], axis=0).astype(BF16), e_ref[...])
        ac_t = ac.T
        causal = _tri(True)
        sprev_ref[0, 0] = state[...]
        yield

        xdtbuf[...] = xbuf[:, 0:db] * exbuf[0:TT, :]
        ybuf[...] = xbuf[:, 0:db] * dsk_ref[...]
        for g in range(SSM_GROUPS):
            gs = slice(g * gw, (g + 1) * gw)
            bg = xbuf[:, db + g * SSM_STATE:db + (g + 1) * SSM_STATE].astype(BF16)
            cg = xbuf[:, db + gn + g * SSM_STATE:db + gn + (g + 1) * SSM_STATE].astype(BF16)
            cb = _dot_nt(cg, bg)
            for e in range(0, hpg, 2):
                h = g * hpg + e
                ps = slice(h * SSM_HEAD_DIM, (h + 2) * SSM_HEAD_DIM)
                xp16 = xdtbuf[:, ps].astype(BF16)
                acc = jnp.zeros((TT, LANES), F32)
                for hh in (h, h + 1):
                    mm = (cb * _decay(ac, ac_t, hh, causal)).astype(BF16)
                    acc = acc + _dot(mm, _own_half(xp16, hh))
                ybuf[:, ps] = ybuf[:, ps] + acc
                yield
            sg = state[:, gs]
            ybuf[:, gs] = ybuf[:, gs] + exbuf[TT:2 * TT, gs] * _dot(cg, sg.astype(BF16))
            state[:, gs] = sg * exbuf[2 * TT - 1:2 * TT, gs] + _dot_tn(
                bg, (xdtbuf[:, gs] * exbuf[2 * TT:3 * TT, gs]).astype(BF16))
            yield

        yraw = ybuf[...]
        yraw_ref[...] = yraw
        bz = bz_ref[...].astype(F32)
        v = yraw * (bz * _sigmoid(bz))
        r = lax.rsqrt(jnp.mean(v * v, axis=-1, keepdims=True) + NORM_EPS)
        y_ref[...] = (v * r * g_ref[...]).astype(BF16)

    bl = dm.BL
    tile = lambda w, k: pl.BlockSpec((bl, TT, w), lambda i: (0, i, k))
    fixed = lambda r, w: pl.BlockSpec((r, w), lambda i: (0, 0))
    proj3, dt3 = proj.reshape(bl, dm.Lp, dm.NP), projdt.reshape(bl, dm.Lp, DT_PAD)
    scratch = [((2 * TT, xbc_w), BF16), (((SSM_CONV_K - 1) * TT, xbc_w), F32), ((TT, xbc_w), F32),
               ((SSM_STATE, db), F32), ((TT, db), F32), ((3 * TT, db), F32), ((TT, db), F32)]
    (y, yraw, sprev), rode = _call(
        _per_sequence(setup, body, bl, [True] * 4 + [False] * 8 + [True, True, "keep"] + [True] * len(scratch)),
        "mix_b_fwd", (dm.NT,),
        [tile(db, dm.WA // db), tile(db, dm.WA // db + 1), tile(2 * gn, (dm.WA + 2 * db) // (2 * gn)),
         tile(DT_PAD, 0),
         fixed(SSM_CONV_K, xbc_w), fixed(1, xbc_w), fixed(1, DT_PAD), fixed(1, DT_PAD),
         fixed(1, db), fixed(1, db), fixed(DT_PAD, db), fixed((SSM_CONV_K - 1) * TT, 2 * TT)],
        [tile(db, 0), tile(db, 0), pl.BlockSpec((bl, 1, SSM_STATE, db), lambda i: (0, i, 0, 0))],
        [jax.ShapeDtypeStruct((bl, dm.Lp, db), BF16), jax.ShapeDtypeStruct((bl, dm.Lp, db), F32),
         jax.ShapeDtypeStruct((bl, dm.NT, SSM_STATE, db), F32)],
        [pltpu.VMEM((bl,) + s, t) for s, t in scratch],
        ("arbitrary",),
        (proj3, proj3, proj3, dt3, conv_w, conv_b, dt_bias, a_log, dskx, norm_g, expand, _shift_matrix()), rider)
    return (y.reshape(dm.R, db), yraw.reshape(dm.R, db), sprev), rode


def _mix_b_bwd(proj, projdt, dyb, yraw, sprev, conv_w, conv_b, dt_bias, a_log, dskx, norm_g, expand, expand_t, dm,
               rider=None):
    db, gn, xbc_w, hpg = dm.DB, dm.GN, dm.XBC, dm.HPG
    gw = db // SSM_GROUPS

    def setup(bz_ref, bx_ref, bc_ref, dt_ref, bxh_ref, bch_ref, dy_ref, yraw_ref, sprev_ref,
              w_ref, b_ref, dtb_ref, alog_ref, dsk_ref, g_ref, e_ref, et_ref, s_ref,
              dp_ref, dpt_ref, dwc_ref, dch_ref, dhd_ref,
              rawwin, sh, xbuf, dsbuf, dstate, dxbuf, z1buf, dprebuf, exbuf, xdtbuf, dyrbuf, uvec):
        i = pl.program_id(0)

        @pl.when(i == 0)
        def _():
            dwc_ref[...] = jnp.zeros_like(dwc_ref)
            dch_ref[...] = jnp.zeros_like(dch_ref)
            dhd_ref[...] = jnp.zeros_like(dhd_ref)
            dstate[...] = jnp.zeros_like(dstate)
            dprebuf[TT:TT + SMALL_HALO, :] = jnp.zeros((SMALL_HALO, xbc_w), F32)
            rawwin[0:TT - RAW_HALO, :] = jnp.zeros((TT - RAW_HALO, xbc_w), BF16)

        @pl.when(i > 0)
        def _():
            dprebuf[TT:TT + SMALL_HALO, :] = dprebuf[0:SMALL_HALO, :]

    def body(bz_ref, bx_ref, bc_ref, dt_ref, bxh_ref, bch_ref, dy_ref, yraw_ref, sprev_ref,
             w_ref, b_ref, dtb_ref, alog_ref, dsk_ref, g_ref, e_ref, et_ref, s_ref,
             dp_ref, dpt_ref, dwc_ref, dch_ref, dhd_ref,
             rawwin, sh, xbuf, dsbuf, dstate, dxbuf, z1buf, dprebuf, exbuf, xdtbuf, dyrbuf, uvec):
        halo_on = jnp.where(pl.program_id(0) == dm.NT - 1, 0.0, 1.0).astype(BF16)

        rawwin[TT - RAW_HALO:TT, 0:db] = bxh_ref[...] * halo_on
        rawwin[TT - RAW_HALO:TT, db:xbc_w] = bch_ref[...] * halo_on
        rawwin[TT:2 * TT, 0:db] = bx_ref[...]
        rawwin[TT:2 * TT, db:xbc_w] = bc_ref[...]
        for cs, pre in _ssm_conv(rawwin, sh, s_ref, w_ref, b_ref, xbc_w):
            sl, dsl = _silu_and_grad(pre)
            xbuf[:, cs] = sl
            dsbuf[:, cs] = dsl
            yield

        z, dtv, a, ac, eac, dst = _head_scalars(dt_ref, dtb_ref, alog_ref)
        exbuf[...] = _dot(jnp.concatenate([dtv, eac, dst], axis=0).astype(BF16), e_ref[...])
        ac_t = ac.T
        causal = _tri(True)
        xdtbuf[...] = xbuf[:, 0:db] * exbuf[0:TT, :]

        yraw = yraw_ref[...]
        sz, dsz = _silu_and_grad(bz_ref[...].astype(F32))
        v = yraw * sz
        r = lax.rsqrt(jnp.mean(v * v, axis=-1, keepdims=True) + NORM_EPS)
        dy = dy_ref[...]
        dyg = dy * g_ref[...]
        dv = r * dyg - v * (r * r * r * jnp.mean(dyg * v, axis=-1, keepdims=True))
        dch_ref[0, 0:1, :] = dch_ref[0, 0:1, :] + jnp.sum(dy * v * r, axis=0, keepdims=True)
        dyr = dv * sz
        dyrbuf[...] = dyr
        dp_ref[:, 0:db] = (dv * yraw * dsz).astype(BF16)
        dch_ref[0, 1:2, :] = dch_ref[0, 1:2, :] + jnp.sum(dyr * xbuf[:, 0:db], axis=0, keepdims=True)

        lane_row = lax.broadcasted_iota(jnp.int32, (1, LANES), 1)
        sub_col = lax.broadcasted_iota(jnp.int32, (LANES, 1), 0)
        dac = jnp.zeros((TT, LANES), F32)
        colacc = jnp.zeros((LANES, TT), F32)
        for g in range(SSM_GROUPS):
            gs = slice(g * gw, (g + 1) * gw)
            bs_ = slice(db + g * SSM_STATE, db + (g + 1) * SSM_STATE)
            cs_ = slice(db + gn + g * SSM_STATE, db + gn + (g + 1) * SSM_STATE)
            bg = xbuf[:, bs_].astype(BF16)
            cg = xbuf[:, cs_].astype(BF16)
            cb = _dot_nt(cg, bg)
            dcb = jnp.zeros((TT, TT), F32)
            for e in range(0, hpg, 2):
                h = g * hpg + e
                ps = slice(h * SSM_HEAD_DIM, (h + 2) * SSM_HEAD_DIM)
                xp16 = xdtbuf[:, ps].astype(BF16)
                dyp16 = dyrbuf[:, ps].astype(BF16)
                acc = jnp.zeros((TT, LANES), F32)
                for hh in (h, h + 1):
                    dec = _decay(ac, ac_t, hh, causal)
                    mm = cb * dec
                    dyh = _own_half(dyp16, hh)
                    dmm = _dot_nt(dyh, xp16)
                    acc = acc + _dot_tn(mm.astype(BF16), dyh)
                    dcb = dcb + dmm * dec
                    gm = dmm * mm
                    dac = jnp.where(lane_row == hh, jnp.sum(gm, axis=1, keepdims=True), dac)
                    colacc = jnp.where(sub_col == hh, jnp.sum(gm, axis=0, keepdims=True), colacc)
                dxbuf[:, ps] = acc
                yield
            sg32 = sprev_ref[0, 0, :, gs]
            sg = sg32.astype(BF16)
            dsn = dstate[:, gs]
            dsn16 = dsn.astype(BF16)
            dcb16 = dcb.astype(BF16)
            eacx = exbuf[TT:2 * TT, gs]
            dstx = exbuf[2 * TT:3 * TT, gs]
            cdx = exbuf[2 * TT - 1:2 * TT, gs]
            dye16 = (dyrbuf[:, gs] * eacx).astype(BF16)
            xdt_g = xdtbuf[:, gs]
            dxbuf[:, cs_] = _dot(dcb16, bg) + _dot_nt(dye16, sg)
            dst_x = dstx * _dot(bg, dsn16)
            dxbuf[:, bs_] = _dot_tn(dcb16, cg) + _dot_nt((dstx * xdt_g).astype(BF16), dsn16)
            dstate[:, gs] = cdx * dsn + _dot_tn(cg, dye16)
            z1buf[:, gs] = dyrbuf[:, gs] * (eacx * _dot(cg, sg)) - xdt_g * dst_x
            uvec[:, gs] = jnp.broadcast_to(
                jnp.sum(xdt_g * dst_x, axis=0, keepdims=True) + jnp.sum(dsn * cdx * sg32, axis=0, keepdims=True),
                (8, gw))
            dxbuf[:, gs] = dxbuf[:, gs] + dst_x
            yield

        zz = _dot(jnp.concatenate([z1buf[...], dxbuf[:, 0:db] * xbuf[:, 0:db]], axis=0).astype(BF16), et_ref[...])
        u1, u2, u3 = _split3(uvec[...])
        ulast = (_dot(u1, et_ref[...]) + _dot(u2, et_ref[...]) + _dot(u3, et_ref[...]))[0:1, :]
        is_last = (lax.broadcasted_iota(jnp.int32, (TT, 1), 0) == TT - 1).astype(F32)
        dac = dac - colacc.T + zz[0:TT] + is_last * ulast
        dda = _exact_01_dot(_tri(False).astype(F32).astype(BF16), dac)
        ddt = dda * a + zz[TT:2 * TT]
        dhd_ref[0, 1:2, :] = dhd_ref[0, 1:2, :] + jnp.sum(dda * dtv, axis=0, keepdims=True) * a
        ddtraw = ddt * _sigmoid(z)
        dhd_ref[0, 0:1, :] = dhd_ref[0, 0:1, :] + jnp.sum(ddtraw, axis=0, keepdims=True)
        dpt_ref[...] = ddtraw.astype(BF16)
        dxbuf[:, 0:db] = dyrbuf[...] * dsk_ref[...] + dxbuf[:, 0:db] * exbuf[0:TT, :]

        for lb in range(xbc_w // LANES):
            cs = slice(lb * LANES, (lb + 1) * LANES)
            dpre = dxbuf[:, cs] * dsbuf[:, cs]
            dprebuf[0:TT, cs] = dpre
            dwc_ref[0, SSM_CONV_K:SSM_CONV_K + 1, cs] = dwc_ref[0, SSM_CONV_K:SSM_CONV_K + 1, cs] + jnp.sum(
                dpre, axis=0, keepdims=True)
            draw = w_ref[SSM_CONV_K - 1:SSM_CONV_K, cs] * dpre
            for k in range(SSM_CONV_K - 1):
                ahead = SSM_CONV_K - 1 - k
                draw = draw + w_ref[k:k + 1, cs] * dprebuf[ahead:ahead + TT, cs]
            for k in range(SSM_CONV_K):
                moved = sh[k * TT:(k + 1) * TT, cs] if k < SSM_CONV_K - 1 else rawwin[TT:2 * TT, cs].astype(F32)
                dwc_ref[0, k:k + 1, cs] = dwc_ref[0, k:k + 1, cs] + jnp.sum(dpre * moved, axis=0, keepdims=True)
            dp_ref[:, db + lb * LANES:db + (lb + 1) * LANES] = draw.astype(BF16)
            yield

    bl, nt = dm.BL, dm.NT
    tile = lambda w, k: pl.BlockSpec((bl, TT, w), lambda i: (0, nt - 1 - i, k))
    halo = lambda w, k: pl.BlockSpec((bl, HALO_BLOCK, w),
                                     lambda i: (0, jnp.maximum((nt - 1 - i) * (TT // HALO_BLOCK) - 1, 0), k))
    fixed = lambda r, w: pl.BlockSpec((r, w), lambda i: (0, 0))
    sums = lambda w: pl.BlockSpec((bl, 8, w), lambda i: (0, 0, 0))
    kz = dm.WA // db
    kc = (dm.WA + 2 * db) // (2 * gn)
    proj3, dt3 = proj.reshape(bl, dm.Lp, dm.NP), projdt.reshape(bl, dm.Lp, DT_PAD)
    scratch = [((2 * TT, xbc_w), BF16), (((SSM_CONV_K - 1) * TT, xbc_w), F32), ((TT, xbc_w), F32),
               ((TT, xbc_w), F32), ((SSM_STATE, db), F32), ((TT, xbc_w), F32), ((TT, db), F32),
               ((TT + SMALL_HALO, xbc_w), F32), ((3 * TT, db), F32), ((TT, db), F32), ((TT, db), F32), ((8, db), F32)]
    how = [True] * 8 + ["keep"] + [False] * 9 + [True, True, "keep", "keep", "keep"] + [True] * len(scratch)
    (dp, dpt, dwc, dch, dhd), rode = _call(
        _per_sequence(setup, body, bl, how), "mix_b_bwd", (nt,),
        [tile(db, kz), tile(db, kz + 1), tile(2 * gn, kc), tile(DT_PAD, 0),
         halo(db, kz + 1), halo(2 * gn, kc), tile(db, 0), tile(db, 0),
         pl.BlockSpec((bl, 1, SSM_STATE, db), lambda i: (0, nt - 1 - i, 0, 0)),
         fixed(SSM_CONV_K, xbc_w), fixed(1, xbc_w), fixed(1, DT_PAD), fixed(1, DT_PAD),
         fixed(1, db), fixed(1, db), fixed(DT_PAD, db), fixed(db, DT_PAD), fixed((SSM_CONV_K - 1) * TT, 2 * TT)],
        [tile(dm.WB, 0), tile(DT_PAD, 0), sums(xbc_w), sums(db), sums(DT_PAD)],
        [jax.ShapeDtypeStruct((bl, dm.Lp, dm.WB), BF16), jax.ShapeDtypeStruct((bl, dm.Lp, DT_PAD), BF16),
         jax.ShapeDtypeStruct((bl, 8, xbc_w), F32), jax.ShapeDtypeStruct((bl, 8, db), F32),
         jax.ShapeDtypeStruct((bl, 8, DT_PAD), F32)],
        [pltpu.VMEM((bl,) + s, t) for s, t in scratch],
        ("arbitrary",),
        (proj3, proj3, proj3, dt3, proj3, proj3, dyb.reshape(bl, dm.Lp, db), yraw.reshape(bl, dm.Lp, db), sprev,
         conv_w, conv_b, dt_bias, a_log, dskx, norm_g, expand, expand_t, _shift_matrix()), rider)
    return (dp.reshape(dm.R, dm.WB), dpt.reshape(dm.R, DT_PAD), dwc, dch, dhd), rode


def _head_consts(dm):
    head_of = jnp.arange(dm.DB) // SSM_HEAD_DIM
    expand = (jnp.arange(DT_PAD)[:, None] == head_of[None, :]).astype(BF16)
    return expand, expand.T


def _ssm_params(lw, dm):
    pad_h = lambda v: jnp.pad(v, (0, DT_PAD - dm.H))[None]
    return (lw["ssm_conv_w"], lw["ssm_conv_b"][None], pad_h(lw["dt_bias"]), pad_h(lw["a_log"]),
            jnp.repeat(lw["d_skip"], SSM_HEAD_DIM)[None], lw["ssm_norm_g"][None])


def _layer_fwd(h, lw, w_in, w_out, cst, dm, next_bases=None, own_out=None):
    nxt = next_bases is not None
    n_next = len(next_bases) if nxt else 0

    def beside(rider, extra):
        return extra if rider is None else (rider if extra is None else _ride_both(rider, extra))

    (proj, projdt, hn), got = _fwd_in(
        h, lw["pre_g"][None], w_in, dm,
        beside(_ride_gather_ici(next_bases, 0, 2) if nxt else None,
               None if own_out is None else _ride_gather_ici([own_out])))
    ya = _mix_a_fwd(proj, lw["conv_a_w"], dm)
    (yb, yraw, sprev), got = _mix_b_fwd(
        proj, projdt, *_ssm_params(lw, dm), cst[0], dm,
        beside(_ride_gather_ici(got[:n_next], 1, 2) if nxt else None,
               None if own_out is None else _ride_gather_d2d(got[n_next:])))
    if own_out is not None:
        w_out = got[n_next].reshape(2 * dm.D, dm.D)
        got = got[:n_next]
    yc, u1 = _mix_c_fwd(proj, lw["conf_conv_w"], lw["conf_conv_b"][None], lw["conf_ln_g"][None],
                        lw["conf_ln_b"][None], dm)
    (h_new, m), got = _fwd_out(ya, yb, yc, w_out, h, lw["post_g"][None], dm, _ride_gather_d2d(got) if nxt else None)
    return h_new, (h, hn, proj, projdt, ya, yb, yc, u1, yraw, sprev, m), got, w_out


def _layer_bwd(dh, saved, lw, w_in, w_out, cst, dm, reduce=None, last=False):
    h_in, hn, proj, projdt, ya, yb, yc, u1, yraw, sprev, m = saved
    (dya, dyb, dyc, dwo, dpost), got = _bwd_out(dh, m, lw["post_g"][None], w_out, ya, yb, yc, dm,
                                                None if reduce is None else reduce.swap())
    dpa, dwa = _mix_a_bwd(proj, dya, lw["conv_a_w"], dm)
    (dpb, dpt, dwcv, dch, dhd), got = _mix_b_bwd(proj, projdt, dyb, yraw, sprev, *_ssm_params(lw, dm), cst[0],
                                                 cst[1], dm, None if reduce is None else reduce.to_owners(got))
    dpc, dwcf, dvc = _mix_c_bwd(proj, u1, dyc, lw["conf_conv_w"], lw["conf_ln_g"][None], lw["conf_ln_b"][None], dm)
    def own_reduce():
        pieces = _bwd_in_dw(hn, [dpa, dpb, dpc, dpt], dm)
        return _GradReduce([_grad_to_shards(pieces, dm), dwo.reshape(N_CHIPS, 2 * dm.D // N_CHIPS, dm.D)])

    rider = None if reduce is None else reduce.join(got)
    n_join = 0 if rider is None else len(rider.out_shapes)
    if last:
        mine = own_reduce()
        to_owners = mine.to_owners(_exchange("grad_swap_halves", mine.swap()))
        rider = to_owners if rider is None else _ride_both(rider, to_owners)
    (dh, dpre), got = _bwd_in_dx(dpa, dpb, dpc, dpt, w_in, h_in, dh, lw["pre_g"][None], dm, rider)
    if reduce is not None:
        reduce.finish(got[:n_join])
    if last:
        mine.finish(_exchange("grad_join_halves", mine.join(got[n_join:])))
    else:
        mine = own_reduce()
    dwcv, dch, dhd, dvc = (jnp.sum(a, axis=0) for a in (dwcv, dch, dhd, dvc))
    small = dict(pre_g=dpre[0], post_g=dpost[0], conv_a_w=jnp.sum(dwa, axis=0)[:CONV_A_K],
                 ssm_conv_w=dwcv[:SSM_CONV_K], ssm_conv_b=dwcv[SSM_CONV_K], ssm_norm_g=dch[0],
                 d_skip=jnp.sum(dch[1].reshape(dm.H, SSM_HEAD_DIM), axis=1), dt_bias=dhd[0, :dm.H],
                 a_log=dhd[1, :dm.H], conf_conv_w=jnp.sum(dwcf, axis=0)[:CONF_K], conf_conv_b=dvc[0],
                 conf_ln_g=dvc[1], conf_ln_b=dvc[2])
    return dh, mine, small


def _shard_runs(dm):
    ab = dm.WA + dm.WB
    order = [(0, 0, ab), (ab, dm.DT0, dm.H), (ab + dm.H, ab, dm.WC)]
    k = dm.NIN // N_CHIPS
    runs = []
    for s in range(N_CHIPS):
        for o0, m0, wd in order:
            lo, hi = max(o0, s * k), min(o0 + wd, (s + 1) * k)
            if lo < hi:
                runs.append((s, lo - s * k, m0 + lo - o0, hi - lo))
    return runs


def _w_in_from_shards(base, dm):
    tr = _row_tile(dm.D, 256)
    k = dm.NIN // N_CHIPS
    runs = _shard_runs(dm)

    def body(in_ref, out_ref):
        for s, sc, mc, wd in runs:
            out_ref[:, mc:mc + wd] = in_ref[s, :, sc:sc + wd]
        out_ref[:, dm.DT0 + dm.H:dm.NP] = jnp.zeros((tr, dm.NP - dm.DT0 - dm.H), BF16)

    return pl.pallas_call(
        body, name="w_in_from_shards", grid=(dm.D // tr,),
        in_specs=[pl.BlockSpec((N_CHIPS, tr, k), lambda r: (0, r, 0))],
        out_specs=pl.BlockSpec((tr, dm.NP), lambda r: (r, 0)),
        out_shape=jax.ShapeDtypeStruct((dm.D, dm.NP), BF16),
        compiler_params=_params(("parallel",)),
    )(base)


def _grad_to_shards(pieces, dm):
    tr = _row_tile(dm.D, 256)
    k = dm.NIN // N_CHIPS
    starts = [0, dm.WA, dm.WA + dm.WB, dm.DT0]
    widths = [dm.WA, dm.WB, dm.WC, DT_PAD]
    runs = _shard_runs(dm)

    def body(a_ref, b_ref, c_ref, t_ref, out_ref):
        refs = (a_ref, b_ref, c_ref, t_ref)
        for s, sc, mc, wd in runs:
            for p in range(4):
                lo, hi = max(mc, starts[p]), min(mc + wd, starts[p] + widths[p])
                if lo < hi:
                    out_ref[s, :, sc + lo - mc:sc + hi - mc] = refs[p][:, lo - starts[p]:hi - starts[p]].astype(BF16)

    return pl.pallas_call(
        body, name="grad_to_shards", grid=(dm.D // tr,),
        in_specs=[pl.BlockSpec((tr, w), lambda r: (r, 0)) for w in widths],
        out_specs=pl.BlockSpec((N_CHIPS, tr, k), lambda r: (0, r, 0)),
        out_shape=jax.ShapeDtypeStruct((N_CHIPS, dm.D, k), BF16),
        compiler_params=_params(("parallel",)),
    )(*pieces)


def _place_own(w, layer, me):
    _, rows, cols = w.shape
    tr = _row_tile(rows, 256)

    def body(me_ref, w_ref, out_ref):
        out_ref[0] = w_ref[0].astype(BF16)

    return pl.pallas_call(
        body, name="place_own",
        grid_spec=pltpu.PrefetchScalarGridSpec(
            num_scalar_prefetch=1, grid=(rows // tr,),
            in_specs=[pl.BlockSpec((1, tr, cols), lambda r, me_ref: (layer, r, 0))],
            out_specs=pl.BlockSpec((1, tr, cols), lambda r, me_ref: (me_ref[0], r, 0))),
        out_shape=jax.ShapeDtypeStruct((N_CHIPS, rows, cols), BF16),
        compiler_params=_params(("parallel",)),
    )(me, w)


def _add_halves(g, got, c, name):
    _, _, rows, cols = g.shape
    tr = _row_tile(rows, 256)

    def body(c_ref, g_ref, got_ref, out_ref):
        out_ref[0] = (g_ref[0, 0].astype(F32) + got_ref[0].astype(F32)).astype(BF16)

    return pl.pallas_call(
        body, name=name,
        grid_spec=pltpu.PrefetchScalarGridSpec(
            num_scalar_prefetch=1, grid=(N_CHIPS, rows // tr),
            in_specs=[pl.BlockSpec((1, 1, tr, cols), lambda s, r, c_ref: (s, c_ref[0], r, 0)),
                      pl.BlockSpec((1, tr, cols), lambda s, r, c_ref: (s, r, 0))],
            out_specs=pl.BlockSpec((1, tr, cols), lambda s, r, c_ref: (s, r, 0))),
        out_shape=jax.ShapeDtypeStruct((N_CHIPS, rows, cols), BF16),
        compiler_params=_params(("parallel", "parallel")),
    )(c, g, got)


def _add_owner(p, got, where, name):
    _, rows, cols = p.shape
    tr = _row_tile(rows, 256)

    def body(w_ref, p_ref, got_ref, out_ref):
        acc = p_ref[0].astype(F32)
        for j in range(3):
            acc = acc + got_ref[j].astype(F32)
        out_ref[0] = acc

    return pl.pallas_call(
        body, name=name,
        grid_spec=pltpu.PrefetchScalarGridSpec(
            num_scalar_prefetch=1, grid=(rows // tr,),
            in_specs=[pl.BlockSpec((1, tr, cols), lambda r, w_ref: (w_ref[0], r, 0)),
                      pl.BlockSpec((3, tr, cols), lambda r, w_ref: (0, r, 0))],
            out_specs=pl.BlockSpec((1, tr, cols), lambda r, w_ref: (w_ref[1], r, 0))),
        out_shape=jax.ShapeDtypeStruct((2, rows, cols), F32),
        compiler_params=_params(("parallel",)),
    )(where, p, got)


class _GradReduce:
    def __init__(self, gs):
        self.gs = [g.reshape((N_CHIPS, 2, g.shape[1] // 2) + g.shape[2:]) for g in gs]
        self.c = lax.axis_index("c").astype(jnp.int32).reshape(1)
        chip = (2 * lax.axis_index("x") + lax.axis_index("y")).astype(jnp.int32)
        self.where = jnp.stack([chip, self.c[0]])
        self.result = None

    def swap(self):
        return _ride_swap_halves(self.gs)

    def to_owners(self, got):
        self.ps = [_add_halves(g, r, self.c, "grad_add_sibling_" + n) for g, r, n in zip(self.gs, got, ("in", "out"))]
        return _ride_to_owners(self.ps)

    def join(self, got):
        qs = [_add_owner(p, r, self.where, "grad_add_chips_" + n) for p, r, n in zip(self.ps, got, ("in", "out"))]
        return _ride_join_halves(qs)

    def finish(self, got):
        self.result = [a.reshape((a.shape[0] * a.shape[1],) + a.shape[2:]) for a in got]


def _adamw_math(w, g, m, v):
    m = ADAM_B1 * m + (1.0 - ADAM_B1) * g
    v = ADAM_B2 * v + (1.0 - ADAM_B2) * (g * g)
    m_hat = m / (1.0 - ADAM_B1 ** ADAM_STEP)
    v_hat = v / (1.0 - ADAM_B2 ** ADAM_STEP)
    delta = -ADAM_LR * (m_hat / (jnp.sqrt(v_hat) + ADAM_EPS) + ADAM_WD * w)
    return delta, m, v


def _adamw_small(w, g, m, v, name):
    def body(w_ref, g_ref, m_ref, v_ref, d_out, m_out, v_out):
        d_out[...], m_out[...], v_out[...] = _adamw_math(w_ref[...], g_ref[...], m_ref[...], v_ref[...])

    shape = jax.ShapeDtypeStruct(w.shape, F32)
    return pl.pallas_call(body, name="adamw_" + name, out_shape=[shape, shape, shape],
                          compiler_params=_params())(w, g, m, v)


def _adamw_layer(i, w, g, m, v, prev, name):
    depth, rows, cols = w.shape
    tr = _row_tile(rows, 256)
    n_prev = 0 if prev is None else 4

    def body(*refs):
        w_ref, g_ref, m_ref, v_ref = refs[:4]
        g_out, d_out, m_out, v_out = refs[4 + n_prev:]
        gv = g_ref[...]
        g_out[0] = gv
        d_out[0], m_out[0], v_out[0] = _adamw_math(w_ref[0], gv, m_ref[0], v_ref[0])

    lay = pl.BlockSpec((1, tr, cols), lambda r: (i, r, 0))
    shape = jax.ShapeDtypeStruct(w.shape, F32)
    return pl.pallas_call(
        body, name="adamw_" + name, grid=(rows // tr,),
        in_specs=[lay, pl.BlockSpec((tr, cols), lambda r: (r, 0)), lay, lay] + [ANY] * n_prev,
        out_specs=[lay] * 4, out_shape=[shape] * 4,
        input_output_aliases={4 + k: k for k in range(n_prev)},
        compiler_params=_params(("parallel",)),
    )(w, g, m, v, *(prev or ()))


def _adamw_cols_major(w, gs, m, v, name):
    depth, rows, cols = w.shape
    tr = max(t for t in range(1, 129) if cols % t == 0)
    wt, mt, vt = (jnp.transpose(a, (2, 0, 1)) for a in (w, m, v))
    gt = jnp.stack([g.T for g in gs], axis=1)

    def body(w_ref, g_ref, m_ref, v_ref, g_out, d_out, m_out, v_out):
        gv = g_ref[...]
        g_out[...] = gv
        d_out[...], m_out[...], v_out[...] = _adamw_math(w_ref[...], gv, m_ref[...], v_ref[...])

    spec = pl.BlockSpec((tr, depth, rows), lambda r: (r, 0, 0))
    shape = jax.ShapeDtypeStruct((cols, depth, rows), F32)
    outs = pl.pallas_call(body, name="adamw_" + name, grid=(cols // tr,), in_specs=[spec] * 4, out_specs=[spec] * 4,
                          out_shape=[shape] * 4, compiler_params=_params(("parallel",)))(wt, gt, mt, vt)
    return [jnp.transpose(a, (1, 2, 0)) for a in outs]


def _sum_leading(buf, name):
    n, rows, cols = buf.shape
    tr = _row_tile(rows, rows)

    def body(in_ref, out_ref):
        acc = in_ref[0]
        for k in range(1, n):
            acc = acc + in_ref[k]
        out_ref[...] = acc

    return pl.pallas_call(
        body, name=name, grid=(rows // tr,),
        in_specs=[pl.BlockSpec((n, tr, cols), lambda i: (0, i, 0))],
        out_specs=pl.BlockSpec((tr, cols), lambda i: (i, 0)),
        out_shape=jax.ShapeDtypeStruct((rows, cols), F32),
        compiler_params=_params(("parallel",)),
    )(buf)


_SHARDED_SMALL = ("meta", "conv_a_w", "ssm_conv_w", "conf_conv_w")
_LAYER_SMALL = ("pre_g", "post_g", "conv_a_w", "ssm_conv_w", "ssm_conv_b", "dt_bias", "a_log", "d_skip",
                "ssm_norm_g", "conf_conv_w", "conf_conv_b", "conf_ln_g", "conf_ln_b")
_WEIGHTS = ("meta", "pre_g", "post_g", "w_in", "w_out", "conv_a_w", "ssm_conv_w", "ssm_conv_b", "dt_bias", "a_log",
            "d_skip", "ssm_norm_g", "conf_conv_w", "conf_conv_b", "conf_ln_g", "conf_ln_b")


def _shard_last(a):
    return jnp.moveaxis(a.reshape(a.shape[:-1] + (N_CHIPS, a.shape[-1] // N_CHIPS)), -2, 0)


def _with_own_block(a, n, at):
    return lax.dynamic_update_index_in_dim(jnp.zeros((n,) + a.shape, a.dtype), a, at, 0)


def _with_own_columns(a, chip):
    k = a.shape[-1]
    return lax.dynamic_update_slice_in_dim(jnp.zeros(a.shape[:-1] + (N_CHIPS * k,), a.dtype), a, chip * k, a.ndim - 1)


def kernel(x, meta, pre_g, post_g, w_in, w_out, conv_a_w, ssm_conv_w, ssm_conv_b, dt_bias, a_log, d_skip, ssm_norm_g, conf_conv_w, conf_conv_b, conf_ln_g, conf_ln_b, loss_target, m_meta, m_pre_g, m_post_g, m_w_in, m_w_out, m_conv_a_w, m_ssm_conv_w, m_ssm_conv_b, m_dt_bias, m_a_log, m_d_skip, m_ssm_norm_g, m_conf_conv_w, m_conf_conv_b, m_conf_ln_g, m_conf_ln_b, v_meta, v_pre_g, v_post_g, v_w_in, v_w_out, v_conv_a_w, v_ssm_conv_w, v_ssm_conv_b, v_dt_bias, v_a_log, v_d_skip, v_ssm_norm_g, v_conf_conv_w, v_conf_conv_b, v_conf_ln_g, v_conf_ln_b):
    w = dict(meta=meta, pre_g=pre_g, post_g=post_g, w_in=w_in, w_out=w_out, conv_a_w=conv_a_w,
             ssm_conv_w=ssm_conv_w, ssm_conv_b=ssm_conv_b, dt_bias=dt_bias, a_log=a_log, d_skip=d_skip,
             ssm_norm_g=ssm_norm_g, conf_conv_w=conf_conv_w, conf_conv_b=conf_conv_b, conf_ln_g=conf_ln_g,
             conf_ln_b=conf_ln_b)
    mom = dict(meta=m_meta, pre_g=m_pre_g, post_g=m_post_g, w_in=m_w_in, w_out=m_w_out, conv_a_w=m_conv_a_w,
               ssm_conv_w=m_ssm_conv_w, ssm_conv_b=m_ssm_conv_b, dt_bias=m_dt_bias, a_log=m_a_log, d_skip=m_d_skip,
               ssm_norm_g=m_ssm_norm_g, conf_conv_w=m_conf_conv_w, conf_conv_b=m_conf_conv_b,
               conf_ln_g=m_conf_ln_g, conf_ln_b=m_conf_ln_b)
    vel = dict(meta=v_meta, pre_g=v_pre_g, post_g=v_post_g, w_in=v_w_in, w_out=v_w_out, conv_a_w=v_conv_a_w,
               ssm_conv_w=v_ssm_conv_w, ssm_conv_b=v_ssm_conv_b, dt_bias=v_dt_bias, a_log=v_a_log, d_skip=v_d_skip,
               ssm_norm_g=v_ssm_norm_g, conf_conv_w=v_conf_conv_w, conf_conv_b=v_conf_conv_b,
               conf_ln_g=v_conf_ln_g, conf_ln_b=v_conf_ln_b)
    bl, seq, d = x.shape
    dm = Dims(bl, seq, d)
    depth = w_in.shape[0]
    chip = (2 * lax.axis_index("x") + lax.axis_index("y")).astype(jnp.int32)
    dev = 2 * chip + lax.axis_index("c").astype(jnp.int32)
    cst = _head_consts(dm)

    bases = [[_place_own(w_in, i, chip.reshape(1)), _place_own(w_out, i, chip.reshape(1))] for i in range(depth)]
    first_in, small_w = _gather_ici_relayed(
        [bases[0][0]], _ride_gather_small([_with_own_columns(w[n], chip) for n in _SHARDED_SMALL]))
    full = dict(w)
    full.update(zip(_SHARDED_SMALL, small_w))
    h, gathered = _embed(x, full["meta"], dm, _ride_gather_d2d(first_in))
    saved, proj_w = [], []
    for i in range(depth):
        lw = {n: full[n][i] for n in _LAYER_SMALL}
        w_in_i = _w_in_from_shards(gathered[0], dm)
        h, keep, gathered, w_out_i = _layer_fwd(
            h, lw, w_in_i, None if i == 0 else gathered[1].reshape(2 * d, d), cst, dm,
            bases[i + 1] if i + 1 < depth else None, bases[0][1] if i == 0 else None)
        proj_w.append((w_in_i, w_out_i))
        saved.append(keep)

    dh, loss = _loss_head(h, loss_target, dm)
    loss = lax.psum(loss, ("x", "y", "c"))

    small_g = {n: [None] * depth for n in _LAYER_SMALL}
    big = {"w_in": None, "w_out": None}
    g_in = [None] * depth
    reduce = None
    for i in reversed(range(depth)):
        lw = {n: full[n][i] for n in _LAYER_SMALL}
        dh, mine, sg = _layer_bwd(dh, saved[i], lw, proj_w[i][0], proj_w[i][1], cst, dm, reduce, last=i == 0)
        for n in _LAYER_SMALL:
            small_g[n][i] = sg[n]
        if reduce is not None:
            g_in[i + 1] = reduce.result[0]
            big["w_out"] = _adamw_layer(i + 1, w_out, reduce.result[1], m_w_out, v_w_out, big["w_out"], "w_out")
        reduce = mine
    g_in[0] = reduce.result[0]
    big["w_out"] = _adamw_layer(0, w_out, reduce.result[1], m_w_out, v_w_out, big["w_out"], "w_out")
    grad_x, gmeta = _unembed(dh, dm)

    g = {n: jnp.stack(v) for n, v in small_g.items()}
    g["meta"] = gmeta
    small = [n for n in _WEIGHTS if n not in ("w_in", "w_out")]
    flat = jnp.concatenate([g[n].reshape(-1) for n in small])
    rows = -(-flat.shape[0] // (16 * LANES)) * 16
    flat = jnp.pad(flat, (0, rows * LANES - flat.shape[0])).reshape(rows, LANES)
    parts = _gather_all(_with_own_block(flat, N_DEV, dev))
    total = _sum_leading(parts, "small_grads_sum").reshape(-1)
    big["w_in"] = _adamw_cols_major(w_in, g_in, m_w_in, v_w_in, "w_in")
    grads, deltas, new_m, new_v = {}, {}, {}, {}
    off = 0
    for n in small:
        size = g[n].size
        fullg = total[off:off + size].reshape(g[n].shape)
        off += size
        if n in _SHARDED_SMALL:
            fullg = lax.dynamic_index_in_dim(_shard_last(fullg), chip, axis=0, keepdims=False)
        grads[n] = fullg
        deltas[n], new_m[n], new_v[n] = _adamw_small(w[n], fullg, mom[n], vel[n], n)
    for n in ("w_in", "w_out"):
        grads[n], deltas[n], new_m[n], new_v[n] = big[n]

    return (loss, grad_x, *[grads[n] for n in _WEIGHTS], *[deltas[n] for n in _WEIGHTS],
            *[new_m[n] for n in _WEIGHTS], *[new_v[n] for n in _WEIGHTS])
```

```python
import jax
import jax.numpy as jnp
from jax import lax
from jax.experimental import pallas as pl
from jax.experimental.pallas import tpu as pltpu

F32 = jnp.float32
BF16 = jnp.bfloat16

N_META = 16
TT = 128
SSM_STATE = 128
SSM_GROUPS = 2
SSM_HEAD_DIM = 64
CONV_A_K = 3
SSM_CONV_K = 4
CONF_K = 31
NORM_EPS = 1e-6
LN_EPS = 1e-5
LANES = 128
MXU_DIM = 256
DT_PAD = LANES
CONF_HALO = 32
SMALL_HALO = 8
VMEM_LIMIT = 56 * 1024 * 1024
N_CHIPS = 4
N_DEV = 8

ADAM_LR = 0.001
ADAM_B1 = 0.9
ADAM_B2 = 0.999
ADAM_EPS = 1e-08
ADAM_WD = 0.01
ADAM_STEP = 10

MESH = pl.DeviceIdType.MESH
ANY = pl.BlockSpec(memory_space=pl.ANY)


class Dims:
    def __init__(self, bl, seq, d):
        self.BL, self.S, self.D = bl, seq, d
        self.L = seq + N_META
        self.Lp = -(-self.L // TT) * TT
        self.NT = self.Lp // TT
        self.R = bl * self.Lp
        self.DA = d // 2
        self.DB = d
        self.DC = d // 2
        self.H = self.DB // SSM_HEAD_DIM
        self.HPG = self.H // SSM_GROUPS
        self.GN = SSM_GROUPS * SSM_STATE
        self.WA = 4 * self.DA
        self.WB = 2 * self.DB + 2 * self.GN
        self.WC = 3 * self.DC
        self.DT0 = self.WA + self.WB + self.WC
        self.NP = -(-(self.DT0 + DT_PAD) // (5 * MXU_DIM)) * (5 * MXU_DIM)
        self.NIN = self.WA + self.WB + self.H + self.WC
        self.XBC = self.DB + 2 * self.GN
        assert self.H % 2 == 0 and self.HPG % 2 == 0 and self.H <= DT_PAD
        assert self.DA % LANES == 0 and (self.WA + self.WB) % self.DC == 0 and self.WA % self.DB == 0


def _row_tile(n, target):
    best = None
    for t in range(16, min(n, target) + 1, 16):
        if n % t == 0:
            best = t
    assert best is not None
    return best


def _col_tile(n, target):
    best = None
    for t in range(LANES, min(n, target) + 1, LANES):
        if n % t == 0:
            best = t
    assert best is not None
    return best


def _params(sem=None):
    return pltpu.CompilerParams(dimension_semantics=sem, vmem_limit_bytes=VMEM_LIMIT)


def _sigmoid(x):
    return 1.0 / (1.0 + jnp.exp(-x))


def _silu_and_grad(x):
    s = _sigmoid(x)
    y = x * s
    return y, s + y * (1.0 - s)


def _dot(a, b):
    return jnp.dot(a, b, preferred_element_type=F32)


def _dot_nt(a, b):
    return lax.dot_general(a, b, (((1,), (1,)), ((), ())), preferred_element_type=F32)


def _dot_tn(a, b):
    return lax.dot_general(a, b, (((0,), (0,)), ((), ())), preferred_element_type=F32)


def _split3(x):
    x1 = x.astype(BF16)
    r1 = x - x1.astype(F32)
    x2 = r1.astype(BF16)
    x3 = (r1 - x2.astype(F32)).astype(BF16)
    return x1, x2, x3


class Rider:
    def __init__(self, plan, ins, out_shapes, aliases, nsem):
        self.plan, self.ins, self.out_shapes, self.aliases, self.nsem = plan, list(ins), list(out_shapes), aliases, nsem


def _place():
    x, y, c = lax.axis_index("x"), lax.axis_index("y"), lax.axis_index("c")
    chips = [(1 - x, y), (x, 1 - y), (1 - x, 1 - y)]
    return x, y, c, chips


def _remote(k, src, dst, to, send_sems, recv_sems):
    return pltpu.make_async_remote_copy(src_ref=src, dst_ref=dst, send_sem=send_sems.at[k], recv_sem=recv_sems.at[k],
                                        device_id=to, device_id_type=MESH)


def _call(body, name, grid, in_specs, out_specs, out_shape, scratch_shapes, sem, args, rider=None):
    if rider is None:
        outs = pl.pallas_call(body, name=name, grid=grid, in_specs=in_specs, out_specs=out_specs, out_shape=out_shape,
                              scratch_shapes=scratch_shapes, compiler_params=_params(sem))(*args)
        return list(outs), []
    n_in, n_out, n_scr = len(args), len(out_shape), len(scratch_shapes)
    r_in, r_out = len(rider.ins), len(rider.out_shapes)

    def hosted(*refs):
        ins, rins = refs[:n_in], refs[n_in:n_in + r_in]
        o0 = n_in + r_in
        outs, routs = refs[o0:o0 + n_out], refs[o0 + n_out:o0 + n_out + r_out]
        scr = refs[o0 + n_out + r_out:o0 + n_out + r_out + n_scr]
        send_sems, recv_sems = refs[o0 + n_out + r_out + n_scr:]
        first = pl.program_id(0) == 0
        last = pl.program_id(0) == grid[0] - 1
        for ax in range(1, len(grid)):
            first = jnp.logical_and(first, pl.program_id(ax) == 0)
            last = jnp.logical_and(last, pl.program_id(ax) == grid[ax] - 1)

        @pl.when(first)
        def _():
            starts, _ = rider.plan(rins, routs, send_sems, recv_sems)
            for cp in starts:
                cp.start()

        body(*ins, *outs, *scr)

        @pl.when(last)
        def _():
            _, waits = rider.plan(rins, routs, send_sems, recv_sems)
            for wait in waits:
                wait()

    res = pl.pallas_call(
        hosted, name=name, grid=grid,
        in_specs=list(in_specs) + [ANY] * r_in, out_specs=list(out_specs) + [ANY] * r_out,
        out_shape=list(out_shape) + rider.out_shapes,
        input_output_aliases={n_in + k: n_out + v for k, v in rider.aliases.items()},
        scratch_shapes=list(scratch_shapes) + [pltpu.SemaphoreType.DMA((rider.nsem,)),
                                               pltpu.SemaphoreType.DMA((rider.nsem,))],
        compiler_params=_params(("arbitrary",) * len(grid)),
    )(*args, *rider.ins)
    return list(res[:n_out]), list(res[n_out:])


def _exchange(name, rider):
    r_in, r_out = len(rider.ins), len(rider.out_shapes)

    def body(*refs):
        rins, routs = refs[:r_in], refs[r_in:r_in + r_out]
        send_sems, recv_sems = refs[r_in + r_out:]
        starts, waits = rider.plan(rins, routs, send_sems, recv_sems)
        for cp in starts:
            cp.start()
        for wait in waits:
            wait()

    res = pl.pallas_call(
        body, name=name, in_specs=[ANY] * r_in, out_specs=[ANY] * r_out, out_shape=rider.out_shapes,
        input_output_aliases=dict(rider.aliases),
        scratch_shapes=[pltpu.SemaphoreType.DMA((rider.nsem,)), pltpu.SemaphoreType.DMA((rider.nsem,))],
    )(*rider.ins)
    return list(res)


def _same(arrays):
    return [jax.ShapeDtypeStruct(a.shape, a.dtype) for a in arrays]


class _SemsFrom:
    def __init__(self, sems, first):
        self.sems, self.first = sems, first

    @property
    def at(self):
        return self

    def __getitem__(self, k):
        return self.sems.at[self.first + k]


def _ride_both(r1, r2):
    n_in, n_out = len(r1.ins), len(r1.out_shapes)

    def plan(ins, outs, ss, rs):
        s1, w1 = r1.plan(ins[:n_in], outs[:n_out], ss, rs)
        s2, w2 = r2.plan(ins[n_in:], outs[n_out:], _SemsFrom(ss, r1.nsem), _SemsFrom(rs, r1.nsem))
        return s1 + s2, w1 + w2

    aliases = dict(r1.aliases)
    aliases.update({n_in + k: n_out + v for k, v in r2.aliases.items()})
    return Rider(plan, r1.ins + r2.ins, r1.out_shapes + r2.out_shapes, aliases, r1.nsem + r2.nsem)


def _ride_gather_ici(bases, part=0, nparts=1):
    n = len(bases)

    def plan(ins, outs, ss, rs):
        x, y, c, chips = _place()
        me = 2 * x + y
        starts, waits = [], []
        for a in range(n):
            half = outs[a].shape[1] // 2
            mine = pl.ds(c * half + part * (half // nparts), half // nparts)
            for j, chip in enumerate(chips):
                cp = _remote(3 * a + j, outs[a].at[me, mine], outs[a].at[me, mine], (*chip, c), ss, rs)
                got = outs[a].at[2 * chip[0] + chip[1], mine]
                starts.append(cp)
                waits += [cp.wait_send, _remote(3 * a + j, got, got, (*chip, c), ss, rs).wait_recv]
        return starts, waits

    return Rider(plan, bases, _same(bases), {a: a for a in range(n)}, 3 * n)


def _gather_ici_relayed(bases, also):
    n, m = len(bases), len(also.ins)

    def body(*refs):
        outs = refs[n + m:2 * n + m]
        ss, rs = refs[2 * (n + m):]
        beside, beside_waits = also.plan(refs[n:n + m], refs[2 * n + m:2 * (n + m)],
                                         _SemsFrom(ss, 4 * n), _SemsFrom(rs, 4 * n))
        for cp in beside:
            cp.start()
        x, y, c, _ = _place()
        me, xn, yn, dg = 2 * x + y, 2 * (1 - x) + y, 2 * x + (1 - y), 2 * (1 - x) + (1 - y)
        to_x, to_y = (1 - x, y, c), (x, 1 - y, c)
        sends = []

        def send(k, piece, to):
            cp = _remote(k, piece, piece, to, ss, rs)
            cp.start()
            sends.append(cp)

        def arrived(k, piece, frm):
            _remote(k, piece, piece, frm, ss, rs).wait_recv()

        rows = []
        for a in range(n):
            half = outs[a].shape[1] // 2
            rows.append((pl.ds(c * half, half), pl.ds(c * half, half // 2), pl.ds(c * half + half // 2, half // 2)))
            send(4 * a, outs[a].at[me, rows[a][0]], to_x)
            send(4 * a + 1, outs[a].at[me, rows[a][0]], to_y)
        for a in range(n):
            mine, lo, hi = rows[a]
            arrived(4 * a, outs[a].at[xn, mine], to_x)
            send(4 * a + 2, outs[a].at[xn, lo], to_y)
            arrived(4 * a + 1, outs[a].at[yn, mine], to_y)
            send(4 * a + 3, outs[a].at[yn, hi], to_x)
        for a in range(n):
            mine, lo, hi = rows[a]
            arrived(4 * a + 2, outs[a].at[dg, lo], to_y)
            arrived(4 * a + 3, outs[a].at[dg, hi], to_x)
        for cp in sends:
            cp.wait_send()
        for wait in beside_waits:
            wait()

    aliases = {a: a for a in range(n)}
    aliases.update({n + k: n + v for k, v in also.aliases.items()})
    nsem = 4 * n + also.nsem
    res = pl.pallas_call(
        body, name="gather_ici_first", in_specs=[ANY] * (n + m), out_specs=[ANY] * (n + len(also.out_shapes)),
        out_shape=_same(bases) + also.out_shapes, input_output_aliases=aliases,
        scratch_shapes=[pltpu.SemaphoreType.DMA((nsem,)), pltpu.SemaphoreType.DMA((nsem,))],
    )(*bases, *also.ins)
    return list(res[:n]), list(res[n:])


def _ride_gather_d2d(bases):
    n = len(bases)

    def plan(ins, outs, ss, rs):
        x, y, c, chips = _place()
        sib = (x, y, 1 - c)
        starts, waits = [], []
        for a in range(n):
            half = outs[a].shape[1] // 2
            for j, chip in enumerate(chips):
                frm = 2 * chip[0] + chip[1]
                got = outs[a].at[frm, pl.ds(c * half, half)]
                theirs = outs[a].at[frm, pl.ds((1 - c) * half, half)]
                cp = _remote(3 * a + j, got, got, sib, ss, rs)
                starts.append(cp)
                waits += [cp.wait_send, _remote(3 * a + j, theirs, theirs, sib, ss, rs).wait_recv]
        return starts, waits

    return Rider(plan, bases, _same(bases), {a: a for a in range(n)}, 3 * n)


def _ride_gather_small(bases):
    n = len(bases)

    def plan(ins, outs, ss, rs):
        x, y, c, chips = _place()
        me = 2 * x + y
        starts, waits = [], []
        for a in range(n):
            k = outs[a].shape[-1] // N_CHIPS
            lead = (slice(None),) * (len(outs[a].shape) - 1)
            at = (lambda s: pl.multiple_of(s * k, LANES)) if k % LANES == 0 else (lambda s: s * k)
            cols = lambda s: outs[a].at[lead + (pl.ds(at(s), k),)]
            for j, chip in enumerate(chips):
                cp = _remote(3 * a + j, cols(me), cols(me), (*chip, c), ss, rs)
                got = cols(2 * chip[0] + chip[1])
                starts.append(cp)
                waits += [cp.wait_send, _remote(3 * a + j, got, got, (*chip, c), ss, rs).wait_recv]
        return starts, waits

    return Rider(plan, bases, _same(bases), {a: a for a in range(n)}, 3 * n)


def _ride_swap_halves(gs):
    n = len(gs)

    def plan(ins, outs, ss, rs):
        x, y, c, _ = _place()
        cps = [_remote(a, ins[a].at[:, 1 - c], outs[a], (x, y, 1 - c), ss, rs) for a in range(n)]
        return cps, [cp.wait for cp in cps]

    shapes = [jax.ShapeDtypeStruct((g.shape[0],) + g.shape[2:], g.dtype) for g in gs]
    return Rider(plan, gs, shapes, {}, n)


def _ride_to_owners(ps):
    n = len(ps)

    def plan(ins, outs, ss, rs):
        x, y, c, chips = _place()
        cps = []
        for a in range(n):
            for j, chip in enumerate(chips):
                cps.append(_remote(3 * a + j, ins[a].at[2 * chip[0] + chip[1]], outs[a].at[j], (*chip, c), ss, rs))
        return cps, [cp.wait for cp in cps]

    shapes = [jax.ShapeDtypeStruct((3,) + p.shape[1:], p.dtype) for p in ps]
    return Rider(plan, ps, shapes, {}, 3 * n)


def _ride_join_halves(qs):
    n = len(qs)

    def plan(ins, outs, ss, rs):
        x, y, c, _ = _place()
        sib = (x, y, 1 - c)
        starts, waits = [], []
        for a in range(n):
            cp = _remote(a, outs[a].at[c], outs[a].at[c], sib, ss, rs)
            starts.append(cp)
            waits += [cp.wait_send, _remote(a, outs[a].at[1 - c], outs[a].at[1 - c], sib, ss, rs).wait_recv]
        return starts, waits

    return Rider(plan, qs, _same(qs), {a: a for a in range(n)}, n)


def _gather_all(base):
    def body(in_ref, out_ref, ss, rs):
        x, y, c, chips = _place()
        sib = (x, y, 1 - c)
        block = lambda cx, cy, cc: out_ref.at[4 * cx + 2 * cy + cc]
        mine = block(x, y, c)
        first = [_remote(j, mine, mine, (*chip, c), ss, rs) for j, chip in enumerate(chips)]
        first.append(_remote(3, mine, mine, sib, ss, rs))
        for cp in first:
            cp.start()
        passed = []
        for j, chip in enumerate(chips):
            got = block(*chip, c)
            _remote(j, got, got, (*chip, c), ss, rs).wait_recv()
            passed.append(_remote(4 + j, got, got, sib, ss, rs))
            passed[-1].start()
        theirs = block(x, y, 1 - c)
        _remote(3, theirs, theirs, sib, ss, rs).wait_recv()
        for j, chip in enumerate(chips):
            got = block(*chip, 1 - c)
            _remote(4 + j, got, got, sib, ss, rs).wait_recv()
        for cp in first + passed:
            cp.wait_send()

    return pl.pallas_call(
        body, name="small_grads_gather_all", in_specs=[ANY], out_specs=ANY,
        out_shape=jax.ShapeDtypeStruct(base.shape, base.dtype), input_output_aliases={0: 0},
        scratch_shapes=[pltpu.SemaphoreType.DMA((N_DEV - 1,)), pltpu.SemaphoreType.DMA((N_DEV - 1,))],
    )(base)


def _embed(x, meta, dm, rider=None):
    dc = _col_tile(dm.D, 256)
    s, lp = dm.S, dm.Lp

    def body(x_ref, meta_ref, h_ref):
        h_ref[0:N_META, :] = meta_ref[...]
        h_ref[N_META:N_META + s, :] = x_ref[0]
        if lp > N_META + s:
            h_ref[N_META + s:lp, :] = jnp.zeros((lp - N_META - s, dc), F32)

    (h,), rode = _call(
        body, "embed", (dm.BL, dm.D // dc),
        [pl.BlockSpec((1, s, dc), lambda b, j: (b, 0, j)), pl.BlockSpec((N_META, dc), lambda b, j: (0, j))],
        [pl.BlockSpec((lp, dc), lambda b, j: (b, j))], [jax.ShapeDtypeStruct((dm.R, dm.D), F32)],
        [], ("parallel", "parallel"), (x, meta), rider)
    return h, rode


def _loss_head(h, target, dm):
    dc = _col_tile(dm.D, 256)
    s, lp, nj = dm.S, dm.Lp, dm.D // dc

    def body(h_ref, t_ref, dh_ref, l_ref):
        diff = h_ref[N_META:N_META + s, :] - t_ref[0]
        dh_ref[0:N_META, :] = jnp.zeros((N_META, dc), F32)
        dh_ref[N_META:N_META + s, :] = diff * (1.0 / dm.D)
        if lp > N_META + s:
            dh_ref[N_META + s:lp, :] = jnp.zeros((lp - N_META - s, dc), F32)
        l_ref[...] = jnp.full((8, LANES), (0.5 / dm.D) * jnp.sum(diff * diff), F32)

    dh, part = pl.pallas_call(
        body, name="loss_head", grid=(dm.BL, nj),
        in_specs=[pl.BlockSpec((lp, dc), lambda b, j: (b, j)),
                  pl.BlockSpec((1, s, dc), lambda b, j: (b, 0, j))],
        out_specs=[pl.BlockSpec((lp, dc), lambda b, j: (b, j)),
                   pl.BlockSpec((8, LANES), lambda b, j: (b * nj + j, 0))],
        out_shape=[jax.ShapeDtypeStruct((dm.R, dm.D), F32),
                   jax.ShapeDtypeStruct((dm.BL * nj * 8, LANES), F32)],
        compiler_params=_params(("parallel", "parallel")),
    )(h, target)
    return dh, jnp.sum(part[::8, 0])


def _unembed(dh, dm):
    dc = _col_tile(dm.D, 256)
    s, lp = dm.S, dm.Lp

    def body(dh_ref, gx_ref, gm_ref):
        gx_ref[0] = dh_ref[N_META:N_META + s, :]

        @pl.when(pl.program_id(1) == 0)
        def _():
            gm_ref[...] = dh_ref[0:N_META, :]

        @pl.when(pl.program_id(1) > 0)
        def _():
            gm_ref[...] = gm_ref[...] + dh_ref[0:N_META, :]

    return pl.pallas_call(
        body, name="unembed", grid=(dm.D // dc, dm.BL),
        in_specs=[pl.BlockSpec((lp, dc), lambda j, b: (b, j))],
        out_specs=[pl.BlockSpec((1, s, dc), lambda j, b: (b, 0, j)),
                   pl.BlockSpec((N_META, dc), lambda j, b: (0, j))],
        out_shape=[jax.ShapeDtypeStruct((dm.BL, s, dm.D), F32),
                   jax.ShapeDtypeStruct((N_META, dm.D), F32)],
        compiler_params=_params(("parallel", "arbitrary")),
    )(dh)


def _fwd_in(h, pre_g, w, dm, rider=None):
    tm = _row_tile(dm.R, 1088)
    tn = _col_tile(dm.NP, 5 * MXU_DIM)
    nj = dm.NP // tn

    def body(h_ref, g_ref, w_ref, wdt_ref, proj_ref, dt_ref, hn_ref):
        @pl.when(pl.program_id(1) == 0)
        def _():
            xf = h_ref[...]
            r = lax.rsqrt(jnp.mean(xf * xf, axis=-1, keepdims=True) + NORM_EPS)
            hn_ref[...] = (xf * r * g_ref[...]).astype(BF16)
            dt_ref[...] = _dot(hn_ref[...], wdt_ref[...])

        proj_ref[...] = _dot(hn_ref[...], w_ref[...]).astype(BF16)

    return _call(
        body, "fwd_in", (dm.R // tm, nj),
        [pl.BlockSpec((tm, dm.D), lambda i, j: (i, 0)),
         pl.BlockSpec((1, dm.D), lambda i, j: (0, 0)),
         pl.BlockSpec((dm.D, tn), lambda i, j: (0, j)),
         pl.BlockSpec((dm.D, DT_PAD), lambda i, j: (0, dm.DT0 // DT_PAD))],
        [pl.BlockSpec((tm, tn), lambda i, j: (i, j)),
         pl.BlockSpec((tm, DT_PAD), lambda i, j: (i, 0)),
         pl.BlockSpec((tm, dm.D), lambda i, j: (i, 0))],
        [jax.ShapeDtypeStruct((dm.R, dm.NP), BF16), jax.ShapeDtypeStruct((dm.R, DT_PAD), F32),
         jax.ShapeDtypeStruct((dm.R, dm.D), BF16)],
        [], ("parallel", "arbitrary"), (h, pre_g, w, w), rider)


def _fwd_out(ya, yb, yc, w_out, h, post_g, dm, rider=None):
    tm = _row_tile(dm.Lp, 544)
    tiles_per_seq = dm.Lp // tm
    da, db, dc = dm.DA, dm.DB, dm.DC

    def body(ya_ref, yb_ref, yc_ref, w_ref, h_ref, g_ref, hn_ref, m_ref):
        m = _dot(ya_ref[...], w_ref[0:da, :])
        m = m + _dot(yb_ref[...], w_ref[da:da + db, :])
        m = m + _dot(yc_ref[...], w_ref[da + db:da + db + dc, :])
        m_ref[...] = m
        r = lax.rsqrt(jnp.mean(m * m, axis=-1, keepdims=True) + NORM_EPS)
        t = (pl.program_id(0) % tiles_per_seq) * tm + lax.broadcasted_iota(jnp.int32, (tm, 1), 0)
        keep = (t < dm.L).astype(F32)
        hn_ref[...] = (h_ref[...] + m * r * g_ref[...]) * keep

    row = lambda i: (i, 0)
    fixed = lambda i: (0, 0)
    return _call(
        body, "fwd_out", (dm.R // tm,),
        [pl.BlockSpec((tm, da), row), pl.BlockSpec((tm, db), row), pl.BlockSpec((tm, dc), row),
         pl.BlockSpec((2 * dm.D, dm.D), fixed), pl.BlockSpec((tm, dm.D), row), pl.BlockSpec((1, dm.D), fixed)],
        [pl.BlockSpec((tm, dm.D), row), pl.BlockSpec((tm, dm.D), row)],
        [jax.ShapeDtypeStruct((dm.R, dm.D), F32), jax.ShapeDtypeStruct((dm.R, dm.D), F32)],
        [], ("parallel",), (ya, yb, yc, w_out, h, post_g), rider)


def _bwd_out(dh, m, post_g, w_out, ya, yb, yc, dm, rider=None):
    tm = _row_tile(dm.R, MXU_DIM)
    da, db, dc = dm.DA, dm.DB, dm.DC

    def body(dh_ref, m_ref, g_ref, w_ref, ya_ref, yb_ref, yc_ref, dya_ref, dyb_ref, dyc_ref, dw_ref, dg_ref):
        @pl.when(pl.program_id(0) == 0)
        def _():
            dw_ref[...] = jnp.zeros_like(dw_ref)
            dg_ref[...] = jnp.zeros_like(dg_ref)

        m = m_ref[...]
        dh_ = dh_ref[...]
        r = lax.rsqrt(jnp.mean(m * m, axis=-1, keepdims=True) + NORM_EPS)
        n = m * r
        dg_ref[0:1, :] = dg_ref[0:1, :] + jnp.sum(dh_ * n, axis=0, keepdims=True)
        dn = dh_ * g_ref[...]
        dm_ = (r * (dn - n * jnp.mean(dn * n, axis=-1, keepdims=True))).astype(BF16)
        dya_ref[...] = _dot_nt(dm_, w_ref[0:da, :])
        dyb_ref[...] = _dot_nt(dm_, w_ref[da:da + db, :])
        dyc_ref[...] = _dot_nt(dm_, w_ref[da + db:da + db + dc, :])
        dw_ref[0:da, :] = dw_ref[0:da, :] + _dot_tn(ya_ref[...], dm_)
        dw_ref[da:da + db, :] = dw_ref[da:da + db, :] + _dot_tn(yb_ref[...], dm_)
        dw_ref[da + db:da + db + dc, :] = dw_ref[da + db:da + db + dc, :] + _dot_tn(yc_ref[...], dm_)

    row = lambda i: (i, 0)
    fixed = lambda i: (0, 0)
    return _call(
        body, "bwd_out", (dm.R // tm,),
        [pl.BlockSpec((tm, dm.D), row), pl.BlockSpec((tm, dm.D), row), pl.BlockSpec((1, dm.D), fixed),
         pl.BlockSpec((2 * dm.D, dm.D), fixed),
         pl.BlockSpec((tm, da), row), pl.BlockSpec((tm, db), row), pl.BlockSpec((tm, dc), row)],
        [pl.BlockSpec((tm, da), row), pl.BlockSpec((tm, db), row), pl.BlockSpec((tm, dc), row),
         pl.BlockSpec((2 * dm.D, dm.D), fixed), pl.BlockSpec((8, dm.D), fixed)],
        [jax.ShapeDtypeStruct((dm.R, da), F32), jax.ShapeDtypeStruct((dm.R, db), F32),
         jax.ShapeDtypeStruct((dm.R, dc), F32),
         jax.ShapeDtypeStruct((2 * dm.D, dm.D), F32), jax.ShapeDtypeStruct((8, dm.D), F32)],
        [], ("arbitrary",), (dh, m, post_g, w_out, ya, yb, yc), rider)


def _bwd_in_dx(dpa, dpb, dpc, dpt, w, h, dh, pre_g, dm, rider=None):
    tm = _row_tile(dm.R, 272)
    wa, wb, wc = dm.WA, dm.WB, dm.WC

    def body(dpa_ref, dpb_ref, dpc_ref, dpt_ref, w_ref, h_ref, dh_ref, g_ref, out_ref, dg_ref):
        @pl.when(pl.program_id(0) == 0)
        def _():
            dg_ref[...] = jnp.zeros_like(dg_ref)

        dhn = _dot_nt(dpa_ref[...], w_ref[:, 0:wa])
        dhn = dhn + _dot_nt(dpb_ref[...], w_ref[:, wa:wa + wb])
        dhn = dhn + _dot_nt(dpc_ref[...], w_ref[:, wa + wb:wa + wb + wc])
        dhn = dhn + _dot_nt(dpt_ref[...], w_ref[:, wa + wb + wc:wa + wb + wc + DT_PAD])
        xf = h_ref[...]
        r = lax.rsqrt(jnp.mean(xf * xf, axis=-1, keepdims=True) + NORM_EPS)
        n = xf * r
        dg_ref[0:1, :] = dg_ref[0:1, :] + jnp.sum(dhn * n, axis=0, keepdims=True)
        dn = dhn * g_ref[...]
        out_ref[...] = dh_ref[...] + r * (dn - n * jnp.mean(dn * n, axis=-1, keepdims=True))

    row = lambda i: (i, 0)
    fixed = lambda i: (0, 0)
    return _call(
        body, "bwd_in_dx", (dm.R // tm,),
        [pl.BlockSpec((tm, wa), row), pl.BlockSpec((tm, wb), row), pl.BlockSpec((tm, wc), row),
         pl.BlockSpec((tm, DT_PAD), row), pl.BlockSpec((dm.D, dm.NP), fixed),
         pl.BlockSpec((tm, dm.D), row), pl.BlockSpec((tm, dm.D), row), pl.BlockSpec((1, dm.D), fixed)],
        [pl.BlockSpec((tm, dm.D), row), pl.BlockSpec((8, dm.D), fixed)],
        [jax.ShapeDtypeStruct((dm.R, dm.D), F32), jax.ShapeDtypeStruct((8, dm.D), F32)],
        [], ("arbitrary",), (dpa, dpb, dpc, dpt, w, h, dh, pre_g), rider)


def _bwd_in_dw(hn, dps, dm):
    widest = max(dp.shape[1] for dp in dps)
    tn = [_col_tile(dp.shape[1], 2 * MXU_DIM if dp.shape[1] == widest else MXU_DIM) for dp in dps]
    nb = [dp.shape[1] // t for dp, t in zip(dps, tn)]
    first = [sum(nb[:p]) for p in range(len(dps))]
    at = lambda p: (lambda j: (0, jnp.clip(j - first[p], 0, nb[p] - 1)))

    def body(hn_ref, *refs):
        j = pl.program_id(0)
        for p in range(len(dps)):
            @pl.when(jnp.logical_and(j >= first[p], j < first[p] + nb[p]))
            def _(p=p):
                refs[len(dps) + p][...] = _dot_tn(hn_ref[...], refs[p][...])

    return pl.pallas_call(
        body, name="bwd_in_dw", grid=(sum(nb),),
        in_specs=[pl.BlockSpec((dm.R, dm.D), lambda j: (0, 0))] + [
            pl.BlockSpec((dm.R, tn[p]), at(p)) for p in range(len(dps))],
        out_specs=[pl.BlockSpec((dm.D, tn[p]), at(p)) for p in range(len(dps))],
        out_shape=[jax.ShapeDtypeStruct((dm.D, dp.shape[1]), F32) for dp in dps],
        compiler_params=_params(("arbitrary",)),
    )(hn, *dps)


def _tile_index(dm, reverse):
    if reverse:
        return lambda b, i: b * dm.NT + (dm.NT - 1 - i)
    return lambda b, i: b * dm.NT + i


def _halo_index(dm, rows):
    per_tile = TT // rows
    return lambda b, i: jnp.maximum((b * dm.NT + (dm.NT - 1 - i)) * per_tile - 1, 0)


HALO_BLOCK = 16


def _last_rows(x):
    return x.astype(F32)[HALO_BLOCK - SMALL_HALO:HALO_BLOCK]


MIX_A_ROWS = 288


def _mix_a_fwd(proj, conv_w, dm):
    da = dm.DA
    ta = _row_tile(dm.Lp, MIX_A_ROWS)
    nta = dm.Lp // ta
    bl = dm.BL

    def setup(ab_ref, ac_ref, ax_ref, az_ref, w_ref, y_ref, pbuf):
        i = pl.program_id(0)

        @pl.when(i == 0)
        def _():
            pbuf[0:SMALL_HALO, :] = jnp.zeros((SMALL_HALO, da), F32)

        @pl.when(i > 0)
        def _():
            pbuf[0:SMALL_HALO, :] = pbuf[ta:ta + SMALL_HALO, :]

    def body(ab_ref, ac_ref, ax_ref, az_ref, w_ref, y_ref, pbuf):
        for lb in range(da // LANES):
            cs = slice(lb * LANES, (lb + 1) * LANES)
            p = ac_ref[:, cs].astype(F32) * ax_ref[:, cs].astype(F32)
            pbuf[SMALL_HALO:SMALL_HALO + ta, cs] = p
            q = (w_ref[0:1, cs] * pbuf[6:6 + ta, cs] + w_ref[1:2, cs] * pbuf[7:7 + ta, cs] + w_ref[2:3, cs] * p)
            az = az_ref[:, cs].astype(F32)
            y_ref[:, cs] = (ab_ref[:, cs].astype(F32) * q * (az * _sigmoid(az))).astype(BF16)
            yield

    proj3 = proj.reshape(bl, dm.Lp, dm.NP)
    col = lambda k: pl.BlockSpec((bl, ta, da), lambda i: (0, i, k))
    return pl.pallas_call(
        _per_sequence(setup, body, bl, [True] * 4 + [False] + [True, True]), name="mix_a_fwd", grid=(nta,),
        in_specs=[col(0), col(1), col(2), col(3), pl.BlockSpec((CONV_A_K, da), lambda i: (0, 0))],
        out_specs=col(0),
        out_shape=jax.ShapeDtypeStruct((bl, dm.Lp, da), BF16),
        scratch_shapes=[pltpu.VMEM((bl, SMALL_HALO + ta, da), F32)],
        compiler_params=_params(("arbitrary",)),
    )(proj3, proj3, proj3, proj3, conv_w).reshape(dm.R, da)


def _mix_a_bwd(proj, dya, conv_w, dm):
    da = dm.DA
    ta = _row_tile(dm.Lp, MIX_A_ROWS)
    nta = dm.Lp // ta
    bl = dm.BL

    def setup(ab_ref, ac_ref, ax_ref, az_ref, ach_ref, axh_ref, dy_ref, w_ref, dp_ref, dw_ref, pbuf, dqbuf):
        i = pl.program_id(0)

        @pl.when(i == 0)
        def _():
            dw_ref[...] = jnp.zeros_like(dw_ref)
            dqbuf[ta:ta + SMALL_HALO, :] = jnp.zeros((SMALL_HALO, da), F32)

        @pl.when(i > 0)
        def _():
            dqbuf[ta:ta + SMALL_HALO, :] = dqbuf[0:SMALL_HALO, :]

    def body(ab_ref, ac_ref, ax_ref, az_ref, ach_ref, axh_ref, dy_ref, w_ref, dp_ref, dw_ref, pbuf, dqbuf):
        halo_on = jnp.where(pl.program_id(0) == nta - 1, 0.0, 1.0)
        for lb in range(da // LANES):
            cs = slice(lb * LANES, (lb + 1) * LANES)
            pbuf[0:SMALL_HALO, cs] = (_last_rows(ach_ref[:, cs]) * _last_rows(axh_ref[:, cs])) * halo_on
            ac, ax, ab, az = (r[:, cs].astype(F32) for r in (ac_ref, ax_ref, ab_ref, az_ref))
            p = ac * ax
            pbuf[SMALL_HALO:SMALL_HALO + ta, cs] = p
            p1 = pbuf[7:7 + ta, cs]
            p2 = pbuf[6:6 + ta, cs]
            w0, w1, w2 = w_ref[0:1, cs], w_ref[1:2, cs], w_ref[2:3, cs]
            q = w0 * p2 + w1 * p1 + w2 * p
            sz, dsz = _silu_and_grad(az)
            dy = dy_ref[:, cs]
            t1 = dy * ab
            dq = t1 * sz
            dqbuf[0:ta, cs] = dq
            dpv = w2 * dq + w1 * dqbuf[1:1 + ta, cs] + w0 * dqbuf[2:2 + ta, cs]
            dp_ref[:, lb * LANES:(lb + 1) * LANES] = (dy * q * sz).astype(BF16)
            dp_ref[:, da + lb * LANES:da + (lb + 1) * LANES] = (dpv * ax).astype(BF16)
            dp_ref[:, 2 * da + lb * LANES:2 * da + (lb + 1) * LANES] = (dpv * ac).astype(BF16)
            dp_ref[:, 3 * da + lb * LANES:3 * da + (lb + 1) * LANES] = (t1 * q * dsz).astype(BF16)
            dw_ref[0, 0:1, cs] = dw_ref[0, 0:1, cs] + jnp.sum(dq * p2, axis=0, keepdims=True)
            dw_ref[0, 1:2, cs] = dw_ref[0, 1:2, cs] + jnp.sum(dq * p1, axis=0, keepdims=True)
            dw_ref[0, 2:3, cs] = dw_ref[0, 2:3, cs] + jnp.sum(dq * p, axis=0, keepdims=True)
            yield

    proj3 = proj.reshape(bl, dm.Lp, dm.NP)
    col = lambda w, k: pl.BlockSpec((bl, ta, w), lambda i: (0, nta - 1 - i, k))
    halo = lambda k: pl.BlockSpec((bl, HALO_BLOCK, da),
                                  lambda i: (0, jnp.maximum((nta - 1 - i) * (ta // HALO_BLOCK) - 1, 0), k))
    dp, dw = pl.pallas_call(
        _per_sequence(setup, body, bl, [True] * 7 + [False] + [True, "keep"] + [True, True]),
        name="mix_a_bwd", grid=(nta,),
        in_specs=[col(da, 0), col(da, 1), col(da, 2), col(da, 3), halo(1), halo(2), col(da, 0),
                  pl.BlockSpec((CONV_A_K, da), lambda i: (0, 0))],
        out_specs=[col(dm.WA, 0), pl.BlockSpec((bl, 8, da), lambda i: (0, 0, 0))],
        out_shape=[jax.ShapeDtypeStruct((bl, dm.Lp, dm.WA), BF16), jax.ShapeDtypeStruct((bl, 8, da), F32)],
        scratch_shapes=[pltpu.VMEM((bl, SMALL_HALO + ta, da), F32), pltpu.VMEM((bl, ta + SMALL_HALO, da), F32)],
        compiler_params=_params(("arbitrary",)),
    )(proj3, proj3, proj3, proj3, proj3, proj3, dya.reshape(bl, dm.Lp, da), conv_w)
    return dp.reshape(dm.R, dm.WA), dw


SUBLANES = 8
SHIFT_ROWS = TT + CONF_HALO - SUBLANES


TAP_ROWS = 64


def _split_lanes(buf, rows, val):
    for lb in range(val.shape[1] // LANES):
        buf[lb, rows, :] = val[:, lb * LANES:(lb + 1) * LANES]


def _join_lanes(buf):
    return jnp.concatenate([buf[lb] for lb in range(buf.shape[0])], axis=1)


def _fill_shifted(buf, shifted):
    def step(lb, carry):
        for r in range(1, SUBLANES):
            shifted[lb, r - 1, 0:SHIFT_ROWS, :] = buf[lb, r:r + SHIFT_ROWS, :]
        return carry

    lax.fori_loop(0, buf.shape[0], step, 0)


def _window(buf, shifted, d, r0, lb):
    r = d % SUBLANES
    rows = pl.ds(pl.multiple_of(r0 + (d - r), SUBLANES), TAP_ROWS)
    return buf[lb, rows, :] if r == 0 else shifted[lb, r - 1, rows, :]


def _tap_loop(nlb, body):
    per_lb = TT // TAP_ROWS

    def step(it, carry):
        lb = it // per_lb
        body(lb, pl.ds(pl.multiple_of(lb * LANES, LANES), LANES), pl.multiple_of((it % per_lb) * TAP_ROWS, TAP_ROWS))
        return carry

    lax.fori_loop(0, nlb * per_lb, step, 0)


TAP_CHAINS = 4


def _tree_sum(terms):
    sums = list(terms[:TAP_CHAINS])
    for n, t in enumerate(terms[TAP_CHAINS:]):
        sums[n % TAP_CHAINS] = sums[n % TAP_CHAINS] + t
    while len(sums) > 1:
        sums = [a + b for a, b in zip(sums[0::2], sums[1::2])] + ([sums[-1]] if len(sums) % 2 else [])
    return sums[0]


def _conf_conv(ubuf, ushift, w_ref, b_ref, u1buf):
    _fill_shifted(ubuf, ushift)

    def piece(lb, cs, r0):
        taps = [w_ref[k:k + 1, cs] * _window(ubuf, ushift, CONF_HALO - (CONF_K - 1) + k, r0, lb)
                for k in range(CONF_K)]
        u1buf[lb, pl.ds(r0, TAP_ROWS), :] = _tree_sum(taps) + b_ref[0:1, cs]

    _tap_loop(ubuf.shape[0], piece)


def _mix_c_fwd(proj, conv_w, conv_b, ln_g, ln_b, dm):
    dc = dm.DC
    nlb = dc // LANES
    c0 = (dm.WA + dm.WB) // dc
    ti = _tile_index(dm, False)

    def body(ca_ref, cg_ref, cz_ref, w_ref, b_ref, g_ref, be_ref, y_ref, u1_ref, ubuf, u1buf, ushift):
        i = pl.program_id(1)

        @pl.when(i == 0)
        def _():
            ubuf[:, 0:CONF_HALO, :] = jnp.zeros((nlb, CONF_HALO, LANES), F32)

        @pl.when(i > 0)
        def _():
            ubuf[:, 0:CONF_HALO, :] = ubuf[:, TT:TT + CONF_HALO, :]

        _split_lanes(ubuf, slice(CONF_HALO, CONF_HALO + TT),
                     ca_ref[...].astype(F32) * _sigmoid(cg_ref[...].astype(F32)))
        _conf_conv(ubuf, ushift, w_ref, b_ref, u1buf)
        u1 = _join_lanes(u1buf)
        u1_ref[...] = u1
        mu = jnp.mean(u1, axis=-1, keepdims=True)
        xc = u1 - mu
        rstd = lax.rsqrt(jnp.mean(xc * xc, axis=-1, keepdims=True) + LN_EPS)
        u2 = xc * rstd * g_ref[...] + be_ref[...]
        cz = cz_ref[...].astype(F32)
        y_ref[...] = ((u2 * _sigmoid(u2)) * (cz * _sigmoid(cz))).astype(BF16)

    col = lambda k: pl.BlockSpec((TT, dc), lambda b, i: (ti(b, i), c0 + k))
    vec = pl.BlockSpec((1, dc), lambda b, i: (0, 0))
    return pl.pallas_call(
        body, name="mix_c_fwd", grid=(dm.BL, dm.NT),
        in_specs=[col(0), col(1), col(2), pl.BlockSpec((CONF_K, dc), lambda b, i: (0, 0)), vec, vec, vec],
        out_specs=[pl.BlockSpec((TT, dc), lambda b, i: (ti(b, i), 0))] * 2,
        out_shape=[jax.ShapeDtypeStruct((dm.R, dc), BF16), jax.ShapeDtypeStruct((dm.R, dc), F32)],
        scratch_shapes=[pltpu.VMEM((nlb, CONF_HALO + TT, LANES), F32), pltpu.VMEM((nlb, TT, LANES), F32),
                        pltpu.VMEM((nlb, SUBLANES - 1, SHIFT_ROWS, LANES), F32)],
        compiler_params=_params(("parallel", "arbitrary")),
    )(proj, proj, proj, conv_w, conv_b, ln_g, ln_b)


def _mix_c_bwd(proj, u1, dyc, conv_w, ln_g, ln_b, dm):
    dc = dm.DC
    nlb = dc // LANES
    c0 = (dm.WA + dm.WB) // dc
    ti = _tile_index(dm, True)
    hi = _halo_index(dm, CONF_HALO)

    def body(ca_ref, cg_ref, cz_ref, cah_ref, cgh_ref, u1_ref, dy_ref, w_ref, g_ref, be_ref,
             dp_ref, dw_ref, dv_ref, ubuf, dubuf, du0buf, ushift, dshift, dwacc):
        i = pl.program_id(1)
        halo_on = jnp.where(i == dm.NT - 1, 0.0, 1.0)

        @pl.when(i == 0)
        def _():
            dwacc[...] = jnp.zeros_like(dwacc)
            dv_ref[...] = jnp.zeros_like(dv_ref)
            dubuf[:, TT:TT + CONF_HALO, :] = jnp.zeros((nlb, CONF_HALO, LANES), F32)

        @pl.when(i > 0)
        def _():
            dubuf[:, TT:TT + CONF_HALO, :] = dubuf[:, 0:CONF_HALO, :]

        _split_lanes(ubuf, slice(0, CONF_HALO),
                     cah_ref[...].astype(F32) * _sigmoid(cgh_ref[...].astype(F32)) * halo_on)
        sgg = _sigmoid(cg_ref[...].astype(F32))
        ca = ca_ref[...].astype(F32)
        _split_lanes(ubuf, slice(CONF_HALO, CONF_HALO + TT), ca * sgg)
        _fill_shifted(ubuf, ushift)
        u1 = u1_ref[...]
        mu = jnp.mean(u1, axis=-1, keepdims=True)
        xc = u1 - mu
        rstd = lax.rsqrt(jnp.mean(xc * xc, axis=-1, keepdims=True) + LN_EPS)
        xhat = xc * rstd
        u2 = xhat * g_ref[...] + be_ref[...]
        su, dsu = _silu_and_grad(u2)
        sz, dsz = _silu_and_grad(cz_ref[...].astype(F32))
        dy = dy_ref[...]
        du2 = dy * dsu * sz
        dp_ref[:, 2 * dc:3 * dc] = (dy * su * dsz).astype(BF16)
        dxhat = du2 * g_ref[...]
        du1 = rstd * (dxhat - jnp.mean(dxhat, axis=-1, keepdims=True)
                      - xhat * jnp.mean(dxhat * xhat, axis=-1, keepdims=True))
        dv_ref[0, 0:1, :] = dv_ref[0, 0:1, :] + jnp.sum(du1, axis=0, keepdims=True)
        dv_ref[0, 1:2, :] = dv_ref[0, 1:2, :] + jnp.sum(du2 * xhat, axis=0, keepdims=True)
        dv_ref[0, 2:3, :] = dv_ref[0, 2:3, :] + jnp.sum(du2, axis=0, keepdims=True)
        _split_lanes(dubuf, slice(0, TT), du1)
        _fill_shifted(dubuf, dshift)

        def piece(lb, cs, r0):
            du0buf[lb, pl.ds(r0, TAP_ROWS), :] = _tree_sum(
                [w_ref[k:k + 1, cs] * _window(dubuf, dshift, CONF_K - 1 - k, r0, lb) for k in range(CONF_K)])
            d1 = dubuf[lb, pl.ds(r0, TAP_ROWS), :]
            for k in range(CONF_K):
                prod = d1 * _window(ubuf, ushift, CONF_HALO - (CONF_K - 1) + k, r0, lb)
                dwacc[lb, k] = dwacc[lb, k] + jnp.sum(prod.reshape(TAP_ROWS // SUBLANES, SUBLANES, LANES), axis=0)

        _tap_loop(nlb, piece)
        du0 = _join_lanes(du0buf)
        dp_ref[:, 0:dc] = (du0 * sgg).astype(BF16)
        dp_ref[:, dc:2 * dc] = (du0 * ca * sgg * (1.0 - sgg)).astype(BF16)

        @pl.when(i == dm.NT - 1)
        def _():
            for lb in range(nlb):
                dw_ref[0, 0:CONF_K, lb * LANES:(lb + 1) * LANES] = jnp.sum(dwacc[lb], axis=1)
            dw_ref[0, CONF_K:CONF_K + 1, :] = jnp.zeros((1, dc), F32)

    col = lambda k: pl.BlockSpec((TT, dc), lambda b, i: (ti(b, i), c0 + k))
    halo = lambda k: pl.BlockSpec((CONF_HALO, dc), lambda b, i: (hi(b, i), c0 + k))
    vec = pl.BlockSpec((1, dc), lambda b, i: (0, 0))
    return pl.pallas_call(
        body, name="mix_c_bwd", grid=(dm.BL, dm.NT),
        in_specs=[col(0), col(1), col(2), halo(0), halo(1),
                  pl.BlockSpec((TT, dc), lambda b, i: (ti(b, i), 0)),
                  pl.BlockSpec((TT, dc), lambda b, i: (ti(b, i), 0)),
                  pl.BlockSpec((CONF_K, dc), lambda b, i: (0, 0)), vec, vec],
        out_specs=[pl.BlockSpec((TT, dm.WC), lambda b, i: (ti(b, i), 0)),
                   pl.BlockSpec((1, 32, dc), lambda b, i: (b, 0, 0)),
                   pl.BlockSpec((1, 8, dc), lambda b, i: (b, 0, 0))],
        out_shape=[jax.ShapeDtypeStruct((dm.R, dm.WC), BF16),
                   jax.ShapeDtypeStruct((dm.BL, 32, dc), F32),
                   jax.ShapeDtypeStruct((dm.BL, 8, dc), F32)],
        scratch_shapes=[pltpu.VMEM((nlb, CONF_HALO + TT, LANES), F32),
                        pltpu.VMEM((nlb, TT + CONF_HALO, LANES), F32), pltpu.VMEM((nlb, TT, LANES), F32),
                        pltpu.VMEM((nlb, SUBLANES - 1, SHIFT_ROWS, LANES), F32),
                        pltpu.VMEM((nlb, SUBLANES - 1, SHIFT_ROWS, LANES), F32),
                        pltpu.VMEM((nlb, CONF_K, SUBLANES, LANES), F32)],
        compiler_params=_params(("parallel", "arbitrary")),
    )(proj, proj, proj, proj, proj, u1, dyc, conv_w, ln_g, ln_b)


RAW_HALO = 16


def _shift_matrix():
    r = jnp.arange((SSM_CONV_K - 1) * TT)[:, None]
    want = TT + r % TT - (SSM_CONV_K - 1 - r // TT)
    return (jnp.arange(2 * TT)[None, :] == want).astype(BF16)


def _ssm_conv(rawwin, sh, s_ref, w_ref, b_ref, width):
    sh[...] = _dot(s_ref[...], rawwin[...])
    for lb in range(width // LANES):
        cs = slice(lb * LANES, (lb + 1) * LANES)
        acc = b_ref[0:1, cs] + w_ref[SSM_CONV_K - 1:SSM_CONV_K, cs] * rawwin[TT:2 * TT, cs].astype(F32)
        for k in range(SSM_CONV_K - 1):
            acc = acc + w_ref[k:k + 1, cs] * sh[k * TT:(k + 1) * TT, cs]
        yield cs, acc


def _softplus(z):
    return jnp.maximum(z, 0.0) + jnp.log(1.0 + jnp.exp(-jnp.abs(z)))


def _tri(lower):
    r = lax.broadcasted_iota(jnp.int32, (TT, TT), 0)
    c = lax.broadcasted_iota(jnp.int32, (TT, TT), 1)
    return (c <= r) if lower else (c >= r)


def _exact_01_dot(mat01, x):
    x1, x2, x3 = _split3(x)
    return _dot(mat01, x1) + _dot(mat01, x2) + _dot(mat01, x3)


def _head_scalars(dt_ref, dtb_ref, alog_ref):
    z = dt_ref[...] + dtb_ref[...]
    dtv = _softplus(z)
    a = -jnp.exp(alog_ref[...])
    ac = _exact_01_dot(_tri(True).astype(F32).astype(BF16), dtv * a)
    eac = jnp.exp(ac)
    dst = jnp.exp(ac[TT - 1:TT, :] - ac)
    return z, dtv, a, ac, eac, dst


FAR_BELOW = -1e30


def _decay(ac, ac_t, h, causal):
    return jnp.exp(jnp.where(causal, ac[:, h:h + 1] - ac_t[h:h + 1, :], FAR_BELOW))


def _own_half(x16, h):
    lane = lax.broadcasted_iota(jnp.int32, (1, LANES), 1)
    keep = (lane >= SSM_HEAD_DIM) if (h % 2) else (lane < SSM_HEAD_DIM)
    return jnp.where(keep, x16, jnp.zeros_like(x16))


def _per_sequence(setup, body, bl, how):
    def all_sequences(*refs):
        views = [[r.at[b] if h is True else (r.at[pl.ds(b, 1)] if h == "keep" else r) for r, h in zip(refs, how)]
                 for b in range(bl)]
        for v in views:
            setup(*v)
        running = [body(*v) for v in views]
        while running:
            running = [g for g in running if next(g, "done") != "done"]

    return all_sequences


def _mix_b_fwd(proj, projdt, conv_w, conv_b, dt_bias, a_log, dskx, norm_g, expand, dm, rider=None):
    db, gn, xbc_w, hpg = dm.DB, dm.GN, dm.XBC, dm.HPG
    gw = db // SSM_GROUPS

    def setup(bz_ref, bx_ref, bc_ref, dt_ref, w_ref, b_ref, dtb_ref, alog_ref, dsk_ref, g_ref, e_ref, s_ref,
              y_ref, yraw_ref, sprev_ref, rawwin, sh, xbuf, state, ybuf, exbuf, xdtbuf):
        i = pl.program_id(0)

        @pl.when(i == 0)
        def _():
            rawwin[0:TT, :] = jnp.zeros((TT, xbc_w), BF16)
            state[...] = jnp.zeros_like(state)

        @pl.when(i > 0)
        def _():
            rawwin[TT - RAW_HALO:TT, :] = rawwin[2 * TT - RAW_HALO:2 * TT, :]

    def body(bz_ref, bx_ref, bc_ref, dt_ref, w_ref, b_ref, dtb_ref, alog_ref, dsk_ref, g_ref, e_ref, s_ref,
             y_ref, yraw_ref, sprev_ref, rawwin, sh, xbuf, state, ybuf, exbuf, xdtbuf):
        rawwin[TT:2 * TT, 0:db] = bx_ref[...]
        rawwin[TT:2 * TT, db:xbc_w] = bc_ref[...]
        for cs, pre in _ssm_conv(rawwin, sh, s_ref, w_ref, b_ref, xbc_w):
            xbuf[:, cs] = pre * _sigmoid(pre)
            yield

        _, dtv, _, ac, eac, dst = _head_scalars(dt_ref, dtb_ref, alog_ref)
        exbuf[...] = _dot(jnp.concatenate([dtv, eac, dst], axis=0).astype(BF16), e_ref[...])
        ac_t = ac.T
        causal = _tri(True)
        sprev_ref[0, 0] = state[...]
        yield

        xdtbuf[...] = xbuf[:, 0:db] * exbuf[0:TT, :]
        ybuf[...] = xbuf[:, 0:db] * dsk_ref[...]
        for g in range(SSM_GROUPS):
            gs = slice(g * gw, (g + 1) * gw)
            bg = xbuf[:, db + g * SSM_STATE:db + (g + 1) * SSM_STATE].astype(BF16)
            cg = xbuf[:, db + gn + g * SSM_STATE:db + gn + (g + 1) * SSM_STATE].astype(BF16)
            cb = _dot_nt(cg, bg)
            for e in range(0, hpg, 2):
                h = g * hpg + e
                ps = slice(h * SSM_HEAD_DIM, (h + 2) * SSM_HEAD_DIM)
                xp16 = xdtbuf[:, ps].astype(BF16)
                acc = jnp.zeros((TT, LANES), F32)
                for hh in (h, h + 1):
                    mm = (cb * _decay(ac, ac_t, hh, causal)).astype(BF16)
                    acc = acc + _dot(mm, _own_half(xp16, hh))
                ybuf[:, ps] = ybuf[:, ps] + acc
                yield
            sg = state[:, gs]
            ybuf[:, gs] = ybuf[:, gs] + exbuf[TT:2 * TT, gs] * _dot(cg, sg.astype(BF16))
            state[:, gs] = sg * exbuf[2 * TT - 1:2 * TT, gs] + _dot_tn(
                bg, (xdtbuf[:, gs] * exbuf[2 * TT:3 * TT, gs]).astype(BF16))
            yield

        yraw = ybuf[...]
        yraw_ref[...] = yraw
        bz = bz_ref[...].astype(F32)
        v = yraw * (bz * _sigmoid(bz))
        r = lax.rsqrt(jnp.mean(v * v, axis=-1, keepdims=True) + NORM_EPS)
        y_ref[...] = (v * r * g_ref[...]).astype(BF16)

    bl = dm.BL
    tile = lambda w, k: pl.BlockSpec((bl, TT, w), lambda i: (0, i, k))
    fixed = lambda r, w: pl.BlockSpec((r, w), lambda i: (0, 0))
    proj3, dt3 = proj.reshape(bl, dm.Lp, dm.NP), projdt.reshape(bl, dm.Lp, DT_PAD)
    scratch = [((2 * TT, xbc_w), BF16), (((SSM_CONV_K - 1) * TT, xbc_w), F32), ((TT, xbc_w), F32),
               ((SSM_STATE, db), F32), ((TT, db), F32), ((3 * TT, db), F32), ((TT, db), F32)]
    (y, yraw, sprev), rode = _call(
        _per_sequence(setup, body, bl, [True] * 4 + [False] * 8 + [True, True, "keep"] + [True] * len(scratch)),
        "mix_b_fwd", (dm.NT,),
        [tile(db, dm.WA // db), tile(db, dm.WA // db + 1), tile(2 * gn, (dm.WA + 2 * db) // (2 * gn)),
         tile(DT_PAD, 0),
         fixed(SSM_CONV_K, xbc_w), fixed(1, xbc_w), fixed(1, DT_PAD), fixed(1, DT_PAD),
         fixed(1, db), fixed(1, db), fixed(DT_PAD, db), fixed((SSM_CONV_K - 1) * TT, 2 * TT)],
        [tile(db, 0), tile(db, 0), pl.BlockSpec((bl, 1, SSM_STATE, db), lambda i: (0, i, 0, 0))],
        [jax.ShapeDtypeStruct((bl, dm.Lp, db), BF16), jax.ShapeDtypeStruct((bl, dm.Lp, db), F32),
         jax.ShapeDtypeStruct((bl, dm.NT, SSM_STATE, db), F32)],
        [pltpu.VMEM((bl,) + s, t) for s, t in scratch],
        ("arbitrary",),
        (proj3, proj3, proj3, dt3, conv_w, conv_b, dt_bias, a_log, dskx, norm_g, expand, _shift_matrix()), rider)
    return (y.reshape(dm.R, db), yraw.reshape(dm.R, db), sprev), rode


def _mix_b_bwd(proj, projdt, dyb, yraw, sprev, conv_w, conv_b, dt_bias, a_log, dskx, norm_g, expand, expand_t, dm,
               rider=None):
    db, gn, xbc_w, hpg = dm.DB, dm.GN, dm.XBC, dm.HPG
    gw = db // SSM_GROUPS

    def setup(bz_ref, bx_ref, bc_ref, dt_ref, bxh_ref, bch_ref, dy_ref, yraw_ref, sprev_ref,
              w_ref, b_ref, dtb_ref, alog_ref, dsk_ref, g_ref, e_ref, et_ref, s_ref,
              dp_ref, dpt_ref, dwc_ref, dch_ref, dhd_ref,
              rawwin, sh, xbuf, dsbuf, dstate, dxbuf, z1buf, dprebuf, exbuf, xdtbuf, dyrbuf, uvec):
        i = pl.program_id(0)

        @pl.when(i == 0)
        def _():
            dwc_ref[...] = jnp.zeros_like(dwc_ref)
            dch_ref[...] = jnp.zeros_like(dch_ref)
            dhd_ref[...] = jnp.zeros_like(dhd_ref)
            dstate[...] = jnp.zeros_like(dstate)
            dprebuf[TT:TT + SMALL_HALO, :] = jnp.zeros((SMALL_HALO, xbc_w), F32)
            rawwin[0:TT - RAW_HALO, :] = jnp.zeros((TT - RAW_HALO, xbc_w), BF16)

        @pl.when(i > 0)
        def _():
            dprebuf[TT:TT + SMALL_HALO, :] = dprebuf[0:SMALL_HALO, :]

    def body(bz_ref, bx_ref, bc_ref, dt_ref, bxh_ref, bch_ref, dy_ref, yraw_ref, sprev_ref,
             w_ref, b_ref, dtb_ref, alog_ref, dsk_ref, g_ref, e_ref, et_ref, s_ref,
             dp_ref, dpt_ref, dwc_ref, dch_ref, dhd_ref,
             rawwin, sh, xbuf, dsbuf, dstate, dxbuf, z1buf, dprebuf, exbuf, xdtbuf, dyrbuf, uvec):
        halo_on = jnp.where(pl.program_id(0) == dm.NT - 1, 0.0, 1.0).astype(BF16)

        rawwin[TT - RAW_HALO:TT, 0:db] = bxh_ref[...] * halo_on
        rawwin[TT - RAW_HALO:TT, db:xbc_w] = bch_ref[...] * halo_on
        rawwin[TT:2 * TT, 0:db] = bx_ref[...]
        rawwin[TT:2 * TT, db:xbc_w] = bc_ref[...]
        for cs, pre in _ssm_conv(rawwin, sh, s_ref, w_ref, b_ref, xbc_w):
            sl, dsl = _silu_and_grad(pre)
            xbuf[:, cs] = sl
            dsbuf[:, cs] = dsl
            yield

        z, dtv, a, ac, eac, dst = _head_scalars(dt_ref, dtb_ref, alog_ref)
        exbuf[...] = _dot(jnp.concatenate([dtv, eac, dst], axis=0).astype(BF16), e_ref[...])
        ac_t = ac.T
        causal = _tri(True)
        xdtbuf[...] = xbuf[:, 0:db] * exbuf[0:TT, :]

        yraw = yraw_ref[...]
        sz, dsz = _silu_and_grad(bz_ref[...].astype(F32))
        v = yraw * sz
        r = lax.rsqrt(jnp.mean(v * v, axis=-1, keepdims=True) + NORM_EPS)
        dy = dy_ref[...]
        dyg = dy * g_ref[...]
        dv = r * dyg - v * (r * r * r * jnp.mean(dyg * v, axis=-1, keepdims=True))
        dch_ref[0, 0:1, :] = dch_ref[0, 0:1, :] + jnp.sum(dy * v * r, axis=0, keepdims=True)
        dyr = dv * sz
        dyrbuf[...] = dyr
        dp_ref[:, 0:db] = (dv * yraw * dsz).astype(BF16)
        dch_ref[0, 1:2, :] = dch_ref[0, 1:2, :] + jnp.sum(dyr * xbuf[:, 0:db], axis=0, keepdims=True)

        lane_row = lax.broadcasted_iota(jnp.int32, (1, LANES), 1)
        sub_col = lax.broadcasted_iota(jnp.int32, (LANES, 1), 0)
        dac = jnp.zeros((TT, LANES), F32)
        colacc = jnp.zeros((LANES, TT), F32)
        for g in range(SSM_GROUPS):
            gs = slice(g * gw, (g + 1) * gw)
            bs_ = slice(db + g * SSM_STATE, db + (g + 1) * SSM_STATE)
            cs_ = slice(db + gn + g * SSM_STATE, db + gn + (g + 1) * SSM_STATE)
            bg = xbuf[:, bs_].astype(BF16)
            cg = xbuf[:, cs_].astype(BF16)
            cb = _dot_nt(cg, bg)
            dcb = jnp.zeros((TT, TT), F32)
            for e in range(0, hpg, 2):
                h = g * hpg + e
                ps = slice(h * SSM_HEAD_DIM, (h + 2) * SSM_HEAD_DIM)
                xp16 = xdtbuf[:, ps].astype(BF16)
                dyp16 = dyrbuf[:, ps].astype(BF16)
                acc = jnp.zeros((TT, LANES), F32)
                for hh in (h, h + 1):
                    dec = _decay(ac, ac_t, hh, causal)
                    mm = cb * dec
                    dyh = _own_half(dyp16, hh)
                    dmm = _dot_nt(dyh, xp16)
                    acc = acc + _dot_tn(mm.astype(BF16), dyh)
                    dcb = dcb + dmm * dec
                    gm = dmm * mm
                    dac = jnp.where(lane_row == hh, jnp.sum(gm, axis=1, keepdims=True), dac)
                    colacc = jnp.where(sub_col == hh, jnp.sum(gm, axis=0, keepdims=True), colacc)
                dxbuf[:, ps] = acc
                yield
            sg32 = sprev_ref[0, 0, :, gs]
            sg = sg32.astype(BF16)
            dsn = dstate[:, gs]
            dsn16 = dsn.astype(BF16)
            dcb16 = dcb.astype(BF16)
            eacx = exbuf[TT:2 * TT, gs]
            dstx = exbuf[2 * TT:3 * TT, gs]
            cdx = exbuf[2 * TT - 1:2 * TT, gs]
            dye16 = (dyrbuf[:, gs] * eacx).astype(BF16)
            xdt_g = xdtbuf[:, gs]
            dxbuf[:, cs_] = _dot(dcb16, bg) + _dot_nt(dye16, sg)
            dst_x = dstx * _dot(bg, dsn16)
            dxbuf[:, bs_] = _dot_tn(dcb16, cg) + _dot_nt((dstx * xdt_g).astype(BF16), dsn16)
            dstate[:, gs] = cdx * dsn + _dot_tn(cg, dye16)
            z1buf[:, gs] = dyrbuf[:, gs] * (eacx * _dot(cg, sg)) - xdt_g * dst_x
            uvec[:, gs] = jnp.broadcast_to(
                jnp.sum(xdt_g * dst_x, axis=0, keepdims=True) + jnp.sum(dsn * cdx * sg32, axis=0, keepdims=True),
                (8, gw))
            dxbuf[:, gs] = dxbuf[:, gs] + dst_x
            yield

        zz = _dot(jnp.concatenate([z1buf[...], dxbuf[:, 0:db] * xbuf[:, 0:db]], axis=0).astype(BF16), et_ref[...])
        u1, u2, u3 = _split3(uvec[...])
        ulast = (_dot(u1, et_ref[...]) + _dot(u2, et_ref[...]) + _dot(u3, et_ref[...]))[0:1, :]
        is_last = (lax.broadcasted_iota(jnp.int32, (TT, 1), 0) == TT - 1).astype(F32)
        dac = dac - colacc.T + zz[0:TT] + is_last * ulast
        dda = _exact_01_dot(_tri(False).astype(F32).astype(BF16), dac)
        ddt = dda * a + zz[TT:2 * TT]
        dhd_ref[0, 1:2, :] = dhd_ref[0, 1:2, :] + jnp.sum(dda * dtv, axis=0, keepdims=True) * a
        ddtraw = ddt * _sigmoid(z)
        dhd_ref[0, 0:1, :] = dhd_ref[0, 0:1, :] + jnp.sum(ddtraw, axis=0, keepdims=True)
        dpt_ref[...] = ddtraw.astype(BF16)
        dxbuf[:, 0:db] = dyrbuf[...] * dsk_ref[...] + dxbuf[:, 0:db] * exbuf[0:TT, :]

        for lb in range(xbc_w // LANES):
            cs = slice(lb * LANES, (lb + 1) * LANES)
            dpre = dxbuf[:, cs] * dsbuf[:, cs]
            dprebuf[0:TT, cs] = dpre
            dwc_ref[0, SSM_CONV_K:SSM_CONV_K + 1, cs] = dwc_ref[0, SSM_CONV_K:SSM_CONV_K + 1, cs] + jnp.sum(
                dpre, axis=0, keepdims=True)
            draw = w_ref[SSM_CONV_K - 1:SSM_CONV_K, cs] * dpre
            for k in range(SSM_CONV_K - 1):
                ahead = SSM_CONV_K - 1 - k
                draw = draw + w_ref[k:k + 1, cs] * dprebuf[ahead:ahead + TT, cs]
            for k in range(SSM_CONV_K):
                moved = sh[k * TT:(k + 1) * TT, cs] if k < SSM_CONV_K - 1 else rawwin[TT:2 * TT, cs].astype(F32)
                dwc_ref[0, k:k + 1, cs] = dwc_ref[0, k:k + 1, cs] + jnp.sum(dpre * moved, axis=0, keepdims=True)
            dp_ref[:, db + lb * LANES:db + (lb + 1) * LANES] = draw.astype(BF16)
            yield

    bl, nt = dm.BL, dm.NT
    tile = lambda w, k: pl.BlockSpec((bl, TT, w), lambda i: (0, nt - 1 - i, k))
    halo = lambda w, k: pl.BlockSpec((bl, HALO_BLOCK, w),
                                     lambda i: (0, jnp.maximum((nt - 1 - i) * (TT // HALO_BLOCK) - 1, 0), k))
    fixed = lambda r, w: pl.BlockSpec((r, w), lambda i: (0, 0))
    sums = lambda w: pl.BlockSpec((bl, 8, w), lambda i: (0, 0, 0))
    kz = dm.WA // db
    kc = (dm.WA + 2 * db) // (2 * gn)
    proj3, dt3 = proj.reshape(bl, dm.Lp, dm.NP), projdt.reshape(bl, dm.Lp, DT_PAD)
    scratch = [((2 * TT, xbc_w), BF16), (((SSM_CONV_K - 1) * TT, xbc_w), F32), ((TT, xbc_w), F32),
               ((TT, xbc_w), F32), ((SSM_STATE, db), F32), ((TT, xbc_w), F32), ((TT, db), F32),
               ((TT + SMALL_HALO, xbc_w), F32), ((3 * TT, db), F32), ((TT, db), F32), ((TT, db), F32), ((8, db), F32)]
    how = [True] * 8 + ["keep"] + [False] * 9 + [True, True, "keep", "keep", "keep"] + [True] * len(scratch)
    (dp, dpt, dwc, dch, dhd), rode = _call(
        _per_sequence(setup, body, bl, how), "mix_b_bwd", (nt,),
        [tile(db, kz), tile(db, kz + 1), tile(2 * gn, kc), tile(DT_PAD, 0),
         halo(db, kz + 1), halo(2 * gn, kc), tile(db, 0), tile(db, 0),
         pl.BlockSpec((bl, 1, SSM_STATE, db), lambda i: (0, nt - 1 - i, 0, 0)),
         fixed(SSM_CONV_K, xbc_w), fixed(1, xbc_w), fixed(1, DT_PAD), fixed(1, DT_PAD),
         fixed(1, db), fixed(1, db), fixed(DT_PAD, db), fixed(db, DT_PAD), fixed((SSM_CONV_K - 1) * TT, 2 * TT)],
        [tile(dm.WB, 0), tile(DT_PAD, 0), sums(xbc_w), sums(db), sums(DT_PAD)],
        [jax.ShapeDtypeStruct((bl, dm.Lp, dm.WB), BF16), jax.ShapeDtypeStruct((bl, dm.Lp, DT_PAD), BF16),
         jax.ShapeDtypeStruct((bl, 8, xbc_w), F32), jax.ShapeDtypeStruct((bl, 8, db), F32),
         jax.ShapeDtypeStruct((bl, 8, DT_PAD), F32)],
        [pltpu.VMEM((bl,) + s, t) for s, t in scratch],
        ("arbitrary",),
        (proj3, proj3, proj3, dt3, proj3, proj3, dyb.reshape(bl, dm.Lp, db), yraw.reshape(bl, dm.Lp, db), sprev,
         conv_w, conv_b, dt_bias, a_log, dskx, norm_g, expand, expand_t, _shift_matrix()), rider)
    return (dp.reshape(dm.R, dm.WB), dpt.reshape(dm.R, DT_PAD), dwc, dch, dhd), rode


def _head_consts(dm):
    head_of = jnp.arange(dm.DB) // SSM_HEAD_DIM
    expand = (jnp.arange(DT_PAD)[:, None] == head_of[None, :]).astype(BF16)
    return expand, expand.T


def _ssm_params(lw, dm):
    pad_h = lambda v: jnp.pad(v, (0, DT_PAD - dm.H))[None]
    return (lw["ssm_conv_w"], lw["ssm_conv_b"][None], pad_h(lw["dt_bias"]), pad_h(lw["a_log"]),
            jnp.repeat(lw["d_skip"], SSM_HEAD_DIM)[None], lw["ssm_norm_g"][None])


def _layer_fwd(h, lw, w_in, w_out, cst, dm, next_bases=None, own_out=None):
    nxt = next_bases is not None
    n_next = len(next_bases) if nxt else 0

    def beside(rider, extra):
        return extra if rider is None else (rider if extra is None else _ride_both(rider, extra))

    (proj, projdt, hn), got = _fwd_in(
        h, lw["pre_g"][None], w_in, dm,
        beside(_ride_gather_ici(next_bases, 0, 2) if nxt else None,
               None if own_out is None else _ride_gather_ici([own_out])))
    ya = _mix_a_fwd(proj, lw["conv_a_w"], dm)
    (yb, yraw, sprev), got = _mix_b_fwd(
        proj, projdt, *_ssm_params(lw, dm), cst[0], dm,
        beside(_ride_gather_ici(got[:n_next], 1, 2) if nxt else None,
               None if own_out is None else _ride_gather_d2d(got[n_next:])))
    if own_out is not None:
        w_out = got[n_next].reshape(2 * dm.D, dm.D)
        got = got[:n_next]
    yc, u1 = _mix_c_fwd(proj, lw["conf_conv_w"], lw["conf_conv_b"][None], lw["conf_ln_g"][None],
                        lw["conf_ln_b"][None], dm)
    (h_new, m), got = _fwd_out(ya, yb, yc, w_out, h, lw["post_g"][None], dm, _ride_gather_d2d(got) if nxt else None)
    return h_new, (h, hn, proj, projdt, ya, yb, yc, u1, yraw, sprev, m), got, w_out


def _layer_bwd(dh, saved, lw, w_in, w_out, cst, dm, reduce=None, last=False):
    h_in, hn, proj, projdt, ya, yb, yc, u1, yraw, sprev, m = saved
    (dya, dyb, dyc, dwo, dpost), got = _bwd_out(dh, m, lw["post_g"][None], w_out, ya, yb, yc, dm,
                                                None if reduce is None else reduce.swap())
    dpa, dwa = _mix_a_bwd(proj, dya, lw["conv_a_w"], dm)
    (dpb, dpt, dwcv, dch, dhd), got = _mix_b_bwd(proj, projdt, dyb, yraw, sprev, *_ssm_params(lw, dm), cst[0],
                                                 cst[1], dm, None if reduce is None else reduce.to_owners(got))
    dpc, dwcf, dvc = _mix_c_bwd(proj, u1, dyc, lw["conf_conv_w"], lw["conf_ln_g"][None], lw["conf_ln_b"][None], dm)
    def own_reduce():
        pieces = _bwd_in_dw(hn, [dpa, dpb, dpc, dpt], dm)
        return _GradReduce([_grad_to_shards(pieces, dm), dwo.reshape(N_CHIPS, 2 * dm.D // N_CHIPS, dm.D)])

    rider = None if reduce is None else reduce.join(got)
    n_join = 0 if rider is None else len(rider.out_shapes)
    if last:
        mine = own_reduce()
        to_owners = mine.to_owners(_exchange("grad_swap_halves", mine.swap()))
        rider = to_owners if rider is None else _ride_both(rider, to_owners)
    (dh, dpre), got = _bwd_in_dx(dpa, dpb, dpc, dpt, w_in, h_in, dh, lw["pre_g"][None], dm, rider)
    if reduce is not None:
        reduce.finish(got[:n_join])
    if last:
        mine.finish(_exchange("grad_join_halves", mine.join(got[n_join:])))
    else:
        mine = own_reduce()
    dwcv, dch, dhd, dvc = (jnp.sum(a, axis=0) for a in (dwcv, dch, dhd, dvc))
    small = dict(pre_g=dpre[0], post_g=dpost[0], conv_a_w=jnp.sum(dwa, axis=0)[:CONV_A_K],
                 ssm_conv_w=dwcv[:SSM_CONV_K], ssm_conv_b=dwcv[SSM_CONV_K], ssm_norm_g=dch[0],
                 d_skip=jnp.sum(dch[1].reshape(dm.H, SSM_HEAD_DIM), axis=1), dt_bias=dhd[0, :dm.H],
                 a_log=dhd[1, :dm.H], conf_conv_w=jnp.sum(dwcf, axis=0)[:CONF_K], conf_conv_b=dvc[0],
                 conf_ln_g=dvc[1], conf_ln_b=dvc[2])
    return dh, mine, small


def _shard_runs(dm):
    ab = dm.WA + dm.WB
    order = [(0, 0, ab), (ab, dm.DT0, dm.H), (ab + dm.H, ab, dm.WC)]
    k = dm.NIN // N_CHIPS
    runs = []
    for s in range(N_CHIPS):
        for o0, m0, wd in order:
            lo, hi = max(o0, s * k), min(o0 + wd, (s + 1) * k)
            if lo < hi:
                runs.append((s, lo - s * k, m0 + lo - o0, hi - lo))
    return runs


def _w_in_from_shards(base, dm):
    tr = _row_tile(dm.D, 256)
    k = dm.NIN // N_CHIPS
    runs = _shard_runs(dm)

    def body(in_ref, out_ref):
        for s, sc, mc, wd in runs:
            out_ref[:, mc:mc + wd] = in_ref[s, :, sc:sc + wd]
        out_ref[:, dm.DT0 + dm.H:dm.NP] = jnp.zeros((tr, dm.NP - dm.DT0 - dm.H), BF16)

    return pl.pallas_call(
        body, name="w_in_from_shards", grid=(dm.D // tr,),
        in_specs=[pl.BlockSpec((N_CHIPS, tr, k), lambda r: (0, r, 0))],
        out_specs=pl.BlockSpec((tr, dm.NP), lambda r: (r, 0)),
        out_shape=jax.ShapeDtypeStruct((dm.D, dm.NP), BF16),
        compiler_params=_params(("parallel",)),
    )(base)


def _grad_to_shards(pieces, dm):
    tr = _row_tile(dm.D, 256)
    k = dm.NIN // N_CHIPS
    starts = [0, dm.WA, dm.WA + dm.WB, dm.DT0]
    widths = [dm.WA, dm.WB, dm.WC, DT_PAD]
    runs = _shard_runs(dm)

    def body(a_ref, b_ref, c_ref, t_ref, out_ref):
        refs = (a_ref, b_ref, c_ref, t_ref)
        for s, sc, mc, wd in runs:
            for p in range(4):
                lo, hi = max(mc, starts[p]), min(mc + wd, starts[p] + widths[p])
                if lo < hi:
                    out_ref[s, :, sc + lo - mc:sc + hi - mc] = refs[p][:, lo - starts[p]:hi - starts[p]].astype(BF16)

    return pl.pallas_call(
        body, name="grad_to_shards", grid=(dm.D // tr,),
        in_specs=[pl.BlockSpec((tr, w), lambda r: (r, 0)) for w in widths],
        out_specs=pl.BlockSpec((N_CHIPS, tr, k), lambda r: (0, r, 0)),
        out_shape=jax.ShapeDtypeStruct((N_CHIPS, dm.D, k), BF16),
        compiler_params=_params(("parallel",)),
    )(*pieces)


def _place_own(w, layer, me):
    _, rows, cols = w.shape
    tr = _row_tile(rows, 256)

    def body(me_ref, w_ref, out_ref):
        out_ref[0] = w_ref[0].astype(BF16)

    return pl.pallas_call(
        body, name="place_own",
        grid_spec=pltpu.PrefetchScalarGridSpec(
            num_scalar_prefetch=1, grid=(rows // tr,),
            in_specs=[pl.BlockSpec((1, tr, cols), lambda r, me_ref: (layer, r, 0))],
            out_specs=pl.BlockSpec((1, tr, cols), lambda r, me_ref: (me_ref[0], r, 0))),
        out_shape=jax.ShapeDtypeStruct((N_CHIPS, rows, cols), BF16),
        compiler_params=_params(("parallel",)),
    )(me, w)


def _add_halves(g, got, c, name):
    _, _, rows, cols = g.shape
    tr = _row_tile(rows, 256)

    def body(c_ref, g_ref, got_ref, out_ref):
        out_ref[0] = (g_ref[0, 0].astype(F32) + got_ref[0].astype(F32)).astype(BF16)

    return pl.pallas_call(
        body, name=name,
        grid_spec=pltpu.PrefetchScalarGridSpec(
            num_scalar_prefetch=1, grid=(N_CHIPS, rows // tr),
            in_specs=[pl.BlockSpec((1, 1, tr, cols), lambda s, r, c_ref: (s, c_ref[0], r, 0)),
                      pl.BlockSpec((1, tr, cols), lambda s, r, c_ref: (s, r, 0))],
            out_specs=pl.BlockSpec((1, tr, cols), lambda s, r, c_ref: (s, r, 0))),
        out_shape=jax.ShapeDtypeStruct((N_CHIPS, rows, cols), BF16),
        compiler_params=_params(("parallel", "parallel")),
    )(c, g, got)


def _add_owner(p, got, where, name):
    _, rows, cols = p.shape
    tr = _row_tile(rows, 256)

    def body(w_ref, p_ref, got_ref, out_ref):
        acc = p_ref[0].astype(F32)
        for j in range(3):
            acc = acc + got_ref[j].astype(F32)
        out_ref[0] = acc

    return pl.pallas_call(
        body, name=name,
        grid_spec=pltpu.PrefetchScalarGridSpec(
            num_scalar_prefetch=1, grid=(rows // tr,),
            in_specs=[pl.BlockSpec((1, tr, cols), lambda r, w_ref: (w_ref[0], r, 0)),
                      pl.BlockSpec((3, tr, cols), lambda r, w_ref: (0, r, 0))],
            out_specs=pl.BlockSpec((1, tr, cols), lambda r, w_ref: (w_ref[1], r, 0))),
        out_shape=jax.ShapeDtypeStruct((2, rows, cols), F32),
        compiler_params=_params(("parallel",)),
    )(where, p, got)


class _GradReduce:
    def __init__(self, gs):
        self.gs = [g.reshape((N_CHIPS, 2, g.shape[1] // 2) + g.shape[2:]) for g in gs]
        self.c = lax.axis_index("c").astype(jnp.int32).reshape(1)
        chip = (2 * lax.axis_index("x") + lax.axis_index("y")).astype(jnp.int32)
        self.where = jnp.stack([chip, self.c[0]])
        self.result = None

    def swap(self):
        return _ride_swap_halves(self.gs)

    def to_owners(self, got):
        self.ps = [_add_halves(g, r, self.c, "grad_add_sibling_" + n) for g, r, n in zip(self.gs, got, ("in", "out"))]
        return _ride_to_owners(self.ps)

    def join(self, got):
        qs = [_add_owner(p, r, self.where, "grad_add_chips_" + n) for p, r, n in zip(self.ps, got, ("in", "out"))]
        return _ride_join_halves(qs)

    def finish(self, got):
        self.result = [a.reshape((a.shape[0] * a.shape[1],) + a.shape[2:]) for a in got]


def _adamw_math(w, g, m, v):
    m = ADAM_B1 * m + (1.0 - ADAM_B1) * g
    v = ADAM_B2 * v + (1.0 - ADAM_B2) * (g * g)
    m_hat = m / (1.0 - ADAM_B1 ** ADAM_STEP)
    v_hat = v / (1.0 - ADAM_B2 ** ADAM_STEP)
    delta = -ADAM_LR * (m_hat / (jnp.sqrt(v_hat) + ADAM_EPS) + ADAM_WD * w)
    return delta, m, v


def _adamw_small(w, g, m, v, name):
    def body(w_ref, g_ref, m_ref, v_ref, d_out, m_out, v_out):
        d_out[...], m_out[...], v_out[...] = _adamw_math(w_ref[...], g_ref[...], m_ref[...], v_ref[...])

    shape = jax.ShapeDtypeStruct(w.shape, F32)
    return pl.pallas_call(body, name="adamw_" + name, out_shape=[shape, shape, shape],
                          compiler_params=_params())(w, g, m, v)


def _adamw_layer(i, w, g, m, v, prev, name):
    depth, rows, cols = w.shape
    tr = _row_tile(rows, 256)
    n_prev = 0 if prev is None else 4

    def body(*refs):
        w_ref, g_ref, m_ref, v_ref = refs[:4]
        g_out, d_out, m_out, v_out = refs[4 + n_prev:]
        gv = g_ref[...]
        g_out[0] = gv
        d_out[0], m_out[0], v_out[0] = _adamw_math(w_ref[0], gv, m_ref[0], v_ref[0])

    lay = pl.BlockSpec((1, tr, cols), lambda r: (i, r, 0))
    shape = jax.ShapeDtypeStruct(w.shape, F32)
    return pl.pallas_call(
        body, name="adamw_" + name, grid=(rows // tr,),
        in_specs=[lay, pl.BlockSpec((tr, cols), lambda r: (r, 0)), lay, lay] + [ANY] * n_prev,
        out_specs=[lay] * 4, out_shape=[shape] * 4,
        input_output_aliases={4 + k: k for k in range(n_prev)},
        compiler_params=_params(("parallel",)),
    )(w, g, m, v, *(prev or ()))


def _adamw_cols_major(w, gs, m, v, name):
    depth, rows, cols = w.shape
    tr = max(t for t in range(1, 129) if cols % t == 0)
    wt, mt, vt = (jnp.transpose(a, (2, 0, 1)) for a in (w, m, v))
    gt = jnp.stack([g.T for g in gs], axis=1)

    def body(w_ref, g_ref, m_ref, v_ref, g_out, d_out, m_out, v_out):
        gv = g_ref[...]
        g_out[...] = gv
        d_out[...], m_out[...], v_out[...] = _adamw_math(w_ref[...], gv, m_ref[...], v_ref[...])

    spec = pl.BlockSpec((tr, depth, rows), lambda r: (r, 0, 0))
    shape = jax.ShapeDtypeStruct((cols, depth, rows), F32)
    outs = pl.pallas_call(body, name="adamw_" + name, grid=(cols // tr,), in_specs=[spec] * 4, out_specs=[spec] * 4,
                          out_shape=[shape] * 4, compiler_params=_params(("parallel",)))(wt, gt, mt, vt)
    return [jnp.transpose(a, (1, 2, 0)) for a in outs]


def _sum_leading(buf, name):
    n, rows, cols = buf.shape
    tr = _row_tile(rows, rows)

    def body(in_ref, out_ref):
        acc = in_ref[0]
        for k in range(1, n):
            acc = acc + in_ref[k]
        out_ref[...] = acc

    return pl.pallas_call(
        body, name=name, grid=(rows // tr,),
        in_specs=[pl.BlockSpec((n, tr, cols), lambda i: (0, i, 0))],
        out_specs=pl.BlockSpec((tr, cols), lambda i: (i, 0)),
        out_shape=jax.ShapeDtypeStruct((rows, cols), F32),
        compiler_params=_params(("parallel",)),
    )(buf)


_SHARDED_SMALL = ("meta", "conv_a_w", "ssm_conv_w", "conf_conv_w")
_LAYER_SMALL = ("pre_g", "post_g", "conv_a_w", "ssm_conv_w", "ssm_conv_b", "dt_bias", "a_log", "d_skip",
                "ssm_norm_g", "conf_conv_w", "conf_conv_b", "conf_ln_g", "conf_ln_b")
_WEIGHTS = ("meta", "pre_g", "post_g", "w_in", "w_out", "conv_a_w", "ssm_conv_w", "ssm_conv_b", "dt_bias", "a_log",
            "d_skip", "ssm_norm_g", "conf_conv_w", "conf_conv_b", "conf_ln_g", "conf_ln_b")


def _shard_last(a):
    return jnp.moveaxis(a.reshape(a.shape[:-1] + (N_CHIPS, a.shape[-1] // N_CHIPS)), -2, 0)


def _with_own_block(a, n, at):
    return lax.dynamic_update_index_in_dim(jnp.zeros((n,) + a.shape, a.dtype), a, at, 0)


def _with_own_columns(a, chip):
    k = a.shape[-1]
    return lax.dynamic_update_slice_in_dim(jnp.zeros(a.shape[:-1] + (N_CHIPS * k,), a.dtype), a, chip * k, a.ndim - 1)


def kernel(x, meta, pre_g, post_g, w_in, w_out, conv_a_w, ssm_conv_w, ssm_conv_b, dt_bias, a_log, d_skip, ssm_norm_g, conf_conv_w, conf_conv_b, conf_ln_g, conf_ln_b, loss_target, m_meta, m_pre_g, m_post_g, m_w_in, m_w_out, m_conv_a_w, m_ssm_conv_w, m_ssm_conv_b, m_dt_bias, m_a_log, m_d_skip, m_ssm_norm_g, m_conf_conv_w, m_conf_conv_b, m_conf_ln_g, m_conf_ln_b, v_meta, v_pre_g, v_post_g, v_w_in, v_w_out, v_conv_a_w, v_ssm_conv_w, v_ssm_conv_b, v_dt_bias, v_a_log, v_d_skip, v_ssm_norm_g, v_conf_conv_w, v_conf_conv_b, v_conf_ln_g, v_conf_ln_b):
    w = dict(meta=meta, pre_g=pre_g, post_g=post_g, w_in=w_in, w_out=w_out, conv_a_w=conv_a_w,
             ssm_conv_w=ssm_conv_w, ssm_conv_b=ssm_conv_b, dt_bias=dt_bias, a_log=a_log, d_skip=d_skip,
             ssm_norm_g=ssm_norm_g, conf_conv_w=conf_conv_w, conf_conv_b=conf_conv_b, conf_ln_g=conf_ln_g,
             conf_ln_b=conf_ln_b)
    mom = dict(meta=m_meta, pre_g=m_pre_g, post_g=m_post_g, w_in=m_w_in, w_out=m_w_out, conv_a_w=m_conv_a_w,
               ssm_conv_w=m_ssm_conv_w, ssm_conv_b=m_ssm_conv_b, dt_bias=m_dt_bias, a_log=m_a_log, d_skip=m_d_skip,
               ssm_norm_g=m_ssm_norm_g, conf_conv_w=m_conf_conv_w, conf_conv_b=m_conf_conv_b,
               conf_ln_g=m_conf_ln_g, conf_ln_b=m_conf_ln_b)
    vel = dict(meta=v_meta, pre_g=v_pre_g, post_g=v_post_g, w_in=v_w_in, w_out=v_w_out, conv_a_w=v_conv_a_w,
               ssm_conv_w=v_ssm_conv_w, ssm_conv_b=v_ssm_conv_b, dt_bias=v_dt_bias, a_log=v_a_log, d_skip=v_d_skip,
               ssm_norm_g=v_ssm_norm_g, conf_conv_w=v_conf_conv_w, conf_conv_b=v_conf_conv_b,
               conf_ln_g=v_conf_ln_g, conf_ln_b=v_conf_ln_b)
    bl, seq, d = x.shape
    dm = Dims(bl, seq, d)
    depth = w_in.shape[0]
    chip = (2 * lax.axis_index("x") + lax.axis_index("y")).astype(jnp.int32)
    dev = 2 * chip + lax.axis_index("c").astype(jnp.int32)
    cst = _head_consts(dm)

    bases = [[_place_own(w_in, i, chip.reshape(1)), _place_own(w_out, i, chip.reshape(1))] for i in range(depth)]
    first_in, small_w = _gather_ici_relayed(
        [bases[0][0]], _ride_gather_small([_with_own_columns(w[n], chip) for n in _SHARDED_SMALL]))
    full = dict(w)
    full.update(zip(_SHARDED_SMALL, small_w))
    h, gathered = _embed(x, full["meta"], dm, _ride_gather_d2d(first_in))
    saved, proj_w = [], []
    for i in range(depth):
        lw = {n: full[n][i] for n in _LAYER_SMALL}
        w_in_i = _w_in_from_shards(gathered[0], dm)
        h, keep, gathered, w_out_i = _layer_fwd(
            h, lw, w_in_i, None if i == 0 else gathered[1].reshape(2 * d, d), cst, dm,
            bases[i + 1] if i + 1 < depth else None, bases[0][1] if i == 0 else None)
        proj_w.append((w_in_i, w_out_i))
        saved.append(keep)

    dh, loss = _loss_head(h, loss_target, dm)
    loss = lax.psum(loss, ("x", "y", "c"))

    small_g = {n: [None] * depth for n in _LAYER_SMALL}
    big = {"w_in": None, "w_out": None}
    g_in = [None] * depth
    reduce = None
    for i in reversed(range(depth)):
        lw = {n: full[n][i] for n in _LAYER_SMALL}
        dh, mine, sg = _layer_bwd(dh, saved[i], lw, proj_w[i][0], proj_w[i][1], cst, dm, reduce, last=i == 0)
        for n in _LAYER_SMALL:
            small_g[n][i] = sg[n]
        if reduce is not None:
            g_in[i + 1] = reduce.result[0]
            big["w_out"] = _adamw_layer(i + 1, w_out, reduce.result[1], m_w_out, v_w_out, big["w_out"], "w_out")
        reduce = mine
    g_in[0] = reduce.result[0]
    big["w_out"] = _adamw_layer(0, w_out, reduce.result[1], m_w_out, v_w_out, big["w_out"], "w_out")
    grad_x, gmeta = _unembed(dh, dm)

    g = {n: jnp.stack(v) for n, v in small_g.items()}
    g["meta"] = gmeta
    small = [n for n in _WEIGHTS if n not in ("w_in", "w_out")]
    flat = jnp.concatenate([g[n].reshape(-1) for n in small])
    rows = -(-flat.shape[0] // (16 * LANES)) * 16
    flat = jnp.pad(flat, (0, rows * LANES - flat.shape[0])).reshape(rows, LANES)
    parts = _gather_all(_with_own_block(flat, N_DEV, dev))
    total = _sum_leading(parts, "small_grads_sum").reshape(-1)
    big["w_in"] = _adamw_cols_major(w_in, g_in, m_w_in, v_w_in, "w_in")
    grads, deltas, new_m, new_v = {}, {}, {}, {}
    off = 0
    for n in small:
        size = g[n].size
        fullg = total[off:off + size].reshape(g[n].shape)
        off += size
        if n in _SHARDED_SMALL:
            fullg = lax.dynamic_index_in_dim(_shard_last(fullg), chip, axis=0, keepdims=False)
        grads[n] = fullg
        deltas[n], new_m[n], new_v[n] = _adamw_small(w[n], fullg, mom[n], vel[n], n)
    for n in ("w_in", "w_out"):
        grads[n], deltas[n], new_m[n], new_v[n] = big[n]

    return (loss, grad_x, *[grads[n] for n in _WEIGHTS], *[deltas[n] for n in _WEIGHTS],
            *[new_m[n] for n in _WEIGHTS], *[new_v[n] for n in _WEIGHTS])
```

```python
import jax
import jax.numpy as jnp
from jax import lax
from jax.experimental import pallas as pl
from jax.experimental.pallas import tpu as pltpu

F32 = jnp.float32
BF16 = jnp.bfloat16

N_META = 16
TT = 128
SSM_STATE = 128
SSM_GROUPS = 2
SSM_HEAD_DIM = 64
CONV_A_K = 3
SSM_CONV_K = 4
CONF_K = 31
NORM_EPS = 1e-6
LN_EPS = 1e-5
LANES = 128
MXU_DIM = 256
DT_PAD = LANES
CONF_HALO = 32
SMALL_HALO = 8
VMEM_LIMIT = 56 * 1024 * 1024
N_CHIPS = 4
N_DEV = 8

ADAM_LR = 0.001
ADAM_B1 = 0.9
ADAM_B2 = 0.999
ADAM_EPS = 1e-08
ADAM_WD = 0.01
ADAM_STEP = 10

MESH = pl.DeviceIdType.MESH
ANY = pl.BlockSpec(memory_space=pl.ANY)


class Dims:
    def __init__(self, bl, seq, d):
        self.BL, self.S, self.D = bl, seq, d
        self.L = seq + N_META
        self.Lp = -(-self.L // TT) * TT
        self.NT = self.Lp // TT
        self.R = bl * self.Lp
        self.DA = d // 2
        self.DB = d
        self.DC = d // 2
        self.H = self.DB // SSM_HEAD_DIM
        self.HPG = self.H // SSM_GROUPS
        self.GN = SSM_GROUPS * SSM_STATE
        self.WA = 4 * self.DA
        self.WB = 2 * self.DB + 2 * self.GN
        self.WC = 3 * self.DC
        self.DT0 = self.WA + self.WB + self.WC
        self.NP = -(-(self.DT0 + DT_PAD) // (5 * MXU_DIM)) * (5 * MXU_DIM)
        self.NIN = self.WA + self.WB + self.H + self.WC
        self.XBC = self.DB + 2 * self.GN
        assert self.H % 2 == 0 and self.HPG % 2 == 0 and self.H <= DT_PAD
        assert self.DA % LANES == 0 and (self.WA + self.WB) % self.DC == 0 and self.WA % self.DB == 0


def _row_tile(n, target):
    best = None
    for t in range(16, min(n, target) + 1, 16):
        if n % t == 0:
            best = t
    assert best is not None
    return best


def _col_tile(n, target):
    best = None
    for t in range(LANES, min(n, target) + 1, LANES):
        if n % t == 0:
            best = t
    assert best is not None
    return best


def _params(sem=None):
    return pltpu.CompilerParams(dimension_semantics=sem, vmem_limit_bytes=VMEM_LIMIT)


def _sigmoid(x):
    return 1.0 / (1.0 + jnp.exp(-x))


def _silu_and_grad(x):
    s = _sigmoid(x)
    y = x * s
    return y, s + y * (1.0 - s)


def _dot(a, b):
    return jnp.dot(a, b, preferred_element_type=F32)


def _dot_nt(a, b):
    return lax.dot_general(a, b, (((1,), (1,)), ((), ())), preferred_element_type=F32)


def _dot_tn(a, b):
    return lax.dot_general(a, b, (((0,), (0,)), ((), ())), preferred_element_type=F32)


def _split3(x):
    x1 = x.astype(BF16)
    r1 = x - x1.astype(F32)
    x2 = r1.astype(BF16)
    x3 = (r1 - x2.astype(F32)).astype(BF16)
    return x1, x2, x3


class Rider:
    def __init__(self, plan, ins, out_shapes, aliases, nsem):
        self.plan, self.ins, self.out_shapes, self.aliases, self.nsem = plan, list(ins), list(out_shapes), aliases, nsem


def _place():
    x, y, c = lax.axis_index("x"), lax.axis_index("y"), lax.axis_index("c")
    chips = [(1 - x, y), (x, 1 - y), (1 - x, 1 - y)]
    return x, y, c, chips


def _remote(k, src, dst, to, send_sems, recv_sems):
    return pltpu.make_async_remote_copy(src_ref=src, dst_ref=dst, send_sem=send_sems.at[k], recv_sem=recv_sems.at[k],
                                        device_id=to, device_id_type=MESH)


def _call(body, name, grid, in_specs, out_specs, out_shape, scratch_shapes, sem, args, rider=None):
    if rider is None:
        outs = pl.pallas_call(body, name=name, grid=grid, in_specs=in_specs, out_specs=out_specs, out_shape=out_shape,
                              scratch_shapes=scratch_shapes, compiler_params=_params(sem))(*args)
        return list(outs), []
    n_in, n_out, n_scr = len(args), len(out_shape), len(scratch_shapes)
    r_in, r_out = len(rider.ins), len(rider.out_shapes)

    def hosted(*refs):
        ins, rins = refs[:n_in], refs[n_in:n_in + r_in]
        o0 = n_in + r_in
        outs, routs = refs[o0:o0 + n_out], refs[o0 + n_out:o0 + n_out + r_out]
        scr = refs[o0 + n_out + r_out:o0 + n_out + r_out + n_scr]
        send_sems, recv_sems = refs[o0 + n_out + r_out + n_scr:]
        first = pl.program_id(0) == 0
        last = pl.program_id(0) == grid[0] - 1
        for ax in range(1, len(grid)):
            first = jnp.logical_and(first, pl.program_id(ax) == 0)
            last = jnp.logical_and(last, pl.program_id(ax) == grid[ax] - 1)

        @pl.when(first)
        def _():
            starts, _ = rider.plan(rins, routs, send_sems, recv_sems)
            for cp in starts:
                cp.start()

        body(*ins, *outs, *scr)

        @pl.when(last)
        def _():
            _, waits = rider.plan(rins, routs, send_sems, recv_sems)
            for wait in waits:
                wait()

    res = pl.pallas_call(
        hosted, name=name, grid=grid,
        in_specs=list(in_specs) + [ANY] * r_in, out_specs=list(out_specs) + [ANY] * r_out,
        out_shape=list(out_shape) + rider.out_shapes,
        input_output_aliases={n_in + k: n_out + v for k, v in rider.aliases.items()},
        scratch_shapes=list(scratch_shapes) + [pltpu.SemaphoreType.DMA((rider.nsem,)),
                                               pltpu.SemaphoreType.DMA((rider.nsem,))],
        compiler_params=_params(("arbitrary",) * len(grid)),
    )(*args, *rider.ins)
    return list(res[:n_out]), list(res[n_out:])


def _exchange(name, rider):
    r_in, r_out = len(rider.ins), len(rider.out_shapes)

    def body(*refs):
        rins, routs = refs[:r_in], refs[r_in:r_in + r_out]
        send_sems, recv_sems = refs[r_in + r_out:]
        starts, waits = rider.plan(rins, routs, send_sems, recv_sems)
        for cp in starts:
            cp.start()
        for wait in waits:
            wait()

    res = pl.pallas_call(
        body, name=name, in_specs=[ANY] * r_in, out_specs=[ANY] * r_out, out_shape=rider.out_shapes,
        input_output_aliases=dict(rider.aliases),
        scratch_shapes=[pltpu.SemaphoreType.DMA((rider.nsem,)), pltpu.SemaphoreType.DMA((rider.nsem,))],
    )(*rider.ins)
    return list(res)


def _same(arrays):
    return [jax.ShapeDtypeStruct(a.shape, a.dtype) for a in arrays]


class _SemsFrom:
    def __init__(self, sems, first):
        self.sems, self.first = sems, first

    @property
    def at(self):
        return self

    def __getitem__(self, k):
        return self.sems.at[self.first + k]


def _ride_both(r1, r2):
    n_in, n_out = len(r1.ins), len(r1.out_shapes)

    def plan(ins, outs, ss, rs):
        s1, w1 = r1.plan(ins[:n_in], outs[:n_out], ss, rs)
        s2, w2 = r2.plan(ins[n_in:], outs[n_out:], _SemsFrom(ss, r1.nsem), _SemsFrom(rs, r1.nsem))
        return s1 + s2, w1 + w2

    aliases = dict(r1.aliases)
    aliases.update({n_in + k: n_out + v for k, v in r2.aliases.items()})
    return Rider(plan, r1.ins + r2.ins, r1.out_shapes + r2.out_shapes, aliases, r1.nsem + r2.nsem)


def _ride_gather_ici(bases, part=0, nparts=1):
    n = len(bases)

    def plan(ins, outs, ss, rs):
        x, y, c, chips = _place()
        me = 2 * x + y
        starts, waits = [], []
        for a in range(n):
            half = outs[a].shape[1] // 2
            mine = pl.ds(c * half + part * (half // nparts), half // nparts)
            for j, chip in enumerate(chips):
                cp = _remote(3 * a + j, outs[a].at[me, mine], outs[a].at[me, mine], (*chip, c), ss, rs)
                got = outs[a].at[2 * chip[0] + chip[1], mine]
                starts.append(cp)
                waits += [cp.wait_send, _remote(3 * a + j, got, got, (*chip, c), ss, rs).wait_recv]
        return starts, waits

    return Rider(plan, bases, _same(bases), {a: a for a in range(n)}, 3 * n)


def _gather_ici_relayed(bases, also):
    n, m = len(bases), len(also.ins)

    def body(*refs):
        outs = refs[n + m:2 * n + m]
        ss, rs = refs[2 * (n + m):]
        beside, beside_waits = also.plan(refs[n:n + m], refs[2 * n + m:2 * (n + m)],
                                         _SemsFrom(ss, 4 * n), _SemsFrom(rs, 4 * n))
        for cp in beside:
            cp.start()
        x, y, c, _ = _place()
        me, xn, yn, dg = 2 * x + y, 2 * (1 - x) + y, 2 * x + (1 - y), 2 * (1 - x) + (1 - y)
        to_x, to_y = (1 - x, y, c), (x, 1 - y, c)
        sends = []

        def send(k, piece, to):
            cp = _remote(k, piece, piece, to, ss, rs)
            cp.start()
            sends.append(cp)

        def arrived(k, piece, frm):
            _remote(k, piece, piece, frm, ss, rs).wait_recv()

        rows = []
        for a in range(n):
            half = outs[a].shape[1] // 2
            rows.append((pl.ds(c * half, half), pl.ds(c * half, half // 2), pl.ds(c * half + half // 2, half // 2)))
            send(4 * a, outs[a].at[me, rows[a][0]], to_x)
            send(4 * a + 1, outs[a].at[me, rows[a][0]], to_y)
        for a in range(n):
            mine, lo, hi = rows[a]
            arrived(4 * a, outs[a].at[xn, mine], to_x)
            send(4 * a + 2, outs[a].at[xn, lo], to_y)
            arrived(4 * a + 1, outs[a].at[yn, mine], to_y)
            send(4 * a + 3, outs[a].at[yn, hi], to_x)
        for a in range(n):
            mine, lo, hi = rows[a]
            arrived(4 * a + 2, outs[a].at[dg, lo], to_y)
            arrived(4 * a + 3, outs[a].at[dg, hi], to_x)
        for cp in sends:
            cp.wait_send()
        for wait in beside_waits:
            wait()

    aliases = {a: a for a in range(n)}
    aliases.update({n + k: n + v for k, v in also.aliases.items()})
    nsem = 4 * n + also.nsem
    res = pl.pallas_call(
        body, name="gather_ici_first", in_specs=[ANY] * (n + m), out_specs=[ANY] * (n + len(also.out_shapes)),
        out_shape=_same(bases) + also.out_shapes, input_output_aliases=aliases,
        scratch_shapes=[pltpu.SemaphoreType.DMA((nsem,)), pltpu.SemaphoreType.DMA((nsem,))],
    )(*bases, *also.ins)
    return list(res[:n]), list(res[n:])


def _ride_gather_d2d(bases):
    n = len(bases)

    def plan(ins, outs, ss, rs):
        x, y, c, chips = _place()
        sib = (x, y, 1 - c)
        starts, waits = [], []
        for a in range(n):
            half = outs[a].shape[1] // 2
            for j, chip in enumerate(chips):
                frm = 2 * chip[0] + chip[1]
                got = outs[a].at[frm, pl.ds(c * half, half)]
                theirs = outs[a].at[frm, pl.ds((1 - c) * half, half)]
                cp = _remote(3 * a + j, got, got, sib, ss, rs)
                starts.append(cp)
                waits += [cp.wait_send, _remote(3 * a + j, theirs, theirs, sib, ss, rs).wait_recv]
        return starts, waits

    return Rider(plan, bases, _same(bases), {a: a for a in range(n)}, 3 * n)


def _ride_gather_small(bases):
    n = len(bases)

    def plan(ins, outs, ss, rs):
        x, y, c, chips = _place()
        me = 2 * x + y
        starts, waits = [], []
        for a in range(n):
            k = outs[a].shape[-1] // N_CHIPS
            lead = (slice(None),) * (len(outs[a].shape) - 1)
            at = (lambda s: pl.multiple_of(s * k, LANES)) if k % LANES == 0 else (lambda s: s * k)
            cols = lambda s: outs[a].at[lead + (pl.ds(at(s), k),)]
            for j, chip in enumerate(chips):
                cp = _remote(3 * a + j, cols(me), cols(me), (*chip, c), ss, rs)
                got = cols(2 * chip[0] + chip[1])
                starts.append(cp)
                waits += [cp.wait_send, _remote(3 * a + j, got, got, (*chip, c), ss, rs).wait_recv]
        return starts, waits

    return Rider(plan, bases, _same(bases), {a: a for a in range(n)}, 3 * n)


def _ride_swap_halves(gs):
    n = len(gs)

    def plan(ins, outs, ss, rs):
        x, y, c, _ = _place()
        cps = [_remote(a, ins[a].at[:, 1 - c], outs[a], (x, y, 1 - c), ss, rs) for a in range(n)]
        return cps, [cp.wait for cp in cps]

    shapes = [jax.ShapeDtypeStruct((g.shape[0],) + g.shape[2:], g.dtype) for g in gs]
    return Rider(plan, gs, shapes, {}, n)


def _ride_to_owners(ps):
    n = len(ps)

    def plan(ins, outs, ss, rs):
        x, y, c, chips = _place()
        cps = []
        for a in range(n):
            for j, chip in enumerate(chips):
                cps.append(_remote(3 * a + j, ins[a].at[2 * chip[0] + chip[1]], outs[a].at[j], (*chip, c), ss, rs))
        return cps, [cp.wait for cp in cps]

    shapes = [jax.ShapeDtypeStruct((3,) + p.shape[1:], p.dtype) for p in ps]
    return Rider(plan, ps, shapes, {}, 3 * n)


def _ride_join_halves(qs):
    n = len(qs)

    def plan(ins, outs, ss, rs):
        x, y, c, _ = _place()
        sib = (x, y, 1 - c)
        starts, waits = [], []
        for a in range(n):
            cp = _remote(a, outs[a].at[c], outs[a].at[c], sib, ss, rs)
            starts.append(cp)
            waits += [cp.wait_send, _remote(a, outs[a].at[1 - c], outs[a].at[1 - c], sib, ss, rs).wait_recv]
        return starts, waits

    return Rider(plan, qs, _same(qs), {a: a for a in range(n)}, n)


def _gather_all(base):
    def body(in_ref, out_ref, ss, rs):
        x, y, c, chips = _place()
        sib = (x, y, 1 - c)
        block = lambda cx, cy, cc: out_ref.at[4 * cx + 2 * cy + cc]
        mine = block(x, y, c)
        first = [_remote(j, mine, mine, (*chip, c), ss, rs) for j, chip in enumerate(chips)]
        first.append(_remote(3, mine, mine, sib, ss, rs))
        for cp in first:
            cp.start()
        passed = []
        for j, chip in enumerate(chips):
            got = block(*chip, c)
            _remote(j, got, got, (*chip, c), ss, rs).wait_recv()
            passed.append(_remote(4 + j, got, got, sib, ss, rs))
            passed[-1].start()
        theirs = block(x, y, 1 - c)
        _remote(3, theirs, theirs, sib, ss, rs).wait_recv()
        for j, chip in enumerate(chips):
            got = block(*chip, 1 - c)
            _remote(4 + j, got, got, sib, ss, rs).wait_recv()
        for cp in first + passed:
            cp.wait_send()

    return pl.pallas_call(
        body, name="small_grads_gather_all", in_specs=[ANY], out_specs=ANY,
        out_shape=jax.ShapeDtypeStruct(base.shape, base.dtype), input_output_aliases={0: 0},
        scratch_shapes=[pltpu.SemaphoreType.DMA((N_DEV - 1,)), pltpu.SemaphoreType.DMA((N_DEV - 1,))],
    )(base)


def _embed(x, meta, dm, rider=None):
    dc = _col_tile(dm.D, 256)
    s, lp = dm.S, dm.Lp

    def body(x_ref, meta_ref, h_ref):
        h_ref[0:N_META, :] = meta_ref[...]
        h_ref[N_META:N_META + s, :] = x_ref[0]
        if lp > N_META + s:
            h_ref[N_META + s:lp, :] = jnp.zeros((lp - N_META - s, dc), F32)

    (h,), rode = _call(
        body, "embed", (dm.BL, dm.D // dc),
        [pl.BlockSpec((1, s, dc), lambda b, j: (b, 0, j)), pl.BlockSpec((N_META, dc), lambda b, j: (0, j))],
        [pl.BlockSpec((lp, dc), lambda b, j: (b, j))], [jax.ShapeDtypeStruct((dm.R, dm.D), F32)],
        [], ("parallel", "parallel"), (x, meta), rider)
    return h, rode


def _loss_head(h, target, dm):
    dc = _col_tile(dm.D, 256)
    s, lp, nj = dm.S, dm.Lp, dm.D // dc

    def body(h_ref, t_ref, dh_ref, l_ref):
        diff = h_ref[N_META:N_META + s, :] - t_ref[0]
        dh_ref[0:N_META, :] = jnp.zeros((N_META, dc), F32)
        dh_ref[N_META:N_META + s, :] = diff * (1.0 / dm.D)
        if lp > N_META + s:
            dh_ref[N_META + s:lp, :] = jnp.zeros((lp - N_META - s, dc), F32)
        l_ref[...] = jnp.full((8, LANES), (0.5 / dm.D) * jnp.sum(diff * diff), F32)

    dh, part = pl.pallas_call(
        body, name="loss_head", grid=(dm.BL, nj),
        in_specs=[pl.BlockSpec((lp, dc), lambda b, j: (b, j)),
                  pl.BlockSpec((1, s, dc), lambda b, j: (b, 0, j))],
        out_specs=[pl.BlockSpec((lp, dc), lambda b, j: (b, j)),
                   pl.BlockSpec((8, LANES), lambda b, j: (b * nj + j, 0))],
        out_shape=[jax.ShapeDtypeStruct((dm.R, dm.D), F32),
                   jax.ShapeDtypeStruct((dm.BL * nj * 8, LANES), F32)],
        compiler_params=_params(("parallel", "parallel")),
    )(h, target)
    return dh, jnp.sum(part[::8, 0])


def _unembed(dh, dm):
    dc = _col_tile(dm.D, 256)
    s, lp = dm.S, dm.Lp

    def body(dh_ref, gx_ref, gm_ref):
        gx_ref[0] = dh_ref[N_META:N_META + s, :]

        @pl.when(pl.program_id(1) == 0)
        def _():
            gm_ref[...] = dh_ref[0:N_META, :]

        @pl.when(pl.program_id(1) > 0)
        def _():
            gm_ref[...] = gm_ref[...] + dh_ref[0:N_META, :]

    return pl.pallas_call(
        body, name="unembed", grid=(dm.D // dc, dm.BL),
        in_specs=[pl.BlockSpec((lp, dc), lambda j, b: (b, j))],
        out_specs=[pl.BlockSpec((1, s, dc), lambda j, b: (b, 0, j)),
                   pl.BlockSpec((N_META, dc), lambda j, b: (0, j))],
        out_shape=[jax.ShapeDtypeStruct((dm.BL, s, dm.D), F32),
                   jax.ShapeDtypeStruct((N_META, dm.D), F32)],
        compiler_params=_params(("parallel", "arbitrary")),
    )(dh)


def _fwd_in(h, pre_g, w, dm, rider=None):
    tm = _row_tile(dm.R, 1088)
    tn = _col_tile(dm.NP, 5 * MXU_DIM)
    nj = dm.NP // tn

    def body(h_ref, g_ref, w_ref, wdt_ref, proj_ref, dt_ref, hn_ref):
        @pl.when(pl.program_id(1) == 0)
        def _():
            xf = h_ref[...]
            r = lax.rsqrt(jnp.mean(xf * xf, axis=-1, keepdims=True) + NORM_EPS)
            hn_ref[...] = (xf * r * g_ref[...]).astype(BF16)
            dt_ref[...] = _dot(hn_ref[...], wdt_ref[...])

        proj_ref[...] = _dot(hn_ref[...], w_ref[...]).astype(BF16)

    return _call(
        body, "fwd_in", (dm.R // tm, nj),
        [pl.BlockSpec((tm, dm.D), lambda i, j: (i, 0)),
         pl.BlockSpec((1, dm.D), lambda i, j: (0, 0)),
         pl.BlockSpec((dm.D, tn), lambda i, j: (0, j)),
         pl.BlockSpec((dm.D, DT_PAD), lambda i, j: (0, dm.DT0 // DT_PAD))],
        [pl.BlockSpec((tm, tn), lambda i, j: (i, j)),
         pl.BlockSpec((tm, DT_PAD), lambda i, j: (i, 0)),
         pl.BlockSpec((tm, dm.D), lambda i, j: (i, 0))],
        [jax.ShapeDtypeStruct((dm.R, dm.NP), BF16), jax.ShapeDtypeStruct((dm.R, DT_PAD), F32),
         jax.ShapeDtypeStruct((dm.R, dm.D), BF16)],
        [], ("parallel", "arbitrary"), (h, pre_g, w, w), rider)


def _fwd_out(ya, yb, yc, w_out, h, post_g, dm, rider=None):
    tm = _row_tile(dm.Lp, 544)
    tiles_per_seq = dm.Lp // tm
    da, db, dc = dm.DA, dm.DB, dm.DC

    def body(ya_ref, yb_ref, yc_ref, w_ref, h_ref, g_ref, hn_ref, m_ref):
        m = _dot(ya_ref[...], w_ref[0:da, :])
        m = m + _dot(yb_ref[...], w_ref[da:da + db, :])
        m = m + _dot(yc_ref[...], w_ref[da + db:da + db + dc, :])
        m_ref[...] = m
        r = lax.rsqrt(jnp.mean(m * m, axis=-1, keepdims=True) + NORM_EPS)
        t = (pl.program_id(0) % tiles_per_seq) * tm + lax.broadcasted_iota(jnp.int32, (tm, 1), 0)
        keep = (t < dm.L).astype(F32)
        hn_ref[...] = (h_ref[...] + m * r * g_ref[...]) * keep

    row = lambda i: (i, 0)
    fixed = lambda i: (0, 0)
    return _call(
        body, "fwd_out", (dm.R // tm,),
        [pl.BlockSpec((tm, da), row), pl.BlockSpec((tm, db), row), pl.BlockSpec((tm, dc), row),
         pl.BlockSpec((2 * dm.D, dm.D), fixed), pl.BlockSpec((tm, dm.D), row), pl.BlockSpec((1, dm.D), fixed)],
        [pl.BlockSpec((tm, dm.D), row), pl.BlockSpec((tm, dm.D), row)],
        [jax.ShapeDtypeStruct((dm.R, dm.D), F32), jax.ShapeDtypeStruct((dm.R, dm.D), F32)],
        [], ("parallel",), (ya, yb, yc, w_out, h, post_g), rider)


def _bwd_out(dh, m, post_g, w_out, ya, yb, yc, dm, rider=None):
    tm = _row_tile(dm.R, MXU_DIM)
    da, db, dc = dm.DA, dm.DB, dm.DC

    def body(dh_ref, m_ref, g_ref, w_ref, ya_ref, yb_ref, yc_ref, dya_ref, dyb_ref, dyc_ref, dw_ref, dg_ref):
        @pl.when(pl.program_id(0) == 0)
        def _():
            dw_ref[...] = jnp.zeros_like(dw_ref)
            dg_ref[...] = jnp.zeros_like(dg_ref)

        m = m_ref[...]
        dh_ = dh_ref[...]
        r = lax.rsqrt(jnp.mean(m * m, axis=-1, keepdims=True) + NORM_EPS)
        n = m * r
        dg_ref[0:1, :] = dg_ref[0:1, :] + jnp.sum(dh_ * n, axis=0, keepdims=True)
        dn = dh_ * g_ref[...]
        dm_ = (r * (dn - n * jnp.mean(dn * n, axis=-1, keepdims=True))).astype(BF16)
        dya_ref[...] = _dot_nt(dm_, w_ref[0:da, :])
        dyb_ref[...] = _dot_nt(dm_, w_ref[da:da + db, :])
        dyc_ref[...] = _dot_nt(dm_, w_ref[da + db:da + db + dc, :])
        dw_ref[0:da, :] = dw_ref[0:da, :] + _dot_tn(ya_ref[...], dm_)
        dw_ref[da:da + db, :] = dw_ref[da:da + db, :] + _dot_tn(yb_ref[...], dm_)
        dw_ref[da + db:da + db + dc, :] = dw_ref[da + db:da + db + dc, :] + _dot_tn(yc_ref[...], dm_)

    row = lambda i: (i, 0)
    fixed = lambda i: (0, 0)
    return _call(
        body, "bwd_out", (dm.R // tm,),
        [pl.BlockSpec((tm, dm.D), row), pl.BlockSpec((tm, dm.D), row), pl.BlockSpec((1, dm.D), fixed),
         pl.BlockSpec((2 * dm.D, dm.D), fixed),
         pl.BlockSpec((tm, da), row), pl.BlockSpec((tm, db), row), pl.BlockSpec((tm, dc), row)],
        [pl.BlockSpec((tm, da), row), pl.BlockSpec((tm, db), row), pl.BlockSpec((tm, dc), row),
         pl.BlockSpec((2 * dm.D, dm.D), fixed), pl.BlockSpec((8, dm.D), fixed)],
        [jax.ShapeDtypeStruct((dm.R, da), F32), jax.ShapeDtypeStruct((dm.R, db), F32),
         jax.ShapeDtypeStruct((dm.R, dc), F32),
         jax.ShapeDtypeStruct((2 * dm.D, dm.D), F32), jax.ShapeDtypeStruct((8, dm.D), F32)],
        [], ("arbitrary",), (dh, m, post_g, w_out, ya, yb, yc), rider)


def _bwd_in_dx(dpa, dpb, dpc, dpt, w, h, dh, pre_g, dm, rider=None):
    tm = _row_tile(dm.R, 272)
    wa, wb, wc = dm.WA, dm.WB, dm.WC

    def body(dpa_ref, dpb_ref, dpc_ref, dpt_ref, w_ref, h_ref, dh_ref, g_ref, out_ref, dg_ref):
        @pl.when(pl.program_id(0) == 0)
        def _():
            dg_ref[...] = jnp.zeros_like(dg_ref)

        dhn = _dot_nt(dpa_ref[...], w_ref[:, 0:wa])
        dhn = dhn + _dot_nt(dpb_ref[...], w_ref[:, wa:wa + wb])
        dhn = dhn + _dot_nt(dpc_ref[...], w_ref[:, wa + wb:wa + wb + wc])
        dhn = dhn + _dot_nt(dpt_ref[...], w_ref[:, wa + wb + wc:wa + wb + wc + DT_PAD])
        xf = h_ref[...]
        r = lax.rsqrt(jnp.mean(xf * xf, axis=-1, keepdims=True) + NORM_EPS)
        n = xf * r
        dg_ref[0:1, :] = dg_ref[0:1, :] + jnp.sum(dhn * n, axis=0, keepdims=True)
        dn = dhn * g_ref[...]
        out_ref[...] = dh_ref[...] + r * (dn - n * jnp.mean(dn * n, axis=-1, keepdims=True))

    row = lambda i: (i, 0)
    fixed = lambda i: (0, 0)
    return _call(
        body, "bwd_in_dx", (dm.R // tm,),
        [pl.BlockSpec((tm, wa), row), pl.BlockSpec((tm, wb), row), pl.BlockSpec((tm, wc), row),
         pl.BlockSpec((tm, DT_PAD), row), pl.BlockSpec((dm.D, dm.NP), fixed),
         pl.BlockSpec((tm, dm.D), row), pl.BlockSpec((tm, dm.D), row), pl.BlockSpec((1, dm.D), fixed)],
        [pl.BlockSpec((tm, dm.D), row), pl.BlockSpec((8, dm.D), fixed)],
        [jax.ShapeDtypeStruct((dm.R, dm.D), F32), jax.ShapeDtypeStruct((8, dm.D), F32)],
        [], ("arbitrary",), (dpa, dpb, dpc, dpt, w, h, dh, pre_g), rider)


def _bwd_in_dw(hn, dps, dm):
    widest = max(dp.shape[1] for dp in dps)
    tn = [_col_tile(dp.shape[1], 2 * MXU_DIM if dp.shape[1] == widest else MXU_DIM) for dp in dps]
    nb = [dp.shape[1] // t for dp, t in zip(dps, tn)]
    first = [sum(nb[:p]) for p in range(len(dps))]
    at = lambda p: (lambda j: (0, jnp.clip(j - first[p], 0, nb[p] - 1)))

    def body(hn_ref, *refs):
        j = pl.program_id(0)
        for p in range(len(dps)):
            @pl.when(jnp.logical_and(j >= first[p], j < first[p] + nb[p]))
            def _(p=p):
                refs[len(dps) + p][...] = _dot_tn(hn_ref[...], refs[p][...])

    return pl.pallas_call(
        body, name="bwd_in_dw", grid=(sum(nb),),
        in_specs=[pl.BlockSpec((dm.R, dm.D), lambda j: (0, 0))] + [
            pl.BlockSpec((dm.R, tn[p]), at(p)) for p in range(len(dps))],
        out_specs=[pl.BlockSpec((dm.D, tn[p]), at(p)) for p in range(len(dps))],
        out_shape=[jax.ShapeDtypeStruct((dm.D, dp.shape[1]), F32) for dp in dps],
        compiler_params=_params(("arbitrary",)),
    )(hn, *dps)


def _tile_index(dm, reverse):
    if reverse:
        return lambda b, i: b * dm.NT + (dm.NT - 1 - i)
    return lambda b, i: b * dm.NT + i


def _halo_index(dm, rows):
    per_tile = TT // rows
    return lambda b, i: jnp.maximum((b * dm.NT + (dm.NT - 1 - i)) * per_tile - 1, 0)


HALO_BLOCK = 16


def _last_rows(x):
    return x.astype(F32)[HALO_BLOCK - SMALL_HALO:HALO_BLOCK]


MIX_A_ROWS = 288


def _mix_a_fwd(proj, conv_w, dm):
    da = dm.DA
    ta = _row_tile(dm.Lp, MIX_A_ROWS)
    nta = dm.Lp // ta
    bl = dm.BL

    def setup(ab_ref, ac_ref, ax_ref, az_ref, w_ref, y_ref, pbuf):
        i = pl.program_id(0)

        @pl.when(i == 0)
        def _():
            pbuf[0:SMALL_HALO, :] = jnp.zeros((SMALL_HALO, da), F32)

        @pl.when(i > 0)
        def _():
            pbuf[0:SMALL_HALO, :] = pbuf[ta:ta + SMALL_HALO, :]

    def body(ab_ref, ac_ref, ax_ref, az_ref, w_ref, y_ref, pbuf):
        for lb in range(da // LANES):
            cs = slice(lb * LANES, (lb + 1) * LANES)
            p = ac_ref[:, cs].astype(F32) * ax_ref[:, cs].astype(F32)
            pbuf[SMALL_HALO:SMALL_HALO + ta, cs] = p
            q = (w_ref[0:1, cs] * pbuf[6:6 + ta, cs] + w_ref[1:2, cs] * pbuf[7:7 + ta, cs] + w_ref[2:3, cs] * p)
            az = az_ref[:, cs].astype(F32)
            y_ref[:, cs] = (ab_ref[:, cs].astype(F32) * q * (az * _sigmoid(az))).astype(BF16)
            yield

    proj3 = proj.reshape(bl, dm.Lp, dm.NP)
    col = lambda k: pl.BlockSpec((bl, ta, da), lambda i: (0, i, k))
    return pl.pallas_call(
        _per_sequence(setup, body, bl, [True] * 4 + [False] + [True, True]), name="mix_a_fwd", grid=(nta,),
        in_specs=[col(0), col(1), col(2), col(3), pl.BlockSpec((CONV_A_K, da), lambda i: (0, 0))],
        out_specs=col(0),
        out_shape=jax.ShapeDtypeStruct((bl, dm.Lp, da), BF16),
        scratch_shapes=[pltpu.VMEM((bl, SMALL_HALO + ta, da), F32)],
        compiler_params=_params(("arbitrary",)),
    )(proj3, proj3, proj3, proj3, conv_w).reshape(dm.R, da)


def _mix_a_bwd(proj, dya, conv_w, dm):
    da = dm.DA
    ta = _row_tile(dm.Lp, MIX_A_ROWS)
    nta = dm.Lp // ta
    bl = dm.BL

    def setup(ab_ref, ac_ref, ax_ref, az_ref, ach_ref, axh_ref, dy_ref, w_ref, dp_ref, dw_ref, pbuf, dqbuf):
        i = pl.program_id(0)

        @pl.when(i == 0)
        def _():
            dw_ref[...] = jnp.zeros_like(dw_ref)
            dqbuf[ta:ta + SMALL_HALO, :] = jnp.zeros((SMALL_HALO, da), F32)

        @pl.when(i > 0)
        def _():
            dqbuf[ta:ta + SMALL_HALO, :] = dqbuf[0:SMALL_HALO, :]

    def body(ab_ref, ac_ref, ax_ref, az_ref, ach_ref, axh_ref, dy_ref, w_ref, dp_ref, dw_ref, pbuf, dqbuf):
        halo_on = jnp.where(pl.program_id(0) == nta - 1, 0.0, 1.0)
        for lb in range(da // LANES):
            cs = slice(lb * LANES, (lb + 1) * LANES)
            pbuf[0:SMALL_HALO, cs] = (_last_rows(ach_ref[:, cs]) * _last_rows(axh_ref[:, cs])) * halo_on
            ac, ax, ab, az = (r[:, cs].astype(F32) for r in (ac_ref, ax_ref, ab_ref, az_ref))
            p = ac * ax
            pbuf[SMALL_HALO:SMALL_HALO + ta, cs] = p
            p1 = pbuf[7:7 + ta, cs]
            p2 = pbuf[6:6 + ta, cs]
            w0, w1, w2 = w_ref[0:1, cs], w_ref[1:2, cs], w_ref[2:3, cs]
            q = w0 * p2 + w1 * p1 + w2 * p
            sz, dsz = _silu_and_grad(az)
            dy = dy_ref[:, cs]
            t1 = dy * ab
            dq = t1 * sz
            dqbuf[0:ta, cs] = dq
            dpv = w2 * dq + w1 * dqbuf[1:1 + ta, cs] + w0 * dqbuf[2:2 + ta, cs]
            dp_ref[:, lb * LANES:(lb + 1) * LANES] = (dy * q * sz).astype(BF16)
            dp_ref[:, da + lb * LANES:da + (lb + 1) * LANES] = (dpv * ax).astype(BF16)
            dp_ref[:, 2 * da + lb * LANES:2 * da + (lb + 1) * LANES] = (dpv * ac).astype(BF16)
            dp_ref[:, 3 * da + lb * LANES:3 * da + (lb + 1) * LANES] = (t1 * q * dsz).astype(BF16)
            dw_ref[0, 0:1, cs] = dw_ref[0, 0:1, cs] + jnp.sum(dq * p2, axis=0, keepdims=True)
            dw_ref[0, 1:2, cs] = dw_ref[0, 1:2, cs] + jnp.sum(dq * p1, axis=0, keepdims=True)
            dw_ref[0, 2:3, cs] = dw_ref[0, 2:3, cs] + jnp.sum(dq * p, axis=0, keepdims=True)
            yield

    proj3 = proj.reshape(bl, dm.Lp, dm.NP)
    col = lambda w, k: pl.BlockSpec((bl, ta, w), lambda i: (0, nta - 1 - i, k))
    halo = lambda k: pl.BlockSpec((bl, HALO_BLOCK, da),
                                  lambda i: (0, jnp.maximum((nta - 1 - i) * (ta // HALO_BLOCK) - 1, 0), k))
    dp, dw = pl.pallas_call(
        _per_sequence(setup, body, bl, [True] * 7 + [False] + [True, "keep"] + [True, True]),
        name="mix_a_bwd", grid=(nta,),
        in_specs=[col(da, 0), col(da, 1), col(da, 2), col(da, 3), halo(1), halo(2), col(da, 0),
                  pl.BlockSpec((CONV_A_K, da), lambda i: (0, 0))],
        out_specs=[col(dm.WA, 0), pl.BlockSpec((bl, 8, da), lambda i: (0, 0, 0))],
        out_shape=[jax.ShapeDtypeStruct((bl, dm.Lp, dm.WA), BF16), jax.ShapeDtypeStruct((bl, 8, da), F32)],
        scratch_shapes=[pltpu.VMEM((bl, SMALL_HALO + ta, da), F32), pltpu.VMEM((bl, ta + SMALL_HALO, da), F32)],
        compiler_params=_params(("arbitrary",)),
    )(proj3, proj3, proj3, proj3, proj3, proj3, dya.reshape(bl, dm.Lp, da), conv_w)
    return dp.reshape(dm.R, dm.WA), dw


SUBLANES = 8
SHIFT_ROWS = TT + CONF_HALO - SUBLANES


TAP_ROWS = 64


def _split_lanes(buf, rows, val):
    for lb in range(val.shape[1] // LANES):
        buf[lb, rows, :] = val[:, lb * LANES:(lb + 1) * LANES]


def _join_lanes(buf):
    return jnp.concatenate([buf[lb] for lb in range(buf.shape[0])], axis=1)


def _fill_shifted(buf, shifted):
    def step(lb, carry):
        for r in range(1, SUBLANES):
            shifted[lb, r - 1, 0:SHIFT_ROWS, :] = buf[lb, r:r + SHIFT_ROWS, :]
        return carry

    lax.fori_loop(0, buf.shape[0], step, 0)


def _window(buf, shifted, d, r0, lb):
    r = d % SUBLANES
    rows = pl.ds(pl.multiple_of(r0 + (d - r), SUBLANES), TAP_ROWS)
    return buf[lb, rows, :] if r == 0 else shifted[lb, r - 1, rows, :]


def _tap_loop(nlb, body):
    per_lb = TT // TAP_ROWS

    def step(it, carry):
        lb = it // per_lb
        body(lb, pl.ds(pl.multiple_of(lb * LANES, LANES), LANES), pl.multiple_of((it % per_lb) * TAP_ROWS, TAP_ROWS))
        return carry

    lax.fori_loop(0, nlb * per_lb, step, 0)


TAP_CHAINS = 4


def _tree_sum(terms):
    sums = list(terms[:TAP_CHAINS])
    for n, t in enumerate(terms[TAP_CHAINS:]):
        sums[n % TAP_CHAINS] = sums[n % TAP_CHAINS] + t
    while len(sums) > 1:
        sums = [a + b for a, b in zip(sums[0::2], sums[1::2])] + ([sums[-1]] if len(sums) % 2 else [])
    return sums[0]


def _conf_conv(ubuf, ushift, w_ref, b_ref, u1buf):
    _fill_shifted(ubuf, ushift)

    def piece(lb, cs, r0):
        taps = [w_ref[k:k + 1, cs] * _window(ubuf, ushift, CONF_HALO - (CONF_K - 1) + k, r0, lb)
                for k in range(CONF_K)]
        u1buf[lb, pl.ds(r0, TAP_ROWS), :] = _tree_sum(taps) + b_ref[0:1, cs]

    _tap_loop(ubuf.shape[0], piece)


def _mix_c_fwd(proj, conv_w, conv_b, ln_g, ln_b, dm):
    dc = dm.DC
    nlb = dc // LANES
    c0 = (dm.WA + dm.WB) // dc
    ti = _tile_index(dm, False)

    def body(ca_ref, cg_ref, cz_ref, w_ref, b_ref, g_ref, be_ref, y_ref, u1_ref, ubuf, u1buf, ushift):
        i = pl.program_id(1)

        @pl.when(i == 0)
        def _():
            ubuf[:, 0:CONF_HALO, :] = jnp.zeros((nlb, CONF_HALO, LANES), F32)

        @pl.when(i > 0)
        def _():
            ubuf[:, 0:CONF_HALO, :] = ubuf[:, TT:TT + CONF_HALO, :]

        _split_lanes(ubuf, slice(CONF_HALO, CONF_HALO + TT),
                     ca_ref[...].astype(F32) * _sigmoid(cg_ref[...].astype(F32)))
        _conf_conv(ubuf, ushift, w_ref, b_ref, u1buf)
        u1 = _join_lanes(u1buf)
        u1_ref[...] = u1
        mu = jnp.mean(u1, axis=-1, keepdims=True)
        xc = u1 - mu
        rstd = lax.rsqrt(jnp.mean(xc * xc, axis=-1, keepdims=True) + LN_EPS)
        u2 = xc * rstd * g_ref[...] + be_ref[...]
        cz = cz_ref[...].astype(F32)
        y_ref[...] = ((u2 * _sigmoid(u2)) * (cz * _sigmoid(cz))).astype(BF16)

    col = lambda k: pl.BlockSpec((TT, dc), lambda b, i: (ti(b, i), c0 + k))
    vec = pl.BlockSpec((1, dc), lambda b, i: (0, 0))
    return pl.pallas_call(
        body, name="mix_c_fwd", grid=(dm.BL, dm.NT),
        in_specs=[col(0), col(1), col(2), pl.BlockSpec((CONF_K, dc), lambda b, i: (0, 0)), vec, vec, vec],
        out_specs=[pl.BlockSpec((TT, dc), lambda b, i: (ti(b, i), 0))] * 2,
        out_shape=[jax.ShapeDtypeStruct((dm.R, dc), BF16), jax.ShapeDtypeStruct((dm.R, dc), F32)],
        scratch_shapes=[pltpu.VMEM((nlb, CONF_HALO + TT, LANES), F32), pltpu.VMEM((nlb, TT, LANES), F32),
                        pltpu.VMEM((nlb, SUBLANES - 1, SHIFT_ROWS, LANES), F32)],
        compiler_params=_params(("parallel", "arbitrary")),
    )(proj, proj, proj, conv_w, conv_b, ln_g, ln_b)


def _mix_c_bwd(proj, u1, dyc, conv_w, ln_g, ln_b, dm):
    dc = dm.DC
    nlb = dc // LANES
    c0 = (dm.WA + dm.WB) // dc
    ti = _tile_index(dm, True)
    hi = _halo_index(dm, CONF_HALO)

    def body(ca_ref, cg_ref, cz_ref, cah_ref, cgh_ref, u1_ref, dy_ref, w_ref, g_ref, be_ref,
             dp_ref, dw_ref, dv_ref, ubuf, dubuf, du0buf, ushift, dshift, dwacc):
        i = pl.program_id(1)
        halo_on = jnp.where(i == dm.NT - 1, 0.0, 1.0)

        @pl.when(i == 0)
        def _():
            dwacc[...] = jnp.zeros_like(dwacc)
            dv_ref[...] = jnp.zeros_like(dv_ref)
            dubuf[:, TT:TT + CONF_HALO, :] = jnp.zeros((nlb, CONF_HALO, LANES), F32)

        @pl.when(i > 0)
        def _():
            dubuf[:, TT:TT + CONF_HALO, :] = dubuf[:, 0:CONF_HALO, :]

        _split_lanes(ubuf, slice(0, CONF_HALO),
                     cah_ref[...].astype(F32) * _sigmoid(cgh_ref[...].astype(F32)) * halo_on)
        sgg = _sigmoid(cg_ref[...].astype(F32))
        ca = ca_ref[...].astype(F32)
        _split_lanes(ubuf, slice(CONF_HALO, CONF_HALO + TT), ca * sgg)
        _fill_shifted(ubuf, ushift)
        u1 = u1_ref[...]
        mu = jnp.mean(u1, axis=-1, keepdims=True)
        xc = u1 - mu
        rstd = lax.rsqrt(jnp.mean(xc * xc, axis=-1, keepdims=True) + LN_EPS)
        xhat = xc * rstd
        u2 = xhat * g_ref[...] + be_ref[...]
        su, dsu = _silu_and_grad(u2)
        sz, dsz = _silu_and_grad(cz_ref[...].astype(F32))
        dy = dy_ref[...]
        du2 = dy * dsu * sz
        dp_ref[:, 2 * dc:3 * dc] = (dy * su * dsz).astype(BF16)
        dxhat = du2 * g_ref[...]
        du1 = rstd * (dxhat - jnp.mean(dxhat, axis=-1, keepdims=True)
                      - xhat * jnp.mean(dxhat * xhat, axis=-1, keepdims=True))
        dv_ref[0, 0:1, :] = dv_ref[0, 0:1, :] + jnp.sum(du1, axis=0, keepdims=True)
        dv_ref[0, 1:2, :] = dv_ref[0, 1:2, :] + jnp.sum(du2 * xhat, axis=0, keepdims=True)
        dv_ref[0, 2:3, :] = dv_ref[0, 2:3, :] + jnp.sum(du2, axis=0, keepdims=True)
        _split_lanes(dubuf, slice(0, TT), du1)
        _fill_shifted(dubuf, dshift)

        def piece(lb, cs, r0):
            du0buf[lb, pl.ds(r0, TAP_ROWS), :] = _tree_sum(
                [w_ref[k:k + 1, cs] * _window(dubuf, dshift, CONF_K - 1 - k, r0, lb) for k in range(CONF_K)])
            d1 = dubuf[lb, pl.ds(r0, TAP_ROWS), :]
            for k in range(CONF_K):
                prod = d1 * _window(ubuf, ushift, CONF_HALO - (CONF_K - 1) + k, r0, lb)
                dwacc[lb, k] = dwacc[lb, k] + jnp.sum(prod.reshape(TAP_ROWS // SUBLANES, SUBLANES, LANES), axis=0)

        _tap_loop(nlb, piece)
        du0 = _join_lanes(du0buf)
        dp_ref[:, 0:dc] = (du0 * sgg).astype(BF16)
        dp_ref[:, dc:2 * dc] = (du0 * ca * sgg * (1.0 - sgg)).astype(BF16)

        @pl.when(i == dm.NT - 1)
        def _():
            for lb in range(nlb):
                dw_ref[0, 0:CONF_K, lb * LANES:(lb + 1) * LANES] = jnp.sum(dwacc[lb], axis=1)
            dw_ref[0, CONF_K:CONF_K + 1, :] = jnp.zeros((1, dc), F32)

    col = lambda k: pl.BlockSpec((TT, dc), lambda b, i: (ti(b, i), c0 + k))
    halo = lambda k: pl.BlockSpec((CONF_HALO, dc), lambda b, i: (hi(b, i), c0 + k))
    vec = pl.BlockSpec((1, dc), lambda b, i: (0, 0))
    return pl.pallas_call(
        body, name="mix_c_bwd", grid=(dm.BL, dm.NT),
        in_specs=[col(0), col(1), col(2), halo(0), halo(1),
                  pl.BlockSpec((TT, dc), lambda b, i: (ti(b, i), 0)),
                  pl.BlockSpec((TT, dc), lambda b, i: (ti(b, i), 0)),
                  pl.BlockSpec((CONF_K, dc), lambda b, i: (0, 0)), vec, vec],
        out_specs=[pl.BlockSpec((TT, dm.WC), lambda b, i: (ti(b, i), 0)),
                   pl.BlockSpec((1, 32, dc), lambda b, i: (b, 0, 0)),
                   pl.BlockSpec((1, 8, dc), lambda b, i: (b, 0, 0))],
        out_shape=[jax.ShapeDtypeStruct((dm.R, dm.WC), BF16),
                   jax.ShapeDtypeStruct((dm.BL, 32, dc), F32),
                   jax.ShapeDtypeStruct((dm.BL, 8, dc), F32)],
        scratch_shapes=[pltpu.VMEM((nlb, CONF_HALO + TT, LANES), F32),
                        pltpu.VMEM((nlb, TT + CONF_HALO, LANES), F32), pltpu.VMEM((nlb, TT, LANES), F32),
                        pltpu.VMEM((nlb, SUBLANES - 1, SHIFT_ROWS, LANES), F32),
                        pltpu.VMEM((nlb, SUBLANES - 1, SHIFT_ROWS, LANES), F32),
                        pltpu.VMEM((nlb, CONF_K, SUBLANES, LANES), F32)],
        compiler_params=_params(("parallel", "arbitrary")),
    )(proj, proj, proj, proj, proj, u1, dyc, conv_w, ln_g, ln_b)


RAW_HALO = 16


def _shift_matrix():
    r = jnp.arange((SSM_CONV_K - 1) * TT)[:, None]
    want = TT + r % TT - (SSM_CONV_K - 1 - r // TT)
    return (jnp.arange(2 * TT)[None, :] == want).astype(BF16)


def _ssm_conv(rawwin, sh, s_ref, w_ref, b_ref, width):
    sh[...] = _dot(s_ref[...], rawwin[...])
    for lb in range(width // LANES):
        cs = slice(lb * LANES, (lb + 1) * LANES)
        acc = b_ref[0:1, cs] + w_ref[SSM_CONV_K - 1:SSM_CONV_K, cs] * rawwin[TT:2 * TT, cs].astype(F32)
        for k in range(SSM_CONV_K - 1):
            acc = acc + w_ref[k:k + 1, cs] * sh[k * TT:(k + 1) * TT, cs]
        yield cs, acc


def _softplus(z):
    return jnp.maximum(z, 0.0) + jnp.log(1.0 + jnp.exp(-jnp.abs(z)))


def _tri(lower):
    r = lax.broadcasted_iota(jnp.int32, (TT, TT), 0)
    c = lax.broadcasted_iota(jnp.int32, (TT, TT), 1)
    return (c <= r) if lower else (c >= r)


def _exact_01_dot(mat01, x):
    x1, x2, x3 = _split3(x)
    return _dot(mat01, x1) + _dot(mat01, x2) + _dot(mat01, x3)


def _head_scalars(dt_ref, dtb_ref, alog_ref):
    z = dt_ref[...] + dtb_ref[...]
    dtv = _softplus(z)
    a = -jnp.exp(alog_ref[...])
    ac = _exact_01_dot(_tri(True).astype(F32).astype(BF16), dtv * a)
    eac = jnp.exp(ac)
    dst = jnp.exp(ac[TT - 1:TT, :] - ac)
    return z, dtv, a, ac, eac, dst


FAR_BELOW = -1e30


def _decay(ac, ac_t, h, causal):
    return jnp.exp(jnp.where(causal, ac[:, h:h + 1] - ac_t[h:h + 1, :], FAR_BELOW))


def _own_half(x16, h):
    lane = lax.broadcasted_iota(jnp.int32, (1, LANES), 1)
    keep = (lane >= SSM_HEAD_DIM) if (h % 2) else (lane < SSM_HEAD_DIM)
    return jnp.where(keep, x16, jnp.zeros_like(x16))


def _per_sequence(setup, body, bl, how):
    def all_sequences(*refs):
        views = [[r.at[b] if h is True else (r.at[pl.ds(b, 1)] if h == "keep" else r) for r, h in zip(refs, how)]
                 for b in range(bl)]
        for v in views:
            setup(*v)
        running = [body(*v) for v in views]
        while running:
            running = [g for g in running if next(g, "done") != "done"]

    return all_sequences


def _mix_b_fwd(proj, projdt, conv_w, conv_b, dt_bias, a_log, dskx, norm_g, expand, dm, rider=None):
    db, gn, xbc_w, hpg = dm.DB, dm.GN, dm.XBC, dm.HPG
    gw = db // SSM_GROUPS

    def setup(bz_ref, bx_ref, bc_ref, dt_ref, w_ref, b_ref, dtb_ref, alog_ref, dsk_ref, g_ref, e_ref, s_ref,
              y_ref, yraw_ref, sprev_ref, rawwin, sh, xbuf, state, ybuf, exbuf, xdtbuf):
        i = pl.program_id(0)

        @pl.when(i == 0)
        def _():
            rawwin[0:TT, :] = jnp.zeros((TT, xbc_w), BF16)
            state[...] = jnp.zeros_like(state)

        @pl.when(i > 0)
        def _():
            rawwin[TT - RAW_HALO:TT, :] = rawwin[2 * TT - RAW_HALO:2 * TT, :]

    def body(bz_ref, bx_ref, bc_ref, dt_ref, w_ref, b_ref, dtb_ref, alog_ref, dsk_ref, g_ref, e_ref, s_ref,
             y_ref, yraw_ref, sprev_ref, rawwin, sh, xbuf, state, ybuf, exbuf, xdtbuf):
        rawwin[TT:2 * TT, 0:db] = bx_ref[...]
        rawwin[TT:2 * TT, db:xbc_w] = bc_ref[...]
        for cs, pre in _ssm_conv(rawwin, sh, s_ref, w_ref, b_ref, xbc_w):
            xbuf[:, cs] = pre * _sigmoid(pre)
            yield

        _, dtv, _, ac, eac, dst = _head_scalars(dt_ref, dtb_ref, alog_ref)
        exbuf[...] = _dot(jnp.concatenate([dtv, eac, dst], axis=0).astype(BF16), e_ref[...])
        ac_t = ac.T
        causal = _tri(True)
        sprev_ref[0, 0] = state[...]
        yield

        xdtbuf[...] = xbuf[:, 0:db] * exbuf[0:TT, :]
        ybuf[...] = xbuf[:, 0:db] * dsk_ref[...]
        for g in range(SSM_GROUPS):
            gs = slice(g * gw, (g + 1) * gw)
            bg = xbuf[:, db + g * SSM_STATE:db + (g + 1) * SSM_STATE].astype(BF16)
            cg = xbuf[:, db + gn + g * SSM_STATE:db + gn + (g + 1) * SSM_STATE].astype(BF16)
            cb = _dot_nt(cg, bg)
            for e in range(0, hpg, 2):
                h = g * hpg + e
                ps = slice(h * SSM_HEAD_DIM, (h + 2) * SSM_HEAD_DIM)
                xp16 = xdtbuf[:, ps].astype(BF16)
                acc = jnp.zeros((TT, LANES), F32)
                for hh in (h, h + 1):
                    mm = (cb * _decay(ac, ac_t, hh, causal)).astype(BF16)
                    acc = acc + _dot(mm, _own_half(xp16, hh))
                ybuf[:, ps] = ybuf[:, ps] + acc
                yield
            sg = state[:, gs]
            ybuf[:, gs] = ybuf[:, gs] + exbuf[TT:2 * TT, gs] * _dot(cg, sg.astype(BF16))
            state[:, gs] = sg * exbuf[2 * TT - 1:2 * TT, gs] + _dot_tn(
                bg, (xdtbuf[:, gs] * exbuf[2 * TT:3 * TT, gs]).astype(BF16))
            yield

        yraw = ybuf[...]
        yraw_ref[...] = yraw
        bz = bz_ref[...].astype(F32)
        v = yraw * (bz * _sigmoid(bz))
        r = lax.rsqrt(jnp.mean(v * v, axis=-1, keepdims=True) + NORM_EPS)
        y_ref[...] = (v * r * g_ref[...]).astype(BF16)

    bl = dm.BL
    tile = lambda w, k: pl.BlockSpec((bl, TT, w), lambda i: (0, i, k))
    fixed = lambda r, w: pl.BlockSpec((r, w), lambda i: (0, 0))
    proj3, dt3 = proj.reshape(bl, dm.Lp, dm.NP), projdt.reshape(bl, dm.Lp, DT_PAD)
    scratch = [((2 * TT, xbc_w), BF16), (((SSM_CONV_K - 1) * TT, xbc_w), F32), ((TT, xbc_w), F32),
               ((SSM_STATE, db), F32), ((TT, db), F32), ((3 * TT, db), F32), ((TT, db), F32)]
    (y, yraw, sprev), rode = _call(
        _per_sequence(setup, body, bl, [True] * 4 + [False] * 8 + [True, True, "keep"] + [True] * len(scratch)),
        "mix_b_fwd", (dm.NT,),
        [tile(db, dm.WA // db), tile(db, dm.WA // db + 1), tile(2 * gn, (dm.WA + 2 * db) // (2 * gn)),
         tile(DT_PAD, 0),
         fixed(SSM_CONV_K, xbc_w), fixed(1, xbc_w), fixed(1, DT_PAD), fixed(1, DT_PAD),
         fixed(1, db), fixed(1, db), fixed(DT_PAD, db), fixed((SSM_CONV_K - 1) * TT, 2 * TT)],
        [tile(db, 0), tile(db, 0), pl.BlockSpec((bl, 1, SSM_STATE, db), lambda i: (0, i, 0, 0))],
        [jax.ShapeDtypeStruct((bl, dm.Lp, db), BF16), jax.ShapeDtypeStruct((bl, dm.Lp, db), F32),
         jax.ShapeDtypeStruct((bl, dm.NT, SSM_STATE, db), F32)],
        [pltpu.VMEM((bl,) + s, t) for s, t in scratch],
        ("arbitrary",),
        (proj3, proj3, proj3, dt3, conv_w, conv_b, dt_bias, a_log, dskx, norm_g, expand, _shift_matrix()), rider)
    return (y.reshape(dm.R, db), yraw.reshape(dm.R, db), sprev), rode


def _mix_b_bwd(proj, projdt, dyb, yraw, sprev, conv_w, conv_b, dt_bias, a_log, dskx, norm_g, expand, expand_t, dm,
               rider=None):
    db, gn, xbc_w, hpg = dm.DB, dm.GN, dm.XBC, dm.HPG
    gw = db // SSM_GROUPS

    def setup(bz_ref, bx_ref, bc_ref, dt_ref, bxh_ref, bch_ref, dy_ref, yraw_ref, sprev_ref,
              w_ref, b_ref, dtb_ref, alog_ref, dsk_ref, g_ref, e_ref, et_ref, s_ref,
              dp_ref, dpt_ref, dwc_ref, dch_ref, dhd_ref,
              rawwin, sh, xbuf, dsbuf, dstate, dxbuf, z1buf, dprebuf, exbuf, xdtbuf, dyrbuf, uvec):
        i = pl.program_id(0)

        @pl.when(i == 0)
        def _():
            dwc_ref[...] = jnp.zeros_like(dwc_ref)
            dch_ref[...] = jnp.zeros_like(dch_ref)
            dhd_ref[...] = jnp.zeros_like(dhd_ref)
            dstate[...] = jnp.zeros_like(dstate)
            dprebuf[TT:TT + SMALL_HALO, :] = jnp.zeros((SMALL_HALO, xbc_w), F32)
            rawwin[0:TT - RAW_HALO, :] = jnp.zeros((TT - RAW_HALO, xbc_w), BF16)

        @pl.when(i > 0)
        def _():
            dprebuf[TT:TT + SMALL_HALO, :] = dprebuf[0:SMALL_HALO, :]

    def body(bz_ref, bx_ref, bc_ref, dt_ref, bxh_ref, bch_ref, dy_ref, yraw_ref, sprev_ref,
             w_ref, b_ref, dtb_ref, alog_ref, dsk_ref, g_ref, e_ref, et_ref, s_ref,
             dp_ref, dpt_ref, dwc_ref, dch_ref, dhd_ref,
             rawwin, sh, xbuf, dsbuf, dstate, dxbuf, z1buf, dprebuf, exbuf, xdtbuf, dyrbuf, uvec):
        halo_on = jnp.where(pl.program_id(0) == dm.NT - 1, 0.0, 1.0).astype(BF16)

        rawwin[TT - RAW_HALO:TT, 0:db] = bxh_ref[...] * halo_on
        rawwin[TT - RAW_HALO:TT, db:xbc_w] = bch_ref[...] * halo_on
        rawwin[TT:2 * TT, 0:db] = bx_ref[...]
        rawwin[TT:2 * TT, db:xbc_w] = bc_ref[...]
        for cs, pre in _ssm_conv(rawwin, sh, s_ref, w_ref, b_ref, xbc_w):
            sl, dsl = _silu_and_grad(pre)
            xbuf[:, cs] = sl
            dsbuf[:, cs] = dsl
            yield

        z, dtv, a, ac, eac, dst = _head_scalars(dt_ref, dtb_ref, alog_ref)
        exbuf[...] = _dot(jnp.concatenate([dtv, eac, dst], axis=0).astype(BF16), e_ref[...])
        ac_t = ac.T
        causal = _tri(True)
        xdtbuf[...] = xbuf[:, 0:db] * exbuf[0:TT, :]

        yraw = yraw_ref[...]
        sz, dsz = _silu_and_grad(bz_ref[...].astype(F32))
        v = yraw * sz
        r = lax.rsqrt(jnp.mean(v * v, axis=-1, keepdims=True) + NORM_EPS)
        dy = dy_ref[...]
        dyg = dy * g_ref[...]
        dv = r * dyg - v * (r * r * r * jnp.mean(dyg * v, axis=-1, keepdims=True))
        dch_ref[0, 0:1, :] = dch_ref[0, 0:1, :] + jnp.sum(dy * v * r, axis=0, keepdims=True)
        dyr = dv * sz
        dyrbuf[...] = dyr
        dp_ref[:, 0:db] = (dv * yraw * dsz).astype(BF16)
        dch_ref[0, 1:2, :] = dch_ref[0, 1:2, :] + jnp.sum(dyr * xbuf[:, 0:db], axis=0, keepdims=True)

        lane_row = lax.broadcasted_iota(jnp.int32, (1, LANES), 1)
        sub_col = lax.broadcasted_iota(jnp.int32, (LANES, 1), 0)
        dac = jnp.zeros((TT, LANES), F32)
        colacc = jnp.zeros((LANES, TT), F32)
        for g in range(SSM_GROUPS):
            gs = slice(g * gw, (g + 1) * gw)
            bs_ = slice(db + g * SSM_STATE, db + (g + 1) * SSM_STATE)
            cs_ = slice(db + gn + g * SSM_STATE, db + gn + (g + 1) * SSM_STATE)
            bg = xbuf[:, bs_].astype(BF16)
            cg = xbuf[:, cs_].astype(BF16)
            cb = _dot_nt(cg, bg)
            dcb = jnp.zeros((TT, TT), F32)
            for e in range(0, hpg, 2):
                h = g * hpg + e
                ps = slice(h * SSM_HEAD_DIM, (h + 2) * SSM_HEAD_DIM)
                xp16 = xdtbuf[:, ps].astype(BF16)
                dyp16 = dyrbuf[:, ps].astype(BF16)
                acc = jnp.zeros((TT, LANES), F32)
                for hh in (h, h + 1):
                    dec = _decay(ac, ac_t, hh, causal)
                    mm = cb * dec
                    dyh = _own_half(dyp16, hh)
                    dmm = _dot_nt(dyh, xp16)
                    acc = acc + _dot_tn(mm.astype(BF16), dyh)
                    dcb = dcb + dmm * dec
                    gm = dmm * mm
                    dac = jnp.where(lane_row == hh, jnp.sum(gm, axis=1, keepdims=True), dac)
                    colacc = jnp.where(sub_col == hh, jnp.sum(gm, axis=0, keepdims=True), colacc)
                dxbuf[:, ps] = acc
                yield
            sg32 = sprev_ref[0, 0, :, gs]
            sg = sg32.astype(BF16)
            dsn = dstate[:, gs]
            dsn16 = dsn.astype(BF16)
            dcb16 = dcb.astype(BF16)
            eacx = exbuf[TT:2 * TT, gs]
            dstx = exbuf[2 * TT:3 * TT, gs]
            cdx = exbuf[2 * TT - 1:2 * TT, gs]
            dye16 = (dyrbuf[:, gs] * eacx).astype(BF16)
            xdt_g = xdtbuf[:, gs]
            dxbuf[:, cs_] = _dot(dcb16, bg) + _dot_nt(dye16, sg)
            dst_x = dstx * _dot(bg, dsn16)
            dxbuf[:, bs_] = _dot_tn(dcb16, cg) + _dot_nt((dstx * xdt_g).astype(BF16), dsn16)
            dstate[:, gs] = cdx * dsn + _dot_tn(cg, dye16)
            z1buf[:, gs] = dyrbuf[:, gs] * (eacx * _dot(cg, sg)) - xdt_g * dst_x
            uvec[:, gs] = jnp.broadcast_to(
                jnp.sum(xdt_g * dst_x, axis=0, keepdims=True) + jnp.sum(dsn * cdx * sg32, axis=0, keepdims=True),
                (8, gw))
            dxbuf[:, gs] = dxbuf[:, gs] + dst_x
            yield

        zz = _dot(jnp.concatenate([z1buf[...], dxbuf[:, 0:db] * xbuf[:, 0:db]], axis=0).astype(BF16), et_ref[...])
        u1, u2, u3 = _split3(uvec[...])
        ulast = (_dot(u1, et_ref[...]) + _dot(u2, et_ref[...]) + _dot(u3, et_ref[...]))[0:1, :]
        is_last = (lax.broadcasted_iota(jnp.int32, (TT, 1), 0) == TT - 1).astype(F32)
        dac = dac - colacc.T + zz[0:TT] + is_last * ulast
        dda = _exact_01_dot(_tri(False).astype(F32).astype(BF16), dac)
        ddt = dda * a + zz[TT:2 * TT]
        dhd_ref[0, 1:2, :] = dhd_ref[0, 1:2, :] + jnp.sum(dda * dtv, axis=0, keepdims=True) * a
        ddtraw = ddt * _sigmoid(z)
        dhd_ref[0, 0:1, :] = dhd_ref[0, 0:1, :] + jnp.sum(ddtraw, axis=0, keepdims=True)
        dpt_ref[...] = ddtraw.astype(BF16)
        dxbuf[:, 0:db] = dyrbuf[...] * dsk_ref[...] + dxbuf[:, 0:db] * exbuf[0:TT, :]

        for lb in range(xbc_w // LANES):
            cs = slice(lb * LANES, (lb + 1) * LANES)
            dpre = dxbuf[:, cs] * dsbuf[:, cs]
            dprebuf[0:TT, cs] = dpre
            dwc_ref[0, SSM_CONV_K:SSM_CONV_K + 1, cs] = dwc_ref[0, SSM_CONV_K:SSM_CONV_K + 1, cs] + jnp.sum(
                dpre, axis=0, keepdims=True)
            draw = w_ref[SSM_CONV_K - 1:SSM_CONV_K, cs] * dpre
            for k in range(SSM_CONV_K - 1):
                ahead = SSM_CONV_K - 1 - k
                draw = draw + w_ref[k:k + 1, cs] * dprebuf[ahead:ahead + TT, cs]
            for k in range(SSM_CONV_K):
                moved = sh[k * TT:(k + 1) * TT, cs] if k < SSM_CONV_K - 1 else rawwin[TT:2 * TT, cs].astype(F32)
                dwc_ref[0, k:k + 1, cs] = dwc_ref[0, k:k + 1, cs] + jnp.sum(dpre * moved, axis=0, keepdims=True)
            dp_ref[:, db + lb * LANES:db + (lb + 1) * LANES] = draw.astype(BF16)
            yield

    bl, nt = dm.BL, dm.NT
    tile = lambda w, k: pl.BlockSpec((bl, TT, w), lambda i: (0, nt - 1 - i, k))
    halo = lambda w, k: pl.BlockSpec((bl, HALO_BLOCK, w),
                                     lambda i: (0, jnp.maximum((nt - 1 - i) * (TT // HALO_BLOCK) - 1, 0), k))
    fixed = lambda r, w: pl.BlockSpec((r, w), lambda i: (0, 0))
    sums = lambda w: pl.BlockSpec((bl, 8, w), lambda i: (0, 0, 0))
    kz = dm.WA // db
    kc = (dm.WA + 2 * db) // (2 * gn)
    proj3, dt3 = proj.reshape(bl, dm.Lp, dm.NP), projdt.reshape(bl, dm.Lp, DT_PAD)
    scratch = [((2 * TT, xbc_w), BF16), (((SSM_CONV_K - 1) * TT, xbc_w), F32), ((TT, xbc_w), F32),
               ((TT, xbc_w), F32), ((SSM_STATE, db), F32), ((TT, xbc_w), F32), ((TT, db), F32),
               ((TT + SMALL_HALO, xbc_w), F32), ((3 * TT, db), F32), ((TT, db), F32), ((TT, db), F32), ((8, db), F32)]
    how = [True] * 8 + ["keep"] + [False] * 9 + [True, True, "keep", "keep", "keep"] + [True] * len(scratch)
    (dp, dpt, dwc, dch, dhd), rode = _call(
        _per_sequence(setup, body, bl, how), "mix_b_bwd", (nt,),
        [tile(db, kz), tile(db, kz + 1), tile(2 * gn, kc), tile(DT_PAD, 0),
         halo(db, kz + 1), halo(2 * gn, kc), tile(db, 0), tile(db, 0),
         pl.BlockSpec((bl, 1, SSM_STATE, db), lambda i: (0, nt - 1 - i, 0, 0)),
         fixed(SSM_CONV_K, xbc_w), fixed(1, xbc_w), fixed(1, DT_PAD), fixed(1, DT_PAD),
         fixed(1, db), fixed(1, db), fixed(DT_PAD, db), fixed(db, DT_PAD), fixed((SSM_CONV_K - 1) * TT, 2 * TT)],
        [tile(dm.WB, 0), tile(DT_PAD, 0), sums(xbc_w), sums(db), sums(DT_PAD)],
        [jax.ShapeDtypeStruct((bl, dm.Lp, dm.WB), BF16), jax.ShapeDtypeStruct((bl, dm.Lp, DT_PAD), BF16),
         jax.ShapeDtypeStruct((bl, 8, xbc_w), F32), jax.ShapeDtypeStruct((bl, 8, db), F32),
         jax.ShapeDtypeStruct((bl, 8, DT_PAD), F32)],
        [pltpu.VMEM((bl,) + s, t) for s, t in scratch],
        ("arbitrary",),
        (proj3, proj3, proj3, dt3, proj3, proj3, dyb.reshape(bl, dm.Lp, db), yraw.reshape(bl, dm.Lp, db), sprev,
         conv_w, conv_b, dt_bias, a_log, dskx, norm_g, expand, expand_t, _shift_matrix()), rider)
    return (dp.reshape(dm.R, dm.WB), dpt.reshape(dm.R, DT_PAD), dwc, dch, dhd), rode


def _head_consts(dm):
    head_of = jnp.arange(dm.DB) // SSM_HEAD_DIM
    expand = (jnp.arange(DT_PAD)[:, None] == head_of[None, :]).astype(BF16)
    return expand, expand.T


def _ssm_params(lw, dm):
    pad_h = lambda v: jnp.pad(v, (0, DT_PAD - dm.H))[None]
    return (lw["ssm_conv_w"], lw["ssm_conv_b"][None], pad_h(lw["dt_bias"]), pad_h(lw["a_log"]),
            jnp.repeat(lw["d_skip"], SSM_HEAD_DIM)[None], lw["ssm_norm_g"][None])


def _layer_fwd(h, lw, w_in, w_out, cst, dm, next_bases=None, own_out=None):
    nxt = next_bases is not None
    n_next = len(next_bases) if nxt else 0

    def beside(rider, extra):
        return extra if rider is None else (rider if extra is None else _ride_both(rider, extra))

    (proj, projdt, hn), got = _fwd_in(
        h, lw["pre_g"][None], w_in, dm,
        beside(_ride_gather_ici(next_bases, 0, 2) if nxt else None,
               None if own_out is None else _ride_gather_ici([own_out])))
    ya = _mix_a_fwd(proj, lw["conv_a_w"], dm)
    (yb, yraw, sprev), got = _mix_b_fwd(
        proj, projdt, *_ssm_params(lw, dm), cst[0], dm,
        beside(_ride_gather_ici(got[:n_next], 1, 2) if nxt else None,
               None if own_out is None else _ride_gather_d2d(got[n_next:])))
    if own_out is not None:
        w_out = got[n_next].reshape(2 * dm.D, dm.D)
        got = got[:n_next]
    yc, u1 = _mix_c_fwd(proj, lw["conf_conv_w"], lw["conf_conv_b"][None], lw["conf_ln_g"][None],
                        lw["conf_ln_b"][None], dm)
    (h_new, m), got = _fwd_out(ya, yb, yc, w_out, h, lw["post_g"][None], dm, _ride_gather_d2d(got) if nxt else None)
    return h_new, (h, hn, proj, projdt, ya, yb, yc, u1, yraw, sprev, m), got, w_out


def _layer_bwd(dh, saved, lw, w_in, w_out, cst, dm, reduce=None, last=False):
    h_in, hn, proj, projdt, ya, yb, yc, u1, yraw, sprev, m = saved
    (dya, dyb, dyc, dwo, dpost), got = _bwd_out(dh, m, lw["post_g"][None], w_out, ya, yb, yc, dm,
                                                None if reduce is None else reduce.swap())
    dpa, dwa = _mix_a_bwd(proj, dya, lw["conv_a_w"], dm)
    (dpb, dpt, dwcv, dch, dhd), got = _mix_b_bwd(proj, projdt, dyb, yraw, sprev, *_ssm_params(lw, dm), cst[0],
                                                 cst[1], dm, None if reduce is None else reduce.to_owners(got))
    dpc, dwcf, dvc = _mix_c_bwd(proj, u1, dyc, lw["conf_conv_w"], lw["conf_ln_g"][None], lw["conf_ln_b"][None], dm)
    def own_reduce():
        pieces = _bwd_in_dw(hn, [dpa, dpb, dpc, dpt], dm)
        return _GradReduce([_grad_to_shards(pieces, dm), dwo.reshape(N_CHIPS, 2 * dm.D // N_CHIPS, dm.D)])

    rider = None if reduce is None else reduce.join(got)
    n_join = 0 if rider is None else len(rider.out_shapes)
    if last:
        mine = own_reduce()
        to_owners = mine.to_owners(_exchange("grad_swap_halves", mine.swap()))
        rider = to_owners if rider is None else _ride_both(rider, to_owners)
    (dh, dpre), got = _bwd_in_dx(dpa, dpb, dpc, dpt, w_in, h_in, dh, lw["pre_g"][None], dm, rider)
    if reduce is not None:
        reduce.finish(got[:n_join])
    if last:
        mine.finish(_exchange("grad_join_halves", mine.join(got[n_join:])))
    else:
        mine = own_reduce()
    dwcv, dch, dhd, dvc = (jnp.sum(a, axis=0) for a in (dwcv, dch, dhd, dvc))
    small = dict(pre_g=dpre[0], post_g=dpost[0], conv_a_w=jnp.sum(dwa, axis=0)[:CONV_A_K],
                 ssm_conv_w=dwcv[:SSM_CONV_K], ssm_conv_b=dwcv[SSM_CONV_K], ssm_norm_g=dch[0],
                 d_skip=jnp.sum(dch[1].reshape(dm.H, SSM_HEAD_DIM), axis=1), dt_bias=dhd[0, :dm.H],
                 a_log=dhd[1, :dm.H], conf_conv_w=jnp.sum(dwcf, axis=0)[:CONF_K], conf_conv_b=dvc[0],
                 conf_ln_g=dvc[1], conf_ln_b=dvc[2])
    return dh, mine, small


def _shard_runs(dm):
    ab = dm.WA + dm.WB
    order = [(0, 0, ab), (ab, dm.DT0, dm.H), (ab + dm.H, ab, dm.WC)]
    k = dm.NIN // N_CHIPS
    runs = []
    for s in range(N_CHIPS):
        for o0, m0, wd in order:
            lo, hi = max(o0, s * k), min(o0 + wd, (s + 1) * k)
            if lo < hi:
                runs.append((s, lo - s * k, m0 + lo - o0, hi - lo))
    return runs


def _w_in_from_shards(base, dm):
    tr = _row_tile(dm.D, 256)
    k = dm.NIN // N_CHIPS
    runs = _shard_runs(dm)

    def body(in_ref, out_ref):
        for s, sc, mc, wd in runs:
            out_ref[:, mc:mc + wd] = in_ref[s, :, sc:sc + wd]
        out_ref[:, dm.DT0 + dm.H:dm.NP] = jnp.zeros((tr, dm.NP - dm.DT0 - dm.H), BF16)

    return pl.pallas_call(
        body, name="w_in_from_shards", grid=(dm.D // tr,),
        in_specs=[pl.BlockSpec((N_CHIPS, tr, k), lambda r: (0, r, 0))],
        out_specs=pl.BlockSpec((tr, dm.NP), lambda r: (r, 0)),
        out_shape=jax.ShapeDtypeStruct((dm.D, dm.NP), BF16),
        compiler_params=_params(("parallel",)),
    )(base)


def _grad_to_shards(pieces, dm):
    tr = _row_tile(dm.D, 256)
    k = dm.NIN // N_CHIPS
    starts = [0, dm.WA, dm.WA + dm.WB, dm.DT0]
    widths = [dm.WA, dm.WB, dm.WC, DT_PAD]
    runs = _shard_runs(dm)

    def body(a_ref, b_ref, c_ref, t_ref, out_ref):
        refs = (a_ref, b_ref, c_ref, t_ref)
        for s, sc, mc, wd in runs:
            for p in range(4):
                lo, hi = max(mc, starts[p]), min(mc + wd, starts[p] + widths[p])
                if lo < hi:
                    out_ref[s, :, sc + lo - mc:sc + hi - mc] = refs[p][:, lo - starts[p]:hi - starts[p]].astype(BF16)

    return pl.pallas_call(
        body, name="grad_to_shards", grid=(dm.D // tr,),
        in_specs=[pl.BlockSpec((tr, w), lambda r: (r, 0)) for w in widths],
        out_specs=pl.BlockSpec((N_CHIPS, tr, k), lambda r: (0, r, 0)),
        out_shape=jax.ShapeDtypeStruct((N_CHIPS, dm.D, k), BF16),
        compiler_params=_params(("parallel",)),
    )(*pieces)


def _place_own(w, layer, me):
    _, rows, cols = w.shape
    tr = _row_tile(rows, 256)

    def body(me_ref, w_ref, out_ref):
        out_ref[0] = w_ref[0].astype(BF16)

    return pl.pallas_call(
        body, name="place_own",
        grid_spec=pltpu.PrefetchScalarGridSpec(
            num_scalar_prefetch=1, grid=(rows // tr,),
            in_specs=[pl.BlockSpec((1, tr, cols), lambda r, me_ref: (layer, r, 0))],
            out_specs=pl.BlockSpec((1, tr, cols), lambda r, me_ref: (me_ref[0], r, 0))),
        out_shape=jax.ShapeDtypeStruct((N_CHIPS, rows, cols), BF16),
        compiler_params=_params(("parallel",)),
    )(me, w)


def _add_halves(g, got, c, name):
    _, _, rows, cols = g.shape
    tr = _row_tile(rows, 256)

    def body(c_ref, g_ref, got_ref, out_ref):
        out_ref[0] = (g_ref[0, 0].astype(F32) + got_ref[0].astype(F32)).astype(BF16)

    return pl.pallas_call(
        body, name=name,
        grid_spec=pltpu.PrefetchScalarGridSpec(
            num_scalar_prefetch=1, grid=(N_CHIPS, rows // tr),
            in_specs=[pl.BlockSpec((1, 1, tr, cols), lambda s, r, c_ref: (s, c_ref[0], r, 0)),
                      pl.BlockSpec((1, tr, cols), lambda s, r, c_ref: (s, r, 0))],
            out_specs=pl.BlockSpec((1, tr, cols), lambda s, r, c_ref: (s, r, 0))),
        out_shape=jax.ShapeDtypeStruct((N_CHIPS, rows, cols), BF16),
        compiler_params=_params(("parallel", "parallel")),
    )(c, g, got)


def _add_owner(p, got, where, name):
    _, rows, cols = p.shape
    tr = _row_tile(rows, 256)

    def body(w_ref, p_ref, got_ref, out_ref):
        acc = p_ref[0].astype(F32)
        for j in range(3):
            acc = acc + got_ref[j].astype(F32)
        out_ref[0] = acc

    return pl.pallas_call(
        body, name=name,
        grid_spec=pltpu.PrefetchScalarGridSpec(
            num_scalar_prefetch=1, grid=(rows // tr,),
            in_specs=[pl.BlockSpec((1, tr, cols), lambda r, w_ref: (w_ref[0], r, 0)),
                      pl.BlockSpec((3, tr, cols), lambda r, w_ref: (0, r, 0))],
            out_specs=pl.BlockSpec((1, tr, cols), lambda r, w_ref: (w_ref[1], r, 0))),
        out_shape=jax.ShapeDtypeStruct((2, rows, cols), F32),
        compiler_params=_params(("parallel",)),
    )(where, p, got)


class _GradReduce:
    def __init__(self, gs):
        self.gs = [g.reshape((N_CHIPS, 2, g.shape[1] // 2) + g.shape[2:]) for g in gs]
        self.c = lax.axis_index("c").astype(jnp.int32).reshape(1)
        chip = (2 * lax.axis_index("x") + lax.axis_index("y")).astype(jnp.int32)
        self.where = jnp.stack([chip, self.c[0]])
        self.result = None

    def swap(self):
        return _ride_swap_halves(self.gs)

    def to_owners(self, got):
        self.ps = [_add_halves(g, r, self.c, "grad_add_sibling_" + n) for g, r, n in zip(self.gs, got, ("in", "out"))]
        return _ride_to_owners(self.ps)

    def join(self, got):
        qs = [_add_owner(p, r, self.where, "grad_add_chips_" + n) for p, r, n in zip(self.ps, got, ("in", "out"))]
        return _ride_join_halves(qs)

    def finish(self, got):
        self.result = [a.reshape((a.shape[0] * a.shape[1],) + a.shape[2:]) for a in got]


def _adamw_math(w, g, m, v):
    m = ADAM_B1 * m + (1.0 - ADAM_B1) * g
    v = ADAM_B2 * v + (1.0 - ADAM_B2) * (g * g)
    m_hat = m / (1.0 - ADAM_B1 ** ADAM_STEP)
    v_hat = v / (1.0 - ADAM_B2 ** ADAM_STEP)
    delta = -ADAM_LR * (m_hat / (jnp.sqrt(v_hat) + ADAM_EPS) + ADAM_WD * w)
    return delta, m, v


def _adamw_small(w, g, m, v, name):
    def body(w_ref, g_ref, m_ref, v_ref, d_out, m_out, v_out):
        d_out[...], m_out[...], v_out[...] = _adamw_math(w_ref[...], g_ref[...], m_ref[...], v_ref[...])

    shape = jax.ShapeDtypeStruct(w.shape, F32)
    return pl.pallas_call(body, name="adamw_" + name, out_shape=[shape, shape, shape],
                          compiler_params=_params())(w, g, m, v)


def _adamw_layer(i, w, g, m, v, prev, name):
    depth, rows, cols = w.shape
    tr = _row_tile(rows, 256)
    n_prev = 0 if prev is None else 4

    def body(*refs):
        w_ref, g_ref, m_ref, v_ref = refs[:4]
        g_out, d_out, m_out, v_out = refs[4 + n_prev:]
        gv = g_ref[...]
        g_out[0] = gv
        d_out[0], m_out[0], v_out[0] = _adamw_math(w_ref[0], gv, m_ref[0], v_ref[0])

    lay = pl.BlockSpec((1, tr, cols), lambda r: (i, r, 0))
    shape = jax.ShapeDtypeStruct(w.shape, F32)
    return pl.pallas_call(
        body, name="adamw_" + name, grid=(rows // tr,),
        in_specs=[lay, pl.BlockSpec((tr, cols), lambda r: (r, 0)), lay, lay] + [ANY] * n_prev,
        out_specs=[lay] * 4, out_shape=[shape] * 4,
        input_output_aliases={4 + k: k for k in range(n_prev)},
        compiler_params=_params(("parallel",)),
    )(w, g, m, v, *(prev or ()))


def _adamw_cols_major(w, gs, m, v, name):
    depth, rows, cols = w.shape
    tr = max(t for t in range(1, 129) if cols % t == 0)
    wt, mt, vt = (jnp.transpose(a, (2, 0, 1)) for a in (w, m, v))
    gt = jnp.transpose(jnp.stack(gs, axis=0), (2, 0, 1))

    def body(w_ref, g_ref, m_ref, v_ref, g_out, d_out, m_out, v_out):
        gv = g_ref[...]
        g_out[...] = gv
        d_out[...], m_out[...], v_out[...] = _adamw_math(w_ref[...], gv, m_ref[...], v_ref[...])

    spec = pl.BlockSpec((tr, depth, rows), lambda r: (r, 0, 0))
    shape = jax.ShapeDtypeStruct((cols, depth, rows), F32)
    outs = pl.pallas_call(body, name="adamw_" + name, grid=(cols // tr,), in_specs=[spec] * 4, out_specs=[spec] * 4,
                          out_shape=[shape] * 4, compiler_params=_params(("parallel",)))(wt, gt, mt, vt)
    return [jnp.transpose(a, (1, 2, 0)) for a in outs]


def _sum_leading(buf, name):
    n, rows, cols = buf.shape
    tr = _row_tile(rows, rows)

    def body(in_ref, out_ref):
        acc = in_ref[0]
        for k in range(1, n):
            acc = acc + in_ref[k]
        out_ref[...] = acc

    return pl.pallas_call(
        body, name=name, grid=(rows // tr,),
        in_specs=[pl.BlockSpec((n, tr, cols), lambda i: (0, i, 0))],
        out_specs=pl.BlockSpec((tr, cols), lambda i: (i, 0)),
        out_shape=jax.ShapeDtypeStruct((rows, cols), F32),
        compiler_params=_params(("parallel",)),
    )(buf)


_SHARDED_SMALL = ("meta", "conv_a_w", "ssm_conv_w", "conf_conv_w")
_LAYER_SMALL = ("pre_g", "post_g", "conv_a_w", "ssm_conv_w", "ssm_conv_b", "dt_bias", "a_log", "d_skip",
                "ssm_norm_g", "conf_conv_w", "conf_conv_b", "conf_ln_g", "conf_ln_b")
_WEIGHTS = ("meta", "pre_g", "post_g", "w_in", "w_out", "conv_a_w", "ssm_conv_w", "ssm_conv_b", "dt_bias", "a_log",
            "d_skip", "ssm_norm_g", "conf_conv_w", "conf_conv_b", "conf_ln_g", "conf_ln_b")


def _shard_last(a):
    return jnp.moveaxis(a.reshape(a.shape[:-1] + (N_CHIPS, a.shape[-1] // N_CHIPS)), -2, 0)


def _with_own_block(a, n, at):
    return lax.dynamic_update_index_in_dim(jnp.zeros((n,) + a.shape, a.dtype), a, at, 0)


def _with_own_columns(a, chip):
    k = a.shape[-1]
    return lax.dynamic_update_slice_in_dim(jnp.zeros(a.shape[:-1] + (N_CHIPS * k,), a.dtype), a, chip * k, a.ndim - 1)


def kernel(x, meta, pre_g, post_g, w_in, w_out, conv_a_w, ssm_conv_w, ssm_conv_b, dt_bias, a_log, d_skip, ssm_norm_g, conf_conv_w, conf_conv_b, conf_ln_g, conf_ln_b, loss_target, m_meta, m_pre_g, m_post_g, m_w_in, m_w_out, m_conv_a_w, m_ssm_conv_w, m_ssm_conv_b, m_dt_bias, m_a_log, m_d_skip, m_ssm_norm_g, m_conf_conv_w, m_conf_conv_b, m_conf_ln_g, m_conf_ln_b, v_meta, v_pre_g, v_post_g, v_w_in, v_w_out, v_conv_a_w, v_ssm_conv_w, v_ssm_conv_b, v_dt_bias, v_a_log, v_d_skip, v_ssm_norm_g, v_conf_conv_w, v_conf_conv_b, v_conf_ln_g, v_conf_ln_b):
    w = dict(meta=meta, pre_g=pre_g, post_g=post_g, w_in=w_in, w_out=w_out, conv_a_w=conv_a_w,
             ssm_conv_w=ssm_conv_w, ssm_conv_b=ssm_conv_b, dt_bias=dt_bias, a_log=a_log, d_skip=d_skip,
             ssm_norm_g=ssm_norm_g, conf_conv_w=conf_conv_w, conf_conv_b=conf_conv_b, conf_ln_g=conf_ln_g,
             conf_ln_b=conf_ln_b)
    mom = dict(meta=m_meta, pre_g=m_pre_g, post_g=m_post_g, w_in=m_w_in, w_out=m_w_out, conv_a_w=m_conv_a_w,
               ssm_conv_w=m_ssm_conv_w, ssm_conv_b=m_ssm_conv_b, dt_bias=m_dt_bias, a_log=m_a_log, d_skip=m_d_skip,
               ssm_norm_g=m_ssm_norm_g, conf_conv_w=m_conf_conv_w, conf_conv_b=m_conf_conv_b,
               conf_ln_g=m_conf_ln_g, conf_ln_b=m_conf_ln_b)
    vel = dict(meta=v_meta, pre_g=v_pre_g, post_g=v_post_g, w_in=v_w_in, w_out=v_w_out, conv_a_w=v_conv_a_w,
               ssm_conv_w=v_ssm_conv_w, ssm_conv_b=v_ssm_conv_b, dt_bias=v_dt_bias, a_log=v_a_log, d_skip=v_d_skip,
               ssm_norm_g=v_ssm_norm_g, conf_conv_w=v_conf_conv_w, conf_conv_b=v_conf_conv_b,
               conf_ln_g=v_conf_ln_g, conf_ln_b=v_conf_ln_b)
    bl, seq, d = x.shape
    dm = Dims(bl, seq, d)
    depth = w_in.shape[0]
    chip = (2 * lax.axis_index("x") + lax.axis_index("y")).astype(jnp.int32)
    dev = 2 * chip + lax.axis_index("c").astype(jnp.int32)
    cst = _head_consts(dm)

    bases = [[_place_own(w_in, i, chip.reshape(1)), _place_own(w_out, i, chip.reshape(1))] for i in range(depth)]
    first_in, small_w = _gather_ici_relayed(
        [bases[0][0]], _ride_gather_small([_with_own_columns(w[n], chip) for n in _SHARDED_SMALL]))
    full = dict(w)
    full.update(zip(_SHARDED_SMALL, small_w))
    h, gathered = _embed(x, full["meta"], dm, _ride_gather_d2d(first_in))
    saved, proj_w = [], []
    for i in range(depth):
        lw = {n: full[n][i] for n in _LAYER_SMALL}
        w_in_i = _w_in_from_shards(gathered[0], dm)
        h, keep, gathered, w_out_i = _layer_fwd(
            h, lw, w_in_i, None if i == 0 else gathered[1].reshape(2 * d, d), cst, dm,
            bases[i + 1] if i + 1 < depth else None, bases[0][1] if i == 0 else None)
        proj_w.append((w_in_i, w_out_i))
        saved.append(keep)

    dh, loss = _loss_head(h, loss_target, dm)
    loss = lax.psum(loss, ("x", "y", "c"))

    small_g = {n: [None] * depth for n in _LAYER_SMALL}
    big = {"w_in": None, "w_out": None}
    g_in = [None] * depth
    reduce = None
    for i in reversed(range(depth)):
        lw = {n: full[n][i] for n in _LAYER_SMALL}
        dh, mine, sg = _layer_bwd(dh, saved[i], lw, proj_w[i][0], proj_w[i][1], cst, dm, reduce, last=i == 0)
        for n in _LAYER_SMALL:
            small_g[n][i] = sg[n]
        if reduce is not None:
            g_in[i + 1] = reduce.result[0]
            big["w_out"] = _adamw_layer(i + 1, w_out, reduce.result[1], m_w_out, v_w_out, big["w_out"], "w_out")
        reduce = mine
    g_in[0] = reduce.result[0]
    big["w_out"] = _adamw_layer(0, w_out, reduce.result[1], m_w_out, v_w_out, big["w_out"], "w_out")
    grad_x, gmeta = _unembed(dh, dm)

    g = {n: jnp.stack(v) for n, v in small_g.items()}
    g["meta"] = gmeta
    small = [n for n in _WEIGHTS if n not in ("w_in", "w_out")]
    flat = jnp.concatenate([g[n].reshape(-1) for n in small])
    rows = -(-flat.shape[0] // (16 * LANES)) * 16
    flat = jnp.pad(flat, (0, rows * LANES - flat.shape[0])).reshape(rows, LANES)
    parts = _gather_all(_with_own_block(flat, N_DEV, dev))
    total = _sum_leading(parts, "small_grads_sum").reshape(-1)
    big["w_in"] = _adamw_cols_major(w_in, g_in, m_w_in, v_w_in, "w_in")
    grads, deltas, new_m, new_v = {}, {}, {}, {}
    off = 0
    for n in small:
        size = g[n].size
        fullg = total[off:off + size].reshape(g[n].shape)
        off += size
        if n in _SHARDED_SMALL:
            fullg = lax.dynamic_index_in_dim(_shard_last(fullg), chip, axis=0, keepdims=False)
        grads[n] = fullg
        deltas[n], new_m[n], new_v[n] = _adamw_small(w[n], fullg, mom[n], vel[n], n)
    for n in ("w_in", "w_out"):
        grads[n], deltas[n], new_m[n], new_v[n] = big[n]

    return (loss, grad_x, *[grads[n] for n in _WEIGHTS], *[deltas[n] for n in _WEIGHTS],
            *[new_m[n] for n in _WEIGHTS], *[new_v[n] for n in _WEIGHTS])
```

```python
import jax
import jax.numpy as jnp
from jax import lax
from jax.experimental import pallas as pl
from jax.experimental.pallas import tpu as pltpu

F32 = jnp.float32
BF16 = jnp.bfloat16

N_META = 16
TT = 128
SSM_STATE = 128
SSM_GROUPS = 2
SSM_HEAD_DIM = 64
CONV_A_K = 3
SSM_CONV_K = 4
CONF_K = 31
NORM_EPS = 1e-6
LN_EPS = 1e-5
LANES = 128
MXU_DIM = 256
DT_PAD = LANES
CONF_HALO = 32
SMALL_HALO = 8
VMEM_LIMIT = 56 * 1024 * 1024
N_CHIPS = 4
N_DEV = 8

ADAM_LR = 0.001
ADAM_B1 = 0.9
ADAM_B2 = 0.999
ADAM_EPS = 1e-08
ADAM_WD = 0.01
ADAM_STEP = 10

MESH = pl.DeviceIdType.MESH
ANY = pl.BlockSpec(memory_space=pl.ANY)


class Dims:
    def __init__(self, bl, seq, d):
        self.BL, self.S, self.D = bl, seq, d
        self.L = seq + N_META
        self.Lp = -(-self.L // TT) * TT
        self.NT = self.Lp // TT
        self.R = bl * self.Lp
        self.DA = d // 2
        self.DB = d
        self.DC = d // 2
        self.H = self.DB // SSM_HEAD_DIM
        self.HPG = self.H // SSM_GROUPS
        self.GN = SSM_GROUPS * SSM_STATE
        self.WA = 4 * self.DA
        self.WB = 2 * self.DB + 2 * self.GN
        self.WC = 3 * self.DC
        self.DT0 = self.WA + self.WB + self.WC
        self.NP = -(-(self.DT0 + DT_PAD) // (5 * MXU_DIM)) * (5 * MXU_DIM)
        self.NIN = self.WA + self.WB + self.H + self.WC
        self.XBC = self.DB + 2 * self.GN
        assert self.H % 2 == 0 and self.HPG % 2 == 0 and self.H <= DT_PAD
        assert self.DA % LANES == 0 and (self.WA + self.WB) % self.DC == 0 and self.WA % self.DB == 0


def _row_tile(n, target):
    best = None
    for t in range(16, min(n, target) + 1, 16):
        if n % t == 0:
            best = t
    assert best is not None
    return best


def _col_tile(n, target):
    best = None
    for t in range(LANES, min(n, target) + 1, LANES):
        if n % t == 0:
            best = t
    assert best is not None
    return best


def _params(sem=None):
    return pltpu.CompilerParams(dimension_semantics=sem, vmem_limit_bytes=VMEM_LIMIT)


def _sigmoid(x):
    return 1.0 / (1.0 + jnp.exp(-x))


def _silu_and_grad(x):
    s = _sigmoid(x)
    y = x * s
    return y, s + y * (1.0 - s)


def _dot(a, b):
    return jnp.dot(a, b, preferred_element_type=F32)


def _dot_nt(a, b):
    return lax.dot_general(a, b, (((1,), (1,)), ((), ())), preferred_element_type=F32)


def _dot_tn(a, b):
    return lax.dot_general(a, b, (((0,), (0,)), ((), ())), preferred_element_type=F32)


def _split3(x):
    x1 = x.astype(BF16)
    r1 = x - x1.astype(F32)
    x2 = r1.astype(BF16)
    x3 = (r1 - x2.astype(F32)).astype(BF16)
    return x1, x2, x3


class Rider:
    def __init__(self, plan, ins, out_shapes, aliases, nsem):
        self.plan, self.ins, self.out_shapes, self.aliases, self.nsem = plan, list(ins), list(out_shapes), aliases, nsem


def _place():
    x, y, c = lax.axis_index("x"), lax.axis_index("y"), lax.axis_index("c")
    chips = [(1 - x, y), (x, 1 - y), (1 - x, 1 - y)]
    return x, y, c, chips


def _remote(k, src, dst, to, send_sems, recv_sems):
    return pltpu.make_async_remote_copy(src_ref=src, dst_ref=dst, send_sem=send_sems.at[k], recv_sem=recv_sems.at[k],
                                        device_id=to, device_id_type=MESH)


def _call(body, name, grid, in_specs, out_specs, out_shape, scratch_shapes, sem, args, rider=None):
    if rider is None:
        outs = pl.pallas_call(body, name=name, grid=grid, in_specs=in_specs, out_specs=out_specs, out_shape=out_shape,
                              scratch_shapes=scratch_shapes, compiler_params=_params(sem))(*args)
        return list(outs), []
    n_in, n_out, n_scr = len(args), len(out_shape), len(scratch_shapes)
    r_in, r_out = len(rider.ins), len(rider.out_shapes)

    def hosted(*refs):
        ins, rins = refs[:n_in], refs[n_in:n_in + r_in]
        o0 = n_in + r_in
        outs, routs = refs[o0:o0 + n_out], refs[o0 + n_out:o0 + n_out + r_out]
        scr = refs[o0 + n_out + r_out:o0 + n_out + r_out + n_scr]
        send_sems, recv_sems = refs[o0 + n_out + r_out + n_scr:]
        first = pl.program_id(0) == 0
        last = pl.program_id(0) == grid[0] - 1
        for ax in range(1, len(grid)):
            first = jnp.logical_and(first, pl.program_id(ax) == 0)
            last = jnp.logical_and(last, pl.program_id(ax) == grid[ax] - 1)

        @pl.when(first)
        def _():
            starts, _ = rider.plan(rins, routs, send_sems, recv_sems)
            for cp in starts:
                cp.start()

        body(*ins, *outs, *scr)

        @pl.when(last)
        def _():
            _, waits = rider.plan(rins, routs, send_sems, recv_sems)
            for wait in waits:
                wait()

    res = pl.pallas_call(
        hosted, name=name, grid=grid,
        in_specs=list(in_specs) + [ANY] * r_in, out_specs=list(out_specs) + [ANY] * r_out,
        out_shape=list(out_shape) + rider.out_shapes,
        input_output_aliases={n_in + k: n_out + v for k, v in rider.aliases.items()},
        scratch_shapes=list(scratch_shapes) + [pltpu.SemaphoreType.DMA((rider.nsem,)),
                                               pltpu.SemaphoreType.DMA((rider.nsem,))],
        compiler_params=_params(("arbitrary",) * len(grid)),
    )(*args, *rider.ins)
    return list(res[:n_out]), list(res[n_out:])


def _exchange(name, rider):
    r_in, r_out = len(rider.ins), len(rider.out_shapes)

    def body(*refs):
        rins, routs = refs[:r_in], refs[r_in:r_in + r_out]
        send_sems, recv_sems = refs[r_in + r_out:]
        starts, waits = rider.plan(rins, routs, send_sems, recv_sems)
        for cp in starts:
            cp.start()
        for wait in waits:
            wait()

    res = pl.pallas_call(
        body, name=name, in_specs=[ANY] * r_in, out_specs=[ANY] * r_out, out_shape=rider.out_shapes,
        input_output_aliases=dict(rider.aliases),
        scratch_shapes=[pltpu.SemaphoreType.DMA((rider.nsem,)), pltpu.SemaphoreType.DMA((rider.nsem,))],
    )(*rider.ins)
    return list(res)


def _same(arrays):
    return [jax.ShapeDtypeStruct(a.shape, a.dtype) for a in arrays]


class _SemsFrom:
    def __init__(self, sems, first):
        self.sems, self.first = sems, first

    @property
    def at(self):
        return self

    def __getitem__(self, k):
        return self.sems.at[self.first + k]


def _ride_both(r1, r2):
    n_in, n_out = len(r1.ins), len(r1.out_shapes)

    def plan(ins, outs, ss, rs):
        s1, w1 = r1.plan(ins[:n_in], outs[:n_out], ss, rs)
        s2, w2 = r2.plan(ins[n_in:], outs[n_out:], _SemsFrom(ss, r1.nsem), _SemsFrom(rs, r1.nsem))
        return s1 + s2, w1 + w2

    aliases = dict(r1.aliases)
    aliases.update({n_in + k: n_out + v for k, v in r2.aliases.items()})
    return Rider(plan, r1.ins + r2.ins, r1.out_shapes + r2.out_shapes, aliases, r1.nsem + r2.nsem)


def _ride_gather_ici(bases, part=0, nparts=1):
    n = len(bases)

    def plan(ins, outs, ss, rs):
        x, y, c, chips = _place()
        me = 2 * x + y
        starts, waits = [], []
        for a in range(n):
            half = outs[a].shape[1] // 2
            mine = pl.ds(c * half + part * (half // nparts), half // nparts)
            for j, chip in enumerate(chips):
                cp = _remote(3 * a + j, outs[a].at[me, mine], outs[a].at[me, mine], (*chip, c), ss, rs)
                got = outs[a].at[2 * chip[0] + chip[1], mine]
                starts.append(cp)
                waits += [cp.wait_send, _remote(3 * a + j, got, got, (*chip, c), ss, rs).wait_recv]
        return starts, waits

    return Rider(plan, bases, _same(bases), {a: a for a in range(n)}, 3 * n)


def _gather_ici_relayed(bases, also):
    n, m = len(bases), len(also.ins)

    def body(*refs):
        outs = refs[n + m:2 * n + m]
        ss, rs = refs[2 * (n + m):]
        beside, beside_waits = also.plan(refs[n:n + m], refs[2 * n + m:2 * (n + m)],
                                         _SemsFrom(ss, 4 * n), _SemsFrom(rs, 4 * n))
        for cp in beside:
            cp.start()
        x, y, c, _ = _place()
        me, xn, yn, dg = 2 * x + y, 2 * (1 - x) + y, 2 * x + (1 - y), 2 * (1 - x) + (1 - y)
        to_x, to_y = (1 - x, y, c), (x, 1 - y, c)
        sends = []

        def send(k, piece, to):
            cp = _remote(k, piece, piece, to, ss, rs)
            cp.start()
            sends.append(cp)

        def arrived(k, piece, frm):
            _remote(k, piece, piece, frm, ss, rs).wait_recv()

        rows = []
        for a in range(n):
            half = outs[a].shape[1] // 2
            rows.append((pl.ds(c * half, half), pl.ds(c * half, half // 2), pl.ds(c * half + half // 2, half // 2)))
            send(4 * a, outs[a].at[me, rows[a][0]], to_x)
            send(4 * a + 1, outs[a].at[me, rows[a][0]], to_y)
        for a in range(n):
            mine, lo, hi = rows[a]
            arrived(4 * a, outs[a].at[xn, mine], to_x)
            send(4 * a + 2, outs[a].at[xn, lo], to_y)
            arrived(4 * a + 1, outs[a].at[yn, mine], to_y)
            send(4 * a + 3, outs[a].at[yn, hi], to_x)
        for a in range(n):
            mine, lo, hi = rows[a]
            arrived(4 * a + 2, outs[a].at[dg, lo], to_y)
            arrived(4 * a + 3, outs[a].at[dg, hi], to_x)
        for cp in sends:
            cp.wait_send()
        for wait in beside_waits:
            wait()

    aliases = {a: a for a in range(n)}
    aliases.update({n + k: n + v for k, v in also.aliases.items()})
    nsem = 4 * n + also.nsem
    res = pl.pallas_call(
        body, name="gather_ici_first", in_specs=[ANY] * (n + m), out_specs=[ANY] * (n + len(also.out_shapes)),
        out_shape=_same(bases) + also.out_shapes, input_output_aliases=aliases,
        scratch_shapes=[pltpu.SemaphoreType.DMA((nsem,)), pltpu.SemaphoreType.DMA((nsem,))],
    )(*bases, *also.ins)
    return list(res[:n]), list(res[n:])


def _ride_gather_d2d(bases):
    n = len(bases)

    def plan(ins, outs, ss, rs):
        x, y, c, chips = _place()
        sib = (x, y, 1 - c)
        starts, waits = [], []
        for a in range(n):
            half = outs[a].shape[1] // 2
            for j, chip in enumerate(chips):
                frm = 2 * chip[0] + chip[1]
                got = outs[a].at[frm, pl.ds(c * half, half)]
                theirs = outs[a].at[frm, pl.ds((1 - c) * half, half)]
                cp = _remote(3 * a + j, got, got, sib, ss, rs)
                starts.append(cp)
                waits += [cp.wait_send, _remote(3 * a + j, theirs, theirs, sib, ss, rs).wait_recv]
        return starts, waits

    return Rider(plan, bases, _same(bases), {a: a for a in range(n)}, 3 * n)


def _ride_gather_small(bases):
    n = len(bases)

    def plan(ins, outs, ss, rs):
        x, y, c, chips = _place()
        me = 2 * x + y
        starts, waits = [], []
        for a in range(n):
            k = outs[a].shape[-1] // N_CHIPS
            lead = (slice(None),) * (len(outs[a].shape) - 1)
            at = (lambda s: pl.multiple_of(s * k, LANES)) if k % LANES == 0 else (lambda s: s * k)
            cols = lambda s: outs[a].at[lead + (pl.ds(at(s), k),)]
            for j, chip in enumerate(chips):
                cp = _remote(3 * a + j, cols(me), cols(me), (*chip, c), ss, rs)
                got = cols(2 * chip[0] + chip[1])
                starts.append(cp)
                waits += [cp.wait_send, _remote(3 * a + j, got, got, (*chip, c), ss, rs).wait_recv]
        return starts, waits

    return Rider(plan, bases, _same(bases), {a: a for a in range(n)}, 3 * n)


def _ride_swap_halves(gs):
    n = len(gs)

    def plan(ins, outs, ss, rs):
        x, y, c, _ = _place()
        cps = [_remote(a, ins[a].at[:, 1 - c], outs[a], (x, y, 1 - c), ss, rs) for a in range(n)]
        return cps, [cp.wait for cp in cps]

    shapes = [jax.ShapeDtypeStruct((g.shape[0],) + g.shape[2:], g.dtype) for g in gs]
    return Rider(plan, gs, shapes, {}, n)


def _ride_to_owners(ps):
    n = len(ps)

    def plan(ins, outs, ss, rs):
        x, y, c, chips = _place()
        cps = []
        for a in range(n):
            for j, chip in enumerate(chips):
                cps.append(_remote(3 * a + j, ins[a].at[2 * chip[0] + chip[1]], outs[a].at[j], (*chip, c), ss, rs))
        return cps, [cp.wait for cp in cps]

    shapes = [jax.ShapeDtypeStruct((3,) + p.shape[1:], p.dtype) for p in ps]
    return Rider(plan, ps, shapes, {}, 3 * n)


def _ride_join_halves(qs):
    n = len(qs)

    def plan(ins, outs, ss, rs):
        x, y, c, _ = _place()
        sib = (x, y, 1 - c)
        starts, waits = [], []
        for a in range(n):
            cp = _remote(a, outs[a].at[c], outs[a].at[c], sib, ss, rs)
            starts.append(cp)
            waits += [cp.wait_send, _remote(a, outs[a].at[1 - c], outs[a].at[1 - c], sib, ss, rs).wait_recv]
        return starts, waits

    return Rider(plan, qs, _same(qs), {a: a for a in range(n)}, n)


def _gather_all(base):
    def body(in_ref, out_ref, ss, rs):
        x, y, c, chips = _place()
        sib = (x, y, 1 - c)
        block = lambda cx, cy, cc: out_ref.at[4 * cx + 2 * cy + cc]
        mine = block(x, y, c)
        first = [_remote(j, mine, mine, (*chip, c), ss, rs) for j, chip in enumerate(chips)]
        first.append(_remote(3, mine, mine, sib, ss, rs))
        for cp in first:
            cp.start()
        passed = []
        for j, chip in enumerate(chips):
            got = block(*chip, c)
            _remote(j, got, got, (*chip, c), ss, rs).wait_recv()
            passed.append(_remote(4 + j, got, got, sib, ss, rs))
            passed[-1].start()
        theirs = block(x, y, 1 - c)
        _remote(3, theirs, theirs, sib, ss, rs).wait_recv()
        for j, chip in enumerate(chips):
            got = block(*chip, 1 - c)
            _remote(4 + j, got, got, sib, ss, rs).wait_recv()
        for cp in first + passed:
            cp.wait_send()

    return pl.pallas_call(
        body, name="small_grads_gather_all", in_specs=[ANY], out_specs=ANY,
        out_shape=jax.ShapeDtypeStruct(base.shape, base.dtype), input_output_aliases={0: 0},
        scratch_shapes=[pltpu.SemaphoreType.DMA((N_DEV - 1,)), pltpu.SemaphoreType.DMA((N_DEV - 1,))],
    )(base)


def _embed(x, meta, dm, rider=None):
    dc = _col_tile(dm.D, 256)
    s, lp = dm.S, dm.Lp

    def body(x_ref, meta_ref, h_ref):
        h_ref[0:N_META, :] = meta_ref[...]
        h_ref[N_META:N_META + s, :] = x_ref[0]
        if lp > N_META + s:
            h_ref[N_META + s:lp, :] = jnp.zeros((lp - N_META - s, dc), F32)

    (h,), rode = _call(
        body, "embed", (dm.BL, dm.D // dc),
        [pl.BlockSpec((1, s, dc), lambda b, j: (b, 0, j)), pl.BlockSpec((N_META, dc), lambda b, j: (0, j))],
        [pl.BlockSpec((lp, dc), lambda b, j: (b, j))], [jax.ShapeDtypeStruct((dm.R, dm.D), F32)],
        [], ("parallel", "parallel"), (x, meta), rider)
    return h, rode


def _loss_head(h, target, dm):
    dc = _col_tile(dm.D, 256)
    s, lp, nj = dm.S, dm.Lp, dm.D // dc

    def body(h_ref, t_ref, dh_ref, l_ref):
        diff = h_ref[N_META:N_META + s, :] - t_ref[0]
        dh_ref[0:N_META, :] = jnp.zeros((N_META, dc), F32)
        dh_ref[N_META:N_META + s, :] = diff * (1.0 / dm.D)
        if lp > N_META + s:
            dh_ref[N_META + s:lp, :] = jnp.zeros((lp - N_META - s, dc), F32)
        l_ref[...] = jnp.full((8, LANES), (0.5 / dm.D) * jnp.sum(diff * diff), F32)

    dh, part = pl.pallas_call(
        body, name="loss_head", grid=(dm.BL, nj),
        in_specs=[pl.BlockSpec((lp, dc), lambda b, j: (b, j)),
                  pl.BlockSpec((1, s, dc), lambda b, j: (b, 0, j))],
        out_specs=[pl.BlockSpec((lp, dc), lambda b, j: (b, j)),
                   pl.BlockSpec((8, LANES), lambda b, j: (b * nj + j, 0))],
        out_shape=[jax.ShapeDtypeStruct((dm.R, dm.D), F32),
                   jax.ShapeDtypeStruct((dm.BL * nj * 8, LANES), F32)],
        compiler_params=_params(("parallel", "parallel")),
    )(h, target)
    return dh, jnp.sum(part[::8, 0])


def _unembed(dh, dm):
    dc = _col_tile(dm.D, 256)
    s, lp = dm.S, dm.Lp

    def body(dh_ref, gx_ref, gm_ref):
        gx_ref[0] = dh_ref[N_META:N_META + s, :]

        @pl.when(pl.program_id(1) == 0)
        def _():
            gm_ref[...] = dh_ref[0:N_META, :]

        @pl.when(pl.program_id(1) > 0)
        def _():
            gm_ref[...] = gm_ref[...] + dh_ref[0:N_META, :]

    return pl.pallas_call(
        body, name="unembed", grid=(dm.D // dc, dm.BL),
        in_specs=[pl.BlockSpec((lp, dc), lambda j, b: (b, j))],
        out_specs=[pl.BlockSpec((1, s, dc), lambda j, b: (b, 0, j)),
                   pl.BlockSpec((N_META, dc), lambda j, b: (0, j))],
        out_shape=[jax.ShapeDtypeStruct((dm.BL, s, dm.D), F32),
                   jax.ShapeDtypeStruct((N_META, dm.D), F32)],
        compiler_params=_params(("parallel", "arbitrary")),
    )(dh)


def _fwd_in(h, pre_g, w, dm, rider=None):
    tm = _row_tile(dm.R, 1088)
    tn = _col_tile(dm.NP, 5 * MXU_DIM)
    nj = dm.NP // tn

    def body(h_ref, g_ref, w_ref, wdt_ref, proj_ref, dt_ref, hn_ref):
        @pl.when(pl.program_id(1) == 0)
        def _():
            xf = h_ref[...]
            r = lax.rsqrt(jnp.mean(xf * xf, axis=-1, keepdims=True) + NORM_EPS)
            hn_ref[...] = (xf * r * g_ref[...]).astype(BF16)
            dt_ref[...] = _dot(hn_ref[...], wdt_ref[...])

        proj_ref[...] = _dot(hn_ref[...], w_ref[...]).astype(BF16)

    return _call(
        body, "fwd_in", (dm.R // tm, nj),
        [pl.BlockSpec((tm, dm.D), lambda i, j: (i, 0)),
         pl.BlockSpec((1, dm.D), lambda i, j: (0, 0)),
         pl.BlockSpec((dm.D, tn), lambda i, j: (0, j)),
         pl.BlockSpec((dm.D, DT_PAD), lambda i, j: (0, dm.DT0 // DT_PAD))],
        [pl.BlockSpec((tm, tn), lambda i, j: (i, j)),
         pl.BlockSpec((tm, DT_PAD), lambda i, j: (i, 0)),
         pl.BlockSpec((tm, dm.D), lambda i, j: (i, 0))],
        [jax.ShapeDtypeStruct((dm.R, dm.NP), BF16), jax.ShapeDtypeStruct((dm.R, DT_PAD), F32),
         jax.ShapeDtypeStruct((dm.R, dm.D), BF16)],
        [], ("parallel", "arbitrary"), (h, pre_g, w, w), rider)


def _fwd_out(ya, yb, yc, w_out, h, post_g, dm, rider=None):
    tm = _row_tile(dm.Lp, 544)
    tiles_per_seq = dm.Lp // tm
    da, db, dc = dm.DA, dm.DB, dm.DC

    def body(ya_ref, yb_ref, yc_ref, w_ref, h_ref, g_ref, hn_ref, m_ref):
        m = _dot(ya_ref[...], w_ref[0:da, :])
        m = m + _dot(yb_ref[...], w_ref[da:da + db, :])
        m = m + _dot(yc_ref[...], w_ref[da + db:da + db + dc, :])
        m_ref[...] = m
        r = lax.rsqrt(jnp.mean(m * m, axis=-1, keepdims=True) + NORM_EPS)
        t = (pl.program_id(0) % tiles_per_seq) * tm + lax.broadcasted_iota(jnp.int32, (tm, 1), 0)
        keep = (t < dm.L).astype(F32)
        hn_ref[...] = (h_ref[...] + m * r * g_ref[...]) * keep

    row = lambda i: (i, 0)
    fixed = lambda i: (0, 0)
    return _call(
        body, "fwd_out", (dm.R // tm,),
        [pl.BlockSpec((tm, da), row), pl.BlockSpec((tm, db), row), pl.BlockSpec((tm, dc), row),
         pl.BlockSpec((2 * dm.D, dm.D), fixed), pl.BlockSpec((tm, dm.D), row), pl.BlockSpec((1, dm.D), fixed)],
        [pl.BlockSpec((tm, dm.D), row), pl.BlockSpec((tm, dm.D), row)],
        [jax.ShapeDtypeStruct((dm.R, dm.D), F32), jax.ShapeDtypeStruct((dm.R, dm.D), F32)],
        [], ("parallel",), (ya, yb, yc, w_out, h, post_g), rider)


def _bwd_out(dh, m, post_g, w_out, ya, yb, yc, dm, rider=None):
    tm = _row_tile(dm.R, MXU_DIM)
    da, db, dc = dm.DA, dm.DB, dm.DC

    def body(dh_ref, m_ref, g_ref, w_ref, ya_ref, yb_ref, yc_ref, dya_ref, dyb_ref, dyc_ref, dw_ref, dg_ref):
        @pl.when(pl.program_id(0) == 0)
        def _():
            dw_ref[...] = jnp.zeros_like(dw_ref)
            dg_ref[...] = jnp.zeros_like(dg_ref)

        m = m_ref[...]
        dh_ = dh_ref[...]
        r = lax.rsqrt(jnp.mean(m * m, axis=-1, keepdims=True) + NORM_EPS)
        n = m * r
        dg_ref[0:1, :] = dg_ref[0:1, :] + jnp.sum(dh_ * n, axis=0, keepdims=True)
        dn = dh_ * g_ref[...]
        dm_ = (r * (dn - n * jnp.mean(dn * n, axis=-1, keepdims=True))).astype(BF16)
        dya_ref[...] = _dot_nt(dm_, w_ref[0:da, :])
        dyb_ref[...] = _dot_nt(dm_, w_ref[da:da + db, :])
        dyc_ref[...] = _dot_nt(dm_, w_ref[da + db:da + db + dc, :])
        dw_ref[0:da, :] = dw_ref[0:da, :] + _dot_tn(ya_ref[...], dm_)
        dw_ref[da:da + db, :] = dw_ref[da:da + db, :] + _dot_tn(yb_ref[...], dm_)
        dw_ref[da + db:da + db + dc, :] = dw_ref[da + db:da + db + dc, :] + _dot_tn(yc_ref[...], dm_)

    row = lambda i: (i, 0)
    fixed = lambda i: (0, 0)
    return _call(
        body, "bwd_out", (dm.R // tm,),
        [pl.BlockSpec((tm, dm.D), row), pl.BlockSpec((tm, dm.D), row), pl.BlockSpec((1, dm.D), fixed),
         pl.BlockSpec((2 * dm.D, dm.D), fixed),
         pl.BlockSpec((tm, da), row), pl.BlockSpec((tm, db), row), pl.BlockSpec((tm, dc), row)],
        [pl.BlockSpec((tm, da), row), pl.BlockSpec((tm, db), row), pl.BlockSpec((tm, dc), row),
         pl.BlockSpec((2 * dm.D, dm.D), fixed), pl.BlockSpec((8, dm.D), fixed)],
        [jax.ShapeDtypeStruct((dm.R, da), F32), jax.ShapeDtypeStruct((dm.R, db), F32),
         jax.ShapeDtypeStruct((dm.R, dc), F32),
         jax.ShapeDtypeStruct((2 * dm.D, dm.D), F32), jax.ShapeDtypeStruct((8, dm.D), F32)],
        [], ("arbitrary",), (dh, m, post_g, w_out, ya, yb, yc), rider)


def _bwd_in_dx(dpa, dpb, dpc, dpt, w, h, dh, pre_g, dm, rider=None):
    tm = _row_tile(dm.R, 272)
    wa, wb, wc = dm.WA, dm.WB, dm.WC

    def body(dpa_ref, dpb_ref, dpc_ref, dpt_ref, w_ref, h_ref, dh_ref, g_ref, out_ref, dg_ref):
        @pl.when(pl.program_id(0) == 0)
        def _():
            dg_ref[...] = jnp.zeros_like(dg_ref)

        dhn = _dot_nt(dpa_ref[...], w_ref[:, 0:wa])
        dhn = dhn + _dot_nt(dpb_ref[...], w_ref[:, wa:wa + wb])
        dhn = dhn + _dot_nt(dpc_ref[...], w_ref[:, wa + wb:wa + wb + wc])
        dhn = dhn + _dot_nt(dpt_ref[...], w_ref[:, wa + wb + wc:wa + wb + wc + DT_PAD])
        xf = h_ref[...]
        r = lax.rsqrt(jnp.mean(xf * xf, axis=-1, keepdims=True) + NORM_EPS)
        n = xf * r
        dg_ref[0:1, :] = dg_ref[0:1, :] + jnp.sum(dhn * n, axis=0, keepdims=True)
        dn = dhn * g_ref[...]
        out_ref[...] = dh_ref[...] + r * (dn - n * jnp.mean(dn * n, axis=-1, keepdims=True))

    row = lambda i: (i, 0)
    fixed = lambda i: (0, 0)
    return _call(
        body, "bwd_in_dx", (dm.R // tm,),
        [pl.BlockSpec((tm, wa), row), pl.BlockSpec((tm, wb), row), pl.BlockSpec((tm, wc), row),
         pl.BlockSpec((tm, DT_PAD), row), pl.BlockSpec((dm.D, dm.NP), fixed),
         pl.BlockSpec((tm, dm.D), row), pl.BlockSpec((tm, dm.D), row), pl.BlockSpec((1, dm.D), fixed)],
        [pl.BlockSpec((tm, dm.D), row), pl.BlockSpec((8, dm.D), fixed)],
        [jax.ShapeDtypeStruct((dm.R, dm.D), F32), jax.ShapeDtypeStruct((8, dm.D), F32)],
        [], ("arbitrary",), (dpa, dpb, dpc, dpt, w, h, dh, pre_g), rider)


def _bwd_in_dw(hn, dps, dm):
    widest = max(dp.shape[1] for dp in dps)
    tn = [_col_tile(dp.shape[1], 2 * MXU_DIM if dp.shape[1] == widest else MXU_DIM) for dp in dps]
    nb = [dp.shape[1] // t for dp, t in zip(dps, tn)]
    first = [sum(nb[:p]) for p in range(len(dps))]
    at = lambda p: (lambda j: (0, jnp.clip(j - first[p], 0, nb[p] - 1)))

    def body(hn_ref, *refs):
        j = pl.program_id(0)
        for p in range(len(dps)):
            @pl.when(jnp.logical_and(j >= first[p], j < first[p] + nb[p]))
            def _(p=p):
                refs[len(dps) + p][...] = _dot_tn(hn_ref[...], refs[p][...])

    return pl.pallas_call(
        body, name="bwd_in_dw", grid=(sum(nb),),
        in_specs=[pl.BlockSpec((dm.R, dm.D), lambda j: (0, 0))] + [
            pl.BlockSpec((dm.R, tn[p]), at(p)) for p in range(len(dps))],
        out_specs=[pl.BlockSpec((dm.D, tn[p]), at(p)) for p in range(len(dps))],
        out_shape=[jax.ShapeDtypeStruct((dm.D, dp.shape[1]), F32) for dp in dps],
        compiler_params=_params(("arbitrary",)),
    )(hn, *dps)


def _tile_index(dm, reverse):
    if reverse:
        return lambda b, i: b * dm.NT + (dm.NT - 1 - i)
    return lambda b, i: b * dm.NT + i


def _halo_index(dm, rows):
    per_tile = TT // rows
    return lambda b, i: jnp.maximum((b * dm.NT + (dm.NT - 1 - i)) * per_tile - 1, 0)


HALO_BLOCK = 16


def _last_rows(x):
    return x.astype(F32)[HALO_BLOCK - SMALL_HALO:HALO_BLOCK]


MIX_A_ROWS = 288


def _mix_a_fwd(proj, conv_w, dm):
    da = dm.DA
    ta = _row_tile(dm.Lp, MIX_A_ROWS)
    nta = dm.Lp // ta
    bl = dm.BL

    def setup(ab_ref, ac_ref, ax_ref, az_ref, w_ref, y_ref, pbuf):
        i = pl.program_id(0)

        @pl.when(i == 0)
        def _():
            pbuf[0:SMALL_HALO, :] = jnp.zeros((SMALL_HALO, da), F32)

        @pl.when(i > 0)
        def _():
            pbuf[0:SMALL_HALO, :] = pbuf[ta:ta + SMALL_HALO, :]

    def body(ab_ref, ac_ref, ax_ref, az_ref, w_ref, y_ref, pbuf):
        for lb in range(da // LANES):
            cs = slice(lb * LANES, (lb + 1) * LANES)
            p = ac_ref[:, cs].astype(F32) * ax_ref[:, cs].astype(F32)
            pbuf[SMALL_HALO:SMALL_HALO + ta, cs] = p
            q = (w_ref[0:1, cs] * pbuf[6:6 + ta, cs] + w_ref[1:2, cs] * pbuf[7:7 + ta, cs] + w_ref[2:3, cs] * p)
            az = az_ref[:, cs].astype(F32)
            y_ref[:, cs] = (ab_ref[:, cs].astype(F32) * q * (az * _sigmoid(az))).astype(BF16)
            yield

    proj3 = proj.reshape(bl, dm.Lp, dm.NP)
    col = lambda k: pl.BlockSpec((bl, ta, da), lambda i: (0, i, k))
    return pl.pallas_call(
        _per_sequence(setup, body, bl, [True] * 4 + [False] + [True, True]), name="mix_a_fwd", grid=(nta,),
        in_specs=[col(0), col(1), col(2), col(3), pl.BlockSpec((CONV_A_K, da), lambda i: (0, 0))],
        out_specs=col(0),
        out_shape=jax.ShapeDtypeStruct((bl, dm.Lp, da), BF16),
        scratch_shapes=[pltpu.VMEM((bl, SMALL_HALO + ta, da), F32)],
        compiler_params=_params(("arbitrary",)),
    )(proj3, proj3, proj3, proj3, conv_w).reshape(dm.R, da)


def _mix_a_bwd(proj, dya, conv_w, dm):
    da = dm.DA
    ta = _row_tile(dm.Lp, MIX_A_ROWS)
    nta = dm.Lp // ta
    bl = dm.BL

    def setup(ab_ref, ac_ref, ax_ref, az_ref, ach_ref, axh_ref, dy_ref, w_ref, dp_ref, dw_ref, pbuf, dqbuf):
        i = pl.program_id(0)

        @pl.when(i == 0)
        def _():
            dw_ref[...] = jnp.zeros_like(dw_ref)
            dqbuf[ta:ta + SMALL_HALO, :] = jnp.zeros((SMALL_HALO, da), F32)

        @pl.when(i > 0)
        def _():
            dqbuf[ta:ta + SMALL_HALO, :] = dqbuf[0:SMALL_HALO, :]

    def body(ab_ref, ac_ref, ax_ref, az_ref, ach_ref, axh_ref, dy_ref, w_ref, dp_ref, dw_ref, pbuf, dqbuf):
        halo_on = jnp.where(pl.program_id(0) == nta - 1, 0.0, 1.0)
        for lb in range(da // LANES):
            cs = slice(lb * LANES, (lb + 1) * LANES)
            pbuf[0:SMALL_HALO, cs] = (_last_rows(ach_ref[:, cs]) * _last_rows(axh_ref[:, cs])) * halo_on
            ac, ax, ab, az = (r[:, cs].astype(F32) for r in (ac_ref, ax_ref, ab_ref, az_ref))
            p = ac * ax
            pbuf[SMALL_HALO:SMALL_HALO + ta, cs] = p
            p1 = pbuf[7:7 + ta, cs]
            p2 = pbuf[6:6 + ta, cs]
            w0, w1, w2 = w_ref[0:1, cs], w_ref[1:2, cs], w_ref[2:3, cs]
            q = w0 * p2 + w1 * p1 + w2 * p
            sz, dsz = _silu_and_grad(az)
            dy = dy_ref[:, cs]
            t1 = dy * ab
            dq = t1 * sz
            dqbuf[0:ta, cs] = dq
            dpv = w2 * dq + w1 * dqbuf[1:1 + ta, cs] + w0 * dqbuf[2:2 + ta, cs]
            dp_ref[:, lb * LANES:(lb + 1) * LANES] = (dy * q * sz).astype(BF16)
            dp_ref[:, da + lb * LANES:da + (lb + 1) * LANES] = (dpv * ax).astype(BF16)
            dp_ref[:, 2 * da + lb * LANES:2 * da + (lb + 1) * LANES] = (dpv * ac).astype(BF16)
            dp_ref[:, 3 * da + lb * LANES:3 * da + (lb + 1) * LANES] = (t1 * q * dsz).astype(BF16)
            dw_ref[0, 0:1, cs] = dw_ref[0, 0:1, cs] + jnp.sum(dq * p2, axis=0, keepdims=True)
            dw_ref[0, 1:2, cs] = dw_ref[0, 1:2, cs] + jnp.sum(dq * p1, axis=0, keepdims=True)
            dw_ref[0, 2:3, cs] = dw_ref[0, 2:3, cs] + jnp.sum(dq * p, axis=0, keepdims=True)
            yield

    proj3 = proj.reshape(bl, dm.Lp, dm.NP)
    col = lambda w, k: pl.BlockSpec((bl, ta, w), lambda i: (0, nta - 1 - i, k))
    halo = lambda k: pl.BlockSpec((bl, HALO_BLOCK, da),
                                  lambda i: (0, jnp.maximum((nta - 1 - i) * (ta // HALO_BLOCK) - 1, 0), k))
    dp, dw = pl.pallas_call(
        _per_sequence(setup, body, bl, [True] * 7 + [False] + [True, "keep"] + [True, True]),
        name="mix_a_bwd", grid=(nta,),
        in_specs=[col(da, 0), col(da, 1), col(da, 2), col(da, 3), halo(1), halo(2), col(da, 0),
                  pl.BlockSpec((CONV_A_K, da), lambda i: (0, 0))],
        out_specs=[col(dm.WA, 0), pl.BlockSpec((bl, 8, da), lambda i: (0, 0, 0))],
        out_shape=[jax.ShapeDtypeStruct((bl, dm.Lp, dm.WA), BF16), jax.ShapeDtypeStruct((bl, 8, da), F32)],
        scratch_shapes=[pltpu.VMEM((bl, SMALL_HALO + ta, da), F32), pltpu.VMEM((bl, ta + SMALL_HALO, da), F32)],
        compiler_params=_params(("arbitrary",)),
    )(proj3, proj3, proj3, proj3, proj3, proj3, dya.reshape(bl, dm.Lp, da), conv_w)
    return dp.reshape(dm.R, dm.WA), dw


SUBLANES = 8
SHIFT_ROWS = TT + CONF_HALO - SUBLANES


TAP_ROWS = 64


def _split_lanes(buf, rows, val):
    for lb in range(val.shape[1] // LANES):
        buf[lb, rows, :] = val[:, lb * LANES:(lb + 1) * LANES]


def _join_lanes(buf):
    return jnp.concatenate([buf[lb] for lb in range(buf.shape[0])], axis=1)


def _fill_shifted(buf, shifted):
    def step(lb, carry):
        for r in range(1, SUBLANES):
            shifted[lb, r - 1, 0:SHIFT_ROWS, :] = buf[lb, r:r + SHIFT_ROWS, :]
        return carry

    lax.fori_loop(0, buf.shape[0], step, 0)


def _window(buf, shifted, d, r0, lb):
    r = d % SUBLANES
    rows = pl.ds(pl.multiple_of(r0 + (d - r), SUBLANES), TAP_ROWS)
    return buf[lb, rows, :] if r == 0 else shifted[lb, r - 1, rows, :]


def _tap_loop(nlb, body):
    per_lb = TT // TAP_ROWS

    def step(it, carry):
        lb = it // per_lb
        body(lb, pl.ds(pl.multiple_of(lb * LANES, LANES), LANES), pl.multiple_of((it % per_lb) * TAP_ROWS, TAP_ROWS))
        return carry

    lax.fori_loop(0, nlb * per_lb, step, 0)


TAP_CHAINS = 4


def _tree_sum(terms):
    sums = list(terms[:TAP_CHAINS])
    for n, t in enumerate(terms[TAP_CHAINS:]):
        sums[n % TAP_CHAINS] = sums[n % TAP_CHAINS] + t
    while len(sums) > 1:
        sums = [a + b for a, b in zip(sums[0::2], sums[1::2])] + ([sums[-1]] if len(sums) % 2 else [])
    return sums[0]


def _conf_conv(ubuf, ushift, w_ref, b_ref, u1buf):
    _fill_shifted(ubuf, ushift)

    def piece(lb, cs, r0):
        taps = [w_ref[k:k + 1, cs] * _window(ubuf, ushift, CONF_HALO - (CONF_K - 1) + k, r0, lb)
                for k in range(CONF_K)]
        u1buf[lb, pl.ds(r0, TAP_ROWS), :] = _tree_sum(taps) + b_ref[0:1, cs]

    _tap_loop(ubuf.shape[0], piece)


def _mix_c_fwd(proj, conv_w, conv_b, ln_g, ln_b, dm):
    dc = dm.DC
    nlb = dc // LANES
    c0 = (dm.WA + dm.WB) // dc
    ti = _tile_index(dm, False)

    def body(ca_ref, cg_ref, cz_ref, w_ref, b_ref, g_ref, be_ref, y_ref, u1_ref, ubuf, u1buf, ushift):
        i = pl.program_id(1)

        @pl.when(i == 0)
        def _():
            ubuf[:, 0:CONF_HALO, :] = jnp.zeros((nlb, CONF_HALO, LANES), F32)

        @pl.when(i > 0)
        def _():
            ubuf[:, 0:CONF_HALO, :] = ubuf[:, TT:TT + CONF_HALO, :]

        _split_lanes(ubuf, slice(CONF_HALO, CONF_HALO + TT),
                     ca_ref[...].astype(F32) * _sigmoid(cg_ref[...].astype(F32)))
        _conf_conv(ubuf, ushift, w_ref, b_ref, u1buf)
        u1 = _join_lanes(u1buf)
        u1_ref[...] = u1
        mu = jnp.mean(u1, axis=-1, keepdims=True)
        xc = u1 - mu
        rstd = lax.rsqrt(jnp.mean(xc * xc, axis=-1, keepdims=True) + LN_EPS)
        u2 = xc * rstd * g_ref[...] + be_ref[...]
        cz = cz_ref[...].astype(F32)
        y_ref[...] = ((u2 * _sigmoid(u2)) * (cz * _sigmoid(cz))).astype(BF16)

    col = lambda k: pl.BlockSpec((TT, dc), lambda b, i: (ti(b, i), c0 + k))
    vec = pl.BlockSpec((1, dc), lambda b, i: (0, 0))
    return pl.pallas_call(
        body, name="mix_c_fwd", grid=(dm.BL, dm.NT),
        in_specs=[col(0), col(1), col(2), pl.BlockSpec((CONF_K, dc), lambda b, i: (0, 0)), vec, vec, vec],
        out_specs=[pl.BlockSpec((TT, dc), lambda b, i: (ti(b, i), 0))] * 2,
        out_shape=[jax.ShapeDtypeStruct((dm.R, dc), BF16), jax.ShapeDtypeStruct((dm.R, dc), F32)],
        scratch_shapes=[pltpu.VMEM((nlb, CONF_HALO + TT, LANES), F32), pltpu.VMEM((nlb, TT, LANES), F32),
                        pltpu.VMEM((nlb, SUBLANES - 1, SHIFT_ROWS, LANES), F32)],
        compiler_params=_params(("parallel", "arbitrary")),
    )(proj, proj, proj, conv_w, conv_b, ln_g, ln_b)


def _mix_c_bwd(proj, u1, dyc, conv_w, ln_g, ln_b, dm):
    dc = dm.DC
    nlb = dc // LANES
    c0 = (dm.WA + dm.WB) // dc
    ti = _tile_index(dm, True)
    hi = _halo_index(dm, CONF_HALO)

    def body(ca_ref, cg_ref, cz_ref, cah_ref, cgh_ref, u1_ref, dy_ref, w_ref, g_ref, be_ref,
             dp_ref, dw_ref, dv_ref, ubuf, dubuf, du0buf, ushift, dshift, dwacc):
        i = pl.program_id(1)
        halo_on = jnp.where(i == dm.NT - 1, 0.0, 1.0)

        @pl.when(i == 0)
        def _():
            dwacc[...] = jnp.zeros_like(dwacc)
            dv_ref[...] = jnp.zeros_like(dv_ref)
            dubuf[:, TT:TT + CONF_HALO, :] = jnp.zeros((nlb, CONF_HALO, LANES), F32)

        @pl.when(i > 0)
        def _():
            dubuf[:, TT:TT + CONF_HALO, :] = dubuf[:, 0:CONF_HALO, :]

        _split_lanes(ubuf, slice(0, CONF_HALO),
                     cah_ref[...].astype(F32) * _sigmoid(cgh_ref[...].astype(F32)) * halo_on)
        sgg = _sigmoid(cg_ref[...].astype(F32))
        ca = ca_ref[...].astype(F32)
        _split_lanes(ubuf, slice(CONF_HALO, CONF_HALO + TT), ca * sgg)
        _fill_shifted(ubuf, ushift)
        u1 = u1_ref[...]
        mu = jnp.mean(u1, axis=-1, keepdims=True)
        xc = u1 - mu
        rstd = lax.rsqrt(jnp.mean(xc * xc, axis=-1, keepdims=True) + LN_EPS)
        xhat = xc * rstd
        u2 = xhat * g_ref[...] + be_ref[...]
        su, dsu = _silu_and_grad(u2)
        sz, dsz = _silu_and_grad(cz_ref[...].astype(F32))
        dy = dy_ref[...]
        du2 = dy * dsu * sz
        dp_ref[:, 2 * dc:3 * dc] = (dy * su * dsz).astype(BF16)
        dxhat = du2 * g_ref[...]
        du1 = rstd * (dxhat - jnp.mean(dxhat, axis=-1, keepdims=True)
                      - xhat * jnp.mean(dxhat * xhat, axis=-1, keepdims=True))
        dv_ref[0, 0:1, :] = dv_ref[0, 0:1, :] + jnp.sum(du1, axis=0, keepdims=True)
        dv_ref[0, 1:2, :] = dv_ref[0, 1:2, :] + jnp.sum(du2 * xhat, axis=0, keepdims=True)
        dv_ref[0, 2:3, :] = dv_ref[0, 2:3, :] + jnp.sum(du2, axis=0, keepdims=True)
        _split_lanes(dubuf, slice(0, TT), du1)
        _fill_shifted(dubuf, dshift)

        def piece(lb, cs, r0):
            du0buf[lb, pl.ds(r0, TAP_ROWS), :] = _tree_sum(
                [w_ref[k:k + 1, cs] * _window(dubuf, dshift, CONF_K - 1 - k, r0, lb) for k in range(CONF_K)])
            d1 = dubuf[lb, pl.ds(r0, TAP_ROWS), :]
            for k in range(CONF_K):
                prod = d1 * _window(ubuf, ushift, CONF_HALO - (CONF_K - 1) + k, r0, lb)
                dwacc[lb, k] = dwacc[lb, k] + jnp.sum(prod.reshape(TAP_ROWS // SUBLANES, SUBLANES, LANES), axis=0)

        _tap_loop(nlb, piece)
        du0 = _join_lanes(du0buf)
        dp_ref[:, 0:dc] = (du0 * sgg).astype(BF16)
        dp_ref[:, dc:2 * dc] = (du0 * ca * sgg * (1.0 - sgg)).astype(BF16)

        @pl.when(i == dm.NT - 1)
        def _():
            for lb in range(nlb):
                dw_ref[0, 0:CONF_K, lb * LANES:(lb + 1) * LANES] = jnp.sum(dwacc[lb], axis=1)
            dw_ref[0, CONF_K:CONF_K + 1, :] = jnp.zeros((1, dc), F32)

    col = lambda k: pl.BlockSpec((TT, dc), lambda b, i: (ti(b, i), c0 + k))
    halo = lambda k: pl.BlockSpec((CONF_HALO, dc), lambda b, i: (hi(b, i), c0 + k))
    vec = pl.BlockSpec((1, dc), lambda b, i: (0, 0))
    return pl.pallas_call(
        body, name="mix_c_bwd", grid=(dm.BL, dm.NT),
        in_specs=[col(0), col(1), col(2), halo(0), halo(1),
                  pl.BlockSpec((TT, dc), lambda b, i: (ti(b, i), 0)),
                  pl.BlockSpec((TT, dc), lambda b, i: (ti(b, i), 0)),
                  pl.BlockSpec((CONF_K, dc), lambda b, i: (0, 0)), vec, vec],
        out_specs=[pl.BlockSpec((TT, dm.WC), lambda b, i: (ti(b, i), 0)),
                   pl.BlockSpec((1, 32, dc), lambda b, i: (b, 0, 0)),
                   pl.BlockSpec((1, 8, dc), lambda b, i: (b, 0, 0))],
        out_shape=[jax.ShapeDtypeStruct((dm.R, dm.WC), BF16),
                   jax.ShapeDtypeStruct((dm.BL, 32, dc), F32),
                   jax.ShapeDtypeStruct((dm.BL, 8, dc), F32)],
        scratch_shapes=[pltpu.VMEM((nlb, CONF_HALO + TT, LANES), F32),
                        pltpu.VMEM((nlb, TT + CONF_HALO, LANES), F32), pltpu.VMEM((nlb, TT, LANES), F32),
                        pltpu.VMEM((nlb, SUBLANES - 1, SHIFT_ROWS, LANES), F32),
                        pltpu.VMEM((nlb, SUBLANES - 1, SHIFT_ROWS, LANES), F32),
                        pltpu.VMEM((nlb, CONF_K, SUBLANES, LANES), F32)],
        compiler_params=_params(("parallel", "arbitrary")),
    )(proj, proj, proj, proj, proj, u1, dyc, conv_w, ln_g, ln_b)


RAW_HALO = 16


def _shift_matrix():
    r = jnp.arange((SSM_CONV_K - 1) * TT)[:, None]
    want = TT + r % TT - (SSM_CONV_K - 1 - r // TT)
    return (jnp.arange(2 * TT)[None, :] == want).astype(BF16)


def _ssm_conv(rawwin, sh, s_ref, w_ref, b_ref, width):
    sh[...] = _dot(s_ref[...], rawwin[...])
    for lb in range(width // LANES):
        cs = slice(lb * LANES, (lb + 1) * LANES)
        acc = b_ref[0:1, cs] + w_ref[SSM_CONV_K - 1:SSM_CONV_K, cs] * rawwin[TT:2 * TT, cs].astype(F32)
        for k in range(SSM_CONV_K - 1):
            acc = acc + w_ref[k:k + 1, cs] * sh[k * TT:(k + 1) * TT, cs]
        yield cs, acc


def _softplus(z):
    return jnp.maximum(z, 0.0) + jnp.log(1.0 + jnp.exp(-jnp.abs(z)))


def _tri(lower):
    r = lax.broadcasted_iota(jnp.int32, (TT, TT), 0)
    c = lax.broadcasted_iota(jnp.int32, (TT, TT), 1)
    return (c <= r) if lower else (c >= r)


def _exact_01_dot(mat01, x):
    x1, x2, x3 = _split3(x)
    return _dot(mat01, x1) + _dot(mat01, x2) + _dot(mat01, x3)


def _head_scalars(dt_ref, dtb_ref, alog_ref):
    z = dt_ref[...] + dtb_ref[...]
    dtv = _softplus(z)
    a = -jnp.exp(alog_ref[...])
    ac = _exact_01_dot(_tri(True).astype(F32).astype(BF16), dtv * a)
    eac = jnp.exp(ac)
    dst = jnp.exp(ac[TT - 1:TT, :] - ac)
    return z, dtv, a, ac, eac, dst


FAR_BELOW = -1e30


def _decay(ac, ac_t, h, causal):
    return jnp.exp(jnp.where(causal, ac[:, h:h + 1] - ac_t[h:h + 1, :], FAR_BELOW))


def _own_half(x16, h):
    lane = lax.broadcasted_iota(jnp.int32, (1, LANES), 1)
    keep = (lane >= SSM_HEAD_DIM) if (h % 2) else (lane < SSM_HEAD_DIM)
    return jnp.where(keep, x16, jnp.zeros_like(x16))


def _per_sequence(setup, body, bl, how):
    def all_sequences(*refs):
        views = [[r.at[b] if h is True else (r.at[pl.ds(b, 1)] if h == "keep" else r) for r, h in zip(refs, how)]
                 for b in range(bl)]
        for v in views:
            setup(*v)
        running = [body(*v) for v in views]
        while running:
            running = [g for g in running if next(g, "done") != "done"]

    return all_sequences


def _mix_b_fwd(proj, projdt, conv_w, conv_b, dt_bias, a_log, dskx, norm_g, expand, dm, rider=None):
    db, gn, xbc_w, hpg = dm.DB, dm.GN, dm.XBC, dm.HPG
    gw = db // SSM_GROUPS

    def setup(bz_ref, bx_ref, bc_ref, dt_ref, w_ref, b_ref, dtb_ref, alog_ref, dsk_ref, g_ref, e_ref, s_ref,
              y_ref, yraw_ref, sprev_ref, rawwin, sh, xbuf, state, ybuf, exbuf, xdtbuf):
        i = pl.program_id(0)

        @pl.when(i == 0)
        def _():
            rawwin[0:TT, :] = jnp.zeros((TT, xbc_w), BF16)
            state[...] = jnp.zeros_like(state)

        @pl.when(i > 0)
        def _():
            rawwin[TT - RAW_HALO:TT, :] = rawwin[2 * TT - RAW_HALO:2 * TT, :]

    def body(bz_ref, bx_ref, bc_ref, dt_ref, w_ref, b_ref, dtb_ref, alog_ref, dsk_ref, g_ref, e_ref, s_ref,
             y_ref, yraw_ref, sprev_ref, rawwin, sh, xbuf, state, ybuf, exbuf, xdtbuf):
        rawwin[TT:2 * TT, 0:db] = bx_ref[...]
        rawwin[TT:2 * TT, db:xbc_w] = bc_ref[...]
        for cs, pre in _ssm_conv(rawwin, sh, s_ref, w_ref, b_ref, xbc_w):
            xbuf[:, cs] = pre * _sigmoid(pre)
            yield

        _, dtv, _, ac, eac, dst = _head_scalars(dt_ref, dtb_ref, alog_ref)
        exbuf[...] = _dot(jnp.concatenate([dtv, eac, dst], axis=0).astype(BF16), e_ref[...])
        ac_t = ac.T
        causal = _tri(True)
        sprev_ref[0, 0] = state[...]
        yield

        xdtbuf[...] = xbuf[:, 0:db] * exbuf[0:TT, :]
        ybuf[...] = xbuf[:, 0:db] * dsk_ref[...]
        for g in range(SSM_GROUPS):
            gs = slice(g * gw, (g + 1) * gw)
            bg = xbuf[:, db + g * SSM_STATE:db + (g + 1) * SSM_STATE].astype(BF16)
            cg = xbuf[:, db + gn + g * SSM_STATE:db + gn + (g + 1) * SSM_STATE].astype(BF16)
            cb = _dot_nt(cg, bg)
            for e in range(0, hpg, 2):
                h = g * hpg + e
                ps = slice(h * SSM_HEAD_DIM, (h + 2) * SSM_HEAD_DIM)
                xp16 = xdtbuf[:, ps].astype(BF16)
                acc = jnp.zeros((TT, LANES), F32)
                for hh in (h, h + 1):
                    mm = (cb * _decay(ac, ac_t, hh, causal)).astype(BF16)
                    acc = acc + _dot(mm, _own_half(xp16, hh))
                ybuf[:, ps] = ybuf[:, ps] + acc
                yield
            sg = state[:, gs]
            ybuf[:, gs] = ybuf[:, gs] + exbuf[TT:2 * TT, gs] * _dot(cg, sg.astype(BF16))
            state[:, gs] = sg * exbuf[2 * TT - 1:2 * TT, gs] + _dot_tn(
                bg, (xdtbuf[:, gs] * exbuf[2 * TT:3 * TT, gs]).astype(BF16))
            yield

        yraw = ybuf[...]
        yraw_ref[...] = yraw
        bz = bz_ref[...].astype(F32)
        v = yraw * (bz * _sigmoid(bz))
        r = lax.rsqrt(jnp.mean(v * v, axis=-1, keepdims=True) + NORM_EPS)
        y_ref[...] = (v * r * g_ref[...]).astype(BF16)

    bl = dm.BL
    tile = lambda w, k: pl.BlockSpec((bl, TT, w), lambda i: (0, i, k))
    fixed = lambda r, w: pl.BlockSpec((r, w), lambda i: (0, 0))
    proj3, dt3 = proj.reshape(bl, dm.Lp, dm.NP), projdt.reshape(bl, dm.Lp, DT_PAD)
    scratch = [((2 * TT, xbc_w), BF16), (((SSM_CONV_K - 1) * TT, xbc_w), F32), ((TT, xbc_w), F32),
               ((SSM_STATE, db), F32), ((TT, db), F32), ((3 * TT, db), F32), ((TT, db), F32)]
    (y, yraw, sprev), rode = _call(
        _per_sequence(setup, body, bl, [True] * 4 + [False] * 8 + [True, True, "keep"] + [True] * len(scratch)),
        "mix_b_fwd", (dm.NT,),
        [tile(db, dm.WA // db), tile(db, dm.WA // db + 1), tile(2 * gn, (dm.WA + 2 * db) // (2 * gn)),
         tile(DT_PAD, 0),
         fixed(SSM_CONV_K, xbc_w), fixed(1, xbc_w), fixed(1, DT_PAD), fixed(1, DT_PAD),
         fixed(1, db), fixed(1, db), fixed(DT_PAD, db), fixed((SSM_CONV_K - 1) * TT, 2 * TT)],
        [tile(db, 0), tile(db, 0), pl.BlockSpec((bl, 1, SSM_STATE, db), lambda i: (0, i, 0, 0))],
        [jax.ShapeDtypeStruct((bl, dm.Lp, db), BF16), jax.ShapeDtypeStruct((bl, dm.Lp, db), F32),
         jax.ShapeDtypeStruct((bl, dm.NT, SSM_STATE, db), F32)],
        [pltpu.VMEM((bl,) + s, t) for s, t in scratch],
        ("arbitrary",),
        (proj3, proj3, proj3, dt3, conv_w, conv_b, dt_bias, a_log, dskx, norm_g, expand, _shift_matrix()), rider)
    return (y.reshape(dm.R, db), yraw.reshape(dm.R, db), sprev), rode


def _mix_b_bwd(proj, projdt, dyb, yraw, sprev, conv_w, conv_b, dt_bias, a_log, dskx, norm_g, expand, expand_t, dm,
               rider=None):
    db, gn, xbc_w, hpg = dm.DB, dm.GN, dm.XBC, dm.HPG
    gw = db // SSM_GROUPS

    def setup(bz_ref, bx_ref, bc_ref, dt_ref, bxh_ref, bch_ref, dy_ref, yraw_ref, sprev_ref,
              w_ref, b_ref, dtb_ref, alog_ref, dsk_ref, g_ref, e_ref, et_ref, s_ref,
              dp_ref, dpt_ref, dwc_ref, dch_ref, dhd_ref,
              rawwin, sh, xbuf, dsbuf, dstate, dxbuf, z1buf, dprebuf, exbuf, xdtbuf, dyrbuf, uvec):
        i = pl.program_id(0)

        @pl.when(i == 0)
        def _():
            dwc_ref[...] = jnp.zeros_like(dwc_ref)
            dch_ref[...] = jnp.zeros_like(dch_ref)
            dhd_ref[...] = jnp.zeros_like(dhd_ref)
            dstate[...] = jnp.zeros_like(dstate)
            dprebuf[TT:TT + SMALL_HALO, :] = jnp.zeros((SMALL_HALO, xbc_w), F32)
            rawwin[0:TT - RAW_HALO, :] = jnp.zeros((TT - RAW_HALO, xbc_w), BF16)

        @pl.when(i > 0)
        def _():
            dprebuf[TT:TT + SMALL_HALO, :] = dprebuf[0:SMALL_HALO, :]

    def body(bz_ref, bx_ref, bc_ref, dt_ref, bxh_ref, bch_ref, dy_ref, yraw_ref, sprev_ref,
             w_ref, b_ref, dtb_ref, alog_ref, dsk_ref, g_ref, e_ref, et_ref, s_ref,
             dp_ref, dpt_ref, dwc_ref, dch_ref, dhd_ref,
             rawwin, sh, xbuf, dsbuf, dstate, dxbuf, z1buf, dprebuf, exbuf, xdtbuf, dyrbuf, uvec):
        halo_on = jnp.where(pl.program_id(0) == dm.NT - 1, 0.0, 1.0).astype(BF16)

        rawwin[TT - RAW_HALO:TT, 0:db] = bxh_ref[...] * halo_on
        rawwin[TT - RAW_HALO:TT, db:xbc_w] = bch_ref[...] * halo_on
        rawwin[TT:2 * TT, 0:db] = bx_ref[...]
        rawwin[TT:2 * TT, db:xbc_w] = bc_ref[...]
        for cs, pre in _ssm_conv(rawwin, sh, s_ref, w_ref, b_ref, xbc_w):
            sl, dsl = _silu_and_grad(pre)
            xbuf[:, cs] = sl
            dsbuf[:, cs] = dsl
            yield

        z, dtv, a, ac, eac, dst = _head_scalars(dt_ref, dtb_ref, alog_ref)
        exbuf[...] = _dot(jnp.concatenate([dtv, eac, dst], axis=0).astype(BF16), e_ref[...])
        ac_t = ac.T
        causal = _tri(True)
        xdtbuf[...] = xbuf[:, 0:db] * exbuf[0:TT, :]

        yraw = yraw_ref[...]
        sz, dsz = _silu_and_grad(bz_ref[...].astype(F32))
        v = yraw * sz
        r = lax.rsqrt(jnp.mean(v * v, axis=-1, keepdims=True) + NORM_EPS)
        dy = dy_ref[...]
        dyg = dy * g_ref[...]
        dv = r * dyg - v * (r * r * r * jnp.mean(dyg * v, axis=-1, keepdims=True))
        dch_ref[0, 0:1, :] = dch_ref[0, 0:1, :] + jnp.sum(dy * v * r, axis=0, keepdims=True)
        dyr = dv * sz
        dyrbuf[...] = dyr
        dp_ref[:, 0:db] = (dv * yraw * dsz).astype(BF16)
        dch_ref[0, 1:2, :] = dch_ref[0, 1:2, :] + jnp.sum(dyr * xbuf[:, 0:db], axis=0, keepdims=True)

        lane_row = lax.broadcasted_iota(jnp.int32, (1, LANES), 1)
        sub_col = lax.broadcasted_iota(jnp.int32, (LANES, 1), 0)
        dac = jnp.zeros((TT, LANES), F32)
        colacc = jnp.zeros((LANES, TT), F32)
        for g in range(SSM_GROUPS):
            gs = slice(g * gw, (g + 1) * gw)
            bs_ = slice(db + g * SSM_STATE, db + (g + 1) * SSM_STATE)
            cs_ = slice(db + gn + g * SSM_STATE, db + gn + (g + 1) * SSM_STATE)
            bg = xbuf[:, bs_].astype(BF16)
            cg = xbuf[:, cs_].astype(BF16)
            cb = _dot_nt(cg, bg)
            dcb = jnp.zeros((TT, TT), F32)
            for e in range(0, hpg, 2):
                h = g * hpg + e
                ps = slice(h * SSM_HEAD_DIM, (h + 2) * SSM_HEAD_DIM)
                xp16 = xdtbuf[:, ps].astype(BF16)
                dyp16 = dyrbuf[:, ps].astype(BF16)
                acc = jnp.zeros((TT, LANES), F32)
                for hh in (h, h + 1):
                    dec = _decay(ac, ac_t, hh, causal)
                    mm = cb * dec
                    dyh = _own_half(dyp16, hh)
                    dmm = _dot_nt(dyh, xp16)
                    acc = acc + _dot_tn(mm.astype(BF16), dyh)
                    dcb = dcb + dmm * dec
                    gm = dmm * mm
                    dac = jnp.where(lane_row == hh, jnp.sum(gm, axis=1, keepdims=True), dac)
                    colacc = jnp.where(sub_col == hh, jnp.sum(gm, axis=0, keepdims=True), colacc)
                dxbuf[:, ps] = acc
                yield
            sg32 = sprev_ref[0, 0, :, gs]
            sg = sg32.astype(BF16)
            dsn = dstate[:, gs]
            dsn16 = dsn.astype(BF16)
            dcb16 = dcb.astype(BF16)
            eacx = exbuf[TT:2 * TT, gs]
            dstx = exbuf[2 * TT:3 * TT, gs]
            cdx = exbuf[2 * TT - 1:2 * TT, gs]
            dye16 = (dyrbuf[:, gs] * eacx).astype(BF16)
            xdt_g = xdtbuf[:, gs]
            dxbuf[:, cs_] = _dot(dcb16, bg) + _dot_nt(dye16, sg)
            dst_x = dstx * _dot(bg, dsn16)
            dxbuf[:, bs_] = _dot_tn(dcb16, cg) + _dot_nt((dstx * xdt_g).astype(BF16), dsn16)
            dstate[:, gs] = cdx * dsn + _dot_tn(cg, dye16)
            z1buf[:, gs] = dyrbuf[:, gs] * (eacx * _dot(cg, sg)) - xdt_g * dst_x
            uvec[:, gs] = jnp.broadcast_to(
                jnp.sum(xdt_g * dst_x, axis=0, keepdims=True) + jnp.sum(dsn * cdx * sg32, axis=0, keepdims=True),
                (8, gw))
            dxbuf[:, gs] = dxbuf[:, gs] + dst_x
            yield

        zz = _dot(jnp.concatenate([z1buf[...], dxbuf[:, 0:db] * xbuf[:, 0:db]], axis=0).astype(BF16), et_ref[...])
        u1, u2, u3 = _split3(uvec[...])
        ulast = (_dot(u1, et_ref[...]) + _dot(u2, et_ref[...]) + _dot(u3, et_ref[...]))[0:1, :]
        is_last = (lax.broadcasted_iota(jnp.int32, (TT, 1), 0) == TT - 1).astype(F32)
        dac = dac - colacc.T + zz[0:TT] + is_last * ulast
        dda = _exact_01_dot(_tri(False).astype(F32).astype(BF16), dac)
        ddt = dda * a + zz[TT:2 * TT]
        dhd_ref[0, 1:2, :] = dhd_ref[0, 1:2, :] + jnp.sum(dda * dtv, axis=0, keepdims=True) * a
        ddtraw = ddt * _sigmoid(z)
        dhd_ref[0, 0:1, :] = dhd_ref[0, 0:1, :] + jnp.sum(ddtraw, axis=0, keepdims=True)
        dpt_ref[...] = ddtraw.astype(BF16)
        dxbuf[:, 0:db] = dyrbuf[...] * dsk_ref[...] + dxbuf[:, 0:db] * exbuf[0:TT, :]

        for lb in range(xbc_w // LANES):
            cs = slice(lb * LANES, (lb + 1) * LANES)
            dpre = dxbuf[:, cs] * dsbuf[:, cs]
            dprebuf[0:TT, cs] = dpre
            dwc_ref[0, SSM_CONV_K:SSM_CONV_K + 1, cs] = dwc_ref[0, SSM_CONV_K:SSM_CONV_K + 1, cs] + jnp.sum(
                dpre, axis=0, keepdims=True)
            draw = w_ref[SSM_CONV_K - 1:SSM_CONV_K, cs] * dpre
            for k in range(SSM_CONV_K - 1):
                ahead = SSM_CONV_K - 1 - k
                draw = draw + w_ref[k:k + 1, cs] * dprebuf[ahead:ahead + TT, cs]
            for k in range(SSM_CONV_K):
                moved = sh[k * TT:(k + 1) * TT, cs] if k < SSM_CONV_K - 1 else rawwin[TT:2 * TT, cs].astype(F32)
                dwc_ref[0, k:k + 1, cs] = dwc_ref[0, k:k + 1, cs] + jnp.sum(dpre * moved, axis=0, keepdims=True)
            dp_ref[:, db + lb * LANES:db + (lb + 1) * LANES] = draw.astype(BF16)
            yield

    bl, nt = dm.BL, dm.NT
    tile = lambda w, k: pl.BlockSpec((bl, TT, w), lambda i: (0, nt - 1 - i, k))
    halo = lambda w, k: pl.BlockSpec((bl, HALO_BLOCK, w),
                                     lambda i: (0, jnp.maximum((nt - 1 - i) * (TT // HALO_BLOCK) - 1, 0), k))
    fixed = lambda r, w: pl.BlockSpec((r, w), lambda i: (0, 0))
    sums = lambda w: pl.BlockSpec((bl, 8, w), lambda i: (0, 0, 0))
    kz = dm.WA // db
    kc = (dm.WA + 2 * db) // (2 * gn)
    proj3, dt3 = proj.reshape(bl, dm.Lp, dm.NP), projdt.reshape(bl, dm.Lp, DT_PAD)
    scratch = [((2 * TT, xbc_w), BF16), (((SSM_CONV_K - 1) * TT, xbc_w), F32), ((TT, xbc_w), F32),
               ((TT, xbc_w), F32), ((SSM_STATE, db), F32), ((TT, xbc_w), F32), ((TT, db), F32),
               ((TT + SMALL_HALO, xbc_w), F32), ((3 * TT, db), F32), ((TT, db), F32), ((TT, db), F32), ((8, db), F32)]
    how = [True] * 8 + ["keep"] + [False] * 9 + [True, True, "keep", "keep", "keep"] + [True] * len(scratch)
    (dp, dpt, dwc, dch, dhd), rode = _call(
        _per_sequence(setup, body, bl, how), "mix_b_bwd", (nt,),
        [tile(db, kz), tile(db, kz + 1), tile(2 * gn, kc), tile(DT_PAD, 0),
         halo(db, kz + 1), halo(2 * gn, kc), tile(db, 0), tile(db, 0),
         pl.BlockSpec((bl, 1, SSM_STATE, db), lambda i: (0, nt - 1 - i, 0, 0)),
         fixed(SSM_CONV_K, xbc_w), fixed(1, xbc_w), fixed(1, DT_PAD), fixed(1, DT_PAD),
         fixed(1, db), fixed(1, db), fixed(DT_PAD, db), fixed(db, DT_PAD), fixed((SSM_CONV_K - 1) * TT, 2 * TT)],
        [tile(dm.WB, 0), tile(DT_PAD, 0), sums(xbc_w), sums(db), sums(DT_PAD)],
        [jax.ShapeDtypeStruct((bl, dm.Lp, dm.WB), BF16), jax.ShapeDtypeStruct((bl, dm.Lp, DT_PAD), BF16),
         jax.ShapeDtypeStruct((bl, 8, xbc_w), F32), jax.ShapeDtypeStruct((bl, 8, db), F32),
         jax.ShapeDtypeStruct((bl, 8, DT_PAD), F32)],
        [pltpu.VMEM((bl,) + s, t) for s, t in scratch],
        ("arbitrary",),
        (proj3, proj3, proj3, dt3, proj3, proj3, dyb.reshape(bl, dm.Lp, db), yraw.reshape(bl, dm.Lp, db), sprev,
         conv_w, conv_b, dt_bias, a_log, dskx, norm_g, expand, expand_t, _shift_matrix()), rider)
    return (dp.reshape(dm.R, dm.WB), dpt.reshape(dm.R, DT_PAD), dwc, dch, dhd), rode


def _head_consts(dm):
    head_of = jnp.arange(dm.DB) // SSM_HEAD_DIM
    expand = (jnp.arange(DT_PAD)[:, None] == head_of[None, :]).astype(BF16)
    return expand, expand.T


def _ssm_params(lw, dm):
    pad_h = lambda v: jnp.pad(v, (0, DT_PAD - dm.H))[None]
    return (lw["ssm_conv_w"], lw["ssm_conv_b"][None], pad_h(lw["dt_bias"]), pad_h(lw["a_log"]),
            jnp.repeat(lw["d_skip"], SSM_HEAD_DIM)[None], lw["ssm_norm_g"][None])


def _layer_fwd(h, lw, w_in, w_out, cst, dm, next_bases=None, own_out=None):
    nxt = next_bases is not None
    n_next = len(next_bases) if nxt else 0

    def beside(rider, extra):
        return extra if rider is None else (rider if extra is None else _ride_both(rider, extra))

    (proj, projdt, hn), got = _fwd_in(
        h, lw["pre_g"][None], w_in, dm,
        beside(_ride_gather_ici(next_bases, 0, 2) if nxt else None,
               None if own_out is None else _ride_gather_ici([own_out])))
    ya = _mix_a_fwd(proj, lw["conv_a_w"], dm)
    (yb, yraw, sprev), got = _mix_b_fwd(
        proj, projdt, *_ssm_params(lw, dm), cst[0], dm,
        beside(_ride_gather_ici(got[:n_next], 1, 2) if nxt else None,
               None if own_out is None else _ride_gather_d2d(got[n_next:])))
    if own_out is not None:
        w_out = got[n_next].reshape(2 * dm.D, dm.D)
        got = got[:n_next]
    yc, u1 = _mix_c_fwd(proj, lw["conf_conv_w"], lw["conf_conv_b"][None], lw["conf_ln_g"][None],
                        lw["conf_ln_b"][None], dm)
    (h_new, m), got = _fwd_out(ya, yb, yc, w_out, h, lw["post_g"][None], dm, _ride_gather_d2d(got) if nxt else None)
    return h_new, (h, hn, proj, projdt, ya, yb, yc, u1, yraw, sprev, m), got, w_out


def _layer_bwd(dh, saved, lw, w_in, w_out, cst, dm, reduce=None, last=False):
    h_in, hn, proj, projdt, ya, yb, yc, u1, yraw, sprev, m = saved
    (dya, dyb, dyc, dwo, dpost), got = _bwd_out(dh, m, lw["post_g"][None], w_out, ya, yb, yc, dm,
                                                None if reduce is None else reduce.swap())
    dpa, dwa = _mix_a_bwd(proj, dya, lw["conv_a_w"], dm)
    (dpb, dpt, dwcv, dch, dhd), got = _mix_b_bwd(proj, projdt, dyb, yraw, sprev, *_ssm_params(lw, dm), cst[0],
                                                 cst[1], dm, None if reduce is None else reduce.to_owners(got))
    dpc, dwcf, dvc = _mix_c_bwd(proj, u1, dyc, lw["conf_conv_w"], lw["conf_ln_g"][None], lw["conf_ln_b"][None], dm)
    def own_reduce():
        pieces = _bwd_in_dw(hn, [dpa, dpb, dpc, dpt], dm)
        return _GradReduce([_grad_to_shards(pieces, dm), dwo.reshape(N_CHIPS, 2 * dm.D // N_CHIPS, dm.D)])

    rider = None if reduce is None else reduce.join(got)
    n_join = 0 if rider is None else len(rider.out_shapes)
    if last:
        mine = own_reduce()
        to_owners = mine.to_owners(_exchange("grad_swap_halves", mine.swap()))
        rider = to_owners if rider is None else _ride_both(rider, to_owners)
    (dh, dpre), got = _bwd_in_dx(dpa, dpb, dpc, dpt, w_in, h_in, dh, lw["pre_g"][None], dm, rider)
    if reduce is not None:
        reduce.finish(got[:n_join])
    if last:
        mine.finish(_exchange("grad_join_halves", mine.join(got[n_join:])))
    else:
        mine = own_reduce()
    dwcv, dch, dhd, dvc = (jnp.sum(a, axis=0) for a in (dwcv, dch, dhd, dvc))
    small = dict(pre_g=dpre[0], post_g=dpost[0], conv_a_w=jnp.sum(dwa, axis=0)[:CONV_A_K],
                 ssm_conv_w=dwcv[:SSM_CONV_K], ssm_conv_b=dwcv[SSM_CONV_K], ssm_norm_g=dch[0],
                 d_skip=jnp.sum(dch[1].reshape(dm.H, SSM_HEAD_DIM), axis=1), dt_bias=dhd[0, :dm.H],
                 a_log=dhd[1, :dm.H], conf_conv_w=jnp.sum(dwcf, axis=0)[:CONF_K], conf_conv_b=dvc[0],
                 conf_ln_g=dvc[1], conf_ln_b=dvc[2])
    return dh, mine, small


def _shard_runs(dm):
    ab = dm.WA + dm.WB
    order = [(0, 0, ab), (ab, dm.DT0, dm.H), (ab + dm.H, ab, dm.WC)]
    k = dm.NIN // N_CHIPS
    runs = []
    for s in range(N_CHIPS):
        for o0, m0, wd in order:
            lo, hi = max(o0, s * k), min(o0 + wd, (s + 1) * k)
            if lo < hi:
                runs.append((s, lo - s * k, m0 + lo - o0, hi - lo))
    return runs


def _w_in_from_shards(base, dm):
    tr = _row_tile(dm.D, 256)
    k = dm.NIN // N_CHIPS
    runs = _shard_runs(dm)

    def body(in_ref, out_ref):
        for s, sc, mc, wd in runs:
            out_ref[:, mc:mc + wd] = in_ref[s, :, sc:sc + wd]
        out_ref[:, dm.DT0 + dm.H:dm.NP] = jnp.zeros((tr, dm.NP - dm.DT0 - dm.H), BF16)

    return pl.pallas_call(
        body, name="w_in_from_shards", grid=(dm.D // tr,),
        in_specs=[pl.BlockSpec((N_CHIPS, tr, k), lambda r: (0, r, 0))],
        out_specs=pl.BlockSpec((tr, dm.NP), lambda r: (r, 0)),
        out_shape=jax.ShapeDtypeStruct((dm.D, dm.NP), BF16),
        compiler_params=_params(("parallel",)),
    )(base)


def _grad_to_shards(pieces, dm):
    tr = _row_tile(dm.D, 256)
    k = dm.NIN // N_CHIPS
    starts = [0, dm.WA, dm.WA + dm.WB, dm.DT0]
    widths = [dm.WA, dm.WB, dm.WC, DT_PAD]
    runs = _shard_runs(dm)

    def body(a_ref, b_ref, c_ref, t_ref, out_ref):
        refs = (a_ref, b_ref, c_ref, t_ref)
        for s, sc, mc, wd in runs:
            for p in range(4):
                lo, hi = max(mc, starts[p]), min(mc + wd, starts[p] + widths[p])
                if lo < hi:
                    out_ref[s, :, sc + lo - mc:sc + hi - mc] = refs[p][:, lo - starts[p]:hi - starts[p]].astype(BF16)

    return pl.pallas_call(
        body, name="grad_to_shards", grid=(dm.D // tr,),
        in_specs=[pl.BlockSpec((tr, w), lambda r: (r, 0)) for w in widths],
        out_specs=pl.BlockSpec((N_CHIPS, tr, k), lambda r: (0, r, 0)),
        out_shape=jax.ShapeDtypeStruct((N_CHIPS, dm.D, k), BF16),
        compiler_params=_params(("parallel",)),
    )(*pieces)


def _place_own(w, layer, me):
    _, rows, cols = w.shape
    tr = _row_tile(rows, 256)

    def body(me_ref, w_ref, out_ref):
        out_ref[0] = w_ref[0].astype(BF16)

    return pl.pallas_call(
        body, name="place_own",
        grid_spec=pltpu.PrefetchScalarGridSpec(
            num_scalar_prefetch=1, grid=(rows // tr,),
            in_specs=[pl.BlockSpec((1, tr, cols), lambda r, me_ref: (layer, r, 0))],
            out_specs=pl.BlockSpec((1, tr, cols), lambda r, me_ref: (me_ref[0], r, 0))),
        out_shape=jax.ShapeDtypeStruct((N_CHIPS, rows, cols), BF16),
        compiler_params=_params(("parallel",)),
    )(me, w)


def _add_halves(g, got, c, name):
    _, _, rows, cols = g.shape
    tr = _row_tile(rows, 256)

    def body(c_ref, g_ref, got_ref, out_ref):
        out_ref[0] = (g_ref[0, 0].astype(F32) + got_ref[0].astype(F32)).astype(BF16)

    return pl.pallas_call(
        body, name=name,
        grid_spec=pltpu.PrefetchScalarGridSpec(
            num_scalar_prefetch=1, grid=(N_CHIPS, rows // tr),
            in_specs=[pl.BlockSpec((1, 1, tr, cols), lambda s, r, c_ref: (s, c_ref[0], r, 0)),
                      pl.BlockSpec((1, tr, cols), lambda s, r, c_ref: (s, r, 0))],
            out_specs=pl.BlockSpec((1, tr, cols), lambda s, r, c_ref: (s, r, 0))),
        out_shape=jax.ShapeDtypeStruct((N_CHIPS, rows, cols), BF16),
        compiler_params=_params(("parallel", "parallel")),
    )(c, g, got)


def _add_owner(p, got, where, name):
    _, rows, cols = p.shape
    tr = _row_tile(rows, 256)

    def body(w_ref, p_ref, got_ref, out_ref):
        acc = p_ref[0].astype(F32)
        for j in range(3):
            acc = acc + got_ref[j].astype(F32)
        out_ref[0] = acc

    return pl.pallas_call(
        body, name=name,
        grid_spec=pltpu.PrefetchScalarGridSpec(
            num_scalar_prefetch=1, grid=(rows // tr,),
            in_specs=[pl.BlockSpec((1, tr, cols), lambda r, w_ref: (w_ref[0], r, 0)),
                      pl.BlockSpec((3, tr, cols), lambda r, w_ref: (0, r, 0))],
            out_specs=pl.BlockSpec((1, tr, cols), lambda r, w_ref: (w_ref[1], r, 0))),
        out_shape=jax.ShapeDtypeStruct((2, rows, cols), F32),
        compiler_params=_params(("parallel",)),
    )(where, p, got)


class _GradReduce:
    def __init__(self, gs):
        self.gs = [g.reshape((N_CHIPS, 2, g.shape[1] // 2) + g.shape[2:]) for g in gs]
        self.c = lax.axis_index("c").astype(jnp.int32).reshape(1)
        chip = (2 * lax.axis_index("x") + lax.axis_index("y")).astype(jnp.int32)
        self.where = jnp.stack([chip, self.c[0]])
        self.result = None

    def swap(self):
        return _ride_swap_halves(self.gs)

    def to_owners(self, got):
        self.ps = [_add_halves(g, r, self.c, "grad_add_sibling_" + n) for g, r, n in zip(self.gs, got, ("in", "out"))]
        return _ride_to_owners(self.ps)

    def join(self, got):
        qs = [_add_owner(p, r, self.where, "grad_add_chips_" + n) for p, r, n in zip(self.ps, got, ("in", "out"))]
        return _ride_join_halves(qs)

    def finish(self, got):
        self.result = [a.reshape((a.shape[0] * a.shape[1],) + a.shape[2:]) for a in got]


def _adamw_math(w, g, m, v):
    m = ADAM_B1 * m + (1.0 - ADAM_B1) * g
    v = ADAM_B2 * v + (1.0 - ADAM_B2) * (g * g)
    m_hat = m / (1.0 - ADAM_B1 ** ADAM_STEP)
    v_hat = v / (1.0 - ADAM_B2 ** ADAM_STEP)
    delta = -ADAM_LR * (m_hat / (jnp.sqrt(v_hat) + ADAM_EPS) + ADAM_WD * w)
    return delta, m, v


def _adamw_small(w, g, m, v, name):
    def body(w_ref, g_ref, m_ref, v_ref, d_out, m_out, v_out):
        d_out[...], m_out[...], v_out[...] = _adamw_math(w_ref[...], g_ref[...], m_ref[...], v_ref[...])

    shape = jax.ShapeDtypeStruct(w.shape, F32)
    return pl.pallas_call(body, name="adamw_" + name, out_shape=[shape, shape, shape],
                          compiler_params=_params())(w, g, m, v)


def _adamw_layer(i, w, g, m, v, prev, name):
    depth, rows, cols = w.shape
    tr = _row_tile(rows, 256)
    n_prev = 0 if prev is None else 4

    def body(*refs):
        w_ref, g_ref, m_ref, v_ref = refs[:4]
        g_out, d_out, m_out, v_out = refs[4 + n_prev:]
        gv = g_ref[...]
        g_out[0] = gv
        d_out[0], m_out[0], v_out[0] = _adamw_math(w_ref[0], gv, m_ref[0], v_ref[0])

    lay = pl.BlockSpec((1, tr, cols), lambda r: (i, r, 0))
    shape = jax.ShapeDtypeStruct(w.shape, F32)
    return pl.pallas_call(
        body, name="adamw_" + name, grid=(rows // tr,),
        in_specs=[lay, pl.BlockSpec((tr, cols), lambda r: (r, 0)), lay, lay] + [ANY] * n_prev,
        out_specs=[lay] * 4, out_shape=[shape] * 4,
        input_output_aliases={4 + k: k for k in range(n_prev)},
        compiler_params=_params(("parallel",)),
    )(w, g, m, v, *(prev or ()))


def _adamw_cols_major(w, gs, m, v, name):
    depth, rows, cols = w.shape
    tr = max(t for t in range(1, 129) if cols % t == 0)
    wt, mt, vt = (jnp.transpose(a, (2, 0, 1)) for a in (w, m, v))
    gt = jnp.transpose(jnp.stack(gs, axis=0), (2, 0, 1))

    def body(w_ref, g_ref, m_ref, v_ref, g_out, d_out, m_out, v_out):
        gv = g_ref[...]
        g_out[...] = gv
        d_out[...], m_out[...], v_out[...] = _adamw_math(w_ref[...], gv, m_ref[...], v_ref[...])

    spec = pl.BlockSpec((tr, depth, rows), lambda r: (r, 0, 0))
    shape = jax.ShapeDtypeStruct((cols, depth, rows), F32)
    outs = pl.pallas_call(body, name="adamw_" + name, grid=(cols // tr,), in_specs=[spec] * 4, out_specs=[spec] * 4,
                          out_shape=[shape] * 4, compiler_params=_params(("parallel",)))(wt, gt, mt, vt)
    return [jnp.transpose(a, (1, 2, 0)) for a in outs]


def _sum_leading(buf, name):
    n, rows, cols = buf.shape
    tr = _row_tile(rows, rows)

    def body(in_ref, out_ref):
        acc = in_ref[0]
        for k in range(1, n):
            acc = acc + in_ref[k]
        out_ref[...] = acc

    return pl.pallas_call(
        body, name=name, grid=(rows // tr,),
        in_specs=[pl.BlockSpec((n, tr, cols), lambda i: (0, i, 0))],
        out_specs=pl.BlockSpec((tr, cols), lambda i: (i, 0)),
        out_shape=jax.ShapeDtypeStruct((rows, cols), F32),
        compiler_params=_params(("parallel",)),
    )(buf)


_SHARDED_SMALL = ("meta", "conv_a_w", "ssm_conv_w", "conf_conv_w")
_LAYER_SMALL = ("pre_g", "post_g", "conv_a_w", "ssm_conv_w", "ssm_conv_b", "dt_bias", "a_log", "d_skip",
                "ssm_norm_g", "conf_conv_w", "conf_conv_b", "conf_ln_g", "conf_ln_b")
_WEIGHTS = ("meta", "pre_g", "post_g", "w_in", "w_out", "conv_a_w", "ssm_conv_w", "ssm_conv_b", "dt_bias", "a_log",
            "d_skip", "ssm_norm_g", "conf_conv_w", "conf_conv_b", "conf_ln_g", "conf_ln_b")


def _shard_last(a):
    return jnp.moveaxis(a.reshape(a.shape[:-1] + (N_CHIPS, a.shape[-1] // N_CHIPS)), -2, 0)


def _with_own_block(a, n, at):
    return lax.dynamic_update_index_in_dim(jnp.zeros((n,) + a.shape, a.dtype), a, at, 0)


def _with_own_columns(a, chip):
    k = a.shape[-1]
    return lax.dynamic_update_slice_in_dim(jnp.zeros(a.shape[:-1] + (N_CHIPS * k,), a.dtype), a, chip * k, a.ndim - 1)


def kernel(x, meta, pre_g, post_g, w_in, w_out, conv_a_w, ssm_conv_w, ssm_conv_b, dt_bias, a_log, d_skip, ssm_norm_g, conf_conv_w, conf_conv_b, conf_ln_g, conf_ln_b, loss_target, m_meta, m_pre_g, m_post_g, m_w_in, m_w_out, m_conv_a_w, m_ssm_conv_w, m_ssm_conv_b, m_dt_bias, m_a_log, m_d_skip, m_ssm_norm_g, m_conf_conv_w, m_conf_conv_b, m_conf_ln_g, m_conf_ln_b, v_meta, v_pre_g, v_post_g, v_w_in, v_w_out, v_conv_a_w, v_ssm_conv_w, v_ssm_conv_b, v_dt_bias, v_a_log, v_d_skip, v_ssm_norm_g, v_conf_conv_w, v_conf_conv_b, v_conf_ln_g, v_conf_ln_b):
    w = dict(meta=meta, pre_g=pre_g, post_g=post_g, w_in=w_in, w_out=w_out, conv_a_w=conv_a_w,
             ssm_conv_w=ssm_conv_w, ssm_conv_b=ssm_conv_b, dt_bias=dt_bias, a_log=a_log, d_skip=d_skip,
             ssm_norm_g=ssm_norm_g, conf_conv_w=conf_conv_w, conf_conv_b=conf_conv_b, conf_ln_g=conf_ln_g,
             conf_ln_b=conf_ln_b)
    mom = dict(meta=m_meta, pre_g=m_pre_g, post_g=m_post_g, w_in=m_w_in, w_out=m_w_out, conv_a_w=m_conv_a_w,
               ssm_conv_w=m_ssm_conv_w, ssm_conv_b=m_ssm_conv_b, dt_bias=m_dt_bias, a_log=m_a_log, d_skip=m_d_skip,
               ssm_norm_g=m_ssm_norm_g, conf_conv_w=m_conf_conv_w, conf_conv_b=m_conf_conv_b,
               conf_ln_g=m_conf_ln_g, conf_ln_b=m_conf_ln_b)
    vel = dict(meta=v_meta, pre_g=v_pre_g, post_g=v_post_g, w_in=v_w_in, w_out=v_w_out, conv_a_w=v_conv_a_w,
               ssm_conv_w=v_ssm_conv_w, ssm_conv_b=v_ssm_conv_b, dt_bias=v_dt_bias, a_log=v_a_log, d_skip=v_d_skip,
               ssm_norm_g=v_ssm_norm_g, conf_conv_w=v_conf_conv_w, conf_conv_b=v_conf_conv_b,
               conf_ln_g=v_conf_ln_g, conf_ln_b=v_conf_ln_b)
    bl, seq, d = x.shape
    dm = Dims(bl, seq, d)
    depth = w_in.shape[0]
    chip = (2 * lax.axis_index("x") + lax.axis_index("y")).astype(jnp.int32)
    dev = 2 * chip + lax.axis_index("c").astype(jnp.int32)
    cst = _head_consts(dm)

    bases = [[_place_own(w_in, i, chip.reshape(1)), _place_own(w_out, i, chip.reshape(1))] for i in range(depth)]
    first_in, small_w = _gather_ici_relayed(
        [bases[0][0]], _ride_gather_small([_with_own_columns(w[n], chip) for n in _SHARDED_SMALL]))
    full = dict(w)
    full.update(zip(_SHARDED_SMALL, small_w))
    h, gathered = _embed(x, full["meta"], dm, _ride_gather_d2d(first_in))
    saved, proj_w = [], []
    for i in range(depth):
        lw = {n: full[n][i] for n in _LAYER_SMALL}
        w_in_i = _w_in_from_shards(gathered[0], dm)
        h, keep, gathered, w_out_i = _layer_fwd(
            h, lw, w_in_i, None if i == 0 else gathered[1].reshape(2 * d, d), cst, dm,
            bases[i + 1] if i + 1 < depth else None, bases[0][1] if i == 0 else None)
        proj_w.append((w_in_i, w_out_i))
        saved.append(keep)

    dh, loss = _loss_head(h, loss_target, dm)

    small_g = {n: [None] * depth for n in _LAYER_SMALL}
    big = {"w_in": None, "w_out": None}
    g_in = [None] * depth
    reduce = None
    for i in reversed(range(depth)):
        lw = {n: full[n][i] for n in _LAYER_SMALL}
        dh, mine, sg = _layer_bwd(dh, saved[i], lw, proj_w[i][0], proj_w[i][1], cst, dm, reduce, last=i == 0)
        for n in _LAYER_SMALL:
            small_g[n][i] = sg[n]
        if reduce is not None:
            g_in[i + 1] = reduce.result[0]
            big["w_out"] = _adamw_layer(i + 1, w_out, reduce.result[1], m_w_out, v_w_out, big["w_out"], "w_out")
        reduce = mine
    g_in[0] = reduce.result[0]
    big["w_out"] = _adamw_layer(0, w_out, reduce.result[1], m_w_out, v_w_out, big["w_out"], "w_out")
    grad_x, gmeta = _unembed(dh, dm)

    g = {n: jnp.stack(v) for n, v in small_g.items()}
    g["meta"] = gmeta
    small = [n for n in _WEIGHTS if n not in ("w_in", "w_out")]
    flat = jnp.concatenate([g[n].reshape(-1) for n in small] + [loss.reshape(1)])
    rows = -(-flat.shape[0] // (16 * LANES)) * 16
    flat = jnp.pad(flat, (0, rows * LANES - flat.shape[0])).reshape(rows, LANES)
    parts = _gather_all(_with_own_block(flat, N_DEV, dev))
    total = _sum_leading(parts, "small_grads_sum").reshape(-1)
    big["w_in"] = _adamw_cols_major(w_in, g_in, m_w_in, v_w_in, "w_in")
    grads, deltas, new_m, new_v = {}, {}, {}, {}
    off = 0
    for n in small:
        size = g[n].size
        fullg = total[off:off + size].reshape(g[n].shape)
        off += size
        if n in _SHARDED_SMALL:
            fullg = lax.dynamic_index_in_dim(_shard_last(fullg), chip, axis=0, keepdims=False)
        grads[n] = fullg
        deltas[n], new_m[n], new_v[n] = _adamw_small(w[n], fullg, mom[n], vel[n], n)
    for n in ("w_in", "w_out"):
        grads[n], deltas[n], new_m[n], new_v[n] = big[n]
    loss = total[off]

    return (loss, grad_x, *[grads[n] for n in _WEIGHTS], *[deltas[n] for n in _WEIGHTS],
            *[new_m[n] for n in _WEIGHTS], *[new_v[n] for n in _WEIGHTS])
```

```python
import jax
import jax.numpy as jnp
from jax import lax
from jax.experimental import pallas as pl
from jax.experimental.pallas import tpu as pltpu

F32 = jnp.float32
BF16 = jnp.bfloat16

N_META = 16
TT = 128
SSM_STATE = 128
SSM_GROUPS = 2
SSM_HEAD_DIM = 64
CONV_A_K = 3
SSM_CONV_K = 4
CONF_K = 31
NORM_EPS = 1e-6
LN_EPS = 1e-5
LANES = 128
MXU_DIM = 256
DT_PAD = LANES
CONF_HALO = 32
SMALL_HALO = 8
VMEM_LIMIT = 56 * 1024 * 1024
N_CHIPS = 4
N_DEV = 8

ADAM_LR = 0.001
ADAM_B1 = 0.9
ADAM_B2 = 0.999
ADAM_EPS = 1e-08
ADAM_WD = 0.01
ADAM_STEP = 10

MESH = pl.DeviceIdType.MESH
ANY = pl.BlockSpec(memory_space=pl.ANY)


class Dims:
    def __init__(self, bl, seq, d):
        self.BL, self.S, self.D = bl, seq, d
        self.L = seq + N_META
        self.Lp = -(-self.L // TT) * TT
        self.NT = self.Lp // TT
        self.R = bl * self.Lp
        self.DA = d // 2
        self.DB = d
        self.DC = d // 2
        self.H = self.DB // SSM_HEAD_DIM
        self.HPG = self.H // SSM_GROUPS
        self.GN = SSM_GROUPS * SSM_STATE
        self.WA = 4 * self.DA
        self.WB = 2 * self.DB + 2 * self.GN
        self.WC = 3 * self.DC
        self.DT0 = self.WA + self.WB + self.WC
        self.NP = -(-(self.DT0 + DT_PAD) // (5 * MXU_DIM)) * (5 * MXU_DIM)
        self.NIN = self.WA + self.WB + self.H + self.WC
        self.XBC = self.DB + 2 * self.GN
        assert self.H % 2 == 0 and self.HPG % 2 == 0 and self.H <= DT_PAD
        assert self.DA % LANES == 0 and (self.WA + self.WB) % self.DC == 0 and self.WA % self.DB == 0


def _row_tile(n, target):
    best = None
    for t in range(16, min(n, target) + 1, 16):
        if n % t == 0:
            best = t
    assert best is not None
    return best


def _col_tile(n, target):
    best = None
    for t in range(LANES, min(n, target) + 1, LANES):
        if n % t == 0:
            best = t
    assert best is not None
    return best


def _params(sem=None):
    return pltpu.CompilerParams(dimension_semantics=sem, vmem_limit_bytes=VMEM_LIMIT)


def _sigmoid(x):
    return 1.0 / (1.0 + jnp.exp(-x))


def _silu_and_grad(x):
    s = _sigmoid(x)
    y = x * s
    return y, s + y * (1.0 - s)


def _dot(a, b):
    return jnp.dot(a, b, preferred_element_type=F32)


def _dot_nt(a, b):
    return lax.dot_general(a, b, (((1,), (1,)), ((), ())), preferred_element_type=F32)


def _dot_tn(a, b):
    return lax.dot_general(a, b, (((0,), (0,)), ((), ())), preferred_element_type=F32)


def _split3(x):
    x1 = x.astype(BF16)
    r1 = x - x1.astype(F32)
    x2 = r1.astype(BF16)
    x3 = (r1 - x2.astype(F32)).astype(BF16)
    return x1, x2, x3


class Rider:
    def __init__(self, plan, ins, out_shapes, aliases, nsem):
        self.plan, self.ins, self.out_shapes, self.aliases, self.nsem = plan, list(ins), list(out_shapes), aliases, nsem


def _place():
    x, y, c = lax.axis_index("x"), lax.axis_index("y"), lax.axis_index("c")
    chips = [(1 - x, y), (x, 1 - y), (1 - x, 1 - y)]
    return x, y, c, chips


def _remote(k, src, dst, to, send_sems, recv_sems):
    return pltpu.make_async_remote_copy(src_ref=src, dst_ref=dst, send_sem=send_sems.at[k], recv_sem=recv_sems.at[k],
                                        device_id=to, device_id_type=MESH)


def _call(body, name, grid, in_specs, out_specs, out_shape, scratch_shapes, sem, args, rider=None):
    if rider is None:
        outs = pl.pallas_call(body, name=name, grid=grid, in_specs=in_specs, out_specs=out_specs, out_shape=out_shape,
                              scratch_shapes=scratch_shapes, compiler_params=_params(sem))(*args)
        return list(outs), []
    n_in, n_out, n_scr = len(args), len(out_shape), len(scratch_shapes)
    r_in, r_out = len(rider.ins), len(rider.out_shapes)

    def hosted(*refs):
        ins, rins = refs[:n_in], refs[n_in:n_in + r_in]
        o0 = n_in + r_in
        outs, routs = refs[o0:o0 + n_out], refs[o0 + n_out:o0 + n_out + r_out]
        scr = refs[o0 + n_out + r_out:o0 + n_out + r_out + n_scr]
        send_sems, recv_sems = refs[o0 + n_out + r_out + n_scr:]
        first = pl.program_id(0) == 0
        last = pl.program_id(0) == grid[0] - 1
        for ax in range(1, len(grid)):
            first = jnp.logical_and(first, pl.program_id(ax) == 0)
            last = jnp.logical_and(last, pl.program_id(ax) == grid[ax] - 1)

        @pl.when(first)
        def _():
            starts, _ = rider.plan(rins, routs, send_sems, recv_sems)
            for cp in starts:
                cp.start()

        body(*ins, *outs, *scr)

        @pl.when(last)
        def _():
            _, waits = rider.plan(rins, routs, send_sems, recv_sems)
            for wait in waits:
                wait()

    res = pl.pallas_call(
        hosted, name=name, grid=grid,
        in_specs=list(in_specs) + [ANY] * r_in, out_specs=list(out_specs) + [ANY] * r_out,
        out_shape=list(out_shape) + rider.out_shapes,
        input_output_aliases={n_in + k: n_out + v for k, v in rider.aliases.items()},
        scratch_shapes=list(scratch_shapes) + [pltpu.SemaphoreType.DMA((rider.nsem,)),
                                               pltpu.SemaphoreType.DMA((rider.nsem,))],
        compiler_params=_params(("arbitrary",) * len(grid)),
    )(*args, *rider.ins)
    return list(res[:n_out]), list(res[n_out:])


def _exchange(name, rider):
    r_in, r_out = len(rider.ins), len(rider.out_shapes)

    def body(*refs):
        rins, routs = refs[:r_in], refs[r_in:r_in + r_out]
        send_sems, recv_sems = refs[r_in + r_out:]
        starts, waits = rider.plan(rins, routs, send_sems, recv_sems)
        for cp in starts:
            cp.start()
        for wait in waits:
            wait()

    res = pl.pallas_call(
        body, name=name, in_specs=[ANY] * r_in, out_specs=[ANY] * r_out, out_shape=rider.out_shapes,
        input_output_aliases=dict(rider.aliases),
        scratch_shapes=[pltpu.SemaphoreType.DMA((rider.nsem,)), pltpu.SemaphoreType.DMA((rider.nsem,))],
    )(*rider.ins)
    return list(res)


def _same(arrays):
    return [jax.ShapeDtypeStruct(a.shape, a.dtype) for a in arrays]


class _SemsFrom:
    def __init__(self, sems, first):
        self.sems, self.first = sems, first

    @property
    def at(self):
        return self

    def __getitem__(self, k):
        return self.sems.at[self.first + k]


def _ride_both(r1, r2):
    n_in, n_out = len(r1.ins), len(r1.out_shapes)

    def plan(ins, outs, ss, rs):
        s1, w1 = r1.plan(ins[:n_in], outs[:n_out], ss, rs)
        s2, w2 = r2.plan(ins[n_in:], outs[n_out:], _SemsFrom(ss, r1.nsem), _SemsFrom(rs, r1.nsem))
        return s1 + s2, w1 + w2

    aliases = dict(r1.aliases)
    aliases.update({n_in + k: n_out + v for k, v in r2.aliases.items()})
    return Rider(plan, r1.ins + r2.ins, r1.out_shapes + r2.out_shapes, aliases, r1.nsem + r2.nsem)


def _ride_gather_ici(bases, part=0, nparts=1):
    n = len(bases)

    def plan(ins, outs, ss, rs):
        x, y, c, chips = _place()
        me = 2 * x + y
        starts, waits = [], []
        for a in range(n):
            half = outs[a].shape[1] // 2
            mine = pl.ds(c * half + part * (half // nparts), half // nparts)
            for j, chip in enumerate(chips):
                cp = _remote(3 * a + j, outs[a].at[me, mine], outs[a].at[me, mine], (*chip, c), ss, rs)
                got = outs[a].at[2 * chip[0] + chip[1], mine]
                starts.append(cp)
                waits += [cp.wait_send, _remote(3 * a + j, got, got, (*chip, c), ss, rs).wait_recv]
        return starts, waits

    return Rider(plan, bases, _same(bases), {a: a for a in range(n)}, 3 * n)


def _gather_ici_relayed(bases, also):
    n, m = len(bases), len(also.ins)

    def body(*refs):
        outs = refs[n + m:2 * n + m]
        ss, rs = refs[2 * (n + m):]
        beside, beside_waits = also.plan(refs[n:n + m], refs[2 * n + m:2 * (n + m)],
                                         _SemsFrom(ss, 4 * n), _SemsFrom(rs, 4 * n))
        for cp in beside:
            cp.start()
        x, y, c, _ = _place()
        me, xn, yn, dg = 2 * x + y, 2 * (1 - x) + y, 2 * x + (1 - y), 2 * (1 - x) + (1 - y)
        to_x, to_y = (1 - x, y, c), (x, 1 - y, c)
        sends = []

        def send(k, piece, to):
            cp = _remote(k, piece, piece, to, ss, rs)
            cp.start()
            sends.append(cp)

        def arrived(k, piece, frm):
            _remote(k, piece, piece, frm, ss, rs).wait_recv()

        rows = []
        for a in range(n):
            half = outs[a].shape[1] // 2
            rows.append((pl.ds(c * half, half), pl.ds(c * half, half // 2), pl.ds(c * half + half // 2, half // 2)))
            send(4 * a, outs[a].at[me, rows[a][0]], to_x)
            send(4 * a + 1, outs[a].at[me, rows[a][0]], to_y)
        for a in range(n):
            mine, lo, hi = rows[a]
            arrived(4 * a, outs[a].at[xn, mine], to_x)
            send(4 * a + 2, outs[a].at[xn, lo], to_y)
            arrived(4 * a + 1, outs[a].at[yn, mine], to_y)
            send(4 * a + 3, outs[a].at[yn, hi], to_x)
        for a in range(n):
            mine, lo, hi = rows[a]
            arrived(4 * a + 2, outs[a].at[dg, lo], to_y)
            arrived(4 * a + 3, outs[a].at[dg, hi], to_x)
        for cp in sends:
            cp.wait_send()
        for wait in beside_waits:
            wait()

    aliases = {a: a for a in range(n)}
    aliases.update({n + k: n + v for k, v in also.aliases.items()})
    nsem = 4 * n + also.nsem
    res = pl.pallas_call(
        body, name="gather_ici_first", in_specs=[ANY] * (n + m), out_specs=[ANY] * (n + len(also.out_shapes)),
        out_shape=_same(bases) + also.out_shapes, input_output_aliases=aliases,
        scratch_shapes=[pltpu.SemaphoreType.DMA((nsem,)), pltpu.SemaphoreType.DMA((nsem,))],
    )(*bases, *also.ins)
    return list(res[:n]), list(res[n:])


def _ride_gather_d2d(bases):
    n = len(bases)

    def plan(ins, outs, ss, rs):
        x, y, c, chips = _place()
        sib = (x, y, 1 - c)
        starts, waits = [], []
        for a in range(n):
            half = outs[a].shape[1] // 2
            for j, chip in enumerate(chips):
                frm = 2 * chip[0] + chip[1]
                got = outs[a].at[frm, pl.ds(c * half, half)]
                theirs = outs[a].at[frm, pl.ds((1 - c) * half, half)]
                cp = _remote(3 * a + j, got, got, sib, ss, rs)
                starts.append(cp)
                waits += [cp.wait_send, _remote(3 * a + j, theirs, theirs, sib, ss, rs).wait_recv]
        return starts, waits

    return Rider(plan, bases, _same(bases), {a: a for a in range(n)}, 3 * n)


def _ride_gather_small(bases):
    n = len(bases)

    def plan(ins, outs, ss, rs):
        x, y, c, chips = _place()
        me = 2 * x + y
        starts, waits = [], []
        for a in range(n):
            k = outs[a].shape[-1] // N_CHIPS
            lead = (slice(None),) * (len(outs[a].shape) - 1)
            at = (lambda s: pl.multiple_of(s * k, LANES)) if k % LANES == 0 else (lambda s: s * k)
            cols = lambda s: outs[a].at[lead + (pl.ds(at(s), k),)]
            for j, chip in enumerate(chips):
                cp = _remote(3 * a + j, cols(me), cols(me), (*chip, c), ss, rs)
                got = cols(2 * chip[0] + chip[1])
                starts.append(cp)
                waits += [cp.wait_send, _remote(3 * a + j, got, got, (*chip, c), ss, rs).wait_recv]
        return starts, waits

    return Rider(plan, bases, _same(bases), {a: a for a in range(n)}, 3 * n)


def _ride_swap_halves(gs):
    n = len(gs)

    def plan(ins, outs, ss, rs):
        x, y, c, _ = _place()
        cps = [_remote(a, ins[a].at[:, 1 - c], outs[a], (x, y, 1 - c), ss, rs) for a in range(n)]
        return cps, [cp.wait for cp in cps]

    shapes = [jax.ShapeDtypeStruct((g.shape[0],) + g.shape[2:], g.dtype) for g in gs]
    return Rider(plan, gs, shapes, {}, n)


def _ride_to_owners(ps):
    n = len(ps)

    def plan(ins, outs, ss, rs):
        x, y, c, chips = _place()
        cps = []
        for a in range(n):
            for j, chip in enumerate(chips):
                cps.append(_remote(3 * a + j, ins[a].at[2 * chip[0] + chip[1]], outs[a].at[j], (*chip, c), ss, rs))
        return cps, [cp.wait for cp in cps]

    shapes = [jax.ShapeDtypeStruct((3,) + p.shape[1:], p.dtype) for p in ps]
    return Rider(plan, ps, shapes, {}, 3 * n)


def _ride_join_halves(qs):
    n = len(qs)

    def plan(ins, outs, ss, rs):
        x, y, c, _ = _place()
        sib = (x, y, 1 - c)
        starts, waits = [], []
        for a in range(n):
            cp = _remote(a, outs[a].at[c], outs[a].at[c], sib, ss, rs)
            starts.append(cp)
            waits += [cp.wait_send, _remote(a, outs[a].at[1 - c], outs[a].at[1 - c], sib, ss, rs).wait_recv]
        return starts, waits

    return Rider(plan, qs, _same(qs), {a: a for a in range(n)}, n)


def _gather_all(base):
    def body(in_ref, out_ref, ss, rs):
        x, y, c, chips = _place()
        sib = (x, y, 1 - c)
        block = lambda cx, cy, cc: out_ref.at[4 * cx + 2 * cy + cc]
        mine = block(x, y, c)
        first = [_remote(j, mine, mine, (*chip, c), ss, rs) for j, chip in enumerate(chips)]
        first.append(_remote(3, mine, mine, sib, ss, rs))
        for cp in first:
            cp.start()
        passed = []
        for j, chip in enumerate(chips):
            got = block(*chip, c)
            _remote(j, got, got, (*chip, c), ss, rs).wait_recv()
            passed.append(_remote(4 + j, got, got, sib, ss, rs))
            passed[-1].start()
        theirs = block(x, y, 1 - c)
        _remote(3, theirs, theirs, sib, ss, rs).wait_recv()
        for j, chip in enumerate(chips):
            got = block(*chip, 1 - c)
            _remote(4 + j, got, got, sib, ss, rs).wait_recv()
        for cp in first + passed:
            cp.wait_send()

    return pl.pallas_call(
        body, name="small_grads_gather_all", in_specs=[ANY], out_specs=ANY,
        out_shape=jax.ShapeDtypeStruct(base.shape, base.dtype), input_output_aliases={0: 0},
        scratch_shapes=[pltpu.SemaphoreType.DMA((N_DEV - 1,)), pltpu.SemaphoreType.DMA((N_DEV - 1,))],
    )(base)


def _embed(x, meta, dm, rider=None):
    dc = _col_tile(dm.D, 256)
    s, lp = dm.S, dm.Lp

    def body(x_ref, meta_ref, h_ref):
        h_ref[0:N_META, :] = meta_ref[...]
        h_ref[N_META:N_META + s, :] = x_ref[0]
        if lp > N_META + s:
            h_ref[N_META + s:lp, :] = jnp.zeros((lp - N_META - s, dc), F32)

    (h,), rode = _call(
        body, "embed", (dm.BL, dm.D // dc),
        [pl.BlockSpec((1, s, dc), lambda b, j: (b, 0, j)), pl.BlockSpec((N_META, dc), lambda b, j: (0, j))],
        [pl.BlockSpec((lp, dc), lambda b, j: (b, j))], [jax.ShapeDtypeStruct((dm.R, dm.D), F32)],
        [], ("parallel", "parallel"), (x, meta), rider)
    return h, rode


def _loss_head(h, target, dm):
    dc = _col_tile(dm.D, 256)
    s, lp, nj = dm.S, dm.Lp, dm.D // dc

    def body(h_ref, t_ref, dh_ref, l_ref):
        diff = h_ref[N_META:N_META + s, :] - t_ref[0]
        dh_ref[0:N_META, :] = jnp.zeros((N_META, dc), F32)
        dh_ref[N_META:N_META + s, :] = diff * (1.0 / dm.D)
        if lp > N_META + s:
            dh_ref[N_META + s:lp, :] = jnp.zeros((lp - N_META - s, dc), F32)
        l_ref[...] = jnp.full((8, LANES), (0.5 / dm.D) * jnp.sum(diff * diff), F32)

    dh, part = pl.pallas_call(
        body, name="loss_head", grid=(dm.BL, nj),
        in_specs=[pl.BlockSpec((lp, dc), lambda b, j: (b, j)),
                  pl.BlockSpec((1, s, dc), lambda b, j: (b, 0, j))],
        out_specs=[pl.BlockSpec((lp, dc), lambda b, j: (b, j)),
                   pl.BlockSpec((8, LANES), lambda b, j: (b * nj + j, 0))],
        out_shape=[jax.ShapeDtypeStruct((dm.R, dm.D), F32),
                   jax.ShapeDtypeStruct((dm.BL * nj * 8, LANES), F32)],
        compiler_params=_params(("parallel", "parallel")),
    )(h, target)
    return dh, jnp.sum(part[::8, 0])


def _unembed(dh, dm):
    dc = _col_tile(dm.D, 256)
    s, lp = dm.S, dm.Lp

    def body(dh_ref, gx_ref, gm_ref):
        gx_ref[0] = dh_ref[N_META:N_META + s, :]

        @pl.when(pl.program_id(1) == 0)
        def _():
            gm_ref[...] = dh_ref[0:N_META, :]

        @pl.when(pl.program_id(1) > 0)
        def _():
            gm_ref[...] = gm_ref[...] + dh_ref[0:N_META, :]

    return pl.pallas_call(
        body, name="unembed", grid=(dm.D // dc, dm.BL),
        in_specs=[pl.BlockSpec((lp, dc), lambda j, b: (b, j))],
        out_specs=[pl.BlockSpec((1, s, dc), lambda j, b: (b, 0, j)),
                   pl.BlockSpec((N_META, dc), lambda j, b: (0, j))],
        out_shape=[jax.ShapeDtypeStruct((dm.BL, s, dm.D), F32),
                   jax.ShapeDtypeStruct((N_META, dm.D), F32)],
        compiler_params=_params(("parallel", "arbitrary")),
    )(dh)


def _fwd_in(h, pre_g, w, dm, rider=None):
    tm = _row_tile(dm.R, 1088)
    tn = _col_tile(dm.NP, 5 * MXU_DIM)
    nj = dm.NP // tn

    def body(h_ref, g_ref, w_ref, wdt_ref, proj_ref, dt_ref, hn_ref):
        @pl.when(pl.program_id(1) == 0)
        def _():
            xf = h_ref[...]
            r = lax.rsqrt(jnp.mean(xf * xf, axis=-1, keepdims=True) + NORM_EPS)
            hn_ref[...] = (xf * r * g_ref[...]).astype(BF16)
            dt_ref[...] = _dot(hn_ref[...], wdt_ref[...])

        proj_ref[...] = _dot(hn_ref[...], w_ref[...]).astype(BF16)

    return _call(
        body, "fwd_in", (dm.R // tm, nj),
        [pl.BlockSpec((tm, dm.D), lambda i, j: (i, 0)),
         pl.BlockSpec((1, dm.D), lambda i, j: (0, 0)),
         pl.BlockSpec((dm.D, tn), lambda i, j: (0, j)),
         pl.BlockSpec((dm.D, DT_PAD), lambda i, j: (0, dm.DT0 // DT_PAD))],
        [pl.BlockSpec((tm, tn), lambda i, j: (i, j)),
         pl.BlockSpec((tm, DT_PAD), lambda i, j: (i, 0)),
         pl.BlockSpec((tm, dm.D), lambda i, j: (i, 0))],
        [jax.ShapeDtypeStruct((dm.R, dm.NP), BF16), jax.ShapeDtypeStruct((dm.R, DT_PAD), F32),
         jax.ShapeDtypeStruct((dm.R, dm.D), BF16)],
        [], ("parallel", "arbitrary"), (h, pre_g, w, w), rider)


def _fwd_out(ya, yb, yc, w_out, h, post_g, dm, rider=None):
    tm = _row_tile(dm.Lp, 1088)
    tiles_per_seq = dm.Lp // tm
    da, db, dc = dm.DA, dm.DB, dm.DC

    def body(ya_ref, yb_ref, yc_ref, w_ref, h_ref, g_ref, hn_ref, m_ref):
        m = _dot(ya_ref[...], w_ref[0:da, :])
        m = m + _dot(yb_ref[...], w_ref[da:da + db, :])
        m = m + _dot(yc_ref[...], w_ref[da + db:da + db + dc, :])
        m_ref[...] = m
        r = lax.rsqrt(jnp.mean(m * m, axis=-1, keepdims=True) + NORM_EPS)
        t = (pl.program_id(0) % tiles_per_seq) * tm + lax.broadcasted_iota(jnp.int32, (tm, 1), 0)
        keep = (t < dm.L).astype(F32)
        hn_ref[...] = (h_ref[...] + m * r * g_ref[...]) * keep

    row = lambda i: (i, 0)
    fixed = lambda i: (0, 0)
    return _call(
        body, "fwd_out", (dm.R // tm,),
        [pl.BlockSpec((tm, da), row), pl.BlockSpec((tm, db), row), pl.BlockSpec((tm, dc), row),
         pl.BlockSpec((2 * dm.D, dm.D), fixed), pl.BlockSpec((tm, dm.D), row), pl.BlockSpec((1, dm.D), fixed)],
        [pl.BlockSpec((tm, dm.D), row), pl.BlockSpec((tm, dm.D), row)],
        [jax.ShapeDtypeStruct((dm.R, dm.D), F32), jax.ShapeDtypeStruct((dm.R, dm.D), F32)],
        [], ("parallel",), (ya, yb, yc, w_out, h, post_g), rider)


def _bwd_out(dh, m, post_g, w_out, ya, yb, yc, dm, rider=None):
    tm = _row_tile(dm.R, MXU_DIM)
    da, db, dc = dm.DA, dm.DB, dm.DC

    def body(dh_ref, m_ref, g_ref, w_ref, ya_ref, yb_ref, yc_ref, dya_ref, dyb_ref, dyc_ref, dw_ref, dg_ref):
        @pl.when(pl.program_id(0) == 0)
        def _():
            dw_ref[...] = jnp.zeros_like(dw_ref)
            dg_ref[...] = jnp.zeros_like(dg_ref)

        m = m_ref[...]
        dh_ = dh_ref[...]
        r = lax.rsqrt(jnp.mean(m * m, axis=-1, keepdims=True) + NORM_EPS)
        n = m * r
        dg_ref[0:1, :] = dg_ref[0:1, :] + jnp.sum(dh_ * n, axis=0, keepdims=True)
        dn = dh_ * g_ref[...]
        dm_ = (r * (dn - n * jnp.mean(dn * n, axis=-1, keepdims=True))).astype(BF16)
        dya_ref[...] = _dot_nt(dm_, w_ref[0:da, :])
        dyb_ref[...] = _dot_nt(dm_, w_ref[da:da + db, :])
        dyc_ref[...] = _dot_nt(dm_, w_ref[da + db:da + db + dc, :])
        dw_ref[0:da, :] = dw_ref[0:da, :] + _dot_tn(ya_ref[...], dm_)
        dw_ref[da:da + db, :] = dw_ref[da:da + db, :] + _dot_tn(yb_ref[...], dm_)
        dw_ref[da + db:da + db + dc, :] = dw_ref[da + db:da + db + dc, :] + _dot_tn(yc_ref[...], dm_)

    row = lambda i: (i, 0)
    fixed = lambda i: (0, 0)
    return _call(
        body, "bwd_out", (dm.R // tm,),
        [pl.BlockSpec((tm, dm.D), row), pl.BlockSpec((tm, dm.D), row), pl.BlockSpec((1, dm.D), fixed),
         pl.BlockSpec((2 * dm.D, dm.D), fixed),
         pl.BlockSpec((tm, da), row), pl.BlockSpec((tm, db), row), pl.BlockSpec((tm, dc), row)],
        [pl.BlockSpec((tm, da), row), pl.BlockSpec((tm, db), row), pl.BlockSpec((tm, dc), row),
         pl.BlockSpec((2 * dm.D, dm.D), fixed), pl.BlockSpec((8, dm.D), fixed)],
        [jax.ShapeDtypeStruct((dm.R, da), F32), jax.ShapeDtypeStruct((dm.R, db), F32),
         jax.ShapeDtypeStruct((dm.R, dc), F32),
         jax.ShapeDtypeStruct((2 * dm.D, dm.D), F32), jax.ShapeDtypeStruct((8, dm.D), F32)],
        [], ("arbitrary",), (dh, m, post_g, w_out, ya, yb, yc), rider)


def _bwd_in_dx(dpa, dpb, dpc, dpt, w, h, dh, pre_g, dm, rider=None):
    tm = _row_tile(dm.R, 272)
    wa, wb, wc = dm.WA, dm.WB, dm.WC

    def body(dpa_ref, dpb_ref, dpc_ref, dpt_ref, w_ref, h_ref, dh_ref, g_ref, out_ref, dg_ref):
        @pl.when(pl.program_id(0) == 0)
        def _():
            dg_ref[...] = jnp.zeros_like(dg_ref)

        dhn = _dot_nt(dpa_ref[...], w_ref[:, 0:wa])
        dhn = dhn + _dot_nt(dpb_ref[...], w_ref[:, wa:wa + wb])
        dhn = dhn + _dot_nt(dpc_ref[...], w_ref[:, wa + wb:wa + wb + wc])
        dhn = dhn + _dot_nt(dpt_ref[...], w_ref[:, wa + wb + wc:wa + wb + wc + DT_PAD])
        xf = h_ref[...]
        r = lax.rsqrt(jnp.mean(xf * xf, axis=-1, keepdims=True) + NORM_EPS)
        n = xf * r
        dg_ref[0:1, :] = dg_ref[0:1, :] + jnp.sum(dhn * n, axis=0, keepdims=True)
        dn = dhn * g_ref[...]
        out_ref[...] = dh_ref[...] + r * (dn - n * jnp.mean(dn * n, axis=-1, keepdims=True))

    row = lambda i: (i, 0)
    fixed = lambda i: (0, 0)
    return _call(
        body, "bwd_in_dx", (dm.R // tm,),
        [pl.BlockSpec((tm, wa), row), pl.BlockSpec((tm, wb), row), pl.BlockSpec((tm, wc), row),
         pl.BlockSpec((tm, DT_PAD), row), pl.BlockSpec((dm.D, dm.NP), fixed),
         pl.BlockSpec((tm, dm.D), row), pl.BlockSpec((tm, dm.D), row), pl.BlockSpec((1, dm.D), fixed)],
        [pl.BlockSpec((tm, dm.D), row), pl.BlockSpec((8, dm.D), fixed)],
        [jax.ShapeDtypeStruct((dm.R, dm.D), F32), jax.ShapeDtypeStruct((8, dm.D), F32)],
        [], ("arbitrary",), (dpa, dpb, dpc, dpt, w, h, dh, pre_g), rider)


def _bwd_in_dw(hn, dps, dm):
    widest = max(dp.shape[1] for dp in dps)
    tn = [_col_tile(dp.shape[1], 2 * MXU_DIM if dp.shape[1] == widest else MXU_DIM) for dp in dps]
    nb = [dp.shape[1] // t for dp, t in zip(dps, tn)]
    first = [sum(nb[:p]) for p in range(len(dps))]
    at = lambda p: (lambda j: (0, jnp.clip(j - first[p], 0, nb[p] - 1)))

    def body(hn_ref, *refs):
        j = pl.program_id(0)
        for p in range(len(dps)):
            @pl.when(jnp.logical_and(j >= first[p], j < first[p] + nb[p]))
            def _(p=p):
                refs[len(dps) + p][...] = _dot_tn(hn_ref[...], refs[p][...])

    return pl.pallas_call(
        body, name="bwd_in_dw", grid=(sum(nb),),
        in_specs=[pl.BlockSpec((dm.R, dm.D), lambda j: (0, 0))] + [
            pl.BlockSpec((dm.R, tn[p]), at(p)) for p in range(len(dps))],
        out_specs=[pl.BlockSpec((dm.D, tn[p]), at(p)) for p in range(len(dps))],
        out_shape=[jax.ShapeDtypeStruct((dm.D, dp.shape[1]), F32) for dp in dps],
        compiler_params=_params(("arbitrary",)),
    )(hn, *dps)


def _tile_index(dm, reverse):
    if reverse:
        return lambda b, i: b * dm.NT + (dm.NT - 1 - i)
    return lambda b, i: b * dm.NT + i


def _halo_index(dm, rows):
    per_tile = TT // rows
    return lambda b, i: jnp.maximum((b * dm.NT + (dm.NT - 1 - i)) * per_tile - 1, 0)


HALO_BLOCK = 16


def _last_rows(x):
    return x.astype(F32)[HALO_BLOCK - SMALL_HALO:HALO_BLOCK]


MIX_A_ROWS = 288


def _mix_a_fwd(proj, conv_w, dm):
    da = dm.DA
    ta = _row_tile(dm.Lp, MIX_A_ROWS)
    nta = dm.Lp // ta
    bl = dm.BL

    def setup(ab_ref, ac_ref, ax_ref, az_ref, w_ref, y_ref, pbuf):
        i = pl.program_id(0)

        @pl.when(i == 0)
        def _():
            pbuf[0:SMALL_HALO, :] = jnp.zeros((SMALL_HALO, da), F32)

        @pl.when(i > 0)
        def _():
            pbuf[0:SMALL_HALO, :] = pbuf[ta:ta + SMALL_HALO, :]

    def body(ab_ref, ac_ref, ax_ref, az_ref, w_ref, y_ref, pbuf):
        for lb in range(da // LANES):
            cs = slice(lb * LANES, (lb + 1) * LANES)
            p = ac_ref[:, cs].astype(F32) * ax_ref[:, cs].astype(F32)
            pbuf[SMALL_HALO:SMALL_HALO + ta, cs] = p
            q = (w_ref[0:1, cs] * pbuf[6:6 + ta, cs] + w_ref[1:2, cs] * pbuf[7:7 + ta, cs] + w_ref[2:3, cs] * p)
            az = az_ref[:, cs].astype(F32)
            y_ref[:, cs] = (ab_ref[:, cs].astype(F32) * q * (az * _sigmoid(az))).astype(BF16)
            yield

    proj3 = proj.reshape(bl, dm.Lp, dm.NP)
    col = lambda k: pl.BlockSpec((bl, ta, da), lambda i: (0, i, k))
    return pl.pallas_call(
        _per_sequence(setup, body, bl, [True] * 4 + [False] + [True, True]), name="mix_a_fwd", grid=(nta,),
        in_specs=[col(0), col(1), col(2), col(3), pl.BlockSpec((CONV_A_K, da), lambda i: (0, 0))],
        out_specs=col(0),
        out_shape=jax.ShapeDtypeStruct((bl, dm.Lp, da), BF16),
        scratch_shapes=[pltpu.VMEM((bl, SMALL_HALO + ta, da), F32)],
        compiler_params=_params(("arbitrary",)),
    )(proj3, proj3, proj3, proj3, conv_w).reshape(dm.R, da)


def _mix_a_bwd(proj, dya, conv_w, dm):
    da = dm.DA
    ta = _row_tile(dm.Lp, MIX_A_ROWS)
    nta = dm.Lp // ta
    bl = dm.BL

    def setup(ab_ref, ac_ref, ax_ref, az_ref, ach_ref, axh_ref, dy_ref, w_ref, dp_ref, dw_ref, pbuf, dqbuf):
        i = pl.program_id(0)

        @pl.when(i == 0)
        def _():
            dw_ref[...] = jnp.zeros_like(dw_ref)
            dqbuf[ta:ta + SMALL_HALO, :] = jnp.zeros((SMALL_HALO, da), F32)

        @pl.when(i > 0)
        def _():
            dqbuf[ta:ta + SMALL_HALO, :] = dqbuf[0:SMALL_HALO, :]

    def body(ab_ref, ac_ref, ax_ref, az_ref, ach_ref, axh_ref, dy_ref, w_ref, dp_ref, dw_ref, pbuf, dqbuf):
        halo_on = jnp.where(pl.program_id(0) == nta - 1, 0.0, 1.0)
        for lb in range(da // LANES):
            cs = slice(lb * LANES, (lb + 1) * LANES)
            pbuf[0:SMALL_HALO, cs] = (_last_rows(ach_ref[:, cs]) * _last_rows(axh_ref[:, cs])) * halo_on
            ac, ax, ab, az = (r[:, cs].astype(F32) for r in (ac_ref, ax_ref, ab_ref, az_ref))
            p = ac * ax
            pbuf[SMALL_HALO:SMALL_HALO + ta, cs] = p
            p1 = pbuf[7:7 + ta, cs]
            p2 = pbuf[6:6 + ta, cs]
            w0, w1, w2 = w_ref[0:1, cs], w_ref[1:2, cs], w_ref[2:3, cs]
            q = w0 * p2 + w1 * p1 + w2 * p
            sz, dsz = _silu_and_grad(az)
            dy = dy_ref[:, cs]
            t1 = dy * ab
            dq = t1 * sz
            dqbuf[0:ta, cs] = dq
            dpv = w2 * dq + w1 * dqbuf[1:1 + ta, cs] + w0 * dqbuf[2:2 + ta, cs]
            dp_ref[:, lb * LANES:(lb + 1) * LANES] = (dy * q * sz).astype(BF16)
            dp_ref[:, da + lb * LANES:da + (lb + 1) * LANES] = (dpv * ax).astype(BF16)
            dp_ref[:, 2 * da + lb * LANES:2 * da + (lb + 1) * LANES] = (dpv * ac).astype(BF16)
            dp_ref[:, 3 * da + lb * LANES:3 * da + (lb + 1) * LANES] = (t1 * q * dsz).astype(BF16)
            dw_ref[0, 0:1, cs] = dw_ref[0, 0:1, cs] + jnp.sum(dq * p2, axis=0, keepdims=True)
            dw_ref[0, 1:2, cs] = dw_ref[0, 1:2, cs] + jnp.sum(dq * p1, axis=0, keepdims=True)
            dw_ref[0, 2:3, cs] = dw_ref[0, 2:3, cs] + jnp.sum(dq * p, axis=0, keepdims=True)
            yield

    proj3 = proj.reshape(bl, dm.Lp, dm.NP)
    col = lambda w, k: pl.BlockSpec((bl, ta, w), lambda i: (0, nta - 1 - i, k))
    halo = lambda k: pl.BlockSpec((bl, HALO_BLOCK, da),
                                  lambda i: (0, jnp.maximum((nta - 1 - i) * (ta // HALO_BLOCK) - 1, 0), k))
    dp, dw = pl.pallas_call(
        _per_sequence(setup, body, bl, [True] * 7 + [False] + [True, "keep"] + [True, True]),
        name="mix_a_bwd", grid=(nta,),
        in_specs=[col(da, 0), col(da, 1), col(da, 2), col(da, 3), halo(1), halo(2), col(da, 0),
                  pl.BlockSpec((CONV_A_K, da), lambda i: (0, 0))],
        out_specs=[col(dm.WA, 0), pl.BlockSpec((bl, 8, da), lambda i: (0, 0, 0))],
        out_shape=[jax.ShapeDtypeStruct((bl, dm.Lp, dm.WA), BF16), jax.ShapeDtypeStruct((bl, 8, da), F32)],
        scratch_shapes=[pltpu.VMEM((bl, SMALL_HALO + ta, da), F32), pltpu.VMEM((bl, ta + SMALL_HALO, da), F32)],
        compiler_params=_params(("arbitrary",)),
    )(proj3, proj3, proj3, proj3, proj3, proj3, dya.reshape(bl, dm.Lp, da), conv_w)
    return dp.reshape(dm.R, dm.WA), dw


SUBLANES = 8
SHIFT_ROWS = TT + CONF_HALO - SUBLANES


TAP_ROWS = 64


def _split_lanes(buf, rows, val):
    for lb in range(val.shape[1] // LANES):
        buf[lb, rows, :] = val[:, lb * LANES:(lb + 1) * LANES]


def _join_lanes(buf):
    return jnp.concatenate([buf[lb] for lb in range(buf.shape[0])], axis=1)


def _fill_shifted(buf, shifted):
    def step(lb, carry):
        for r in range(1, SUBLANES):
            shifted[lb, r - 1, 0:SHIFT_ROWS, :] = buf[lb, r:r + SHIFT_ROWS, :]
        return carry

    lax.fori_loop(0, buf.shape[0], step, 0)


def _window(buf, shifted, d, r0, lb):
    r = d % SUBLANES
    rows = pl.ds(pl.multiple_of(r0 + (d - r), SUBLANES), TAP_ROWS)
    return buf[lb, rows, :] if r == 0 else shifted[lb, r - 1, rows, :]


def _tap_loop(nlb, body):
    per_lb = TT // TAP_ROWS

    def step(it, carry):
        lb = it // per_lb
        body(lb, pl.ds(pl.multiple_of(lb * LANES, LANES), LANES), pl.multiple_of((it % per_lb) * TAP_ROWS, TAP_ROWS))
        return carry

    lax.fori_loop(0, nlb * per_lb, step, 0)


TAP_CHAINS = 4


def _tree_sum(terms):
    sums = list(terms[:TAP_CHAINS])
    for n, t in enumerate(terms[TAP_CHAINS:]):
        sums[n % TAP_CHAINS] = sums[n % TAP_CHAINS] + t
    while len(sums) > 1:
        sums = [a + b for a, b in zip(sums[0::2], sums[1::2])] + ([sums[-1]] if len(sums) % 2 else [])
    return sums[0]


def _conf_conv(ubuf, ushift, w_ref, b_ref, u1buf):
    _fill_shifted(ubuf, ushift)

    def piece(lb, cs, r0):
        taps = [w_ref[k:k + 1, cs] * _window(ubuf, ushift, CONF_HALO - (CONF_K - 1) + k, r0, lb)
                for k in range(CONF_K)]
        u1buf[lb, pl.ds(r0, TAP_ROWS), :] = _tree_sum(taps) + b_ref[0:1, cs]

    _tap_loop(ubuf.shape[0], piece)


def _mix_c_fwd(proj, conv_w, conv_b, ln_g, ln_b, dm):
    dc = dm.DC
    nlb = dc // LANES
    c0 = (dm.WA + dm.WB) // dc
    ti = _tile_index(dm, False)

    def body(ca_ref, cg_ref, cz_ref, w_ref, b_ref, g_ref, be_ref, y_ref, u1_ref, ubuf, u1buf, ushift):
        i = pl.program_id(1)

        @pl.when(i == 0)
        def _():
            ubuf[:, 0:CONF_HALO, :] = jnp.zeros((nlb, CONF_HALO, LANES), F32)

        @pl.when(i > 0)
        def _():
            ubuf[:, 0:CONF_HALO, :] = ubuf[:, TT:TT + CONF_HALO, :]

        _split_lanes(ubuf, slice(CONF_HALO, CONF_HALO + TT),
                     ca_ref[...].astype(F32) * _sigmoid(cg_ref[...].astype(F32)))
        _conf_conv(ubuf, ushift, w_ref, b_ref, u1buf)
        u1 = _join_lanes(u1buf)
        u1_ref[...] = u1
        mu = jnp.mean(u1, axis=-1, keepdims=True)
        xc = u1 - mu
        rstd = lax.rsqrt(jnp.mean(xc * xc, axis=-1, keepdims=True) + LN_EPS)
        u2 = xc * rstd * g_ref[...] + be_ref[...]
        cz = cz_ref[...].astype(F32)
        y_ref[...] = ((u2 * _sigmoid(u2)) * (cz * _sigmoid(cz))).astype(BF16)

    col = lambda k: pl.BlockSpec((TT, dc), lambda b, i: (ti(b, i), c0 + k))
    vec = pl.BlockSpec((1, dc), lambda b, i: (0, 0))
    return pl.pallas_call(
        body, name="mix_c_fwd", grid=(dm.BL, dm.NT),
        in_specs=[col(0), col(1), col(2), pl.BlockSpec((CONF_K, dc), lambda b, i: (0, 0)), vec, vec, vec],
        out_specs=[pl.BlockSpec((TT, dc), lambda b, i: (ti(b, i), 0))] * 2,
        out_shape=[jax.ShapeDtypeStruct((dm.R, dc), BF16), jax.ShapeDtypeStruct((dm.R, dc), F32)],
        scratch_shapes=[pltpu.VMEM((nlb, CONF_HALO + TT, LANES), F32), pltpu.VMEM((nlb, TT, LANES), F32),
                        pltpu.VMEM((nlb, SUBLANES - 1, SHIFT_ROWS, LANES), F32)],
        compiler_params=_params(("parallel", "arbitrary")),
    )(proj, proj, proj, conv_w, conv_b, ln_g, ln_b)


def _mix_c_bwd(proj, u1, dyc, conv_w, ln_g, ln_b, dm):
    dc = dm.DC
    nlb = dc // LANES
    c0 = (dm.WA + dm.WB) // dc
    ti = _tile_index(dm, True)
    hi = _halo_index(dm, CONF_HALO)

    def body(ca_ref, cg_ref, cz_ref, cah_ref, cgh_ref, u1_ref, dy_ref, w_ref, g_ref, be_ref,
             dp_ref, dw_ref, dv_ref, ubuf, dubuf, du0buf, ushift, dshift, dwacc):
        i = pl.program_id(1)
        halo_on = jnp.where(i == dm.NT - 1, 0.0, 1.0)

        @pl.when(i == 0)
        def _():
            dwacc[...] = jnp.zeros_like(dwacc)
            dv_ref[...] = jnp.zeros_like(dv_ref)
            dubuf[:, TT:TT + CONF_HALO, :] = jnp.zeros((nlb, CONF_HALO, LANES), F32)

        @pl.when(i > 0)
        def _():
            dubuf[:, TT:TT + CONF_HALO, :] = dubuf[:, 0:CONF_HALO, :]

        _split_lanes(ubuf, slice(0, CONF_HALO),
                     cah_ref[...].astype(F32) * _sigmoid(cgh_ref[...].astype(F32)) * halo_on)
        sgg = _sigmoid(cg_ref[...].astype(F32))
        ca = ca_ref[...].astype(F32)
        _split_lanes(ubuf, slice(CONF_HALO, CONF_HALO + TT), ca * sgg)
        _fill_shifted(ubuf, ushift)
        u1 = u1_ref[...]
        mu = jnp.mean(u1, axis=-1, keepdims=True)
        xc = u1 - mu
        rstd = lax.rsqrt(jnp.mean(xc * xc, axis=-1, keepdims=True) + LN_EPS)
        xhat = xc * rstd
        u2 = xhat * g_ref[...] + be_ref[...]
        su, dsu = _silu_and_grad(u2)
        sz, dsz = _silu_and_grad(cz_ref[...].astype(F32))
        dy = dy_ref[...]
        du2 = dy * dsu * sz
        dp_ref[:, 2 * dc:3 * dc] = (dy * su * dsz).astype(BF16)
        dxhat = du2 * g_ref[...]
        du1 = rstd * (dxhat - jnp.mean(dxhat, axis=-1, keepdims=True)
                      - xhat * jnp.mean(dxhat * xhat, axis=-1, keepdims=True))
        dv_ref[0, 0:1, :] = dv_ref[0, 0:1, :] + jnp.sum(du1, axis=0, keepdims=True)
        dv_ref[0, 1:2, :] = dv_ref[0, 1:2, :] + jnp.sum(du2 * xhat, axis=0, keepdims=True)
        dv_ref[0, 2:3, :] = dv_ref[0, 2:3, :] + jnp.sum(du2, axis=0, keepdims=True)
        _split_lanes(dubuf, slice(0, TT), du1)
        _fill_shifted(dubuf, dshift)

        def piece(lb, cs, r0):
            du0buf[lb, pl.ds(r0, TAP_ROWS), :] = _tree_sum(
                [w_ref[k:k + 1, cs] * _window(dubuf, dshift, CONF_K - 1 - k, r0, lb) for k in range(CONF_K)])
            d1 = dubuf[lb, pl.ds(r0, TAP_ROWS), :]
            for k in range(CONF_K):
                prod = d1 * _window(ubuf, ushift, CONF_HALO - (CONF_K - 1) + k, r0, lb)
                dwacc[lb, k] = dwacc[lb, k] + jnp.sum(prod.reshape(TAP_ROWS // SUBLANES, SUBLANES, LANES), axis=0)

        _tap_loop(nlb, piece)
        du0 = _join_lanes(du0buf)
        dp_ref[:, 0:dc] = (du0 * sgg).astype(BF16)
        dp_ref[:, dc:2 * dc] = (du0 * ca * sgg * (1.0 - sgg)).astype(BF16)

        @pl.when(i == dm.NT - 1)
        def _():
            for lb in range(nlb):
                dw_ref[0, 0:CONF_K, lb * LANES:(lb + 1) * LANES] = jnp.sum(dwacc[lb], axis=1)
            dw_ref[0, CONF_K:CONF_K + 1, :] = jnp.zeros((1, dc), F32)

    col = lambda k: pl.BlockSpec((TT, dc), lambda b, i: (ti(b, i), c0 + k))
    halo = lambda k: pl.BlockSpec((CONF_HALO, dc), lambda b, i: (hi(b, i), c0 + k))
    vec = pl.BlockSpec((1, dc), lambda b, i: (0, 0))
    return pl.pallas_call(
        body, name="mix_c_bwd", grid=(dm.BL, dm.NT),
        in_specs=[col(0), col(1), col(2), halo(0), halo(1),
                  pl.BlockSpec((TT, dc), lambda b, i: (ti(b, i), 0)),
                  pl.BlockSpec((TT, dc), lambda b, i: (ti(b, i), 0)),
                  pl.BlockSpec((CONF_K, dc), lambda b, i: (0, 0)), vec, vec],
        out_specs=[pl.BlockSpec((TT, dm.WC), lambda b, i: (ti(b, i), 0)),
                   pl.BlockSpec((1, 32, dc), lambda b, i: (b, 0, 0)),
                   pl.BlockSpec((1, 8, dc), lambda b, i: (b, 0, 0))],
        out_shape=[jax.ShapeDtypeStruct((dm.R, dm.WC), BF16),
                   jax.ShapeDtypeStruct((dm.BL, 32, dc), F32),
                   jax.ShapeDtypeStruct((dm.BL, 8, dc), F32)],
        scratch_shapes=[pltpu.VMEM((nlb, CONF_HALO + TT, LANES), F32),
                        pltpu.VMEM((nlb, TT + CONF_HALO, LANES), F32), pltpu.VMEM((nlb, TT, LANES), F32),
                        pltpu.VMEM((nlb, SUBLANES - 1, SHIFT_ROWS, LANES), F32),
                        pltpu.VMEM((nlb, SUBLANES - 1, SHIFT_ROWS, LANES), F32),
                        pltpu.VMEM((nlb, CONF_K, SUBLANES, LANES), F32)],
        compiler_params=_params(("parallel", "arbitrary")),
    )(proj, proj, proj, proj, proj, u1, dyc, conv_w, ln_g, ln_b)


RAW_HALO = 16


def _shift_matrix():
    r = jnp.arange((SSM_CONV_K - 1) * TT)[:, None]
    want = TT + r % TT - (SSM_CONV_K - 1 - r // TT)
    return (jnp.arange(2 * TT)[None, :] == want).astype(BF16)


def _ssm_conv(rawwin, sh, s_ref, w_ref, b_ref, width):
    sh[...] = _dot(s_ref[...], rawwin[...])
    for lb in range(width // LANES):
        cs = slice(lb * LANES, (lb + 1) * LANES)
        acc = b_ref[0:1, cs] + w_ref[SSM_CONV_K - 1:SSM_CONV_K, cs] * rawwin[TT:2 * TT, cs].astype(F32)
        for k in range(SSM_CONV_K - 1):
            acc = acc + w_ref[k:k + 1, cs] * sh[k * TT:(k + 1) * TT, cs]
        yield cs, acc


def _softplus(z):
    return jnp.maximum(z, 0.0) + jnp.log(1.0 + jnp.exp(-jnp.abs(z)))


def _tri(lower):
    r = lax.broadcasted_iota(jnp.int32, (TT, TT), 0)
    c = lax.broadcasted_iota(jnp.int32, (TT, TT), 1)
    return (c <= r) if lower else (c >= r)


def _exact_01_dot(mat01, x):
    x1, x2, x3 = _split3(x)
    return _dot(mat01, x1) + _dot(mat01, x2) + _dot(mat01, x3)


def _head_scalars(dt_ref, dtb_ref, alog_ref):
    z = dt_ref[...] + dtb_ref[...]
    dtv = _softplus(z)
    a = -jnp.exp(alog_ref[...])
    ac = _exact_01_dot(_tri(True).astype(F32).astype(BF16), dtv * a)
    eac = jnp.exp(ac)
    dst = jnp.exp(ac[TT - 1:TT, :] - ac)
    return z, dtv, a, ac, eac, dst


FAR_BELOW = -1e30


def _decay(ac, ac_t, h, causal):
    return jnp.exp(jnp.where(causal, ac[:, h:h + 1] - ac_t[h:h + 1, :], FAR_BELOW))


def _own_half(x16, h):
    lane = lax.broadcasted_iota(jnp.int32, (1, LANES), 1)
    keep = (lane >= SSM_HEAD_DIM) if (h % 2) else (lane < SSM_HEAD_DIM)
    return jnp.where(keep, x16, jnp.zeros_like(x16))


def _per_sequence(setup, body, bl, how):
    def all_sequences(*refs):
        views = [[r.at[b] if h is True else (r.at[pl.ds(b, 1)] if h == "keep" else r) for r, h in zip(refs, how)]
                 for b in range(bl)]
        for v in views:
            setup(*v)
        running = [body(*v) for v in views]
        while running:
            running = [g for g in running if next(g, "done") != "done"]

    return all_sequences


def _mix_b_fwd(proj, projdt, conv_w, conv_b, dt_bias, a_log, dskx, norm_g, expand, dm, rider=None):
    db, gn, xbc_w, hpg = dm.DB, dm.GN, dm.XBC, dm.HPG
    gw = db // SSM_GROUPS

    def setup(bz_ref, bx_ref, bc_ref, dt_ref, w_ref, b_ref, dtb_ref, alog_ref, dsk_ref, g_ref, e_ref, s_ref,
              y_ref, yraw_ref, sprev_ref, rawwin, sh, xbuf, state, ybuf, exbuf, xdtbuf):
        i = pl.program_id(0)

        @pl.when(i == 0)
        def _():
            rawwin[0:TT, :] = jnp.zeros((TT, xbc_w), BF16)
            state[...] = jnp.zeros_like(state)

        @pl.when(i > 0)
        def _():
            rawwin[TT - RAW_HALO:TT, :] = rawwin[2 * TT - RAW_HALO:2 * TT, :]

    def body(bz_ref, bx_ref, bc_ref, dt_ref, w_ref, b_ref, dtb_ref, alog_ref, dsk_ref, g_ref, e_ref, s_ref,
             y_ref, yraw_ref, sprev_ref, rawwin, sh, xbuf, state, ybuf, exbuf, xdtbuf):
        rawwin[TT:2 * TT, 0:db] = bx_ref[...]
        rawwin[TT:2 * TT, db:xbc_w] = bc_ref[...]
        for cs, pre in _ssm_conv(rawwin, sh, s_ref, w_ref, b_ref, xbc_w):
            xbuf[:, cs] = pre * _sigmoid(pre)
            yield

        _, dtv, _, ac, eac, dst = _head_scalars(dt_ref, dtb_ref, alog_ref)
        exbuf[...] = _dot(jnp.concatenate([dtv, eac, dst], axis=0).astype(BF16), e_ref[...])
        ac_t = ac.T
        causal = _tri(True)
        sprev_ref[0, 0] = state[...]
        yield

        xdtbuf[...] = xbuf[:, 0:db] * exbuf[0:TT, :]
        ybuf[...] = xbuf[:, 0:db] * dsk_ref[...]
        for g in range(SSM_GROUPS):
            gs = slice(g * gw, (g + 1) * gw)
            bg = xbuf[:, db + g * SSM_STATE:db + (g + 1) * SSM_STATE].astype(BF16)
            cg = xbuf[:, db + gn + g * SSM_STATE:db + gn + (g + 1) * SSM_STATE].astype(BF16)
            cb = _dot_nt(cg, bg)
            for e in range(0, hpg, 2):
                h = g * hpg + e
                ps = slice(h * SSM_HEAD_DIM, (h + 2) * SSM_HEAD_DIM)
                xp16 = xdtbuf[:, ps].astype(BF16)
                acc = jnp.zeros((TT, LANES), F32)
                for hh in (h, h + 1):
                    mm = (cb * _decay(ac, ac_t, hh, causal)).astype(BF16)
                    acc = acc + _dot(mm, _own_half(xp16, hh))
                ybuf[:, ps] = ybuf[:, ps] + acc
                yield
            sg = state[:, gs]
            ybuf[:, gs] = ybuf[:, gs] + exbuf[TT:2 * TT, gs] * _dot(cg, sg.astype(BF16))
            state[:, gs] = sg * exbuf[2 * TT - 1:2 * TT, gs] + _dot_tn(
                bg, (xdtbuf[:, gs] * exbuf[2 * TT:3 * TT, gs]).astype(BF16))
            yield

        yraw = ybuf[...]
        yraw_ref[...] = yraw
        bz = bz_ref[...].astype(F32)
        v = yraw * (bz * _sigmoid(bz))
        r = lax.rsqrt(jnp.mean(v * v, axis=-1, keepdims=True) + NORM_EPS)
        y_ref[...] = (v * r * g_ref[...]).astype(BF16)

    bl = dm.BL
    tile = lambda w, k: pl.BlockSpec((bl, TT, w), lambda i: (0, i, k))
    fixed = lambda r, w: pl.BlockSpec((r, w), lambda i: (0, 0))
    proj3, dt3 = proj.reshape(bl, dm.Lp, dm.NP), projdt.reshape(bl, dm.Lp, DT_PAD)
    scratch = [((2 * TT, xbc_w), BF16), (((SSM_CONV_K - 1) * TT, xbc_w), F32), ((TT, xbc_w), F32),
               ((SSM_STATE, db), F32), ((TT, db), F32), ((3 * TT, db), F32), ((TT, db), F32)]
    (y, yraw, sprev), rode = _call(
        _per_sequence(setup, body, bl, [True] * 4 + [False] * 8 + [True, True, "keep"] + [True] * len(scratch)),
        "mix_b_fwd", (dm.NT,),
        [tile(db, dm.WA // db), tile(db, dm.WA // db + 1), tile(2 * gn, (dm.WA + 2 * db) // (2 * gn)),
         tile(DT_PAD, 0),
         fixed(SSM_CONV_K, xbc_w), fixed(1, xbc_w), fixed(1, DT_PAD), fixed(1, DT_PAD),
         fixed(1, db), fixed(1, db), fixed(DT_PAD, db), fixed((SSM_CONV_K - 1) * TT, 2 * TT)],
        [tile(db, 0), tile(db, 0), pl.BlockSpec((bl, 1, SSM_STATE, db), lambda i: (0, i, 0, 0))],
        [jax.ShapeDtypeStruct((bl, dm.Lp, db), BF16), jax.ShapeDtypeStruct((bl, dm.Lp, db), F32),
         jax.ShapeDtypeStruct((bl, dm.NT, SSM_STATE, db), F32)],
        [pltpu.VMEM((bl,) + s, t) for s, t in scratch],
        ("arbitrary",),
        (proj3, proj3, proj3, dt3, conv_w, conv_b, dt_bias, a_log, dskx, norm_g, expand, _shift_matrix()), rider)
    return (y.reshape(dm.R, db), yraw.reshape(dm.R, db), sprev), rode


def _mix_b_bwd(proj, projdt, dyb, yraw, sprev, conv_w, conv_b, dt_bias, a_log, dskx, norm_g, expand, expand_t, dm,
               rider=None):
    db, gn, xbc_w, hpg = dm.DB, dm.GN, dm.XBC, dm.HPG
    gw = db // SSM_GROUPS

    def setup(bz_ref, bx_ref, bc_ref, dt_ref, bxh_ref, bch_ref, dy_ref, yraw_ref, sprev_ref,
              w_ref, b_ref, dtb_ref, alog_ref, dsk_ref, g_ref, e_ref, et_ref, s_ref,
              dp_ref, dpt_ref, dwc_ref, dch_ref, dhd_ref,
              rawwin, sh, xbuf, dsbuf, dstate, dxbuf, z1buf, dprebuf, exbuf, xdtbuf, dyrbuf, uvec):
        i = pl.program_id(0)

        @pl.when(i == 0)
        def _():
            dwc_ref[...] = jnp.zeros_like(dwc_ref)
            dch_ref[...] = jnp.zeros_like(dch_ref)
            dhd_ref[...] = jnp.zeros_like(dhd_ref)
            dstate[...] = jnp.zeros_like(dstate)
            dprebuf[TT:TT + SMALL_HALO, :] = jnp.zeros((SMALL_HALO, xbc_w), F32)
            rawwin[0:TT - RAW_HALO, :] = jnp.zeros((TT - RAW_HALO, xbc_w), BF16)

        @pl.when(i > 0)
        def _():
            dprebuf[TT:TT + SMALL_HALO, :] = dprebuf[0:SMALL_HALO, :]

    def body(bz_ref, bx_ref, bc_ref, dt_ref, bxh_ref, bch_ref, dy_ref, yraw_ref, sprev_ref,
             w_ref, b_ref, dtb_ref, alog_ref, dsk_ref, g_ref, e_ref, et_ref, s_ref,
             dp_ref, dpt_ref, dwc_ref, dch_ref, dhd_ref,
             rawwin, sh, xbuf, dsbuf, dstate, dxbuf, z1buf, dprebuf, exbuf, xdtbuf, dyrbuf, uvec):
        halo_on = jnp.where(pl.program_id(0) == dm.NT - 1, 0.0, 1.0).astype(BF16)

        rawwin[TT - RAW_HALO:TT, 0:db] = bxh_ref[...] * halo_on
        rawwin[TT - RAW_HALO:TT, db:xbc_w] = bch_ref[...] * halo_on
        rawwin[TT:2 * TT, 0:db] = bx_ref[...]
        rawwin[TT:2 * TT, db:xbc_w] = bc_ref[...]
        for cs, pre in _ssm_conv(rawwin, sh, s_ref, w_ref, b_ref, xbc_w):
            sl, dsl = _silu_and_grad(pre)
            xbuf[:, cs] = sl
            dsbuf[:, cs] = dsl
            yield

        z, dtv, a, ac, eac, dst = _head_scalars(dt_ref, dtb_ref, alog_ref)
        exbuf[...] = _dot(jnp.concatenate([dtv, eac, dst], axis=0).astype(BF16), e_ref[...])
        ac_t = ac.T
        causal = _tri(True)
        xdtbuf[...] = xbuf[:, 0:db] * exbuf[0:TT, :]

        yraw = yraw_ref[...]
        sz, dsz = _silu_and_grad(bz_ref[...].astype(F32))
        v = yraw * sz
        r = lax.rsqrt(jnp.mean(v * v, axis=-1, keepdims=True) + NORM_EPS)
        dy = dy_ref[...]
        dyg = dy * g_ref[...]
        dv = r * dyg - v * (r * r * r * jnp.mean(dyg * v, axis=-1, keepdims=True))
        dch_ref[0, 0:1, :] = dch_ref[0, 0:1, :] + jnp.sum(dy * v * r, axis=0, keepdims=True)
        dyr = dv * sz
        dyrbuf[...] = dyr
        dp_ref[:, 0:db] = (dv * yraw * dsz).astype(BF16)
        dch_ref[0, 1:2, :] = dch_ref[0, 1:2, :] + jnp.sum(dyr * xbuf[:, 0:db], axis=0, keepdims=True)

        lane_row = lax.broadcasted_iota(jnp.int32, (1, LANES), 1)
        sub_col = lax.broadcasted_iota(jnp.int32, (LANES, 1), 0)
        dac = jnp.zeros((TT, LANES), F32)
        colacc = jnp.zeros((LANES, TT), F32)
        for g in range(SSM_GROUPS):
            gs = slice(g * gw, (g + 1) * gw)
            bs_ = slice(db + g * SSM_STATE, db + (g + 1) * SSM_STATE)
            cs_ = slice(db + gn + g * SSM_STATE, db + gn + (g + 1) * SSM_STATE)
            bg = xbuf[:, bs_].astype(BF16)
            cg = xbuf[:, cs_].astype(BF16)
            cb = _dot_nt(cg, bg)
            dcb = jnp.zeros((TT, TT), F32)
            for e in range(0, hpg, 2):
                h = g * hpg + e
                ps = slice(h * SSM_HEAD_DIM, (h + 2) * SSM_HEAD_DIM)
                xp16 = xdtbuf[:, ps].astype(BF16)
                dyp16 = dyrbuf[:, ps].astype(BF16)
                acc = jnp.zeros((TT, LANES), F32)
                for hh in (h, h + 1):
                    dec = _decay(ac, ac_t, hh, causal)
                    mm = cb * dec
                    dyh = _own_half(dyp16, hh)
                    dmm = _dot_nt(dyh, xp16)
                    acc = acc + _dot_tn(mm.astype(BF16), dyh)
                    dcb = dcb + dmm * dec
                    gm = dmm * mm
                    dac = jnp.where(lane_row == hh, jnp.sum(gm, axis=1, keepdims=True), dac)
                    colacc = jnp.where(sub_col == hh, jnp.sum(gm, axis=0, keepdims=True), colacc)
                dxbuf[:, ps] = acc
                yield
            sg32 = sprev_ref[0, 0, :, gs]
            sg = sg32.astype(BF16)
            dsn = dstate[:, gs]
            dsn16 = dsn.astype(BF16)
            dcb16 = dcb.astype(BF16)
            eacx = exbuf[TT:2 * TT, gs]
            dstx = exbuf[2 * TT:3 * TT, gs]
            cdx = exbuf[2 * TT - 1:2 * TT, gs]
            dye16 = (dyrbuf[:, gs] * eacx).astype(BF16)
            xdt_g = xdtbuf[:, gs]
            dxbuf[:, cs_] = _dot(dcb16, bg) + _dot_nt(dye16, sg)
            dst_x = dstx * _dot(bg, dsn16)
            dxbuf[:, bs_] = _dot_tn(dcb16, cg) + _dot_nt((dstx * xdt_g).astype(BF16), dsn16)
            dstate[:, gs] = cdx * dsn + _dot_tn(cg, dye16)
            z1buf[:, gs] = dyrbuf[:, gs] * (eacx * _dot(cg, sg)) - xdt_g * dst_x
            uvec[:, gs] = jnp.broadcast_to(
                jnp.sum(xdt_g * dst_x, axis=0, keepdims=True) + jnp.sum(dsn * cdx * sg32, axis=0, keepdims=True),
                (8, gw))
            dxbuf[:, gs] = dxbuf[:, gs] + dst_x
            yield

        zz = _dot(jnp.concatenate([z1buf[...], dxbuf[:, 0:db] * xbuf[:, 0:db]], axis=0).astype(BF16), et_ref[...])
        u1, u2, u3 = _split3(uvec[...])
        ulast = (_dot(u1, et_ref[...]) + _dot(u2, et_ref[...]) + _dot(u3, et_ref[...]))[0:1, :]
        is_last = (lax.broadcasted_iota(jnp.int32, (TT, 1), 0) == TT - 1).astype(F32)
        dac = dac - colacc.T + zz[0:TT] + is_last * ulast
        dda = _exact_01_dot(_tri(False).astype(F32).astype(BF16), dac)
        ddt = dda * a + zz[TT:2 * TT]
        dhd_ref[0, 1:2, :] = dhd_ref[0, 1:2, :] + jnp.sum(dda * dtv, axis=0, keepdims=True) * a
        ddtraw = ddt * _sigmoid(z)
        dhd_ref[0, 0:1, :] = dhd_ref[0, 0:1, :] + jnp.sum(ddtraw, axis=0, keepdims=True)
        dpt_ref[...] = ddtraw.astype(BF16)
        dxbuf[:, 0:db] = dyrbuf[...] * dsk_ref[...] + dxbuf[:, 0:db] * exbuf[0:TT, :]

        for lb in range(xbc_w // LANES):
            cs = slice(lb * LANES, (lb + 1) * LANES)
            dpre = dxbuf[:, cs] * dsbuf[:, cs]
            dprebuf[0:TT, cs] = dpre
            dwc_ref[0, SSM_CONV_K:SSM_CONV_K + 1, cs] = dwc_ref[0, SSM_CONV_K:SSM_CONV_K + 1, cs] + jnp.sum(
                dpre, axis=0, keepdims=True)
            draw = w_ref[SSM_CONV_K - 1:SSM_CONV_K, cs] * dpre
            for k in range(SSM_CONV_K - 1):
                ahead = SSM_CONV_K - 1 - k
                draw = draw + w_ref[k:k + 1, cs] * dprebuf[ahead:ahead + TT, cs]
            for k in range(SSM_CONV_K):
                moved = sh[k * TT:(k + 1) * TT, cs] if k < SSM_CONV_K - 1 else rawwin[TT:2 * TT, cs].astype(F32)
                dwc_ref[0, k:k + 1, cs] = dwc_ref[0, k:k + 1, cs] + jnp.sum(dpre * moved, axis=0, keepdims=True)
            dp_ref[:, db + lb * LANES:db + (lb + 1) * LANES] = draw.astype(BF16)
            yield

    bl, nt = dm.BL, dm.NT
    tile = lambda w, k: pl.BlockSpec((bl, TT, w), lambda i: (0, nt - 1 - i, k))
    halo = lambda w, k: pl.BlockSpec((bl, HALO_BLOCK, w),
                                     lambda i: (0, jnp.maximum((nt - 1 - i) * (TT // HALO_BLOCK) - 1, 0), k))
    fixed = lambda r, w: pl.BlockSpec((r, w), lambda i: (0, 0))
    sums = lambda w: pl.BlockSpec((bl, 8, w), lambda i: (0, 0, 0))
    kz = dm.WA // db
    kc = (dm.WA + 2 * db) // (2 * gn)
    proj3, dt3 = proj.reshape(bl, dm.Lp, dm.NP), projdt.reshape(bl, dm.Lp, DT_PAD)
    scratch = [((2 * TT, xbc_w), BF16), (((SSM_CONV_K - 1) * TT, xbc_w), F32), ((TT, xbc_w), F32),
               ((TT, xbc_w), F32), ((SSM_STATE, db), F32), ((TT, xbc_w), F32), ((TT, db), F32),
               ((TT + SMALL_HALO, xbc_w), F32), ((3 * TT, db), F32), ((TT, db), F32), ((TT, db), F32), ((8, db), F32)]
    how = [True] * 8 + ["keep"] + [False] * 9 + [True, True, "keep", "keep", "keep"] + [True] * len(scratch)
    (dp, dpt, dwc, dch, dhd), rode = _call(
        _per_sequence(setup, body, bl, how), "mix_b_bwd", (nt,),
        [tile(db, kz), tile(db, kz + 1), tile(2 * gn, kc), tile(DT_PAD, 0),
         halo(db, kz + 1), halo(2 * gn, kc), tile(db, 0), tile(db, 0),
         pl.BlockSpec((bl, 1, SSM_STATE, db), lambda i: (0, nt - 1 - i, 0, 0)),
         fixed(SSM_CONV_K, xbc_w), fixed(1, xbc_w), fixed(1, DT_PAD), fixed(1, DT_PAD),
         fixed(1, db), fixed(1, db), fixed(DT_PAD, db), fixed(db, DT_PAD), fixed((SSM_CONV_K - 1) * TT, 2 * TT)],
        [tile(dm.WB, 0), tile(DT_PAD, 0), sums(xbc_w), sums(db), sums(DT_PAD)],
        [jax.ShapeDtypeStruct((bl, dm.Lp, dm.WB), BF16), jax.ShapeDtypeStruct((bl, dm.Lp, DT_PAD), BF16),
         jax.ShapeDtypeStruct((bl, 8, xbc_w), F32), jax.ShapeDtypeStruct((bl, 8, db), F32),
         jax.ShapeDtypeStruct((bl, 8, DT_PAD), F32)],
        [pltpu.VMEM((bl,) + s, t) for s, t in scratch],
        ("arbitrary",),
        (proj3, proj3, proj3, dt3, proj3, proj3, dyb.reshape(bl, dm.Lp, db), yraw.reshape(bl, dm.Lp, db), sprev,
         conv_w, conv_b, dt_bias, a_log, dskx, norm_g, expand, expand_t, _shift_matrix()), rider)
    return (dp.reshape(dm.R, dm.WB), dpt.reshape(dm.R, DT_PAD), dwc, dch, dhd), rode


def _head_consts(dm):
    head_of = jnp.arange(dm.DB) // SSM_HEAD_DIM
    expand = (jnp.arange(DT_PAD)[:, None] == head_of[None, :]).astype(BF16)
    return expand, expand.T


def _ssm_params(lw, dm):
    pad_h = lambda v: jnp.pad(v, (0, DT_PAD - dm.H))[None]
    return (lw["ssm_conv_w"], lw["ssm_conv_b"][None], pad_h(lw["dt_bias"]), pad_h(lw["a_log"]),
            jnp.repeat(lw["d_skip"], SSM_HEAD_DIM)[None], lw["ssm_norm_g"][None])


def _layer_fwd(h, lw, w_in, w_out, cst, dm, next_bases=None, own_out=None):
    nxt = next_bases is not None
    n_next = len(next_bases) if nxt else 0

    def beside(rider, extra):
        return extra if rider is None else (rider if extra is None else _ride_both(rider, extra))

    (proj, projdt, hn), got = _fwd_in(
        h, lw["pre_g"][None], w_in, dm,
        beside(_ride_gather_ici(next_bases, 0, 2) if nxt else None,
               None if own_out is None else _ride_gather_ici([own_out])))
    ya = _mix_a_fwd(proj, lw["conv_a_w"], dm)
    (yb, yraw, sprev), got = _mix_b_fwd(
        proj, projdt, *_ssm_params(lw, dm), cst[0], dm,
        beside(_ride_gather_ici(got[:n_next], 1, 2) if nxt else None,
               None if own_out is None else _ride_gather_d2d(got[n_next:])))
    if own_out is not None:
        w_out = got[n_next].reshape(2 * dm.D, dm.D)
        got = got[:n_next]
    yc, u1 = _mix_c_fwd(proj, lw["conf_conv_w"], lw["conf_conv_b"][None], lw["conf_ln_g"][None],
                        lw["conf_ln_b"][None], dm)
    (h_new, m), got = _fwd_out(ya, yb, yc, w_out, h, lw["post_g"][None], dm, _ride_gather_d2d(got) if nxt else None)
    return h_new, (h, hn, proj, projdt, ya, yb, yc, u1, yraw, sprev, m), got, w_out


def _layer_bwd(dh, saved, lw, w_in, w_out, cst, dm, reduce=None, last=False):
    h_in, hn, proj, projdt, ya, yb, yc, u1, yraw, sprev, m = saved
    (dya, dyb, dyc, dwo, dpost), got = _bwd_out(dh, m, lw["post_g"][None], w_out, ya, yb, yc, dm,
                                                None if reduce is None else reduce.swap())
    dpa, dwa = _mix_a_bwd(proj, dya, lw["conv_a_w"], dm)
    (dpb, dpt, dwcv, dch, dhd), got = _mix_b_bwd(proj, projdt, dyb, yraw, sprev, *_ssm_params(lw, dm), cst[0],
                                                 cst[1], dm, None if reduce is None else reduce.to_owners(got))
    dpc, dwcf, dvc = _mix_c_bwd(proj, u1, dyc, lw["conf_conv_w"], lw["conf_ln_g"][None], lw["conf_ln_b"][None], dm)
    def own_reduce():
        pieces = _bwd_in_dw(hn, [dpa, dpb, dpc, dpt], dm)
        return _GradReduce([_grad_to_shards(pieces, dm), dwo.reshape(N_CHIPS, 2 * dm.D // N_CHIPS, dm.D)])

    rider = None if reduce is None else reduce.join(got)
    n_join = 0 if rider is None else len(rider.out_shapes)
    if last:
        mine = own_reduce()
        to_owners = mine.to_owners(_exchange("grad_swap_halves", mine.swap()))
        rider = to_owners if rider is None else _ride_both(rider, to_owners)
    (dh, dpre), got = _bwd_in_dx(dpa, dpb, dpc, dpt, w_in, h_in, dh, lw["pre_g"][None], dm, rider)
    if reduce is not None:
        reduce.finish(got[:n_join])
    if last:
        mine.finish(_exchange("grad_join_halves", mine.join(got[n_join:])))
    else:
        mine = own_reduce()
    dwcv, dch, dhd, dvc = (jnp.sum(a, axis=0) for a in (dwcv, dch, dhd, dvc))
    small = dict(pre_g=dpre[0], post_g=dpost[0], conv_a_w=jnp.sum(dwa, axis=0)[:CONV_A_K],
                 ssm_conv_w=dwcv[:SSM_CONV_K], ssm_conv_b=dwcv[SSM_CONV_K], ssm_norm_g=dch[0],
                 d_skip=jnp.sum(dch[1].reshape(dm.H, SSM_HEAD_DIM), axis=1), dt_bias=dhd[0, :dm.H],
                 a_log=dhd[1, :dm.H], conf_conv_w=jnp.sum(dwcf, axis=0)[:CONF_K], conf_conv_b=dvc[0],
                 conf_ln_g=dvc[1], conf_ln_b=dvc[2])
    return dh, mine, small


def _shard_runs(dm):
    ab = dm.WA + dm.WB
    order = [(0, 0, ab), (ab, dm.DT0, dm.H), (ab + dm.H, ab, dm.WC)]
    k = dm.NIN // N_CHIPS
    runs = []
    for s in range(N_CHIPS):
        for o0, m0, wd in order:
            lo, hi = max(o0, s * k), min(o0 + wd, (s + 1) * k)
            if lo < hi:
                runs.append((s, lo - s * k, m0 + lo - o0, hi - lo))
    return runs


def _w_in_from_shards(base, dm):
    tr = _row_tile(dm.D, 256)
    k = dm.NIN // N_CHIPS
    runs = _shard_runs(dm)

    def body(in_ref, out_ref):
        for s, sc, mc, wd in runs:
            out_ref[:, mc:mc + wd] = in_ref[s, :, sc:sc + wd]
        out_ref[:, dm.DT0 + dm.H:dm.NP] = jnp.zeros((tr, dm.NP - dm.DT0 - dm.H), BF16)

    return pl.pallas_call(
        body, name="w_in_from_shards", grid=(dm.D // tr,),
        in_specs=[pl.BlockSpec((N_CHIPS, tr, k), lambda r: (0, r, 0))],
        out_specs=pl.BlockSpec((tr, dm.NP), lambda r: (r, 0)),
        out_shape=jax.ShapeDtypeStruct((dm.D, dm.NP), BF16),
        compiler_params=_params(("parallel",)),
    )(base)


def _grad_to_shards(pieces, dm):
    tr = _row_tile(dm.D, 256)
    k = dm.NIN // N_CHIPS
    starts = [0, dm.WA, dm.WA + dm.WB, dm.DT0]
    widths = [dm.WA, dm.WB, dm.WC, DT_PAD]
    runs = _shard_runs(dm)

    def body(a_ref, b_ref, c_ref, t_ref, out_ref):
        refs = (a_ref, b_ref, c_ref, t_ref)
        for s, sc, mc, wd in runs:
            for p in range(4):
                lo, hi = max(mc, starts[p]), min(mc + wd, starts[p] + widths[p])
                if lo < hi:
                    out_ref[s, :, sc + lo - mc:sc + hi - mc] = refs[p][:, lo - starts[p]:hi - starts[p]].astype(BF16)

    return pl.pallas_call(
        body, name="grad_to_shards", grid=(dm.D // tr,),
        in_specs=[pl.BlockSpec((tr, w), lambda r: (r, 0)) for w in widths],
        out_specs=pl.BlockSpec((N_CHIPS, tr, k), lambda r: (0, r, 0)),
        out_shape=jax.ShapeDtypeStruct((N_CHIPS, dm.D, k), BF16),
        compiler_params=_params(("parallel",)),
    )(*pieces)


def _place_own(w, layer, me):
    _, rows, cols = w.shape
    tr = _row_tile(rows, 256)

    def body(me_ref, w_ref, out_ref):
        out_ref[0] = w_ref[0].astype(BF16)

    return pl.pallas_call(
        body, name="place_own",
        grid_spec=pltpu.PrefetchScalarGridSpec(
            num_scalar_prefetch=1, grid=(rows // tr,),
            in_specs=[pl.BlockSpec((1, tr, cols), lambda r, me_ref: (layer, r, 0))],
            out_specs=pl.BlockSpec((1, tr, cols), lambda r, me_ref: (me_ref[0], r, 0))),
        out_shape=jax.ShapeDtypeStruct((N_CHIPS, rows, cols), BF16),
        compiler_params=_params(("parallel",)),
    )(me, w)


def _add_halves(g, got, c, name):
    _, _, rows, cols = g.shape
    tr = _row_tile(rows, 256)

    def body(c_ref, g_ref, got_ref, out_ref):
        out_ref[0] = (g_ref[0, 0].astype(F32) + got_ref[0].astype(F32)).astype(BF16)

    return pl.pallas_call(
        body, name=name,
        grid_spec=pltpu.PrefetchScalarGridSpec(
            num_scalar_prefetch=1, grid=(N_CHIPS, rows // tr),
            in_specs=[pl.BlockSpec((1, 1, tr, cols), lambda s, r, c_ref: (s, c_ref[0], r, 0)),
                      pl.BlockSpec((1, tr, cols), lambda s, r, c_ref: (s, r, 0))],
            out_specs=pl.BlockSpec((1, tr, cols), lambda s, r, c_ref: (s, r, 0))),
        out_shape=jax.ShapeDtypeStruct((N_CHIPS, rows, cols), BF16),
        compiler_params=_params(("parallel", "parallel")),
    )(c, g, got)


def _add_owner(p, got, where, name):
    _, rows, cols = p.shape
    tr = _row_tile(rows, 256)

    def body(w_ref, p_ref, got_ref, out_ref):
        acc = p_ref[0].astype(F32)
        for j in range(3):
            acc = acc + got_ref[j].astype(F32)
        out_ref[0] = acc

    return pl.pallas_call(
        body, name=name,
        grid_spec=pltpu.PrefetchScalarGridSpec(
            num_scalar_prefetch=1, grid=(rows // tr,),
            in_specs=[pl.BlockSpec((1, tr, cols), lambda r, w_ref: (w_ref[0], r, 0)),
                      pl.BlockSpec((3, tr, cols), lambda r, w_ref: (0, r, 0))],
            out_specs=pl.BlockSpec((1, tr, cols), lambda r, w_ref: (w_ref[1], r, 0))),
        out_shape=jax.ShapeDtypeStruct((2, rows, cols), F32),
        compiler_params=_params(("parallel",)),
    )(where, p, got)


class _GradReduce:
    def __init__(self, gs):
        self.gs = [g.reshape((N_CHIPS, 2, g.shape[1] // 2) + g.shape[2:]) for g in gs]
        self.c = lax.axis_index("c").astype(jnp.int32).reshape(1)
        chip = (2 * lax.axis_index("x") + lax.axis_index("y")).astype(jnp.int32)
        self.where = jnp.stack([chip, self.c[0]])
        self.result = None

    def swap(self):
        return _ride_swap_halves(self.gs)

    def to_owners(self, got):
        self.ps = [_add_halves(g, r, self.c, "grad_add_sibling_" + n) for g, r, n in zip(self.gs, got, ("in", "out"))]
        return _ride_to_owners(self.ps)

    def join(self, got):
        qs = [_add_owner(p, r, self.where, "grad_add_chips_" + n) for p, r, n in zip(self.ps, got, ("in", "out"))]
        return _ride_join_halves(qs)

    def finish(self, got):
        self.result = [a.reshape((a.shape[0] * a.shape[1],) + a.shape[2:]) for a in got]


def _adamw_math(w, g, m, v):
    m = ADAM_B1 * m + (1.0 - ADAM_B1) * g
    v = ADAM_B2 * v + (1.0 - ADAM_B2) * (g * g)
    m_hat = m / (1.0 - ADAM_B1 ** ADAM_STEP)
    v_hat = v / (1.0 - ADAM_B2 ** ADAM_STEP)
    delta = -ADAM_LR * (m_hat / (jnp.sqrt(v_hat) + ADAM_EPS) + ADAM_WD * w)
    return delta, m, v


def _adamw_small(w, g, m, v, name):
    def body(w_ref, g_ref, m_ref, v_ref, d_out, m_out, v_out):
        d_out[...], m_out[...], v_out[...] = _adamw_math(w_ref[...], g_ref[...], m_ref[...], v_ref[...])

    shape = jax.ShapeDtypeStruct(w.shape, F32)
    return pl.pallas_call(body, name="adamw_" + name, out_shape=[shape, shape, shape],
                          compiler_params=_params())(w, g, m, v)


def _adamw_layer(i, w, g, m, v, prev, name):
    depth, rows, cols = w.shape
    tr = _row_tile(rows, 256)
    n_prev = 0 if prev is None else 4

    def body(*refs):
        w_ref, g_ref, m_ref, v_ref = refs[:4]
        g_out, d_out, m_out, v_out = refs[4 + n_prev:]
        gv = g_ref[...]
        g_out[0] = gv
        d_out[0], m_out[0], v_out[0] = _adamw_math(w_ref[0], gv, m_ref[0], v_ref[0])

    lay = pl.BlockSpec((1, tr, cols), lambda r: (i, r, 0))
    shape = jax.ShapeDtypeStruct(w.shape, F32)
    return pl.pallas_call(
        body, name="adamw_" + name, grid=(rows // tr,),
        in_specs=[lay, pl.BlockSpec((tr, cols), lambda r: (r, 0)), lay, lay] + [ANY] * n_prev,
        out_specs=[lay] * 4, out_shape=[shape] * 4,
        input_output_aliases={4 + k: k for k in range(n_prev)},
        compiler_params=_params(("parallel",)),
    )(w, g, m, v, *(prev or ()))


def _adamw_cols_major(w, gs, m, v, name):
    depth, rows, cols = w.shape
    tr = max(t for t in range(1, 129) if cols % t == 0)
    wt, mt, vt = (jnp.transpose(a, (2, 0, 1)) for a in (w, m, v))
    gt = jnp.transpose(jnp.stack(gs, axis=0), (2, 0, 1))

    def body(w_ref, g_ref, m_ref, v_ref, g_out, d_out, m_out, v_out):
        gv = g_ref[...]
        g_out[...] = gv
        d_out[...], m_out[...], v_out[...] = _adamw_math(w_ref[...], gv, m_ref[...], v_ref[...])

    spec = pl.BlockSpec((tr, depth, rows), lambda r: (r, 0, 0))
    shape = jax.ShapeDtypeStruct((cols, depth, rows), F32)
    outs = pl.pallas_call(body, name="adamw_" + name, grid=(cols // tr,), in_specs=[spec] * 4, out_specs=[spec] * 4,
                          out_shape=[shape] * 4, compiler_params=_params(("parallel",)))(wt, gt, mt, vt)
    return [jnp.transpose(a, (1, 2, 0)) for a in outs]


def _sum_leading(buf, name):
    n, rows, cols = buf.shape
    tr = _row_tile(rows, rows)

    def body(in_ref, out_ref):
        acc = in_ref[0]
        for k in range(1, n):
            acc = acc + in_ref[k]
        out_ref[...] = acc

    return pl.pallas_call(
        body, name=name, grid=(rows // tr,),
        in_specs=[pl.BlockSpec((n, tr, cols), lambda i: (0, i, 0))],
        out_specs=pl.BlockSpec((tr, cols), lambda i: (i, 0)),
        out_shape=jax.ShapeDtypeStruct((rows, cols), F32),
        compiler_params=_params(("parallel",)),
    )(buf)


_SHARDED_SMALL = ("meta", "conv_a_w", "ssm_conv_w", "conf_conv_w")
_LAYER_SMALL = ("pre_g", "post_g", "conv_a_w", "ssm_conv_w", "ssm_conv_b", "dt_bias", "a_log", "d_skip",
                "ssm_norm_g", "conf_conv_w", "conf_conv_b", "conf_ln_g", "conf_ln_b")
_WEIGHTS = ("meta", "pre_g", "post_g", "w_in", "w_out", "conv_a_w", "ssm_conv_w", "ssm_conv_b", "dt_bias", "a_log",
            "d_skip", "ssm_norm_g", "conf_conv_w", "conf_conv_b", "conf_ln_g", "conf_ln_b")


def _shard_last(a):
    return jnp.moveaxis(a.reshape(a.shape[:-1] + (N_CHIPS, a.shape[-1] // N_CHIPS)), -2, 0)


def _with_own_block(a, n, at):
    return lax.dynamic_update_index_in_dim(jnp.zeros((n,) + a.shape, a.dtype), a, at, 0)


def _with_own_columns(a, chip):
    k = a.shape[-1]
    return lax.dynamic_update_slice_in_dim(jnp.zeros(a.shape[:-1] + (N_CHIPS * k,), a.dtype), a, chip * k, a.ndim - 1)


def kernel(x, meta, pre_g, post_g, w_in, w_out, conv_a_w, ssm_conv_w, ssm_conv_b, dt_bias, a_log, d_skip, ssm_norm_g, conf_conv_w, conf_conv_b, conf_ln_g, conf_ln_b, loss_target, m_meta, m_pre_g, m_post_g, m_w_in, m_w_out, m_conv_a_w, m_ssm_conv_w, m_ssm_conv_b, m_dt_bias, m_a_log, m_d_skip, m_ssm_norm_g, m_conf_conv_w, m_conf_conv_b, m_conf_ln_g, m_conf_ln_b, v_meta, v_pre_g, v_post_g, v_w_in, v_w_out, v_conv_a_w, v_ssm_conv_w, v_ssm_conv_b, v_dt_bias, v_a_log, v_d_skip, v_ssm_norm_g, v_conf_conv_w, v_conf_conv_b, v_conf_ln_g, v_conf_ln_b):
    w = dict(meta=meta, pre_g=pre_g, post_g=post_g, w_in=w_in, w_out=w_out, conv_a_w=conv_a_w,
             ssm_conv_w=ssm_conv_w, ssm_conv_b=ssm_conv_b, dt_bias=dt_bias, a_log=a_log, d_skip=d_skip,
             ssm_norm_g=ssm_norm_g, conf_conv_w=conf_conv_w, conf_conv_b=conf_conv_b, conf_ln_g=conf_ln_g,
             conf_ln_b=conf_ln_b)
    mom = dict(meta=m_meta, pre_g=m_pre_g, post_g=m_post_g, w_in=m_w_in, w_out=m_w_out, conv_a_w=m_conv_a_w,
               ssm_conv_w=m_ssm_conv_w, ssm_conv_b=m_ssm_conv_b, dt_bias=m_dt_bias, a_log=m_a_log, d_skip=m_d_skip,
               ssm_norm_g=m_ssm_norm_g, conf_conv_w=m_conf_conv_w, conf_conv_b=m_conf_conv_b,
               conf_ln_g=m_conf_ln_g, conf_ln_b=m_conf_ln_b)
    vel = dict(meta=v_meta, pre_g=v_pre_g, post_g=v_post_g, w_in=v_w_in, w_out=v_w_out, conv_a_w=v_conv_a_w,
               ssm_conv_w=v_ssm_conv_w, ssm_conv_b=v_ssm_conv_b, dt_bias=v_dt_bias, a_log=v_a_log, d_skip=v_d_skip,
               ssm_norm_g=v_ssm_norm_g, conf_conv_w=v_conf_conv_w, conf_conv_b=v_conf_conv_b,
               conf_ln_g=v_conf_ln_g, conf_ln_b=v_conf_ln_b)
    bl, seq, d = x.shape
    dm = Dims(bl, seq, d)
    depth = w_in.shape[0]
    chip = (2 * lax.axis_index("x") + lax.axis_index("y")).astype(jnp.int32)
    dev = 2 * chip + lax.axis_index("c").astype(jnp.int32)
    cst = _head_consts(dm)

    bases = [[_place_own(w_in, i, chip.reshape(1)), _place_own(w_out, i, chip.reshape(1))] for i in range(depth)]
    first_in, small_w = _gather_ici_relayed(
        [bases[0][0]], _ride_gather_small([_with_own_columns(w[n], chip) for n in _SHARDED_SMALL]))
    full = dict(w)
    full.update(zip(_SHARDED_SMALL, small_w))
    h, gathered = _embed(x, full["meta"], dm, _ride_gather_d2d(first_in))
    saved, proj_w = [], []
    for i in range(depth):
        lw = {n: full[n][i] for n in _LAYER_SMALL}
        w_in_i = _w_in_from_shards(gathered[0], dm)
        h, keep, gathered, w_out_i = _layer_fwd(
            h, lw, w_in_i, None if i == 0 else gathered[1].reshape(2 * d, d), cst, dm,
            bases[i + 1] if i + 1 < depth else None, bases[0][1] if i == 0 else None)
        proj_w.append((w_in_i, w_out_i))
        saved.append(keep)

    dh, loss = _loss_head(h, loss_target, dm)

    small_g = {n: [None] * depth for n in _LAYER_SMALL}
    big = {"w_in": None, "w_out": None}
    g_in = [None] * depth
    reduce = None
    for i in reversed(range(depth)):
        lw = {n: full[n][i] for n in _LAYER_SMALL}
        dh, mine, sg = _layer_bwd(dh, saved[i], lw, proj_w[i][0], proj_w[i][1], cst, dm, reduce, last=i == 0)
        for n in _LAYER_SMALL:
            small_g[n][i] = sg[n]
        if reduce is not None:
            g_in[i + 1] = reduce.result[0]
            big["w_out"] = _adamw_layer(i + 1, w_out, reduce.result[1], m_w_out, v_w_out, big["w_out"], "w_out")
        reduce = mine
    g_in[0] = reduce.result[0]
    big["w_out"] = _adamw_layer(0, w_out, reduce.result[1], m_w_out, v_w_out, big["w_out"], "w_out")
    grad_x, gmeta = _unembed(dh, dm)

    g = {n: jnp.stack(v) for n, v in small_g.items()}
    g["meta"] = gmeta
    small = [n for n in _WEIGHTS if n not in ("w_in", "w_out")]
    flat = jnp.concatenate([g[n].reshape(-1) for n in small] + [loss.reshape(1)])
    rows = -(-flat.shape[0] // (16 * LANES)) * 16
    flat = jnp.pad(flat, (0, rows * LANES - flat.shape[0])).reshape(rows, LANES)
    parts = _gather_all(_with_own_block(flat, N_DEV, dev))
    total = _sum_leading(parts, "small_grads_sum").reshape(-1)
    big["w_in"] = _adamw_cols_major(w_in, g_in, m_w_in, v_w_in, "w_in")
    grads, deltas, new_m, new_v = {}, {}, {}, {}
    off = 0
    for n in small:
        size = g[n].size
        fullg = total[off:off + size].reshape(g[n].shape)
        off += size
        if n in _SHARDED_SMALL:
            fullg = lax.dynamic_index_in_dim(_shard_last(fullg), chip, axis=0, keepdims=False)
        grads[n] = fullg
        deltas[n], new_m[n], new_v[n] = _adamw_small(w[n], fullg, mom[n], vel[n], n)
    for n in ("w_in", "w_out"):
        grads[n], deltas[n], new_m[n], new_v[n] = big[n]
    loss = total[off]

    return (loss, grad_x, *[grads[n] for n in _WEIGHTS], *[deltas[n] for n in _WEIGHTS],
            *[new_m[n] for n in _WEIGHTS], *[new_v[n] for n in _WEIGHTS])
```

```python
import jax
import jax.numpy as jnp
from jax import lax
from jax.experimental import pallas as pl
from jax.experimental.pallas import tpu as pltpu

F32 = jnp.float32
BF16 = jnp.bfloat16

N_META = 16
TT = 128
SSM_STATE = 128
SSM_GROUPS = 2
SSM_HEAD_DIM = 64
CONV_A_K = 3
SSM_CONV_K = 4
CONF_K = 31
NORM_EPS = 1e-6
LN_EPS = 1e-5
LANES = 128
MXU_DIM = 256
DT_PAD = LANES
CONF_HALO = 32
SMALL_HALO = 8
VMEM_LIMIT = 56 * 1024 * 1024
N_CHIPS = 4
N_DEV = 8

ADAM_LR = 0.001
ADAM_B1 = 0.9
ADAM_B2 = 0.999
ADAM_EPS = 1e-08
ADAM_WD = 0.01
ADAM_STEP = 10

MESH = pl.DeviceIdType.MESH
ANY = pl.BlockSpec(memory_space=pl.ANY)


class Dims:
    def __init__(self, bl, seq, d):
        self.BL, self.S, self.D = bl, seq, d
        self.L = seq + N_META
        self.Lp = -(-self.L // TT) * TT
        self.NT = self.Lp // TT
        self.R = bl * self.Lp
        self.DA = d // 2
        self.DB = d
        self.DC = d // 2
        self.H = self.DB // SSM_HEAD_DIM
        self.HPG = self.H // SSM_GROUPS
        self.GN = SSM_GROUPS * SSM_STATE
        self.WA = 4 * self.DA
        self.WB = 2 * self.DB + 2 * self.GN
        self.WC = 3 * self.DC
        self.DT0 = self.WA + self.WB + self.WC
        self.NP = -(-(self.DT0 + DT_PAD) // (5 * MXU_DIM)) * (5 * MXU_DIM)
        self.NIN = self.WA + self.WB + self.H + self.WC
        self.XBC = self.DB + 2 * self.GN
        assert self.H % 2 == 0 and self.HPG % 2 == 0 and self.H <= DT_PAD
        assert self.DA % LANES == 0 and (self.WA + self.WB) % self.DC == 0 and self.WA % self.DB == 0


def _row_tile(n, target):
    best = None
    for t in range(16, min(n, target) + 1, 16):
        if n % t == 0:
            best = t
    assert best is not None
    return best


def _col_tile(n, target):
    best = None
    for t in range(LANES, min(n, target) + 1, LANES):
        if n % t == 0:
            best = t
    assert best is not None
    return best


def _params(sem=None):
    return pltpu.CompilerParams(dimension_semantics=sem, vmem_limit_bytes=VMEM_LIMIT)


def _sigmoid(x):
    return 1.0 / (1.0 + jnp.exp(-x))


def _silu_and_grad(x):
    s = _sigmoid(x)
    y = x * s
    return y, s + y * (1.0 - s)


def _dot(a, b):
    return jnp.dot(a, b, preferred_element_type=F32)


def _dot_nt(a, b):
    return lax.dot_general(a, b, (((1,), (1,)), ((), ())), preferred_element_type=F32)


def _dot_tn(a, b):
    return lax.dot_general(a, b, (((0,), (0,)), ((), ())), preferred_element_type=F32)


def _split3(x):
    x1 = x.astype(BF16)
    r1 = x - x1.astype(F32)
    x2 = r1.astype(BF16)
    x3 = (r1 - x2.astype(F32)).astype(BF16)
    return x1, x2, x3


class Rider:
    def __init__(self, plan, ins, out_shapes, aliases, nsem):
        self.plan, self.ins, self.out_shapes, self.aliases, self.nsem = plan, list(ins), list(out_shapes), aliases, nsem


def _place():
    x, y, c = lax.axis_index("x"), lax.axis_index("y"), lax.axis_index("c")
    chips = [(1 - x, y), (x, 1 - y), (1 - x, 1 - y)]
    return x, y, c, chips


def _remote(k, src, dst, to, send_sems, recv_sems):
    return pltpu.make_async_remote_copy(src_ref=src, dst_ref=dst, send_sem=send_sems.at[k], recv_sem=recv_sems.at[k],
                                        device_id=to, device_id_type=MESH)


def _call(body, name, grid, in_specs, out_specs, out_shape, scratch_shapes, sem, args, rider=None):
    if rider is None:
        outs = pl.pallas_call(body, name=name, grid=grid, in_specs=in_specs, out_specs=out_specs, out_shape=out_shape,
                              scratch_shapes=scratch_shapes, compiler_params=_params(sem))(*args)
        return list(outs), []
    n_in, n_out, n_scr = len(args), len(out_shape), len(scratch_shapes)
    r_in, r_out = len(rider.ins), len(rider.out_shapes)

    def hosted(*refs):
        ins, rins = refs[:n_in], refs[n_in:n_in + r_in]
        o0 = n_in + r_in
        outs, routs = refs[o0:o0 + n_out], refs[o0 + n_out:o0 + n_out + r_out]
        scr = refs[o0 + n_out + r_out:o0 + n_out + r_out + n_scr]
        send_sems, recv_sems = refs[o0 + n_out + r_out + n_scr:]
        first = pl.program_id(0) == 0
        last = pl.program_id(0) == grid[0] - 1
        for ax in range(1, len(grid)):
            first = jnp.logical_and(first, pl.program_id(ax) == 0)
            last = jnp.logical_and(last, pl.program_id(ax) == grid[ax] - 1)

        @pl.when(first)
        def _():
            starts, _ = rider.plan(rins, routs, send_sems, recv_sems)
            for cp in starts:
                cp.start()

        body(*ins, *outs, *scr)

        @pl.when(last)
        def _():
            _, waits = rider.plan(rins, routs, send_sems, recv_sems)
            for wait in waits:
                wait()

    res = pl.pallas_call(
        hosted, name=name, grid=grid,
        in_specs=list(in_specs) + [ANY] * r_in, out_specs=list(out_specs) + [ANY] * r_out,
        out_shape=list(out_shape) + rider.out_shapes,
        input_output_aliases={n_in + k: n_out + v for k, v in rider.aliases.items()},
        scratch_shapes=list(scratch_shapes) + [pltpu.SemaphoreType.DMA((rider.nsem,)),
                                               pltpu.SemaphoreType.DMA((rider.nsem,))],
        compiler_params=_params(("arbitrary",) * len(grid)),
    )(*args, *rider.ins)
    return list(res[:n_out]), list(res[n_out:])


def _exchange(name, rider):
    r_in, r_out = len(rider.ins), len(rider.out_shapes)

    def body(*refs):
        rins, routs = refs[:r_in], refs[r_in:r_in + r_out]
        send_sems, recv_sems = refs[r_in + r_out:]
        starts, waits = rider.plan(rins, routs, send_sems, recv_sems)
        for cp in starts:
            cp.start()
        for wait in waits:
            wait()

    res = pl.pallas_call(
        body, name=name, in_specs=[ANY] * r_in, out_specs=[ANY] * r_out, out_shape=rider.out_shapes,
        input_output_aliases=dict(rider.aliases),
        scratch_shapes=[pltpu.SemaphoreType.DMA((rider.nsem,)), pltpu.SemaphoreType.DMA((rider.nsem,))],
    )(*rider.ins)
    return list(res)


def _same(arrays):
    return [jax.ShapeDtypeStruct(a.shape, a.dtype) for a in arrays]


class _SemsFrom:
    def __init__(self, sems, first):
        self.sems, self.first = sems, first

    @property
    def at(self):
        return self

    def __getitem__(self, k):
        return self.sems.at[self.first + k]


def _ride_both(r1, r2):
    n_in, n_out = len(r1.ins), len(r1.out_shapes)

    def plan(ins, outs, ss, rs):
        s1, w1 = r1.plan(ins[:n_in], outs[:n_out], ss, rs)
        s2, w2 = r2.plan(ins[n_in:], outs[n_out:], _SemsFrom(ss, r1.nsem), _SemsFrom(rs, r1.nsem))
        return s1 + s2, w1 + w2

    aliases = dict(r1.aliases)
    aliases.update({n_in + k: n_out + v for k, v in r2.aliases.items()})
    return Rider(plan, r1.ins + r2.ins, r1.out_shapes + r2.out_shapes, aliases, r1.nsem + r2.nsem)


def _ride_gather_ici(bases, part=0, nparts=1):
    n = len(bases)

    def plan(ins, outs, ss, rs):
        x, y, c, chips = _place()
        me = 2 * x + y
        starts, waits = [], []
        for a in range(n):
            half = outs[a].shape[1] // 2
            mine = pl.ds(c * half + part * (half // nparts), half // nparts)
            for j, chip in enumerate(chips):
                cp = _remote(3 * a + j, outs[a].at[me, mine], outs[a].at[me, mine], (*chip, c), ss, rs)
                got = outs[a].at[2 * chip[0] + chip[1], mine]
                starts.append(cp)
                waits += [cp.wait_send, _remote(3 * a + j, got, got, (*chip, c), ss, rs).wait_recv]
        return starts, waits

    return Rider(plan, bases, _same(bases), {a: a for a in range(n)}, 3 * n)


def _gather_ici_relayed(bases, also):
    n, m = len(bases), len(also.ins)

    def body(*refs):
        outs = refs[n + m:2 * n + m]
        ss, rs = refs[2 * (n + m):]
        beside, beside_waits = also.plan(refs[n:n + m], refs[2 * n + m:2 * (n + m)],
                                         _SemsFrom(ss, 4 * n), _SemsFrom(rs, 4 * n))
        for cp in beside:
            cp.start()
        x, y, c, _ = _place()
        me, xn, yn, dg = 2 * x + y, 2 * (1 - x) + y, 2 * x + (1 - y), 2 * (1 - x) + (1 - y)
        to_x, to_y = (1 - x, y, c), (x, 1 - y, c)
        sends = []

        def send(k, piece, to):
            cp = _remote(k, piece, piece, to, ss, rs)
            cp.start()
            sends.append(cp)

        def arrived(k, piece, frm):
            _remote(k, piece, piece, frm, ss, rs).wait_recv()

        rows = []
        for a in range(n):
            half = outs[a].shape[1] // 2
            rows.append((pl.ds(c * half, half), pl.ds(c * half, half // 2), pl.ds(c * half + half // 2, half // 2)))
            send(4 * a, outs[a].at[me, rows[a][0]], to_x)
            send(4 * a + 1, outs[a].at[me, rows[a][0]], to_y)
        for a in range(n):
            mine, lo, hi = rows[a]
            arrived(4 * a, outs[a].at[xn, mine], to_x)
            send(4 * a + 2, outs[a].at[xn, lo], to_y)
            arrived(4 * a + 1, outs[a].at[yn, mine], to_y)
            send(4 * a + 3, outs[a].at[yn, hi], to_x)
        for a in range(n):
            mine, lo, hi = rows[a]
            arrived(4 * a + 2, outs[a].at[dg, lo], to_y)
            arrived(4 * a + 3, outs[a].at[dg, hi], to_x)
        for cp in sends:
            cp.wait_send()
        for wait in beside_waits:
            wait()

    aliases = {a: a for a in range(n)}
    aliases.update({n + k: n + v for k, v in also.aliases.items()})
    nsem = 4 * n + also.nsem
    res = pl.pallas_call(
        body, name="gather_ici_first", in_specs=[ANY] * (n + m), out_specs=[ANY] * (n + len(also.out_shapes)),
        out_shape=_same(bases) + also.out_shapes, input_output_aliases=aliases,
        scratch_shapes=[pltpu.SemaphoreType.DMA((nsem,)), pltpu.SemaphoreType.DMA((nsem,))],
    )(*bases, *also.ins)
    return list(res[:n]), list(res[n:])


def _ride_gather_d2d(bases):
    n = len(bases)

    def plan(ins, outs, ss, rs):
        x, y, c, chips = _place()
        sib = (x, y, 1 - c)
        starts, waits = [], []
        for a in range(n):
            half = outs[a].shape[1] // 2
            for j, chip in enumerate(chips):
                frm = 2 * chip[0] + chip[1]
                got = outs[a].at[frm, pl.ds(c * half, half)]
                theirs = outs[a].at[frm, pl.ds((1 - c) * half, half)]
                cp = _remote(3 * a + j, got, got, sib, ss, rs)
                starts.append(cp)
                waits += [cp.wait_send, _remote(3 * a + j, theirs, theirs, sib, ss, rs).wait_recv]
        return starts, waits

    return Rider(plan, bases, _same(bases), {a: a for a in range(n)}, 3 * n)


def _ride_gather_small(bases):
    n = len(bases)

    def plan(ins, outs, ss, rs):
        x, y, c, chips = _place()
        me = 2 * x + y
        starts, waits = [], []
        for a in range(n):
            k = outs[a].shape[-1] // N_CHIPS
            lead = (slice(None),) * (len(outs[a].shape) - 1)
            at = (lambda s: pl.multiple_of(s * k, LANES)) if k % LANES == 0 else (lambda s: s * k)
            cols = lambda s: outs[a].at[lead + (pl.ds(at(s), k),)]
            for j, chip in enumerate(chips):
                cp = _remote(3 * a + j, cols(me), cols(me), (*chip, c), ss, rs)
                got = cols(2 * chip[0] + chip[1])
                starts.append(cp)
                waits += [cp.wait_send, _remote(3 * a + j, got, got, (*chip, c), ss, rs).wait_recv]
        return starts, waits

    return Rider(plan, bases, _same(bases), {a: a for a in range(n)}, 3 * n)


def _ride_swap_halves(gs):
    n = len(gs)

    def plan(ins, outs, ss, rs):
        x, y, c, _ = _place()
        cps = [_remote(a, ins[a].at[:, 1 - c], outs[a], (x, y, 1 - c), ss, rs) for a in range(n)]
        return cps, [cp.wait for cp in cps]

    shapes = [jax.ShapeDtypeStruct((g.shape[0],) + g.shape[2:], g.dtype) for g in gs]
    return Rider(plan, gs, shapes, {}, n)


def _ride_to_owners(ps):
    n = len(ps)

    def plan(ins, outs, ss, rs):
        x, y, c, chips = _place()
        cps = []
        for a in range(n):
            for j, chip in enumerate(chips):
                cps.append(_remote(3 * a + j, ins[a].at[2 * chip[0] + chip[1]], outs[a].at[j], (*chip, c), ss, rs))
        return cps, [cp.wait for cp in cps]

    shapes = [jax.ShapeDtypeStruct((3,) + p.shape[1:], p.dtype) for p in ps]
    return Rider(plan, ps, shapes, {}, 3 * n)


def _ride_join_halves(qs):
    n = len(qs)

    def plan(ins, outs, ss, rs):
        x, y, c, _ = _place()
        sib = (x, y, 1 - c)
        starts, waits = [], []
        for a in range(n):
            cp = _remote(a, outs[a].at[c], outs[a].at[c], sib, ss, rs)
            starts.append(cp)
            waits += [cp.wait_send, _remote(a, outs[a].at[1 - c], outs[a].at[1 - c], sib, ss, rs).wait_recv]
        return starts, waits

    return Rider(plan, qs, _same(qs), {a: a for a in range(n)}, n)


def _gather_all(base):
    def body(in_ref, out_ref, ss, rs):
        x, y, c, chips = _place()
        sib = (x, y, 1 - c)
        block = lambda cx, cy, cc: out_ref.at[4 * cx + 2 * cy + cc]
        mine = block(x, y, c)
        first = [_remote(j, mine, mine, (*chip, c), ss, rs) for j, chip in enumerate(chips)]
        first.append(_remote(3, mine, mine, sib, ss, rs))
        for cp in first:
            cp.start()
        passed = []
        for j, chip in enumerate(chips):
            got = block(*chip, c)
            _remote(j, got, got, (*chip, c), ss, rs).wait_recv()
            passed.append(_remote(4 + j, got, got, sib, ss, rs))
            passed[-1].start()
        theirs = block(x, y, 1 - c)
        _remote(3, theirs, theirs, sib, ss, rs).wait_recv()
        for j, chip in enumerate(chips):
            got = block(*chip, 1 - c)
            _remote(4 + j, got, got, sib, ss, rs).wait_recv()
        for cp in first + passed:
            cp.wait_send()

    return pl.pallas_call(
        body, name="small_grads_gather_all", in_specs=[ANY], out_specs=ANY,
        out_shape=jax.ShapeDtypeStruct(base.shape, base.dtype), input_output_aliases={0: 0},
        scratch_shapes=[pltpu.SemaphoreType.DMA((N_DEV - 1,)), pltpu.SemaphoreType.DMA((N_DEV - 1,))],
    )(base)


def _embed(x, meta, dm, rider=None):
    dc = _col_tile(dm.D, 256)
    s, lp = dm.S, dm.Lp

    def body(x_ref, meta_ref, h_ref):
        h_ref[0:N_META, :] = meta_ref[...]
        h_ref[N_META:N_META + s, :] = x_ref[0]
        if lp > N_META + s:
            h_ref[N_META + s:lp, :] = jnp.zeros((lp - N_META - s, dc), F32)

    (h,), rode = _call(
        body, "embed", (dm.BL, dm.D // dc),
        [pl.BlockSpec((1, s, dc), lambda b, j: (b, 0, j)), pl.BlockSpec((N_META, dc), lambda b, j: (0, j))],
        [pl.BlockSpec((lp, dc), lambda b, j: (b, j))], [jax.ShapeDtypeStruct((dm.R, dm.D), F32)],
        [], ("parallel", "parallel"), (x, meta), rider)
    return h, rode


def _loss_head(h, target, dm):
    dc = _col_tile(dm.D, 256)
    s, lp, nj = dm.S, dm.Lp, dm.D // dc

    def body(h_ref, t_ref, dh_ref, l_ref):
        diff = h_ref[N_META:N_META + s, :] - t_ref[0]
        dh_ref[0:N_META, :] = jnp.zeros((N_META, dc), F32)
        dh_ref[N_META:N_META + s, :] = diff * (1.0 / dm.D)
        if lp > N_META + s:
            dh_ref[N_META + s:lp, :] = jnp.zeros((lp - N_META - s, dc), F32)
        l_ref[...] = jnp.full((8, LANES), (0.5 / dm.D) * jnp.sum(diff * diff), F32)

    dh, part = pl.pallas_call(
        body, name="loss_head", grid=(dm.BL, nj),
        in_specs=[pl.BlockSpec((lp, dc), lambda b, j: (b, j)),
                  pl.BlockSpec((1, s, dc), lambda b, j: (b, 0, j))],
        out_specs=[pl.BlockSpec((lp, dc), lambda b, j: (b, j)),
                   pl.BlockSpec((8, LANES), lambda b, j: (b * nj + j, 0))],
        out_shape=[jax.ShapeDtypeStruct((dm.R, dm.D), F32),
                   jax.ShapeDtypeStruct((dm.BL * nj * 8, LANES), F32)],
        compiler_params=_params(("parallel", "parallel")),
    )(h, target)
    return dh, jnp.sum(part[::8, 0])


def _unembed(dh, dm):
    dc = _col_tile(dm.D, 256)
    s, lp = dm.S, dm.Lp

    def body(dh_ref, gx_ref, gm_ref):
        gx_ref[0] = dh_ref[N_META:N_META + s, :]

        @pl.when(pl.program_id(1) == 0)
        def _():
            gm_ref[...] = dh_ref[0:N_META, :]

        @pl.when(pl.program_id(1) > 0)
        def _():
            gm_ref[...] = gm_ref[...] + dh_ref[0:N_META, :]

    return pl.pallas_call(
        body, name="unembed", grid=(dm.D // dc, dm.BL),
        in_specs=[pl.BlockSpec((lp, dc), lambda j, b: (b, j))],
        out_specs=[pl.BlockSpec((1, s, dc), lambda j, b: (b, 0, j)),
                   pl.BlockSpec((N_META, dc), lambda j, b: (0, j))],
        out_shape=[jax.ShapeDtypeStruct((dm.BL, s, dm.D), F32),
                   jax.ShapeDtypeStruct((N_META, dm.D), F32)],
        compiler_params=_params(("parallel", "arbitrary")),
    )(dh)


def _fwd_in(h, pre_g, w, dm, rider=None):
    tm = _row_tile(dm.R, 1088)
    tn = _col_tile(dm.NP, 5 * MXU_DIM)
    nj = dm.NP // tn

    def body(h_ref, g_ref, w_ref, wdt_ref, proj_ref, dt_ref, hn_ref):
        @pl.when(pl.program_id(1) == 0)
        def _():
            xf = h_ref[...]
            r = lax.rsqrt(jnp.mean(xf * xf, axis=-1, keepdims=True) + NORM_EPS)
            hn_ref[...] = (xf * r * g_ref[...]).astype(BF16)
            dt_ref[...] = _dot(hn_ref[...], wdt_ref[...])

        proj_ref[...] = _dot(hn_ref[...], w_ref[...]).astype(BF16)

    return _call(
        body, "fwd_in", (dm.R // tm, nj),
        [pl.BlockSpec((tm, dm.D), lambda i, j: (i, 0)),
         pl.BlockSpec((1, dm.D), lambda i, j: (0, 0)),
         pl.BlockSpec((dm.D, tn), lambda i, j: (0, j)),
         pl.BlockSpec((dm.D, DT_PAD), lambda i, j: (0, dm.DT0 // DT_PAD))],
        [pl.BlockSpec((tm, tn), lambda i, j: (i, j)),
         pl.BlockSpec((tm, DT_PAD), lambda i, j: (i, 0)),
         pl.BlockSpec((tm, dm.D), lambda i, j: (i, 0))],
        [jax.ShapeDtypeStruct((dm.R, dm.NP), BF16), jax.ShapeDtypeStruct((dm.R, DT_PAD), F32),
         jax.ShapeDtypeStruct((dm.R, dm.D), BF16)],
        [], ("parallel", "arbitrary"), (h, pre_g, w, w), rider)


def _fwd_out(ya, yb, yc, w_out, h, post_g, dm, rider=None):
    tm = _row_tile(dm.Lp, 1088)
    tiles_per_seq = dm.Lp // tm
    da, db, dc = dm.DA, dm.DB, dm.DC

    def body(ya_ref, yb_ref, yc_ref, w_ref, h_ref, g_ref, hn_ref, m_ref):
        m = _dot(ya_ref[...], w_ref[0:da, :])
        m = m + _dot(yb_ref[...], w_ref[da:da + db, :])
        m = m + _dot(yc_ref[...], w_ref[da + db:da + db + dc, :])
        m_ref[...] = m
        r = lax.rsqrt(jnp.mean(m * m, axis=-1, keepdims=True) + NORM_EPS)
        t = (pl.program_id(0) % tiles_per_seq) * tm + lax.broadcasted_iota(jnp.int32, (tm, 1), 0)
        keep = (t < dm.L).astype(F32)
        hn_ref[...] = (h_ref[...] + m * r * g_ref[...]) * keep

    row = lambda i: (i, 0)
    fixed = lambda i: (0, 0)
    return _call(
        body, "fwd_out", (dm.R // tm,),
        [pl.BlockSpec((tm, da), row), pl.BlockSpec((tm, db), row), pl.BlockSpec((tm, dc), row),
         pl.BlockSpec((2 * dm.D, dm.D), fixed), pl.BlockSpec((tm, dm.D), row), pl.BlockSpec((1, dm.D), fixed)],
        [pl.BlockSpec((tm, dm.D), row), pl.BlockSpec((tm, dm.D), row)],
        [jax.ShapeDtypeStruct((dm.R, dm.D), F32), jax.ShapeDtypeStruct((dm.R, dm.D), F32)],
        [], ("parallel",), (ya, yb, yc, w_out, h, post_g), rider)


def _bwd_out(dh, m, post_g, w_out, ya, yb, yc, dm, rider=None):
    tm = _row_tile(dm.R, MXU_DIM)
    da, db, dc = dm.DA, dm.DB, dm.DC

    def body(dh_ref, m_ref, g_ref, w_ref, ya_ref, yb_ref, yc_ref, dya_ref, dyb_ref, dyc_ref, dw_ref, dg_ref):
        @pl.when(pl.program_id(0) == 0)
        def _():
            dw_ref[...] = jnp.zeros_like(dw_ref)
            dg_ref[...] = jnp.zeros_like(dg_ref)

        m = m_ref[...]
        dh_ = dh_ref[...]
        r = lax.rsqrt(jnp.mean(m * m, axis=-1, keepdims=True) + NORM_EPS)
        n = m * r
        dg_ref[0:1, :] = dg_ref[0:1, :] + jnp.sum(dh_ * n, axis=0, keepdims=True)
        dn = dh_ * g_ref[...]
        dm_ = (r * (dn - n * jnp.mean(dn * n, axis=-1, keepdims=True))).astype(BF16)
        dya_ref[...] = _dot_nt(dm_, w_ref[0:da, :])
        dyb_ref[...] = _dot_nt(dm_, w_ref[da:da + db, :])
        dyc_ref[...] = _dot_nt(dm_, w_ref[da + db:da + db + dc, :])
        dw_ref[0:da, :] = dw_ref[0:da, :] + _dot_tn(ya_ref[...], dm_)
        dw_ref[da:da + db, :] = dw_ref[da:da + db, :] + _dot_tn(yb_ref[...], dm_)
        dw_ref[da + db:da + db + dc, :] = dw_ref[da + db:da + db + dc, :] + _dot_tn(yc_ref[...], dm_)

    row = lambda i: (i, 0)
    fixed = lambda i: (0, 0)
    return _call(
        body, "bwd_out", (dm.R // tm,),
        [pl.BlockSpec((tm, dm.D), row), pl.BlockSpec((tm, dm.D), row), pl.BlockSpec((1, dm.D), fixed),
         pl.BlockSpec((2 * dm.D, dm.D), fixed),
         pl.BlockSpec((tm, da), row), pl.BlockSpec((tm, db), row), pl.BlockSpec((tm, dc), row)],
        [pl.BlockSpec((tm, da), row), pl.BlockSpec((tm, db), row), pl.BlockSpec((tm, dc), row),
         pl.BlockSpec((2 * dm.D, dm.D), fixed), pl.BlockSpec((8, dm.D), fixed)],
        [jax.ShapeDtypeStruct((dm.R, da), F32), jax.ShapeDtypeStruct((dm.R, db), F32),
         jax.ShapeDtypeStruct((dm.R, dc), F32),
         jax.ShapeDtypeStruct((2 * dm.D, dm.D), F32), jax.ShapeDtypeStruct((8, dm.D), F32)],
        [], ("arbitrary",), (dh, m, post_g, w_out, ya, yb, yc), rider)


def _bwd_in_dx(dpa, dpb, dpc, dpt, w, h, dh, pre_g, dm, rider=None):
    tm = _row_tile(dm.R, 272)
    wa, wb, wc = dm.WA, dm.WB, dm.WC

    def body(dpa_ref, dpb_ref, dpc_ref, dpt_ref, w_ref, h_ref, dh_ref, g_ref, out_ref, dg_ref):
        @pl.when(pl.program_id(0) == 0)
        def _():
            dg_ref[...] = jnp.zeros_like(dg_ref)

        dhn = _dot_nt(dpa_ref[...], w_ref[:, 0:wa])
        dhn = dhn + _dot_nt(dpb_ref[...], w_ref[:, wa:wa + wb])
        dhn = dhn + _dot_nt(dpc_ref[...], w_ref[:, wa + wb:wa + wb + wc])
        dhn = dhn + _dot_nt(dpt_ref[...], w_ref[:, wa + wb + wc:wa + wb + wc + DT_PAD])
        xf = h_ref[...]
        r = lax.rsqrt(jnp.mean(xf * xf, axis=-1, keepdims=True) + NORM_EPS)
        n = xf * r
        dg_ref[0:1, :] = dg_ref[0:1, :] + jnp.sum(dhn * n, axis=0, keepdims=True)
        dn = dhn * g_ref[...]
        out_ref[...] = dh_ref[...] + r * (dn - n * jnp.mean(dn * n, axis=-1, keepdims=True))

    row = lambda i: (i, 0)
    fixed = lambda i: (0, 0)
    return _call(
        body, "bwd_in_dx", (dm.R // tm,),
        [pl.BlockSpec((tm, wa), row), pl.BlockSpec((tm, wb), row), pl.BlockSpec((tm, wc), row),
         pl.BlockSpec((tm, DT_PAD), row), pl.BlockSpec((dm.D, dm.NP), fixed),
         pl.BlockSpec((tm, dm.D), row), pl.BlockSpec((tm, dm.D), row), pl.BlockSpec((1, dm.D), fixed)],
        [pl.BlockSpec((tm, dm.D), row), pl.BlockSpec((8, dm.D), fixed)],
        [jax.ShapeDtypeStruct((dm.R, dm.D), F32), jax.ShapeDtypeStruct((8, dm.D), F32)],
        [], ("arbitrary",), (dpa, dpb, dpc, dpt, w, h, dh, pre_g), rider)


def _bwd_in_dw(hn, dps, dm):
    wide = sorted(dp.shape[1] for dp in dps)[-2]
    tn = [_col_tile(dp.shape[1], 2 * MXU_DIM if dp.shape[1] >= wide else MXU_DIM) for dp in dps]
    nb = [dp.shape[1] // t for dp, t in zip(dps, tn)]
    first = [sum(nb[:p]) for p in range(len(dps))]
    at = lambda p: (lambda j: (0, jnp.clip(j - first[p], 0, nb[p] - 1)))

    def body(hn_ref, *refs):
        j = pl.program_id(0)
        for p in range(len(dps)):
            @pl.when(jnp.logical_and(j >= first[p], j < first[p] + nb[p]))
            def _(p=p):
                refs[len(dps) + p][...] = _dot_tn(hn_ref[...], refs[p][...])

    return pl.pallas_call(
        body, name="bwd_in_dw", grid=(sum(nb),),
        in_specs=[pl.BlockSpec((dm.R, dm.D), lambda j: (0, 0), pipeline_mode=pl.Buffered(1))] + [
            pl.BlockSpec((dm.R, tn[p]), at(p)) for p in range(len(dps))],
        out_specs=[pl.BlockSpec((dm.D, tn[p]), at(p)) for p in range(len(dps))],
        out_shape=[jax.ShapeDtypeStruct((dm.D, dp.shape[1]), F32) for dp in dps],
        compiler_params=_params(("arbitrary",)),
    )(hn, *dps)


def _tile_index(dm, reverse):
    if reverse:
        return lambda b, i: b * dm.NT + (dm.NT - 1 - i)
    return lambda b, i: b * dm.NT + i


def _halo_index(dm, rows):
    per_tile = TT // rows
    return lambda b, i: jnp.maximum((b * dm.NT + (dm.NT - 1 - i)) * per_tile - 1, 0)


HALO_BLOCK = 16


def _last_rows(x):
    return x.astype(F32)[HALO_BLOCK - SMALL_HALO:HALO_BLOCK]


MIX_A_ROWS = 288


def _mix_a_fwd(proj, conv_w, dm):
    da = dm.DA
    ta = _row_tile(dm.Lp, MIX_A_ROWS)
    nta = dm.Lp // ta
    bl = dm.BL

    def setup(ab_ref, ac_ref, ax_ref, az_ref, w_ref, y_ref, pbuf):
        i = pl.program_id(0)

        @pl.when(i == 0)
        def _():
            pbuf[0:SMALL_HALO, :] = jnp.zeros((SMALL_HALO, da), F32)

        @pl.when(i > 0)
        def _():
            pbuf[0:SMALL_HALO, :] = pbuf[ta:ta + SMALL_HALO, :]

    def body(ab_ref, ac_ref, ax_ref, az_ref, w_ref, y_ref, pbuf):
        for lb in range(da // LANES):
            cs = slice(lb * LANES, (lb + 1) * LANES)
            p = ac_ref[:, cs].astype(F32) * ax_ref[:, cs].astype(F32)
            pbuf[SMALL_HALO:SMALL_HALO + ta, cs] = p
            q = (w_ref[0:1, cs] * pbuf[6:6 + ta, cs] + w_ref[1:2, cs] * pbuf[7:7 + ta, cs] + w_ref[2:3, cs] * p)
            az = az_ref[:, cs].astype(F32)
            y_ref[:, cs] = (ab_ref[:, cs].astype(F32) * q * (az * _sigmoid(az))).astype(BF16)
            yield

    proj3 = proj.reshape(bl, dm.Lp, dm.NP)
    col = lambda k: pl.BlockSpec((bl, ta, da), lambda i: (0, i, k))
    return pl.pallas_call(
        _per_sequence(setup, body, bl, [True] * 4 + [False] + [True, True]), name="mix_a_fwd", grid=(nta,),
        in_specs=[col(0), col(1), col(2), col(3), pl.BlockSpec((CONV_A_K, da), lambda i: (0, 0))],
        out_specs=col(0),
        out_shape=jax.ShapeDtypeStruct((bl, dm.Lp, da), BF16),
        scratch_shapes=[pltpu.VMEM((bl, SMALL_HALO + ta, da), F32)],
        compiler_params=_params(("arbitrary",)),
    )(proj3, proj3, proj3, proj3, conv_w).reshape(dm.R, da)


def _mix_a_bwd(proj, dya, conv_w, dm):
    da = dm.DA
    ta = _row_tile(dm.Lp, MIX_A_ROWS)
    nta = dm.Lp // ta
    bl = dm.BL

    def setup(ab_ref, ac_ref, ax_ref, az_ref, ach_ref, axh_ref, dy_ref, w_ref, dp_ref, dw_ref, pbuf, dqbuf):
        i = pl.program_id(0)

        @pl.when(i == 0)
        def _():
            dw_ref[...] = jnp.zeros_like(dw_ref)
            dqbuf[ta:ta + SMALL_HALO, :] = jnp.zeros((SMALL_HALO, da), F32)

        @pl.when(i > 0)
        def _():
            dqbuf[ta:ta + SMALL_HALO, :] = dqbuf[0:SMALL_HALO, :]

    def body(ab_ref, ac_ref, ax_ref, az_ref, ach_ref, axh_ref, dy_ref, w_ref, dp_ref, dw_ref, pbuf, dqbuf):
        halo_on = jnp.where(pl.program_id(0) == nta - 1, 0.0, 1.0)
        for lb in range(da // LANES):
            cs = slice(lb * LANES, (lb + 1) * LANES)
            pbuf[0:SMALL_HALO, cs] = (_last_rows(ach_ref[:, cs]) * _last_rows(axh_ref[:, cs])) * halo_on
            ac, ax, ab, az = (r[:, cs].astype(F32) for r in (ac_ref, ax_ref, ab_ref, az_ref))
            p = ac * ax
            pbuf[SMALL_HALO:SMALL_HALO + ta, cs] = p
            p1 = pbuf[7:7 + ta, cs]
            p2 = pbuf[6:6 + ta, cs]
            w0, w1, w2 = w_ref[0:1, cs], w_ref[1:2, cs], w_ref[2:3, cs]
            q = w0 * p2 + w1 * p1 + w2 * p
            sz, dsz = _silu_and_grad(az)
            dy = dy_ref[:, cs]
            t1 = dy * ab
            dq = t1 * sz
            dqbuf[0:ta, cs] = dq
            dpv = w2 * dq + w1 * dqbuf[1:1 + ta, cs] + w0 * dqbuf[2:2 + ta, cs]
            dp_ref[:, lb * LANES:(lb + 1) * LANES] = (dy * q * sz).astype(BF16)
            dp_ref[:, da + lb * LANES:da + (lb + 1) * LANES] = (dpv * ax).astype(BF16)
            dp_ref[:, 2 * da + lb * LANES:2 * da + (lb + 1) * LANES] = (dpv * ac).astype(BF16)
            dp_ref[:, 3 * da + lb * LANES:3 * da + (lb + 1) * LANES] = (t1 * q * dsz).astype(BF16)
            dw_ref[0, 0:1, cs] = dw_ref[0, 0:1, cs] + jnp.sum(dq * p2, axis=0, keepdims=True)
            dw_ref[0, 1:2, cs] = dw_ref[0, 1:2, cs] + jnp.sum(dq * p1, axis=0, keepdims=True)
            dw_ref[0, 2:3, cs] = dw_ref[0, 2:3, cs] + jnp.sum(dq * p, axis=0, keepdims=True)
            yield

    proj3 = proj.reshape(bl, dm.Lp, dm.NP)
    col = lambda w, k: pl.BlockSpec((bl, ta, w), lambda i: (0, nta - 1 - i, k))
    halo = lambda k: pl.BlockSpec((bl, HALO_BLOCK, da),
                                  lambda i: (0, jnp.maximum((nta - 1 - i) * (ta // HALO_BLOCK) - 1, 0), k))
    dp, dw = pl.pallas_call(
        _per_sequence(setup, body, bl, [True] * 7 + [False] + [True, "keep"] + [True, True]),
        name="mix_a_bwd", grid=(nta,),
        in_specs=[col(da, 0), col(da, 1), col(da, 2), col(da, 3), halo(1), halo(2), col(da, 0),
                  pl.BlockSpec((CONV_A_K, da), lambda i: (0, 0))],
        out_specs=[col(dm.WA, 0), pl.BlockSpec((bl, 8, da), lambda i: (0, 0, 0))],
        out_shape=[jax.ShapeDtypeStruct((bl, dm.Lp, dm.WA), BF16), jax.ShapeDtypeStruct((bl, 8, da), F32)],
        scratch_shapes=[pltpu.VMEM((bl, SMALL_HALO + ta, da), F32), pltpu.VMEM((bl, ta + SMALL_HALO, da), F32)],
        compiler_params=_params(("arbitrary",)),
    )(proj3, proj3, proj3, proj3, proj3, proj3, dya.reshape(bl, dm.Lp, da), conv_w)
    return dp.reshape(dm.R, dm.WA), dw


SUBLANES = 8
SHIFT_ROWS = TT + CONF_HALO - SUBLANES


TAP_ROWS = 64


def _split_lanes(buf, rows, val):
    for lb in range(val.shape[1] // LANES):
        buf[lb, rows, :] = val[:, lb * LANES:(lb + 1) * LANES]


def _join_lanes(buf):
    return jnp.concatenate([buf[lb] for lb in range(buf.shape[0])], axis=1)


def _fill_shifted(buf, shifted):
    def step(lb, carry):
        for r in range(1, SUBLANES):
            shifted[lb, r - 1, 0:SHIFT_ROWS, :] = buf[lb, r:r + SHIFT_ROWS, :]
        return carry

    lax.fori_loop(0, buf.shape[0], step, 0)


def _window(buf, shifted, d, r0, lb):
    r = d % SUBLANES
    rows = pl.ds(pl.multiple_of(r0 + (d - r), SUBLANES), TAP_ROWS)
    return buf[lb, rows, :] if r == 0 else shifted[lb, r - 1, rows, :]


def _tap_loop(nlb, body):
    per_lb = TT // TAP_ROWS

    def step(it, carry):
        lb = it // per_lb
        body(lb, pl.ds(pl.multiple_of(lb * LANES, LANES), LANES), pl.multiple_of((it % per_lb) * TAP_ROWS, TAP_ROWS))
        return carry

    lax.fori_loop(0, nlb * per_lb, step, 0)


TAP_CHAINS = 4


def _tree_sum(terms):
    sums = list(terms[:TAP_CHAINS])
    for n, t in enumerate(terms[TAP_CHAINS:]):
        sums[n % TAP_CHAINS] = sums[n % TAP_CHAINS] + t
    while len(sums) > 1:
        sums = [a + b for a, b in zip(sums[0::2], sums[1::2])] + ([sums[-1]] if len(sums) % 2 else [])
    return sums[0]


def _conf_conv(ubuf, ushift, w_ref, b_ref, u1buf):
    _fill_shifted(ubuf, ushift)

    def piece(lb, cs, r0):
        taps = [w_ref[k:k + 1, cs] * _window(ubuf, ushift, CONF_HALO - (CONF_K - 1) + k, r0, lb)
                for k in range(CONF_K)]
        u1buf[lb, pl.ds(r0, TAP_ROWS), :] = _tree_sum(taps) + b_ref[0:1, cs]

    _tap_loop(ubuf.shape[0], piece)


def _mix_c_fwd(proj, conv_w, conv_b, ln_g, ln_b, dm):
    dc = dm.DC
    nlb = dc // LANES
    c0 = (dm.WA + dm.WB) // dc
    ti = _tile_index(dm, False)

    def body(ca_ref, cg_ref, cz_ref, w_ref, b_ref, g_ref, be_ref, y_ref, u1_ref, ubuf, u1buf, ushift):
        i = pl.program_id(1)

        @pl.when(i == 0)
        def _():
            ubuf[:, 0:CONF_HALO, :] = jnp.zeros((nlb, CONF_HALO, LANES), F32)

        @pl.when(i > 0)
        def _():
            ubuf[:, 0:CONF_HALO, :] = ubuf[:, TT:TT + CONF_HALO, :]

        _split_lanes(ubuf, slice(CONF_HALO, CONF_HALO + TT),
                     ca_ref[...].astype(F32) * _sigmoid(cg_ref[...].astype(F32)))
        _conf_conv(ubuf, ushift, w_ref, b_ref, u1buf)
        u1 = _join_lanes(u1buf)
        u1_ref[...] = u1
        mu = jnp.mean(u1, axis=-1, keepdims=True)
        xc = u1 - mu
        rstd = lax.rsqrt(jnp.mean(xc * xc, axis=-1, keepdims=True) + LN_EPS)
        u2 = xc * rstd * g_ref[...] + be_ref[...]
        cz = cz_ref[...].astype(F32)
        y_ref[...] = ((u2 * _sigmoid(u2)) * (cz * _sigmoid(cz))).astype(BF16)

    col = lambda k: pl.BlockSpec((TT, dc), lambda b, i: (ti(b, i), c0 + k))
    vec = pl.BlockSpec((1, dc), lambda b, i: (0, 0))
    return pl.pallas_call(
        body, name="mix_c_fwd", grid=(dm.BL, dm.NT),
        in_specs=[col(0), col(1), col(2), pl.BlockSpec((CONF_K, dc), lambda b, i: (0, 0)), vec, vec, vec],
        out_specs=[pl.BlockSpec((TT, dc), lambda b, i: (ti(b, i), 0))] * 2,
        out_shape=[jax.ShapeDtypeStruct((dm.R, dc), BF16), jax.ShapeDtypeStruct((dm.R, dc), F32)],
        scratch_shapes=[pltpu.VMEM((nlb, CONF_HALO + TT, LANES), F32), pltpu.VMEM((nlb, TT, LANES), F32),
                        pltpu.VMEM((nlb, SUBLANES - 1, SHIFT_ROWS, LANES), F32)],
        compiler_params=_params(("parallel", "arbitrary")),
    )(proj, proj, proj, conv_w, conv_b, ln_g, ln_b)


def _mix_c_bwd(proj, u1, dyc, conv_w, ln_g, ln_b, dm):
    dc = dm.DC
    nlb = dc // LANES
    c0 = (dm.WA + dm.WB) // dc
    ti = _tile_index(dm, True)
    hi = _halo_index(dm, CONF_HALO)

    def body(ca_ref, cg_ref, cz_ref, cah_ref, cgh_ref, u1_ref, dy_ref, w_ref, g_ref, be_ref,
             dp_ref, dw_ref, dv_ref, ubuf, dubuf, du0buf, ushift, dshift, dwacc):
        i = pl.program_id(1)
        halo_on = jnp.where(i == dm.NT - 1, 0.0, 1.0)

        @pl.when(i == 0)
        def _():
            dwacc[...] = jnp.zeros_like(dwacc)
            dv_ref[...] = jnp.zeros_like(dv_ref)
            dubuf[:, TT:TT + CONF_HALO, :] = jnp.zeros((nlb, CONF_HALO, LANES), F32)

        @pl.when(i > 0)
        def _():
            dubuf[:, TT:TT + CONF_HALO, :] = dubuf[:, 0:CONF_HALO, :]

        _split_lanes(ubuf, slice(0, CONF_HALO),
                     cah_ref[...].astype(F32) * _sigmoid(cgh_ref[...].astype(F32)) * halo_on)
        sgg = _sigmoid(cg_ref[...].astype(F32))
        ca = ca_ref[...].astype(F32)
        _split_lanes(ubuf, slice(CONF_HALO, CONF_HALO + TT), ca * sgg)
        _fill_shifted(ubuf, ushift)
        u1 = u1_ref[...]
        mu = jnp.mean(u1, axis=-1, keepdims=True)
        xc = u1 - mu
        rstd = lax.rsqrt(jnp.mean(xc * xc, axis=-1, keepdims=True) + LN_EPS)
        xhat = xc * rstd
        u2 = xhat * g_ref[...] + be_ref[...]
        su, dsu = _silu_and_grad(u2)
        sz, dsz = _silu_and_grad(cz_ref[...].astype(F32))
        dy = dy_ref[...]
        du2 = dy * dsu * sz
        dp_ref[:, 2 * dc:3 * dc] = (dy * su * dsz).astype(BF16)
        dxhat = du2 * g_ref[...]
        du1 = rstd * (dxhat - jnp.mean(dxhat, axis=-1, keepdims=True)
                      - xhat * jnp.mean(dxhat * xhat, axis=-1, keepdims=True))
        dv_ref[0, 0:1, :] = dv_ref[0, 0:1, :] + jnp.sum(du1, axis=0, keepdims=True)
        dv_ref[0, 1:2, :] = dv_ref[0, 1:2, :] + jnp.sum(du2 * xhat, axis=0, keepdims=True)
        dv_ref[0, 2:3, :] = dv_ref[0, 2:3, :] + jnp.sum(du2, axis=0, keepdims=True)
        _split_lanes(dubuf, slice(0, TT), du1)
        _fill_shifted(dubuf, dshift)

        def piece(lb, cs, r0):
            du0buf[lb, pl.ds(r0, TAP_ROWS), :] = _tree_sum(
                [w_ref[k:k + 1, cs] * _window(dubuf, dshift, CONF_K - 1 - k, r0, lb) for k in range(CONF_K)])
            d1 = dubuf[lb, pl.ds(r0, TAP_ROWS), :]
            for k in range(CONF_K):
                prod = d1 * _window(ubuf, ushift, CONF_HALO - (CONF_K - 1) + k, r0, lb)
                dwacc[lb, k] = dwacc[lb, k] + jnp.sum(prod.reshape(TAP_ROWS // SUBLANES, SUBLANES, LANES), axis=0)

        _tap_loop(nlb, piece)
        du0 = _join_lanes(du0buf)
        dp_ref[:, 0:dc] = (du0 * sgg).astype(BF16)
        dp_ref[:, dc:2 * dc] = (du0 * ca * sgg * (1.0 - sgg)).astype(BF16)

        @pl.when(i == dm.NT - 1)
        def _():
            for lb in range(nlb):
                dw_ref[0, 0:CONF_K, lb * LANES:(lb + 1) * LANES] = jnp.sum(dwacc[lb], axis=1)
            dw_ref[0, CONF_K:CONF_K + 1, :] = jnp.zeros((1, dc), F32)

    col = lambda k: pl.BlockSpec((TT, dc), lambda b, i: (ti(b, i), c0 + k))
    halo = lambda k: pl.BlockSpec((CONF_HALO, dc), lambda b, i: (hi(b, i), c0 + k))
    vec = pl.BlockSpec((1, dc), lambda b, i: (0, 0))
    return pl.pallas_call(
        body, name="mix_c_bwd", grid=(dm.BL, dm.NT),
        in_specs=[col(0), col(1), col(2), halo(0), halo(1),
                  pl.BlockSpec((TT, dc), lambda b, i: (ti(b, i), 0)),
                  pl.BlockSpec((TT, dc), lambda b, i: (ti(b, i), 0)),
                  pl.BlockSpec((CONF_K, dc), lambda b, i: (0, 0)), vec, vec],
        out_specs=[pl.BlockSpec((TT, dm.WC), lambda b, i: (ti(b, i), 0)),
                   pl.BlockSpec((1, 32, dc), lambda b, i: (b, 0, 0)),
                   pl.BlockSpec((1, 8, dc), lambda b, i: (b, 0, 0))],
        out_shape=[jax.ShapeDtypeStruct((dm.R, dm.WC), BF16),
                   jax.ShapeDtypeStruct((dm.BL, 32, dc), F32),
                   jax.ShapeDtypeStruct((dm.BL, 8, dc), F32)],
        scratch_shapes=[pltpu.VMEM((nlb, CONF_HALO + TT, LANES), F32),
                        pltpu.VMEM((nlb, TT + CONF_HALO, LANES), F32), pltpu.VMEM((nlb, TT, LANES), F32),
                        pltpu.VMEM((nlb, SUBLANES - 1, SHIFT_ROWS, LANES), F32),
                        pltpu.VMEM((nlb, SUBLANES - 1, SHIFT_ROWS, LANES), F32),
                        pltpu.VMEM((nlb, CONF_K, SUBLANES, LANES), F32)],
        compiler_params=_params(("parallel", "arbitrary")),
    )(proj, proj, proj, proj, proj, u1, dyc, conv_w, ln_g, ln_b)


RAW_HALO = 16


def _shift_matrix():
    r = jnp.arange((SSM_CONV_K - 1) * TT)[:, None]
    want = TT + r % TT - (SSM_CONV_K - 1 - r // TT)
    return (jnp.arange(2 * TT)[None, :] == want).astype(BF16)


def _ssm_conv(rawwin, sh, s_ref, w_ref, b_ref, width):
    sh[...] = _dot(s_ref[...], rawwin[...])
    for lb in range(width // LANES):
        cs = slice(lb * LANES, (lb + 1) * LANES)
        acc = b_ref[0:1, cs] + w_ref[SSM_CONV_K - 1:SSM_CONV_K, cs] * rawwin[TT:2 * TT, cs].astype(F32)
        for k in range(SSM_CONV_K - 1):
            acc = acc + w_ref[k:k + 1, cs] * sh[k * TT:(k + 1) * TT, cs]
        yield cs, acc


def _softplus(z):
    return jnp.maximum(z, 0.0) + jnp.log(1.0 + jnp.exp(-jnp.abs(z)))


def _tri(lower):
    r = lax.broadcasted_iota(jnp.int32, (TT, TT), 0)
    c = lax.broadcasted_iota(jnp.int32, (TT, TT), 1)
    return (c <= r) if lower else (c >= r)


def _exact_01_dot(mat01, x):
    x1, x2, x3 = _split3(x)
    return _dot(mat01, x1) + _dot(mat01, x2) + _dot(mat01, x3)


def _head_scalars(dt_ref, dtb_ref, alog_ref):
    z = dt_ref[...] + dtb_ref[...]
    dtv = _softplus(z)
    a = -jnp.exp(alog_ref[...])
    ac = _exact_01_dot(_tri(True).astype(F32).astype(BF16), dtv * a)
    eac = jnp.exp(ac)
    dst = jnp.exp(ac[TT - 1:TT, :] - ac)
    return z, dtv, a, ac, eac, dst


FAR_BELOW = -1e30


def _decay(ac, ac_t, h, causal):
    return jnp.exp(jnp.where(causal, ac[:, h:h + 1] - ac_t[h:h + 1, :], FAR_BELOW))


def _own_half(x16, h):
    lane = lax.broadcasted_iota(jnp.int32, (1, LANES), 1)
    keep = (lane >= SSM_HEAD_DIM) if (h % 2) else (lane < SSM_HEAD_DIM)
    return jnp.where(keep, x16, jnp.zeros_like(x16))


def _per_sequence(setup, body, bl, how):
    def all_sequences(*refs):
        views = [[r.at[b] if h is True else (r.at[pl.ds(b, 1)] if h == "keep" else r) for r, h in zip(refs, how)]
                 for b in range(bl)]
        for v in views:
            setup(*v)
        running = [body(*v) for v in views]
        while running:
            running = [g for g in running if next(g, "done") != "done"]

    return all_sequences


def _mix_b_fwd(proj, projdt, conv_w, conv_b, dt_bias, a_log, dskx, norm_g, expand, dm, rider=None):
    db, gn, xbc_w, hpg = dm.DB, dm.GN, dm.XBC, dm.HPG
    gw = db // SSM_GROUPS

    def setup(bz_ref, bx_ref, bc_ref, dt_ref, w_ref, b_ref, dtb_ref, alog_ref, dsk_ref, g_ref, e_ref, s_ref,
              y_ref, yraw_ref, sprev_ref, rawwin, sh, xbuf, state, ybuf, exbuf, xdtbuf):
        i = pl.program_id(0)

        @pl.when(i == 0)
        def _():
            rawwin[0:TT, :] = jnp.zeros((TT, xbc_w), BF16)
            state[...] = jnp.zeros_like(state)

        @pl.when(i > 0)
        def _():
            rawwin[TT - RAW_HALO:TT, :] = rawwin[2 * TT - RAW_HALO:2 * TT, :]

    def body(bz_ref, bx_ref, bc_ref, dt_ref, w_ref, b_ref, dtb_ref, alog_ref, dsk_ref, g_ref, e_ref, s_ref,
             y_ref, yraw_ref, sprev_ref, rawwin, sh, xbuf, state, ybuf, exbuf, xdtbuf):
        rawwin[TT:2 * TT, 0:db] = bx_ref[...]
        rawwin[TT:2 * TT, db:xbc_w] = bc_ref[...]
        for cs, pre in _ssm_conv(rawwin, sh, s_ref, w_ref, b_ref, xbc_w):
            xbuf[:, cs] = pre * _sigmoid(pre)
            yield

        _, dtv, _, ac, eac, dst = _head_scalars(dt_ref, dtb_ref, alog_ref)
        exbuf[...] = _dot(jnp.concatenate([dtv, eac, dst], axis=0).astype(BF16), e_ref[...])
        ac_t = ac.T
        causal = _tri(True)
        sprev_ref[0, 0] = state[...]
        yield

        xdtbuf[...] = xbuf[:, 0:db] * exbuf[0:TT, :]
        ybuf[...] = xbuf[:, 0:db] * dsk_ref[...]
        for g in range(SSM_GROUPS):
            gs = slice(g * gw, (g + 1) * gw)
            bg = xbuf[:, db + g * SSM_STATE:db + (g + 1) * SSM_STATE].astype(BF16)
            cg = xbuf[:, db + gn + g * SSM_STATE:db + gn + (g + 1) * SSM_STATE].astype(BF16)
            cb = _dot_nt(cg, bg)
            for e in range(0, hpg, 2):
                h = g * hpg + e
                ps = slice(h * SSM_HEAD_DIM, (h + 2) * SSM_HEAD_DIM)
                xp16 = xdtbuf[:, ps].astype(BF16)
                acc = jnp.zeros((TT, LANES), F32)
                for hh in (h, h + 1):
                    mm = (cb * _decay(ac, ac_t, hh, causal)).astype(BF16)
                    acc = acc + _dot(mm, _own_half(xp16, hh))
                ybuf[:, ps] = ybuf[:, ps] + acc
                yield
            sg = state[:, gs]
            ybuf[:, gs] = ybuf[:, gs] + exbuf[TT:2 * TT, gs] * _dot(cg, sg.astype(BF16))
            state[:, gs] = sg * exbuf[2 * TT - 1:2 * TT, gs] + _dot_tn(
                bg, (xdtbuf[:, gs] * exbuf[2 * TT:3 * TT, gs]).astype(BF16))
            yield

        yraw = ybuf[...]
        yraw_ref[...] = yraw
        bz = bz_ref[...].astype(F32)
        v = yraw * (bz * _sigmoid(bz))
        r = lax.rsqrt(jnp.mean(v * v, axis=-1, keepdims=True) + NORM_EPS)
        y_ref[...] = (v * r * g_ref[...]).astype(BF16)

    bl = dm.BL
    tile = lambda w, k: pl.BlockSpec((bl, TT, w), lambda i: (0, i, k))
    fixed = lambda r, w: pl.BlockSpec((r, w), lambda i: (0, 0))
    proj3, dt3 = proj.reshape(bl, dm.Lp, dm.NP), projdt.reshape(bl, dm.Lp, DT_PAD)
    scratch = [((2 * TT, xbc_w), BF16), (((SSM_CONV_K - 1) * TT, xbc_w), F32), ((TT, xbc_w), F32),
               ((SSM_STATE, db), F32), ((TT, db), F32), ((3 * TT, db), F32), ((TT, db), F32)]
    (y, yraw, sprev), rode = _call(
        _per_sequence(setup, body, bl, [True] * 4 + [False] * 8 + [True, True, "keep"] + [True] * len(scratch)),
        "mix_b_fwd", (dm.NT,),
        [tile(db, dm.WA // db), tile(db, dm.WA // db + 1), tile(2 * gn, (dm.WA + 2 * db) // (2 * gn)),
         tile(DT_PAD, 0),
         fixed(SSM_CONV_K, xbc_w), fixed(1, xbc_w), fixed(1, DT_PAD), fixed(1, DT_PAD),
         fixed(1, db), fixed(1, db), fixed(DT_PAD, db), fixed((SSM_CONV_K - 1) * TT, 2 * TT)],
        [tile(db, 0), tile(db, 0), pl.BlockSpec((bl, 1, SSM_STATE, db), lambda i: (0, i, 0, 0))],
        [jax.ShapeDtypeStruct((bl, dm.Lp, db), BF16), jax.ShapeDtypeStruct((bl, dm.Lp, db), F32),
         jax.ShapeDtypeStruct((bl, dm.NT, SSM_STATE, db), F32)],
        [pltpu.VMEM((bl,) + s, t) for s, t in scratch],
        ("arbitrary",),
        (proj3, proj3, proj3, dt3, conv_w, conv_b, dt_bias, a_log, dskx, norm_g, expand, _shift_matrix()), rider)
    return (y.reshape(dm.R, db), yraw.reshape(dm.R, db), sprev), rode


def _mix_b_bwd(proj, projdt, dyb, yraw, sprev, conv_w, conv_b, dt_bias, a_log, dskx, norm_g, expand, expand_t, dm,
               rider=None):
    db, gn, xbc_w, hpg = dm.DB, dm.GN, dm.XBC, dm.HPG
    gw = db // SSM_GROUPS

    def setup(bz_ref, bx_ref, bc_ref, dt_ref, bxh_ref, bch_ref, dy_ref, yraw_ref, sprev_ref,
              w_ref, b_ref, dtb_ref, alog_ref, dsk_ref, g_ref, e_ref, et_ref, s_ref,
              dp_ref, dpt_ref, dwc_ref, dch_ref, dhd_ref,
              rawwin, sh, xbuf, dsbuf, dstate, dxbuf, z1buf, dprebuf, exbuf, xdtbuf, dyrbuf, uvec):
        i = pl.program_id(0)

        @pl.when(i == 0)
        def _():
            dwc_ref[...] = jnp.zeros_like(dwc_ref)
            dch_ref[...] = jnp.zeros_like(dch_ref)
            dhd_ref[...] = jnp.zeros_like(dhd_ref)
            dstate[...] = jnp.zeros_like(dstate)
            dprebuf[TT:TT + SMALL_HALO, :] = jnp.zeros((SMALL_HALO, xbc_w), F32)
            rawwin[0:TT - RAW_HALO, :] = jnp.zeros((TT - RAW_HALO, xbc_w), BF16)

        @pl.when(i > 0)
        def _():
            dprebuf[TT:TT + SMALL_HALO, :] = dprebuf[0:SMALL_HALO, :]

    def body(bz_ref, bx_ref, bc_ref, dt_ref, bxh_ref, bch_ref, dy_ref, yraw_ref, sprev_ref,
             w_ref, b_ref, dtb_ref, alog_ref, dsk_ref, g_ref, e_ref, et_ref, s_ref,
             dp_ref, dpt_ref, dwc_ref, dch_ref, dhd_ref,
             rawwin, sh, xbuf, dsbuf, dstate, dxbuf, z1buf, dprebuf, exbuf, xdtbuf, dyrbuf, uvec):
        halo_on = jnp.where(pl.program_id(0) == dm.NT - 1, 0.0, 1.0).astype(BF16)

        rawwin[TT - RAW_HALO:TT, 0:db] = bxh_ref[...] * halo_on
        rawwin[TT - RAW_HALO:TT, db:xbc_w] = bch_ref[...] * halo_on
        rawwin[TT:2 * TT, 0:db] = bx_ref[...]
        rawwin[TT:2 * TT, db:xbc_w] = bc_ref[...]
        for cs, pre in _ssm_conv(rawwin, sh, s_ref, w_ref, b_ref, xbc_w):
            sl, dsl = _silu_and_grad(pre)
            xbuf[:, cs] = sl
            dsbuf[:, cs] = dsl
            yield

        z, dtv, a, ac, eac, dst = _head_scalars(dt_ref, dtb_ref, alog_ref)
        exbuf[...] = _dot(jnp.concatenate([dtv, eac, dst], axis=0).astype(BF16), e_ref[...])
        ac_t = ac.T
        causal = _tri(True)
        xdtbuf[...] = xbuf[:, 0:db] * exbuf[0:TT, :]

        yraw = yraw_ref[...]
        sz, dsz = _silu_and_grad(bz_ref[...].astype(F32))
        v = yraw * sz
        r = lax.rsqrt(jnp.mean(v * v, axis=-1, keepdims=True) + NORM_EPS)
        dy = dy_ref[...]
        dyg = dy * g_ref[...]
        dv = r * dyg - v * (r * r * r * jnp.mean(dyg * v, axis=-1, keepdims=True))
        dch_ref[0, 0:1, :] = dch_ref[0, 0:1, :] + jnp.sum(dy * v * r, axis=0, keepdims=True)
        dyr = dv * sz
        dyrbuf[...] = dyr
        dp_ref[:, 0:db] = (dv * yraw * dsz).astype(BF16)
        dch_ref[0, 1:2, :] = dch_ref[0, 1:2, :] + jnp.sum(dyr * xbuf[:, 0:db], axis=0, keepdims=True)

        lane_row = lax.broadcasted_iota(jnp.int32, (1, LANES), 1)
        sub_col = lax.broadcasted_iota(jnp.int32, (LANES, 1), 0)
        dac = jnp.zeros((TT, LANES), F32)
        colacc = jnp.zeros((LANES, TT), F32)
        for g in range(SSM_GROUPS):
            gs = slice(g * gw, (g + 1) * gw)
            bs_ = slice(db + g * SSM_STATE, db + (g + 1) * SSM_STATE)
            cs_ = slice(db + gn + g * SSM_STATE, db + gn + (g + 1) * SSM_STATE)
            bg = xbuf[:, bs_].astype(BF16)
            cg = xbuf[:, cs_].astype(BF16)
            cb = _dot_nt(cg, bg)
            dcb = jnp.zeros((TT, TT), F32)
            for e in range(0, hpg, 2):
                h = g * hpg + e
                ps = slice(h * SSM_HEAD_DIM, (h + 2) * SSM_HEAD_DIM)
                xp16 = xdtbuf[:, ps].astype(BF16)
                dyp16 = dyrbuf[:, ps].astype(BF16)
                acc = jnp.zeros((TT, LANES), F32)
                for hh in (h, h + 1):
                    dec = _decay(ac, ac_t, hh, causal)
                    mm = cb * dec
                    dyh = _own_half(dyp16, hh)
                    dmm = _dot_nt(dyh, xp16)
                    acc = acc + _dot_tn(mm.astype(BF16), dyh)
                    dcb = dcb + dmm * dec
                    gm = dmm * mm
                    dac = jnp.where(lane_row == hh, jnp.sum(gm, axis=1, keepdims=True), dac)
                    colacc = jnp.where(sub_col == hh, jnp.sum(gm, axis=0, keepdims=True), colacc)
                dxbuf[:, ps] = acc
                yield
            sg32 = sprev_ref[0, 0, :, gs]
            sg = sg32.astype(BF16)
            dsn = dstate[:, gs]
            dsn16 = dsn.astype(BF16)
            dcb16 = dcb.astype(BF16)
            eacx = exbuf[TT:2 * TT, gs]
            dstx = exbuf[2 * TT:3 * TT, gs]
            cdx = exbuf[2 * TT - 1:2 * TT, gs]
            dye16 = (dyrbuf[:, gs] * eacx).astype(BF16)
            xdt_g = xdtbuf[:, gs]
            dxbuf[:, cs_] = _dot(dcb16, bg) + _dot_nt(dye16, sg)
            dst_x = dstx * _dot(bg, dsn16)
            dxbuf[:, bs_] = _dot_tn(dcb16, cg) + _dot_nt((dstx * xdt_g).astype(BF16), dsn16)
            dstate[:, gs] = cdx * dsn + _dot_tn(cg, dye16)
            z1buf[:, gs] = dyrbuf[:, gs] * (eacx * _dot(cg, sg)) - xdt_g * dst_x
            uvec[:, gs] = jnp.broadcast_to(
                jnp.sum(xdt_g * dst_x, axis=0, keepdims=True) + jnp.sum(dsn * cdx * sg32, axis=0, keepdims=True),
                (8, gw))
            dxbuf[:, gs] = dxbuf[:, gs] + dst_x
            yield

        zz = _dot(jnp.concatenate([z1buf[...], dxbuf[:, 0:db] * xbuf[:, 0:db]], axis=0).astype(BF16), et_ref[...])
        u1, u2, u3 = _split3(uvec[...])
        ulast = (_dot(u1, et_ref[...]) + _dot(u2, et_ref[...]) + _dot(u3, et_ref[...]))[0:1, :]
        is_last = (lax.broadcasted_iota(jnp.int32, (TT, 1), 0) == TT - 1).astype(F32)
        dac = dac - colacc.T + zz[0:TT] + is_last * ulast
        dda = _exact_01_dot(_tri(False).astype(F32).astype(BF16), dac)
        ddt = dda * a + zz[TT:2 * TT]
        dhd_ref[0, 1:2, :] = dhd_ref[0, 1:2, :] + jnp.sum(dda * dtv, axis=0, keepdims=True) * a
        ddtraw = ddt * _sigmoid(z)
        dhd_ref[0, 0:1, :] = dhd_ref[0, 0:1, :] + jnp.sum(ddtraw, axis=0, keepdims=True)
        dpt_ref[...] = ddtraw.astype(BF16)
        dxbuf[:, 0:db] = dyrbuf[...] * dsk_ref[...] + dxbuf[:, 0:db] * exbuf[0:TT, :]

        for lb in range(xbc_w // LANES):
            cs = slice(lb * LANES, (lb + 1) * LANES)
            dpre = dxbuf[:, cs] * dsbuf[:, cs]
            dprebuf[0:TT, cs] = dpre
            dwc_ref[0, SSM_CONV_K:SSM_CONV_K + 1, cs] = dwc_ref[0, SSM_CONV_K:SSM_CONV_K + 1, cs] + jnp.sum(
                dpre, axis=0, keepdims=True)
            draw = w_ref[SSM_CONV_K - 1:SSM_CONV_K, cs] * dpre
            for k in range(SSM_CONV_K - 1):
                ahead = SSM_CONV_K - 1 - k
                draw = draw + w_ref[k:k + 1, cs] * dprebuf[ahead:ahead + TT, cs]
            for k in range(SSM_CONV_K):
                moved = sh[k * TT:(k + 1) * TT, cs] if k < SSM_CONV_K - 1 else rawwin[TT:2 * TT, cs].astype(F32)
                dwc_ref[0, k:k + 1, cs] = dwc_ref[0, k:k + 1, cs] + jnp.sum(dpre * moved, axis=0, keepdims=True)
            dp_ref[:, db + lb * LANES:db + (lb + 1) * LANES] = draw.astype(BF16)
            yield

    bl, nt = dm.BL, dm.NT
    tile = lambda w, k: pl.BlockSpec((bl, TT, w), lambda i: (0, nt - 1 - i, k))
    halo = lambda w, k: pl.BlockSpec((bl, HALO_BLOCK, w),
                                     lambda i: (0, jnp.maximum((nt - 1 - i) * (TT // HALO_BLOCK) - 1, 0), k))
    fixed = lambda r, w: pl.BlockSpec((r, w), lambda i: (0, 0))
    sums = lambda w: pl.BlockSpec((bl, 8, w), lambda i: (0, 0, 0))
    kz = dm.WA // db
    kc = (dm.WA + 2 * db) // (2 * gn)
    proj3, dt3 = proj.reshape(bl, dm.Lp, dm.NP), projdt.reshape(bl, dm.Lp, DT_PAD)
    scratch = [((2 * TT, xbc_w), BF16), (((SSM_CONV_K - 1) * TT, xbc_w), F32), ((TT, xbc_w), F32),
               ((TT, xbc_w), F32), ((SSM_STATE, db), F32), ((TT, xbc_w), F32), ((TT, db), F32),
               ((TT + SMALL_HALO, xbc_w), F32), ((3 * TT, db), F32), ((TT, db), F32), ((TT, db), F32), ((8, db), F32)]
    how = [True] * 8 + ["keep"] + [False] * 9 + [True, True, "keep", "keep", "keep"] + [True] * len(scratch)
    (dp, dpt, dwc, dch, dhd), rode = _call(
        _per_sequence(setup, body, bl, how), "mix_b_bwd", (nt,),
        [tile(db, kz), tile(db, kz + 1), tile(2 * gn, kc), tile(DT_PAD, 0),
         halo(db, kz + 1), halo(2 * gn, kc), tile(db, 0), tile(db, 0),
         pl.BlockSpec((bl, 1, SSM_STATE, db), lambda i: (0, nt - 1 - i, 0, 0)),
         fixed(SSM_CONV_K, xbc_w), fixed(1, xbc_w), fixed(1, DT_PAD), fixed(1, DT_PAD),
         fixed(1, db), fixed(1, db), fixed(DT_PAD, db), fixed(db, DT_PAD), fixed((SSM_CONV_K - 1) * TT, 2 * TT)],
        [tile(dm.WB, 0), tile(DT_PAD, 0), sums(xbc_w), sums(db), sums(DT_PAD)],
        [jax.ShapeDtypeStruct((bl, dm.Lp, dm.WB), BF16), jax.ShapeDtypeStruct((bl, dm.Lp, DT_PAD), BF16),
         jax.ShapeDtypeStruct((bl, 8, xbc_w), F32), jax.ShapeDtypeStruct((bl, 8, db), F32),
         jax.ShapeDtypeStruct((bl, 8, DT_PAD), F32)],
        [pltpu.VMEM((bl,) + s, t) for s, t in scratch],
        ("arbitrary",),
        (proj3, proj3, proj3, dt3, proj3, proj3, dyb.reshape(bl, dm.Lp, db), yraw.reshape(bl, dm.Lp, db), sprev,
         conv_w, conv_b, dt_bias, a_log, dskx, norm_g, expand, expand_t, _shift_matrix()), rider)
    return (dp.reshape(dm.R, dm.WB), dpt.reshape(dm.R, DT_PAD), dwc, dch, dhd), rode


def _head_consts(dm):
    head_of = jnp.arange(dm.DB) // SSM_HEAD_DIM
    expand = (jnp.arange(DT_PAD)[:, None] == head_of[None, :]).astype(BF16)
    return expand, expand.T


def _ssm_params(lw, dm):
    pad_h = lambda v: jnp.pad(v, (0, DT_PAD - dm.H))[None]
    return (lw["ssm_conv_w"], lw["ssm_conv_b"][None], pad_h(lw["dt_bias"]), pad_h(lw["a_log"]),
            jnp.repeat(lw["d_skip"], SSM_HEAD_DIM)[None], lw["ssm_norm_g"][None])


def _layer_fwd(h, lw, w_in, w_out, cst, dm, next_bases=None, own_out=None):
    nxt = next_bases is not None
    n_next = len(next_bases) if nxt else 0

    def beside(rider, extra):
        return extra if rider is None else (rider if extra is None else _ride_both(rider, extra))

    (proj, projdt, hn), got = _fwd_in(
        h, lw["pre_g"][None], w_in, dm,
        beside(_ride_gather_ici(next_bases, 0, 2) if nxt else None,
               None if own_out is None else _ride_gather_ici([own_out])))
    ya = _mix_a_fwd(proj, lw["conv_a_w"], dm)
    (yb, yraw, sprev), got = _mix_b_fwd(
        proj, projdt, *_ssm_params(lw, dm), cst[0], dm,
        beside(_ride_gather_ici(got[:n_next], 1, 2) if nxt else None,
               None if own_out is None else _ride_gather_d2d(got[n_next:])))
    if own_out is not None:
        w_out = got[n_next].reshape(2 * dm.D, dm.D)
        got = got[:n_next]
    yc, u1 = _mix_c_fwd(proj, lw["conf_conv_w"], lw["conf_conv_b"][None], lw["conf_ln_g"][None],
                        lw["conf_ln_b"][None], dm)
    (h_new, m), got = _fwd_out(ya, yb, yc, w_out, h, lw["post_g"][None], dm, _ride_gather_d2d(got) if nxt else None)
    return h_new, (h, hn, proj, projdt, ya, yb, yc, u1, yraw, sprev, m), got, w_out


def _layer_bwd(dh, saved, lw, w_in, w_out, cst, dm, reduce=None, last=False):
    h_in, hn, proj, projdt, ya, yb, yc, u1, yraw, sprev, m = saved
    (dya, dyb, dyc, dwo, dpost), got = _bwd_out(dh, m, lw["post_g"][None], w_out, ya, yb, yc, dm,
                                                None if reduce is None else reduce.swap())
    dpa, dwa = _mix_a_bwd(proj, dya, lw["conv_a_w"], dm)
    (dpb, dpt, dwcv, dch, dhd), got = _mix_b_bwd(proj, projdt, dyb, yraw, sprev, *_ssm_params(lw, dm), cst[0],
                                                 cst[1], dm, None if reduce is None else reduce.to_owners(got))
    dpc, dwcf, dvc = _mix_c_bwd(proj, u1, dyc, lw["conf_conv_w"], lw["conf_ln_g"][None], lw["conf_ln_b"][None], dm)
    def own_reduce():
        pieces = _bwd_in_dw(hn, [dpa, dpb, dpc, dpt], dm)
        return _GradReduce([_grad_to_shards(pieces, dm), dwo.reshape(N_CHIPS, 2 * dm.D // N_CHIPS, dm.D)])

    rider = None if reduce is None else reduce.join(got)
    n_join = 0 if rider is None else len(rider.out_shapes)
    if last:
        mine = own_reduce()
        to_owners = mine.to_owners(_exchange("grad_swap_halves", mine.swap()))
        rider = to_owners if rider is None else _ride_both(rider, to_owners)
    (dh, dpre), got = _bwd_in_dx(dpa, dpb, dpc, dpt, w_in, h_in, dh, lw["pre_g"][None], dm, rider)
    if reduce is not None:
        reduce.finish(got[:n_join])
    if last:
        mine.finish(_exchange("grad_join_halves", mine.join(got[n_join:])))
    else:
        mine = own_reduce()
    dwcv, dch, dhd, dvc = (jnp.sum(a, axis=0) for a in (dwcv, dch, dhd, dvc))
    small = dict(pre_g=dpre[0], post_g=dpost[0], conv_a_w=jnp.sum(dwa, axis=0)[:CONV_A_K],
                 ssm_conv_w=dwcv[:SSM_CONV_K], ssm_conv_b=dwcv[SSM_CONV_K], ssm_norm_g=dch[0],
                 d_skip=jnp.sum(dch[1].reshape(dm.H, SSM_HEAD_DIM), axis=1), dt_bias=dhd[0, :dm.H],
                 a_log=dhd[1, :dm.H], conf_conv_w=jnp.sum(dwcf, axis=0)[:CONF_K], conf_conv_b=dvc[0],
                 conf_ln_g=dvc[1], conf_ln_b=dvc[2])
    return dh, mine, small


def _shard_runs(dm):
    ab = dm.WA + dm.WB
    order = [(0, 0, ab), (ab, dm.DT0, dm.H), (ab + dm.H, ab, dm.WC)]
    k = dm.NIN // N_CHIPS
    runs = []
    for s in range(N_CHIPS):
        for o0, m0, wd in order:
            lo, hi = max(o0, s * k), min(o0 + wd, (s + 1) * k)
            if lo < hi:
                runs.append((s, lo - s * k, m0 + lo - o0, hi - lo))
    return runs


def _w_in_from_shards(base, dm):
    tr = _row_tile(dm.D, 256)
    k = dm.NIN // N_CHIPS
    runs = _shard_runs(dm)

    def body(in_ref, out_ref):
        for s, sc, mc, wd in runs:
            out_ref[:, mc:mc + wd] = in_ref[s, :, sc:sc + wd]
        out_ref[:, dm.DT0 + dm.H:dm.NP] = jnp.zeros((tr, dm.NP - dm.DT0 - dm.H), BF16)

    return pl.pallas_call(
        body, name="w_in_from_shards", grid=(dm.D // tr,),
        in_specs=[pl.BlockSpec((N_CHIPS, tr, k), lambda r: (0, r, 0))],
        out_specs=pl.BlockSpec((tr, dm.NP), lambda r: (r, 0)),
        out_shape=jax.ShapeDtypeStruct((dm.D, dm.NP), BF16),
        compiler_params=_params(("parallel",)),
    )(base)


def _grad_to_shards(pieces, dm):
    tr = _row_tile(dm.D, 256)
    k = dm.NIN // N_CHIPS
    starts = [0, dm.WA, dm.WA + dm.WB, dm.DT0]
    widths = [dm.WA, dm.WB, dm.WC, DT_PAD]
    runs = _shard_runs(dm)

    def body(a_ref, b_ref, c_ref, t_ref, out_ref):
        refs = (a_ref, b_ref, c_ref, t_ref)
        for s, sc, mc, wd in runs:
            for p in range(4):
                lo, hi = max(mc, starts[p]), min(mc + wd, starts[p] + widths[p])
                if lo < hi:
                    out_ref[s, :, sc + lo - mc:sc + hi - mc] = refs[p][:, lo - starts[p]:hi - starts[p]].astype(BF16)

    return pl.pallas_call(
        body, name="grad_to_shards", grid=(dm.D // tr,),
        in_specs=[pl.BlockSpec((tr, w), lambda r: (r, 0)) for w in widths],
        out_specs=pl.BlockSpec((N_CHIPS, tr, k), lambda r: (0, r, 0)),
        out_shape=jax.ShapeDtypeStruct((N_CHIPS, dm.D, k), BF16),
        compiler_params=_params(("parallel",)),
    )(*pieces)


def _place_own(w, layer, me):
    _, rows, cols = w.shape
    tr = _row_tile(rows, 256)

    def body(me_ref, w_ref, out_ref):
        out_ref[0] = w_ref[0].astype(BF16)

    return pl.pallas_call(
        body, name="place_own",
        grid_spec=pltpu.PrefetchScalarGridSpec(
            num_scalar_prefetch=1, grid=(rows // tr,),
            in_specs=[pl.BlockSpec((1, tr, cols), lambda r, me_ref: (layer, r, 0))],
            out_specs=pl.BlockSpec((1, tr, cols), lambda r, me_ref: (me_ref[0], r, 0))),
        out_shape=jax.ShapeDtypeStruct((N_CHIPS, rows, cols), BF16),
        compiler_params=_params(("parallel",)),
    )(me, w)


def _add_halves(g, got, c, name):
    _, _, rows, cols = g.shape
    tr = _row_tile(rows, 256)

    def body(c_ref, g_ref, got_ref, out_ref):
        out_ref[0] = (g_ref[0, 0].astype(F32) + got_ref[0].astype(F32)).astype(BF16)

    return pl.pallas_call(
        body, name=name,
        grid_spec=pltpu.PrefetchScalarGridSpec(
            num_scalar_prefetch=1, grid=(N_CHIPS, rows // tr),
            in_specs=[pl.BlockSpec((1, 1, tr, cols), lambda s, r, c_ref: (s, c_ref[0], r, 0)),
                      pl.BlockSpec((1, tr, cols), lambda s, r, c_ref: (s, r, 0))],
            out_specs=pl.BlockSpec((1, tr, cols), lambda s, r, c_ref: (s, r, 0))),
        out_shape=jax.ShapeDtypeStruct((N_CHIPS, rows, cols), BF16),
        compiler_params=_params(("parallel", "parallel")),
    )(c, g, got)


def _add_owner(p, got, where, name):
    _, rows, cols = p.shape
    tr = _row_tile(rows, 256)

    def body(w_ref, p_ref, got_ref, out_ref):
        acc = p_ref[0].astype(F32)
        for j in range(3):
            acc = acc + got_ref[j].astype(F32)
        out_ref[0] = acc

    return pl.pallas_call(
        body, name=name,
        grid_spec=pltpu.PrefetchScalarGridSpec(
            num_scalar_prefetch=1, grid=(rows // tr,),
            in_specs=[pl.BlockSpec((1, tr, cols), lambda r, w_ref: (w_ref[0], r, 0)),
                      pl.BlockSpec((3, tr, cols), lambda r, w_ref: (0, r, 0))],
            out_specs=pl.BlockSpec((1, tr, cols), lambda r, w_ref: (w_ref[1], r, 0))),
        out_shape=jax.ShapeDtypeStruct((2, rows, cols), F32),
        compiler_params=_params(("parallel",)),
    )(where, p, got)


class _GradReduce:
    def __init__(self, gs):
        self.gs = [g.reshape((N_CHIPS, 2, g.shape[1] // 2) + g.shape[2:]) for g in gs]
        self.c = lax.axis_index("c").astype(jnp.int32).reshape(1)
        chip = (2 * lax.axis_index("x") + lax.axis_index("y")).astype(jnp.int32)
        self.where = jnp.stack([chip, self.c[0]])
        self.result = None

    def swap(self):
        return _ride_swap_halves(self.gs)

    def to_owners(self, got):
        self.ps = [_add_halves(g, r, self.c, "grad_add_sibling_" + n) for g, r, n in zip(self.gs, got, ("in", "out"))]
        return _ride_to_owners(self.ps)

    def join(self, got):
        qs = [_add_owner(p, r, self.where, "grad_add_chips_" + n) for p, r, n in zip(self.ps, got, ("in", "out"))]
        return _ride_join_halves(qs)

    def finish(self, got):
        self.result = [a.reshape((a.shape[0] * a.shape[1],) + a.shape[2:]) for a in got]


def _adamw_math(w, g, m, v):
    m = ADAM_B1 * m + (1.0 - ADAM_B1) * g
    v = ADAM_B2 * v + (1.0 - ADAM_B2) * (g * g)
    m_hat = m / (1.0 - ADAM_B1 ** ADAM_STEP)
    v_hat = v / (1.0 - ADAM_B2 ** ADAM_STEP)
    delta = -ADAM_LR * (m_hat / (jnp.sqrt(v_hat) + ADAM_EPS) + ADAM_WD * w)
    return delta, m, v


def _adamw_small(w, g, m, v, name):
    def body(w_ref, g_ref, m_ref, v_ref, d_out, m_out, v_out):
        d_out[...], m_out[...], v_out[...] = _adamw_math(w_ref[...], g_ref[...], m_ref[...], v_ref[...])

    shape = jax.ShapeDtypeStruct(w.shape, F32)
    return pl.pallas_call(body, name="adamw_" + name, out_shape=[shape, shape, shape],
                          compiler_params=_params())(w, g, m, v)


def _adamw_layer(i, w, g, m, v, prev, name):
    depth, rows, cols = w.shape
    tr = _row_tile(rows, 256)
    n_prev = 0 if prev is None else 4

    def body(*refs):
        w_ref, g_ref, m_ref, v_ref = refs[:4]
        g_out, d_out, m_out, v_out = refs[4 + n_prev:]
        gv = g_ref[...]
        g_out[0] = gv
        d_out[0], m_out[0], v_out[0] = _adamw_math(w_ref[0], gv, m_ref[0], v_ref[0])

    lay = pl.BlockSpec((1, tr, cols), lambda r: (i, r, 0))
    shape = jax.ShapeDtypeStruct(w.shape, F32)
    return pl.pallas_call(
        body, name="adamw_" + name, grid=(rows // tr,),
        in_specs=[lay, pl.BlockSpec((tr, cols), lambda r: (r, 0)), lay, lay] + [ANY] * n_prev,
        out_specs=[lay] * 4, out_shape=[shape] * 4,
        input_output_aliases={4 + k: k for k in range(n_prev)},
        compiler_params=_params(("parallel",)),
    )(w, g, m, v, *(prev or ()))


def _adamw_cols_major(w, gs, m, v, name):
    depth, rows, cols = w.shape
    tr = max(t for t in range(1, 129) if cols % t == 0)
    wt, mt, vt = (jnp.transpose(a, (2, 0, 1)) for a in (w, m, v))
    gt = jnp.transpose(jnp.stack(gs, axis=0), (2, 0, 1))

    def body(w_ref, g_ref, m_ref, v_ref, g_out, d_out, m_out, v_out):
        gv = g_ref[...]
        g_out[...] = gv
        d_out[...], m_out[...], v_out[...] = _adamw_math(w_ref[...], gv, m_ref[...], v_ref[...])

    spec = pl.BlockSpec((tr, depth, rows), lambda r: (r, 0, 0))
    shape = jax.ShapeDtypeStruct((cols, depth, rows), F32)
    outs = pl.pallas_call(body, name="adamw_" + name, grid=(cols // tr,), in_specs=[spec] * 4, out_specs=[spec] * 4,
                          out_shape=[shape] * 4, compiler_params=_params(("parallel",)))(wt, gt, mt, vt)
    return [jnp.transpose(a, (1, 2, 0)) for a in outs]


def _sum_leading(buf, name):
    n, rows, cols = buf.shape
    tr = _row_tile(rows, rows)

    def body(in_ref, out_ref):
        acc = in_ref[0]
        for k in range(1, n):
            acc = acc + in_ref[k]
        out_ref[...] = acc

    return pl.pallas_call(
        body, name=name, grid=(rows // tr,),
        in_specs=[pl.BlockSpec((n, tr, cols), lambda i: (0, i, 0))],
        out_specs=pl.BlockSpec((tr, cols), lambda i: (i, 0)),
        out_shape=jax.ShapeDtypeStruct((rows, cols), F32),
        compiler_params=_params(("parallel",)),
    )(buf)


_SHARDED_SMALL = ("meta", "conv_a_w", "ssm_conv_w", "conf_conv_w")
_LAYER_SMALL = ("pre_g", "post_g", "conv_a_w", "ssm_conv_w", "ssm_conv_b", "dt_bias", "a_log", "d_skip",
                "ssm_norm_g", "conf_conv_w", "conf_conv_b", "conf_ln_g", "conf_ln_b")
_WEIGHTS = ("meta", "pre_g", "post_g", "w_in", "w_out", "conv_a_w", "ssm_conv_w", "ssm_conv_b", "dt_bias", "a_log",
            "d_skip", "ssm_norm_g", "conf_conv_w", "conf_conv_b", "conf_ln_g", "conf_ln_b")


def _shard_last(a):
    return jnp.moveaxis(a.reshape(a.shape[:-1] + (N_CHIPS, a.shape[-1] // N_CHIPS)), -2, 0)


def _with_own_block(a, n, at):
    return lax.dynamic_update_index_in_dim(jnp.zeros((n,) + a.shape, a.dtype), a, at, 0)


def _with_own_columns(a, chip):
    k = a.shape[-1]
    return lax.dynamic_update_slice_in_dim(jnp.zeros(a.shape[:-1] + (N_CHIPS * k,), a.dtype), a, chip * k, a.ndim - 1)


def kernel(x, meta, pre_g, post_g, w_in, w_out, conv_a_w, ssm_conv_w, ssm_conv_b, dt_bias, a_log, d_skip, ssm_norm_g, conf_conv_w, conf_conv_b, conf_ln_g, conf_ln_b, loss_target, m_meta, m_pre_g, m_post_g, m_w_in, m_w_out, m_conv_a_w, m_ssm_conv_w, m_ssm_conv_b, m_dt_bias, m_a_log, m_d_skip, m_ssm_norm_g, m_conf_conv_w, m_conf_conv_b, m_conf_ln_g, m_conf_ln_b, v_meta, v_pre_g, v_post_g, v_w_in, v_w_out, v_conv_a_w, v_ssm_conv_w, v_ssm_conv_b, v_dt_bias, v_a_log, v_d_skip, v_ssm_norm_g, v_conf_conv_w, v_conf_conv_b, v_conf_ln_g, v_conf_ln_b):
    w = dict(meta=meta, pre_g=pre_g, post_g=post_g, w_in=w_in, w_out=w_out, conv_a_w=conv_a_w,
             ssm_conv_w=ssm_conv_w, ssm_conv_b=ssm_conv_b, dt_bias=dt_bias, a_log=a_log, d_skip=d_skip,
             ssm_norm_g=ssm_norm_g, conf_conv_w=conf_conv_w, conf_conv_b=conf_conv_b, conf_ln_g=conf_ln_g,
             conf_ln_b=conf_ln_b)
    mom = dict(meta=m_meta, pre_g=m_pre_g, post_g=m_post_g, w_in=m_w_in, w_out=m_w_out, conv_a_w=m_conv_a_w,
               ssm_conv_w=m_ssm_conv_w, ssm_conv_b=m_ssm_conv_b, dt_bias=m_dt_bias, a_log=m_a_log, d_skip=m_d_skip,
               ssm_norm_g=m_ssm_norm_g, conf_conv_w=m_conf_conv_w, conf_conv_b=m_conf_conv_b,
               conf_ln_g=m_conf_ln_g, conf_ln_b=m_conf_ln_b)
    vel = dict(meta=v_meta, pre_g=v_pre_g, post_g=v_post_g, w_in=v_w_in, w_out=v_w_out, conv_a_w=v_conv_a_w,
               ssm_conv_w=v_ssm_conv_w, ssm_conv_b=v_ssm_conv_b, dt_bias=v_dt_bias, a_log=v_a_log, d_skip=v_d_skip,
               ssm_norm_g=v_ssm_norm_g, conf_conv_w=v_conf_conv_w, conf_conv_b=v_conf_conv_b,
               conf_ln_g=v_conf_ln_g, conf_ln_b=v_conf_ln_b)
    bl, seq, d = x.shape
    dm = Dims(bl, seq, d)
    depth = w_in.shape[0]
    chip = (2 * lax.axis_index("x") + lax.axis_index("y")).astype(jnp.int32)
    dev = 2 * chip + lax.axis_index("c").astype(jnp.int32)
    cst = _head_consts(dm)

    bases = [[_place_own(w_in, i, chip.reshape(1)), _place_own(w_out, i, chip.reshape(1))] for i in range(depth)]
    first_in, small_w = _gather_ici_relayed(
        [bases[0][0]], _ride_gather_small([_with_own_columns(w[n], chip) for n in _SHARDED_SMALL]))
    full = dict(w)
    full.update(zip(_SHARDED_SMALL, small_w))
    h, gathered = _embed(x, full["meta"], dm, _ride_gather_d2d(first_in))
    saved, proj_w = [], []
    for i in range(depth):
        lw = {n: full[n][i] for n in _LAYER_SMALL}
        w_in_i = _w_in_from_shards(gathered[0], dm)
        h, keep, gathered, w_out_i = _layer_fwd(
            h, lw, w_in_i, None if i == 0 else gathered[1].reshape(2 * d, d), cst, dm,
            bases[i + 1] if i + 1 < depth else None, bases[0][1] if i == 0 else None)
        proj_w.append((w_in_i, w_out_i))
        saved.append(keep)

    dh, loss = _loss_head(h, loss_target, dm)

    small_g = {n: [None] * depth for n in _LAYER_SMALL}
    big = {"w_in": None, "w_out": None}
    g_in = [None] * depth
    reduce = None
    for i in reversed(range(depth)):
        lw = {n: full[n][i] for n in _LAYER_SMALL}
        dh, mine, sg = _layer_bwd(dh, saved[i], lw, proj_w[i][0], proj_w[i][1], cst, dm, reduce, last=i == 0)
        for n in _LAYER_SMALL:
            small_g[n][i] = sg[n]
        if reduce is not None:
            g_in[i + 1] = reduce.result[0]
            big["w_out"] = _adamw_layer(i + 1, w_out, reduce.result[1], m_w_out, v_w_out, big["w_out"], "w_out")
        reduce = mine
    g_in[0] = reduce.result[0]
    big["w_out"] = _adamw_layer(0, w_out, reduce.result[1], m_w_out, v_w_out, big["w_out"], "w_out")
    grad_x, gmeta = _unembed(dh, dm)

    g = {n: jnp.stack(v) for n, v in small_g.items()}
    g["meta"] = gmeta
    small = [n for n in _WEIGHTS if n not in ("w_in", "w_out")]
    flat = jnp.concatenate([g[n].reshape(-1) for n in small] + [loss.reshape(1)])
    rows = -(-flat.shape[0] // (16 * LANES)) * 16
    flat = jnp.pad(flat, (0, rows * LANES - flat.shape[0])).reshape(rows, LANES)
    parts = _gather_all(_with_own_block(flat, N_DEV, dev))
    total = _sum_leading(parts, "small_grads_sum").reshape(-1)
    big["w_in"] = _adamw_cols_major(w_in, g_in, m_w_in, v_w_in, "w_in")
    grads, deltas, new_m, new_v = {}, {}, {}, {}
    off = 0
    for n in small:
        size = g[n].size
        fullg = total[off:off + size].reshape(g[n].shape)
        off += size
        if n in _SHARDED_SMALL:
            fullg = lax.dynamic_index_in_dim(_shard_last(fullg), chip, axis=0, keepdims=False)
        grads[n] = fullg
        deltas[n], new_m[n], new_v[n] = _adamw_small(w[n], fullg, mom[n], vel[n], n)
    for n in ("w_in", "w_out"):
        grads[n], deltas[n], new_m[n], new_v[n] = big[n]
    loss = total[off]

    return (loss, grad_x, *[grads[n] for n in _WEIGHTS], *[deltas[n] for n in _WEIGHTS],
            *[new_m[n] for n in _WEIGHTS], *[new_v[n] for n in _WEIGHTS])
```
